```python
import jax, jax.numpy as jnp
from jax import lax
import numpy as np

D_MODEL = 1024
BATCH = 8
SEQ = 2048
DEPTH = 1

MLA_HEADS = 8
QK_NOPE_DIM = 64
QK_ROPE_DIM = 32
QK_HEAD_DIM = QK_NOPE_DIM + QK_ROPE_DIM
V_HEAD_DIM = 64
Q_LORA_RANK = 384
KV_LORA_RANK = 256
ROPE_THETA = 10000.0
Q_BLOCK = 128
MLSTM_HEADS = 4
MLSTM_HEAD_DIM = 128
MLSTM_INNER = MLSTM_HEADS * MLSTM_HEAD_DIM
CONV_WIDTH = 4
CHUNK = 64
MIX_WIDTH = MLA_HEADS * V_HEAD_DIM + MLSTM_INNER
IN_WIDTHS = (Q_LORA_RANK, KV_LORA_RANK, QK_ROPE_DIM, MLSTM_INNER, MLSTM_INNER, MLSTM_INNER, MLSTM_HEADS, MLSTM_HEADS)
IN_DIM = Q_LORA_RANK + KV_LORA_RANK + QK_ROPE_DIM + 3 * MLSTM_INNER + 2 * MLSTM_HEADS
N_EXPERTS = 256
TOP_K = 8
N_GROUPS = 8
TOPK_GROUPS = 4
EXPERT_DIM = 256
SHARED_DIM = 256
ROUTED_SCALE = 2.5
MOE_BLOCK = 128
RMS_EPS = 1e-6

kernel_name = 'hybrid_mla_mlstm_moe_adaln'


def rms_norm(x, g, eps=RMS_EPS):
    xf = x.astype(jnp.float32)
    y = xf * lax.rsqrt(jnp.mean(xf * xf, axis=-1, keepdims=True) + eps)
    return (y * g.astype(jnp.float32)).astype(x.dtype)


def modulate(h, shift, scale):
    return h * (1 + scale[:, None, :]) + shift[:, None, :]


def rope(x, cos, sin):
    x1, x2 = jnp.split(x, 2, axis=-1)
    return jnp.concatenate([x1 * cos - x2 * sin, x1 * sin + x2 * cos], axis=-1)


def mla_group(cq, ckv, k_rope, cos, sin, g_cq, w_uq, g_ckv, w_ukv, g_qn, g_kn, g_out):
    B, S, _ = cq.shape
    q = (rms_norm(cq, g_cq) @ w_uq).reshape(B, S, MLA_HEADS, QK_HEAD_DIM)
    kv = (rms_norm(ckv, g_ckv) @ w_ukv).reshape(B, S, MLA_HEADS, QK_NOPE_DIM + V_HEAD_DIM)
    k_nope, v = kv[..., :QK_NOPE_DIM], kv[..., QK_NOPE_DIM:]
    k = jnp.concatenate([k_nope, jnp.broadcast_to(k_rope[:, :, None, :], (B, S, MLA_HEADS, QK_ROPE_DIM))], axis=-1)
    q = rms_norm(q, g_qn)
    k = rms_norm(k, g_kn)
    c4, s4 = cos[:, :, None, :], sin[:, :, None, :]
    q = jnp.concatenate([q[..., :QK_NOPE_DIM], rope(q[..., QK_NOPE_DIM:], c4, s4)], axis=-1)
    k = jnp.concatenate([k[..., :QK_NOPE_DIM], rope(k[..., QK_NOPE_DIM:], c4, s4)], axis=-1)
    scale = QK_HEAD_DIM ** -0.5
    n_blocks = S // Q_BLOCK
    q_blocks = q.reshape(B, n_blocks, Q_BLOCK, MLA_HEADS, QK_HEAD_DIM).transpose(1, 0, 2, 3, 4)
    key_pos = jnp.arange(S)

    def attend(args):
        qb, start = args
        s = jnp.einsum('bqhd,bkhd->bhqk', qb, k).astype(jnp.float32) * scale
        mask = (start + jnp.arange(Q_BLOCK))[:, None] >= key_pos[None, :]
        p = jax.nn.softmax(jnp.where(mask, s, -jnp.inf), axis=-1).astype(v.dtype)
        return jnp.einsum('bhqk,bkhd->bqhd', p, v)

    o = lax.map(attend, (q_blocks, jnp.arange(n_blocks) * Q_BLOCK))
    o = o.transpose(1, 0, 2, 3, 4).reshape(B, S, MLA_HEADS * V_HEAD_DIM)
    return rms_norm(o, g_out)


def mlstm_chunkwise(q, k, v, log_i, log_f):
    B, S, H, d = q.shape
    nc = S // CHUNK
    to_c4 = lambda t: t.reshape(B, nc, CHUNK, H, d).transpose(1, 0, 3, 2, 4)
    to_c3 = lambda t: t.reshape(B, nc, CHUNK, H).transpose(1, 0, 3, 2)
    tri = jnp.arange(CHUNK)[:, None] >= jnp.arange(CHUNK)[None, :]

    def step(carry, inp):
        C, n, m = carry
        qc, kc, vc, li, lf = inp
        b = jnp.cumsum(lf, axis=-1)
        dmat = jnp.where(tri, b[..., :, None] - b[..., None, :] + li[..., None, :], -jnp.inf)
        inter = b + m[..., None]
        mt = jnp.maximum(inter, jnp.max(dmat, axis=-1))
        w = jnp.exp(dmat - mt[..., None]) * jnp.einsum('bhtd,bhsd->bhts', qc, kc)
        decay = jnp.exp(inter - mt)
        num = decay[..., None] * jnp.einsum('bhed,bhtd->bhte', C, qc) + jnp.einsum('bhts,bhse->bhte', w, vc)
        den = decay * jnp.einsum('bhd,bhtd->bht', n, qc) + jnp.sum(w, axis=-1)
        h = num / jnp.maximum(jnp.abs(den), jnp.exp(-mt))[..., None]
        b_last = b[..., -1]
        g = b_last[..., None] - b + li
        m_new = jnp.maximum(b_last + m, jnp.max(g, axis=-1))
        a = jnp.exp(g - m_new[..., None])
        carry_decay = jnp.exp(b_last + m - m_new)
        C = carry_decay[..., None, None] * C + jnp.einsum('bhs,bhse,bhsd->bhed', a, vc, kc)
        n = carry_decay[..., None] * n + jnp.einsum('bhs,bhsd->bhd', a, kc)
        return (C, n, m_new), h

    init = (jnp.zeros((B, H, d, d), jnp.float32), jnp.zeros((B, H, d), jnp.float32), jnp.zeros((B, H), jnp.float32))
    _, hs = lax.scan(step, init, (to_c4(q), to_c4(k), to_c4(v), to_c3(log_i), to_c3(log_f)))
    return hs.transpose(1, 0, 3, 2, 4).reshape(B, S, H, d)


def mlstm_group(z, v, o_pre, i_pre, f_pre, conv_w, conv_b, w_mq, w_mk, b_igate, b_fgate, g_out):
    B, S, _ = z.shape
    zc = lax.conv_general_dilated(z, conv_w, window_strides=(1,), padding=[(CONV_WIDTH - 1, 0)],
                                  dimension_numbers=('NWC', 'WIO', 'NWC'),
                                  feature_group_count=MLSTM_INNER) + conv_b
    zc = jax.nn.silu(zc).reshape(B, S, MLSTM_HEADS, MLSTM_HEAD_DIM)
    q = jnp.einsum('bshd,hde->bshe', zc, w_mq)
    k = jnp.einsum('bshd,hde->bshe', zc, w_mk) * (MLSTM_HEAD_DIM ** -0.5)
    v = v.reshape(B, S, MLSTM_HEADS, MLSTM_HEAD_DIM)
    log_i = (i_pre + b_igate).astype(jnp.float32)
    log_f = jax.nn.log_sigmoid((f_pre + b_fgate).astype(jnp.float32))
    h = mlstm_chunkwise(q.astype(jnp.float32), k.astype(jnp.float32), v.astype(jnp.float32), log_i, log_f)
    h = rms_norm(h, g_out.reshape(MLSTM_HEADS, MLSTM_HEAD_DIM)).reshape(B, S, MLSTM_INNER)
    return (jax.nn.sigmoid(o_pre.astype(jnp.float32)) * h).astype(z.dtype)


def moe_ffn(h, w_router, router_bias, w_gate_exp, w_up_exp, w_down_exp, w_gate_sh, w_up_sh, w_down_sh):
    B, S, D = h.shape
    T = B * S
    hf = h.reshape(T, D)
    shared = (jax.nn.silu(hf @ w_gate_sh) * (hf @ w_up_sh)) @ w_down_sh
    scores = jax.nn.sigmoid((hf @ w_router).astype(jnp.float32))
    grouped = (scores + router_bias.astype(jnp.float32)).reshape(T, N_GROUPS, N_EXPERTS // N_GROUPS)
    group_score = jnp.sum(lax.top_k(grouped, 2)[0], axis=-1)
    _, top_groups = lax.top_k(group_score, TOPK_GROUPS)
    group_keep = jnp.any(top_groups[:, :, None] == jnp.arange(N_GROUPS)[None, None, :], axis=1)
    masked = jnp.where(group_keep[:, :, None], grouped, -jnp.inf).reshape(T, N_EXPERTS)
    _, expert_idx = lax.top_k(masked, TOP_K)
    gate_w = jnp.take_along_axis(scores, expert_idx, axis=-1)
    gate_w = gate_w / jnp.sum(gate_w, axis=-1, keepdims=True) * ROUTED_SCALE
    A = T * TOP_K
    n_blocks = -(-A // MOE_BLOCK) + N_EXPERTS
    flat_e = expert_idx.reshape(A)
    flat_tok = jnp.arange(A, dtype=jnp.int32) // TOP_K
    order = jnp.argsort(flat_e)
    sorted_e = flat_e[order]
    counts = jnp.bincount(flat_e, length=N_EXPERTS)
    starts = jnp.cumsum(counts) - counts
    padded = (counts + MOE_BLOCK - 1) // MOE_BLOCK * MOE_BLOCK
    padded_end = jnp.cumsum(padded)
    dest = (padded_end - padded)[sorted_e] + jnp.arange(A, dtype=jnp.int32) - starts[sorted_e]
    row_tok = jnp.zeros((n_blocks * MOE_BLOCK,), jnp.int32).at[dest].set(flat_tok[order])
    row_w = jnp.zeros((n_blocks * MOE_BLOCK,), jnp.float32).at[dest].set(gate_w.reshape(A)[order])
    block_e = jnp.minimum(jnp.searchsorted(padded_end, jnp.arange(n_blocks) * MOE_BLOCK, side='right'), N_EXPERTS - 1)

    def expert_block(acc, blk):
        tok, wt, e = blk
        xb = hf[tok]
        act = jax.nn.silu(xb @ w_gate_exp[e]) * (xb @ w_up_exp[e])
        y = (act @ w_down_exp[e]) * wt[:, None].astype(xb.dtype)
        return acc.at[tok].add(y), None

    routed, _ = lax.scan(expert_block, jnp.zeros_like(hf),
                         (row_tok.reshape(n_blocks, MOE_BLOCK), row_w.reshape(n_blocks, MOE_BLOCK), block_e))
    return (shared + routed).reshape(B, S, D)


def setup_inputs(seed: int = 0) -> dict:
    key = jax.random.key(seed)
    ks = iter(jax.random.split(key, 40))
    nrm = lambda shape, s: jax.random.normal(next(ks), shape, jnp.float32) * s
    gain = lambda shape: 1.0 + nrm(shape, 0.02)
    L = DEPTH
    x = nrm((BATCH, SEQ, D_MODEL), 1.0)
    c = nrm((BATCH, D_MODEL), 1.0)
    offsets = jax.random.randint(next(ks), (BATCH, 1), 0, 4096, dtype=jnp.int32)
    positions = offsets + jnp.arange(SEQ, dtype=jnp.int32)[None, :]
    return {
        'x': x, 'c': c, 'positions': positions,
        'w_ada': nrm((L, D_MODEL, 6 * D_MODEL), 0.5 * D_MODEL ** -0.5),
        'b_ada': nrm((L, 6 * D_MODEL), 0.02),
        'g_norm1': gain((L, D_MODEL)),
        'w_in': nrm((L, D_MODEL, IN_DIM), D_MODEL ** -0.5),
        'g_cq': gain((L, Q_LORA_RANK)),
        'w_uq': nrm((L, Q_LORA_RANK, MLA_HEADS * QK_HEAD_DIM), Q_LORA_RANK ** -0.5),
        'g_ckv': gain((L, KV_LORA_RANK)),
        'w_ukv': nrm((L, KV_LORA_RANK, MLA_HEADS * (QK_NOPE_DIM + V_HEAD_DIM)), KV_LORA_RANK ** -0.5),
        'g_qn': gain((L, QK_HEAD_DIM)),
        'g_kn': gain((L, QK_HEAD_DIM)),
        'g_attn_out': gain((L, MLA_HEADS * V_HEAD_DIM)),
        'conv_w': nrm((L, CONV_WIDTH, 1, MLSTM_INNER), CONV_WIDTH ** -0.5),
        'conv_b': nrm((L, MLSTM_INNER), 0.02),
        'w_mq': nrm((L, MLSTM_HEADS, MLSTM_HEAD_DIM, MLSTM_HEAD_DIM), MLSTM_HEAD_DIM ** -0.5),
        'w_mk': nrm((L, MLSTM_HEADS, MLSTM_HEAD_DIM, MLSTM_HEAD_DIM), MLSTM_HEAD_DIM ** -0.5),
        'b_igate': nrm((L, MLSTM_HEADS), 0.1),
        'b_fgate': 3.0 + nrm((L, MLSTM_HEADS), 0.1),
        'g_mlstm_out': gain((L, MLSTM_INNER)),
        'w_out': nrm((L, MIX_WIDTH, D_MODEL), MIX_WIDTH ** -0.5),
        'g_norm2': gain((L, D_MODEL)),
        'w_router': nrm((L, D_MODEL, N_EXPERTS), D_MODEL ** -0.5),
        'router_bias': nrm((L, N_EXPERTS), 0.01),
        'w_gate_exp': nrm((L, N_EXPERTS, D_MODEL, EXPERT_DIM), D_MODEL ** -0.5),
        'w_up_exp': nrm((L, N_EXPERTS, D_MODEL, EXPERT_DIM), D_MODEL ** -0.5),
        'w_down_exp': nrm((L, N_EXPERTS, EXPERT_DIM, D_MODEL), EXPERT_DIM ** -0.5),
        'w_gate_sh': nrm((L, D_MODEL, SHARED_DIM), D_MODEL ** -0.5),
        'w_up_sh': nrm((L, D_MODEL, SHARED_DIM), D_MODEL ** -0.5),
        'w_down_sh': nrm((L, SHARED_DIM, D_MODEL), SHARED_DIM ** -0.5),
    }


def reference(x, c, positions, w_ada, b_ada, g_norm1, w_in, g_cq, w_uq, g_ckv, w_ukv, g_qn, g_kn, g_attn_out,
              conv_w, conv_b, w_mq, w_mk, b_igate, b_fgate, g_mlstm_out, w_out, g_norm2, w_router, router_bias,
              w_gate_exp, w_up_exp, w_down_exp, w_gate_sh, w_up_sh, w_down_sh):
    inv_freq = ROPE_THETA ** (-jnp.arange(0, QK_ROPE_DIM, 2, dtype=jnp.float32) / QK_ROPE_DIM)
    ang = positions.astype(jnp.float32)[..., None] * inv_freq
    cos = jnp.cos(ang).astype(x.dtype)
    sin = jnp.sin(ang).astype(x.dtype)
    cond = jax.nn.silu(c)
    split_points = np.cumsum(IN_WIDTHS)[:-1].tolist()
    for l in range(DEPTH):
        ada = cond @ w_ada[l] + b_ada[l]
        sh1, sc1, gt1, sh2, sc2, gt2 = jnp.split(ada, 6, axis=-1)
        h = modulate(rms_norm(x, g_norm1[l]), sh1, sc1)
        proj = h @ w_in[l]
        cq, ckv, k_rope, z, mv, mo, mi, mf = jnp.split(proj, split_points, axis=-1)
        attn_o = mla_group(cq, ckv, k_rope, cos, sin, g_cq[l], w_uq[l], g_ckv[l], w_ukv[l], g_qn[l], g_kn[l], g_attn_out[l])
        mlstm_o = mlstm_group(z, mv, mo, mi, mf, conv_w[l], conv_b[l], w_mq[l], w_mk[l], b_igate[l], b_fgate[l], g_mlstm_out[l])
        mix = jnp.concatenate([attn_o, mlstm_o], axis=-1) @ w_out[l]
        x = x + gt1[:, None, :] * mix
        h2 = modulate(rms_norm(x, g_norm2[l]), sh2, sc2)
        ffn = moe_ffn(h2, w_router[l], router_bias[l], w_gate_exp[l], w_up_exp[l], w_down_exp[l],
                      w_gate_sh[l], w_up_sh[l], w_down_sh[l])
        x = x + gt2[:, None, :] * ffn
    return x
```

```python
import functools
import math

import jax
import jax.numpy as jnp
from jax import lax
from jax.experimental import pallas as pl
from jax.experimental.pallas import tpu as pltpu

F32 = jnp.float32
BF16 = jnp.bfloat16
I32 = jnp.int32

D_MODEL = 1024
MLA_HEADS = 8
QK_NOPE = 64
QK_ROPE = 32
QK_DIM = QK_NOPE + QK_ROPE
V_DIM = 64
Q_LORA = 384
KV_LORA = 256
ROPE_THETA = 10000.0
MLSTM_HEADS = 4
MLSTM_DIM = 128
MLSTM_INNER = MLSTM_HEADS * MLSTM_DIM
CONV_WIDTH = 4
N_EXPERTS = 256
TOP_K = 8
N_GROUPS = 8
TOPK_GROUPS = 4
GROUP_SIZE = N_EXPERTS // N_GROUPS
EXPERT_DIM = 256
SHARED_DIM = 256
ROUTED_SCALE = 2.5
RMS_EPS = 1e-6

LANES = 128
SEQ_TILE = 256
ROW_BLOCK = 256
VMEM_LIMIT = 56 * 1024 * 1024

_OFF_CQ = 0
_OFF_CKV = _OFF_CQ + Q_LORA
_OFF_Z = _OFF_CKV + KV_LORA
_OFF_O = _OFF_Z + MLSTM_INNER
_OFF_KR = _OFF_O + MLSTM_INNER
_OFF_KRS = _OFF_KR + LANES
_WA_COLS = _OFF_KRS + LANES
_WT_ROWS = MLSTM_INNER + 16


def _params(n_axes, vmem=VMEM_LIMIT):
    return pltpu.CompilerParams(dimension_semantics=("arbitrary",) * n_axes, vmem_limit_bytes=vmem)


def _dot(a, b):
    return jnp.dot(a, b, preferred_element_type=F32)


def _dot_nt(a, b):
    return lax.dot_general(a, b, (((1,), (1,)), ((), ())), preferred_element_type=F32)


def _sigmoid(x):
    return 1.0 / (1.0 + jnp.exp(-x))


def _silu(x):
    return x * _sigmoid(x)


def _ada_body(c_ref, w_ref, b_ref, o_ref):
    c = c_ref[...]
    cond = _silu(c).astype(BF16)
    o_ref[...] = _dot(cond, w_ref[...].astype(BF16)) + b_ref[...]


def _ada(c, w_ada, b_ada):
    B = c.shape[0]
    n = w_ada.shape[1]
    tn = 1024
    return pl.pallas_call(
        _ada_body,
        out_shape=jax.ShapeDtypeStruct((B, n), F32),
        grid=(n // tn,),
        in_specs=[
            pl.BlockSpec((B, D_MODEL), lambda j: (0, 0)),
            pl.BlockSpec((D_MODEL, tn), lambda j: (0, j)),
            pl.BlockSpec((1, tn), lambda j: (0, j)),
        ],
        out_specs=pl.BlockSpec((B, tn), lambda j: (0, j)),
        compiler_params=_params(1),
        name="ada",
    )(c, w_ada, b_ada.reshape(1, n))


def _inproj_body(x_ref, sc_ref, sh_ref, g1_ref, wa_ref, wt_ref, gcq_ref, wuq_ref, wuqs_ref, gckv_ref,
                 wuk_ref, wuvt_ref, gq_ref, gqs_ref, gk_ref, gks_ref, cos_ref, sin_ref,
                 z_ref, o_ref, vt_ref, gt_ref, q_ref, k_ref, mvt_ref):
    x = x_ref[...]
    hn = x * lax.rsqrt(jnp.mean(x * x, axis=-1, keepdims=True) + RMS_EPS) * g1_ref[...]
    hm = (hn * sc_ref[...] + sh_ref[...]).astype(BF16)
    p = _dot(hm, wa_ref[...])
    z_ref[...] = p[:, _OFF_Z:_OFF_Z + MLSTM_INNER].astype(BF16)
    o_ref[...] = p[:, _OFF_O:_OFF_O + MLSTM_INNER].astype(BF16)
    rt = _dot_nt(wt_ref[...], hm)
    vt_ref[...] = rt[:MLSTM_INNER].astype(BF16)
    gt_ref[...] = rt[MLSTM_INNER:]

    cq = p[:, _OFF_CQ:_OFF_CQ + Q_LORA]
    cqn = (cq * lax.rsqrt(jnp.mean(cq * cq, axis=-1, keepdims=True) + RMS_EPS) * gcq_ref[...]).astype(BF16)
    ckv = p[:, _OFF_CKV:_OFF_CKV + KV_LORA]
    ckvn = (ckv * lax.rsqrt(jnp.mean(ckv * ckv, axis=-1, keepdims=True) + RMS_EPS) * gckv_ref[...]).astype(BF16)
    qa = _dot(cqn, wuq_ref[...])
    qb = _dot(cqn, wuqs_ref[...])
    ka = _dot(ckvn, wuk_ref[...])
    mvt_ref[...] = _dot_nt(wuvt_ref[...], ckvn).astype(BF16)
    kr = p[:, _OFF_KR:_OFF_KR + LANES]
    krs = p[:, _OFF_KRS:_OFF_KRS + LANES]

    cosp = cos_ref[...]
    sinp = sin_ref[...]
    gqc = gq_ref[...] * cosp
    gqs = gqs_ref[...] * sinp
    gkc = gk_ref[...] * cosp
    k_rot = krs * (gks_ref[...] * sinp)
    kr_ss = jnp.sum(kr * kr, axis=-1, keepdims=True)
    q_scale = (QK_DIM ** -0.5) * math.log2(math.e)
    for h in range(MLA_HEADS):
        sl = slice(h * LANES, (h + 1) * LANES)
        qah = qa[:, sl]
        rq = lax.rsqrt(jnp.sum(qah * qah, axis=-1, keepdims=True) * (1.0 / QK_DIM) + RMS_EPS) * q_scale
        q_ref[:, sl] = ((qah * gqc + qb[:, sl] * gqs) * rq).astype(BF16)
        kah = ka[:, sl]
        rk = lax.rsqrt((jnp.sum(kah * kah, axis=-1, keepdims=True) + kr_ss) * (1.0 / QK_DIM) + RMS_EPS)
        k_ref[:, sl] = (((kah + kr) * gkc + k_rot) * rk).astype(BF16)


def _inproj(x2, sc1p, sh1, g1, wa, wt, gcq, wuq, wuqs, gckv, wuk, wuvt, gq, gqs, gk, gks, cosp, sinp, S):
    T = x2.shape[0]
    tm = SEQ_TILE
    nt = T // tm
    per_b = S // tm
    hp = MLA_HEADS * LANES
    full = lambda a: pl.BlockSpec(a.shape, lambda i: (0,) * a.ndim)
    row = lambda w: pl.BlockSpec((tm, w), lambda i: (i, 0))
    bmod = pl.BlockSpec((None, 1, D_MODEL), lambda i: (i // per_b, 0, 0))
    out_shapes = (
        jax.ShapeDtypeStruct((T, MLSTM_INNER), BF16),
        jax.ShapeDtypeStruct((T, MLSTM_INNER), BF16),
        jax.ShapeDtypeStruct((nt, MLSTM_INNER, tm), BF16),
        jax.ShapeDtypeStruct((nt, 16, tm), F32),
        jax.ShapeDtypeStruct((T, hp), BF16),
        jax.ShapeDtypeStruct((T, hp), BF16),
        jax.ShapeDtypeStruct((nt, MLA_HEADS * V_DIM, tm), BF16),
    )
    out_specs = (
        row(MLSTM_INNER), row(MLSTM_INNER),
        pl.BlockSpec((None, MLSTM_INNER, tm), lambda i: (i, 0, 0)),
        pl.BlockSpec((None, 16, tm), lambda i: (i, 0, 0)),
        row(hp), row(hp),
        pl.BlockSpec((None, MLA_HEADS * V_DIM, tm), lambda i: (i, 0, 0)),
    )
    in_specs = [row(D_MODEL), bmod, bmod, full(g1), full(wa), full(wt), full(gcq), full(wuq), full(wuqs),
                full(gckv), full(wuk), full(wuvt), full(gq), full(gqs), full(gk), full(gks),
                row(LANES), row(LANES)]
    return pl.pallas_call(
        _inproj_body, out_shape=out_shapes, grid=(nt,), in_specs=in_specs, out_specs=out_specs,
        compiler_params=_params(1), name="inproj",
    )(x2, sc1p, sh1, g1, wa, wt, gcq, wuq, wuqs, gckv, wuk, wuvt, gq, gqs, gk, gks, cosp, sinp)


def _attn_body(q_ref, k_ref, vt_ref, o_ref):
    i = pl.program_id(2)
    t = SEQ_TILE
    key_pos = lax.broadcasted_iota(I32, (t, t), 0)
    qry_pos = lax.broadcasted_iota(I32, (t, t), 1)
    causal = key_pos <= qry_pos
    outs = []
    for hh in range(2):
        lanes = slice(hh * LANES, (hh + 1) * LANES)
        vrows = slice(hh * V_DIM, (hh + 1) * V_DIM)
        q = q_ref[:, lanes]

        def tile(j, carry, diag):
            m, l, acc = carry
            kj = k_ref[pl.ds(pl.multiple_of(j * t, t), t), lanes]
            s = _dot_nt(kj, q)
            if diag:
                s = jnp.where(causal, s, -jnp.inf)
            m_new = jnp.maximum(m, jnp.max(s, axis=0, keepdims=True))
            p = jnp.exp2(s - m_new)
            alpha = jnp.exp2(m - m_new)
            l = alpha * l + jnp.sum(p, axis=0, keepdims=True)
            acc = alpha * acc + _dot(vt_ref[j, vrows, :], p.astype(BF16))
            return m_new, l, acc

        init = (jnp.full((1, t), -jnp.inf, F32), jnp.zeros((1, t), F32), jnp.zeros((V_DIM, t), F32))
        carry = lax.fori_loop(0, i, lambda j, c: tile(j, c, False), init)
        _, l, acc = tile(i, carry, True)
        outs.append(acc / l)
    o_ref[...] = jnp.concatenate(outs, axis=0).T.astype(BF16)


def _attn(q, k, mvt, B, S):
    T = q.shape[0]
    t = SEQ_TILE
    per_b = S // t
    return pl.pallas_call(
        _attn_body,
        out_shape=jax.ShapeDtypeStruct((T, MLA_HEADS * V_DIM), BF16),
        grid=(B, MLA_HEADS // 2, per_b),
        in_specs=[
            pl.BlockSpec((t, 2 * LANES), lambda b, hp, i: (b * per_b + i, hp)),
            pl.BlockSpec((S, 2 * LANES), lambda b, hp, i: (b, hp)),
            pl.BlockSpec((per_b, 2 * V_DIM, t), lambda b, hp, i: (b, hp, 0)),
        ],
        out_specs=pl.BlockSpec((t, 2 * V_DIM), lambda b, hp, i: (b * per_b + i, hp)),
        compiler_params=_params(3), name="attn",
    )(q, k, mvt)


def _split3(a):
    hi = a.astype(BF16)
    r = a - hi.astype(F32)
    mid = r.astype(BF16)
    lo = (r - mid.astype(F32)).astype(BF16)
    return hi, mid, lo


def _mlstm_body(bi_ref, bf_ref, z_ref, vt_ref, g_ref, og_ref, cw_ref, cb_ref, wq_ref, wk_ref, gout_ref,
                out_ref, zpad, q_s, k_s, st_s, m_s, *, S):
    h = pl.program_id(1)
    L = SEQ_TILE
    nchunk = S // L
    hd = MLSTM_DIM
    zpad[0:8, :] = jnp.zeros((8, hd), F32)
    zpad[8:, :] = z_ref[...].astype(F32)
    zc = cb_ref[...] + jnp.zeros((S, hd), F32)
    for j in range(CONV_WIDTH):
        zc = zc + cw_ref[j:j + 1, :] * zpad[pl.ds(8 - (CONV_WIDTH - 1) + j, S), :]
    zs = _silu(zc).astype(BF16)
    q_s[...] = _dot(zs, wq_ref[...]).astype(BF16)
    k_s[...] = (_dot(zs, wk_ref[...]) * (hd ** -0.5)).astype(BF16)

    st_s[...] = jnp.zeros_like(st_s)
    m_s[...] = jnp.zeros_like(m_s)
    b_i = bi_ref[h]
    b_f = bf_ref[h]
    r_i = lax.broadcasted_iota(I32, (L, L), 0)
    c_i = lax.broadcasted_iota(I32, (L, L), 1)
    tril = r_i >= c_i
    strict_lower = jnp.where(r_i > c_i, 1.0, 0.0).astype(BF16)
    ones_row = jnp.where(lax.broadcasted_iota(I32, (hd, L), 0) == 0, 1.0, 0.0).astype(BF16)

    def chunk(c, _):
        start = pl.multiple_of(c * L, L)
        qc = q_s[pl.ds(start, L), :]
        kc = k_s[pl.ds(start, L), :]
        vt_aug = jnp.concatenate([vt_ref[c], ones_row], axis=0)
        li = g_ref[c, pl.ds(h, 1), :] + b_i
        fp = g_ref[c, pl.ds(MLSTM_HEADS + h, 1), :] + b_f
        lf = jnp.minimum(fp, 0.0) - jnp.log(1.0 + jnp.exp(-jnp.abs(fp)))
        a_mat = jnp.where(tril, lf, 0.0)
        a_hi, a_mid, a_lo = _split3(a_mat)
        d0 = _dot(a_hi, strict_lower) + _dot(a_mid, strict_lower) + _dot(a_lo, strict_lower)
        dmat = jnp.where(tril, d0 + li, -jnp.inf)
        b_col = jnp.sum(a_mat, axis=1, keepdims=True)
        m_prev = m_s[...]
        inter = b_col + m_prev
        mt = jnp.maximum(inter, jnp.max(dmat, axis=1, keepdims=True))
        w = jnp.exp(dmat - mt) * _dot_nt(qc, kc)
        decay = jnp.exp(inter - mt)
        st = st_s[...]
        num_aug = decay * _dot_nt(qc, st.astype(BF16)) + _dot_nt(w.astype(BF16), vt_aug)
        num = num_aug[:, :hd]
        den = num_aug[:, hd:hd + 1]
        hv = num / jnp.maximum(jnp.abs(den), jnp.exp(-mt))
        g = d0[L - 1:L, :] + li
        b_last = b_col[L - 1:L, :]
        m_new = jnp.maximum(b_last + m_prev, jnp.max(g, axis=1, keepdims=True))
        a = jnp.exp(g - m_new)
        cd = jnp.exp(b_last + m_prev - m_new)
        st_s[...] = cd * st + _dot((vt_aug.astype(F32) * a).astype(BF16), kc)
        m_s[...] = m_new
        hn = hv * lax.rsqrt(jnp.mean(hv * hv, axis=-1, keepdims=True) + RMS_EPS) * gout_ref[...]
        og = og_ref[pl.ds(start, L), :].astype(F32)
        out_ref[pl.ds(start, L), :] = (_sigmoid(og) * hn).astype(BF16)
        return 0

    lax.fori_loop(0, nchunk, chunk, 0)


def _mlstm(z, vt3, g3, og, conv_w, conv_b, wq, wk, b_i, b_f, g_out, B, S):
    T = z.shape[0]
    L = SEQ_TILE
    per_b = S // L
    hd = MLSTM_DIM
    smem = pl.BlockSpec(memory_space=pltpu.SMEM)
    seq = pl.BlockSpec((S, hd), lambda b, h: (b, h))
    return pl.pallas_call(
        functools.partial(_mlstm_body, S=S),
        out_shape=jax.ShapeDtypeStruct((T, MLSTM_INNER), BF16),
        grid=(B, MLSTM_HEADS),
        in_specs=[
            smem, smem, seq,
            pl.BlockSpec((per_b, hd, L), lambda b, h: (b, h, 0)),
            pl.BlockSpec((per_b, 16, L), lambda b, h: (b, 0, 0)),
            seq,
            pl.BlockSpec((CONV_WIDTH, hd), lambda b, h: (0, h)),
            pl.BlockSpec((1, hd), lambda b, h: (0, h)),
            pl.BlockSpec((None, hd, hd), lambda b, h: (h, 0, 0)),
            pl.BlockSpec((None, hd, hd), lambda b, h: (h, 0, 0)),
            pl.BlockSpec((1, hd), lambda b, h: (0, h)),
        ],
        out_specs=seq,
        scratch_shapes=[
            pltpu.VMEM((S + 8, hd), F32),
            pltpu.VMEM((S, hd), BF16),
            pltpu.VMEM((S, hd), BF16),
            pltpu.VMEM((2 * hd, hd), F32),
            pltpu.VMEM((1, 1), F32),
        ],
        compiler_params=_params(2), name="mlstm",
    )(b_i, b_f, z, vt3, g3, og, conv_w, conv_b, wq, wk, g_out)


def _post_body(ao_ref, mo_ref, x_ref, gt1_ref, gattn_ref, wout_ref, g2_ref, sc2_ref, sh2_ref, gt2_ref,
               wrt_ref, wgu_ref, wds_ref, base_ref, h2_ref, lg_ref):
    ao = ao_ref[...].astype(F32)
    aon = ao * lax.rsqrt(jnp.mean(ao * ao, axis=-1, keepdims=True) + RMS_EPS) * gattn_ref[...]
    mix_in = jnp.concatenate([aon.astype(BF16), mo_ref[...]], axis=1)
    x1 = x_ref[...] + gt1_ref[...] * _dot(mix_in, wout_ref[...])
    hn = x1 * lax.rsqrt(jnp.mean(x1 * x1, axis=-1, keepdims=True) + RMS_EPS) * g2_ref[...]
    h2 = hn * sc2_ref[...] + sh2_ref[...]
    h2_ref[...] = h2
    h2b = h2.astype(BF16)
    gu = _dot(h2b, wgu_ref[...])
    act = (_silu(gu[:, :SHARED_DIM]) * gu[:, SHARED_DIM:]).astype(BF16)
    base_ref[...] = x1 + gt2_ref[...] * _dot(act, wds_ref[...])
    lg_ref[...] = _dot_nt(wrt_ref[...], h2b)


def _post(ao, mo, x2, gt1, gattn, wout, g2, sc2p, sh2, gt2, wrt, wgu, wds, S):
    T = x2.shape[0]
    tm = SEQ_TILE
    per_b = S // tm
    full = lambda a: pl.BlockSpec(a.shape, lambda i: (0,) * a.ndim)
    row = lambda w: pl.BlockSpec((tm, w), lambda i: (i, 0))
    bmod = pl.BlockSpec((None, 1, D_MODEL), lambda i: (i // per_b, 0, 0))
    return pl.pallas_call(
        _post_body,
        out_shape=(jax.ShapeDtypeStruct((T, D_MODEL), F32), jax.ShapeDtypeStruct((T, D_MODEL), F32),
                   jax.ShapeDtypeStruct((N_EXPERTS, T), F32)),
        grid=(T // tm,),
        in_specs=[row(MLA_HEADS * V_DIM), row(MLSTM_INNER), row(D_MODEL), bmod, full(gattn), full(wout),
                  full(g2), bmod, bmod, bmod, full(wrt), full(wgu), full(wds)],
        out_specs=(row(D_MODEL), row(D_MODEL), pl.BlockSpec((N_EXPERTS, tm), lambda i: (0, i))),
        compiler_params=_params(1), name="post",
    )(ao, mo, x2, gt1, gattn, wout, g2, sc2p, sh2, gt2, wrt, wgu, wds)


def _first_max(x, rows, n):
    mx = jnp.max(x, axis=0, keepdims=True)
    idx = jnp.min(jnp.where(x == mx, rows, n), axis=0, keepdims=True)
    return mx, idx


def _route_body(lg_ref, bias_ref, idx_ref, w_ref, rank_ref, cnt_ref, carry):
    i = pl.program_id(0)
    tt = lg_ref.shape[1]

    @pl.when(i == 0)
    def _():
        carry[...] = jnp.zeros_like(carry)

    sc = _sigmoid(lg_ref[...])
    bi = sc + bias_ref[...]
    rows = lax.broadcasted_iota(I32, (N_EXPERTS, tt), 0)
    rows_g = lax.broadcasted_iota(I32, (GROUP_SIZE, tt), 0)
    rows_n = lax.broadcasted_iota(I32, (N_GROUPS, tt), 0)
    neg = -jnp.inf
    gs = []
    for g in range(N_GROUPS):
        xg = bi[g * GROUP_SIZE:(g + 1) * GROUP_SIZE]
        m1, i1 = _first_max(xg, rows_g, GROUP_SIZE)
        m2 = jnp.max(jnp.where(rows_g == i1, neg, xg), axis=0, keepdims=True)
        gs.append(m1 + m2)
    cur = jnp.concatenate(gs, axis=0)
    keep = jnp.zeros((N_GROUPS, tt), F32)
    for _ in range(TOPK_GROUPS):
        _, gi = _first_max(cur, rows_n, N_GROUPS)
        sel = rows_n == gi
        keep = jnp.where(sel, 1.0, keep)
        cur = jnp.where(sel, neg, cur)
    masked = jnp.concatenate(
        [jnp.where(keep[g:g + 1] > 0.0, bi[g * GROUP_SIZE:(g + 1) * GROUP_SIZE], neg) for g in range(N_GROUPS)], axis=0)
    cur = masked
    onehot = jnp.zeros((N_EXPERTS, tt), F32)
    idxs, ws = [], []
    for _ in range(TOP_K):
        _, ei = _first_max(cur, rows, N_EXPERTS)
        sel = rows == ei
        ws.append(jnp.sum(jnp.where(sel, sc, 0.0), axis=0, keepdims=True))
        idxs.append(ei)
        cur = jnp.where(sel, neg, cur)
        onehot = jnp.where(sel, 1.0, onehot)
    wsum = ws[0]
    for k in range(1, TOP_K):
        wsum = wsum + ws[k]
    idx_ref[...] = jnp.concatenate(idxs, axis=0)
    w_ref[...] = jnp.concatenate([w / wsum * ROUTED_SCALE for w in ws], axis=0)
    r_i = lax.broadcasted_iota(I32, (tt, tt), 0)
    c_i = lax.broadcasted_iota(I32, (tt, tt), 1)
    before = jnp.where(r_i < c_i, 1.0, 0.0).astype(BF16)
    tot = _dot(onehot.astype(BF16), before) + carry[...]
    rank_ref[...] = jnp.concatenate(
        [jnp.sum(jnp.where(rows == ei, tot, 0.0), axis=0, keepdims=True) for ei in idxs], axis=0).astype(I32)
    carry[...] = carry[...] + jnp.sum(onehot, axis=1, keepdims=True)
    cnt_ref[...] = jnp.broadcast_to(carry[...], cnt_ref.shape)


def _route(lgt, bias_col):
    T = lgt.shape[1]
    tt = SEQ_TILE
    blk = pl.BlockSpec((TOP_K, tt), lambda i: (0, i))
    return pl.pallas_call(
        _route_body,
        out_shape=(jax.ShapeDtypeStruct((TOP_K, T), I32), jax.ShapeDtypeStruct((TOP_K, T), F32),
                   jax.ShapeDtypeStruct((TOP_K, T), I32), jax.ShapeDtypeStruct((N_EXPERTS, LANES), F32)),
        grid=(T // tt,),
        in_specs=[pl.BlockSpec((N_EXPERTS, tt), lambda i: (0, i)), pl.BlockSpec((N_EXPERTS, 1), lambda i: (0, 0))],
        out_specs=(blk, blk, blk, pl.BlockSpec((N_EXPERTS, LANES), lambda i: (0, 0))),
        scratch_shapes=[pltpu.VMEM((N_EXPERTS, 1), F32)],
        compiler_params=_params(1), name="route",
    )(lgt, bias_col)


def _dest_body(idx_ref, rank_ref, off_ref, dest_ref):
    tt = idx_ref.shape[1]
    rows = lax.broadcasted_iota(I32, (N_EXPERTS, tt), 0)
    off = off_ref[...]
    outs = []
    for k in range(TOP_K):
        sel = rows == idx_ref[k:k + 1, :]
        outs.append(jnp.sum(jnp.where(sel, off, 0), axis=0, keepdims=True) + rank_ref[k:k + 1, :])
    dest_ref[...] = jnp.concatenate(outs, axis=0)


def _dest(idx_t, rank_t, off_col):
    T = idx_t.shape[1]
    tt = SEQ_TILE
    blk = pl.BlockSpec((TOP_K, tt), lambda i: (0, i))
    return pl.pallas_call(
        _dest_body, out_shape=jax.ShapeDtypeStruct((TOP_K, T), I32), grid=(T // tt,),
        in_specs=[blk, blk, pl.BlockSpec((N_EXPERTS, 1), lambda i: (0, 0))], out_specs=blk,
        compiler_params=_params(1), name="dest",
    )(idx_t, rank_t, off_col)


def _row_copy(src, s, dst, d, sem):
    return pltpu.make_async_copy(src.at[pl.ds(s, 1)], dst.at[pl.ds(d, 1)], sem)


def _dispatch_body(dest_ref, h2_ref, xs_in_ref, xs_ref, sem):
    del xs_in_ref
    i = pl.program_id(0)
    tt = dest_ref.shape[1]

    def issue(r, _):
        for k in range(TOP_K):
            _row_copy(h2_ref, i * tt + r, xs_ref, dest_ref[k, r], sem).start()
        return 0

    lax.fori_loop(0, tt, issue, 0)

    def drain(r, _):
        for k in range(TOP_K):
            _row_copy(h2_ref, i * tt + r, xs_ref, dest_ref[k, r], sem).wait()
        return 0

    lax.fori_loop(0, tt, drain, 0)


def _dispatch(dest_t, h2, n_rows):
    T = h2.shape[0]
    tt = SEQ_TILE
    xs0 = jnp.zeros((n_rows, D_MODEL), h2.dtype)
    anyspec = pl.BlockSpec(memory_space=pl.ANY)
    return pl.pallas_call(
        _dispatch_body, out_shape=jax.ShapeDtypeStruct((n_rows, D_MODEL), h2.dtype), grid=(T // tt,),
        in_specs=[pl.BlockSpec((TOP_K, tt), lambda i: (0, i), memory_space=pltpu.SMEM), anyspec, anyspec],
        out_specs=anyspec,
        scratch_shapes=[pltpu.SemaphoreType.DMA],
        input_output_aliases={2: 0},
        compiler_params=_params(1), name="dispatch",
    )(dest_t, h2, xs0)


def _experts_body(be_ref, nb_ref, x_ref, wg_ref, wu_ref, wd_ref, y_ref):
    j = pl.program_id(0)

    @pl.when(j < nb_ref[0])
    def _():
        x = x_ref[...].astype(BF16)
        g = _dot(x, wg_ref[...].astype(BF16))
        u = _dot(x, wu_ref[...].astype(BF16))
        act = (_silu(g) * u).astype(BF16)
        y_ref[...] = _dot(act, wd_ref[...].astype(BF16))


def _experts(block_e, n_used, xs, wg, wu, wd):
    n_rows = xs.shape[0]
    nblk = n_rows // ROW_BLOCK
    clamp = lambda j, nb: jnp.minimum(j, nb[0] - 1)
    xmap = lambda j, be, nb: (clamp(j, nb), 0)
    wmap = lambda j, be, nb: (be[clamp(j, nb)], 0, 0)
    grid_spec = pltpu.PrefetchScalarGridSpec(
        num_scalar_prefetch=2, grid=(nblk,),
        in_specs=[
            pl.BlockSpec((ROW_BLOCK, D_MODEL), xmap),
            pl.BlockSpec((None, D_MODEL, EXPERT_DIM), wmap),
            pl.BlockSpec((None, D_MODEL, EXPERT_DIM), wmap),
            pl.BlockSpec((None, EXPERT_DIM, D_MODEL), wmap),
        ],
        out_specs=pl.BlockSpec((ROW_BLOCK, D_MODEL), xmap),
    )
    return pl.pallas_call(
        _experts_body, out_shape=jax.ShapeDtypeStruct((n_rows, D_MODEL), F32), grid_spec=grid_spec,
        compiler_params=_params(1), name="experts",
    )(block_e, n_used, xs, wg, wu, wd)


def _combine_body(dest_ref, y_ref, base_ref, w_ref, gt2_ref, out_ref, ybuf, sem):
    tt = base_ref.shape[0]

    def issue(r, _):
        for k in range(TOP_K):
            _row_copy(y_ref, dest_ref[k, r], ybuf.at[k], r, sem).start()
        return 0

    lax.fori_loop(0, tt, issue, 0)

    def drain(r, _):
        for k in range(TOP_K):
            _row_copy(y_ref, dest_ref[k, r], ybuf.at[k], r, sem).wait()
        return 0

    lax.fori_loop(0, tt, drain, 0)
    w = w_ref[...]
    routed = w[:, 0:1] * ybuf[0]
    for k in range(1, TOP_K):
        routed = routed + w[:, k:k + 1] * ybuf[k]
    out_ref[...] = base_ref[...] + gt2_ref[...] * routed


def _combine(dest_t, y, base, w_tk, gt2, S):
    T = base.shape[0]
    tt = LANES
    per_b = S // tt
    return pl.pallas_call(
        _combine_body, out_shape=jax.ShapeDtypeStruct((T, D_MODEL), F32), grid=(T // tt,),
        in_specs=[
            pl.BlockSpec((TOP_K, tt), lambda i: (0, i), memory_space=pltpu.SMEM),
            pl.BlockSpec(memory_space=pl.ANY),
            pl.BlockSpec((tt, D_MODEL), lambda i: (i, 0)),
            pl.BlockSpec((tt, TOP_K), lambda i: (i, 0)),
            pl.BlockSpec((None, 1, D_MODEL), lambda i: (i // per_b, 0, 0)),
        ],
        out_specs=pl.BlockSpec((tt, D_MODEL), lambda i: (i, 0)),
        scratch_shapes=[pltpu.VMEM((TOP_K, tt, D_MODEL), F32), pltpu.SemaphoreType.DMA],
        compiler_params=_params(1), name="combine",
    )(dest_t, y, base, w_tk, gt2)


def _pad_heads(w, width):
    kdim = w.shape[0]
    w3 = w.reshape(kdim, MLA_HEADS, width)
    return jnp.pad(w3, ((0, 0), (0, 0), (0, LANES - width))).reshape(kdim, MLA_HEADS * LANES)


def _rot_partner(w_rope):
    half = QK_ROPE // 2
    return jnp.concatenate([-w_rope[..., half:], w_rope[..., :half]], axis=-1)


def _lane_vec(nope, rope):
    return jnp.concatenate([nope, rope, jnp.zeros((LANES - QK_DIM,), F32)]).reshape(1, LANES)


def _layer(x2, ada, positions, B, S, g_norm1, w_in, g_cq, w_uq, g_ckv, w_ukv, g_qn, g_kn, g_attn_out,
           conv_w, conv_b, w_mq, w_mk, b_igate, b_fgate, g_mlstm_out, w_out, g_norm2, w_router, router_bias,
           w_gate_exp, w_up_exp, w_down_exp, w_gate_sh, w_up_sh, w_down_sh):
    T = B * S
    sh1, sc1, gt1, sh2, sc2, gt2 = [a.reshape(B, 1, D_MODEL) for a in jnp.split(ada, 6, axis=-1)]
    row = lambda v: v.reshape(1, -1).astype(F32)

    inv_freq = ROPE_THETA ** (-jnp.arange(0, QK_ROPE, 2, dtype=F32) / QK_ROPE)
    ang = positions.astype(F32).reshape(T, 1) * inv_freq
    cos, sin = jnp.cos(ang), jnp.sin(ang)
    pad_r = jnp.zeros((T, LANES - QK_DIM), F32)
    cosp = jnp.concatenate([jnp.ones((T, QK_NOPE), F32), cos, cos, pad_r], axis=1)
    sinp = jnp.concatenate([jnp.zeros((T, QK_NOPE), F32), sin, sin, pad_r], axis=1)

    o0 = 0
    w_cq = w_in[:, o0:o0 + Q_LORA]; o0 += Q_LORA
    w_ckv = w_in[:, o0:o0 + KV_LORA]; o0 += KV_LORA
    w_kr = w_in[:, o0:o0 + QK_ROPE]; o0 += QK_ROPE
    w_z = w_in[:, o0:o0 + MLSTM_INNER]; o0 += MLSTM_INNER
    w_v = w_in[:, o0:o0 + MLSTM_INNER]; o0 += MLSTM_INNER
    w_o = w_in[:, o0:o0 + MLSTM_INNER]; o0 += MLSTM_INNER
    w_i = w_in[:, o0:o0 + MLSTM_HEADS]; o0 += MLSTM_HEADS
    w_f = w_in[:, o0:o0 + MLSTM_HEADS]
    zl = jnp.zeros((D_MODEL, QK_NOPE), F32)
    zr = jnp.zeros((D_MODEL, LANES - QK_DIM), F32)
    wa = jnp.concatenate([w_cq, w_ckv, w_z, w_o, zl, w_kr, zr, zl, _rot_partner(w_kr), zr], axis=1).astype(BF16)
    wt = jnp.concatenate([w_v.T, w_i.T, w_f.T, jnp.zeros((16 - 2 * MLSTM_HEADS, D_MODEL), F32)], axis=0).astype(BF16)

    uq3 = w_uq.reshape(Q_LORA, MLA_HEADS, QK_DIM)
    uq_rot = jnp.concatenate([jnp.zeros((Q_LORA, MLA_HEADS, QK_NOPE), F32), _rot_partner(uq3[..., QK_NOPE:])], axis=-1)
    wuq = _pad_heads(w_uq, QK_DIM).astype(BF16)
    wuqs = _pad_heads(uq_rot.reshape(Q_LORA, MLA_HEADS * QK_DIM), QK_DIM).astype(BF16)
    ukv3 = w_ukv.reshape(KV_LORA, MLA_HEADS, QK_NOPE + V_DIM)
    wuk = _pad_heads(ukv3[..., :QK_NOPE].reshape(KV_LORA, MLA_HEADS * QK_NOPE), QK_NOPE).astype(BF16)
    wuvt = ukv3[..., QK_NOPE:].reshape(KV_LORA, MLA_HEADS * V_DIM).T.astype(BF16)
    gq = _lane_vec(g_qn[:QK_NOPE], g_qn[QK_NOPE:])
    gqs = _lane_vec(jnp.zeros((QK_NOPE,), F32), _rot_partner(g_qn[QK_NOPE:]) * jnp.concatenate(
        [-jnp.ones((QK_ROPE // 2,), F32), jnp.ones((QK_ROPE // 2,), F32)]))
    gk = _lane_vec(g_kn[:QK_NOPE], g_kn[QK_NOPE:])
    gks = _lane_vec(jnp.zeros((QK_NOPE,), F32), _rot_partner(g_kn[QK_NOPE:]) * jnp.concatenate(
        [-jnp.ones((QK_ROPE // 2,), F32), jnp.ones((QK_ROPE // 2,), F32)]))

    z, og, vt3, g3, q, k, mvt = _inproj(
        x2, 1.0 + sc1, sh1, row(g_norm1), wa, wt, row(g_cq), wuq, wuqs, row(g_ckv), wuk, wuvt,
        gq, gqs, gk, gks, cosp, sinp, S)
    attn_o = _attn(q, k, mvt, B, S)
    mlstm_o = _mlstm(z, vt3, g3, og, conv_w.reshape(CONV_WIDTH, MLSTM_INNER), row(conv_b),
                     w_mq.astype(BF16), w_mk.astype(BF16), b_igate, b_fgate, row(g_mlstm_out), B, S)

    wgu = jnp.concatenate([w_gate_sh, w_up_sh], axis=1).astype(BF16)
    base, h2, lgt = _post(attn_o, mlstm_o, x2, gt1, row(g_attn_out), w_out.astype(BF16), row(g_norm2),
                          1.0 + sc2, sh2, gt2, w_router.T.astype(BF16), wgu, w_down_sh.astype(BF16), S)

    idx_t, w_t, rank_t, cnt = _route(lgt, router_bias.reshape(N_EXPERTS, 1))
    counts = cnt[:, 0].astype(I32)
    padded = (counts + ROW_BLOCK - 1) // ROW_BLOCK * ROW_BLOCK
    padded_end = jnp.cumsum(padded)
    off = padded_end - padded
    n_rows = T * TOP_K + N_EXPERTS * ROW_BLOCK
    nblk = n_rows // ROW_BLOCK
    block_e = jnp.minimum(jnp.searchsorted(padded_end, jnp.arange(nblk, dtype=I32) * ROW_BLOCK, side="right"),
                          N_EXPERTS - 1).astype(I32)
    n_used = jnp.maximum(padded_end[-1:] // ROW_BLOCK, 1).astype(I32)
    dest_t = _dest(idx_t, rank_t, off.reshape(N_EXPERTS, 1))

    xs = _dispatch(dest_t, h2, n_rows)
    y = _experts(block_e, n_used, xs, w_gate_exp, w_up_exp, w_down_exp)
    return _combine(dest_t, y, base, w_t.T, gt2, S)


def kernel(x, c, positions, w_ada, b_ada, g_norm1, w_in, g_cq, w_uq, g_ckv, w_ukv, g_qn, g_kn, g_attn_out, conv_w, conv_b, w_mq, w_mk, b_igate, b_fgate, g_mlstm_out, w_out, g_norm2, w_router, router_bias, w_gate_exp, w_up_exp, w_down_exp, w_gate_sh, w_up_sh, w_down_sh):
    B, S, _ = x.shape
    depth = w_ada.shape[0]
    x2 = x.reshape(B * S, D_MODEL)
    for l in range(depth):
        ada = _ada(c, w_ada[l], b_ada[l])
        x2 = _layer(x2, ada, positions, B, S, g_norm1[l], w_in[l], g_cq[l], w_uq[l], g_ckv[l], w_ukv[l], g_qn[l],
                    g_kn[l], g_attn_out[l], conv_w[l], conv_b[l], w_mq[l], w_mk[l], b_igate[l], b_fgate[l],
                    g_mlstm_out[l], w_out[l], g_norm2[l], w_router[l], router_bias[l], w_gate_exp[l], w_up_exp[l],
                    w_down_exp[l], w_gate_sh[l], w_up_sh[l], w_down_sh[l])
    return x2.reshape(B, S, D_MODEL)
```

```python
import functools
import math

import jax
import jax.numpy as jnp
from jax import lax
from jax.experimental import pallas as pl
from jax.experimental.pallas import tpu as pltpu

F32 = jnp.float32
BF16 = jnp.bfloat16
I32 = jnp.int32

D_MODEL = 1024
MLA_HEADS = 8
QK_NOPE = 64
QK_ROPE = 32
QK_DIM = QK_NOPE + QK_ROPE
V_DIM = 64
Q_LORA = 384
KV_LORA = 256
ROPE_THETA = 10000.0
MLSTM_HEADS = 4
MLSTM_DIM = 128
MLSTM_INNER = MLSTM_HEADS * MLSTM_DIM
CONV_WIDTH = 4
N_EXPERTS = 256
TOP_K = 8
N_GROUPS = 8
TOPK_GROUPS = 4
GROUP_SIZE = N_EXPERTS // N_GROUPS
EXPERT_DIM = 256
SHARED_DIM = 256
ROUTED_SCALE = 2.5
RMS_EPS = 1e-6

LANES = 128
SEQ_TILE = 256
ROW_BLOCK = 256
VMEM_LIMIT = 56 * 1024 * 1024

_OFF_CQ = 0
_OFF_CKV = _OFF_CQ + Q_LORA
_OFF_Z = _OFF_CKV + KV_LORA
_OFF_O = _OFF_Z + MLSTM_INNER
_OFF_KR = _OFF_O + MLSTM_INNER
_OFF_KRS = _OFF_KR + LANES
_WA_COLS = _OFF_KRS + LANES
_WT_ROWS = MLSTM_INNER + 16


def _params(n_axes, vmem=VMEM_LIMIT):
    return pltpu.CompilerParams(dimension_semantics=("arbitrary",) * n_axes, vmem_limit_bytes=vmem)


def _dot(a, b):
    return jnp.dot(a, b, preferred_element_type=F32)


def _dot_nt(a, b):
    return lax.dot_general(a, b, (((1,), (1,)), ((), ())), preferred_element_type=F32)


def _sigmoid(x):
    return 1.0 / (1.0 + jnp.exp(-x))


def _silu(x):
    return x * _sigmoid(x)


def _ada_body(c_ref, w_ref, b_ref, o_ref):
    c = c_ref[...]
    cond = _silu(c).astype(BF16)
    o_ref[...] = _dot(cond, w_ref[...].astype(BF16)) + b_ref[...]


def _ada(c, w_ada, b_ada):
    B = c.shape[0]
    n = w_ada.shape[1]
    tn = 1024
    return pl.pallas_call(
        _ada_body,
        out_shape=jax.ShapeDtypeStruct((B, n), F32),
        grid=(n // tn,),
        in_specs=[
            pl.BlockSpec((B, D_MODEL), lambda j: (0, 0)),
            pl.BlockSpec((D_MODEL, tn), lambda j: (0, j)),
            pl.BlockSpec((1, tn), lambda j: (0, j)),
        ],
        out_specs=pl.BlockSpec((B, tn), lambda j: (0, j)),
        compiler_params=_params(1),
        name="ada",
    )(c, w_ada, b_ada.reshape(1, n))


def _inproj_body(x_ref, sc_ref, sh_ref, g1_ref, wa_ref, wt_ref, gcq_ref, wuq_ref, wuqs_ref, gckv_ref,
                 wuk_ref, wuvt_ref, gq_ref, gqs_ref, gk_ref, gks_ref, cos_ref, sin_ref,
                 z_ref, o_ref, vt_ref, gt_ref, q_ref, k_ref, mvt_ref):
    x = x_ref[...]
    hn = x * lax.rsqrt(jnp.mean(x * x, axis=-1, keepdims=True) + RMS_EPS) * g1_ref[...]
    hm = (hn * sc_ref[...] + sh_ref[...]).astype(BF16)
    p = _dot(hm, wa_ref[...])
    z_ref[...] = p[:, _OFF_Z:_OFF_Z + MLSTM_INNER].astype(BF16)
    o_ref[...] = p[:, _OFF_O:_OFF_O + MLSTM_INNER].astype(BF16)
    rt = _dot_nt(wt_ref[...], hm)
    vt_ref[...] = rt[:MLSTM_INNER].astype(BF16)
    gt_ref[...] = rt[MLSTM_INNER:]

    cq = p[:, _OFF_CQ:_OFF_CQ + Q_LORA]
    cqn = (cq * lax.rsqrt(jnp.mean(cq * cq, axis=-1, keepdims=True) + RMS_EPS) * gcq_ref[...]).astype(BF16)
    ckv = p[:, _OFF_CKV:_OFF_CKV + KV_LORA]
    ckvn = (ckv * lax.rsqrt(jnp.mean(ckv * ckv, axis=-1, keepdims=True) + RMS_EPS) * gckv_ref[...]).astype(BF16)
    qa = _dot(cqn, wuq_ref[...])
    qb = _dot(cqn, wuqs_ref[...])
    ka = _dot(ckvn, wuk_ref[...])
    mvt_ref[...] = _dot_nt(wuvt_ref[...], ckvn).astype(BF16)
    kr = p[:, _OFF_KR:_OFF_KR + LANES]
    krs = p[:, _OFF_KRS:_OFF_KRS + LANES]

    cosp = cos_ref[...]
    sinp = sin_ref[...]
    gqc = gq_ref[...] * cosp
    gqs = gqs_ref[...] * sinp
    gkc = gk_ref[...] * cosp
    k_rot = krs * (gks_ref[...] * sinp)
    kr_ss = jnp.sum(kr * kr, axis=-1, keepdims=True)
    q_scale = (QK_DIM ** -0.5) * math.log2(math.e)
    for h in range(MLA_HEADS):
        sl = slice(h * LANES, (h + 1) * LANES)
        qah = qa[:, sl]
        rq = lax.rsqrt(jnp.sum(qah * qah, axis=-1, keepdims=True) * (1.0 / QK_DIM) + RMS_EPS) * q_scale
        q_ref[:, sl] = ((qah * gqc + qb[:, sl] * gqs) * rq).astype(BF16)
        kah = ka[:, sl]
        rk = lax.rsqrt((jnp.sum(kah * kah, axis=-1, keepdims=True) + kr_ss) * (1.0 / QK_DIM) + RMS_EPS)
        k_ref[:, sl] = (((kah + kr) * gkc + k_rot) * rk).astype(BF16)


def _inproj(x2, sc1p, sh1, g1, wa, wt, gcq, wuq, wuqs, gckv, wuk, wuvt, gq, gqs, gk, gks, cosp, sinp, S):
    T = x2.shape[0]
    tm = SEQ_TILE
    nt = T // tm
    per_b = S // tm
    hp = MLA_HEADS * LANES
    full = lambda a: pl.BlockSpec(a.shape, lambda i: (0,) * a.ndim)
    row = lambda w: pl.BlockSpec((tm, w), lambda i: (i, 0))
    bmod = pl.BlockSpec((None, 1, D_MODEL), lambda i: (i // per_b, 0, 0))
    out_shapes = (
        jax.ShapeDtypeStruct((T, MLSTM_INNER), BF16),
        jax.ShapeDtypeStruct((T, MLSTM_INNER), BF16),
        jax.ShapeDtypeStruct((nt, MLSTM_INNER, tm), BF16),
        jax.ShapeDtypeStruct((nt, 16, tm), F32),
        jax.ShapeDtypeStruct((T, hp), BF16),
        jax.ShapeDtypeStruct((T, hp), BF16),
        jax.ShapeDtypeStruct((nt, MLA_HEADS * V_DIM, tm), BF16),
    )
    out_specs = (
        row(MLSTM_INNER), row(MLSTM_INNER),
        pl.BlockSpec((None, MLSTM_INNER, tm), lambda i: (i, 0, 0)),
        pl.BlockSpec((None, 16, tm), lambda i: (i, 0, 0)),
        row(hp), row(hp),
        pl.BlockSpec((None, MLA_HEADS * V_DIM, tm), lambda i: (i, 0, 0)),
    )
    in_specs = [row(D_MODEL), bmod, bmod, full(g1), full(wa), full(wt), full(gcq), full(wuq), full(wuqs),
                full(gckv), full(wuk), full(wuvt), full(gq), full(gqs), full(gk), full(gks),
                row(LANES), row(LANES)]
    return pl.pallas_call(
        _inproj_body, out_shape=out_shapes, grid=(nt,), in_specs=in_specs, out_specs=out_specs,
        compiler_params=_params(1), name="inproj",
    )(x2, sc1p, sh1, g1, wa, wt, gcq, wuq, wuqs, gckv, wuk, wuvt, gq, gqs, gk, gks, cosp, sinp)


ATTN_HEADS_PER_STEP = 4


def _attn_body(q_ref, k_ref, vt_ref, o_ref):
    i = pl.program_id(2)
    t = SEQ_TILE
    nh = ATTN_HEADS_PER_STEP
    key_pos = lax.broadcasted_iota(I32, (t, t), 0)
    qry_pos = lax.broadcasted_iota(I32, (t, t), 1)
    causal = key_pos <= qry_pos
    qs = [q_ref[:, hh * LANES:(hh + 1) * LANES] for hh in range(nh)]

    def tile(j, carry, diag):
        row0 = pl.multiple_of(j * t, t)
        out = []
        for hh in range(nh):
            m, l, acc = carry[hh]
            kj = k_ref[pl.ds(row0, t), hh * LANES:(hh + 1) * LANES]
            s = _dot_nt(kj, qs[hh])
            if diag:
                s = jnp.where(causal, s, -jnp.inf)
            m_new = jnp.maximum(m, jnp.max(s, axis=0, keepdims=True))
            p = jnp.exp2(s - m_new)
            alpha = jnp.exp2(m - m_new)
            l = alpha * l + jnp.sum(p, axis=0, keepdims=True)
            acc = alpha * acc + _dot(vt_ref[j, hh * V_DIM:(hh + 1) * V_DIM, :], p.astype(BF16))
            out.append((m_new, l, acc))
        return tuple(out)

    init = tuple((jnp.full((1, t), -jnp.inf, F32), jnp.zeros((1, t), F32), jnp.zeros((V_DIM, t), F32))
                 for _ in range(nh))
    carry = lax.fori_loop(0, i, lambda j, c: tile(j, c, False), init)
    final = tile(i, carry, True)
    outs = [acc / l for (_, l, acc) in final]
    o_ref[...] = jnp.concatenate(outs, axis=0).T.astype(BF16)


def _attn(q, k, mvt, B, S):
    T = q.shape[0]
    t = SEQ_TILE
    nh = ATTN_HEADS_PER_STEP
    per_b = S // t
    return pl.pallas_call(
        _attn_body,
        out_shape=jax.ShapeDtypeStruct((T, MLA_HEADS * V_DIM), BF16),
        grid=(B, MLA_HEADS // nh, per_b),
        in_specs=[
            pl.BlockSpec((t, nh * LANES), lambda b, hp, i: (b * per_b + i, hp)),
            pl.BlockSpec((S, nh * LANES), lambda b, hp, i: (b, hp)),
            pl.BlockSpec((per_b, nh * V_DIM, t), lambda b, hp, i: (b, hp, 0)),
        ],
        out_specs=pl.BlockSpec((t, nh * V_DIM), lambda b, hp, i: (b * per_b + i, hp)),
        compiler_params=_params(3), name="attn",
    )(q, k, mvt)


def _split3(a):
    hi = a.astype(BF16)
    r = a - hi.astype(F32)
    mid = r.astype(BF16)
    lo = (r - mid.astype(F32)).astype(BF16)
    return hi, mid, lo


def _mlstm_body(bi_ref, bf_ref, z_ref, vt_ref, g_ref, og_ref, cw_ref, cb_ref, wq_ref, wk_ref, gout_ref,
                out_ref, zpad, q_s, k_s, st_s, m_s, *, S):
    h = pl.program_id(1)
    L = SEQ_TILE
    nchunk = S // L
    hd = MLSTM_DIM
    zpad[0:8, :] = jnp.zeros((8, hd), F32)
    zpad[8:, :] = z_ref[...].astype(F32)
    zc = cb_ref[...] + jnp.zeros((S, hd), F32)
    for j in range(CONV_WIDTH):
        zc = zc + cw_ref[j:j + 1, :] * zpad[pl.ds(8 - (CONV_WIDTH - 1) + j, S), :]
    zs = _silu(zc).astype(BF16)
    q_s[...] = _dot(zs, wq_ref[...]).astype(BF16)
    k_s[...] = (_dot(zs, wk_ref[...]) * (hd ** -0.5)).astype(BF16)

    st_s[...] = jnp.zeros_like(st_s)
    m_s[...] = jnp.zeros_like(m_s)
    b_i = bi_ref[h]
    b_f = bf_ref[h]
    r_i = lax.broadcasted_iota(I32, (L, L), 0)
    c_i = lax.broadcasted_iota(I32, (L, L), 1)
    tril = r_i >= c_i
    strict_lower = jnp.where(r_i > c_i, 1.0, 0.0).astype(BF16)
    ones_row = jnp.where(lax.broadcasted_iota(I32, (hd, L), 0) == 0, 1.0, 0.0).astype(BF16)

    def chunk(c, _):
        start = pl.multiple_of(c * L, L)
        qc = q_s[pl.ds(start, L), :]
        kc = k_s[pl.ds(start, L), :]
        vt_aug = jnp.concatenate([vt_ref[c], ones_row], axis=0)
        li = g_ref[c, pl.ds(h, 1), :] + b_i
        fp = g_ref[c, pl.ds(MLSTM_HEADS + h, 1), :] + b_f
        lf = jnp.minimum(fp, 0.0) - jnp.log(1.0 + jnp.exp(-jnp.abs(fp)))
        a_mat = jnp.where(tril, lf, 0.0)
        a_hi, a_mid, a_lo = _split3(a_mat)
        d0 = _dot(a_hi, strict_lower) + _dot(a_mid, strict_lower) + _dot(a_lo, strict_lower)
        dmat = jnp.where(tril, d0 + li, -jnp.inf)
        b_col = jnp.sum(a_mat, axis=1, keepdims=True)
        m_prev = m_s[...]
        inter = b_col + m_prev
        mt = jnp.maximum(inter, jnp.max(dmat, axis=1, keepdims=True))
        w = jnp.exp(dmat - mt) * _dot_nt(qc, kc)
        decay = jnp.exp(inter - mt)
        st = st_s[...]
        num_aug = decay * _dot_nt(qc, st.astype(BF16)) + _dot_nt(w.astype(BF16), vt_aug)
        num = num_aug[:, :hd]
        den = num_aug[:, hd:hd + 1]
        hv = num / jnp.maximum(jnp.abs(den), jnp.exp(-mt))
        g = d0[L - 1:L, :] + li
        b_last = b_col[L - 1:L, :]
        m_new = jnp.maximum(b_last + m_prev, jnp.max(g, axis=1, keepdims=True))
        a = jnp.exp(g - m_new)
        cd = jnp.exp(b_last + m_prev - m_new)
        st_s[...] = cd * st + _dot((vt_aug.astype(F32) * a).astype(BF16), kc)
        m_s[...] = m_new
        hn = hv * lax.rsqrt(jnp.mean(hv * hv, axis=-1, keepdims=True) + RMS_EPS) * gout_ref[...]
        og = og_ref[pl.ds(start, L), :].astype(F32)
        out_ref[pl.ds(start, L), :] = (_sigmoid(og) * hn).astype(BF16)
        return 0

    lax.fori_loop(0, nchunk, chunk, 0)


def _mlstm(z, vt3, g3, og, conv_w, conv_b, wq, wk, b_i, b_f, g_out, B, S):
    T = z.shape[0]
    L = SEQ_TILE
    per_b = S // L
    hd = MLSTM_DIM
    smem = pl.BlockSpec(memory_space=pltpu.SMEM)
    seq = pl.BlockSpec((S, hd), lambda b, h: (b, h))
    return pl.pallas_call(
        functools.partial(_mlstm_body, S=S),
        out_shape=jax.ShapeDtypeStruct((T, MLSTM_INNER), BF16),
        grid=(B, MLSTM_HEADS),
        in_specs=[
            smem, smem, seq,
            pl.BlockSpec((per_b, hd, L), lambda b, h: (b, h, 0)),
            pl.BlockSpec((per_b, 16, L), lambda b, h: (b, 0, 0)),
            seq,
            pl.BlockSpec((CONV_WIDTH, hd), lambda b, h: (0, h)),
            pl.BlockSpec((1, hd), lambda b, h: (0, h)),
            pl.BlockSpec((None, hd, hd), lambda b, h: (h, 0, 0)),
            pl.BlockSpec((None, hd, hd), lambda b, h: (h, 0, 0)),
            pl.BlockSpec((1, hd), lambda b, h: (0, h)),
        ],
        out_specs=seq,
        scratch_shapes=[
            pltpu.VMEM((S + 8, hd), F32),
            pltpu.VMEM((S, hd), BF16),
            pltpu.VMEM((S, hd), BF16),
            pltpu.VMEM((2 * hd, hd), F32),
            pltpu.VMEM((1, 1), F32),
        ],
        compiler_params=_params(2), name="mlstm",
    )(b_i, b_f, z, vt3, g3, og, conv_w, conv_b, wq, wk, g_out)


def _post_body(ao_ref, mo_ref, x_ref, gt1_ref, gattn_ref, wout_ref, g2_ref, sc2_ref, sh2_ref, gt2_ref,
               wrt_ref, wgu_ref, wds_ref, base_ref, h2_ref, lg_ref):
    ao = ao_ref[...].astype(F32)
    aon = ao * lax.rsqrt(jnp.mean(ao * ao, axis=-1, keepdims=True) + RMS_EPS) * gattn_ref[...]
    mix_in = jnp.concatenate([aon.astype(BF16), mo_ref[...]], axis=1)
    x1 = x_ref[...] + gt1_ref[...] * _dot(mix_in, wout_ref[...])
    hn = x1 * lax.rsqrt(jnp.mean(x1 * x1, axis=-1, keepdims=True) + RMS_EPS) * g2_ref[...]
    h2 = hn * sc2_ref[...] + sh2_ref[...]
    h2_ref[...] = h2
    h2b = h2.astype(BF16)
    gu = _dot(h2b, wgu_ref[...])
    act = (_silu(gu[:, :SHARED_DIM]) * gu[:, SHARED_DIM:]).astype(BF16)
    base_ref[...] = x1 + gt2_ref[...] * _dot(act, wds_ref[...])
    lg_ref[...] = _dot_nt(wrt_ref[...], h2b)


def _post(ao, mo, x2, gt1, gattn, wout, g2, sc2p, sh2, gt2, wrt, wgu, wds, S):
    T = x2.shape[0]
    tm = SEQ_TILE
    per_b = S // tm
    full = lambda a: pl.BlockSpec(a.shape, lambda i: (0,) * a.ndim)
    row = lambda w: pl.BlockSpec((tm, w), lambda i: (i, 0))
    bmod = pl.BlockSpec((None, 1, D_MODEL), lambda i: (i // per_b, 0, 0))
    return pl.pallas_call(
        _post_body,
        out_shape=(jax.ShapeDtypeStruct((T, D_MODEL), F32), jax.ShapeDtypeStruct((T, D_MODEL), F32),
                   jax.ShapeDtypeStruct((N_EXPERTS, T), F32)),
        grid=(T // tm,),
        in_specs=[row(MLA_HEADS * V_DIM), row(MLSTM_INNER), row(D_MODEL), bmod, full(gattn), full(wout),
                  full(g2), bmod, bmod, bmod, full(wrt), full(wgu), full(wds)],
        out_specs=(row(D_MODEL), row(D_MODEL), pl.BlockSpec((N_EXPERTS, tm), lambda i: (0, i))),
        compiler_params=_params(1), name="post",
    )(ao, mo, x2, gt1, gattn, wout, g2, sc2p, sh2, gt2, wrt, wgu, wds)


def _first_max(x, rows, n):
    mx = jnp.max(x, axis=0, keepdims=True)
    idx = jnp.min(jnp.where(x == mx, rows, n), axis=0, keepdims=True)
    return mx, idx


def _route_body(lg_ref, bias_ref, idx_ref, w_ref, rank_ref, cnt_ref, carry):
    i = pl.program_id(0)
    tt = lg_ref.shape[1]

    @pl.when(i == 0)
    def _():
        carry[...] = jnp.zeros_like(carry)

    sc = _sigmoid(lg_ref[...])
    bi = sc + bias_ref[...]
    rows = lax.broadcasted_iota(I32, (N_EXPERTS, tt), 0)
    rows_g = lax.broadcasted_iota(I32, (GROUP_SIZE, tt), 0)
    rows_n = lax.broadcasted_iota(I32, (N_GROUPS, tt), 0)
    neg = -jnp.inf
    gs = []
    for g in range(N_GROUPS):
        xg = bi[g * GROUP_SIZE:(g + 1) * GROUP_SIZE]
        m1, i1 = _first_max(xg, rows_g, GROUP_SIZE)
        m2 = jnp.max(jnp.where(rows_g == i1, neg, xg), axis=0, keepdims=True)
        gs.append(m1 + m2)
    cur = jnp.concatenate(gs, axis=0)
    keep = jnp.zeros((N_GROUPS, tt), F32)
    for _ in range(TOPK_GROUPS):
        _, gi = _first_max(cur, rows_n, N_GROUPS)
        sel = rows_n == gi
        keep = jnp.where(sel, 1.0, keep)
        cur = jnp.where(sel, neg, cur)
    masked = jnp.concatenate(
        [jnp.where(keep[g:g + 1] > 0.0, bi[g * GROUP_SIZE:(g + 1) * GROUP_SIZE], neg) for g in range(N_GROUPS)], axis=0)
    cur = masked
    onehot = jnp.zeros((N_EXPERTS, tt), F32)
    idxs, ws = [], []
    for _ in range(TOP_K):
        _, ei = _first_max(cur, rows, N_EXPERTS)
        sel = rows == ei
        ws.append(jnp.sum(jnp.where(sel, sc, 0.0), axis=0, keepdims=True))
        idxs.append(ei)
        cur = jnp.where(sel, neg, cur)
        onehot = jnp.where(sel, 1.0, onehot)
    wsum = ws[0]
    for k in range(1, TOP_K):
        wsum = wsum + ws[k]
    idx_ref[...] = jnp.concatenate(idxs, axis=0)
    w_ref[...] = jnp.concatenate([w / wsum * ROUTED_SCALE for w in ws], axis=0)
    r_i = lax.broadcasted_iota(I32, (tt, tt), 0)
    c_i = lax.broadcasted_iota(I32, (tt, tt), 1)
    before = jnp.where(r_i < c_i, 1.0, 0.0).astype(BF16)
    tot = _dot(onehot.astype(BF16), before) + carry[...]
    rank_ref[...] = jnp.concatenate(
        [jnp.sum(jnp.where(rows == ei, tot, 0.0), axis=0, keepdims=True) for ei in idxs], axis=0).astype(I32)
    carry[...] = carry[...] + jnp.sum(onehot, axis=1, keepdims=True)
    cnt_ref[...] = jnp.broadcast_to(carry[...], cnt_ref.shape)


def _route(lgt, bias_col):
    T = lgt.shape[1]
    tt = SEQ_TILE
    blk = pl.BlockSpec((TOP_K, tt), lambda i: (0, i))
    return pl.pallas_call(
        _route_body,
        out_shape=(jax.ShapeDtypeStruct((TOP_K, T), I32), jax.ShapeDtypeStruct((TOP_K, T), F32),
                   jax.ShapeDtypeStruct((TOP_K, T), I32), jax.ShapeDtypeStruct((N_EXPERTS, LANES), F32)),
        grid=(T // tt,),
        in_specs=[pl.BlockSpec((N_EXPERTS, tt), lambda i: (0, i)), pl.BlockSpec((N_EXPERTS, 1), lambda i: (0, 0))],
        out_specs=(blk, blk, blk, pl.BlockSpec((N_EXPERTS, LANES), lambda i: (0, 0))),
        scratch_shapes=[pltpu.VMEM((N_EXPERTS, 1), F32)],
        compiler_params=_params(1), name="route",
    )(lgt, bias_col)


def _dest_body(idx_ref, rank_ref, off_ref, dest_ref):
    tt = idx_ref.shape[1]
    rows = lax.broadcasted_iota(I32, (N_EXPERTS, tt), 0)
    off = off_ref[...]
    outs = []
    for k in range(TOP_K):
        sel = rows == idx_ref[k:k + 1, :]
        outs.append(jnp.sum(jnp.where(sel, off, 0), axis=0, keepdims=True) + rank_ref[k:k + 1, :])
    dest_ref[...] = jnp.concatenate(outs, axis=0)


def _dest(idx_t, rank_t, off_col):
    T = idx_t.shape[1]
    tt = SEQ_TILE
    blk = pl.BlockSpec((TOP_K, tt), lambda i: (0, i))
    return pl.pallas_call(
        _dest_body, out_shape=jax.ShapeDtypeStruct((TOP_K, T), I32), grid=(T // tt,),
        in_specs=[blk, blk, pl.BlockSpec((N_EXPERTS, 1), lambda i: (0, 0))], out_specs=blk,
        compiler_params=_params(1), name="dest",
    )(idx_t, rank_t, off_col)


def _row_copy(src, s, dst, d, sem):
    return pltpu.make_async_copy(src.at[pl.ds(s, 1)], dst.at[pl.ds(d, 1)], sem)


def _dispatch_body(zstart_ref, zhas_ref, dest_ref, h2_ref, xs_ref, zeros_s, sem, zsem):
    tt = dest_ref.shape[1]

    @pl.when(pl.program_id(0) == 0)
    def _():
        zeros_s[...] = jnp.zeros_like(zeros_s)

        def zero_copy(e):
            start = pl.multiple_of(zstart_ref[e], ROW_BLOCK)
            return pltpu.make_async_copy(zeros_s, xs_ref.at[pl.ds(start, ROW_BLOCK)], zsem)

        def zissue(e, _):
            @pl.when(zhas_ref[e] > 0)
            def _():
                zero_copy(e).start()
            return 0

        def zdrain(e, _):
            @pl.when(zhas_ref[e] > 0)
            def _():
                zero_copy(e).wait()
            return 0

        lax.fori_loop(0, N_EXPERTS, zissue, 0)
        lax.fori_loop(0, N_EXPERTS, zdrain, 0)

    def issue(r, _):
        for k in range(TOP_K):
            _row_copy(h2_ref, r, xs_ref, dest_ref[k, r], sem).start(priority=k % 2)
        return 0

    def drain(r, _):
        for k in range(TOP_K):
            _row_copy(h2_ref, r, xs_ref, dest_ref[k, r], sem).wait()
        return 0

    lax.fori_loop(0, tt, issue, 0)
    lax.fori_loop(0, tt, drain, 0)


def _dispatch(zstart, zhas, dest_t, h2, n_rows):
    T = h2.shape[0]
    tt = SEQ_TILE
    grid_spec = pltpu.PrefetchScalarGridSpec(
        num_scalar_prefetch=2, grid=(T // tt,),
        in_specs=[pl.BlockSpec((TOP_K, tt), lambda i, zs, zh: (0, i), memory_space=pltpu.SMEM),
                  pl.BlockSpec((tt, D_MODEL), lambda i, zs, zh: (i, 0))],
        out_specs=pl.BlockSpec(memory_space=pl.ANY),
        scratch_shapes=[pltpu.VMEM((ROW_BLOCK, D_MODEL), h2.dtype), pltpu.SemaphoreType.DMA, pltpu.SemaphoreType.DMA],
    )
    return pl.pallas_call(
        _dispatch_body, out_shape=jax.ShapeDtypeStruct((n_rows, D_MODEL), h2.dtype), grid_spec=grid_spec,
        compiler_params=_params(1), name="dispatch",
    )(zstart, zhas, dest_t, h2)


def _experts_body(be_ref, nb_ref, x_ref, wg_ref, wu_ref, wd_ref, y_ref):
    j = pl.program_id(0)

    @pl.when(j < nb_ref[0])
    def _():
        x = x_ref[...].astype(BF16)
        g = _dot(x, wg_ref[...].astype(BF16))
        u = _dot(x, wu_ref[...].astype(BF16))
        act = (_silu(g) * u).astype(BF16)
        y_ref[...] = _dot(act, wd_ref[...].astype(BF16))


def _experts(block_e, n_used, xs, wg, wu, wd):
    n_rows = xs.shape[0]
    nblk = n_rows // ROW_BLOCK
    clamp = lambda j, nb: jnp.minimum(j, nb[0] - 1)
    xmap = lambda j, be, nb: (clamp(j, nb), 0)
    wmap = lambda j, be, nb: (be[clamp(j, nb)], 0, 0)
    grid_spec = pltpu.PrefetchScalarGridSpec(
        num_scalar_prefetch=2, grid=(nblk,),
        in_specs=[
            pl.BlockSpec((ROW_BLOCK, D_MODEL), xmap),
            pl.BlockSpec((None, D_MODEL, EXPERT_DIM), wmap),
            pl.BlockSpec((None, D_MODEL, EXPERT_DIM), wmap),
            pl.BlockSpec((None, EXPERT_DIM, D_MODEL), wmap),
        ],
        out_specs=pl.BlockSpec((ROW_BLOCK, D_MODEL), xmap),
    )
    return pl.pallas_call(
        _experts_body, out_shape=jax.ShapeDtypeStruct((n_rows, D_MODEL), F32), grid_spec=grid_spec,
        compiler_params=_params(1), name="experts",
    )(block_e, n_used, xs, wg, wu, wd)


def _combine_body(dest_cur_ref, dest_nxt_ref, y_ref, base_ref, w_ref, gt2_ref, out_ref, ybuf, sems):
    i = pl.program_id(0)
    n = pl.num_programs(0)
    tt = base_ref.shape[0]

    def gather(dest_ref, slot):
        def issue(r, _):
            for k in range(TOP_K):
                _row_copy(y_ref, dest_ref[k, r], ybuf.at[slot, k], r, sems.at[slot]).start(priority=k % 2)
            return 0
        lax.fori_loop(0, tt, issue, 0)

    def drain(slot):
        def wait(r, _):
            for k in range(TOP_K):
                _row_copy(y_ref, 0, ybuf.at[slot, k], r, sems.at[slot]).wait()
            return 0
        lax.fori_loop(0, tt, wait, 0)

    @pl.when(i == 0)
    def _():
        gather(dest_cur_ref, 0)

    for slot in range(2):
        @pl.when(jnp.logical_and(i + 1 < n, (i + 1) % 2 == slot))
        def _():
            gather(dest_nxt_ref, slot)

    for slot in range(2):
        @pl.when(i % 2 == slot)
        def _():
            drain(slot)
            w = w_ref[...]
            routed = w[:, 0:1] * ybuf[slot, 0]
            for k in range(1, TOP_K):
                routed = routed + w[:, k:k + 1] * ybuf[slot, k]
            out_ref[...] = base_ref[...] + gt2_ref[...] * routed


def _combine(dest_t, y, base, w_tk, gt2, S):
    T = base.shape[0]
    tt = LANES
    nt = T // tt
    per_b = S // tt
    smem = lambda f: pl.BlockSpec((TOP_K, tt), f, memory_space=pltpu.SMEM)
    return pl.pallas_call(
        _combine_body, out_shape=jax.ShapeDtypeStruct((T, D_MODEL), F32), grid=(nt,),
        in_specs=[
            smem(lambda i: (0, i)),
            smem(lambda i: (0, jnp.minimum(i + 1, nt - 1))),
            pl.BlockSpec(memory_space=pl.ANY),
            pl.BlockSpec((tt, D_MODEL), lambda i: (i, 0)),
            pl.BlockSpec((tt, TOP_K), lambda i: (i, 0)),
            pl.BlockSpec((None, 1, D_MODEL), lambda i: (i // per_b, 0, 0)),
        ],
        out_specs=pl.BlockSpec((tt, D_MODEL), lambda i: (i, 0)),
        scratch_shapes=[pltpu.VMEM((2, TOP_K, tt, D_MODEL), F32), pltpu.SemaphoreType.DMA((2,))],
        compiler_params=_params(1), name="combine",
    )(dest_t, dest_t, y, base, w_tk, gt2)


def _pad_heads(w, width):
    kdim = w.shape[0]
    w3 = w.reshape(kdim, MLA_HEADS, width)
    return jnp.pad(w3, ((0, 0), (0, 0), (0, LANES - width))).reshape(kdim, MLA_HEADS * LANES)


def _rot_partner(w_rope):
    half = QK_ROPE // 2
    return jnp.concatenate([-w_rope[..., half:], w_rope[..., :half]], axis=-1)


def _lane_vec(nope, rope):
    return jnp.concatenate([nope, rope, jnp.zeros((LANES - QK_DIM,), F32)]).reshape(1, LANES)


def _layer(x2, ada, positions, B, S, g_norm1, w_in, g_cq, w_uq, g_ckv, w_ukv, g_qn, g_kn, g_attn_out,
           conv_w, conv_b, w_mq, w_mk, b_igate, b_fgate, g_mlstm_out, w_out, g_norm2, w_router, router_bias,
           w_gate_exp, w_up_exp, w_down_exp, w_gate_sh, w_up_sh, w_down_sh):
    T = B * S
    sh1, sc1, gt1, sh2, sc2, gt2 = [a.reshape(B, 1, D_MODEL) for a in jnp.split(ada, 6, axis=-1)]
    row = lambda v: v.reshape(1, -1).astype(F32)

    inv_freq = ROPE_THETA ** (-jnp.arange(0, QK_ROPE, 2, dtype=F32) / QK_ROPE)
    ang = positions.astype(F32).reshape(T, 1) * inv_freq
    cos, sin = jnp.cos(ang), jnp.sin(ang)
    pad_r = jnp.zeros((T, LANES - QK_DIM), F32)
    cosp = jnp.concatenate([jnp.ones((T, QK_NOPE), F32), cos, cos, pad_r], axis=1)
    sinp = jnp.concatenate([jnp.zeros((T, QK_NOPE), F32), sin, sin, pad_r], axis=1)

    o0 = 0
    w_cq = w_in[:, o0:o0 + Q_LORA]; o0 += Q_LORA
    w_ckv = w_in[:, o0:o0 + KV_LORA]; o0 += KV_LORA
    w_kr = w_in[:, o0:o0 + QK_ROPE]; o0 += QK_ROPE
    w_z = w_in[:, o0:o0 + MLSTM_INNER]; o0 += MLSTM_INNER
    w_v = w_in[:, o0:o0 + MLSTM_INNER]; o0 += MLSTM_INNER
    w_o = w_in[:, o0:o0 + MLSTM_INNER]; o0 += MLSTM_INNER
    w_i = w_in[:, o0:o0 + MLSTM_HEADS]; o0 += MLSTM_HEADS
    w_f = w_in[:, o0:o0 + MLSTM_HEADS]
    zl = jnp.zeros((D_MODEL, QK_NOPE), F32)
    zr = jnp.zeros((D_MODEL, LANES - QK_DIM), F32)
    wa = jnp.concatenate([w_cq, w_ckv, w_z, w_o, zl, w_kr, zr, zl, _rot_partner(w_kr), zr], axis=1).astype(BF16)
    wt = jnp.concatenate([w_v.T, w_i.T, w_f.T, jnp.zeros((16 - 2 * MLSTM_HEADS, D_MODEL), F32)], axis=0).astype(BF16)

    uq3 = w_uq.reshape(Q_LORA, MLA_HEADS, QK_DIM)
    uq_rot = jnp.concatenate([jnp.zeros((Q_LORA, MLA_HEADS, QK_NOPE), F32), _rot_partner(uq3[..., QK_NOPE:])], axis=-1)
    wuq = _pad_heads(w_uq, QK_DIM).astype(BF16)
    wuqs = _pad_heads(uq_rot.reshape(Q_LORA, MLA_HEADS * QK_DIM), QK_DIM).astype(BF16)
    ukv3 = w_ukv.reshape(KV_LORA, MLA_HEADS, QK_NOPE + V_DIM)
    wuk = _pad_heads(ukv3[..., :QK_NOPE].reshape(KV_LORA, MLA_HEADS * QK_NOPE), QK_NOPE).astype(BF16)
    wuvt = ukv3[..., QK_NOPE:].reshape(KV_LORA, MLA_HEADS * V_DIM).T.astype(BF16)
    gq = _lane_vec(g_qn[:QK_NOPE], g_qn[QK_NOPE:])
    gqs = _lane_vec(jnp.zeros((QK_NOPE,), F32), _rot_partner(g_qn[QK_NOPE:]) * jnp.concatenate(
        [-jnp.ones((QK_ROPE // 2,), F32), jnp.ones((QK_ROPE // 2,), F32)]))
    gk = _lane_vec(g_kn[:QK_NOPE], g_kn[QK_NOPE:])
    gks = _lane_vec(jnp.zeros((QK_NOPE,), F32), _rot_partner(g_kn[QK_NOPE:]) * jnp.concatenate(
        [-jnp.ones((QK_ROPE // 2,), F32), jnp.ones((QK_ROPE // 2,), F32)]))

    z, og, vt3, g3, q, k, mvt = _inproj(
        x2, 1.0 + sc1, sh1, row(g_norm1), wa, wt, row(g_cq), wuq, wuqs, row(g_ckv), wuk, wuvt,
        gq, gqs, gk, gks, cosp, sinp, S)
    attn_o = _attn(q, k, mvt, B, S)
    mlstm_o = _mlstm(z, vt3, g3, og, conv_w.reshape(CONV_WIDTH, MLSTM_INNER), row(conv_b),
                     w_mq.astype(BF16), w_mk.astype(BF16), b_igate, b_fgate, row(g_mlstm_out), B, S)

    wgu = jnp.concatenate([w_gate_sh, w_up_sh], axis=1).astype(BF16)
    base, h2, lgt = _post(attn_o, mlstm_o, x2, gt1, row(g_attn_out), w_out.astype(BF16), row(g_norm2),
                          1.0 + sc2, sh2, gt2, w_router.T.astype(BF16), wgu, w_down_sh.astype(BF16), S)

    idx_t, w_t, rank_t, cnt = _route(lgt, router_bias.reshape(N_EXPERTS, 1))
    counts = cnt[:, 0].astype(I32)
    padded = (counts + ROW_BLOCK - 1) // ROW_BLOCK * ROW_BLOCK
    padded_end = jnp.cumsum(padded)
    off = padded_end - padded
    n_rows = T * TOP_K + N_EXPERTS * ROW_BLOCK
    nblk = n_rows // ROW_BLOCK
    block_start = jnp.arange(nblk, dtype=I32) * ROW_BLOCK
    block_e = jnp.minimum(jnp.sum((padded_end[None, :] <= block_start[:, None]).astype(I32), axis=1), N_EXPERTS - 1)
    n_used = jnp.maximum(padded_end[-1:] // ROW_BLOCK, 1).astype(I32)
    dest_t = _dest(idx_t, rank_t, off.reshape(N_EXPERTS, 1))

    xs = _dispatch(jnp.maximum(padded_end - ROW_BLOCK, 0).astype(I32), padded.astype(I32), dest_t, h2, n_rows)
    y = _experts(block_e, n_used, xs, w_gate_exp, w_up_exp, w_down_exp)
    return _combine(dest_t, y, base, w_t.T, gt2, S)


def kernel(x, c, positions, w_ada, b_ada, g_norm1, w_in, g_cq, w_uq, g_ckv, w_ukv, g_qn, g_kn, g_attn_out, conv_w, conv_b, w_mq, w_mk, b_igate, b_fgate, g_mlstm_out, w_out, g_norm2, w_router, router_bias, w_gate_exp, w_up_exp, w_down_exp, w_gate_sh, w_up_sh, w_down_sh):
    B, S, _ = x.shape
    depth = w_ada.shape[0]
    x2 = x.reshape(B * S, D_MODEL)
    for l in range(depth):
        ada = _ada(c, w_ada[l], b_ada[l])
        x2 = _layer(x2, ada, positions, B, S, g_norm1[l], w_in[l], g_cq[l], w_uq[l], g_ckv[l], w_ukv[l], g_qn[l],
                    g_kn[l], g_attn_out[l], conv_w[l], conv_b[l], w_mq[l], w_mk[l], b_igate[l], b_fgate[l],
                    g_mlstm_out[l], w_out[l], g_norm2[l], w_router[l], router_bias[l], w_gate_exp[l], w_up_exp[l],
                    w_down_exp[l], w_gate_sh[l], w_up_sh[l], w_down_sh[l])
    return x2.reshape(B, S, D_MODEL)
```

```python
import functools
import math

import jax
import jax.numpy as jnp
from jax import lax
from jax.experimental import pallas as pl
from jax.experimental.pallas import tpu as pltpu

F32 = jnp.float32
BF16 = jnp.bfloat16
I32 = jnp.int32

D_MODEL = 1024
MLA_HEADS = 8
QK_NOPE = 64
QK_ROPE = 32
QK_DIM = QK_NOPE + QK_ROPE
V_DIM = 64
Q_LORA = 384
KV_LORA = 256
ROPE_THETA = 10000.0
MLSTM_HEADS = 4
MLSTM_DIM = 128
MLSTM_INNER = MLSTM_HEADS * MLSTM_DIM
CONV_WIDTH = 4
N_EXPERTS = 256
TOP_K = 8
N_GROUPS = 8
TOPK_GROUPS = 4
GROUP_SIZE = N_EXPERTS // N_GROUPS
EXPERT_DIM = 256
SHARED_DIM = 256
ROUTED_SCALE = 2.5
RMS_EPS = 1e-6

LANES = 128
SEQ_TILE = 256
ROW_BLOCK = 256
VMEM_LIMIT = 56 * 1024 * 1024

_OFF_CQ = 0
_OFF_CKV = _OFF_CQ + Q_LORA
_OFF_Z = _OFF_CKV + KV_LORA
_OFF_O = _OFF_Z + MLSTM_INNER
_OFF_KR = _OFF_O + MLSTM_INNER
_OFF_KRS = _OFF_KR + LANES
_WA_COLS = _OFF_KRS + LANES
_WT_ROWS = MLSTM_INNER + 16


def _params(n_axes, vmem=VMEM_LIMIT):
    return pltpu.CompilerParams(dimension_semantics=("arbitrary",) * n_axes, vmem_limit_bytes=vmem)


def _dot(a, b):
    return jnp.dot(a, b, preferred_element_type=F32)


def _dot_nt(a, b):
    return lax.dot_general(a, b, (((1,), (1,)), ((), ())), preferred_element_type=F32)


def _sigmoid(x):
    return 1.0 / (1.0 + jnp.exp(-x))


def _silu(x):
    return x * _sigmoid(x)


U32 = jnp.uint32
PACKED = D_MODEL // 2


def _pack_rows(x):
    w = x.shape[1] // 2
    lo = lax.bitcast_convert_type(x[:, :w].astype(BF16).astype(F32), U32)
    hi = lax.bitcast_convert_type(x[:, w:].astype(BF16).astype(F32), U32)
    return lax.shift_right_logical(lo, U32(16)) | (hi & U32(0xFFFF0000))


def _unpack_rows(p):
    lo = lax.bitcast_convert_type(lax.shift_left(p, U32(16)), F32)
    hi = lax.bitcast_convert_type(p & U32(0xFFFF0000), F32)
    return lo, hi


def _ada_body(c_ref, w_ref, b_ref, o_ref):
    c = c_ref[...]
    cond = _silu(c).astype(BF16)
    o_ref[...] = _dot(cond, w_ref[...].astype(BF16)) + b_ref[...]


def _ada(c, w_ada, b_ada):
    B = c.shape[0]
    n = w_ada.shape[1]
    tn = 1024
    return pl.pallas_call(
        _ada_body,
        out_shape=jax.ShapeDtypeStruct((B, n), F32),
        grid=(n // tn,),
        in_specs=[
            pl.BlockSpec((B, D_MODEL), lambda j: (0, 0)),
            pl.BlockSpec((D_MODEL, tn), lambda j: (0, j)),
            pl.BlockSpec((1, tn), lambda j: (0, j)),
        ],
        out_specs=pl.BlockSpec((B, tn), lambda j: (0, j)),
        compiler_params=_params(1),
        name="ada",
    )(c, w_ada, b_ada.reshape(1, n))


def _inproj_body(x_ref, sc_ref, sh_ref, g1_ref, wa_ref, wt_ref, gcq_ref, wuq_ref, wuqs_ref, gckv_ref,
                 wuk_ref, wuv_ref, gq_ref, gqs_ref, gk_ref, gks_ref, cos_ref, sin_ref,
                 z_ref, o_ref, vt_ref, gt_ref, q_ref, k_ref, mv_ref):
    x = x_ref[...]
    hn = x * lax.rsqrt(jnp.mean(x * x, axis=-1, keepdims=True) + RMS_EPS) * g1_ref[...]
    hm = (hn * sc_ref[...] + sh_ref[...]).astype(BF16)
    p = _dot(hm, wa_ref[...])
    z_ref[...] = p[:, _OFF_Z:_OFF_Z + MLSTM_INNER].astype(BF16)
    o_ref[...] = p[:, _OFF_O:_OFF_O + MLSTM_INNER].astype(BF16)
    rt = _dot_nt(wt_ref[...], hm)
    vt_ref[...] = rt[:MLSTM_INNER].astype(BF16)
    gt_ref[...] = rt[MLSTM_INNER:]

    cq = p[:, _OFF_CQ:_OFF_CQ + Q_LORA]
    cqn = (cq * lax.rsqrt(jnp.mean(cq * cq, axis=-1, keepdims=True) + RMS_EPS) * gcq_ref[...]).astype(BF16)
    ckv = p[:, _OFF_CKV:_OFF_CKV + KV_LORA]
    ckvn = (ckv * lax.rsqrt(jnp.mean(ckv * ckv, axis=-1, keepdims=True) + RMS_EPS) * gckv_ref[...]).astype(BF16)
    qa = _dot(cqn, wuq_ref[...])
    qb = _dot(cqn, wuqs_ref[...])
    ka = _dot(ckvn, wuk_ref[...])
    mv_ref[...] = _dot(ckvn, wuv_ref[...]).astype(BF16)
    kr = p[:, _OFF_KR:_OFF_KR + LANES]
    krs = p[:, _OFF_KRS:_OFF_KRS + LANES]

    cosp = cos_ref[...]
    sinp = sin_ref[...]
    gqc = gq_ref[...] * cosp
    gqs = gqs_ref[...] * sinp
    gkc = gk_ref[...] * cosp
    k_rot = krs * (gks_ref[...] * sinp)
    kr_ss = jnp.sum(kr * kr, axis=-1, keepdims=True)
    q_scale = (QK_DIM ** -0.5) * math.log2(math.e)
    for h in range(MLA_HEADS):
        sl = slice(h * LANES, (h + 1) * LANES)
        qah = qa[:, sl]
        rq = lax.rsqrt(jnp.sum(qah * qah, axis=-1, keepdims=True) * (1.0 / QK_DIM) + RMS_EPS) * q_scale
        q_ref[:, sl] = ((qah * gqc + qb[:, sl] * gqs) * rq).astype(BF16)
        kah = ka[:, sl]
        rk = lax.rsqrt((jnp.sum(kah * kah, axis=-1, keepdims=True) + kr_ss) * (1.0 / QK_DIM) + RMS_EPS)
        k_ref[:, sl] = (((kah + kr) * gkc + k_rot) * rk).astype(BF16)


def _inproj(x2, sc1p, sh1, g1, wa, wt, gcq, wuq, wuqs, gckv, wuk, wuv, gq, gqs, gk, gks, cosp, sinp, S):
    T = x2.shape[0]
    tm = SEQ_TILE
    nt = T // tm
    per_b = S // tm
    hp = MLA_HEADS * LANES
    full = lambda a: pl.BlockSpec(a.shape, lambda i: (0,) * a.ndim)
    row = lambda w: pl.BlockSpec((tm, w), lambda i: (i, 0))
    bmod = pl.BlockSpec((None, 1, D_MODEL), lambda i: (i // per_b, 0, 0))
    out_shapes = (
        jax.ShapeDtypeStruct((T, MLSTM_INNER), BF16),
        jax.ShapeDtypeStruct((T, MLSTM_INNER), BF16),
        jax.ShapeDtypeStruct((nt, MLSTM_INNER, tm), BF16),
        jax.ShapeDtypeStruct((nt, 16, tm), F32),
        jax.ShapeDtypeStruct((T, hp), BF16),
        jax.ShapeDtypeStruct((T, hp), BF16),
        jax.ShapeDtypeStruct((T, MLA_HEADS * V_DIM), BF16),
    )
    out_specs = (
        row(MLSTM_INNER), row(MLSTM_INNER),
        pl.BlockSpec((None, MLSTM_INNER, tm), lambda i: (i, 0, 0)),
        pl.BlockSpec((None, 16, tm), lambda i: (i, 0, 0)),
        row(hp), row(hp), row(MLA_HEADS * V_DIM),
    )
    in_specs = [row(D_MODEL), bmod, bmod, full(g1), full(wa), full(wt), full(gcq), full(wuq), full(wuqs),
                full(gckv), full(wuk), full(wuv), full(gq), full(gqs), full(gk), full(gks),
                row(LANES), row(LANES)]
    return pl.pallas_call(
        _inproj_body, out_shape=out_shapes, grid=(nt,), in_specs=in_specs, out_specs=out_specs,
        compiler_params=_params(1), name="inproj",
    )(x2, sc1p, sh1, g1, wa, wt, gcq, wuq, wuqs, gckv, wuk, wuv, gq, gqs, gk, gks, cosp, sinp)


ATTN_HEADS_PER_STEP = 4


def _attn_body(q_ref, k_ref, v_ref, o_ref):
    i = pl.program_id(2)
    t = SEQ_TILE
    nh = ATTN_HEADS_PER_STEP
    qry_pos = lax.broadcasted_iota(I32, (t, t), 0)
    key_pos = lax.broadcasted_iota(I32, (t, t), 1)
    causal = key_pos <= qry_pos
    qs = [q_ref[:, hh * LANES:(hh + 1) * LANES] for hh in range(nh)]

    def scores(j, hh, diag):
        kj = k_ref[pl.ds(pl.multiple_of(j * t, t), t), hh * LANES:(hh + 1) * LANES]
        s = _dot_nt(qs[hh], kj)
        return jnp.where(causal, s, -jnp.inf) if diag else s

    def halves(x):
        return x[:, :LANES], x[:, LANES:]

    def max_pass(j, mx, diag):
        out = []
        for hh in range(nh):
            lo, hi = halves(scores(j, hh, diag))
            out.append(jnp.maximum(mx[hh], jnp.maximum(lo, hi)))
        return tuple(out)

    mx = tuple(jnp.full((t, LANES), -jnp.inf, F32) for _ in range(nh))
    mx = lax.fori_loop(0, i, lambda j, c: max_pass(j, c, False), mx)
    mx = max_pass(i, mx, True)
    mb = [jnp.broadcast_to(jnp.max(mx[hh], axis=1, keepdims=True), (t, LANES)) for hh in range(nh)]

    def acc_pass(j, carry, diag):
        row0 = pl.multiple_of(j * t, t)
        out = []
        for hh in range(nh):
            l128, acc = carry[hh]
            lo, hi = halves(scores(j, hh, diag))
            p_lo = jnp.exp2(lo - mb[hh])
            p_hi = jnp.exp2(hi - mb[hh])
            l128 = l128 + (p_lo + p_hi)
            p = jnp.concatenate([p_lo, p_hi], axis=1).astype(BF16)
            pair = hh // 2
            acc = acc + _dot(p, v_ref[pl.ds(row0, t), pair * LANES:(pair + 1) * LANES])
            out.append((l128, acc))
        return tuple(out)

    init = tuple((jnp.zeros((t, LANES), F32), jnp.zeros((t, LANES), F32)) for _ in range(nh))
    carry = lax.fori_loop(0, i, lambda j, c: acc_pass(j, c, False), init)
    final = acc_pass(i, carry, True)
    outs = [acc / jnp.sum(l128, axis=1, keepdims=True) for (l128, acc) in final]
    first_head = lax.broadcasted_iota(I32, (t, LANES), 1) < V_DIM
    o_ref[...] = jnp.concatenate(
        [jnp.where(first_head, outs[2 * pr], outs[2 * pr + 1]) for pr in range(nh // 2)], axis=1).astype(BF16)


def _attn(q, k, mvt, B, S):
    T = q.shape[0]
    t = SEQ_TILE
    nh = ATTN_HEADS_PER_STEP
    per_b = S // t
    return pl.pallas_call(
        _attn_body,
        out_shape=jax.ShapeDtypeStruct((T, MLA_HEADS * V_DIM), BF16),
        grid=(B, MLA_HEADS // nh, per_b),
        in_specs=[
            pl.BlockSpec((t, nh * LANES), lambda b, hp, i: (b * per_b + i, hp)),
            pl.BlockSpec((S, nh * LANES), lambda b, hp, i: (b, hp)),
            pl.BlockSpec((S, nh * V_DIM), lambda b, hp, i: (b, hp)),
        ],
        out_specs=pl.BlockSpec((t, nh * V_DIM), lambda b, hp, i: (b * per_b + i, hp)),
        compiler_params=_params(3), name="attn",
    )(q, k, mvt)


def _split3(a):
    hi = a.astype(BF16)
    r = a - hi.astype(F32)
    mid = r.astype(BF16)
    lo = (r - mid.astype(F32)).astype(BF16)
    return hi, mid, lo


def _mlstm_body(bi_ref, bf_ref, z_ref, vt_ref, g_ref, og_ref, cw_ref, cb_ref, wq_ref, wk_ref, gout_ref,
                out_ref, zpad, q_s, k_s, st_s, m_s, *, S):
    h = pl.program_id(1)
    L = SEQ_TILE
    nchunk = S // L
    hd = MLSTM_DIM
    zpad[0:8, :] = jnp.zeros((8, hd), F32)
    zpad[8:, :] = z_ref[...].astype(F32)
    zc = cb_ref[...] + jnp.zeros((S, hd), F32)
    for j in range(CONV_WIDTH):
        zc = zc + cw_ref[j:j + 1, :] * zpad[pl.ds(8 - (CONV_WIDTH - 1) + j, S), :]
    zs = _silu(zc).astype(BF16)
    q_s[...] = _dot(zs, wq_ref[...]).astype(BF16)
    k_s[...] = (_dot(zs, wk_ref[...]) * (hd ** -0.5)).astype(BF16)

    st_s[...] = jnp.zeros_like(st_s)
    m_s[...] = jnp.zeros_like(m_s)
    b_i = bi_ref[h]
    b_f = bf_ref[h]
    r_i = lax.broadcasted_iota(I32, (L, L), 0)
    c_i = lax.broadcasted_iota(I32, (L, L), 1)
    tril = r_i >= c_i
    strict_lower = jnp.where(r_i > c_i, 1.0, 0.0).astype(BF16)
    ones_row = jnp.where(lax.broadcasted_iota(I32, (hd, L), 0) == 0, 1.0, 0.0).astype(BF16)

    def chunk(c, _):
        start = pl.multiple_of(c * L, L)
        qc = q_s[pl.ds(start, L), :]
        kc = k_s[pl.ds(start, L), :]
        vt_aug = jnp.concatenate([vt_ref[c], ones_row], axis=0)
        li = g_ref[c, pl.ds(h, 1), :] + b_i
        fp = g_ref[c, pl.ds(MLSTM_HEADS + h, 1), :] + b_f
        lf = jnp.minimum(fp, 0.0) - jnp.log(1.0 + jnp.exp(-jnp.abs(fp)))
        a_mat = jnp.where(tril, lf, 0.0)
        a_hi, a_mid, a_lo = _split3(a_mat)
        d0 = _dot(a_hi, strict_lower) + _dot(a_mid, strict_lower) + _dot(a_lo, strict_lower)
        dmat = jnp.where(tril, d0 + li, -jnp.inf)
        b_col = jnp.sum(a_mat, axis=1, keepdims=True)
        m_prev = m_s[...]
        inter = b_col + m_prev
        mt = jnp.maximum(inter, jnp.max(dmat, axis=1, keepdims=True))
        w = jnp.exp(dmat - mt) * _dot_nt(qc, kc)
        decay = jnp.exp(inter - mt)
        st = st_s[...]
        num_aug = decay * _dot_nt(qc, st.astype(BF16)) + _dot_nt(w.astype(BF16), vt_aug)
        num = num_aug[:, :hd]
        den = num_aug[:, hd:hd + 1]
        hv = num / jnp.maximum(jnp.abs(den), jnp.exp(-mt))
        g = d0[L - 1:L, :] + li
        b_last = b_col[L - 1:L, :]
        m_new = jnp.maximum(b_last + m_prev, jnp.max(g, axis=1, keepdims=True))
        a = jnp.exp(g - m_new)
        cd = jnp.exp(b_last + m_prev - m_new)
        st_s[...] = cd * st + _dot((vt_aug.astype(F32) * a).astype(BF16), kc)
        m_s[...] = m_new
        hn = hv * lax.rsqrt(jnp.mean(hv * hv, axis=-1, keepdims=True) + RMS_EPS) * gout_ref[...]
        og = og_ref[pl.ds(start, L), :].astype(F32)
        out_ref[pl.ds(start, L), :] = (_sigmoid(og) * hn).astype(BF16)
        return 0

    lax.fori_loop(0, nchunk, chunk, 0)


def _mlstm(z, vt3, g3, og, conv_w, conv_b, wq, wk, b_i, b_f, g_out, B, S):
    T = z.shape[0]
    L = SEQ_TILE
    per_b = S // L
    hd = MLSTM_DIM
    smem = pl.BlockSpec(memory_space=pltpu.SMEM)
    seq = pl.BlockSpec((S, hd), lambda b, h: (b, h))
    return pl.pallas_call(
        functools.partial(_mlstm_body, S=S),
        out_shape=jax.ShapeDtypeStruct((T, MLSTM_INNER), BF16),
        grid=(B, MLSTM_HEADS),
        in_specs=[
            smem, smem, seq,
            pl.BlockSpec((per_b, hd, L), lambda b, h: (b, h, 0)),
            pl.BlockSpec((per_b, 16, L), lambda b, h: (b, 0, 0)),
            seq,
            pl.BlockSpec((CONV_WIDTH, hd), lambda b, h: (0, h)),
            pl.BlockSpec((1, hd), lambda b, h: (0, h)),
            pl.BlockSpec((None, hd, hd), lambda b, h: (h, 0, 0)),
            pl.BlockSpec((None, hd, hd), lambda b, h: (h, 0, 0)),
            pl.BlockSpec((1, hd), lambda b, h: (0, h)),
        ],
        out_specs=seq,
        scratch_shapes=[
            pltpu.VMEM((S + 8, hd), F32),
            pltpu.VMEM((S, hd), BF16),
            pltpu.VMEM((S, hd), BF16),
            pltpu.VMEM((2 * hd, hd), F32),
            pltpu.VMEM((1, 1), F32),
        ],
        compiler_params=_params(2), name="mlstm",
    )(b_i, b_f, z, vt3, g3, og, conv_w, conv_b, wq, wk, g_out)


def _post_body(ao_ref, mo_ref, x_ref, gt1_ref, gattn_ref, wout_ref, g2_ref, sc2_ref, sh2_ref, gt2_ref,
               wrt_ref, wgu_ref, wds_ref, base_ref, h2_ref, lg_ref):
    ao = ao_ref[...].astype(F32)
    aon = ao * lax.rsqrt(jnp.mean(ao * ao, axis=-1, keepdims=True) + RMS_EPS) * gattn_ref[...]
    mix_in = jnp.concatenate([aon.astype(BF16), mo_ref[...]], axis=1)
    x1 = x_ref[...] + gt1_ref[...] * _dot(mix_in, wout_ref[...])
    hn = x1 * lax.rsqrt(jnp.mean(x1 * x1, axis=-1, keepdims=True) + RMS_EPS) * g2_ref[...]
    h2 = hn * sc2_ref[...] + sh2_ref[...]
    h2_ref[...] = _pack_rows(h2)
    h2b = h2.astype(BF16)
    gu = _dot(h2b, wgu_ref[...])
    act = (_silu(gu[:, :SHARED_DIM]) * gu[:, SHARED_DIM:]).astype(BF16)
    base_ref[...] = x1 + gt2_ref[...] * _dot(act, wds_ref[...])
    lg_ref[...] = _dot_nt(wrt_ref[...], h2b)


def _post(ao, mo, x2, gt1, gattn, wout, g2, sc2p, sh2, gt2, wrt, wgu, wds, S):
    T = x2.shape[0]
    tm = SEQ_TILE
    per_b = S // tm
    full = lambda a: pl.BlockSpec(a.shape, lambda i: (0,) * a.ndim)
    row = lambda w: pl.BlockSpec((tm, w), lambda i: (i, 0))
    bmod = pl.BlockSpec((None, 1, D_MODEL), lambda i: (i // per_b, 0, 0))
    return pl.pallas_call(
        _post_body,
        out_shape=(jax.ShapeDtypeStruct((T, D_MODEL), F32), jax.ShapeDtypeStruct((T, PACKED), U32),
                   jax.ShapeDtypeStruct((N_EXPERTS, T), F32)),
        grid=(T // tm,),
        in_specs=[row(MLA_HEADS * V_DIM), row(MLSTM_INNER), row(D_MODEL), bmod, full(gattn), full(wout),
                  full(g2), bmod, bmod, bmod, full(wrt), full(wgu), full(wds)],
        out_specs=(row(D_MODEL), row(PACKED), pl.BlockSpec((N_EXPERTS, tm), lambda i: (0, i))),
        compiler_params=_params(1), name="post",
    )(ao, mo, x2, gt1, gattn, wout, g2, sc2p, sh2, gt2, wrt, wgu, wds)


def _first_max(x, rows, n):
    mx = jnp.max(x, axis=0, keepdims=True)
    idx = jnp.min(jnp.where(x == mx, rows, n), axis=0, keepdims=True)
    return mx, idx


def _route_body(lg_ref, bias_ref, idx_ref, w_ref, rank_ref, cnt_ref, carry):
    i = pl.program_id(0)
    tt = lg_ref.shape[1]

    @pl.when(i == 0)
    def _():
        carry[...] = jnp.zeros_like(carry)

    sc = _sigmoid(lg_ref[...])
    bi = sc + bias_ref[...]
    rows = lax.broadcasted_iota(I32, (N_EXPERTS, tt), 0)
    rows_g = lax.broadcasted_iota(I32, (GROUP_SIZE, tt), 0)
    rows_n = lax.broadcasted_iota(I32, (N_GROUPS, tt), 0)
    neg = -jnp.inf
    gs = []
    for g in range(N_GROUPS):
        xg = bi[g * GROUP_SIZE:(g + 1) * GROUP_SIZE]
        m1, i1 = _first_max(xg, rows_g, GROUP_SIZE)
        m2 = jnp.max(jnp.where(rows_g == i1, neg, xg), axis=0, keepdims=True)
        gs.append(m1 + m2)
    cur = jnp.concatenate(gs, axis=0)
    keep = jnp.zeros((N_GROUPS, tt), F32)
    for _ in range(TOPK_GROUPS):
        _, gi = _first_max(cur, rows_n, N_GROUPS)
        sel = rows_n == gi
        keep = jnp.where(sel, 1.0, keep)
        cur = jnp.where(sel, neg, cur)
    masked = jnp.concatenate(
        [jnp.where(keep[g:g + 1] > 0.0, bi[g * GROUP_SIZE:(g + 1) * GROUP_SIZE], neg) for g in range(N_GROUPS)], axis=0)
    cur = masked
    onehot = jnp.zeros((N_EXPERTS, tt), F32)
    idxs, ws = [], []
    for _ in range(TOP_K):
        _, ei = _first_max(cur, rows, N_EXPERTS)
        sel = rows == ei
        ws.append(jnp.sum(jnp.where(sel, sc, 0.0), axis=0, keepdims=True))
        idxs.append(ei)
        cur = jnp.where(sel, neg, cur)
        onehot = jnp.where(sel, 1.0, onehot)
    wsum = ws[0]
    for k in range(1, TOP_K):
        wsum = wsum + ws[k]
    idx_ref[...] = jnp.concatenate(idxs, axis=0)
    w_ref[...] = jnp.concatenate([w / wsum * ROUTED_SCALE for w in ws], axis=0)
    r_i = lax.broadcasted_iota(I32, (tt, tt), 0)
    c_i = lax.broadcasted_iota(I32, (tt, tt), 1)
    before = jnp.where(r_i < c_i, 1.0, 0.0).astype(BF16)
    tot = _dot(onehot.astype(BF16), before) + carry[...]
    rank_ref[...] = jnp.concatenate(
        [jnp.sum(jnp.where(rows == ei, tot, 0.0), axis=0, keepdims=True) for ei in idxs], axis=0).astype(I32)
    carry[...] = carry[...] + jnp.sum(onehot, axis=1, keepdims=True)
    cnt_ref[...] = jnp.broadcast_to(carry[...], cnt_ref.shape)


def _route(lgt, bias_col):
    T = lgt.shape[1]
    tt = SEQ_TILE
    blk = pl.BlockSpec((TOP_K, tt), lambda i: (0, i))
    return pl.pallas_call(
        _route_body,
        out_shape=(jax.ShapeDtypeStruct((TOP_K, T), I32), jax.ShapeDtypeStruct((TOP_K, T), F32),
                   jax.ShapeDtypeStruct((TOP_K, T), I32), jax.ShapeDtypeStruct((N_EXPERTS, LANES), F32)),
        grid=(T // tt,),
        in_specs=[pl.BlockSpec((N_EXPERTS, tt), lambda i: (0, i)), pl.BlockSpec((N_EXPERTS, 1), lambda i: (0, 0))],
        out_specs=(blk, blk, blk, pl.BlockSpec((N_EXPERTS, LANES), lambda i: (0, 0))),
        scratch_shapes=[pltpu.VMEM((N_EXPERTS, 1), F32)],
        compiler_params=_params(1), name="route",
    )(lgt, bias_col)


def _dest_body(idx_ref, rank_ref, off_ref, dest_ref):
    tt = idx_ref.shape[1]
    rows = lax.broadcasted_iota(I32, (N_EXPERTS, tt), 0)
    off = off_ref[...]
    outs = []
    for k in range(TOP_K):
        sel = rows == idx_ref[k:k + 1, :]
        outs.append(jnp.sum(jnp.where(sel, off, 0), axis=0, keepdims=True) + rank_ref[k:k + 1, :])
    dest_ref[...] = jnp.concatenate(outs, axis=0)


def _dest(idx_t, rank_t, off_col):
    T = idx_t.shape[1]
    tt = SEQ_TILE
    blk = pl.BlockSpec((TOP_K, tt), lambda i: (0, i))
    return pl.pallas_call(
        _dest_body, out_shape=jax.ShapeDtypeStruct((TOP_K, T), I32), grid=(T // tt,),
        in_specs=[blk, blk, pl.BlockSpec((N_EXPERTS, 1), lambda i: (0, 0))], out_specs=blk,
        compiler_params=_params(1), name="dest",
    )(idx_t, rank_t, off_col)


def _row_copy(src, s, dst, d, sem):
    return pltpu.make_async_copy(src.at[pl.ds(s, 1)], dst.at[pl.ds(d, 1)], sem)


def _dispatch_body(zstart_ref, zhas_ref, dest_ref, h2_ref, xs_ref, zeros_s, sem, zsem):
    tt = dest_ref.shape[1]

    @pl.when(pl.program_id(0) == 0)
    def _():
        zeros_s[...] = jnp.zeros_like(zeros_s)

        def zero_copy(e):
            start = pl.multiple_of(zstart_ref[e], ROW_BLOCK)
            return pltpu.make_async_copy(zeros_s, xs_ref.at[pl.ds(start, ROW_BLOCK)], zsem)

        def zissue(e, _):
            @pl.when(zhas_ref[e] > 0)
            def _():
                zero_copy(e).start()
            return 0

        def zdrain(e, _):
            @pl.when(zhas_ref[e] > 0)
            def _():
                zero_copy(e).wait()
            return 0

        lax.fori_loop(0, N_EXPERTS, zissue, 0)
        lax.fori_loop(0, N_EXPERTS, zdrain, 0)

    def issue(r, _):
        for k in range(TOP_K):
            _row_copy(h2_ref, r, xs_ref, dest_ref[k, r], sem).start(priority=k % 2)
        return 0

    def drain(r, _):
        for k in range(TOP_K):
            _row_copy(h2_ref, r, xs_ref, dest_ref[k, r], sem).wait()
        return 0

    lax.fori_loop(0, tt, issue, 0)
    lax.fori_loop(0, tt, drain, 0)


def _dispatch(zstart, zhas, dest_t, h2, n_rows):
    T = h2.shape[0]
    tt = SEQ_TILE
    grid_spec = pltpu.PrefetchScalarGridSpec(
        num_scalar_prefetch=2, grid=(T // tt,),
        in_specs=[pl.BlockSpec((TOP_K, tt), lambda i, zs, zh: (0, i), memory_space=pltpu.SMEM),
                  pl.BlockSpec((tt, PACKED), lambda i, zs, zh: (i, 0))],
        out_specs=pl.BlockSpec(memory_space=pl.ANY),
        scratch_shapes=[pltpu.VMEM((ROW_BLOCK, PACKED), h2.dtype), pltpu.SemaphoreType.DMA, pltpu.SemaphoreType.DMA],
    )
    return pl.pallas_call(
        _dispatch_body, out_shape=jax.ShapeDtypeStruct((n_rows, PACKED), h2.dtype), grid_spec=grid_spec,
        compiler_params=_params(1), name="dispatch",
    )(zstart, zhas, dest_t, h2)


def _experts_body(be_ref, nb_ref, x_ref, wg_ref, wu_ref, wd_ref, y_ref):
    j = pl.program_id(0)

    @pl.when(j < nb_ref[0])
    def _():
        x = jnp.concatenate(_unpack_rows(x_ref[...]), axis=1).astype(BF16)
        g = _dot(x, wg_ref[...].astype(BF16))
        u = _dot(x, wu_ref[...].astype(BF16))
        act = (_silu(g) * u).astype(BF16)
        y_ref[...] = _pack_rows(_dot(act, wd_ref[...].astype(BF16)))


def _experts(block_e, n_used, xs, wg, wu, wd):
    n_rows = xs.shape[0]
    nblk = n_rows // ROW_BLOCK
    clamp = lambda j, nb: jnp.minimum(j, nb[0] - 1)
    xmap = lambda j, be, nb: (clamp(j, nb), 0)
    wmap = lambda j, be, nb: (be[clamp(j, nb)], 0, 0)
    grid_spec = pltpu.PrefetchScalarGridSpec(
        num_scalar_prefetch=2, grid=(nblk,),
        in_specs=[
            pl.BlockSpec((ROW_BLOCK, PACKED), xmap),
            pl.BlockSpec((None, D_MODEL, EXPERT_DIM), wmap),
            pl.BlockSpec((None, D_MODEL, EXPERT_DIM), wmap),
            pl.BlockSpec((None, EXPERT_DIM, D_MODEL), wmap),
        ],
        out_specs=pl.BlockSpec((ROW_BLOCK, PACKED), xmap),
    )
    return pl.pallas_call(
        _experts_body, out_shape=jax.ShapeDtypeStruct((n_rows, PACKED), U32), grid_spec=grid_spec,
        compiler_params=_params(1), name="experts",
    )(block_e, n_used, xs, wg, wu, wd)


def _combine_body(dest_cur_ref, dest_nxt_ref, y_ref, base_ref, w_ref, gt2_ref, out_ref, ybuf, sems):
    i = pl.program_id(0)
    n = pl.num_programs(0)
    tt = base_ref.shape[0]

    def gather(dest_ref, slot):
        def issue(r, _):
            for k in range(TOP_K):
                _row_copy(y_ref, dest_ref[k, r], ybuf.at[slot, k], r, sems.at[slot]).start(priority=k % 2)
            return 0
        lax.fori_loop(0, tt, issue, 0)

    def drain(slot):
        def wait(r, _):
            for k in range(TOP_K):
                _row_copy(y_ref, 0, ybuf.at[slot, k], r, sems.at[slot]).wait()
            return 0
        lax.fori_loop(0, tt, wait, 0)

    @pl.when(i == 0)
    def _():
        gather(dest_cur_ref, 0)

    for slot in range(2):
        @pl.when(jnp.logical_and(i + 1 < n, (i + 1) % 2 == slot))
        def _():
            gather(dest_nxt_ref, slot)

    for slot in range(2):
        @pl.when(i % 2 == slot)
        def _():
            drain(slot)
            w = w_ref[...]
            lo, hi = _unpack_rows(ybuf[slot, 0])
            r_lo, r_hi = w[:, 0:1] * lo, w[:, 0:1] * hi
            for k in range(1, TOP_K):
                lo, hi = _unpack_rows(ybuf[slot, k])
                r_lo, r_hi = r_lo + w[:, k:k + 1] * lo, r_hi + w[:, k:k + 1] * hi
            out_ref[...] = base_ref[...] + gt2_ref[...] * jnp.concatenate([r_lo, r_hi], axis=1)


def _combine(dest_t, y, base, w_tk, gt2, S):
    T = base.shape[0]
    tt = LANES
    nt = T // tt
    per_b = S // tt
    smem = lambda f: pl.BlockSpec((TOP_K, tt), f, memory_space=pltpu.SMEM)
    return pl.pallas_call(
        _combine_body, out_shape=jax.ShapeDtypeStruct((T, D_MODEL), F32), grid=(nt,),
        in_specs=[
            smem(lambda i: (0, i)),
            smem(lambda i: (0, jnp.minimum(i + 1, nt - 1))),
            pl.BlockSpec(memory_space=pl.ANY),
            pl.BlockSpec((tt, D_MODEL), lambda i: (i, 0)),
            pl.BlockSpec((tt, TOP_K), lambda i: (i, 0)),
            pl.BlockSpec((None, 1, D_MODEL), lambda i: (i // per_b, 0, 0)),
        ],
        out_specs=pl.BlockSpec((tt, D_MODEL), lambda i: (i, 0)),
        scratch_shapes=[pltpu.VMEM((2, TOP_K, tt, PACKED), U32), pltpu.SemaphoreType.DMA((2,))],
        compiler_params=_params(1), name="combine",
    )(dest_t, dest_t, y, base, w_tk, gt2)


def _pad_heads(w, width):
    kdim = w.shape[0]
    w3 = w.reshape(kdim, MLA_HEADS, width)
    return jnp.pad(w3, ((0, 0), (0, 0), (0, LANES - width))).reshape(kdim, MLA_HEADS * LANES)


def _rot_partner(w_rope):
    half = QK_ROPE // 2
    return jnp.concatenate([-w_rope[..., half:], w_rope[..., :half]], axis=-1)


def _lane_vec(nope, rope):
    return jnp.concatenate([nope, rope, jnp.zeros((LANES - QK_DIM,), F32)]).reshape(1, LANES)


def _layer(x2, ada, positions, B, S, g_norm1, w_in, g_cq, w_uq, g_ckv, w_ukv, g_qn, g_kn, g_attn_out,
           conv_w, conv_b, w_mq, w_mk, b_igate, b_fgate, g_mlstm_out, w_out, g_norm2, w_router, router_bias,
           w_gate_exp, w_up_exp, w_down_exp, w_gate_sh, w_up_sh, w_down_sh):
    T = B * S
    sh1, sc1, gt1, sh2, sc2, gt2 = [a.reshape(B, 1, D_MODEL) for a in jnp.split(ada, 6, axis=-1)]
    row = lambda v: v.reshape(1, -1).astype(F32)

    inv_freq = ROPE_THETA ** (-jnp.arange(0, QK_ROPE, 2, dtype=F32) / QK_ROPE)
    ang = positions.astype(F32).reshape(T, 1) * inv_freq
    cos, sin = jnp.cos(ang), jnp.sin(ang)
    pad_r = jnp.zeros((T, LANES - QK_DIM), F32)
    cosp = jnp.concatenate([jnp.ones((T, QK_NOPE), F32), cos, cos, pad_r], axis=1)
    sinp = jnp.concatenate([jnp.zeros((T, QK_NOPE), F32), sin, sin, pad_r], axis=1)

    o0 = 0
    w_cq = w_in[:, o0:o0 + Q_LORA]; o0 += Q_LORA
    w_ckv = w_in[:, o0:o0 + KV_LORA]; o0 += KV_LORA
    w_kr = w_in[:, o0:o0 + QK_ROPE]; o0 += QK_ROPE
    w_z = w_in[:, o0:o0 + MLSTM_INNER]; o0 += MLSTM_INNER
    w_v = w_in[:, o0:o0 + MLSTM_INNER]; o0 += MLSTM_INNER
    w_o = w_in[:, o0:o0 + MLSTM_INNER]; o0 += MLSTM_INNER
    w_i = w_in[:, o0:o0 + MLSTM_HEADS]; o0 += MLSTM_HEADS
    w_f = w_in[:, o0:o0 + MLSTM_HEADS]
    zl = jnp.zeros((D_MODEL, QK_NOPE), F32)
    zr = jnp.zeros((D_MODEL, LANES - QK_DIM), F32)
    wa = jnp.concatenate([w_cq, w_ckv, w_z, w_o, zl, w_kr, zr, zl, _rot_partner(w_kr), zr], axis=1).astype(BF16)
    wt = jnp.concatenate([w_v.T, w_i.T, w_f.T, jnp.zeros((16 - 2 * MLSTM_HEADS, D_MODEL), F32)], axis=0).astype(BF16)

    uq3 = w_uq.reshape(Q_LORA, MLA_HEADS, QK_DIM)
    uq_rot = jnp.concatenate([jnp.zeros((Q_LORA, MLA_HEADS, QK_NOPE), F32), _rot_partner(uq3[..., QK_NOPE:])], axis=-1)
    wuq = _pad_heads(w_uq, QK_DIM).astype(BF16)
    wuqs = _pad_heads(uq_rot.reshape(Q_LORA, MLA_HEADS * QK_DIM), QK_DIM).astype(BF16)
    ukv3 = w_ukv.reshape(KV_LORA, MLA_HEADS, QK_NOPE + V_DIM)
    wuk = _pad_heads(ukv3[..., :QK_NOPE].reshape(KV_LORA, MLA_HEADS * QK_NOPE), QK_NOPE).astype(BF16)
    wuv = ukv3[..., QK_NOPE:].reshape(KV_LORA, MLA_HEADS * V_DIM).astype(BF16)
    gq = _lane_vec(g_qn[:QK_NOPE], g_qn[QK_NOPE:])
    gqs = _lane_vec(jnp.zeros((QK_NOPE,), F32), _rot_partner(g_qn[QK_NOPE:]) * jnp.concatenate(
        [-jnp.ones((QK_ROPE // 2,), F32), jnp.ones((QK_ROPE // 2,), F32)]))
    gk = _lane_vec(g_kn[:QK_NOPE], g_kn[QK_NOPE:])
    gks = _lane_vec(jnp.zeros((QK_NOPE,), F32), _rot_partner(g_kn[QK_NOPE:]) * jnp.concatenate(
        [-jnp.ones((QK_ROPE // 2,), F32), jnp.ones((QK_ROPE // 2,), F32)]))

    z, og, vt3, g3, q, k, mv = _inproj(
        x2, 1.0 + sc1, sh1, row(g_norm1), wa, wt, row(g_cq), wuq, wuqs, row(g_ckv), wuk, wuv,
        gq, gqs, gk, gks, cosp, sinp, S)
    attn_o = _attn(q, k, mv, B, S)
    mlstm_o = _mlstm(z, vt3, g3, og, conv_w.reshape(CONV_WIDTH, MLSTM_INNER), row(conv_b),
                     w_mq.astype(BF16), w_mk.astype(BF16), b_igate, b_fgate, row(g_mlstm_out), B, S)

    wgu = jnp.concatenate([w_gate_sh, w_up_sh], axis=1).astype(BF16)
    base, h2, lgt = _post(attn_o, mlstm_o, x2, gt1, row(g_attn_out), w_out.astype(BF16), row(g_norm2),
                          1.0 + sc2, sh2, gt2, w_router.T.astype(BF16), wgu, w_down_sh.astype(BF16), S)

    idx_t, w_t, rank_t, cnt = _route(lgt, router_bias.reshape(N_EXPERTS, 1))
    counts = cnt[:, 0].astype(I32)
    padded = (counts + ROW_BLOCK - 1) // ROW_BLOCK * ROW_BLOCK
    padded_end = jnp.cumsum(padded)
    off = padded_end - padded
    n_rows = T * TOP_K + N_EXPERTS * ROW_BLOCK
    nblk = n_rows // ROW_BLOCK
    block_start = jnp.arange(nblk, dtype=I32) * ROW_BLOCK
    block_e = jnp.minimum(jnp.sum((padded_end[None, :] <= block_start[:, None]).astype(I32), axis=1), N_EXPERTS - 1)
    n_used = jnp.maximum(padded_end[-1:] // ROW_BLOCK, 1).astype(I32)
    dest_t = _dest(idx_t, rank_t, off.reshape(N_EXPERTS, 1))

    xs = _dispatch(jnp.maximum(padded_end - ROW_BLOCK, 0).astype(I32), padded.astype(I32), dest_t, h2, n_rows)
    y = _experts(block_e, n_used, xs, w_gate_exp, w_up_exp, w_down_exp)
    return _combine(dest_t, y, base, w_t.T, gt2, S)


def kernel(x, c, positions, w_ada, b_ada, g_norm1, w_in, g_cq, w_uq, g_ckv, w_ukv, g_qn, g_kn, g_attn_out, conv_w, conv_b, w_mq, w_mk, b_igate, b_fgate, g_mlstm_out, w_out, g_norm2, w_router, router_bias, w_gate_exp, w_up_exp, w_down_exp, w_gate_sh, w_up_sh, w_down_sh):
    B, S, _ = x.shape
    depth = w_ada.shape[0]
    x2 = x.reshape(B * S, D_MODEL)
    for l in range(depth):
        ada = _ada(c, w_ada[l], b_ada[l])
        x2 = _layer(x2, ada, positions, B, S, g_norm1[l], w_in[l], g_cq[l], w_uq[l], g_ckv[l], w_ukv[l], g_qn[l],
                    g_kn[l], g_attn_out[l], conv_w[l], conv_b[l], w_mq[l], w_mk[l], b_igate[l], b_fgate[l],
                    g_mlstm_out[l], w_out[l], g_norm2[l], w_router[l], router_bias[l], w_gate_exp[l], w_up_exp[l],
                    w_down_exp[l], w_gate_sh[l], w_up_sh[l], w_down_sh[l])
    return x2.reshape(B, S, D_MODEL)
```

```python
import functools
import math

import jax
import jax.numpy as jnp
from jax import lax
from jax.experimental import pallas as pl
from jax.experimental.pallas import tpu as pltpu
from jax.experimental.pallas import tpu_sc as plsc

F32 = jnp.float32
BF16 = jnp.bfloat16
I32 = jnp.int32

D_MODEL = 1024
MLA_HEADS = 8
QK_NOPE = 64
QK_ROPE = 32
QK_DIM = QK_NOPE + QK_ROPE
V_DIM = 64
Q_LORA = 384
KV_LORA = 256
ROPE_THETA = 10000.0
MLSTM_HEADS = 4
MLSTM_DIM = 128
MLSTM_INNER = MLSTM_HEADS * MLSTM_DIM
CONV_WIDTH = 4
N_EXPERTS = 256
TOP_K = 8
N_GROUPS = 8
TOPK_GROUPS = 4
GROUP_SIZE = N_EXPERTS // N_GROUPS
EXPERT_DIM = 256
SHARED_DIM = 256
ROUTED_SCALE = 2.5
RMS_EPS = 1e-6

LANES = 128
SEQ_TILE = 256
ROW_BLOCK = 256
VMEM_LIMIT = 56 * 1024 * 1024

_OFF_CQ = 0
_OFF_CKV = _OFF_CQ + Q_LORA
_OFF_Z = _OFF_CKV + KV_LORA
_OFF_O = _OFF_Z + MLSTM_INNER
_OFF_KR = _OFF_O + MLSTM_INNER
_OFF_KRS = _OFF_KR + LANES
_WA_COLS = _OFF_KRS + LANES
_WT_ROWS = MLSTM_INNER + 16


def _params(n_axes, vmem=VMEM_LIMIT):
    return pltpu.CompilerParams(dimension_semantics=("arbitrary",) * n_axes, vmem_limit_bytes=vmem)


def _dot(a, b):
    return jnp.dot(a, b, preferred_element_type=F32)


def _dot_nt(a, b):
    return lax.dot_general(a, b, (((1,), (1,)), ((), ())), preferred_element_type=F32)


def _sigmoid(x):
    return 1.0 / (1.0 + jnp.exp(-x))


def _silu(x):
    return x * _sigmoid(x)


U32 = jnp.uint32
PACKED = D_MODEL // 2


def _pack_rows(x):
    w = x.shape[1] // 2
    lo = lax.bitcast_convert_type(x[:, :w].astype(BF16).astype(F32), U32)
    hi = lax.bitcast_convert_type(x[:, w:].astype(BF16).astype(F32), U32)
    return lax.shift_right_logical(lo, U32(16)) | (hi & U32(0xFFFF0000))


def _unpack_rows(p):
    lo = lax.bitcast_convert_type(lax.shift_left(p, U32(16)), F32)
    hi = lax.bitcast_convert_type(p & U32(0xFFFF0000), F32)
    return lo, hi


def _ada_body(c_ref, w_ref, b_ref, o_ref):
    c = c_ref[...]
    cond = _silu(c).astype(BF16)
    o_ref[...] = _dot(cond, w_ref[...].astype(BF16)) + b_ref[...]


def _ada(c, w_ada, b_ada):
    B = c.shape[0]
    n = w_ada.shape[1]
    tn = 1024
    return pl.pallas_call(
        _ada_body,
        out_shape=jax.ShapeDtypeStruct((B, n), F32),
        grid=(n // tn,),
        in_specs=[
            pl.BlockSpec((B, D_MODEL), lambda j: (0, 0)),
            pl.BlockSpec((D_MODEL, tn), lambda j: (0, j)),
            pl.BlockSpec((1, tn), lambda j: (0, j)),
        ],
        out_specs=pl.BlockSpec((B, tn), lambda j: (0, j)),
        compiler_params=_params(1),
        name="ada",
    )(c, w_ada, b_ada.reshape(1, n))


def _inproj_body(x_ref, sc_ref, sh_ref, g1_ref, wa_ref, wt_ref, gcq_ref, wuq_ref, wuqs_ref, gckv_ref,
                 wuk_ref, wuv_ref, gq_ref, gqs_ref, gk_ref, gks_ref, cos_ref, sin_ref,
                 z_ref, o_ref, vt_ref, gt_ref, q_ref, k_ref, mv_ref):
    x = x_ref[...]
    hn = x * lax.rsqrt(jnp.mean(x * x, axis=-1, keepdims=True) + RMS_EPS) * g1_ref[...]
    hm = (hn * sc_ref[...] + sh_ref[...]).astype(BF16)
    p = _dot(hm, wa_ref[...])
    z_ref[...] = p[:, _OFF_Z:_OFF_Z + MLSTM_INNER].astype(BF16)
    o_ref[...] = p[:, _OFF_O:_OFF_O + MLSTM_INNER].astype(BF16)
    rt = _dot_nt(wt_ref[...], hm)
    vt_ref[...] = rt[:MLSTM_INNER].astype(BF16)
    gt_ref[...] = rt[MLSTM_INNER:]

    cq = p[:, _OFF_CQ:_OFF_CQ + Q_LORA]
    cqn = (cq * lax.rsqrt(jnp.mean(cq * cq, axis=-1, keepdims=True) + RMS_EPS) * gcq_ref[...]).astype(BF16)
    ckv = p[:, _OFF_CKV:_OFF_CKV + KV_LORA]
    ckvn = (ckv * lax.rsqrt(jnp.mean(ckv * ckv, axis=-1, keepdims=True) + RMS_EPS) * gckv_ref[...]).astype(BF16)
    qa = _dot(cqn, wuq_ref[...])
    qb = _dot(cqn, wuqs_ref[...])
    ka = _dot(ckvn, wuk_ref[...])
    mv_ref[...] = _dot(ckvn, wuv_ref[...]).astype(BF16)
    kr = p[:, _OFF_KR:_OFF_KR + LANES]
    krs = p[:, _OFF_KRS:_OFF_KRS + LANES]

    cosp = cos_ref[...]
    sinp = sin_ref[...]
    gqc = gq_ref[...] * cosp
    gqs = gqs_ref[...] * sinp
    gkc = gk_ref[...] * cosp
    k_rot = krs * (gks_ref[...] * sinp)
    kr_ss = jnp.sum(kr * kr, axis=-1, keepdims=True)
    q_scale = (QK_DIM ** -0.5) * math.log2(math.e)
    for h in range(MLA_HEADS):
        sl = slice(h * LANES, (h + 1) * LANES)
        qah = qa[:, sl]
        rq = lax.rsqrt(jnp.sum(qah * qah, axis=-1, keepdims=True) * (1.0 / QK_DIM) + RMS_EPS) * q_scale
        q_ref[:, sl] = ((qah * gqc + qb[:, sl] * gqs) * rq).astype(BF16)
        kah = ka[:, sl]
        rk = lax.rsqrt((jnp.sum(kah * kah, axis=-1, keepdims=True) + kr_ss) * (1.0 / QK_DIM) + RMS_EPS)
        k_ref[:, sl] = (((kah + kr) * gkc + k_rot) * rk).astype(BF16)


def _inproj(x2, sc1p, sh1, g1, wa, wt, gcq, wuq, wuqs, gckv, wuk, wuv, gq, gqs, gk, gks, cosp, sinp, S):
    T = x2.shape[0]
    tm = SEQ_TILE
    nt = T // tm
    per_b = S // tm
    hp = MLA_HEADS * LANES
    full = lambda a: pl.BlockSpec(a.shape, lambda i: (0,) * a.ndim)
    row = lambda w: pl.BlockSpec((tm, w), lambda i: (i, 0))
    bmod = pl.BlockSpec((None, 1, D_MODEL), lambda i: (i // per_b, 0, 0))
    out_shapes = (
        jax.ShapeDtypeStruct((T, MLSTM_INNER), BF16),
        jax.ShapeDtypeStruct((T, MLSTM_INNER), BF16),
        jax.ShapeDtypeStruct((nt, MLSTM_INNER, tm), BF16),
        jax.ShapeDtypeStruct((nt, 16, tm), F32),
        jax.ShapeDtypeStruct((T, hp), BF16),
        jax.ShapeDtypeStruct((T, hp), BF16),
        jax.ShapeDtypeStruct((T, MLA_HEADS * V_DIM), BF16),
    )
    out_specs = (
        row(MLSTM_INNER), row(MLSTM_INNER),
        pl.BlockSpec((None, MLSTM_INNER, tm), lambda i: (i, 0, 0)),
        pl.BlockSpec((None, 16, tm), lambda i: (i, 0, 0)),
        row(hp), row(hp), row(MLA_HEADS * V_DIM),
    )
    in_specs = [row(D_MODEL), bmod, bmod, full(g1), full(wa), full(wt), full(gcq), full(wuq), full(wuqs),
                full(gckv), full(wuk), full(wuv), full(gq), full(gqs), full(gk), full(gks),
                row(LANES), row(LANES)]
    return pl.pallas_call(
        _inproj_body, out_shape=out_shapes, grid=(nt,), in_specs=in_specs, out_specs=out_specs,
        compiler_params=_params(1), name="inproj",
    )(x2, sc1p, sh1, g1, wa, wt, gcq, wuq, wuqs, gckv, wuk, wuv, gq, gqs, gk, gks, cosp, sinp)


ATTN_HEADS_PER_STEP = 4


def _attn_body(q_ref, k_ref, v_ref, o_ref):
    i = pl.program_id(2)
    t = SEQ_TILE
    nh = ATTN_HEADS_PER_STEP
    qry_pos = lax.broadcasted_iota(I32, (t, t), 0)
    key_pos = lax.broadcasted_iota(I32, (t, t), 1)
    causal = key_pos <= qry_pos
    qs = [q_ref[:, hh * LANES:(hh + 1) * LANES] for hh in range(nh)]

    def scores(j, hh, diag):
        kj = k_ref[pl.ds(pl.multiple_of(j * t, t), t), hh * LANES:(hh + 1) * LANES]
        s = _dot_nt(qs[hh], kj)
        return jnp.where(causal, s, -jnp.inf) if diag else s

    def halves(x):
        return x[:, :LANES], x[:, LANES:]

    def max_pass(j, mx, diag):
        out = []
        for hh in range(nh):
            lo, hi = halves(scores(j, hh, diag))
            out.append(jnp.maximum(mx[hh], jnp.maximum(lo, hi)))
        return tuple(out)

    mx = tuple(jnp.full((t, LANES), -jnp.inf, F32) for _ in range(nh))
    mx = lax.fori_loop(0, i, lambda j, c: max_pass(j, c, False), mx)
    mx = max_pass(i, mx, True)
    mb = [jnp.broadcast_to(jnp.max(mx[hh], axis=1, keepdims=True), (t, LANES)) for hh in range(nh)]

    def acc_pass(j, carry, diag):
        row0 = pl.multiple_of(j * t, t)
        out = []
        for hh in range(nh):
            l128, acc = carry[hh]
            lo, hi = halves(scores(j, hh, diag))
            p_lo = jnp.exp2(lo - mb[hh])
            p_hi = jnp.exp2(hi - mb[hh])
            l128 = l128 + (p_lo + p_hi)
            p = jnp.concatenate([p_lo, p_hi], axis=1).astype(BF16)
            pair = hh // 2
            acc = acc + _dot(p, v_ref[pl.ds(row0, t), pair * LANES:(pair + 1) * LANES])
            out.append((l128, acc))
        return tuple(out)

    init = tuple((jnp.zeros((t, LANES), F32), jnp.zeros((t, LANES), F32)) for _ in range(nh))
    carry = lax.fori_loop(0, i, lambda j, c: acc_pass(j, c, False), init)
    final = acc_pass(i, carry, True)
    outs = [acc / jnp.sum(l128, axis=1, keepdims=True) for (l128, acc) in final]
    first_head = lax.broadcasted_iota(I32, (t, LANES), 1) < V_DIM
    o_ref[...] = jnp.concatenate(
        [jnp.where(first_head, outs[2 * pr], outs[2 * pr + 1]) for pr in range(nh // 2)], axis=1).astype(BF16)


def _attn(q, k, mvt, B, S):
    T = q.shape[0]
    t = SEQ_TILE
    nh = ATTN_HEADS_PER_STEP
    per_b = S // t
    return pl.pallas_call(
        _attn_body,
        out_shape=jax.ShapeDtypeStruct((T, MLA_HEADS * V_DIM), BF16),
        grid=(B, MLA_HEADS // nh, per_b),
        in_specs=[
            pl.BlockSpec((t, nh * LANES), lambda b, hp, i: (b * per_b + i, hp)),
            pl.BlockSpec((S, nh * LANES), lambda b, hp, i: (b, hp)),
            pl.BlockSpec((S, nh * V_DIM), lambda b, hp, i: (b, hp)),
        ],
        out_specs=pl.BlockSpec((t, nh * V_DIM), lambda b, hp, i: (b * per_b + i, hp)),
        compiler_params=_params(3), name="attn",
    )(q, k, mvt)


def _split3(a):
    hi = a.astype(BF16)
    r = a - hi.astype(F32)
    mid = r.astype(BF16)
    lo = (r - mid.astype(F32)).astype(BF16)
    return hi, mid, lo


def _mlstm_body(bi_ref, bf_ref, z_ref, vt_ref, g_ref, og_ref, cw_ref, cb_ref, wq_ref, wk_ref, gout_ref,
                out_ref, zpad, q_s, k_s, st_s, m_s, *, S):
    h = pl.program_id(1)
    L = SEQ_TILE
    nchunk = S // L
    hd = MLSTM_DIM
    zpad[0:8, :] = jnp.zeros((8, hd), F32)
    zpad[8:, :] = z_ref[...].astype(F32)
    zc = cb_ref[...] + jnp.zeros((S, hd), F32)
    for j in range(CONV_WIDTH):
        zc = zc + cw_ref[j:j + 1, :] * zpad[pl.ds(8 - (CONV_WIDTH - 1) + j, S), :]
    zs = _silu(zc).astype(BF16)
    q_s[...] = _dot(zs, wq_ref[...]).astype(BF16)
    k_s[...] = (_dot(zs, wk_ref[...]) * (hd ** -0.5)).astype(BF16)

    st_s[...] = jnp.zeros_like(st_s)
    m_s[...] = jnp.zeros_like(m_s)
    b_i = bi_ref[h]
    b_f = bf_ref[h]
    r_i = lax.broadcasted_iota(I32, (L, L), 0)
    c_i = lax.broadcasted_iota(I32, (L, L), 1)
    tril = r_i >= c_i
    strict_lower = jnp.where(r_i > c_i, 1.0, 0.0).astype(BF16)
    ones_row = jnp.where(lax.broadcasted_iota(I32, (hd, L), 0) == 0, 1.0, 0.0).astype(BF16)

    def chunk(c, _):
        start = pl.multiple_of(c * L, L)
        qc = q_s[pl.ds(start, L), :]
        kc = k_s[pl.ds(start, L), :]
        vt_aug = jnp.concatenate([vt_ref[c], ones_row], axis=0)
        li = g_ref[c, pl.ds(h, 1), :] + b_i
        fp = g_ref[c, pl.ds(MLSTM_HEADS + h, 1), :] + b_f
        lf = jnp.minimum(fp, 0.0) - jnp.log(1.0 + jnp.exp(-jnp.abs(fp)))
        a_mat = jnp.where(tril, lf, 0.0)
        a_hi, a_mid, a_lo = _split3(a_mat)
        d0 = _dot(a_hi, strict_lower) + _dot(a_mid, strict_lower) + _dot(a_lo, strict_lower)
        dmat = jnp.where(tril, d0 + li, -jnp.inf)
        b_col = jnp.sum(a_mat, axis=1, keepdims=True)
        m_prev = m_s[...]
        inter = b_col + m_prev
        mt = jnp.maximum(inter, jnp.max(dmat, axis=1, keepdims=True))
        w = jnp.exp(dmat - mt) * _dot_nt(qc, kc)
        decay = jnp.exp(inter - mt)
        st = st_s[...]
        num_aug = decay * _dot_nt(qc, st.astype(BF16)) + _dot_nt(w.astype(BF16), vt_aug)
        num = num_aug[:, :hd]
        den = num_aug[:, hd:hd + 1]
        hv = num / jnp.maximum(jnp.abs(den), jnp.exp(-mt))
        g = d0[L - 1:L, :] + li
        b_last = b_col[L - 1:L, :]
        m_new = jnp.maximum(b_last + m_prev, jnp.max(g, axis=1, keepdims=True))
        a = jnp.exp(g - m_new)
        cd = jnp.exp(b_last + m_prev - m_new)
        st_s[...] = cd * st + _dot((vt_aug.astype(F32) * a).astype(BF16), kc)
        m_s[...] = m_new
        hn = hv * lax.rsqrt(jnp.mean(hv * hv, axis=-1, keepdims=True) + RMS_EPS) * gout_ref[...]
        og = og_ref[pl.ds(start, L), :].astype(F32)
        out_ref[pl.ds(start, L), :] = (_sigmoid(og) * hn).astype(BF16)
        return 0

    lax.fori_loop(0, nchunk, chunk, 0)


def _mlstm(z, vt3, g3, og, conv_w, conv_b, wq, wk, b_i, b_f, g_out, B, S):
    T = z.shape[0]
    L = SEQ_TILE
    per_b = S // L
    hd = MLSTM_DIM
    smem = pl.BlockSpec(memory_space=pltpu.SMEM)
    seq = pl.BlockSpec((S, hd), lambda b, h: (b, h))
    return pl.pallas_call(
        functools.partial(_mlstm_body, S=S),
        out_shape=jax.ShapeDtypeStruct((T, MLSTM_INNER), BF16),
        grid=(B, MLSTM_HEADS),
        in_specs=[
            smem, smem, seq,
            pl.BlockSpec((per_b, hd, L), lambda b, h: (b, h, 0)),
            pl.BlockSpec((per_b, 16, L), lambda b, h: (b, 0, 0)),
            seq,
            pl.BlockSpec((CONV_WIDTH, hd), lambda b, h: (0, h)),
            pl.BlockSpec((1, hd), lambda b, h: (0, h)),
            pl.BlockSpec((None, hd, hd), lambda b, h: (h, 0, 0)),
            pl.BlockSpec((None, hd, hd), lambda b, h: (h, 0, 0)),
            pl.BlockSpec((1, hd), lambda b, h: (0, h)),
        ],
        out_specs=seq,
        scratch_shapes=[
            pltpu.VMEM((S + 8, hd), F32),
            pltpu.VMEM((S, hd), BF16),
            pltpu.VMEM((S, hd), BF16),
            pltpu.VMEM((2 * hd, hd), F32),
            pltpu.VMEM((1, 1), F32),
        ],
        compiler_params=_params(2), name="mlstm",
    )(b_i, b_f, z, vt3, g3, og, conv_w, conv_b, wq, wk, g_out)


def _post_body(ao_ref, mo_ref, x_ref, gt1_ref, gattn_ref, wout_ref, g2_ref, sc2_ref, sh2_ref, gt2_ref,
               wrt_ref, wgu_ref, wds_ref, base_ref, h2_ref, lg_ref):
    ao = ao_ref[...].astype(F32)
    aon = ao * lax.rsqrt(jnp.mean(ao * ao, axis=-1, keepdims=True) + RMS_EPS) * gattn_ref[...]
    mix_in = jnp.concatenate([aon.astype(BF16), mo_ref[...]], axis=1)
    x1 = x_ref[...] + gt1_ref[...] * _dot(mix_in, wout_ref[...])
    hn = x1 * lax.rsqrt(jnp.mean(x1 * x1, axis=-1, keepdims=True) + RMS_EPS) * g2_ref[...]
    h2 = hn * sc2_ref[...] + sh2_ref[...]
    h2_ref[...] = _pack_rows(h2)
    h2b = h2.astype(BF16)
    gu = _dot(h2b, wgu_ref[...])
    act = (_silu(gu[:, :SHARED_DIM]) * gu[:, SHARED_DIM:]).astype(BF16)
    base_ref[...] = x1 + gt2_ref[...] * _dot(act, wds_ref[...])
    lg_ref[...] = _dot_nt(wrt_ref[...], h2b)


def _post(ao, mo, x2, gt1, gattn, wout, g2, sc2p, sh2, gt2, wrt, wgu, wds, S):
    T = x2.shape[0]
    tm = SEQ_TILE
    per_b = S // tm
    full = lambda a: pl.BlockSpec(a.shape, lambda i: (0,) * a.ndim)
    row = lambda w: pl.BlockSpec((tm, w), lambda i: (i, 0))
    bmod = pl.BlockSpec((None, 1, D_MODEL), lambda i: (i // per_b, 0, 0))
    return pl.pallas_call(
        _post_body,
        out_shape=(jax.ShapeDtypeStruct((T, D_MODEL), F32), jax.ShapeDtypeStruct((T, PACKED), U32),
                   jax.ShapeDtypeStruct((N_EXPERTS, T), F32)),
        grid=(T // tm,),
        in_specs=[row(MLA_HEADS * V_DIM), row(MLSTM_INNER), row(D_MODEL), bmod, full(gattn), full(wout),
                  full(g2), bmod, bmod, bmod, full(wrt), full(wgu), full(wds)],
        out_specs=(row(D_MODEL), row(PACKED), pl.BlockSpec((N_EXPERTS, tm), lambda i: (0, i))),
        compiler_params=_params(1), name="post",
    )(ao, mo, x2, gt1, gattn, wout, g2, sc2p, sh2, gt2, wrt, wgu, wds)


def _first_max(x, rows, n):
    mx = jnp.max(x, axis=0, keepdims=True)
    idx = jnp.min(jnp.where(x == mx, rows, n), axis=0, keepdims=True)
    return mx, idx


def _route_body(lg_ref, bias_ref, idx_ref, w_ref, rank_ref, cnt_ref, carry):
    i = pl.program_id(0)
    tt = lg_ref.shape[1]

    @pl.when(i == 0)
    def _():
        carry[...] = jnp.zeros_like(carry)

    sc = _sigmoid(lg_ref[...])
    bi = sc + bias_ref[...]
    rows = lax.broadcasted_iota(I32, (N_EXPERTS, tt), 0)
    rows_g = lax.broadcasted_iota(I32, (GROUP_SIZE, tt), 0)
    rows_n = lax.broadcasted_iota(I32, (N_GROUPS, tt), 0)
    neg = -jnp.inf
    gs = []
    for g in range(N_GROUPS):
        xg = bi[g * GROUP_SIZE:(g + 1) * GROUP_SIZE]
        m1, i1 = _first_max(xg, rows_g, GROUP_SIZE)
        m2 = jnp.max(jnp.where(rows_g == i1, neg, xg), axis=0, keepdims=True)
        gs.append(m1 + m2)
    cur = jnp.concatenate(gs, axis=0)
    keep = jnp.zeros((N_GROUPS, tt), F32)
    for _ in range(TOPK_GROUPS):
        _, gi = _first_max(cur, rows_n, N_GROUPS)
        sel = rows_n == gi
        keep = jnp.where(sel, 1.0, keep)
        cur = jnp.where(sel, neg, cur)
    masked = jnp.concatenate(
        [jnp.where(keep[g:g + 1] > 0.0, bi[g * GROUP_SIZE:(g + 1) * GROUP_SIZE], neg) for g in range(N_GROUPS)], axis=0)
    cur = masked
    onehot = jnp.zeros((N_EXPERTS, tt), F32)
    idxs, ws = [], []
    for _ in range(TOP_K):
        _, ei = _first_max(cur, rows, N_EXPERTS)
        sel = rows == ei
        ws.append(jnp.sum(jnp.where(sel, sc, 0.0), axis=0, keepdims=True))
        idxs.append(ei)
        cur = jnp.where(sel, neg, cur)
        onehot = jnp.where(sel, 1.0, onehot)
    wsum = ws[0]
    for k in range(1, TOP_K):
        wsum = wsum + ws[k]
    idx_ref[...] = jnp.concatenate(idxs, axis=0)
    w_ref[...] = jnp.concatenate([w / wsum * ROUTED_SCALE for w in ws], axis=0)
    r_i = lax.broadcasted_iota(I32, (tt, tt), 0)
    c_i = lax.broadcasted_iota(I32, (tt, tt), 1)
    before = jnp.where(r_i < c_i, 1.0, 0.0).astype(BF16)
    tot = _dot(onehot.astype(BF16), before) + carry[...]
    rank_ref[...] = jnp.concatenate(
        [jnp.sum(jnp.where(rows == ei, tot, 0.0), axis=0, keepdims=True) for ei in idxs], axis=0).astype(I32)
    carry[...] = carry[...] + jnp.sum(onehot, axis=1, keepdims=True)
    cnt_ref[...] = jnp.broadcast_to(carry[...], cnt_ref.shape)


def _route(lgt, bias_col):
    T = lgt.shape[1]
    tt = SEQ_TILE
    blk = pl.BlockSpec((TOP_K, tt), lambda i: (0, i))
    return pl.pallas_call(
        _route_body,
        out_shape=(jax.ShapeDtypeStruct((TOP_K, T), I32), jax.ShapeDtypeStruct((TOP_K, T), F32),
                   jax.ShapeDtypeStruct((TOP_K, T), I32), jax.ShapeDtypeStruct((N_EXPERTS, LANES), F32)),
        grid=(T // tt,),
        in_specs=[pl.BlockSpec((N_EXPERTS, tt), lambda i: (0, i)), pl.BlockSpec((N_EXPERTS, 1), lambda i: (0, 0))],
        out_specs=(blk, blk, blk, pl.BlockSpec((N_EXPERTS, LANES), lambda i: (0, 0))),
        scratch_shapes=[pltpu.VMEM((N_EXPERTS, 1), F32)],
        compiler_params=_params(1), name="route",
    )(lgt, bias_col)


def _dest_body(idx_ref, rank_ref, off_ref, dest_ref):
    tt = idx_ref.shape[1]
    rows = lax.broadcasted_iota(I32, (N_EXPERTS, tt), 0)
    off = off_ref[...]
    outs = []
    for k in range(TOP_K):
        sel = rows == idx_ref[k:k + 1, :]
        outs.append(jnp.sum(jnp.where(sel, off, 0), axis=0, keepdims=True) + rank_ref[k:k + 1, :])
    dest_ref[...] = jnp.concatenate(outs, axis=0)


def _dest(idx_t, rank_t, off_col):
    T = idx_t.shape[1]
    tt = SEQ_TILE
    blk = pl.BlockSpec((TOP_K, tt), lambda i: (0, i))
    return pl.pallas_call(
        _dest_body, out_shape=jax.ShapeDtypeStruct((TOP_K, T), I32), grid=(T // tt,),
        in_specs=[blk, blk, pl.BlockSpec((N_EXPERTS, 1), lambda i: (0, 0))], out_specs=blk,
        compiler_params=_params(1), name="dest",
    )(idx_t, rank_t, off_col)


SC_WINDOW = 128


def _invert_slots(dest_flat, init):
    n = dest_flat.shape[0]
    mesh = plsc.VectorSubcoreMesh(core_axis_name="c", subcore_axis_name="s")
    slots = jnp.arange(n, dtype=I32).reshape(1, n)
    out = jax.new_ref(init)

    @pl.kernel(out_type=(), mesh=mesh, scratch_types=[])
    def scatter(d_hbm, s_hbm, o_hbm):
        def body(d_vmem, s_vmem):
            pltpu.sync_copy(s_vmem.at[0], o_hbm.at[d_vmem.at[0]])

        pltpu.emit_pipeline(
            body, grid=(n // SC_WINDOW,),
            in_specs=[pl.BlockSpec((1, SC_WINDOW), lambda i: (0, i)), pl.BlockSpec((1, SC_WINDOW), lambda i: (0, i))],
            out_specs=[], core_axis_name=("c", "s"), dimension_semantics=(pltpu.PARALLEL,),
        )(d_hbm, s_hbm)

    scatter(dest_flat.reshape(1, n), slots, out)
    return out[...]


def _experts_body(be_ref, nb_ref, rs_prev_ref, rs_cur_ref, rs_nxt_ref, h2_ref, wg_ref, wu_ref, wd_ref, yt_ref,
                  xb0, xb1, yb0, yb1, gsem, ssem, *, n_tok, spare_row0):
    del be_ref
    j = pl.program_id(0)
    nb = nb_ref[0]
    rb = ROW_BLOCK
    xb = (xb0, xb1)
    yb = (yb0, yb1)

    def gather(rs_ref, r, slot):
        tok = rs_ref[0, r] & (n_tok - 1)
        return pltpu.make_async_copy(h2_ref.at[pl.ds(tok, 1)], xb[slot].at[pl.ds(r, 1)], gsem.at[slot])

    def scatter(row, r, slot):
        return pltpu.make_async_copy(yb[slot].at[pl.ds(r, 1)], yt_ref.at[pl.ds(row, 1)], ssem.at[slot])

    def loop(fn):
        lax.fori_loop(0, rb, lambda r, c: (fn(r), c)[1], 0)

    @pl.when(j < nb)
    def _():
        @pl.when(j == 0)
        def _():
            yb0[...] = jnp.zeros_like(yb0)
            yb1[...] = jnp.zeros_like(yb1)
            loop(lambda r: gather(rs_cur_ref, r, 0).start())
            loop(lambda r: scatter(spare_row0 + r, r, 0).start())

        def main(p):
            q = 1 - p
            for r in range(rb):
                scatter(0, r, p).wait()
            for r in range(rb):
                gather(rs_cur_ref, r, p).wait()
            for r in range(rb):
                scatter(rs_prev_ref[0, r], r, q).start(priority=r % 2)
            for r in range(rb):
                gather(rs_nxt_ref, r, q).start(priority=r % 2)
            x = jnp.concatenate(_unpack_rows(xb[p][...]), axis=1).astype(BF16)
            g = _dot(x, wg_ref[...].astype(BF16))
            u = _dot(x, wu_ref[...].astype(BF16))
            act = (_silu(g) * u).astype(BF16)
            yb[p][...] = _pack_rows(_dot(act, wd_ref[...].astype(BF16)))

        for p in range(2):
            @pl.when(j % 2 == p)
            def _():
                main(p)

        for p in range(2):
            @pl.when(jnp.logical_and(j == nb - 1, j % 2 == p))
            def _():
                q = 1 - p
                loop(lambda r: scatter(rs_cur_ref[0, r], r, p).start())
                loop(lambda r: scatter(0, r, q).wait())
                loop(lambda r: scatter(0, r, p).wait())
                loop(lambda r: gather(rs_cur_ref, r, q).wait())


def _experts(block_e, n_used, rs_all, h2, wg, wu, wd, n_out_rows, spare_row0):
    T = h2.shape[0]
    assert T & (T - 1) == 0, "token count must be a power of two (slot -> token uses a bit mask)"
    nblk = rs_all.shape[0] - 1
    smem = lambda f: pl.BlockSpec((None, 1, ROW_BLOCK), f, memory_space=pltpu.SMEM)
    live = lambda j, nb: jnp.minimum(j, nb[0] - 1)
    wmap = lambda j, be, nb: (be[live(j, nb)], 0, 0)
    anyspec = pl.BlockSpec(memory_space=pl.ANY)
    buf = pltpu.VMEM((ROW_BLOCK, PACKED), U32)
    grid_spec = pltpu.PrefetchScalarGridSpec(
        num_scalar_prefetch=2, grid=(nblk,),
        in_specs=[
            smem(lambda j, be, nb: (live(j, nb), 0, 0)),
            smem(lambda j, be, nb: (live(j, nb) + 1, 0, 0)),
            smem(lambda j, be, nb: (jnp.minimum(live(j, nb) + 2, nb[0]), 0, 0)),
            anyspec,
            pl.BlockSpec((None, D_MODEL, EXPERT_DIM), wmap),
            pl.BlockSpec((None, D_MODEL, EXPERT_DIM), wmap),
            pl.BlockSpec((None, EXPERT_DIM, D_MODEL), wmap),
        ],
        out_specs=anyspec,
        scratch_shapes=[buf, buf, buf, buf, pltpu.SemaphoreType.DMA((2,)), pltpu.SemaphoreType.DMA((2,))],
    )
    return pl.pallas_call(
        functools.partial(_experts_body, n_tok=T, spare_row0=spare_row0),
        out_shape=jax.ShapeDtypeStruct((n_out_rows, PACKED), U32), grid_spec=grid_spec,
        compiler_params=_params(1), name="experts",
    )(block_e, n_used, rs_all, rs_all, rs_all, h2, wg, wu, wd)


def _combine_body(*refs):
    y_refs = refs[:TOP_K]
    base_ref, w_ref, gt2_ref, out_ref = refs[TOP_K:]
    w = w_ref[...]
    lo, hi = _unpack_rows(y_refs[0][...])
    r_lo, r_hi = w[:, 0:1] * lo, w[:, 0:1] * hi
    for k in range(1, TOP_K):
        lo, hi = _unpack_rows(y_refs[k][...])
        r_lo, r_hi = r_lo + w[:, k:k + 1] * lo, r_hi + w[:, k:k + 1] * hi
    out_ref[...] = base_ref[...] + gt2_ref[...] * jnp.concatenate([r_lo, r_hi], axis=1)


def _combine(y_slots, base, w_tk, gt2, S):
    T = base.shape[0]
    tt = SEQ_TILE
    nt = T // tt
    per_b = S // tt
    yspec = lambda k: pl.BlockSpec((tt, PACKED), lambda i: (k * nt + i, 0))
    return pl.pallas_call(
        _combine_body, out_shape=jax.ShapeDtypeStruct((T, D_MODEL), F32), grid=(nt,),
        in_specs=[yspec(k) for k in range(TOP_K)] + [
            pl.BlockSpec((tt, D_MODEL), lambda i: (i, 0)),
            pl.BlockSpec((tt, TOP_K), lambda i: (i, 0)),
            pl.BlockSpec((None, 1, D_MODEL), lambda i: (i // per_b, 0, 0)),
        ],
        out_specs=pl.BlockSpec((tt, D_MODEL), lambda i: (i, 0)),
        compiler_params=_params(1), name="combine",
    )(*([y_slots] * TOP_K), base, w_tk, gt2)


def _pad_heads(w, width):
    kdim = w.shape[0]
    w3 = w.reshape(kdim, MLA_HEADS, width)
    return jnp.pad(w3, ((0, 0), (0, 0), (0, LANES - width))).reshape(kdim, MLA_HEADS * LANES)


def _rot_partner(w_rope):
    half = QK_ROPE // 2
    return jnp.concatenate([-w_rope[..., half:], w_rope[..., :half]], axis=-1)


def _lane_vec(nope, rope):
    return jnp.concatenate([nope, rope, jnp.zeros((LANES - QK_DIM,), F32)]).reshape(1, LANES)


def _layer(x2, ada, positions, B, S, g_norm1, w_in, g_cq, w_uq, g_ckv, w_ukv, g_qn, g_kn, g_attn_out,
           conv_w, conv_b, w_mq, w_mk, b_igate, b_fgate, g_mlstm_out, w_out, g_norm2, w_router, router_bias,
           w_gate_exp, w_up_exp, w_down_exp, w_gate_sh, w_up_sh, w_down_sh):
    T = B * S
    sh1, sc1, gt1, sh2, sc2, gt2 = [a.reshape(B, 1, D_MODEL) for a in jnp.split(ada, 6, axis=-1)]
    row = lambda v: v.reshape(1, -1).astype(F32)

    inv_freq = ROPE_THETA ** (-jnp.arange(0, QK_ROPE, 2, dtype=F32) / QK_ROPE)
    ang = positions.astype(F32).reshape(T, 1) * inv_freq
    cos, sin = jnp.cos(ang), jnp.sin(ang)
    pad_r = jnp.zeros((T, LANES - QK_DIM), F32)
    cosp = jnp.concatenate([jnp.ones((T, QK_NOPE), F32), cos, cos, pad_r], axis=1)
    sinp = jnp.concatenate([jnp.zeros((T, QK_NOPE), F32), sin, sin, pad_r], axis=1)

    o0 = 0
    w_cq = w_in[:, o0:o0 + Q_LORA]; o0 += Q_LORA
    w_ckv = w_in[:, o0:o0 + KV_LORA]; o0 += KV_LORA
    w_kr = w_in[:, o0:o0 + QK_ROPE]; o0 += QK_ROPE
    w_z = w_in[:, o0:o0 + MLSTM_INNER]; o0 += MLSTM_INNER
    w_v = w_in[:, o0:o0 + MLSTM_INNER]; o0 += MLSTM_INNER
    w_o = w_in[:, o0:o0 + MLSTM_INNER]; o0 += MLSTM_INNER
    w_i = w_in[:, o0:o0 + MLSTM_HEADS]; o0 += MLSTM_HEADS
    w_f = w_in[:, o0:o0 + MLSTM_HEADS]
    zl = jnp.zeros((D_MODEL, QK_NOPE), F32)
    zr = jnp.zeros((D_MODEL, LANES - QK_DIM), F32)
    wa = jnp.concatenate([w_cq, w_ckv, w_z, w_o, zl, w_kr, zr, zl, _rot_partner(w_kr), zr], axis=1).astype(BF16)
    wt = jnp.concatenate([w_v.T, w_i.T, w_f.T, jnp.zeros((16 - 2 * MLSTM_HEADS, D_MODEL), F32)], axis=0).astype(BF16)

    uq3 = w_uq.reshape(Q_LORA, MLA_HEADS, QK_DIM)
    uq_rot = jnp.concatenate([jnp.zeros((Q_LORA, MLA_HEADS, QK_NOPE), F32), _rot_partner(uq3[..., QK_NOPE:])], axis=-1)
    wuq = _pad_heads(w_uq, QK_DIM).astype(BF16)
    wuqs = _pad_heads(uq_rot.reshape(Q_LORA, MLA_HEADS * QK_DIM), QK_DIM).astype(BF16)
    ukv3 = w_ukv.reshape(KV_LORA, MLA_HEADS, QK_NOPE + V_DIM)
    wuk = _pad_heads(ukv3[..., :QK_NOPE].reshape(KV_LORA, MLA_HEADS * QK_NOPE), QK_NOPE).astype(BF16)
    wuv = ukv3[..., QK_NOPE:].reshape(KV_LORA, MLA_HEADS * V_DIM).astype(BF16)
    gq = _lane_vec(g_qn[:QK_NOPE], g_qn[QK_NOPE:])
    gqs = _lane_vec(jnp.zeros((QK_NOPE,), F32), _rot_partner(g_qn[QK_NOPE:]) * jnp.concatenate(
        [-jnp.ones((QK_ROPE // 2,), F32), jnp.ones((QK_ROPE // 2,), F32)]))
    gk = _lane_vec(g_kn[:QK_NOPE], g_kn[QK_NOPE:])
    gks = _lane_vec(jnp.zeros((QK_NOPE,), F32), _rot_partner(g_kn[QK_NOPE:]) * jnp.concatenate(
        [-jnp.ones((QK_ROPE // 2,), F32), jnp.ones((QK_ROPE // 2,), F32)]))

    z, og, vt3, g3, q, k, mv = _inproj(
        x2, 1.0 + sc1, sh1, row(g_norm1), wa, wt, row(g_cq), wuq, wuqs, row(g_ckv), wuk, wuv,
        gq, gqs, gk, gks, cosp, sinp, S)
    attn_o = _attn(q, k, mv, B, S)
    mlstm_o = _mlstm(z, vt3, g3, og, conv_w.reshape(CONV_WIDTH, MLSTM_INNER), row(conv_b),
                     w_mq.astype(BF16), w_mk.astype(BF16), b_igate, b_fgate, row(g_mlstm_out), B, S)

    wgu = jnp.concatenate([w_gate_sh, w_up_sh], axis=1).astype(BF16)
    base, h2, lgt = _post(attn_o, mlstm_o, x2, gt1, row(g_attn_out), w_out.astype(BF16), row(g_norm2),
                          1.0 + sc2, sh2, gt2, w_router.T.astype(BF16), wgu, w_down_sh.astype(BF16), S)

    idx_t, w_t, rank_t, cnt = _route(lgt, router_bias.reshape(N_EXPERTS, 1))
    counts = cnt[:, 0].astype(I32)
    padded = (counts + ROW_BLOCK - 1) // ROW_BLOCK * ROW_BLOCK
    padded_end = jnp.cumsum(padded)
    off = padded_end - padded
    n_rows = T * TOP_K + N_EXPERTS * ROW_BLOCK
    nblk = n_rows // ROW_BLOCK
    block_start = jnp.arange(nblk, dtype=I32) * ROW_BLOCK
    block_e = jnp.minimum(jnp.sum((padded_end[None, :] <= block_start[:, None]).astype(I32), axis=1), N_EXPERTS - 1)
    n_used = jnp.maximum(padded_end[-1:] // ROW_BLOCK, 1).astype(I32)
    dest_t = _dest(idx_t, rank_t, off.reshape(N_EXPERTS, 1))

    n_slots = T * TOP_K
    spare = n_slots + jnp.arange(n_rows + ROW_BLOCK, dtype=I32)
    row_slot = _invert_slots(dest_t.reshape(n_slots), spare[:n_rows])
    rs_all = jnp.concatenate([spare[n_rows:], row_slot]).reshape(nblk + 1, 1, ROW_BLOCK)
    y = _experts(block_e, n_used, rs_all, h2, w_gate_exp, w_up_exp, w_down_exp,
                 n_slots + n_rows + 2 * ROW_BLOCK, n_slots + n_rows + ROW_BLOCK)
    return _combine(y, base, w_t.T, gt2, S)


def kernel(x, c, positions, w_ada, b_ada, g_norm1, w_in, g_cq, w_uq, g_ckv, w_ukv, g_qn, g_kn, g_attn_out, conv_w, conv_b, w_mq, w_mk, b_igate, b_fgate, g_mlstm_out, w_out, g_norm2, w_router, router_bias, w_gate_exp, w_up_exp, w_down_exp, w_gate_sh, w_up_sh, w_down_sh):
    B, S, _ = x.shape
    depth = w_ada.shape[0]
    x2 = x.reshape(B * S, D_MODEL)
    for l in range(depth):
        ada = _ada(c, w_ada[l], b_ada[l])
        x2 = _layer(x2, ada, positions, B, S, g_norm1[l], w_in[l], g_cq[l], w_uq[l], g_ckv[l], w_ukv[l], g_qn[l],
                    g_kn[l], g_attn_out[l], conv_w[l], conv_b[l], w_mq[l], w_mk[l], b_igate[l], b_fgate[l],
                    g_mlstm_out[l], w_out[l], g_norm2[l], w_router[l], router_bias[l], w_gate_exp[l], w_up_exp[l],
                    w_down_exp[l], w_gate_sh[l], w_up_sh[l], w_down_sh[l])
    return x2.reshape(B, S, D_MODEL)
```

```python
import functools
import math

import jax
import jax.numpy as jnp
from jax import lax
from jax.experimental import pallas as pl
from jax.experimental.pallas import tpu as pltpu
from jax.experimental.pallas import tpu_sc as plsc

F32 = jnp.float32
BF16 = jnp.bfloat16
I32 = jnp.int32

D_MODEL = 1024
MLA_HEADS = 8
QK_NOPE = 64
QK_ROPE = 32
QK_DIM = QK_NOPE + QK_ROPE
V_DIM = 64
Q_LORA = 384
KV_LORA = 256
ROPE_THETA = 10000.0
MLSTM_HEADS = 4
MLSTM_DIM = 128
MLSTM_INNER = MLSTM_HEADS * MLSTM_DIM
CONV_WIDTH = 4
N_EXPERTS = 256
TOP_K = 8
N_GROUPS = 8
TOPK_GROUPS = 4
GROUP_SIZE = N_EXPERTS // N_GROUPS
EXPERT_DIM = 256
SHARED_DIM = 256
ROUTED_SCALE = 2.5
RMS_EPS = 1e-6

LANES = 128
SEQ_TILE = 256
ROW_BLOCK = 256
VMEM_LIMIT = 56 * 1024 * 1024

_OFF_CQ = 0
_OFF_CKV = _OFF_CQ + Q_LORA
_OFF_Z = _OFF_CKV + KV_LORA
_OFF_O = _OFF_Z + MLSTM_INNER
_OFF_KR = _OFF_O + MLSTM_INNER
_OFF_KRS = _OFF_KR + LANES
_WA_COLS = _OFF_KRS + LANES
_WT_ROWS = MLSTM_INNER + 16


def _params(n_axes, vmem=VMEM_LIMIT):
    return pltpu.CompilerParams(dimension_semantics=("arbitrary",) * n_axes, vmem_limit_bytes=vmem)


def _dot(a, b):
    return jnp.dot(a, b, preferred_element_type=F32)


def _dot_nt(a, b):
    return lax.dot_general(a, b, (((1,), (1,)), ((), ())), preferred_element_type=F32)


def _sigmoid(x):
    return 1.0 / (1.0 + jnp.exp(-x))


def _silu(x):
    return x * _sigmoid(x)


U32 = jnp.uint32
PACKED = D_MODEL // 2


def _pack_rows(x):
    w = x.shape[1] // 2
    lo = lax.bitcast_convert_type(x[:, :w].astype(BF16).astype(F32), U32)
    hi = lax.bitcast_convert_type(x[:, w:].astype(BF16).astype(F32), U32)
    return lax.shift_right_logical(lo, U32(16)) | (hi & U32(0xFFFF0000))


def _unpack_rows(p):
    lo = lax.bitcast_convert_type(lax.shift_left(p, U32(16)), F32)
    hi = lax.bitcast_convert_type(p & U32(0xFFFF0000), F32)
    return lo, hi


def _ada_body(c_ref, w_ref, b_ref, o_ref):
    c = c_ref[...]
    cond = _silu(c).astype(BF16)
    o_ref[...] = _dot(cond, w_ref[...].astype(BF16)) + b_ref[...]


def _ada(c, w_ada, b_ada):
    B = c.shape[0]
    n = w_ada.shape[1]
    tn = 1024
    return pl.pallas_call(
        _ada_body,
        out_shape=jax.ShapeDtypeStruct((B, n), F32),
        grid=(n // tn,),
        in_specs=[
            pl.BlockSpec((B, D_MODEL), lambda j: (0, 0)),
            pl.BlockSpec((D_MODEL, tn), lambda j: (0, j)),
            pl.BlockSpec((1, tn), lambda j: (0, j)),
        ],
        out_specs=pl.BlockSpec((B, tn), lambda j: (0, j)),
        compiler_params=_params(1),
        name="ada",
    )(c, w_ada, b_ada.reshape(1, n))


def _inproj_body(x_ref, sc_ref, sh_ref, g1_ref, wa_ref, wt_ref, gcq_ref, wuq_ref, wuqs_ref, gckv_ref,
                 wuk_ref, wuv_ref, gq_ref, gqs_ref, gk_ref, gks_ref, cos_ref, sin_ref,
                 z_ref, o_ref, vt_ref, gt_ref, q_ref, k_ref, mv_ref):
    x = x_ref[...]
    hn = x * lax.rsqrt(jnp.mean(x * x, axis=-1, keepdims=True) + RMS_EPS) * g1_ref[...]
    hm = (hn * sc_ref[...] + sh_ref[...]).astype(BF16)
    p = _dot(hm, wa_ref[...])
    z_ref[...] = p[:, _OFF_Z:_OFF_Z + MLSTM_INNER].astype(BF16)
    o_ref[...] = p[:, _OFF_O:_OFF_O + MLSTM_INNER].astype(BF16)
    rt = _dot_nt(wt_ref[...], hm)
    vt_ref[...] = rt[:MLSTM_INNER].astype(BF16)
    gt_ref[...] = rt[MLSTM_INNER:]

    cq = p[:, _OFF_CQ:_OFF_CQ + Q_LORA]
    cqn = (cq * lax.rsqrt(jnp.mean(cq * cq, axis=-1, keepdims=True) + RMS_EPS) * gcq_ref[...]).astype(BF16)
    ckv = p[:, _OFF_CKV:_OFF_CKV + KV_LORA]
    ckvn = (ckv * lax.rsqrt(jnp.mean(ckv * ckv, axis=-1, keepdims=True) + RMS_EPS) * gckv_ref[...]).astype(BF16)
    qa = _dot(cqn, wuq_ref[...])
    qb = _dot(cqn, wuqs_ref[...])
    ka = _dot(ckvn, wuk_ref[...])
    mv_ref[...] = _dot(ckvn, wuv_ref[...]).astype(BF16)
    kr = p[:, _OFF_KR:_OFF_KR + LANES]
    krs = p[:, _OFF_KRS:_OFF_KRS + LANES]

    cosp = cos_ref[...]
    sinp = sin_ref[...]
    gqc = gq_ref[...] * cosp
    gqs = gqs_ref[...] * sinp
    gkc = gk_ref[...] * cosp
    k_rot = krs * (gks_ref[...] * sinp)
    kr_ss = jnp.sum(kr * kr, axis=-1, keepdims=True)
    q_scale = (QK_DIM ** -0.5) * math.log2(math.e)
    for h in range(MLA_HEADS):
        sl = slice(h * LANES, (h + 1) * LANES)
        qah = qa[:, sl]
        rq = lax.rsqrt(jnp.sum(qah * qah, axis=-1, keepdims=True) * (1.0 / QK_DIM) + RMS_EPS) * q_scale
        q_ref[:, sl] = ((qah * gqc + qb[:, sl] * gqs) * rq).astype(BF16)
        kah = ka[:, sl]
        rk = lax.rsqrt((jnp.sum(kah * kah, axis=-1, keepdims=True) + kr_ss) * (1.0 / QK_DIM) + RMS_EPS)
        k_ref[:, sl] = (((kah + kr) * gkc + k_rot) * rk).astype(BF16)


def _inproj(x2, sc1p, sh1, g1, wa, wt, gcq, wuq, wuqs, gckv, wuk, wuv, gq, gqs, gk, gks, cosp, sinp, S):
    T = x2.shape[0]
    tm = SEQ_TILE
    nt = T // tm
    per_b = S // tm
    hp = MLA_HEADS * LANES
    full = lambda a: pl.BlockSpec(a.shape, lambda i: (0,) * a.ndim)
    row = lambda w: pl.BlockSpec((tm, w), lambda i: (i, 0))
    bmod = pl.BlockSpec((None, 1, D_MODEL), lambda i: (i // per_b, 0, 0))
    out_shapes = (
        jax.ShapeDtypeStruct((T, MLSTM_INNER), BF16),
        jax.ShapeDtypeStruct((T, MLSTM_INNER), BF16),
        jax.ShapeDtypeStruct((nt, MLSTM_INNER, tm), BF16),
        jax.ShapeDtypeStruct((nt, 16, tm), F32),
        jax.ShapeDtypeStruct((T, hp), BF16),
        jax.ShapeDtypeStruct((T, hp), BF16),
        jax.ShapeDtypeStruct((T, MLA_HEADS * V_DIM), BF16),
    )
    out_specs = (
        row(MLSTM_INNER), row(MLSTM_INNER),
        pl.BlockSpec((None, MLSTM_INNER, tm), lambda i: (i, 0, 0)),
        pl.BlockSpec((None, 16, tm), lambda i: (i, 0, 0)),
        row(hp), row(hp), row(MLA_HEADS * V_DIM),
    )
    in_specs = [row(D_MODEL), bmod, bmod, full(g1), full(wa), full(wt), full(gcq), full(wuq), full(wuqs),
                full(gckv), full(wuk), full(wuv), full(gq), full(gqs), full(gk), full(gks),
                row(LANES), row(LANES)]
    return pl.pallas_call(
        _inproj_body, out_shape=out_shapes, grid=(nt,), in_specs=in_specs, out_specs=out_specs,
        compiler_params=_params(1), name="inproj",
    )(x2, sc1p, sh1, g1, wa, wt, gcq, wuq, wuqs, gckv, wuk, wuv, gq, gqs, gk, gks, cosp, sinp)


ATTN_HEADS_PER_STEP = 4


def _attn_body(q_ref, k_ref, v_ref, o_ref):
    i = pl.program_id(2)
    t = SEQ_TILE
    nh = ATTN_HEADS_PER_STEP
    qry_pos = lax.broadcasted_iota(I32, (t, t), 0)
    key_pos = lax.broadcasted_iota(I32, (t, t), 1)
    causal = key_pos <= qry_pos
    qs = [q_ref[:, hh * LANES:(hh + 1) * LANES] for hh in range(nh)]

    def scores(j, hh, diag):
        kj = k_ref[pl.ds(pl.multiple_of(j * t, t), t), hh * LANES:(hh + 1) * LANES]
        s = _dot_nt(qs[hh], kj)
        return jnp.where(causal, s, -jnp.inf) if diag else s

    def halves(x):
        return x[:, :LANES], x[:, LANES:]

    def max_pass(j, mx, diag):
        out = []
        for hh in range(nh):
            lo, hi = halves(scores(j, hh, diag))
            out.append(jnp.maximum(mx[hh], jnp.maximum(lo, hi)))
        return tuple(out)

    mx = tuple(jnp.full((t, LANES), -jnp.inf, F32) for _ in range(nh))
    mx = lax.fori_loop(0, i, lambda j, c: max_pass(j, c, False), mx)
    mx = max_pass(i, mx, True)
    mb = [jnp.broadcast_to(jnp.max(mx[hh], axis=1, keepdims=True), (t, LANES)) for hh in range(nh)]

    def acc_pass(j, carry, diag):
        row0 = pl.multiple_of(j * t, t)
        out = []
        for hh in range(nh):
            l128, acc = carry[hh]
            lo, hi = halves(scores(j, hh, diag))
            p_lo = jnp.exp2(lo - mb[hh])
            p_hi = jnp.exp2(hi - mb[hh])
            l128 = l128 + (p_lo + p_hi)
            p = jnp.concatenate([p_lo, p_hi], axis=1).astype(BF16)
            pair = hh // 2
            acc = acc + _dot(p, v_ref[pl.ds(row0, t), pair * LANES:(pair + 1) * LANES])
            out.append((l128, acc))
        return tuple(out)

    init = tuple((jnp.zeros((t, LANES), F32), jnp.zeros((t, LANES), F32)) for _ in range(nh))
    carry = lax.fori_loop(0, i, lambda j, c: acc_pass(j, c, False), init)
    final = acc_pass(i, carry, True)
    outs = [acc / jnp.sum(l128, axis=1, keepdims=True) for (l128, acc) in final]
    first_head = lax.broadcasted_iota(I32, (t, LANES), 1) < V_DIM
    o_ref[...] = jnp.concatenate(
        [jnp.where(first_head, outs[2 * pr], outs[2 * pr + 1]) for pr in range(nh // 2)], axis=1).astype(BF16)


def _attn(q, k, mvt, B, S):
    T = q.shape[0]
    t = SEQ_TILE
    nh = ATTN_HEADS_PER_STEP
    per_b = S // t
    return pl.pallas_call(
        _attn_body,
        out_shape=jax.ShapeDtypeStruct((T, MLA_HEADS * V_DIM), BF16),
        grid=(B, MLA_HEADS // nh, per_b),
        in_specs=[
            pl.BlockSpec((t, nh * LANES), lambda b, hp, i: (b * per_b + i, hp)),
            pl.BlockSpec((S, nh * LANES), lambda b, hp, i: (b, hp)),
            pl.BlockSpec((S, nh * V_DIM), lambda b, hp, i: (b, hp)),
        ],
        out_specs=pl.BlockSpec((t, nh * V_DIM), lambda b, hp, i: (b * per_b + i, hp)),
        compiler_params=_params(3), name="attn",
    )(q, k, mvt)


def _split3(a):
    hi = a.astype(BF16)
    r = a - hi.astype(F32)
    mid = r.astype(BF16)
    lo = (r - mid.astype(F32)).astype(BF16)
    return hi, mid, lo


def _mlstm_body(bi_ref, bf_ref, z_ref, vt_ref, g_ref, og_ref, cw_ref, cb_ref, wq_ref, wk_ref, gout_ref,
                out_ref, zpad, q_s, k_s, st_s, m_s, *, S):
    h = pl.program_id(1)
    L = SEQ_TILE
    nchunk = S // L
    hd = MLSTM_DIM
    zpad[0:8, :] = jnp.zeros((8, hd), F32)
    zpad[8:, :] = z_ref[...].astype(F32)
    zc = cb_ref[...] + jnp.zeros((S, hd), F32)
    for j in range(CONV_WIDTH):
        zc = zc + cw_ref[j:j + 1, :] * zpad[pl.ds(8 - (CONV_WIDTH - 1) + j, S), :]
    zs = _silu(zc).astype(BF16)
    q_s[...] = _dot(zs, wq_ref[...]).astype(BF16)
    k_s[...] = (_dot(zs, wk_ref[...]) * (hd ** -0.5)).astype(BF16)

    st_s[...] = jnp.zeros_like(st_s)
    m_s[...] = jnp.zeros_like(m_s)
    b_i = bi_ref[h]
    b_f = bf_ref[h]
    r_i = lax.broadcasted_iota(I32, (L, L), 0)
    c_i = lax.broadcasted_iota(I32, (L, L), 1)
    tril = r_i >= c_i
    strict_lower = jnp.where(r_i > c_i, 1.0, 0.0).astype(BF16)
    ones_row = jnp.where(lax.broadcasted_iota(I32, (hd, L), 0) == 0, 1.0, 0.0).astype(BF16)

    def chunk(c, _):
        start = pl.multiple_of(c * L, L)
        qc = q_s[pl.ds(start, L), :]
        kc = k_s[pl.ds(start, L), :]
        vt_aug = jnp.concatenate([vt_ref[c], ones_row], axis=0)
        li = g_ref[c, pl.ds(h, 1), :] + b_i
        fp = g_ref[c, pl.ds(MLSTM_HEADS + h, 1), :] + b_f
        lf = jnp.minimum(fp, 0.0) - jnp.log(1.0 + jnp.exp(-jnp.abs(fp)))
        a_mat = jnp.where(tril, lf, 0.0)
        a_hi, a_mid, a_lo = _split3(a_mat)
        d0 = _dot(a_hi, strict_lower) + _dot(a_mid, strict_lower) + _dot(a_lo, strict_lower)
        dmat = jnp.where(tril, d0 + li, -jnp.inf)
        b_col = jnp.sum(a_mat, axis=1, keepdims=True)
        m_prev = m_s[...]
        inter = b_col + m_prev
        mt = jnp.maximum(inter, jnp.max(dmat, axis=1, keepdims=True))
        w = jnp.exp(dmat - mt) * _dot_nt(qc, kc)
        decay = jnp.exp(inter - mt)
        st = st_s[...]
        num_aug = decay * _dot_nt(qc, st.astype(BF16)) + _dot_nt(w.astype(BF16), vt_aug)
        num = num_aug[:, :hd]
        den = num_aug[:, hd:hd + 1]
        hv = num / jnp.maximum(jnp.abs(den), jnp.exp(-mt))
        g = d0[L - 1:L, :] + li
        b_last = b_col[L - 1:L, :]
        m_new = jnp.maximum(b_last + m_prev, jnp.max(g, axis=1, keepdims=True))
        a = jnp.exp(g - m_new)
        cd = jnp.exp(b_last + m_prev - m_new)
        st_s[...] = cd * st + _dot((vt_aug.astype(F32) * a).astype(BF16), kc)
        m_s[...] = m_new
        hn = hv * lax.rsqrt(jnp.mean(hv * hv, axis=-1, keepdims=True) + RMS_EPS) * gout_ref[...]
        og = og_ref[pl.ds(start, L), :].astype(F32)
        out_ref[pl.ds(start, L), :] = (_sigmoid(og) * hn).astype(BF16)
        return 0

    lax.fori_loop(0, nchunk, chunk, 0)


def _mlstm(z, vt3, g3, og, conv_w, conv_b, wq, wk, b_i, b_f, g_out, B, S):
    T = z.shape[0]
    L = SEQ_TILE
    per_b = S // L
    hd = MLSTM_DIM
    smem = pl.BlockSpec(memory_space=pltpu.SMEM)
    seq = pl.BlockSpec((S, hd), lambda b, h: (b, h))
    return pl.pallas_call(
        functools.partial(_mlstm_body, S=S),
        out_shape=jax.ShapeDtypeStruct((T, MLSTM_INNER), BF16),
        grid=(B, MLSTM_HEADS),
        in_specs=[
            smem, smem, seq,
            pl.BlockSpec((per_b, hd, L), lambda b, h: (b, h, 0)),
            pl.BlockSpec((per_b, 16, L), lambda b, h: (b, 0, 0)),
            seq,
            pl.BlockSpec((CONV_WIDTH, hd), lambda b, h: (0, h)),
            pl.BlockSpec((1, hd), lambda b, h: (0, h)),
            pl.BlockSpec((None, hd, hd), lambda b, h: (h, 0, 0)),
            pl.BlockSpec((None, hd, hd), lambda b, h: (h, 0, 0)),
            pl.BlockSpec((1, hd), lambda b, h: (0, h)),
        ],
        out_specs=seq,
        scratch_shapes=[
            pltpu.VMEM((S + 8, hd), F32),
            pltpu.VMEM((S, hd), BF16),
            pltpu.VMEM((S, hd), BF16),
            pltpu.VMEM((2 * hd, hd), F32),
            pltpu.VMEM((1, 1), F32),
        ],
        compiler_params=_params(2), name="mlstm",
    )(b_i, b_f, z, vt3, g3, og, conv_w, conv_b, wq, wk, g_out)


def _post_body(ao_ref, mo_ref, x_ref, gt1_ref, gattn_ref, wout_ref, g2_ref, sc2_ref, sh2_ref, gt2_ref,
               wrt_ref, wgu_ref, wds_ref, base_ref, h2_ref, lg_ref):
    ao = ao_ref[...].astype(F32)
    aon = ao * lax.rsqrt(jnp.mean(ao * ao, axis=-1, keepdims=True) + RMS_EPS) * gattn_ref[...]
    mix_in = jnp.concatenate([aon.astype(BF16), mo_ref[...]], axis=1)
    x1 = x_ref[...] + gt1_ref[...] * _dot(mix_in, wout_ref[...])
    hn = x1 * lax.rsqrt(jnp.mean(x1 * x1, axis=-1, keepdims=True) + RMS_EPS) * g2_ref[...]
    h2 = hn * sc2_ref[...] + sh2_ref[...]
    words = _pack_rows(h2)
    h2_ref[0] = words[:, :HALF]
    h2_ref[1] = words[:, HALF:]
    h2b = h2.astype(BF16)
    gu = _dot(h2b, wgu_ref[...])
    act = (_silu(gu[:, :SHARED_DIM]) * gu[:, SHARED_DIM:]).astype(BF16)
    base_ref[...] = x1 + gt2_ref[...] * _dot(act, wds_ref[...])
    lg_ref[...] = _dot_nt(wrt_ref[...], h2b)


def _post(ao, mo, x2, gt1, gattn, wout, g2, sc2p, sh2, gt2, wrt, wgu, wds, S):
    T = x2.shape[0]
    tm = SEQ_TILE
    per_b = S // tm
    full = lambda a: pl.BlockSpec(a.shape, lambda i: (0,) * a.ndim)
    row = lambda w: pl.BlockSpec((tm, w), lambda i: (i, 0))
    bmod = pl.BlockSpec((None, 1, D_MODEL), lambda i: (i // per_b, 0, 0))
    return pl.pallas_call(
        _post_body,
        out_shape=(jax.ShapeDtypeStruct((T, D_MODEL), F32), jax.ShapeDtypeStruct((2, T, HALF), U32),
                   jax.ShapeDtypeStruct((N_EXPERTS, T), F32)),
        grid=(T // tm,),
        in_specs=[row(MLA_HEADS * V_DIM), row(MLSTM_INNER), row(D_MODEL), bmod, full(gattn), full(wout),
                  full(g2), bmod, bmod, bmod, full(wrt), full(wgu), full(wds)],
        out_specs=(row(D_MODEL), pl.BlockSpec((2, tm, HALF), lambda i: (0, i, 0)),
                   pl.BlockSpec((N_EXPERTS, tm), lambda i: (0, i))),
        compiler_params=_params(1), name="post",
    )(ao, mo, x2, gt1, gattn, wout, g2, sc2p, sh2, gt2, wrt, wgu, wds)


def _first_max(x, rows, n):
    mx = jnp.max(x, axis=0, keepdims=True)
    idx = jnp.min(jnp.where(x == mx, rows, n), axis=0, keepdims=True)
    return mx, idx


def _route_body(lg_ref, bias_ref, idx_ref, w_ref, rank_ref, cnt_ref, carry):
    i = pl.program_id(0)
    tt = lg_ref.shape[1]

    @pl.when(i == 0)
    def _():
        carry[...] = jnp.zeros_like(carry)

    sc = _sigmoid(lg_ref[...])
    bi = sc + bias_ref[...]
    rows = lax.broadcasted_iota(I32, (N_EXPERTS, tt), 0)
    rows_g = lax.broadcasted_iota(I32, (GROUP_SIZE, tt), 0)
    rows_n = lax.broadcasted_iota(I32, (N_GROUPS, tt), 0)
    neg = -jnp.inf
    gs = []
    for g in range(N_GROUPS):
        xg = bi[g * GROUP_SIZE:(g + 1) * GROUP_SIZE]
        m1, i1 = _first_max(xg, rows_g, GROUP_SIZE)
        m2 = jnp.max(jnp.where(rows_g == i1, neg, xg), axis=0, keepdims=True)
        gs.append(m1 + m2)
    cur = jnp.concatenate(gs, axis=0)
    keep = jnp.zeros((N_GROUPS, tt), F32)
    for _ in range(TOPK_GROUPS):
        _, gi = _first_max(cur, rows_n, N_GROUPS)
        sel = rows_n == gi
        keep = jnp.where(sel, 1.0, keep)
        cur = jnp.where(sel, neg, cur)
    masked = jnp.concatenate(
        [jnp.where(keep[g:g + 1] > 0.0, bi[g * GROUP_SIZE:(g + 1) * GROUP_SIZE], neg) for g in range(N_GROUPS)], axis=0)
    cur = masked
    onehot = jnp.zeros((N_EXPERTS, tt), F32)
    idxs, ws = [], []
    for _ in range(TOP_K):
        _, ei = _first_max(cur, rows, N_EXPERTS)
        sel = rows == ei
        ws.append(jnp.sum(jnp.where(sel, sc, 0.0), axis=0, keepdims=True))
        idxs.append(ei)
        cur = jnp.where(sel, neg, cur)
        onehot = jnp.where(sel, 1.0, onehot)
    wsum = ws[0]
    for k in range(1, TOP_K):
        wsum = wsum + ws[k]
    idx_ref[...] = jnp.concatenate(idxs, axis=0)
    w_ref[...] = jnp.concatenate([w / wsum * ROUTED_SCALE for w in ws], axis=0)
    r_i = lax.broadcasted_iota(I32, (tt, tt), 0)
    c_i = lax.broadcasted_iota(I32, (tt, tt), 1)
    before = jnp.where(r_i < c_i, 1.0, 0.0).astype(BF16)
    tot = _dot(onehot.astype(BF16), before) + carry[...]
    rank_ref[...] = jnp.concatenate(
        [jnp.sum(jnp.where(rows == ei, tot, 0.0), axis=0, keepdims=True) for ei in idxs], axis=0).astype(I32)
    carry[...] = carry[...] + jnp.sum(onehot, axis=1, keepdims=True)
    cnt_ref[...] = jnp.broadcast_to(carry[...], cnt_ref.shape)


def _route(lgt, bias_col):
    T = lgt.shape[1]
    tt = SEQ_TILE
    blk = pl.BlockSpec((TOP_K, tt), lambda i: (0, i))
    return pl.pallas_call(
        _route_body,
        out_shape=(jax.ShapeDtypeStruct((TOP_K, T), I32), jax.ShapeDtypeStruct((TOP_K, T), F32),
                   jax.ShapeDtypeStruct((TOP_K, T), I32), jax.ShapeDtypeStruct((N_EXPERTS, LANES), F32)),
        grid=(T // tt,),
        in_specs=[pl.BlockSpec((N_EXPERTS, tt), lambda i: (0, i)), pl.BlockSpec((N_EXPERTS, 1), lambda i: (0, 0))],
        out_specs=(blk, blk, blk, pl.BlockSpec((N_EXPERTS, LANES), lambda i: (0, 0))),
        scratch_shapes=[pltpu.VMEM((N_EXPERTS, 1), F32)],
        compiler_params=_params(1), name="route",
    )(lgt, bias_col)


def _dest_body(idx_ref, rank_ref, off_ref, dest_ref):
    tt = idx_ref.shape[1]
    rows = lax.broadcasted_iota(I32, (N_EXPERTS, tt), 0)
    off = off_ref[...]
    outs = []
    for k in range(TOP_K):
        sel = rows == idx_ref[k:k + 1, :]
        outs.append(jnp.sum(jnp.where(sel, off, 0), axis=0, keepdims=True) + rank_ref[k:k + 1, :])
    dest_ref[...] = jnp.concatenate(outs, axis=0)


def _dest(idx_t, rank_t, off_col):
    T = idx_t.shape[1]
    tt = SEQ_TILE
    blk = pl.BlockSpec((TOP_K, tt), lambda i: (0, i))
    return pl.pallas_call(
        _dest_body, out_shape=jax.ShapeDtypeStruct((TOP_K, T), I32), grid=(T // tt,),
        in_specs=[blk, blk, pl.BlockSpec((N_EXPERTS, 1), lambda i: (0, 0))], out_specs=blk,
        compiler_params=_params(1), name="dest",
    )(idx_t, rank_t, off_col)


SC_WINDOW = 128
HALF = PACKED // 2


def _sc_mesh():
    return plsc.VectorSubcoreMesh(core_axis_name="c", subcore_axis_name="s")


def _scatter_rows(dest2_t, rows, n_out):
    n = rows.shape[0]

    @pl.kernel(out_type=jax.ShapeDtypeStruct((n_out, HALF), U32), mesh=_sc_mesh(), scratch_types=[])
    def scatter(d_hbm, x_hbm, o_hbm):
        def body(d_vmem, x_vmem):
            for k in range(TOP_K):
                pltpu.sync_copy(x_vmem, o_hbm.at[d_vmem.at[k]])

        pltpu.emit_pipeline(
            body, grid=(n // SC_WINDOW,),
            in_specs=[pl.BlockSpec((TOP_K, SC_WINDOW), lambda i: (0, i)),
                      pl.BlockSpec((SC_WINDOW, HALF), lambda i: (i, 0))],
            out_specs=[], core_axis_name=("c", "s"), dimension_semantics=(pltpu.PARALLEL,),
        )(d_hbm, x_hbm)

    return scatter(dest2_t, rows)


def _gather_rows(index_row, rows):
    n = index_row.shape[1]

    @pl.kernel(out_type=jax.ShapeDtypeStruct((n, HALF), U32), mesh=_sc_mesh(), scratch_types=[])
    def gather(i_hbm, y_hbm, o_hbm):
        def body(i_vmem, o_vmem):
            pltpu.sync_copy(y_hbm.at[i_vmem.at[0]], o_vmem)

        pltpu.emit_pipeline(
            body, grid=(n // SC_WINDOW,),
            in_specs=[pl.BlockSpec((1, SC_WINDOW), lambda i: (0, i))],
            out_specs=[pl.BlockSpec((SC_WINDOW, HALF), lambda i: (i, 0))],
            core_axis_name=("c", "s"), dimension_semantics=(pltpu.PARALLEL,),
        )(i_hbm, o_hbm)

    return gather(index_row, rows)


def _experts_body(be_ref, nb_ref, nv_ref, x_ref, wg_ref, wu_ref, wd_ref, y_ref):
    del be_ref
    j = pl.program_id(0)

    @pl.when(j < nb_ref[0])
    def _():
        words = jnp.concatenate([x_ref[0], x_ref[1]], axis=1)
        live = lax.broadcasted_iota(I32, words.shape, 0) < nv_ref[j]
        words = jnp.where(live, words, U32(0))
        x = jnp.concatenate(_unpack_rows(words), axis=1).astype(BF16)
        g = _dot(x, wg_ref[...].astype(BF16))
        u = _dot(x, wu_ref[...].astype(BF16))
        act = (_silu(g) * u).astype(BF16)
        y = _pack_rows(_dot(act, wd_ref[...].astype(BF16)))
        y_ref[0] = y[:, :HALF]
        y_ref[1] = y[:, HALF:]


def _experts(block_e, n_used, n_valid, xs, wg, wu, wd):
    n_rows = xs.shape[1]
    nblk = n_rows // ROW_BLOCK
    live = lambda j, nb: jnp.minimum(j, nb[0] - 1)
    xmap = lambda j, be, nb, nv: (0, live(j, nb), 0)
    wmap = lambda j, be, nb, nv: (be[live(j, nb)], 0, 0)
    grid_spec = pltpu.PrefetchScalarGridSpec(
        num_scalar_prefetch=3, grid=(nblk,),
        in_specs=[
            pl.BlockSpec((2, ROW_BLOCK, HALF), xmap),
            pl.BlockSpec((None, D_MODEL, EXPERT_DIM), wmap),
            pl.BlockSpec((None, D_MODEL, EXPERT_DIM), wmap),
            pl.BlockSpec((None, EXPERT_DIM, D_MODEL), wmap),
        ],
        out_specs=pl.BlockSpec((2, ROW_BLOCK, HALF), xmap),
    )
    return pl.pallas_call(
        _experts_body, out_shape=jax.ShapeDtypeStruct((2, n_rows, HALF), U32), grid_spec=grid_spec,
        compiler_params=_params(1), name="experts",
    )(block_e, n_used, n_valid, xs, wg, wu, wd)


def _combine_body(*refs):
    ya_refs = refs[:TOP_K]
    yb_refs = refs[TOP_K:2 * TOP_K]
    base_ref, w_ref, gt2_ref, out_ref = refs[2 * TOP_K:]
    w = w_ref[...]
    r_lo = r_hi = None
    for k in range(TOP_K):
        lo, hi = _unpack_rows(jnp.concatenate([ya_refs[k][...], yb_refs[k][...]], axis=1))
        wk = w[:, k:k + 1]
        r_lo = wk * lo if r_lo is None else r_lo + wk * lo
        r_hi = wk * hi if r_hi is None else r_hi + wk * hi
    out_ref[...] = base_ref[...] + gt2_ref[...] * jnp.concatenate([r_lo, r_hi], axis=1)


def _combine(y_slots, base, w_tk, gt2, S):
    T = base.shape[0]
    tt = SEQ_TILE
    nt = T // tt
    per_b = S // tt
    yspec = lambda half, k: pl.BlockSpec((tt, HALF), lambda i: ((half * TOP_K + k) * nt + i, 0))
    return pl.pallas_call(
        _combine_body, out_shape=jax.ShapeDtypeStruct((T, D_MODEL), F32), grid=(nt,),
        in_specs=[yspec(half, k) for half in range(2) for k in range(TOP_K)] + [
            pl.BlockSpec((tt, D_MODEL), lambda i: (i, 0)),
            pl.BlockSpec((tt, TOP_K), lambda i: (i, 0)),
            pl.BlockSpec((None, 1, D_MODEL), lambda i: (i // per_b, 0, 0)),
        ],
        out_specs=pl.BlockSpec((tt, D_MODEL), lambda i: (i, 0)),
        compiler_params=_params(1), name="combine",
    )(*([y_slots] * (2 * TOP_K)), base, w_tk, gt2)


def _pad_heads(w, width):
    kdim = w.shape[0]
    w3 = w.reshape(kdim, MLA_HEADS, width)
    return jnp.pad(w3, ((0, 0), (0, 0), (0, LANES - width))).reshape(kdim, MLA_HEADS * LANES)


def _rot_partner(w_rope):
    half = QK_ROPE // 2
    return jnp.concatenate([-w_rope[..., half:], w_rope[..., :half]], axis=-1)


def _lane_vec(nope, rope):
    return jnp.concatenate([nope, rope, jnp.zeros((LANES - QK_DIM,), F32)]).reshape(1, LANES)


def _layer(x2, ada, positions, B, S, g_norm1, w_in, g_cq, w_uq, g_ckv, w_ukv, g_qn, g_kn, g_attn_out,
           conv_w, conv_b, w_mq, w_mk, b_igate, b_fgate, g_mlstm_out, w_out, g_norm2, w_router, router_bias,
           w_gate_exp, w_up_exp, w_down_exp, w_gate_sh, w_up_sh, w_down_sh):
    T = B * S
    sh1, sc1, gt1, sh2, sc2, gt2 = [a.reshape(B, 1, D_MODEL) for a in jnp.split(ada, 6, axis=-1)]
    row = lambda v: v.reshape(1, -1).astype(F32)

    inv_freq = ROPE_THETA ** (-jnp.arange(0, QK_ROPE, 2, dtype=F32) / QK_ROPE)
    ang = positions.astype(F32).reshape(T, 1) * inv_freq
    cos, sin = jnp.cos(ang), jnp.sin(ang)
    pad_r = jnp.zeros((T, LANES - QK_DIM), F32)
    cosp = jnp.concatenate([jnp.ones((T, QK_NOPE), F32), cos, cos, pad_r], axis=1)
    sinp = jnp.concatenate([jnp.zeros((T, QK_NOPE), F32), sin, sin, pad_r], axis=1)

    o0 = 0
    w_cq = w_in[:, o0:o0 + Q_LORA]; o0 += Q_LORA
    w_ckv = w_in[:, o0:o0 + KV_LORA]; o0 += KV_LORA
    w_kr = w_in[:, o0:o0 + QK_ROPE]; o0 += QK_ROPE
    w_z = w_in[:, o0:o0 + MLSTM_INNER]; o0 += MLSTM_INNER
    w_v = w_in[:, o0:o0 + MLSTM_INNER]; o0 += MLSTM_INNER
    w_o = w_in[:, o0:o0 + MLSTM_INNER]; o0 += MLSTM_INNER
    w_i = w_in[:, o0:o0 + MLSTM_HEADS]; o0 += MLSTM_HEADS
    w_f = w_in[:, o0:o0 + MLSTM_HEADS]
    zl = jnp.zeros((D_MODEL, QK_NOPE), F32)
    zr = jnp.zeros((D_MODEL, LANES - QK_DIM), F32)
    wa = jnp.concatenate([w_cq, w_ckv, w_z, w_o, zl, w_kr, zr, zl, _rot_partner(w_kr), zr], axis=1).astype(BF16)
    wt = jnp.concatenate([w_v.T, w_i.T, w_f.T, jnp.zeros((16 - 2 * MLSTM_HEADS, D_MODEL), F32)], axis=0).astype(BF16)

    uq3 = w_uq.reshape(Q_LORA, MLA_HEADS, QK_DIM)
    uq_rot = jnp.concatenate([jnp.zeros((Q_LORA, MLA_HEADS, QK_NOPE), F32), _rot_partner(uq3[..., QK_NOPE:])], axis=-1)
    wuq = _pad_heads(w_uq, QK_DIM).astype(BF16)
    wuqs = _pad_heads(uq_rot.reshape(Q_LORA, MLA_HEADS * QK_DIM), QK_DIM).astype(BF16)
    ukv3 = w_ukv.reshape(KV_LORA, MLA_HEADS, QK_NOPE + V_DIM)
    wuk = _pad_heads(ukv3[..., :QK_NOPE].reshape(KV_LORA, MLA_HEADS * QK_NOPE), QK_NOPE).astype(BF16)
    wuv = ukv3[..., QK_NOPE:].reshape(KV_LORA, MLA_HEADS * V_DIM).astype(BF16)
    gq = _lane_vec(g_qn[:QK_NOPE], g_qn[QK_NOPE:])
    gqs = _lane_vec(jnp.zeros((QK_NOPE,), F32), _rot_partner(g_qn[QK_NOPE:]) * jnp.concatenate(
        [-jnp.ones((QK_ROPE // 2,), F32), jnp.ones((QK_ROPE // 2,), F32)]))
    gk = _lane_vec(g_kn[:QK_NOPE], g_kn[QK_NOPE:])
    gks = _lane_vec(jnp.zeros((QK_NOPE,), F32), _rot_partner(g_kn[QK_NOPE:]) * jnp.concatenate(
        [-jnp.ones((QK_ROPE // 2,), F32), jnp.ones((QK_ROPE // 2,), F32)]))

    z, og, vt3, g3, q, k, mv = _inproj(
        x2, 1.0 + sc1, sh1, row(g_norm1), wa, wt, row(g_cq), wuq, wuqs, row(g_ckv), wuk, wuv,
        gq, gqs, gk, gks, cosp, sinp, S)
    attn_o = _attn(q, k, mv, B, S)
    mlstm_o = _mlstm(z, vt3, g3, og, conv_w.reshape(CONV_WIDTH, MLSTM_INNER), row(conv_b),
                     w_mq.astype(BF16), w_mk.astype(BF16), b_igate, b_fgate, row(g_mlstm_out), B, S)

    wgu = jnp.concatenate([w_gate_sh, w_up_sh], axis=1).astype(BF16)
    base, h2, lgt = _post(attn_o, mlstm_o, x2, gt1, row(g_attn_out), w_out.astype(BF16), row(g_norm2),
                          1.0 + sc2, sh2, gt2, w_router.T.astype(BF16), wgu, w_down_sh.astype(BF16), S)

    idx_t, w_t, rank_t, cnt = _route(lgt, router_bias.reshape(N_EXPERTS, 1))
    counts = cnt[:, 0].astype(I32)
    padded = (counts + ROW_BLOCK - 1) // ROW_BLOCK * ROW_BLOCK
    padded_end = jnp.cumsum(padded)
    off = padded_end - padded
    n_rows = T * TOP_K + N_EXPERTS * ROW_BLOCK
    nblk = n_rows // ROW_BLOCK
    block_start = jnp.arange(nblk, dtype=I32) * ROW_BLOCK
    block_e = jnp.minimum(jnp.sum((padded_end[None, :] <= block_start[:, None]).astype(I32), axis=1), N_EXPERTS - 1)
    n_used = jnp.maximum(padded_end[-1:] // ROW_BLOCK, 1).astype(I32)
    dest_t = _dest(idx_t, rank_t, off.reshape(N_EXPERTS, 1))

    first_block = off // ROW_BLOCK
    n_valid = jnp.clip(counts[block_e] - (jnp.arange(nblk, dtype=I32) - first_block[block_e]) * ROW_BLOCK,
                       0, ROW_BLOCK).astype(I32)
    dest2_t = jnp.concatenate([dest_t, dest_t + n_rows], axis=1)
    xs = _scatter_rows(dest2_t, h2.reshape(2 * T, HALF), 2 * n_rows).reshape(2, n_rows, HALF)
    y = _experts(block_e, n_used, n_valid, xs, w_gate_exp, w_up_exp, w_down_exp)
    dest_row = dest_t.reshape(1, TOP_K * T)
    y_slots = _gather_rows(jnp.concatenate([dest_row, dest_row + n_rows], axis=1), y.reshape(2 * n_rows, HALF))
    return _combine(y_slots, base, w_t.T, gt2, S)


def kernel(x, c, positions, w_ada, b_ada, g_norm1, w_in, g_cq, w_uq, g_ckv, w_ukv, g_qn, g_kn, g_attn_out, conv_w, conv_b, w_mq, w_mk, b_igate, b_fgate, g_mlstm_out, w_out, g_norm2, w_router, router_bias, w_gate_exp, w_up_exp, w_down_exp, w_gate_sh, w_up_sh, w_down_sh):
    B, S, _ = x.shape
    depth = w_ada.shape[0]
    x2 = x.reshape(B * S, D_MODEL)
    for l in range(depth):
        ada = _ada(c, w_ada[l], b_ada[l])
        x2 = _layer(x2, ada, positions, B, S, g_norm1[l], w_in[l], g_cq[l], w_uq[l], g_ckv[l], w_ukv[l], g_qn[l],
                    g_kn[l], g_attn_out[l], conv_w[l], conv_b[l], w_mq[l], w_mk[l], b_igate[l], b_fgate[l],
                    g_mlstm_out[l], w_out[l], g_norm2[l], w_router[l], router_bias[l], w_gate_exp[l], w_up_exp[l],
                    w_down_exp[l], w_gate_sh[l], w_up_sh[l], w_down_sh[l])
    return x2.reshape(B, S, D_MODEL)
```

```python
import functools
import math

import jax
import jax.numpy as jnp
from jax import lax
from jax.experimental import pallas as pl
from jax.experimental.pallas import tpu as pltpu
from jax.experimental.pallas import tpu_sc as plsc

F32 = jnp.float32
BF16 = jnp.bfloat16
I32 = jnp.int32

D_MODEL = 1024
MLA_HEADS = 8
QK_NOPE = 64
QK_ROPE = 32
QK_DIM = QK_NOPE + QK_ROPE
V_DIM = 64
Q_LORA = 384
KV_LORA = 256
ROPE_THETA = 10000.0
MLSTM_HEADS = 4
MLSTM_DIM = 128
MLSTM_INNER = MLSTM_HEADS * MLSTM_DIM
CONV_WIDTH = 4
N_EXPERTS = 256
TOP_K = 8
N_GROUPS = 8
TOPK_GROUPS = 4
GROUP_SIZE = N_EXPERTS // N_GROUPS
EXPERT_DIM = 256
SHARED_DIM = 256
ROUTED_SCALE = 2.5
RMS_EPS = 1e-6

LANES = 128
SEQ_TILE = 256
ROW_BLOCK = 256
VMEM_LIMIT = 56 * 1024 * 1024

_OFF_CQ = 0
_OFF_CKV = _OFF_CQ + Q_LORA
_OFF_Z = _OFF_CKV + KV_LORA
_OFF_O = _OFF_Z + MLSTM_INNER
_OFF_KR = _OFF_O + MLSTM_INNER
_OFF_KRS = _OFF_KR + LANES
_WA_COLS = _OFF_KRS + LANES
_WT_ROWS = MLSTM_INNER + 16


def _params(n_axes, vmem=VMEM_LIMIT):
    return pltpu.CompilerParams(dimension_semantics=("arbitrary",) * n_axes, vmem_limit_bytes=vmem)


def _dot(a, b):
    return jnp.dot(a, b, preferred_element_type=F32)


def _dot_nt(a, b):
    return lax.dot_general(a, b, (((1,), (1,)), ((), ())), preferred_element_type=F32)


def _sigmoid(x):
    return 1.0 / (1.0 + jnp.exp(-x))


def _silu(x):
    return x * _sigmoid(x)


U32 = jnp.uint32
PACKED = D_MODEL // 2


def _pack_rows(x):
    w = x.shape[1] // 2
    lo = lax.bitcast_convert_type(x[:, :w].astype(BF16).astype(F32), U32)
    hi = lax.bitcast_convert_type(x[:, w:].astype(BF16).astype(F32), U32)
    return lax.shift_right_logical(lo, U32(16)) | (hi & U32(0xFFFF0000))


def _unpack_rows(p):
    lo = lax.bitcast_convert_type(lax.shift_left(p, U32(16)), F32)
    hi = lax.bitcast_convert_type(p & U32(0xFFFF0000), F32)
    return lo, hi


def _ada_body(c_ref, w_ref, b_ref, o_ref):
    c = c_ref[...]
    cond = _silu(c).astype(BF16)
    o_ref[...] = _dot(cond, w_ref[...].astype(BF16)) + b_ref[...]


def _ada(c, w_ada, b_ada):
    B = c.shape[0]
    n = w_ada.shape[1]
    tn = 1024
    return pl.pallas_call(
        _ada_body,
        out_shape=jax.ShapeDtypeStruct((B, n), F32),
        grid=(n // tn,),
        in_specs=[
            pl.BlockSpec((B, D_MODEL), lambda j: (0, 0)),
            pl.BlockSpec((D_MODEL, tn), lambda j: (0, j)),
            pl.BlockSpec((1, tn), lambda j: (0, j)),
        ],
        out_specs=pl.BlockSpec((B, tn), lambda j: (0, j)),
        compiler_params=_params(1),
        name="ada",
    )(c, w_ada, b_ada.reshape(1, n))


def _inproj_body(x_ref, sc_ref, sh_ref, g1_ref, wa_ref, wt_ref, gcq_ref, wuq_ref, wuqs_ref, gckv_ref,
                 wuk_ref, wuv_ref, gq_ref, gqs_ref, gk_ref, gks_ref, cos_ref, sin_ref,
                 z_ref, o_ref, vt_ref, gt_ref, q_ref, k_ref, mv_ref):
    x = x_ref[...]
    hn = x * lax.rsqrt(jnp.mean(x * x, axis=-1, keepdims=True) + RMS_EPS) * g1_ref[...]
    hm = (hn * sc_ref[...] + sh_ref[...]).astype(BF16)
    p = _dot(hm, wa_ref[...])
    z_ref[...] = p[:, _OFF_Z:_OFF_Z + MLSTM_INNER].astype(BF16)
    o_ref[...] = p[:, _OFF_O:_OFF_O + MLSTM_INNER].astype(BF16)
    rt = _dot_nt(wt_ref[...], hm)
    vt_ref[...] = rt[:MLSTM_INNER].astype(BF16)
    gt_ref[...] = rt[MLSTM_INNER:]

    cq = p[:, _OFF_CQ:_OFF_CQ + Q_LORA]
    cqn = (cq * lax.rsqrt(jnp.mean(cq * cq, axis=-1, keepdims=True) + RMS_EPS) * gcq_ref[...]).astype(BF16)
    ckv = p[:, _OFF_CKV:_OFF_CKV + KV_LORA]
    ckvn = (ckv * lax.rsqrt(jnp.mean(ckv * ckv, axis=-1, keepdims=True) + RMS_EPS) * gckv_ref[...]).astype(BF16)
    qa = _dot(cqn, wuq_ref[...])
    qb = _dot(cqn, wuqs_ref[...])
    ka = _dot(ckvn, wuk_ref[...])
    mv_ref[...] = _dot(ckvn, wuv_ref[...]).astype(BF16)
    kr = p[:, _OFF_KR:_OFF_KR + LANES]
    krs = p[:, _OFF_KRS:_OFF_KRS + LANES]

    cosp = cos_ref[...]
    sinp = sin_ref[...]
    gqc = gq_ref[...] * cosp
    gqs = gqs_ref[...] * sinp
    gkc = gk_ref[...] * cosp
    k_rot = krs * (gks_ref[...] * sinp)
    kr_ss = jnp.sum(kr * kr, axis=-1, keepdims=True)
    q_scale = (QK_DIM ** -0.5) * math.log2(math.e)
    for h in range(MLA_HEADS):
        sl = slice(h * LANES, (h + 1) * LANES)
        qah = qa[:, sl]
        rq = lax.rsqrt(jnp.sum(qah * qah, axis=-1, keepdims=True) * (1.0 / QK_DIM) + RMS_EPS) * q_scale
        q_ref[:, sl] = ((qah * gqc + qb[:, sl] * gqs) * rq).astype(BF16)
        kah = ka[:, sl]
        rk = lax.rsqrt((jnp.sum(kah * kah, axis=-1, keepdims=True) + kr_ss) * (1.0 / QK_DIM) + RMS_EPS)
        k_ref[:, sl] = (((kah + kr) * gkc + k_rot) * rk).astype(BF16)


def _inproj(x2, sc1p, sh1, g1, wa, wt, gcq, wuq, wuqs, gckv, wuk, wuv, gq, gqs, gk, gks, cosp, sinp, S):
    T = x2.shape[0]
    tm = SEQ_TILE
    nt = T // tm
    per_b = S // tm
    hp = MLA_HEADS * LANES
    full = lambda a: pl.BlockSpec(a.shape, lambda i: (0,) * a.ndim)
    row = lambda w: pl.BlockSpec((tm, w), lambda i: (i, 0))
    bmod = pl.BlockSpec((None, 1, D_MODEL), lambda i: (i // per_b, 0, 0))
    out_shapes = (
        jax.ShapeDtypeStruct((T, MLSTM_INNER), BF16),
        jax.ShapeDtypeStruct((T, MLSTM_INNER), BF16),
        jax.ShapeDtypeStruct((nt, MLSTM_INNER, tm), BF16),
        jax.ShapeDtypeStruct((nt, 16, tm), F32),
        jax.ShapeDtypeStruct((T, hp), BF16),
        jax.ShapeDtypeStruct((T, hp), BF16),
        jax.ShapeDtypeStruct((T, MLA_HEADS * V_DIM), BF16),
    )
    out_specs = (
        row(MLSTM_INNER), row(MLSTM_INNER),
        pl.BlockSpec((None, MLSTM_INNER, tm), lambda i: (i, 0, 0)),
        pl.BlockSpec((None, 16, tm), lambda i: (i, 0, 0)),
        row(hp), row(hp), row(MLA_HEADS * V_DIM),
    )
    in_specs = [row(D_MODEL), bmod, bmod, full(g1), full(wa), full(wt), full(gcq), full(wuq), full(wuqs),
                full(gckv), full(wuk), full(wuv), full(gq), full(gqs), full(gk), full(gks),
                row(LANES), row(LANES)]
    return pl.pallas_call(
        _inproj_body, out_shape=out_shapes, grid=(nt,), in_specs=in_specs, out_specs=out_specs,
        compiler_params=_params(1), name="inproj",
    )(x2, sc1p, sh1, g1, wa, wt, gcq, wuq, wuqs, gckv, wuk, wuv, gq, gqs, gk, gks, cosp, sinp)


ATTN_HEADS_PER_STEP = 4


def _attn_body(q_ref, k_ref, v_ref, o_ref):
    i = pl.program_id(2)
    t = SEQ_TILE
    nh = ATTN_HEADS_PER_STEP
    qry_pos = lax.broadcasted_iota(I32, (t, t), 0)
    key_pos = lax.broadcasted_iota(I32, (t, t), 1)
    causal = key_pos <= qry_pos
    qs = [q_ref[:, hh * LANES:(hh + 1) * LANES] for hh in range(nh)]

    def scores(j, hh, diag):
        kj = k_ref[pl.ds(pl.multiple_of(j * t, t), t), hh * LANES:(hh + 1) * LANES]
        s = _dot_nt(qs[hh], kj)
        return jnp.where(causal, s, -jnp.inf) if diag else s

    def halves(x):
        return x[:, :LANES], x[:, LANES:]

    def max_pass(j, mx, diag):
        out = []
        for hh in range(nh):
            lo, hi = halves(scores(j, hh, diag))
            out.append(jnp.maximum(mx[hh], jnp.maximum(lo, hi)))
        return tuple(out)

    mx = tuple(jnp.full((t, LANES), -jnp.inf, F32) for _ in range(nh))
    mx = lax.fori_loop(0, i, lambda j, c: max_pass(j, c, False), mx)
    mx = max_pass(i, mx, True)
    mb = [jnp.broadcast_to(jnp.max(mx[hh], axis=1, keepdims=True), (t, LANES)) for hh in range(nh)]

    def acc_pass(j, carry, diag):
        row0 = pl.multiple_of(j * t, t)
        out = []
        for hh in range(nh):
            l128, acc = carry[hh]
            lo, hi = halves(scores(j, hh, diag))
            p_lo = jnp.exp2(lo - mb[hh])
            p_hi = jnp.exp2(hi - mb[hh])
            l128 = l128 + (p_lo + p_hi)
            p = jnp.concatenate([p_lo, p_hi], axis=1).astype(BF16)
            pair = hh // 2
            acc = acc + _dot(p, v_ref[pl.ds(row0, t), pair * LANES:(pair + 1) * LANES])
            out.append((l128, acc))
        return tuple(out)

    init = tuple((jnp.zeros((t, LANES), F32), jnp.zeros((t, LANES), F32)) for _ in range(nh))
    carry = lax.fori_loop(0, i, lambda j, c: acc_pass(j, c, False), init)
    final = acc_pass(i, carry, True)
    outs = [acc / jnp.sum(l128, axis=1, keepdims=True) for (l128, acc) in final]
    first_head = lax.broadcasted_iota(I32, (t, LANES), 1) < V_DIM
    o_ref[...] = jnp.concatenate(
        [jnp.where(first_head, outs[2 * pr], outs[2 * pr + 1]) for pr in range(nh // 2)], axis=1).astype(BF16)


def _attn(q, k, mvt, B, S):
    T = q.shape[0]
    t = SEQ_TILE
    nh = ATTN_HEADS_PER_STEP
    per_b = S // t
    return pl.pallas_call(
        _attn_body,
        out_shape=jax.ShapeDtypeStruct((T, MLA_HEADS * V_DIM), BF16),
        grid=(B, MLA_HEADS // nh, per_b),
        in_specs=[
            pl.BlockSpec((t, nh * LANES), lambda b, hp, i: (b * per_b + i, hp)),
            pl.BlockSpec((S, nh * LANES), lambda b, hp, i: (b, hp)),
            pl.BlockSpec((S, nh * V_DIM), lambda b, hp, i: (b, hp)),
        ],
        out_specs=pl.BlockSpec((t, nh * V_DIM), lambda b, hp, i: (b * per_b + i, hp)),
        compiler_params=_params(3), name="attn",
    )(q, k, mvt)


def _split3(a):
    hi = a.astype(BF16)
    r = a - hi.astype(F32)
    mid = r.astype(BF16)
    lo = (r - mid.astype(F32)).astype(BF16)
    return hi, mid, lo


def _mlstm_body(bi_ref, bf_ref, z_ref, vt_ref, g_ref, og_ref, cw_ref, cb_ref, wq_ref, wk_ref, gout_ref,
                out_ref, zpad, q_s, k_s, st_s, m_s, *, S):
    h = pl.program_id(1)
    L = SEQ_TILE
    nchunk = S // L
    hd = MLSTM_DIM
    zpad[0:8, :] = jnp.zeros((8, hd), F32)
    zpad[8:, :] = z_ref[...].astype(F32)
    zc = cb_ref[...] + jnp.zeros((S, hd), F32)
    for j in range(CONV_WIDTH):
        zc = zc + cw_ref[j:j + 1, :] * zpad[pl.ds(8 - (CONV_WIDTH - 1) + j, S), :]
    zs = _silu(zc).astype(BF16)
    q_s[...] = _dot(zs, wq_ref[...]).astype(BF16)
    k_s[...] = (_dot(zs, wk_ref[...]) * (hd ** -0.5)).astype(BF16)

    st_s[...] = jnp.zeros_like(st_s)
    m_s[...] = jnp.zeros_like(m_s)
    b_i = bi_ref[h]
    b_f = bf_ref[h]
    r_i = lax.broadcasted_iota(I32, (L, L), 0)
    c_i = lax.broadcasted_iota(I32, (L, L), 1)
    tril = r_i >= c_i
    strict_lower = jnp.where(r_i > c_i, 1.0, 0.0).astype(BF16)
    ones_row = jnp.where(lax.broadcasted_iota(I32, (hd, L), 0) == 0, 1.0, 0.0).astype(BF16)

    def chunk(c, _):
        start = pl.multiple_of(c * L, L)
        qc = q_s[pl.ds(start, L), :]
        kc = k_s[pl.ds(start, L), :]
        vt_aug = jnp.concatenate([vt_ref[c], ones_row], axis=0)
        li = g_ref[c, pl.ds(h, 1), :] + b_i
        fp = g_ref[c, pl.ds(MLSTM_HEADS + h, 1), :] + b_f
        lf = jnp.minimum(fp, 0.0) - jnp.log(1.0 + jnp.exp(-jnp.abs(fp)))
        a_mat = jnp.where(tril, lf, 0.0)
        a_hi, a_mid, a_lo = _split3(a_mat)
        d0 = _dot(a_hi, strict_lower) + _dot(a_mid, strict_lower) + _dot(a_lo, strict_lower)
        dmat = jnp.where(tril, d0 + li, -jnp.inf)
        b_col = jnp.sum(a_mat, axis=1, keepdims=True)
        m_prev = m_s[...]
        inter = b_col + m_prev
        mt = jnp.maximum(inter, jnp.max(dmat, axis=1, keepdims=True))
        w = jnp.exp(dmat - mt) * _dot_nt(qc, kc)
        decay = jnp.exp(inter - mt)
        st = st_s[...]
        num_aug = decay * _dot_nt(qc, st.astype(BF16)) + _dot_nt(w.astype(BF16), vt_aug)
        num = num_aug[:, :hd]
        den = num_aug[:, hd:hd + 1]
        hv = num / jnp.maximum(jnp.abs(den), jnp.exp(-mt))
        g = d0[L - 1:L, :] + li
        b_last = b_col[L - 1:L, :]
        m_new = jnp.maximum(b_last + m_prev, jnp.max(g, axis=1, keepdims=True))
        a = jnp.exp(g - m_new)
        cd = jnp.exp(b_last + m_prev - m_new)
        st_s[...] = cd * st + _dot((vt_aug.astype(F32) * a).astype(BF16), kc)
        m_s[...] = m_new
        hn = hv * lax.rsqrt(jnp.mean(hv * hv, axis=-1, keepdims=True) + RMS_EPS) * gout_ref[...]
        og = og_ref[pl.ds(start, L), :].astype(F32)
        out_ref[pl.ds(start, L), :] = (_sigmoid(og) * hn).astype(BF16)
        return 0

    lax.fori_loop(0, nchunk, chunk, 0)


def _mlstm(z, vt3, g3, og, conv_w, conv_b, wq, wk, b_i, b_f, g_out, B, S):
    T = z.shape[0]
    L = SEQ_TILE
    per_b = S // L
    hd = MLSTM_DIM
    smem = pl.BlockSpec(memory_space=pltpu.SMEM)
    seq = pl.BlockSpec((S, hd), lambda b, h: (b, h))
    return pl.pallas_call(
        functools.partial(_mlstm_body, S=S),
        out_shape=jax.ShapeDtypeStruct((T, MLSTM_INNER), BF16),
        grid=(B, MLSTM_HEADS),
        in_specs=[
            smem, smem, seq,
            pl.BlockSpec((per_b, hd, L), lambda b, h: (b, h, 0)),
            pl.BlockSpec((per_b, 16, L), lambda b, h: (b, 0, 0)),
            seq,
            pl.BlockSpec((CONV_WIDTH, hd), lambda b, h: (0, h)),
            pl.BlockSpec((1, hd), lambda b, h: (0, h)),
            pl.BlockSpec((None, hd, hd), lambda b, h: (h, 0, 0)),
            pl.BlockSpec((None, hd, hd), lambda b, h: (h, 0, 0)),
            pl.BlockSpec((1, hd), lambda b, h: (0, h)),
        ],
        out_specs=seq,
        scratch_shapes=[
            pltpu.VMEM((S + 8, hd), F32),
            pltpu.VMEM((S, hd), BF16),
            pltpu.VMEM((S, hd), BF16),
            pltpu.VMEM((2 * hd, hd), F32),
            pltpu.VMEM((1, 1), F32),
        ],
        compiler_params=_params(2), name="mlstm",
    )(b_i, b_f, z, vt3, g3, og, conv_w, conv_b, wq, wk, g_out)


def _post_body(ao_ref, mo_ref, x_ref, gt1_ref, gattn_ref, wout_ref, g2_ref, sc2_ref, sh2_ref, gt2_ref,
               wrt_ref, wgu_ref, wds_ref, base_ref, h2_ref, lg_ref):
    ao = ao_ref[...].astype(F32)
    aon = ao * lax.rsqrt(jnp.mean(ao * ao, axis=-1, keepdims=True) + RMS_EPS) * gattn_ref[...]
    mix_in = jnp.concatenate([aon.astype(BF16), mo_ref[...]], axis=1)
    x1 = x_ref[...] + gt1_ref[...] * _dot(mix_in, wout_ref[...])
    hn = x1 * lax.rsqrt(jnp.mean(x1 * x1, axis=-1, keepdims=True) + RMS_EPS) * g2_ref[...]
    h2 = hn * sc2_ref[...] + sh2_ref[...]
    words = _pack_rows(h2)
    h2_ref[0] = words[:, :HALF]
    h2_ref[1] = words[:, HALF:]
    h2b = h2.astype(BF16)
    gu = _dot(h2b, wgu_ref[...])
    act = (_silu(gu[:, :SHARED_DIM]) * gu[:, SHARED_DIM:]).astype(BF16)
    base_ref[...] = x1 + gt2_ref[...] * _dot(act, wds_ref[...])
    lg_ref[...] = _dot_nt(wrt_ref[...], h2b)


def _post(ao, mo, x2, gt1, gattn, wout, g2, sc2p, sh2, gt2, wrt, wgu, wds, S):
    T = x2.shape[0]
    tm = SEQ_TILE
    per_b = S // tm
    full = lambda a: pl.BlockSpec(a.shape, lambda i: (0,) * a.ndim)
    row = lambda w: pl.BlockSpec((tm, w), lambda i: (i, 0))
    bmod = pl.BlockSpec((None, 1, D_MODEL), lambda i: (i // per_b, 0, 0))
    return pl.pallas_call(
        _post_body,
        out_shape=(jax.ShapeDtypeStruct((T, D_MODEL), F32), jax.ShapeDtypeStruct((2, T, HALF), U32),
                   jax.ShapeDtypeStruct((N_EXPERTS, T), F32)),
        grid=(T // tm,),
        in_specs=[row(MLA_HEADS * V_DIM), row(MLSTM_INNER), row(D_MODEL), bmod, full(gattn), full(wout),
                  full(g2), bmod, bmod, bmod, full(wrt), full(wgu), full(wds)],
        out_specs=(row(D_MODEL), pl.BlockSpec((2, tm, HALF), lambda i: (0, i, 0)),
                   pl.BlockSpec((N_EXPERTS, tm), lambda i: (0, i))),
        compiler_params=_params(1), name="post",
    )(ao, mo, x2, gt1, gattn, wout, g2, sc2p, sh2, gt2, wrt, wgu, wds)


def _first_max(x, rows, n):
    mx = jnp.max(x, axis=0, keepdims=True)
    idx = jnp.min(jnp.where(x == mx, rows, n), axis=0, keepdims=True)
    return mx, idx


def _route_body(lg_ref, bias_ref, idx_ref, w_ref, rank_ref, cnt_ref, carry):
    i = pl.program_id(0)
    tt = lg_ref.shape[1]

    @pl.when(i == 0)
    def _():
        carry[...] = jnp.zeros_like(carry)

    sc = _sigmoid(lg_ref[...])
    bi = sc + bias_ref[...]
    rows = lax.broadcasted_iota(I32, (N_EXPERTS, tt), 0)
    rows_g = lax.broadcasted_iota(I32, (GROUP_SIZE, tt), 0)
    rows_n = lax.broadcasted_iota(I32, (N_GROUPS, tt), 0)
    neg = -jnp.inf
    gs = []
    for g in range(N_GROUPS):
        xg = bi[g * GROUP_SIZE:(g + 1) * GROUP_SIZE]
        m1, i1 = _first_max(xg, rows_g, GROUP_SIZE)
        m2 = jnp.max(jnp.where(rows_g == i1, neg, xg), axis=0, keepdims=True)
        gs.append(m1 + m2)
    cur = jnp.concatenate(gs, axis=0)
    keep = jnp.zeros((N_GROUPS, tt), F32)
    for _ in range(TOPK_GROUPS):
        _, gi = _first_max(cur, rows_n, N_GROUPS)
        sel = rows_n == gi
        keep = jnp.where(sel, 1.0, keep)
        cur = jnp.where(sel, neg, cur)
    masked = jnp.concatenate(
        [jnp.where(keep[g:g + 1] > 0.0, bi[g * GROUP_SIZE:(g + 1) * GROUP_SIZE], neg) for g in range(N_GROUPS)], axis=0)
    cur = masked
    onehot = jnp.zeros((N_EXPERTS, tt), F32)
    idxs, ws = [], []
    for _ in range(TOP_K):
        _, ei = _first_max(cur, rows, N_EXPERTS)
        sel = rows == ei
        ws.append(jnp.sum(jnp.where(sel, sc, 0.0), axis=0, keepdims=True))
        idxs.append(ei)
        cur = jnp.where(sel, neg, cur)
        onehot = jnp.where(sel, 1.0, onehot)
    wsum = ws[0]
    for k in range(1, TOP_K):
        wsum = wsum + ws[k]
    idx_ref[...] = jnp.concatenate(idxs, axis=0)
    w_ref[...] = jnp.concatenate([w / wsum * ROUTED_SCALE for w in ws], axis=0)
    r_i = lax.broadcasted_iota(I32, (tt, tt), 0)
    c_i = lax.broadcasted_iota(I32, (tt, tt), 1)
    before = jnp.where(r_i < c_i, 1.0, 0.0).astype(BF16)
    tot = _dot(onehot.astype(BF16), before) + carry[...]
    rank_ref[...] = jnp.concatenate(
        [jnp.sum(jnp.where(rows == ei, tot, 0.0), axis=0, keepdims=True) for ei in idxs], axis=0).astype(I32)
    carry[...] = carry[...] + jnp.sum(onehot, axis=1, keepdims=True)
    cnt_ref[...] = jnp.broadcast_to(carry[...], cnt_ref.shape)


def _route(lgt, bias_col):
    T = lgt.shape[1]
    tt = SEQ_TILE
    blk = pl.BlockSpec((TOP_K, tt), lambda i: (0, i))
    return pl.pallas_call(
        _route_body,
        out_shape=(jax.ShapeDtypeStruct((TOP_K, T), I32), jax.ShapeDtypeStruct((TOP_K, T), F32),
                   jax.ShapeDtypeStruct((TOP_K, T), I32), jax.ShapeDtypeStruct((N_EXPERTS, LANES), F32)),
        grid=(T // tt,),
        in_specs=[pl.BlockSpec((N_EXPERTS, tt), lambda i: (0, i)), pl.BlockSpec((N_EXPERTS, 1), lambda i: (0, 0))],
        out_specs=(blk, blk, blk, pl.BlockSpec((N_EXPERTS, LANES), lambda i: (0, 0))),
        scratch_shapes=[pltpu.VMEM((N_EXPERTS, 1), F32)],
        compiler_params=_params(1), name="route",
    )(lgt, bias_col)


def _dest_body(idx_ref, rank_ref, off_ref, dest_ref):
    tt = idx_ref.shape[1]
    rows = lax.broadcasted_iota(I32, (N_EXPERTS, tt), 0)
    off = off_ref[...]
    outs = []
    for k in range(TOP_K):
        sel = rows == idx_ref[k:k + 1, :]
        outs.append(jnp.sum(jnp.where(sel, off, 0), axis=0, keepdims=True) + rank_ref[k:k + 1, :])
    dest_ref[...] = jnp.concatenate(outs, axis=0)


def _dest(idx_t, rank_t, off_col):
    T = idx_t.shape[1]
    tt = SEQ_TILE
    blk = pl.BlockSpec((TOP_K, tt), lambda i: (0, i))
    return pl.pallas_call(
        _dest_body, out_shape=jax.ShapeDtypeStruct((TOP_K, T), I32), grid=(T // tt,),
        in_specs=[blk, blk, pl.BlockSpec((N_EXPERTS, 1), lambda i: (0, 0))], out_specs=blk,
        compiler_params=_params(1), name="dest",
    )(idx_t, rank_t, off_col)


SC_WINDOW = 128
HALF = PACKED // 2


def _sc_mesh():
    return plsc.VectorSubcoreMesh(core_axis_name="c", subcore_axis_name="s")


def _scatter_rows(dest2_t, rows, n_out):
    n = rows.shape[0]

    @pl.kernel(out_type=jax.ShapeDtypeStruct((n_out, HALF), U32), mesh=_sc_mesh(), scratch_types=[])
    def scatter(d_hbm, x_hbm, o_hbm):
        def body(d_vmem, x_vmem):
            for k in range(TOP_K):
                pltpu.sync_copy(x_vmem, o_hbm.at[d_vmem.at[k]])

        pltpu.emit_pipeline(
            body, grid=(n // SC_WINDOW,),
            in_specs=[pl.BlockSpec((TOP_K, SC_WINDOW), lambda i: (0, i)),
                      pl.BlockSpec((SC_WINDOW, HALF), lambda i: (i, 0))],
            out_specs=[], core_axis_name=("c", "s"), dimension_semantics=(pltpu.PARALLEL,),
        )(d_hbm, x_hbm)

    return scatter(dest2_t, rows)


def _gather_rows(index_row, rows):
    n = index_row.shape[1]

    @pl.kernel(out_type=jax.ShapeDtypeStruct((n, HALF), U32), mesh=_sc_mesh(), scratch_types=[])
    def gather(i_hbm, y_hbm, o_hbm):
        def body(i_vmem, o_vmem):
            pltpu.sync_copy(y_hbm.at[i_vmem.at[0]], o_vmem)

        pltpu.emit_pipeline(
            body, grid=(n // SC_WINDOW,),
            in_specs=[pl.BlockSpec((1, SC_WINDOW), lambda i: (0, i))],
            out_specs=[pl.BlockSpec((SC_WINDOW, HALF), lambda i: (i, 0))],
            core_axis_name=("c", "s"), dimension_semantics=(pltpu.PARALLEL,),
        )(i_hbm, o_hbm)

    return gather(index_row, rows)


def _experts_body(first_ref, nblk_ref, cnt_ref, xs_ref, wg_ref, wu_ref, wd_ref, y_ref,
                  xbuf, ybuf, wgb, wub, wdb, xsem, ysem):
    e = pl.program_id(0)
    last = pl.num_programs(0) - 1
    b0 = first_ref[e]
    nb = nblk_ref[e]
    total = first_ref[last] + nblk_ref[last]

    def rows(g):
        return pl.ds(pl.multiple_of(g * ROW_BLOCK, ROW_BLOCK), ROW_BLOCK)

    def x_copy(g, slot):
        return pltpu.make_async_copy(xs_ref.at[:, rows(g), :], xbuf.at[slot], xsem.at[slot])

    def y_copy(g, slot):
        return pltpu.make_async_copy(ybuf.at[slot], y_ref.at[:, rows(g), :], ysem.at[slot])

    @pl.when(e == 0)
    def _():
        x_copy(0, 0).start()

    @pl.when(nb > 0)
    def _():
        wgb[...] = wg_ref[...].astype(BF16)
        wub[...] = wu_ref[...].astype(BF16)
        wdb[...] = wd_ref[...].astype(BF16)

    def block(i, carry):
        g = b0 + i
        slot = g % 2

        @pl.when(g + 1 < total)
        def _():
            x_copy(g + 1, 1 - slot).start()

        x_copy(g, slot).wait()

        @pl.when(g >= 2)
        def _():
            y_copy(g - 2, slot).wait()

        words = jnp.concatenate([xbuf[slot, 0], xbuf[slot, 1]], axis=1)
        n_live = cnt_ref[e] - i * ROW_BLOCK
        words = jnp.where(lax.broadcasted_iota(I32, words.shape, 0) < n_live, words, U32(0))
        x = jnp.concatenate(_unpack_rows(words), axis=1).astype(BF16)
        act = (_silu(_dot(x, wgb[...])) * _dot(x, wub[...])).astype(BF16)
        y = _pack_rows(_dot(act, wdb[...]))
        ybuf[slot, 0] = y[:, :HALF]
        ybuf[slot, 1] = y[:, HALF:]
        y_copy(g, slot).start()
        return carry

    lax.fori_loop(0, nb, block, 0)

    @pl.when(e == last)
    def _():
        for back in (2, 1):
            @pl.when(total >= back)
            def _():
                g = total - back
                y_copy(g, g % 2).wait()


def _experts(first_block, n_blocks, counts, xs, wg, wu, wd):
    n_rows = xs.shape[1]
    wmap = lambda e, fb, nb, cnt: (e, 0, 0)
    anyspec = pl.BlockSpec(memory_space=pl.ANY)
    ring = pltpu.VMEM((2, 2, ROW_BLOCK, HALF), U32)
    grid_spec = pltpu.PrefetchScalarGridSpec(
        num_scalar_prefetch=3, grid=(N_EXPERTS,),
        in_specs=[
            anyspec,
            pl.BlockSpec((None, D_MODEL, EXPERT_DIM), wmap),
            pl.BlockSpec((None, D_MODEL, EXPERT_DIM), wmap),
            pl.BlockSpec((None, EXPERT_DIM, D_MODEL), wmap),
        ],
        out_specs=anyspec,
        scratch_shapes=[ring, ring,
                        pltpu.VMEM((D_MODEL, EXPERT_DIM), BF16), pltpu.VMEM((D_MODEL, EXPERT_DIM), BF16),
                        pltpu.VMEM((EXPERT_DIM, D_MODEL), BF16),
                        pltpu.SemaphoreType.DMA((2,)), pltpu.SemaphoreType.DMA((2,))],
    )
    return pl.pallas_call(
        _experts_body, out_shape=jax.ShapeDtypeStruct((2, n_rows, HALF), U32), grid_spec=grid_spec,
        compiler_params=_params(1), name="experts",
    )(first_block, n_blocks, counts, xs, wg, wu, wd)


def _combine_body(*refs):
    ya_refs = refs[:TOP_K]
    yb_refs = refs[TOP_K:2 * TOP_K]
    base_ref, w_ref, gt2_ref, out_ref = refs[2 * TOP_K:]
    w = w_ref[...]
    r_lo = r_hi = None
    for k in range(TOP_K):
        lo, hi = _unpack_rows(jnp.concatenate([ya_refs[k][...], yb_refs[k][...]], axis=1))
        wk = w[:, k:k + 1]
        r_lo = wk * lo if r_lo is None else r_lo + wk * lo
        r_hi = wk * hi if r_hi is None else r_hi + wk * hi
    out_ref[...] = base_ref[...] + gt2_ref[...] * jnp.concatenate([r_lo, r_hi], axis=1)


def _combine(y_slots, base, w_tk, gt2, S):
    T = base.shape[0]
    tt = SEQ_TILE
    nt = T // tt
    per_b = S // tt
    yspec = lambda half, k: pl.BlockSpec((tt, HALF), lambda i: ((half * TOP_K + k) * nt + i, 0))
    return pl.pallas_call(
        _combine_body, out_shape=jax.ShapeDtypeStruct((T, D_MODEL), F32), grid=(nt,),
        in_specs=[yspec(half, k) for half in range(2) for k in range(TOP_K)] + [
            pl.BlockSpec((tt, D_MODEL), lambda i: (i, 0)),
            pl.BlockSpec((tt, TOP_K), lambda i: (i, 0)),
            pl.BlockSpec((None, 1, D_MODEL), lambda i: (i // per_b, 0, 0)),
        ],
        out_specs=pl.BlockSpec((tt, D_MODEL), lambda i: (i, 0)),
        compiler_params=_params(1), name="combine",
    )(*([y_slots] * (2 * TOP_K)), base, w_tk, gt2)


def _pad_heads(w, width):
    kdim = w.shape[0]
    w3 = w.reshape(kdim, MLA_HEADS, width)
    return jnp.pad(w3, ((0, 0), (0, 0), (0, LANES - width))).reshape(kdim, MLA_HEADS * LANES)


def _rot_partner(w_rope):
    half = QK_ROPE // 2
    return jnp.concatenate([-w_rope[..., half:], w_rope[..., :half]], axis=-1)


def _lane_vec(nope, rope):
    return jnp.concatenate([nope, rope, jnp.zeros((LANES - QK_DIM,), F32)]).reshape(1, LANES)


def _layer(x2, ada, positions, B, S, g_norm1, w_in, g_cq, w_uq, g_ckv, w_ukv, g_qn, g_kn, g_attn_out,
           conv_w, conv_b, w_mq, w_mk, b_igate, b_fgate, g_mlstm_out, w_out, g_norm2, w_router, router_bias,
           w_gate_exp, w_up_exp, w_down_exp, w_gate_sh, w_up_sh, w_down_sh):
    T = B * S
    sh1, sc1, gt1, sh2, sc2, gt2 = [a.reshape(B, 1, D_MODEL) for a in jnp.split(ada, 6, axis=-1)]
    row = lambda v: v.reshape(1, -1).astype(F32)

    inv_freq = ROPE_THETA ** (-jnp.arange(0, QK_ROPE, 2, dtype=F32) / QK_ROPE)
    ang = positions.astype(F32).reshape(T, 1) * inv_freq
    cos, sin = jnp.cos(ang), jnp.sin(ang)
    pad_r = jnp.zeros((T, LANES - QK_DIM), F32)
    cosp = jnp.concatenate([jnp.ones((T, QK_NOPE), F32), cos, cos, pad_r], axis=1)
    sinp = jnp.concatenate([jnp.zeros((T, QK_NOPE), F32), sin, sin, pad_r], axis=1)

    o0 = 0
    w_cq = w_in[:, o0:o0 + Q_LORA]; o0 += Q_LORA
    w_ckv = w_in[:, o0:o0 + KV_LORA]; o0 += KV_LORA
    w_kr = w_in[:, o0:o0 + QK_ROPE]; o0 += QK_ROPE
    w_z = w_in[:, o0:o0 + MLSTM_INNER]; o0 += MLSTM_INNER
    w_v = w_in[:, o0:o0 + MLSTM_INNER]; o0 += MLSTM_INNER
    w_o = w_in[:, o0:o0 + MLSTM_INNER]; o0 += MLSTM_INNER
    w_i = w_in[:, o0:o0 + MLSTM_HEADS]; o0 += MLSTM_HEADS
    w_f = w_in[:, o0:o0 + MLSTM_HEADS]
    zl = jnp.zeros((D_MODEL, QK_NOPE), F32)
    zr = jnp.zeros((D_MODEL, LANES - QK_DIM), F32)
    wa = jnp.concatenate([w_cq, w_ckv, w_z, w_o, zl, w_kr, zr, zl, _rot_partner(w_kr), zr], axis=1).astype(BF16)
    wt = jnp.concatenate([w_v.T, w_i.T, w_f.T, jnp.zeros((16 - 2 * MLSTM_HEADS, D_MODEL), F32)], axis=0).astype(BF16)

    uq3 = w_uq.reshape(Q_LORA, MLA_HEADS, QK_DIM)
    uq_rot = jnp.concatenate([jnp.zeros((Q_LORA, MLA_HEADS, QK_NOPE), F32), _rot_partner(uq3[..., QK_NOPE:])], axis=-1)
    wuq = _pad_heads(w_uq, QK_DIM).astype(BF16)
    wuqs = _pad_heads(uq_rot.reshape(Q_LORA, MLA_HEADS * QK_DIM), QK_DIM).astype(BF16)
    ukv3 = w_ukv.reshape(KV_LORA, MLA_HEADS, QK_NOPE + V_DIM)
    wuk = _pad_heads(ukv3[..., :QK_NOPE].reshape(KV_LORA, MLA_HEADS * QK_NOPE), QK_NOPE).astype(BF16)
    wuv = ukv3[..., QK_NOPE:].reshape(KV_LORA, MLA_HEADS * V_DIM).astype(BF16)
    gq = _lane_vec(g_qn[:QK_NOPE], g_qn[QK_NOPE:])
    gqs = _lane_vec(jnp.zeros((QK_NOPE,), F32), _rot_partner(g_qn[QK_NOPE:]) * jnp.concatenate(
        [-jnp.ones((QK_ROPE // 2,), F32), jnp.ones((QK_ROPE // 2,), F32)]))
    gk = _lane_vec(g_kn[:QK_NOPE], g_kn[QK_NOPE:])
    gks = _lane_vec(jnp.zeros((QK_NOPE,), F32), _rot_partner(g_kn[QK_NOPE:]) * jnp.concatenate(
        [-jnp.ones((QK_ROPE // 2,), F32), jnp.ones((QK_ROPE // 2,), F32)]))

    z, og, vt3, g3, q, k, mv = _inproj(
        x2, 1.0 + sc1, sh1, row(g_norm1), wa, wt, row(g_cq), wuq, wuqs, row(g_ckv), wuk, wuv,
        gq, gqs, gk, gks, cosp, sinp, S)
    attn_o = _attn(q, k, mv, B, S)
    mlstm_o = _mlstm(z, vt3, g3, og, conv_w.reshape(CONV_WIDTH, MLSTM_INNER), row(conv_b),
                     w_mq.astype(BF16), w_mk.astype(BF16), b_igate, b_fgate, row(g_mlstm_out), B, S)

    wgu = jnp.concatenate([w_gate_sh, w_up_sh], axis=1).astype(BF16)
    base, h2, lgt = _post(attn_o, mlstm_o, x2, gt1, row(g_attn_out), w_out.astype(BF16), row(g_norm2),
                          1.0 + sc2, sh2, gt2, w_router.T.astype(BF16), wgu, w_down_sh.astype(BF16), S)

    idx_t, w_t, rank_t, cnt = _route(lgt, router_bias.reshape(N_EXPERTS, 1))
    counts = cnt[:, 0].astype(I32)
    padded = (counts + ROW_BLOCK - 1) // ROW_BLOCK * ROW_BLOCK
    padded_end = jnp.cumsum(padded)
    off = padded_end - padded
    n_rows = T * TOP_K + N_EXPERTS * ROW_BLOCK
    dest_t = _dest(idx_t, rank_t, off.reshape(N_EXPERTS, 1))

    dest2_t = jnp.concatenate([dest_t, dest_t + n_rows], axis=1)
    xs = _scatter_rows(dest2_t, h2.reshape(2 * T, HALF), 2 * n_rows).reshape(2, n_rows, HALF)
    y = _experts(off // ROW_BLOCK, padded // ROW_BLOCK, counts, xs, w_gate_exp, w_up_exp, w_down_exp)
    dest_row = dest_t.reshape(1, TOP_K * T)
    y_slots = _gather_rows(jnp.concatenate([dest_row, dest_row + n_rows], axis=1), y.reshape(2 * n_rows, HALF))
    return _combine(y_slots, base, w_t.T, gt2, S)


def kernel(x, c, positions, w_ada, b_ada, g_norm1, w_in, g_cq, w_uq, g_ckv, w_ukv, g_qn, g_kn, g_attn_out, conv_w, conv_b, w_mq, w_mk, b_igate, b_fgate, g_mlstm_out, w_out, g_norm2, w_router, router_bias, w_gate_exp, w_up_exp, w_down_exp, w_gate_sh, w_up_sh, w_down_sh):
    B, S, _ = x.shape
    depth = w_ada.shape[0]
    x2 = x.reshape(B * S, D_MODEL)
    for l in range(depth):
        ada = _ada(c, w_ada[l], b_ada[l])
        x2 = _layer(x2, ada, positions, B, S, g_norm1[l], w_in[l], g_cq[l], w_uq[l], g_ckv[l], w_ukv[l], g_qn[l],
                    g_kn[l], g_attn_out[l], conv_w[l], conv_b[l], w_mq[l], w_mk[l], b_igate[l], b_fgate[l],
                    g_mlstm_out[l], w_out[l], g_norm2[l], w_router[l], router_bias[l], w_gate_exp[l], w_up_exp[l],
                    w_down_exp[l], w_gate_sh[l], w_up_sh[l], w_down_sh[l])
    return x2.reshape(B, S, D_MODEL)
```

```python
import functools
import math

import jax
import jax.numpy as jnp
from jax import lax
from jax.experimental import pallas as pl
from jax.experimental.pallas import tpu as pltpu
from jax.experimental.pallas import tpu_sc as plsc

F32 = jnp.float32
BF16 = jnp.bfloat16
I32 = jnp.int32

D_MODEL = 1024
MLA_HEADS = 8
QK_NOPE = 64
QK_ROPE = 32
QK_DIM = QK_NOPE + QK_ROPE
V_DIM = 64
Q_LORA = 384
KV_LORA = 256
ROPE_THETA = 10000.0
MLSTM_HEADS = 4
MLSTM_DIM = 128
MLSTM_INNER = MLSTM_HEADS * MLSTM_DIM
CONV_WIDTH = 4
N_EXPERTS = 256
TOP_K = 8
N_GROUPS = 8
TOPK_GROUPS = 4
GROUP_SIZE = N_EXPERTS // N_GROUPS
EXPERT_DIM = 256
SHARED_DIM = 256
ROUTED_SCALE = 2.5
RMS_EPS = 1e-6

LANES = 128
SEQ_TILE = 256
ROW_BLOCK = 256
VMEM_LIMIT = 56 * 1024 * 1024

_OFF_CQ = 0
_OFF_CKV = _OFF_CQ + Q_LORA
_OFF_Z = _OFF_CKV + KV_LORA
_OFF_O = _OFF_Z + MLSTM_INNER
_OFF_KR = _OFF_O + MLSTM_INNER
_OFF_KRS = _OFF_KR + LANES
_WA_COLS = _OFF_KRS + LANES
_WT_ROWS = MLSTM_INNER + 16


def _params(n_axes, vmem=VMEM_LIMIT):
    return pltpu.CompilerParams(dimension_semantics=("arbitrary",) * n_axes, vmem_limit_bytes=vmem)


def _dot(a, b):
    return jnp.dot(a, b, preferred_element_type=F32)


def _dot_nt(a, b):
    return lax.dot_general(a, b, (((1,), (1,)), ((), ())), preferred_element_type=F32)


def _sigmoid(x):
    return 1.0 / (1.0 + jnp.exp(-x))


def _silu(x):
    return x * _sigmoid(x)


U32 = jnp.uint32
PACKED = D_MODEL // 2


def _pack_rows(x):
    w = x.shape[1] // 2
    lo = lax.bitcast_convert_type(x[:, :w].astype(BF16).astype(F32), U32)
    hi = lax.bitcast_convert_type(x[:, w:].astype(BF16).astype(F32), U32)
    return lax.shift_right_logical(lo, U32(16)) | (hi & U32(0xFFFF0000))


def _unpack_rows(p):
    lo = lax.bitcast_convert_type(lax.shift_left(p, U32(16)), F32)
    hi = lax.bitcast_convert_type(p & U32(0xFFFF0000), F32)
    return lo, hi


def _ada_body(c_ref, w_ref, b_ref, o_ref):
    c = c_ref[...]
    cond = _silu(c).astype(BF16)
    o_ref[...] = _dot(cond, w_ref[...].astype(BF16)) + b_ref[...]


def _ada(c, w_ada, b_ada):
    B = c.shape[0]
    n = w_ada.shape[1]
    tn = 1024
    return pl.pallas_call(
        _ada_body,
        out_shape=jax.ShapeDtypeStruct((B, n), F32),
        grid=(n // tn,),
        in_specs=[
            pl.BlockSpec((B, D_MODEL), lambda j: (0, 0)),
            pl.BlockSpec((D_MODEL, tn), lambda j: (0, j)),
            pl.BlockSpec((1, tn), lambda j: (0, j)),
        ],
        out_specs=pl.BlockSpec((B, tn), lambda j: (0, j)),
        compiler_params=_params(1),
        name="ada",
    )(c, w_ada, b_ada.reshape(1, n))


def _inproj_body(x_ref, sc_ref, sh_ref, g1_ref, wa_ref, wt_ref, gcq_ref, wuq_ref, wuqs_ref, gckv_ref,
                 wuk_ref, wuv_ref, gq_ref, gqs_ref, gk_ref, gks_ref, cos_ref, sin_ref,
                 z_ref, o_ref, vt_ref, gt_ref, q_ref, k_ref, mv_ref):
    x = x_ref[...]
    hn = x * lax.rsqrt(jnp.mean(x * x, axis=-1, keepdims=True) + RMS_EPS) * g1_ref[...]
    hm = (hn * sc_ref[...] + sh_ref[...]).astype(BF16)
    p = _dot(hm, wa_ref[...])
    z_ref[...] = p[:, _OFF_Z:_OFF_Z + MLSTM_INNER].astype(BF16)
    o_ref[...] = p[:, _OFF_O:_OFF_O + MLSTM_INNER].astype(BF16)
    rt = _dot_nt(wt_ref[...], hm)
    vt_ref[...] = rt[:MLSTM_INNER].astype(BF16)
    gt_ref[...] = rt[MLSTM_INNER:]

    cq = p[:, _OFF_CQ:_OFF_CQ + Q_LORA]
    cqn = (cq * lax.rsqrt(jnp.mean(cq * cq, axis=-1, keepdims=True) + RMS_EPS) * gcq_ref[...]).astype(BF16)
    ckv = p[:, _OFF_CKV:_OFF_CKV + KV_LORA]
    ckvn = (ckv * lax.rsqrt(jnp.mean(ckv * ckv, axis=-1, keepdims=True) + RMS_EPS) * gckv_ref[...]).astype(BF16)
    qa = _dot(cqn, wuq_ref[...])
    qb = _dot(cqn, wuqs_ref[...])
    ka = _dot(ckvn, wuk_ref[...])
    mv_ref[...] = _dot(ckvn, wuv_ref[...]).astype(BF16)
    kr = p[:, _OFF_KR:_OFF_KR + LANES]
    krs = p[:, _OFF_KRS:_OFF_KRS + LANES]

    cosp = cos_ref[...]
    sinp = sin_ref[...]
    gqc = gq_ref[...] * cosp
    gqs = gqs_ref[...] * sinp
    gkc = gk_ref[...] * cosp
    k_rot = krs * (gks_ref[...] * sinp)
    kr_ss = jnp.sum(kr * kr, axis=-1, keepdims=True)
    q_scale = (QK_DIM ** -0.5) * math.log2(math.e)
    for h in range(MLA_HEADS):
        sl = slice(h * LANES, (h + 1) * LANES)
        qah = qa[:, sl]
        rq = lax.rsqrt(jnp.sum(qah * qah, axis=-1, keepdims=True) * (1.0 / QK_DIM) + RMS_EPS) * q_scale
        q_ref[:, sl] = ((qah * gqc + qb[:, sl] * gqs) * rq).astype(BF16)
        kah = ka[:, sl]
        rk = lax.rsqrt((jnp.sum(kah * kah, axis=-1, keepdims=True) + kr_ss) * (1.0 / QK_DIM) + RMS_EPS)
        k_ref[:, sl] = (((kah + kr) * gkc + k_rot) * rk).astype(BF16)


def _inproj(x2, sc1p, sh1, g1, wa, wt, gcq, wuq, wuqs, gckv, wuk, wuv, gq, gqs, gk, gks, cosp, sinp, S):
    T = x2.shape[0]
    tm = SEQ_TILE
    nt = T // tm
    per_b = S // tm
    hp = MLA_HEADS * LANES
    full = lambda a: pl.BlockSpec(a.shape, lambda i: (0,) * a.ndim)
    row = lambda w: pl.BlockSpec((tm, w), lambda i: (i, 0))
    bmod = pl.BlockSpec((None, 1, D_MODEL), lambda i: (i // per_b, 0, 0))
    out_shapes = (
        jax.ShapeDtypeStruct((T, MLSTM_INNER), BF16),
        jax.ShapeDtypeStruct((T, MLSTM_INNER), BF16),
        jax.ShapeDtypeStruct((nt, MLSTM_INNER, tm), BF16),
        jax.ShapeDtypeStruct((nt, 16, tm), F32),
        jax.ShapeDtypeStruct((T, hp), BF16),
        jax.ShapeDtypeStruct((T, hp), BF16),
        jax.ShapeDtypeStruct((T, MLA_HEADS * V_DIM), BF16),
    )
    out_specs = (
        row(MLSTM_INNER), row(MLSTM_INNER),
        pl.BlockSpec((None, MLSTM_INNER, tm), lambda i: (i, 0, 0)),
        pl.BlockSpec((None, 16, tm), lambda i: (i, 0, 0)),
        row(hp), row(hp), row(MLA_HEADS * V_DIM),
    )
    in_specs = [row(D_MODEL), bmod, bmod, full(g1), full(wa), full(wt), full(gcq), full(wuq), full(wuqs),
                full(gckv), full(wuk), full(wuv), full(gq), full(gqs), full(gk), full(gks),
                row(LANES), row(LANES)]
    return pl.pallas_call(
        _inproj_body, out_shape=out_shapes, grid=(nt,), in_specs=in_specs, out_specs=out_specs,
        compiler_params=_params(1), name="inproj",
    )(x2, sc1p, sh1, g1, wa, wt, gcq, wuq, wuqs, gckv, wuk, wuv, gq, gqs, gk, gks, cosp, sinp)


ATTN_HEADS_PER_STEP = 4


def _pairs_loop(n, fn, init):
    c = lax.fori_loop(0, n // 2, lambda jj, c: fn(2 * jj + 1, fn(2 * jj, c)), init)
    return lax.fori_loop(0, n % 2, lambda _, c: fn(n - 1, c), c)


def _attn_body(q_ref, k_ref, v_ref, o_ref):
    i = pl.program_id(2)
    t = SEQ_TILE
    nh = ATTN_HEADS_PER_STEP
    qry_pos = lax.broadcasted_iota(I32, (t, t), 0)
    key_pos = lax.broadcasted_iota(I32, (t, t), 1)
    causal = key_pos <= qry_pos
    qs = [q_ref[:, hh * LANES:(hh + 1) * LANES] for hh in range(nh)]

    def scores(j, hh, diag):
        kj = k_ref[pl.ds(pl.multiple_of(j * t, t), t), hh * LANES:(hh + 1) * LANES]
        s = _dot_nt(qs[hh], kj)
        return jnp.where(causal, s, -jnp.inf) if diag else s

    def halves(x):
        return x[:, :LANES], x[:, LANES:]

    def max_pass(j, mx, diag):
        out = []
        for hh in range(nh):
            lo, hi = halves(scores(j, hh, diag))
            out.append(jnp.maximum(mx[hh], jnp.maximum(lo, hi)))
        return tuple(out)

    mx = tuple(jnp.full((t, LANES), -jnp.inf, F32) for _ in range(nh))
    mx = _pairs_loop(i, lambda j, c: max_pass(j, c, False), mx)
    mx = max_pass(i, mx, True)
    mb = [jnp.broadcast_to(jnp.max(mx[hh], axis=1, keepdims=True), (t, LANES)) for hh in range(nh)]

    def acc_pass(j, carry, diag):
        row0 = pl.multiple_of(j * t, t)
        out = []
        for hh in range(nh):
            l128, acc = carry[hh]
            lo, hi = halves(scores(j, hh, diag))
            p_lo = jnp.exp2(lo - mb[hh])
            p_hi = jnp.exp2(hi - mb[hh])
            l128 = l128 + (p_lo + p_hi)
            p = jnp.concatenate([p_lo, p_hi], axis=1).astype(BF16)
            pair = hh // 2
            acc = acc + _dot(p, v_ref[pl.ds(row0, t), pair * LANES:(pair + 1) * LANES])
            out.append((l128, acc))
        return tuple(out)

    init = tuple((jnp.zeros((t, LANES), F32), jnp.zeros((t, LANES), F32)) for _ in range(nh))
    carry = _pairs_loop(i, lambda j, c: acc_pass(j, c, False), init)
    final = acc_pass(i, carry, True)
    outs = [acc / jnp.sum(l128, axis=1, keepdims=True) for (l128, acc) in final]
    first_head = lax.broadcasted_iota(I32, (t, LANES), 1) < V_DIM
    o_ref[...] = jnp.concatenate(
        [jnp.where(first_head, outs[2 * pr], outs[2 * pr + 1]) for pr in range(nh // 2)], axis=1).astype(BF16)


def _attn(q, k, mvt, B, S):
    T = q.shape[0]
    t = SEQ_TILE
    nh = ATTN_HEADS_PER_STEP
    per_b = S // t
    return pl.pallas_call(
        _attn_body,
        out_shape=jax.ShapeDtypeStruct((T, MLA_HEADS * V_DIM), BF16),
        grid=(B, MLA_HEADS // nh, per_b),
        in_specs=[
            pl.BlockSpec((t, nh * LANES), lambda b, hp, i: (b * per_b + i, hp)),
            pl.BlockSpec((S, nh * LANES), lambda b, hp, i: (b, hp)),
            pl.BlockSpec((S, nh * V_DIM), lambda b, hp, i: (b, hp)),
        ],
        out_specs=pl.BlockSpec((t, nh * V_DIM), lambda b, hp, i: (b * per_b + i, hp)),
        compiler_params=_params(3), name="attn",
    )(q, k, mvt)


def _split3(a):
    hi = a.astype(BF16)
    r = a - hi.astype(F32)
    mid = r.astype(BF16)
    lo = (r - mid.astype(F32)).astype(BF16)
    return hi, mid, lo


def _mlstm_body(bi_ref, bf_ref, z_ref, vt_ref, g_ref, og_ref, cw_ref, cb_ref, wq_ref, wk_ref, gout_ref,
                out_ref, zpad, q_s, k_s, st_s, m_s, *, S):
    h = pl.program_id(1)
    L = SEQ_TILE
    nchunk = S // L
    hd = MLSTM_DIM
    zpad[0:8, :] = jnp.zeros((8, hd), F32)
    zpad[8:, :] = z_ref[...].astype(F32)
    zc = cb_ref[...] + jnp.zeros((S, hd), F32)
    for j in range(CONV_WIDTH):
        zc = zc + cw_ref[j:j + 1, :] * zpad[pl.ds(8 - (CONV_WIDTH - 1) + j, S), :]
    zs = _silu(zc).astype(BF16)
    q_s[...] = _dot(zs, wq_ref[...]).astype(BF16)
    k_s[...] = (_dot(zs, wk_ref[...]) * (hd ** -0.5)).astype(BF16)

    st_s[...] = jnp.zeros_like(st_s)
    m_s[...] = jnp.zeros_like(m_s)
    b_i = bi_ref[h]
    b_f = bf_ref[h]
    r_i = lax.broadcasted_iota(I32, (L, L), 0)
    c_i = lax.broadcasted_iota(I32, (L, L), 1)
    tril = r_i >= c_i
    strict_lower = jnp.where(r_i > c_i, 1.0, 0.0).astype(BF16)
    ones_row = jnp.where(lax.broadcasted_iota(I32, (hd, L), 0) == 0, 1.0, 0.0).astype(BF16)

    def chunk(c, _):
        start = pl.multiple_of(c * L, L)
        qc = q_s[pl.ds(start, L), :]
        kc = k_s[pl.ds(start, L), :]
        vt_aug = jnp.concatenate([vt_ref[c], ones_row], axis=0)
        li = g_ref[c, pl.ds(h, 1), :] + b_i
        fp = g_ref[c, pl.ds(MLSTM_HEADS + h, 1), :] + b_f
        lf = jnp.minimum(fp, 0.0) - jnp.log(1.0 + jnp.exp(-jnp.abs(fp)))
        a_mat = jnp.where(tril, lf, 0.0)
        a_hi, a_mid, a_lo = _split3(a_mat)
        d0 = _dot(a_hi, strict_lower) + _dot(a_mid, strict_lower) + _dot(a_lo, strict_lower)
        dmat = jnp.where(tril, d0 + li, -jnp.inf)
        b_col = jnp.sum(a_mat, axis=1, keepdims=True)
        m_prev = m_s[...]
        inter = b_col + m_prev
        mt = jnp.maximum(inter, jnp.max(dmat, axis=1, keepdims=True))
        w = jnp.exp(dmat - mt) * _dot_nt(qc, kc)
        decay = jnp.exp(inter - mt)
        st = st_s[...]
        num_aug = decay * _dot_nt(qc, st.astype(BF16)) + _dot_nt(w.astype(BF16), vt_aug)
        num = num_aug[:, :hd]
        den = num_aug[:, hd:hd + 1]
        hv = num / jnp.maximum(jnp.abs(den), jnp.exp(-mt))
        g = d0[L - 1:L, :] + li
        b_last = b_col[L - 1:L, :]
        m_new = jnp.maximum(b_last + m_prev, jnp.max(g, axis=1, keepdims=True))
        a = jnp.exp(g - m_new)
        cd = jnp.exp(b_last + m_prev - m_new)
        st_s[...] = cd * st + _dot((vt_aug.astype(F32) * a).astype(BF16), kc)
        m_s[...] = m_new
        hn = hv * lax.rsqrt(jnp.mean(hv * hv, axis=-1, keepdims=True) + RMS_EPS) * gout_ref[...]
        og = og_ref[pl.ds(start, L), :].astype(F32)
        out_ref[pl.ds(start, L), :] = (_sigmoid(og) * hn).astype(BF16)
        return 0

    lax.fori_loop(0, nchunk, chunk, 0)


def _mlstm(z, vt3, g3, og, conv_w, conv_b, wq, wk, b_i, b_f, g_out, B, S):
    T = z.shape[0]
    L = SEQ_TILE
    per_b = S // L
    hd = MLSTM_DIM
    smem = pl.BlockSpec(memory_space=pltpu.SMEM)
    seq = pl.BlockSpec((S, hd), lambda b, h: (b, h))
    return pl.pallas_call(
        functools.partial(_mlstm_body, S=S),
        out_shape=jax.ShapeDtypeStruct((T, MLSTM_INNER), BF16),
        grid=(B, MLSTM_HEADS),
        in_specs=[
            smem, smem, seq,
            pl.BlockSpec((per_b, hd, L), lambda b, h: (b, h, 0)),
            pl.BlockSpec((per_b, 16, L), lambda b, h: (b, 0, 0)),
            seq,
            pl.BlockSpec((CONV_WIDTH, hd), lambda b, h: (0, h)),
            pl.BlockSpec((1, hd), lambda b, h: (0, h)),
            pl.BlockSpec((None, hd, hd), lambda b, h: (h, 0, 0)),
            pl.BlockSpec((None, hd, hd), lambda b, h: (h, 0, 0)),
            pl.BlockSpec((1, hd), lambda b, h: (0, h)),
        ],
        out_specs=seq,
        scratch_shapes=[
            pltpu.VMEM((S + 8, hd), F32),
            pltpu.VMEM((S, hd), BF16),
            pltpu.VMEM((S, hd), BF16),
            pltpu.VMEM((2 * hd, hd), F32),
            pltpu.VMEM((1, 1), F32),
        ],
        compiler_params=_params(2), name="mlstm",
    )(b_i, b_f, z, vt3, g3, og, conv_w, conv_b, wq, wk, g_out)


def _post_body(ao_ref, mo_ref, x_ref, gt1_ref, gattn_ref, wout_ref, g2_ref, sc2_ref, sh2_ref, gt2_ref,
               wrt_ref, wgu_ref, wds_ref, base_ref, h2_ref, lg_ref):
    ao = ao_ref[...].astype(F32)
    aon = ao * lax.rsqrt(jnp.mean(ao * ao, axis=-1, keepdims=True) + RMS_EPS) * gattn_ref[...]
    mix_in = jnp.concatenate([aon.astype(BF16), mo_ref[...]], axis=1)
    x1 = x_ref[...] + gt1_ref[...] * _dot(mix_in, wout_ref[...])
    hn = x1 * lax.rsqrt(jnp.mean(x1 * x1, axis=-1, keepdims=True) + RMS_EPS) * g2_ref[...]
    h2 = hn * sc2_ref[...] + sh2_ref[...]
    words = _pack_rows(h2)
    h2_ref[0] = words[:, :HALF]
    h2_ref[1] = words[:, HALF:]
    h2b = h2.astype(BF16)
    gu = _dot(h2b, wgu_ref[...])
    act = (_silu(gu[:, :SHARED_DIM]) * gu[:, SHARED_DIM:]).astype(BF16)
    base_ref[...] = x1 + gt2_ref[...] * _dot(act, wds_ref[...])
    lg_ref[...] = _dot_nt(wrt_ref[...], h2b)


def _post(ao, mo, x2, gt1, gattn, wout, g2, sc2p, sh2, gt2, wrt, wgu, wds, S):
    T = x2.shape[0]
    tm = SEQ_TILE
    per_b = S // tm
    full = lambda a: pl.BlockSpec(a.shape, lambda i: (0,) * a.ndim)
    row = lambda w: pl.BlockSpec((tm, w), lambda i: (i, 0))
    bmod = pl.BlockSpec((None, 1, D_MODEL), lambda i: (i // per_b, 0, 0))
    return pl.pallas_call(
        _post_body,
        out_shape=(jax.ShapeDtypeStruct((T, D_MODEL), F32), jax.ShapeDtypeStruct((2, T, HALF), U32),
                   jax.ShapeDtypeStruct((N_EXPERTS, T), F32)),
        grid=(T // tm,),
        in_specs=[row(MLA_HEADS * V_DIM), row(MLSTM_INNER), row(D_MODEL), bmod, full(gattn), full(wout),
                  full(g2), bmod, bmod, bmod, full(wrt), full(wgu), full(wds)],
        out_specs=(row(D_MODEL), pl.BlockSpec((2, tm, HALF), lambda i: (0, i, 0)),
                   pl.BlockSpec((N_EXPERTS, tm), lambda i: (0, i))),
        compiler_params=_params(1), name="post",
    )(ao, mo, x2, gt1, gattn, wout, g2, sc2p, sh2, gt2, wrt, wgu, wds)


def _first_max(x, rows, n):
    mx = jnp.max(x, axis=0, keepdims=True)
    idx = jnp.min(jnp.where(x == mx, rows, n), axis=0, keepdims=True)
    return mx, idx


def _route_body(lg_ref, bias_ref, idx_ref, w_ref, rank_ref, cnt_ref, carry):
    i = pl.program_id(0)
    tt = lg_ref.shape[1]

    @pl.when(i == 0)
    def _():
        carry[...] = jnp.zeros_like(carry)

    sc = _sigmoid(lg_ref[...])
    bi = sc + bias_ref[...]
    rows = lax.broadcasted_iota(I32, (N_EXPERTS, tt), 0)
    rows_g = lax.broadcasted_iota(I32, (GROUP_SIZE, tt), 0)
    rows_n = lax.broadcasted_iota(I32, (N_GROUPS, tt), 0)
    neg = -jnp.inf
    gs = []
    for g in range(N_GROUPS):
        xg = bi[g * GROUP_SIZE:(g + 1) * GROUP_SIZE]
        m1, i1 = _first_max(xg, rows_g, GROUP_SIZE)
        m2 = jnp.max(jnp.where(rows_g == i1, neg, xg), axis=0, keepdims=True)
        gs.append(m1 + m2)
    cur = jnp.concatenate(gs, axis=0)
    keep = jnp.zeros((N_GROUPS, tt), F32)
    for _ in range(TOPK_GROUPS):
        _, gi = _first_max(cur, rows_n, N_GROUPS)
        sel = rows_n == gi
        keep = jnp.where(sel, 1.0, keep)
        cur = jnp.where(sel, neg, cur)
    masked = jnp.concatenate(
        [jnp.where(keep[g:g + 1] > 0.0, bi[g * GROUP_SIZE:(g + 1) * GROUP_SIZE], neg) for g in range(N_GROUPS)], axis=0)
    cur = masked
    onehot = jnp.zeros((N_EXPERTS, tt), F32)
    idxs, ws = [], []
    for _ in range(TOP_K):
        _, ei = _first_max(cur, rows, N_EXPERTS)
        sel = rows == ei
        ws.append(jnp.sum(jnp.where(sel, sc, 0.0), axis=0, keepdims=True))
        idxs.append(ei)
        cur = jnp.where(sel, neg, cur)
        onehot = jnp.where(sel, 1.0, onehot)
    wsum = ws[0]
    for k in range(1, TOP_K):
        wsum = wsum + ws[k]
    idx_ref[...] = jnp.concatenate(idxs, axis=0)
    w_ref[...] = jnp.concatenate([w / wsum * ROUTED_SCALE for w in ws], axis=0)
    r_i = lax.broadcasted_iota(I32, (tt, tt), 0)
    c_i = lax.broadcasted_iota(I32, (tt, tt), 1)
    before = jnp.where(r_i < c_i, 1.0, 0.0).astype(BF16)
    tot = _dot(onehot.astype(BF16), before) + carry[...]
    rank_ref[...] = jnp.concatenate(
        [jnp.sum(jnp.where(rows == ei, tot, 0.0), axis=0, keepdims=True) for ei in idxs], axis=0).astype(I32)
    carry[...] = carry[...] + jnp.sum(onehot, axis=1, keepdims=True)
    cnt_ref[...] = jnp.broadcast_to(carry[...], cnt_ref.shape)


def _route(lgt, bias_col):
    T = lgt.shape[1]
    tt = SEQ_TILE
    blk = pl.BlockSpec((TOP_K, tt), lambda i: (0, i))
    return pl.pallas_call(
        _route_body,
        out_shape=(jax.ShapeDtypeStruct((TOP_K, T), I32), jax.ShapeDtypeStruct((TOP_K, T), F32),
                   jax.ShapeDtypeStruct((TOP_K, T), I32), jax.ShapeDtypeStruct((N_EXPERTS, LANES), F32)),
        grid=(T // tt,),
        in_specs=[pl.BlockSpec((N_EXPERTS, tt), lambda i: (0, i)), pl.BlockSpec((N_EXPERTS, 1), lambda i: (0, 0))],
        out_specs=(blk, blk, blk, pl.BlockSpec((N_EXPERTS, LANES), lambda i: (0, 0))),
        scratch_shapes=[pltpu.VMEM((N_EXPERTS, 1), F32)],
        compiler_params=_params(1), name="route",
    )(lgt, bias_col)


def _dest_body(idx_ref, rank_ref, off_ref, dest_ref):
    tt = idx_ref.shape[1]
    rows = lax.broadcasted_iota(I32, (N_EXPERTS, tt), 0)
    off = off_ref[...]
    outs = []
    for k in range(TOP_K):
        sel = rows == idx_ref[k:k + 1, :]
        outs.append(jnp.sum(jnp.where(sel, off, 0), axis=0, keepdims=True) + rank_ref[k:k + 1, :])
    dest_ref[...] = jnp.concatenate(outs, axis=0)


def _dest(idx_t, rank_t, off_col):
    T = idx_t.shape[1]
    tt = SEQ_TILE
    blk = pl.BlockSpec((TOP_K, tt), lambda i: (0, i))
    return pl.pallas_call(
        _dest_body, out_shape=jax.ShapeDtypeStruct((TOP_K, T), I32), grid=(T // tt,),
        in_specs=[blk, blk, pl.BlockSpec((N_EXPERTS, 1), lambda i: (0, 0))], out_specs=blk,
        compiler_params=_params(1), name="dest",
    )(idx_t, rank_t, off_col)


SC_WINDOW = 128
HALF = PACKED // 2


def _sc_mesh():
    return plsc.VectorSubcoreMesh(core_axis_name="c", subcore_axis_name="s")


def _scatter_rows(dest2_t, rows, n_out):
    n = rows.shape[0]

    @pl.kernel(out_type=jax.ShapeDtypeStruct((n_out, HALF), U32), mesh=_sc_mesh(), scratch_types=[])
    def scatter(d_hbm, x_hbm, o_hbm):
        def body(d_vmem, x_vmem):
            for k in range(TOP_K):
                pltpu.sync_copy(x_vmem, o_hbm.at[d_vmem.at[k]])

        pltpu.emit_pipeline(
            body, grid=(n // SC_WINDOW,),
            in_specs=[pl.BlockSpec((TOP_K, SC_WINDOW), lambda i: (0, i)),
                      pl.BlockSpec((SC_WINDOW, HALF), lambda i: (i, 0))],
            out_specs=[], core_axis_name=("c", "s"), dimension_semantics=(pltpu.PARALLEL,),
        )(d_hbm, x_hbm)

    return scatter(dest2_t, rows)


def _gather_rows(index_row, rows):
    n = index_row.shape[1]

    @pl.kernel(out_type=jax.ShapeDtypeStruct((n, HALF), U32), mesh=_sc_mesh(), scratch_types=[])
    def gather(i_hbm, y_hbm, o_hbm):
        def body(i_vmem, o_vmem):
            pltpu.sync_copy(y_hbm.at[i_vmem.at[0]], o_vmem)

        pltpu.emit_pipeline(
            body, grid=(n // SC_WINDOW,),
            in_specs=[pl.BlockSpec((1, SC_WINDOW), lambda i: (0, i))],
            out_specs=[pl.BlockSpec((SC_WINDOW, HALF), lambda i: (i, 0))],
            core_axis_name=("c", "s"), dimension_semantics=(pltpu.PARALLEL,),
        )(i_hbm, o_hbm)

    return gather(index_row, rows)


EXPERT_RING = 4


def _experts_body(first_ref, nblk_ref, cnt_ref, xs_ref, wg_ref, wu_ref, wd_ref, y_ref,
                  xbuf, ybuf, wgb, wub, wdb, xsem, ysem):
    e = pl.program_id(0)
    last = pl.num_programs(0) - 1
    b0 = first_ref[e]
    nb = nblk_ref[e]
    total = first_ref[last] + nblk_ref[last]
    ring = xbuf.shape[0]

    def rows(g):
        return pl.ds(pl.multiple_of(g * ROW_BLOCK, ROW_BLOCK), ROW_BLOCK)

    def x_copy(g):
        slot = g % ring
        return pltpu.make_async_copy(xs_ref.at[:, rows(g), :], xbuf.at[slot], xsem.at[slot])

    def y_copy(g):
        slot = g % ring
        return pltpu.make_async_copy(ybuf.at[slot], y_ref.at[:, rows(g), :], ysem.at[slot])

    @pl.when(e == 0)
    def _():
        for g in range(ring - 1):
            @pl.when(g < total)
            def _():
                x_copy(g).start()

    @pl.when(nb > 0)
    def _():
        wgb[...] = wg_ref[...].astype(BF16)
        wub[...] = wu_ref[...].astype(BF16)
        wdb[...] = wd_ref[...].astype(BF16)

    def block(i, carry):
        g = b0 + i
        slot = g % ring

        @pl.when(g + ring - 1 < total)
        def _():
            x_copy(g + ring - 1).start()

        x_copy(g).wait()

        @pl.when(g >= ring)
        def _():
            y_copy(g - ring).wait()

        words = jnp.concatenate([xbuf[slot, 0], xbuf[slot, 1]], axis=1)
        n_live = cnt_ref[e] - i * ROW_BLOCK
        words = jnp.where(lax.broadcasted_iota(I32, words.shape, 0) < n_live, words, U32(0))
        x = jnp.concatenate(_unpack_rows(words), axis=1).astype(BF16)
        act = (_silu(_dot(x, wgb[...])) * _dot(x, wub[...])).astype(BF16)
        y = _pack_rows(_dot(act, wdb[...]))
        ybuf[slot, 0] = y[:, :HALF]
        ybuf[slot, 1] = y[:, HALF:]
        y_copy(g).start()
        return carry

    lax.fori_loop(0, nb, block, 0)

    @pl.when(e == last)
    def _():
        for back in range(ring, 0, -1):
            @pl.when(total >= back)
            def _():
                y_copy(total - back).wait()


def _experts(first_block, n_blocks, counts, xs, wg, wu, wd):
    n_rows = xs.shape[1]
    wmap = lambda e, fb, nb, cnt: (e, 0, 0)
    anyspec = pl.BlockSpec(memory_space=pl.ANY)
    ring = pltpu.VMEM((EXPERT_RING, 2, ROW_BLOCK, HALF), U32)
    grid_spec = pltpu.PrefetchScalarGridSpec(
        num_scalar_prefetch=3, grid=(N_EXPERTS,),
        in_specs=[
            anyspec,
            pl.BlockSpec((None, D_MODEL, EXPERT_DIM), wmap),
            pl.BlockSpec((None, D_MODEL, EXPERT_DIM), wmap),
            pl.BlockSpec((None, EXPERT_DIM, D_MODEL), wmap),
        ],
        out_specs=anyspec,
        scratch_shapes=[ring, ring,
                        pltpu.VMEM((D_MODEL, EXPERT_DIM), BF16), pltpu.VMEM((D_MODEL, EXPERT_DIM), BF16),
                        pltpu.VMEM((EXPERT_DIM, D_MODEL), BF16),
                        pltpu.SemaphoreType.DMA((EXPERT_RING,)), pltpu.SemaphoreType.DMA((EXPERT_RING,))],
    )
    return pl.pallas_call(
        _experts_body, out_shape=jax.ShapeDtypeStruct((2, n_rows, HALF), U32), grid_spec=grid_spec,
        compiler_params=_params(1), name="experts",
    )(first_block, n_blocks, counts, xs, wg, wu, wd)


def _combine_body(*refs):
    ya_refs = refs[:TOP_K]
    yb_refs = refs[TOP_K:2 * TOP_K]
    base_ref, w_ref, gt2_ref, out_ref = refs[2 * TOP_K:]
    w = w_ref[...]
    r_lo = r_hi = None
    for k in range(TOP_K):
        lo, hi = _unpack_rows(jnp.concatenate([ya_refs[k][...], yb_refs[k][...]], axis=1))
        wk = w[:, k:k + 1]
        r_lo = wk * lo if r_lo is None else r_lo + wk * lo
        r_hi = wk * hi if r_hi is None else r_hi + wk * hi
    out_ref[...] = base_ref[...] + gt2_ref[...] * jnp.concatenate([r_lo, r_hi], axis=1)


def _combine(y_slots, base, w_tk, gt2, S):
    T = base.shape[0]
    tt = SEQ_TILE
    nt = T // tt
    per_b = S // tt
    yspec = lambda half, k: pl.BlockSpec((tt, HALF), lambda i: ((half * TOP_K + k) * nt + i, 0))
    return pl.pallas_call(
        _combine_body, out_shape=jax.ShapeDtypeStruct((T, D_MODEL), F32), grid=(nt,),
        in_specs=[yspec(half, k) for half in range(2) for k in range(TOP_K)] + [
            pl.BlockSpec((tt, D_MODEL), lambda i: (i, 0)),
            pl.BlockSpec((tt, TOP_K), lambda i: (i, 0)),
            pl.BlockSpec((None, 1, D_MODEL), lambda i: (i // per_b, 0, 0)),
        ],
        out_specs=pl.BlockSpec((tt, D_MODEL), lambda i: (i, 0)),
        compiler_params=_params(1), name="combine",
    )(*([y_slots] * (2 * TOP_K)), base, w_tk, gt2)


def _pad_heads(w, width):
    kdim = w.shape[0]
    w3 = w.reshape(kdim, MLA_HEADS, width)
    return jnp.pad(w3, ((0, 0), (0, 0), (0, LANES - width))).reshape(kdim, MLA_HEADS * LANES)


def _rot_partner(w_rope):
    half = QK_ROPE // 2
    return jnp.concatenate([-w_rope[..., half:], w_rope[..., :half]], axis=-1)


def _lane_vec(nope, rope):
    return jnp.concatenate([nope, rope, jnp.zeros((LANES - QK_DIM,), F32)]).reshape(1, LANES)


def _layer(x2, ada, positions, B, S, g_norm1, w_in, g_cq, w_uq, g_ckv, w_ukv, g_qn, g_kn, g_attn_out,
           conv_w, conv_b, w_mq, w_mk, b_igate, b_fgate, g_mlstm_out, w_out, g_norm2, w_router, router_bias,
           w_gate_exp, w_up_exp, w_down_exp, w_gate_sh, w_up_sh, w_down_sh):
    T = B * S
    sh1, sc1, gt1, sh2, sc2, gt2 = [a.reshape(B, 1, D_MODEL) for a in jnp.split(ada, 6, axis=-1)]
    row = lambda v: v.reshape(1, -1).astype(F32)

    inv_freq = ROPE_THETA ** (-jnp.arange(0, QK_ROPE, 2, dtype=F32) / QK_ROPE)
    ang = positions.astype(F32).reshape(T, 1) * inv_freq
    cos, sin = jnp.cos(ang), jnp.sin(ang)
    pad_r = jnp.zeros((T, LANES - QK_DIM), F32)
    cosp = jnp.concatenate([jnp.ones((T, QK_NOPE), F32), cos, cos, pad_r], axis=1)
    sinp = jnp.concatenate([jnp.zeros((T, QK_NOPE), F32), sin, sin, pad_r], axis=1)

    o0 = 0
    w_cq = w_in[:, o0:o0 + Q_LORA]; o0 += Q_LORA
    w_ckv = w_in[:, o0:o0 + KV_LORA]; o0 += KV_LORA
    w_kr = w_in[:, o0:o0 + QK_ROPE]; o0 += QK_ROPE
    w_z = w_in[:, o0:o0 + MLSTM_INNER]; o0 += MLSTM_INNER
    w_v = w_in[:, o0:o0 + MLSTM_INNER]; o0 += MLSTM_INNER
    w_o = w_in[:, o0:o0 + MLSTM_INNER]; o0 += MLSTM_INNER
    w_i = w_in[:, o0:o0 + MLSTM_HEADS]; o0 += MLSTM_HEADS
    w_f = w_in[:, o0:o0 + MLSTM_HEADS]
    zl = jnp.zeros((D_MODEL, QK_NOPE), F32)
    zr = jnp.zeros((D_MODEL, LANES - QK_DIM), F32)
    wa = jnp.concatenate([w_cq, w_ckv, w_z, w_o, zl, w_kr, zr, zl, _rot_partner(w_kr), zr], axis=1).astype(BF16)
    wt = jnp.concatenate([w_v.T, w_i.T, w_f.T, jnp.zeros((16 - 2 * MLSTM_HEADS, D_MODEL), F32)], axis=0).astype(BF16)

    uq3 = w_uq.reshape(Q_LORA, MLA_HEADS, QK_DIM)
    uq_rot = jnp.concatenate([jnp.zeros((Q_LORA, MLA_HEADS, QK_NOPE), F32), _rot_partner(uq3[..., QK_NOPE:])], axis=-1)
    wuq = _pad_heads(w_uq, QK_DIM).astype(BF16)
    wuqs = _pad_heads(uq_rot.reshape(Q_LORA, MLA_HEADS * QK_DIM), QK_DIM).astype(BF16)
    ukv3 = w_ukv.reshape(KV_LORA, MLA_HEADS, QK_NOPE + V_DIM)
    wuk = _pad_heads(ukv3[..., :QK_NOPE].reshape(KV_LORA, MLA_HEADS * QK_NOPE), QK_NOPE).astype(BF16)
    wuv = ukv3[..., QK_NOPE:].reshape(KV_LORA, MLA_HEADS * V_DIM).astype(BF16)
    gq = _lane_vec(g_qn[:QK_NOPE], g_qn[QK_NOPE:])
    gqs = _lane_vec(jnp.zeros((QK_NOPE,), F32), _rot_partner(g_qn[QK_NOPE:]) * jnp.concatenate(
        [-jnp.ones((QK_ROPE // 2,), F32), jnp.ones((QK_ROPE // 2,), F32)]))
    gk = _lane_vec(g_kn[:QK_NOPE], g_kn[QK_NOPE:])
    gks = _lane_vec(jnp.zeros((QK_NOPE,), F32), _rot_partner(g_kn[QK_NOPE:]) * jnp.concatenate(
        [-jnp.ones((QK_ROPE // 2,), F32), jnp.ones((QK_ROPE // 2,), F32)]))

    z, og, vt3, g3, q, k, mv = _inproj(
        x2, 1.0 + sc1, sh1, row(g_norm1), wa, wt, row(g_cq), wuq, wuqs, row(g_ckv), wuk, wuv,
        gq, gqs, gk, gks, cosp, sinp, S)
    attn_o = _attn(q, k, mv, B, S)
    mlstm_o = _mlstm(z, vt3, g3, og, conv_w.reshape(CONV_WIDTH, MLSTM_INNER), row(conv_b),
                     w_mq.astype(BF16), w_mk.astype(BF16), b_igate, b_fgate, row(g_mlstm_out), B, S)

    wgu = jnp.concatenate([w_gate_sh, w_up_sh], axis=1).astype(BF16)
    base, h2, lgt = _post(attn_o, mlstm_o, x2, gt1, row(g_attn_out), w_out.astype(BF16), row(g_norm2),
                          1.0 + sc2, sh2, gt2, w_router.T.astype(BF16), wgu, w_down_sh.astype(BF16), S)

    idx_t, w_t, rank_t, cnt = _route(lgt, router_bias.reshape(N_EXPERTS, 1))
    counts = cnt[:, 0].astype(I32)
    padded = (counts + ROW_BLOCK - 1) // ROW_BLOCK * ROW_BLOCK
    padded_end = jnp.cumsum(padded)
    off = padded_end - padded
    n_rows = T * TOP_K + N_EXPERTS * ROW_BLOCK
    dest_t = _dest(idx_t, rank_t, off.reshape(N_EXPERTS, 1))

    dest2_t = jnp.concatenate([dest_t, dest_t + n_rows], axis=1)
    xs = _scatter_rows(dest2_t, h2.reshape(2 * T, HALF), 2 * n_rows).reshape(2, n_rows, HALF)
    y = _experts(off // ROW_BLOCK, padded // ROW_BLOCK, counts, xs, w_gate_exp, w_up_exp, w_down_exp)
    dest_row = dest_t.reshape(1, TOP_K * T)
    y_slots = _gather_rows(jnp.concatenate([dest_row, dest_row + n_rows], axis=1), y.reshape(2 * n_rows, HALF))
    return _combine(y_slots, base, w_t.T, gt2, S)


def kernel(x, c, positions, w_ada, b_ada, g_norm1, w_in, g_cq, w_uq, g_ckv, w_ukv, g_qn, g_kn, g_attn_out, conv_w, conv_b, w_mq, w_mk, b_igate, b_fgate, g_mlstm_out, w_out, g_norm2, w_router, router_bias, w_gate_exp, w_up_exp, w_down_exp, w_gate_sh, w_up_sh, w_down_sh):
    B, S, _ = x.shape
    depth = w_ada.shape[0]
    x2 = x.reshape(B * S, D_MODEL)
    for l in range(depth):
        ada = _ada(c, w_ada[l], b_ada[l])
        x2 = _layer(x2, ada, positions, B, S, g_norm1[l], w_in[l], g_cq[l], w_uq[l], g_ckv[l], w_ukv[l], g_qn[l],
                    g_kn[l], g_attn_out[l], conv_w[l], conv_b[l], w_mq[l], w_mk[l], b_igate[l], b_fgate[l],
                    g_mlstm_out[l], w_out[l], g_norm2[l], w_router[l], router_bias[l], w_gate_exp[l], w_up_exp[l],
                    w_down_exp[l], w_gate_sh[l], w_up_sh[l], w_down_sh[l])
    return x2.reshape(B, S, D_MODEL)
```

```python
import functools
import math

import jax
import jax.numpy as jnp
from jax import lax
from jax.experimental import pallas as pl
from jax.experimental.pallas import tpu as pltpu
from jax.experimental.pallas import tpu_sc as plsc

F32 = jnp.float32
BF16 = jnp.bfloat16
I32 = jnp.int32

D_MODEL = 1024
MLA_HEADS = 8
QK_NOPE = 64
QK_ROPE = 32
QK_DIM = QK_NOPE + QK_ROPE
V_DIM = 64
Q_LORA = 384
KV_LORA = 256
ROPE_THETA = 10000.0
MLSTM_HEADS = 4
MLSTM_DIM = 128
MLSTM_INNER = MLSTM_HEADS * MLSTM_DIM
CONV_WIDTH = 4
N_EXPERTS = 256
TOP_K = 8
N_GROUPS = 8
TOPK_GROUPS = 4
GROUP_SIZE = N_EXPERTS // N_GROUPS
EXPERT_DIM = 256
SHARED_DIM = 256
ROUTED_SCALE = 2.5
RMS_EPS = 1e-6

LANES = 128
SEQ_TILE = 256
ROW_BLOCK = 256
VMEM_LIMIT = 56 * 1024 * 1024

_OFF_CQ = 0
_OFF_CKV = _OFF_CQ + Q_LORA
_OFF_Z = _OFF_CKV + KV_LORA
_OFF_O = _OFF_Z + MLSTM_INNER
_OFF_KR = _OFF_O + MLSTM_INNER
_OFF_KRS = _OFF_KR + LANES
_WA_COLS = _OFF_KRS + LANES
_WT_ROWS = MLSTM_INNER + 16


def _params(n_axes, vmem=VMEM_LIMIT):
    return pltpu.CompilerParams(dimension_semantics=("arbitrary",) * n_axes, vmem_limit_bytes=vmem)


def _dot(a, b):
    return jnp.dot(a, b, preferred_element_type=F32)


def _dot_nt(a, b):
    return lax.dot_general(a, b, (((1,), (1,)), ((), ())), preferred_element_type=F32)


def _sigmoid(x):
    return 1.0 / (1.0 + jnp.exp(-x))


def _silu(x):
    return x * _sigmoid(x)


U32 = jnp.uint32
PACKED = D_MODEL // 2


def _pack_rows(x):
    w = x.shape[1] // 2
    lo = lax.bitcast_convert_type(x[:, :w].astype(BF16).astype(F32), U32)
    hi = lax.bitcast_convert_type(x[:, w:].astype(BF16).astype(F32), U32)
    return lax.shift_right_logical(lo, U32(16)) | (hi & U32(0xFFFF0000))


def _unpack_rows(p):
    lo = lax.bitcast_convert_type(lax.shift_left(p, U32(16)), F32)
    hi = lax.bitcast_convert_type(p & U32(0xFFFF0000), F32)
    return lo, hi


def _ada_body(c_ref, w_ref, b_ref, o_ref):
    c = c_ref[...]
    cond = _silu(c).astype(BF16)
    o_ref[...] = _dot(cond, w_ref[...].astype(BF16)) + b_ref[...]


def _ada(c, w_ada, b_ada):
    B = c.shape[0]
    n = w_ada.shape[1]
    tn = 1024
    return pl.pallas_call(
        _ada_body,
        out_shape=jax.ShapeDtypeStruct((B, n), F32),
        grid=(n // tn,),
        in_specs=[
            pl.BlockSpec((B, D_MODEL), lambda j: (0, 0)),
            pl.BlockSpec((D_MODEL, tn), lambda j: (0, j)),
            pl.BlockSpec((1, tn), lambda j: (0, j)),
        ],
        out_specs=pl.BlockSpec((B, tn), lambda j: (0, j)),
        compiler_params=_params(1),
        name="ada",
    )(c, w_ada, b_ada.reshape(1, n))


def _inproj_body(x_ref, sc_ref, sh_ref, g1_ref, wa_ref, wt_ref, gcq_ref, wuq_ref, wuqs_ref, gckv_ref,
                 wuk_ref, wuv_ref, gq_ref, gqs_ref, gk_ref, gks_ref, cos_ref, sin_ref,
                 z_ref, o_ref, vt_ref, gt_ref, q_ref, k_ref, mv_ref):
    x = x_ref[...]
    hn = x * lax.rsqrt(jnp.mean(x * x, axis=-1, keepdims=True) + RMS_EPS) * g1_ref[...]
    hm = (hn * sc_ref[...] + sh_ref[...]).astype(BF16)
    p = _dot(hm, wa_ref[...])
    z_ref[...] = p[:, _OFF_Z:_OFF_Z + MLSTM_INNER].astype(BF16)
    o_ref[...] = p[:, _OFF_O:_OFF_O + MLSTM_INNER].astype(BF16)
    rt = _dot_nt(wt_ref[...], hm)
    vt_ref[...] = rt[:MLSTM_INNER].astype(BF16)
    gt_ref[...] = rt[MLSTM_INNER:]

    cq = p[:, _OFF_CQ:_OFF_CQ + Q_LORA]
    cqn = (cq * lax.rsqrt(jnp.mean(cq * cq, axis=-1, keepdims=True) + RMS_EPS) * gcq_ref[...]).astype(BF16)
    ckv = p[:, _OFF_CKV:_OFF_CKV + KV_LORA]
    ckvn = (ckv * lax.rsqrt(jnp.mean(ckv * ckv, axis=-1, keepdims=True) + RMS_EPS) * gckv_ref[...]).astype(BF16)
    qa = _dot(cqn, wuq_ref[...])
    qb = _dot(cqn, wuqs_ref[...])
    ka = _dot(ckvn, wuk_ref[...])
    mv_ref[...] = _dot(ckvn, wuv_ref[...]).astype(BF16)
    kr = p[:, _OFF_KR:_OFF_KR + LANES]
    krs = p[:, _OFF_KRS:_OFF_KRS + LANES]

    cosp = cos_ref[...]
    sinp = sin_ref[...]
    gqc = gq_ref[...] * cosp
    gqs = gqs_ref[...] * sinp
    gkc = gk_ref[...] * cosp
    k_rot = krs * (gks_ref[...] * sinp)
    kr_ss = jnp.sum(kr * kr, axis=-1, keepdims=True)
    q_scale = (QK_DIM ** -0.5) * math.log2(math.e)
    for h in range(MLA_HEADS):
        sl = slice(h * LANES, (h + 1) * LANES)
        qah = qa[:, sl]
        rq = lax.rsqrt(jnp.sum(qah * qah, axis=-1, keepdims=True) * (1.0 / QK_DIM) + RMS_EPS) * q_scale
        q_ref[:, sl] = ((qah * gqc + qb[:, sl] * gqs) * rq).astype(BF16)
        kah = ka[:, sl]
        rk = lax.rsqrt((jnp.sum(kah * kah, axis=-1, keepdims=True) + kr_ss) * (1.0 / QK_DIM) + RMS_EPS)
        k_ref[:, sl] = (((kah + kr) * gkc + k_rot) * rk).astype(BF16)


def _inproj(x2, sc1p, sh1, g1, wa, wt, gcq, wuq, wuqs, gckv, wuk, wuv, gq, gqs, gk, gks, cosp, sinp, S):
    T = x2.shape[0]
    tm = SEQ_TILE
    nt = T // tm
    per_b = S // tm
    hp = MLA_HEADS * LANES
    full = lambda a: pl.BlockSpec(a.shape, lambda i: (0,) * a.ndim)
    row = lambda w: pl.BlockSpec((tm, w), lambda i: (i, 0))
    bmod = pl.BlockSpec((None, 1, D_MODEL), lambda i: (i // per_b, 0, 0))
    out_shapes = (
        jax.ShapeDtypeStruct((T, MLSTM_INNER), BF16),
        jax.ShapeDtypeStruct((T, MLSTM_INNER), BF16),
        jax.ShapeDtypeStruct((nt, MLSTM_INNER, tm), BF16),
        jax.ShapeDtypeStruct((nt, 16, tm), F32),
        jax.ShapeDtypeStruct((T, hp), BF16),
        jax.ShapeDtypeStruct((T, hp), BF16),
        jax.ShapeDtypeStruct((T, MLA_HEADS * V_DIM), BF16),
    )
    out_specs = (
        row(MLSTM_INNER), row(MLSTM_INNER),
        pl.BlockSpec((None, MLSTM_INNER, tm), lambda i: (i, 0, 0)),
        pl.BlockSpec((None, 16, tm), lambda i: (i, 0, 0)),
        row(hp), row(hp), row(MLA_HEADS * V_DIM),
    )
    in_specs = [row(D_MODEL), bmod, bmod, full(g1), full(wa), full(wt), full(gcq), full(wuq), full(wuqs),
                full(gckv), full(wuk), full(wuv), full(gq), full(gqs), full(gk), full(gks),
                row(LANES), row(LANES)]
    return pl.pallas_call(
        _inproj_body, out_shape=out_shapes, grid=(nt,), in_specs=in_specs, out_specs=out_specs,
        compiler_params=_params(1), name="inproj",
    )(x2, sc1p, sh1, g1, wa, wt, gcq, wuq, wuqs, gckv, wuk, wuv, gq, gqs, gk, gks, cosp, sinp)


ATTN_HEADS_PER_STEP = 4


def _pairs_loop(n, fn, init):
    c = lax.fori_loop(0, n // 2, lambda jj, c: fn(2 * jj + 1, fn(2 * jj, c)), init)
    return lax.fori_loop(0, n % 2, lambda _, c: fn(n - 1, c), c)


def _attn_body(q_ref, k_ref, v_ref, o_ref, s_scr):
    i = pl.program_id(2)
    t = SEQ_TILE
    nh = ATTN_HEADS_PER_STEP
    qry_pos = lax.broadcasted_iota(I32, (t, t), 0)
    key_pos = lax.broadcasted_iota(I32, (t, t), 1)
    causal = key_pos <= qry_pos
    qs = [q_ref[:, hh * LANES:(hh + 1) * LANES] for hh in range(nh)]

    def scores(j, hh, diag):
        kj = k_ref[pl.ds(pl.multiple_of(j * t, t), t), hh * LANES:(hh + 1) * LANES]
        s = _dot_nt(qs[hh], kj)
        return jnp.where(causal, s, -jnp.inf) if diag else s

    def halves(x):
        return x[:, :LANES], x[:, LANES:]

    def max_pass(j, mx, diag):
        out = []
        for hh in range(nh):
            s = scores(j, hh, diag)
            s_scr[j, hh] = s
            lo, hi = halves(s)
            out.append(jnp.maximum(mx[hh], jnp.maximum(lo, hi)))
        return tuple(out)

    mx = tuple(jnp.full((t, LANES), -jnp.inf, F32) for _ in range(nh))
    mx = _pairs_loop(i, lambda j, c: max_pass(j, c, False), mx)
    mx = max_pass(i, mx, True)
    mb = [jnp.broadcast_to(jnp.max(mx[hh], axis=1, keepdims=True), (t, LANES)) for hh in range(nh)]

    def acc_pass(j, carry, diag):
        row0 = pl.multiple_of(j * t, t)
        out = []
        for hh in range(nh):
            l128, acc = carry[hh]
            lo, hi = halves(s_scr[j, hh])
            p_lo = jnp.exp2(lo - mb[hh])
            p_hi = jnp.exp2(hi - mb[hh])
            l128 = l128 + (p_lo + p_hi)
            p = jnp.concatenate([p_lo, p_hi], axis=1).astype(BF16)
            pair = hh // 2
            acc = acc + _dot(p, v_ref[pl.ds(row0, t), pair * LANES:(pair + 1) * LANES])
            out.append((l128, acc))
        return tuple(out)

    init = tuple((jnp.zeros((t, LANES), F32), jnp.zeros((t, LANES), F32)) for _ in range(nh))
    carry = _pairs_loop(i, lambda j, c: acc_pass(j, c, False), init)
    final = acc_pass(i, carry, True)
    outs = [acc / jnp.sum(l128, axis=1, keepdims=True) for (l128, acc) in final]
    first_head = lax.broadcasted_iota(I32, (t, LANES), 1) < V_DIM
    o_ref[...] = jnp.concatenate(
        [jnp.where(first_head, outs[2 * pr], outs[2 * pr + 1]) for pr in range(nh // 2)], axis=1).astype(BF16)


def _attn(q, k, mvt, B, S):
    T = q.shape[0]
    t = SEQ_TILE
    nh = ATTN_HEADS_PER_STEP
    per_b = S // t
    return pl.pallas_call(
        _attn_body,
        out_shape=jax.ShapeDtypeStruct((T, MLA_HEADS * V_DIM), BF16),
        grid=(B, MLA_HEADS // nh, per_b),
        in_specs=[
            pl.BlockSpec((t, nh * LANES), lambda b, hp, i: (b * per_b + i, hp)),
            pl.BlockSpec((S, nh * LANES), lambda b, hp, i: (b, hp)),
            pl.BlockSpec((S, nh * V_DIM), lambda b, hp, i: (b, hp)),
        ],
        out_specs=pl.BlockSpec((t, nh * V_DIM), lambda b, hp, i: (b * per_b + i, hp)),
        scratch_shapes=[pltpu.VMEM((per_b, nh, t, t), F32)],
        compiler_params=_params(3), name="attn",
    )(q, k, mvt)


def _split3(a):
    hi = a.astype(BF16)
    r = a - hi.astype(F32)
    mid = r.astype(BF16)
    lo = (r - mid.astype(F32)).astype(BF16)
    return hi, mid, lo


def _mlstm_body(bi_ref, bf_ref, z_ref, vt_ref, g_ref, og_ref, cw_ref, cb_ref, wq_ref, wk_ref, gout_ref,
                out_ref, zpad, q_s, k_s, st_s, m_s, *, S):
    h = pl.program_id(1)
    L = SEQ_TILE
    nchunk = S // L
    hd = MLSTM_DIM
    zpad[0:8, :] = jnp.zeros((8, hd), F32)
    zpad[8:, :] = z_ref[...].astype(F32)
    zc = cb_ref[...] + jnp.zeros((S, hd), F32)
    for j in range(CONV_WIDTH):
        zc = zc + cw_ref[j:j + 1, :] * zpad[pl.ds(8 - (CONV_WIDTH - 1) + j, S), :]
    zs = _silu(zc).astype(BF16)
    q_s[...] = _dot(zs, wq_ref[...]).astype(BF16)
    k_s[...] = (_dot(zs, wk_ref[...]) * (hd ** -0.5)).astype(BF16)

    st_s[...] = jnp.zeros_like(st_s)
    m_s[...] = jnp.zeros_like(m_s)
    b_i = bi_ref[h]
    b_f = bf_ref[h]
    r_i = lax.broadcasted_iota(I32, (L, L), 0)
    c_i = lax.broadcasted_iota(I32, (L, L), 1)
    tril = r_i >= c_i
    strict_lower = jnp.where(r_i > c_i, 1.0, 0.0).astype(BF16)
    ones_row = jnp.where(lax.broadcasted_iota(I32, (hd, L), 0) == 0, 1.0, 0.0).astype(BF16)

    def chunk(c, _):
        start = pl.multiple_of(c * L, L)
        qc = q_s[pl.ds(start, L), :]
        kc = k_s[pl.ds(start, L), :]
        vt_aug = jnp.concatenate([vt_ref[c], ones_row], axis=0)
        li = g_ref[c, pl.ds(h, 1), :] + b_i
        fp = g_ref[c, pl.ds(MLSTM_HEADS + h, 1), :] + b_f
        lf = jnp.minimum(fp, 0.0) - jnp.log(1.0 + jnp.exp(-jnp.abs(fp)))
        a_mat = jnp.where(tril, lf, 0.0)
        a_hi, a_mid, a_lo = _split3(a_mat)
        d0 = _dot(a_hi, strict_lower) + _dot(a_mid, strict_lower) + _dot(a_lo, strict_lower)
        dmat = jnp.where(tril, d0 + li, -jnp.inf)
        b_col = jnp.sum(a_mat, axis=1, keepdims=True)
        m_prev = m_s[...]
        inter = b_col + m_prev
        mt = jnp.maximum(inter, jnp.max(dmat, axis=1, keepdims=True))
        w = jnp.exp(dmat - mt) * _dot_nt(qc, kc)
        decay = jnp.exp(inter - mt)
        st = st_s[...]
        num_aug = decay * _dot_nt(qc, st.astype(BF16)) + _dot_nt(w.astype(BF16), vt_aug)
        num = num_aug[:, :hd]
        den = num_aug[:, hd:hd + 1]
        hv = num / jnp.maximum(jnp.abs(den), jnp.exp(-mt))
        g = d0[L - 1:L, :] + li
        b_last = b_col[L - 1:L, :]
        m_new = jnp.maximum(b_last + m_prev, jnp.max(g, axis=1, keepdims=True))
        a = jnp.exp(g - m_new)
        cd = jnp.exp(b_last + m_prev - m_new)
        st_s[...] = cd * st + _dot((vt_aug.astype(F32) * a).astype(BF16), kc)
        m_s[...] = m_new
        hn = hv * lax.rsqrt(jnp.mean(hv * hv, axis=-1, keepdims=True) + RMS_EPS) * gout_ref[...]
        og = og_ref[pl.ds(start, L), :].astype(F32)
        out_ref[pl.ds(start, L), :] = (_sigmoid(og) * hn).astype(BF16)
        return 0

    lax.fori_loop(0, nchunk, chunk, 0)


def _mlstm(z, vt3, g3, og, conv_w, conv_b, wq, wk, b_i, b_f, g_out, B, S):
    T = z.shape[0]
    L = SEQ_TILE
    per_b = S // L
    hd = MLSTM_DIM
    smem = pl.BlockSpec(memory_space=pltpu.SMEM)
    seq = pl.BlockSpec((S, hd), lambda b, h: (b, h))
    return pl.pallas_call(
        functools.partial(_mlstm_body, S=S),
        out_shape=jax.ShapeDtypeStruct((T, MLSTM_INNER), BF16),
        grid=(B, MLSTM_HEADS),
        in_specs=[
            smem, smem, seq,
            pl.BlockSpec((per_b, hd, L), lambda b, h: (b, h, 0)),
            pl.BlockSpec((per_b, 16, L), lambda b, h: (b, 0, 0)),
            seq,
            pl.BlockSpec((CONV_WIDTH, hd), lambda b, h: (0, h)),
            pl.BlockSpec((1, hd), lambda b, h: (0, h)),
            pl.BlockSpec((None, hd, hd), lambda b, h: (h, 0, 0)),
            pl.BlockSpec((None, hd, hd), lambda b, h: (h, 0, 0)),
            pl.BlockSpec((1, hd), lambda b, h: (0, h)),
        ],
        out_specs=seq,
        scratch_shapes=[
            pltpu.VMEM((S + 8, hd), F32),
            pltpu.VMEM((S, hd), BF16),
            pltpu.VMEM((S, hd), BF16),
            pltpu.VMEM((2 * hd, hd), F32),
            pltpu.VMEM((1, 1), F32),
        ],
        compiler_params=_params(2), name="mlstm",
    )(b_i, b_f, z, vt3, g3, og, conv_w, conv_b, wq, wk, g_out)


def _post_body(ao_ref, mo_ref, x_ref, gt1_ref, gattn_ref, wout_ref, g2_ref, sc2_ref, sh2_ref, gt2_ref,
               wrt_ref, wgu_ref, wds_ref, base_ref, h2_ref, lg_ref):
    ao = ao_ref[...].astype(F32)
    aon = ao * lax.rsqrt(jnp.mean(ao * ao, axis=-1, keepdims=True) + RMS_EPS) * gattn_ref[...]
    mix_in = jnp.concatenate([aon.astype(BF16), mo_ref[...]], axis=1)
    x1 = x_ref[...] + gt1_ref[...] * _dot(mix_in, wout_ref[...])
    hn = x1 * lax.rsqrt(jnp.mean(x1 * x1, axis=-1, keepdims=True) + RMS_EPS) * g2_ref[...]
    h2 = hn * sc2_ref[...] + sh2_ref[...]
    words = _pack_rows(h2)
    h2_ref[0] = words[:, :HALF]
    h2_ref[1] = words[:, HALF:]
    h2b = h2.astype(BF16)
    gu = _dot(h2b, wgu_ref[...])
    act = (_silu(gu[:, :SHARED_DIM]) * gu[:, SHARED_DIM:]).astype(BF16)
    base_ref[...] = x1 + gt2_ref[...] * _dot(act, wds_ref[...])
    lg_ref[...] = _dot_nt(wrt_ref[...], h2b)


def _post(ao, mo, x2, gt1, gattn, wout, g2, sc2p, sh2, gt2, wrt, wgu, wds, S):
    T = x2.shape[0]
    tm = SEQ_TILE
    per_b = S // tm
    full = lambda a: pl.BlockSpec(a.shape, lambda i: (0,) * a.ndim)
    row = lambda w: pl.BlockSpec((tm, w), lambda i: (i, 0))
    bmod = pl.BlockSpec((None, 1, D_MODEL), lambda i: (i // per_b, 0, 0))
    return pl.pallas_call(
        _post_body,
        out_shape=(jax.ShapeDtypeStruct((T, D_MODEL), F32), jax.ShapeDtypeStruct((2, T, HALF), U32),
                   jax.ShapeDtypeStruct((N_EXPERTS, T), F32)),
        grid=(T // tm,),
        in_specs=[row(MLA_HEADS * V_DIM), row(MLSTM_INNER), row(D_MODEL), bmod, full(gattn), full(wout),
                  full(g2), bmod, bmod, bmod, full(wrt), full(wgu), full(wds)],
        out_specs=(row(D_MODEL), pl.BlockSpec((2, tm, HALF), lambda i: (0, i, 0)),
                   pl.BlockSpec((N_EXPERTS, tm), lambda i: (0, i))),
        compiler_params=_params(1), name="post",
    )(ao, mo, x2, gt1, gattn, wout, g2, sc2p, sh2, gt2, wrt, wgu, wds)


def _first_max(x, rows, n):
    mx = jnp.max(x, axis=0, keepdims=True)
    idx = jnp.min(jnp.where(x == mx, rows, n), axis=0, keepdims=True)
    return mx, idx


def _route_body(lg_ref, bias_ref, idx_ref, w_ref, rank_ref, cnt_ref, carry):
    i = pl.program_id(0)
    tt = lg_ref.shape[1]

    @pl.when(i == 0)
    def _():
        carry[...] = jnp.zeros_like(carry)

    sc = _sigmoid(lg_ref[...])
    bi = sc + bias_ref[...]
    rows = lax.broadcasted_iota(I32, (N_EXPERTS, tt), 0)
    rows_g = lax.broadcasted_iota(I32, (GROUP_SIZE, tt), 0)
    rows_n = lax.broadcasted_iota(I32, (N_GROUPS, tt), 0)
    neg = -jnp.inf
    gs = []
    for g in range(N_GROUPS):
        xg = bi[g * GROUP_SIZE:(g + 1) * GROUP_SIZE]
        m1, i1 = _first_max(xg, rows_g, GROUP_SIZE)
        m2 = jnp.max(jnp.where(rows_g == i1, neg, xg), axis=0, keepdims=True)
        gs.append(m1 + m2)
    cur = jnp.concatenate(gs, axis=0)
    keep = jnp.zeros((N_GROUPS, tt), F32)
    for _ in range(TOPK_GROUPS):
        _, gi = _first_max(cur, rows_n, N_GROUPS)
        sel = rows_n == gi
        keep = jnp.where(sel, 1.0, keep)
        cur = jnp.where(sel, neg, cur)
    masked = jnp.concatenate(
        [jnp.where(keep[g:g + 1] > 0.0, bi[g * GROUP_SIZE:(g + 1) * GROUP_SIZE], neg) for g in range(N_GROUPS)], axis=0)
    cur = masked
    onehot = jnp.zeros((N_EXPERTS, tt), F32)
    idxs, ws = [], []
    for _ in range(TOP_K):
        _, ei = _first_max(cur, rows, N_EXPERTS)
        sel = rows == ei
        ws.append(jnp.sum(jnp.where(sel, sc, 0.0), axis=0, keepdims=True))
        idxs.append(ei)
        cur = jnp.where(sel, neg, cur)
        onehot = jnp.where(sel, 1.0, onehot)
    wsum = ws[0]
    for k in range(1, TOP_K):
        wsum = wsum + ws[k]
    idx_ref[...] = jnp.concatenate(idxs, axis=0)
    w_ref[...] = jnp.concatenate([w / wsum * ROUTED_SCALE for w in ws], axis=0)
    r_i = lax.broadcasted_iota(I32, (tt, tt), 0)
    c_i = lax.broadcasted_iota(I32, (tt, tt), 1)
    before = jnp.where(r_i < c_i, 1.0, 0.0).astype(BF16)
    tot = _dot(onehot.astype(BF16), before) + carry[...]
    rank_ref[...] = jnp.concatenate(
        [jnp.sum(jnp.where(rows == ei, tot, 0.0), axis=0, keepdims=True) for ei in idxs], axis=0).astype(I32)
    carry[...] = carry[...] + jnp.sum(onehot, axis=1, keepdims=True)
    cnt_ref[...] = jnp.broadcast_to(carry[...], cnt_ref.shape)


def _route(lgt, bias_col):
    T = lgt.shape[1]
    tt = SEQ_TILE
    blk = pl.BlockSpec((TOP_K, tt), lambda i: (0, i))
    return pl.pallas_call(
        _route_body,
        out_shape=(jax.ShapeDtypeStruct((TOP_K, T), I32), jax.ShapeDtypeStruct((TOP_K, T), F32),
                   jax.ShapeDtypeStruct((TOP_K, T), I32), jax.ShapeDtypeStruct((N_EXPERTS, LANES), F32)),
        grid=(T // tt,),
        in_specs=[pl.BlockSpec((N_EXPERTS, tt), lambda i: (0, i)), pl.BlockSpec((N_EXPERTS, 1), lambda i: (0, 0))],
        out_specs=(blk, blk, blk, pl.BlockSpec((N_EXPERTS, LANES), lambda i: (0, 0))),
        scratch_shapes=[pltpu.VMEM((N_EXPERTS, 1), F32)],
        compiler_params=_params(1), name="route",
    )(lgt, bias_col)


def _dest_body(idx_ref, rank_ref, off_ref, dest_ref):
    tt = idx_ref.shape[1]
    rows = lax.broadcasted_iota(I32, (N_EXPERTS, tt), 0)
    off = off_ref[...]
    outs = []
    for k in range(TOP_K):
        sel = rows == idx_ref[k:k + 1, :]
        outs.append(jnp.sum(jnp.where(sel, off, 0), axis=0, keepdims=True) + rank_ref[k:k + 1, :])
    dest_ref[...] = jnp.concatenate(outs, axis=0)


def _dest(idx_t, rank_t, off_col):
    T = idx_t.shape[1]
    tt = SEQ_TILE
    blk = pl.BlockSpec((TOP_K, tt), lambda i: (0, i))
    return pl.pallas_call(
        _dest_body, out_shape=jax.ShapeDtypeStruct((TOP_K, T), I32), grid=(T // tt,),
        in_specs=[blk, blk, pl.BlockSpec((N_EXPERTS, 1), lambda i: (0, 0))], out_specs=blk,
        compiler_params=_params(1), name="dest",
    )(idx_t, rank_t, off_col)


SC_WINDOW = 128
HALF = PACKED // 2


def _sc_mesh():
    return plsc.VectorSubcoreMesh(core_axis_name="c", subcore_axis_name="s")


def _scatter_rows(dest2_t, rows, n_out):
    n = rows.shape[0]

    @pl.kernel(out_type=jax.ShapeDtypeStruct((n_out, HALF), U32), mesh=_sc_mesh(), scratch_types=[])
    def scatter(d_hbm, x_hbm, o_hbm):
        def body(d_vmem, x_vmem):
            for k in range(TOP_K):
                pltpu.sync_copy(x_vmem, o_hbm.at[d_vmem.at[k]])

        pltpu.emit_pipeline(
            body, grid=(n // SC_WINDOW,),
            in_specs=[pl.BlockSpec((TOP_K, SC_WINDOW), lambda i: (0, i)),
                      pl.BlockSpec((SC_WINDOW, HALF), lambda i: (i, 0))],
            out_specs=[], core_axis_name=("c", "s"), dimension_semantics=(pltpu.PARALLEL,),
        )(d_hbm, x_hbm)

    return scatter(dest2_t, rows)


def _gather_rows(index_row, rows):
    n = index_row.shape[1]

    @pl.kernel(out_type=jax.ShapeDtypeStruct((n, HALF), U32), mesh=_sc_mesh(), scratch_types=[])
    def gather(i_hbm, y_hbm, o_hbm):
        def body(i_vmem, o_vmem):
            pltpu.sync_copy(y_hbm.at[i_vmem.at[0]], o_vmem)

        pltpu.emit_pipeline(
            body, grid=(n // SC_WINDOW,),
            in_specs=[pl.BlockSpec((1, SC_WINDOW), lambda i: (0, i))],
            out_specs=[pl.BlockSpec((SC_WINDOW, HALF), lambda i: (i, 0))],
            core_axis_name=("c", "s"), dimension_semantics=(pltpu.PARALLEL,),
        )(i_hbm, o_hbm)

    return gather(index_row, rows)


EXPERT_RING = 4


def _experts_body(first_ref, nblk_ref, cnt_ref, xs_ref, wg_ref, wu_ref, wd_ref, y_ref,
                  xbuf, ybuf, wgb, wub, wdb, xsem, ysem):
    e = pl.program_id(0)
    last = pl.num_programs(0) - 1
    b0 = first_ref[e]
    nb = nblk_ref[e]
    total = first_ref[last] + nblk_ref[last]
    ring = xbuf.shape[0]

    def rows(g):
        return pl.ds(pl.multiple_of(g * ROW_BLOCK, ROW_BLOCK), ROW_BLOCK)

    def x_copy(g):
        slot = g % ring
        return pltpu.make_async_copy(xs_ref.at[:, rows(g), :], xbuf.at[slot], xsem.at[slot])

    def y_copy(g):
        slot = g % ring
        return pltpu.make_async_copy(ybuf.at[slot], y_ref.at[:, rows(g), :], ysem.at[slot])

    @pl.when(e == 0)
    def _():
        for g in range(ring - 1):
            @pl.when(g < total)
            def _():
                x_copy(g).start()

    @pl.when(nb > 0)
    def _():
        wgb[...] = wg_ref[...].astype(BF16)
        wub[...] = wu_ref[...].astype(BF16)
        wdb[...] = wd_ref[...].astype(BF16)

    def block(i, carry):
        g = b0 + i
        slot = g % ring

        @pl.when(g + ring - 1 < total)
        def _():
            x_copy(g + ring - 1).start()

        x_copy(g).wait()

        @pl.when(g >= ring)
        def _():
            y_copy(g - ring).wait()

        words = jnp.concatenate([xbuf[slot, 0], xbuf[slot, 1]], axis=1)
        n_live = cnt_ref[e] - i * ROW_BLOCK
        words = jnp.where(lax.broadcasted_iota(I32, words.shape, 0) < n_live, words, U32(0))
        x = jnp.concatenate(_unpack_rows(words), axis=1).astype(BF16)
        act = (_silu(_dot(x, wgb[...])) * _dot(x, wub[...])).astype(BF16)
        y = _pack_rows(_dot(act, wdb[...]))
        ybuf[slot, 0] = y[:, :HALF]
        ybuf[slot, 1] = y[:, HALF:]
        y_copy(g).start()
        return carry

    lax.fori_loop(0, nb, block, 0)

    @pl.when(e == last)
    def _():
        for back in range(ring, 0, -1):
            @pl.when(total >= back)
            def _():
                y_copy(total - back).wait()


def _experts(first_block, n_blocks, counts, xs, wg, wu, wd):
    n_rows = xs.shape[1]
    wmap = lambda e, fb, nb, cnt: (e, 0, 0)
    anyspec = pl.BlockSpec(memory_space=pl.ANY)
    ring = pltpu.VMEM((EXPERT_RING, 2, ROW_BLOCK, HALF), U32)
    grid_spec = pltpu.PrefetchScalarGridSpec(
        num_scalar_prefetch=3, grid=(N_EXPERTS,),
        in_specs=[
            anyspec,
            pl.BlockSpec((None, D_MODEL, EXPERT_DIM), wmap),
            pl.BlockSpec((None, D_MODEL, EXPERT_DIM), wmap),
            pl.BlockSpec((None, EXPERT_DIM, D_MODEL), wmap),
        ],
        out_specs=anyspec,
        scratch_shapes=[ring, ring,
                        pltpu.VMEM((D_MODEL, EXPERT_DIM), BF16), pltpu.VMEM((D_MODEL, EXPERT_DIM), BF16),
                        pltpu.VMEM((EXPERT_DIM, D_MODEL), BF16),
                        pltpu.SemaphoreType.DMA((EXPERT_RING,)), pltpu.SemaphoreType.DMA((EXPERT_RING,))],
    )
    return pl.pallas_call(
        _experts_body, out_shape=jax.ShapeDtypeStruct((2, n_rows, HALF), U32), grid_spec=grid_spec,
        compiler_params=_params(1), name="experts",
    )(first_block, n_blocks, counts, xs, wg, wu, wd)


def _combine_body(*refs):
    ya_refs = refs[:TOP_K]
    yb_refs = refs[TOP_K:2 * TOP_K]
    base_ref, w_ref, gt2_ref, out_ref = refs[2 * TOP_K:]
    w = w_ref[...]
    r_lo = r_hi = None
    for k in range(TOP_K):
        lo, hi = _unpack_rows(jnp.concatenate([ya_refs[k][...], yb_refs[k][...]], axis=1))
        wk = w[:, k:k + 1]
        r_lo = wk * lo if r_lo is None else r_lo + wk * lo
        r_hi = wk * hi if r_hi is None else r_hi + wk * hi
    out_ref[...] = base_ref[...] + gt2_ref[...] * jnp.concatenate([r_lo, r_hi], axis=1)


def _combine(y_slots, base, w_tk, gt2, S):
    T = base.shape[0]
    tt = SEQ_TILE
    nt = T // tt
    per_b = S // tt
    yspec = lambda half, k: pl.BlockSpec((tt, HALF), lambda i: ((half * TOP_K + k) * nt + i, 0))
    return pl.pallas_call(
        _combine_body, out_shape=jax.ShapeDtypeStruct((T, D_MODEL), F32), grid=(nt,),
        in_specs=[yspec(half, k) for half in range(2) for k in range(TOP_K)] + [
            pl.BlockSpec((tt, D_MODEL), lambda i: (i, 0)),
            pl.BlockSpec((tt, TOP_K), lambda i: (i, 0)),
            pl.BlockSpec((None, 1, D_MODEL), lambda i: (i // per_b, 0, 0)),
        ],
        out_specs=pl.BlockSpec((tt, D_MODEL), lambda i: (i, 0)),
        compiler_params=_params(1), name="combine",
    )(*([y_slots] * (2 * TOP_K)), base, w_tk, gt2)


def _pad_heads(w, width):
    kdim = w.shape[0]
    w3 = w.reshape(kdim, MLA_HEADS, width)
    return jnp.pad(w3, ((0, 0), (0, 0), (0, LANES - width))).reshape(kdim, MLA_HEADS * LANES)


def _rot_partner(w_rope):
    half = QK_ROPE // 2
    return jnp.concatenate([-w_rope[..., half:], w_rope[..., :half]], axis=-1)


def _lane_vec(nope, rope):
    return jnp.concatenate([nope, rope, jnp.zeros((LANES - QK_DIM,), F32)]).reshape(1, LANES)


def _layer(x2, ada, positions, B, S, g_norm1, w_in, g_cq, w_uq, g_ckv, w_ukv, g_qn, g_kn, g_attn_out,
           conv_w, conv_b, w_mq, w_mk, b_igate, b_fgate, g_mlstm_out, w_out, g_norm2, w_router, router_bias,
           w_gate_exp, w_up_exp, w_down_exp, w_gate_sh, w_up_sh, w_down_sh):
    T = B * S
    sh1, sc1, gt1, sh2, sc2, gt2 = [a.reshape(B, 1, D_MODEL) for a in jnp.split(ada, 6, axis=-1)]
    row = lambda v: v.reshape(1, -1).astype(F32)

    inv_freq = ROPE_THETA ** (-jnp.arange(0, QK_ROPE, 2, dtype=F32) / QK_ROPE)
    ang = positions.astype(F32).reshape(T, 1) * inv_freq
    cos, sin = jnp.cos(ang), jnp.sin(ang)
    pad_r = jnp.zeros((T, LANES - QK_DIM), F32)
    cosp = jnp.concatenate([jnp.ones((T, QK_NOPE), F32), cos, cos, pad_r], axis=1)
    sinp = jnp.concatenate([jnp.zeros((T, QK_NOPE), F32), sin, sin, pad_r], axis=1)

    o0 = 0
    w_cq = w_in[:, o0:o0 + Q_LORA]; o0 += Q_LORA
    w_ckv = w_in[:, o0:o0 + KV_LORA]; o0 += KV_LORA
    w_kr = w_in[:, o0:o0 + QK_ROPE]; o0 += QK_ROPE
    w_z = w_in[:, o0:o0 + MLSTM_INNER]; o0 += MLSTM_INNER
    w_v = w_in[:, o0:o0 + MLSTM_INNER]; o0 += MLSTM_INNER
    w_o = w_in[:, o0:o0 + MLSTM_INNER]; o0 += MLSTM_INNER
    w_i = w_in[:, o0:o0 + MLSTM_HEADS]; o0 += MLSTM_HEADS
    w_f = w_in[:, o0:o0 + MLSTM_HEADS]
    zl = jnp.zeros((D_MODEL, QK_NOPE), F32)
    zr = jnp.zeros((D_MODEL, LANES - QK_DIM), F32)
    wa = jnp.concatenate([w_cq, w_ckv, w_z, w_o, zl, w_kr, zr, zl, _rot_partner(w_kr), zr], axis=1).astype(BF16)
    wt = jnp.concatenate([w_v.T, w_i.T, w_f.T, jnp.zeros((16 - 2 * MLSTM_HEADS, D_MODEL), F32)], axis=0).astype(BF16)

    uq3 = w_uq.reshape(Q_LORA, MLA_HEADS, QK_DIM)
    uq_rot = jnp.concatenate([jnp.zeros((Q_LORA, MLA_HEADS, QK_NOPE), F32), _rot_partner(uq3[..., QK_NOPE:])], axis=-1)
    wuq = _pad_heads(w_uq, QK_DIM).astype(BF16)
    wuqs = _pad_heads(uq_rot.reshape(Q_LORA, MLA_HEADS * QK_DIM), QK_DIM).astype(BF16)
    ukv3 = w_ukv.reshape(KV_LORA, MLA_HEADS, QK_NOPE + V_DIM)
    wuk = _pad_heads(ukv3[..., :QK_NOPE].reshape(KV_LORA, MLA_HEADS * QK_NOPE), QK_NOPE).astype(BF16)
    wuv = ukv3[..., QK_NOPE:].reshape(KV_LORA, MLA_HEADS * V_DIM).astype(BF16)
    gq = _lane_vec(g_qn[:QK_NOPE], g_qn[QK_NOPE:])
    gqs = _lane_vec(jnp.zeros((QK_NOPE,), F32), _rot_partner(g_qn[QK_NOPE:]) * jnp.concatenate(
        [-jnp.ones((QK_ROPE // 2,), F32), jnp.ones((QK_ROPE // 2,), F32)]))
    gk = _lane_vec(g_kn[:QK_NOPE], g_kn[QK_NOPE:])
    gks = _lane_vec(jnp.zeros((QK_NOPE,), F32), _rot_partner(g_kn[QK_NOPE:]) * jnp.concatenate(
        [-jnp.ones((QK_ROPE // 2,), F32), jnp.ones((QK_ROPE // 2,), F32)]))

    z, og, vt3, g3, q, k, mv = _inproj(
        x2, 1.0 + sc1, sh1, row(g_norm1), wa, wt, row(g_cq), wuq, wuqs, row(g_ckv), wuk, wuv,
        gq, gqs, gk, gks, cosp, sinp, S)
    attn_o = _attn(q, k, mv, B, S)
    mlstm_o = _mlstm(z, vt3, g3, og, conv_w.reshape(CONV_WIDTH, MLSTM_INNER), row(conv_b),
                     w_mq.astype(BF16), w_mk.astype(BF16), b_igate, b_fgate, row(g_mlstm_out), B, S)

    wgu = jnp.concatenate([w_gate_sh, w_up_sh], axis=1).astype(BF16)
    base, h2, lgt = _post(attn_o, mlstm_o, x2, gt1, row(g_attn_out), w_out.astype(BF16), row(g_norm2),
                          1.0 + sc2, sh2, gt2, w_router.T.astype(BF16), wgu, w_down_sh.astype(BF16), S)

    idx_t, w_t, rank_t, cnt = _route(lgt, router_bias.reshape(N_EXPERTS, 1))
    counts = cnt[:, 0].astype(I32)
    padded = (counts + ROW_BLOCK - 1) // ROW_BLOCK * ROW_BLOCK
    padded_end = jnp.cumsum(padded)
    off = padded_end - padded
    n_rows = T * TOP_K + N_EXPERTS * ROW_BLOCK
    dest_t = _dest(idx_t, rank_t, off.reshape(N_EXPERTS, 1))

    dest2_t = jnp.concatenate([dest_t, dest_t + n_rows], axis=1)
    xs = _scatter_rows(dest2_t, h2.reshape(2 * T, HALF), 2 * n_rows).reshape(2, n_rows, HALF)
    y = _experts(off // ROW_BLOCK, padded // ROW_BLOCK, counts, xs, w_gate_exp, w_up_exp, w_down_exp)
    dest_row = dest_t.reshape(1, TOP_K * T)
    y_slots = _gather_rows(jnp.concatenate([dest_row, dest_row + n_rows], axis=1), y.reshape(2 * n_rows, HALF))
    return _combine(y_slots, base, w_t.T, gt2, S)


def kernel(x, c, positions, w_ada, b_ada, g_norm1, w_in, g_cq, w_uq, g_ckv, w_ukv, g_qn, g_kn, g_attn_out, conv_w, conv_b, w_mq, w_mk, b_igate, b_fgate, g_mlstm_out, w_out, g_norm2, w_router, router_bias, w_gate_exp, w_up_exp, w_down_exp, w_gate_sh, w_up_sh, w_down_sh):
    B, S, _ = x.shape
    depth = w_ada.shape[0]
    x2 = x.reshape(B * S, D_MODEL)
    for l in range(depth):
        ada = _ada(c, w_ada[l], b_ada[l])
        x2 = _layer(x2, ada, positions, B, S, g_norm1[l], w_in[l], g_cq[l], w_uq[l], g_ckv[l], w_ukv[l], g_qn[l],
                    g_kn[l], g_attn_out[l], conv_w[l], conv_b[l], w_mq[l], w_mk[l], b_igate[l], b_fgate[l],
                    g_mlstm_out[l], w_out[l], g_norm2[l], w_router[l], router_bias[l], w_gate_exp[l], w_up_exp[l],
                    w_down_exp[l], w_gate_sh[l], w_up_sh[l], w_down_sh[l])
    return x2.reshape(B, S, D_MODEL)
```

```python
import functools
import math

import jax
import jax.numpy as jnp
from jax import lax
from jax.experimental import pallas as pl
from jax.experimental.pallas import tpu as pltpu
from jax.experimental.pallas import tpu_sc as plsc

F32 = jnp.float32
BF16 = jnp.bfloat16
I32 = jnp.int32

D_MODEL = 1024
MLA_HEADS = 8
QK_NOPE = 64
QK_ROPE = 32
QK_DIM = QK_NOPE + QK_ROPE
V_DIM = 64
Q_LORA = 384
KV_LORA = 256
ROPE_THETA = 10000.0
MLSTM_HEADS = 4
MLSTM_DIM = 128
MLSTM_INNER = MLSTM_HEADS * MLSTM_DIM
CONV_WIDTH = 4
N_EXPERTS = 256
TOP_K = 8
N_GROUPS = 8
TOPK_GROUPS = 4
GROUP_SIZE = N_EXPERTS // N_GROUPS
EXPERT_DIM = 256
SHARED_DIM = 256
ROUTED_SCALE = 2.5
RMS_EPS = 1e-6

LANES = 128
SEQ_TILE = 256
ROW_BLOCK = 256
VMEM_LIMIT = 56 * 1024 * 1024

_OFF_CQ = 0
_OFF_CKV = _OFF_CQ + Q_LORA
_OFF_Z = _OFF_CKV + KV_LORA
_OFF_O = _OFF_Z + MLSTM_INNER
_OFF_KR = _OFF_O + MLSTM_INNER
_OFF_KRS = _OFF_KR + LANES
_WA_COLS = _OFF_KRS + LANES
_WT_ROWS = MLSTM_INNER + 16


def _params(n_axes, vmem=VMEM_LIMIT):
    return pltpu.CompilerParams(dimension_semantics=("arbitrary",) * n_axes, vmem_limit_bytes=vmem)


def _dot(a, b):
    return jnp.dot(a, b, preferred_element_type=F32)


def _dot_nt(a, b):
    return lax.dot_general(a, b, (((1,), (1,)), ((), ())), preferred_element_type=F32)


def _sigmoid(x):
    return 1.0 / (1.0 + jnp.exp(-x))


def _silu(x):
    return x * _sigmoid(x)


U32 = jnp.uint32
PACKED = D_MODEL // 2


def _pack_rows(x):
    w = x.shape[1] // 2
    lo = lax.bitcast_convert_type(x[:, :w].astype(BF16).astype(F32), U32)
    hi = lax.bitcast_convert_type(x[:, w:].astype(BF16).astype(F32), U32)
    return lax.shift_right_logical(lo, U32(16)) | (hi & U32(0xFFFF0000))


def _unpack_rows(p):
    lo = lax.bitcast_convert_type(lax.shift_left(p, U32(16)), F32)
    hi = lax.bitcast_convert_type(p & U32(0xFFFF0000), F32)
    return lo, hi


def _ada_body(c_ref, w_ref, b_ref, o_ref):
    c = c_ref[...]
    cond = _silu(c).astype(BF16)
    o_ref[...] = _dot(cond, w_ref[...].astype(BF16)) + b_ref[...]


def _ada(c, w_ada, b_ada):
    B = c.shape[0]
    n = w_ada.shape[1]
    tn = 1024
    return pl.pallas_call(
        _ada_body,
        out_shape=jax.ShapeDtypeStruct((B, n), F32),
        grid=(n // tn,),
        in_specs=[
            pl.BlockSpec((B, D_MODEL), lambda j: (0, 0)),
            pl.BlockSpec((D_MODEL, tn), lambda j: (0, j)),
            pl.BlockSpec((1, tn), lambda j: (0, j)),
        ],
        out_specs=pl.BlockSpec((B, tn), lambda j: (0, j)),
        compiler_params=_params(1),
        name="ada",
    )(c, w_ada, b_ada.reshape(1, n))


def _inproj_body(x_ref, sc_ref, sh_ref, g1_ref, wa_ref, wt_ref, gcq_ref, wuq_ref, wuqs_ref, gckv_ref,
                 wuk_ref, wuv_ref, vone_ref, gq_ref, gqs_ref, gk_ref, gks_ref, cos_ref, sin_ref,
                 z_ref, o_ref, vt_ref, gt_ref, q_ref, k_ref, mv_ref):
    x = x_ref[...]
    hn = x * lax.rsqrt(jnp.mean(x * x, axis=-1, keepdims=True) + RMS_EPS) * g1_ref[...]
    hm = (hn * sc_ref[...] + sh_ref[...]).astype(BF16)
    p = _dot(hm, wa_ref[...])
    z_ref[...] = p[:, _OFF_Z:_OFF_Z + MLSTM_INNER].astype(BF16)
    o_ref[...] = p[:, _OFF_O:_OFF_O + MLSTM_INNER].astype(BF16)
    rt = _dot_nt(wt_ref[...], hm)
    vt_ref[...] = rt[:MLSTM_INNER].astype(BF16)
    gt_ref[...] = rt[MLSTM_INNER:]

    cq = p[:, _OFF_CQ:_OFF_CQ + Q_LORA]
    cqn = (cq * lax.rsqrt(jnp.mean(cq * cq, axis=-1, keepdims=True) + RMS_EPS) * gcq_ref[...]).astype(BF16)
    ckv = p[:, _OFF_CKV:_OFF_CKV + KV_LORA]
    ckvn = (ckv * lax.rsqrt(jnp.mean(ckv * ckv, axis=-1, keepdims=True) + RMS_EPS) * gckv_ref[...]).astype(BF16)
    qa = _dot(cqn, wuq_ref[...])
    qb = _dot(cqn, wuqs_ref[...])
    ka = _dot(ckvn, wuk_ref[...])
    mv_ref[...] = (_dot(ckvn, wuv_ref[...]) + vone_ref[...]).astype(BF16)
    kr = p[:, _OFF_KR:_OFF_KR + LANES]
    krs = p[:, _OFF_KRS:_OFF_KRS + LANES]

    cosp = cos_ref[...]
    sinp = sin_ref[...]
    gqc = gq_ref[...] * cosp
    gqs = gqs_ref[...] * sinp
    gkc = gk_ref[...] * cosp
    k_rot = krs * (gks_ref[...] * sinp)
    kr_ss = jnp.sum(kr * kr, axis=-1, keepdims=True)
    q_scale = (QK_DIM ** -0.5) * math.log2(math.e)
    for h in range(MLA_HEADS):
        sl = slice(h * LANES, (h + 1) * LANES)
        qah = qa[:, sl]
        rq = lax.rsqrt(jnp.sum(qah * qah, axis=-1, keepdims=True) * (1.0 / QK_DIM) + RMS_EPS) * q_scale
        q_ref[:, sl] = ((qah * gqc + qb[:, sl] * gqs) * rq).astype(BF16)
        kah = ka[:, sl]
        rk = lax.rsqrt((jnp.sum(kah * kah, axis=-1, keepdims=True) + kr_ss) * (1.0 / QK_DIM) + RMS_EPS)
        k_ref[:, sl] = (((kah + kr) * gkc + k_rot) * rk).astype(BF16)


def _inproj(x2, sc1p, sh1, g1, wa, wt, gcq, wuq, wuqs, gckv, wuk, wuv, vone, gq, gqs, gk, gks, cosp, sinp, S):
    T = x2.shape[0]
    tm = SEQ_TILE
    nt = T // tm
    per_b = S // tm
    hp = MLA_HEADS * LANES
    full = lambda a: pl.BlockSpec(a.shape, lambda i: (0,) * a.ndim)
    row = lambda w: pl.BlockSpec((tm, w), lambda i: (i, 0))
    bmod = pl.BlockSpec((None, 1, D_MODEL), lambda i: (i // per_b, 0, 0))
    out_shapes = (
        jax.ShapeDtypeStruct((T, MLSTM_INNER), BF16),
        jax.ShapeDtypeStruct((T, MLSTM_INNER), BF16),
        jax.ShapeDtypeStruct((nt, MLSTM_INNER, tm), BF16),
        jax.ShapeDtypeStruct((nt, 16, tm), F32),
        jax.ShapeDtypeStruct((T, hp), BF16),
        jax.ShapeDtypeStruct((T, hp), BF16),
        jax.ShapeDtypeStruct((T, hp), BF16),
    )
    out_specs = (
        row(MLSTM_INNER), row(MLSTM_INNER),
        pl.BlockSpec((None, MLSTM_INNER, tm), lambda i: (i, 0, 0)),
        pl.BlockSpec((None, 16, tm), lambda i: (i, 0, 0)),
        row(hp), row(hp), row(hp),
    )
    in_specs = [row(D_MODEL), bmod, bmod, full(g1), full(wa), full(wt), full(gcq), full(wuq), full(wuqs),
                full(gckv), full(wuk), full(wuv), full(vone), full(gq), full(gqs), full(gk), full(gks),
                row(LANES), row(LANES)]
    return pl.pallas_call(
        _inproj_body, out_shape=out_shapes, grid=(nt,), in_specs=in_specs, out_specs=out_specs,
        compiler_params=_params(1), name="inproj",
    )(x2, sc1p, sh1, g1, wa, wt, gcq, wuq, wuqs, gckv, wuk, wuv, vone, gq, gqs, gk, gks, cosp, sinp)


ATTN_HEADS_PER_STEP = 4


def _pairs_loop(n, fn, init):
    c = lax.fori_loop(0, n // 2, lambda jj, c: fn(2 * jj + 1, fn(2 * jj, c)), init)
    return lax.fori_loop(0, n % 2, lambda _, c: fn(n - 1, c), c)


def _attn_body(q_ref, k_ref, v_ref, o_ref, s_scr):
    i = pl.program_id(2)
    t = SEQ_TILE
    nh = ATTN_HEADS_PER_STEP
    qry_pos = lax.broadcasted_iota(I32, (t, t), 0)
    key_pos = lax.broadcasted_iota(I32, (t, t), 1)
    causal = key_pos <= qry_pos
    qs = [q_ref[:, hh * LANES:(hh + 1) * LANES] for hh in range(nh)]

    def scores(j, hh, diag):
        kj = k_ref[pl.ds(pl.multiple_of(j * t, t), t), hh * LANES:(hh + 1) * LANES]
        s = _dot_nt(qs[hh], kj)
        return jnp.where(causal, s, -jnp.inf) if diag else s

    def halves(x):
        return x[:, :LANES], x[:, LANES:]

    def max_pass(j, mx, diag):
        out = []
        for hh in range(nh):
            s = scores(j, hh, diag)
            s_scr[j, hh] = s
            lo, hi = halves(s)
            out.append(jnp.maximum(mx[hh], jnp.maximum(lo, hi)))
        return tuple(out)

    mx = tuple(jnp.full((t, LANES), -jnp.inf, F32) for _ in range(nh))
    mx = _pairs_loop(i, lambda j, c: max_pass(j, c, False), mx)
    mx = max_pass(i, mx, True)
    mb = [jnp.broadcast_to(jnp.max(mx[hh], axis=1, keepdims=True), (t, LANES)) for hh in range(nh)]

    def acc_pass(j, carry, diag):
        row0 = pl.multiple_of(j * t, t)
        out = []
        for hh in range(nh):
            lo, hi = halves(s_scr[j, hh])
            p = jnp.concatenate([jnp.exp2(lo - mb[hh]), jnp.exp2(hi - mb[hh])], axis=1).astype(BF16)
            out.append(carry[hh] + _dot(p, v_ref[pl.ds(row0, t), hh * LANES:(hh + 1) * LANES]))
        return tuple(out)

    init = tuple(jnp.zeros((t, LANES), F32) for _ in range(nh))
    carry = _pairs_loop(i, lambda j, c: acc_pass(j, c, False), init)
    final = acc_pass(i, carry, True)
    outs = [acc / acc[:, V_DIM:V_DIM + 1] for acc in final]
    first_head = lax.broadcasted_iota(I32, (t, LANES), 1) < V_DIM
    o_ref[...] = jnp.concatenate(
        [jnp.where(first_head, outs[2 * pr], pltpu.roll(outs[2 * pr + 1], V_DIM, axis=1)) for pr in range(nh // 2)],
        axis=1).astype(BF16)


def _attn(q, k, mvt, B, S):
    T = q.shape[0]
    t = SEQ_TILE
    nh = ATTN_HEADS_PER_STEP
    per_b = S // t
    return pl.pallas_call(
        _attn_body,
        out_shape=jax.ShapeDtypeStruct((T, MLA_HEADS * V_DIM), BF16),
        grid=(B, MLA_HEADS // nh, per_b),
        in_specs=[
            pl.BlockSpec((t, nh * LANES), lambda b, hp, i: (b * per_b + i, hp)),
            pl.BlockSpec((S, nh * LANES), lambda b, hp, i: (b, hp)),
            pl.BlockSpec((S, nh * LANES), lambda b, hp, i: (b, hp)),
        ],
        out_specs=pl.BlockSpec((t, nh * V_DIM), lambda b, hp, i: (b * per_b + i, hp)),
        scratch_shapes=[pltpu.VMEM((per_b, nh, t, t), F32)],
        compiler_params=_params(3), name="attn",
    )(q, k, mvt)


def _split3(a):
    hi = a.astype(BF16)
    r = a - hi.astype(F32)
    mid = r.astype(BF16)
    lo = (r - mid.astype(F32)).astype(BF16)
    return hi, mid, lo


def _mlstm_body(bi_ref, bf_ref, z_ref, vt_ref, g_ref, og_ref, cw_ref, cb_ref, wq_ref, wk_ref, gout_ref,
                out_ref, zpad, q_s, k_s, st_s, m_s, *, S):
    h = pl.program_id(1)
    L = SEQ_TILE
    nchunk = S // L
    hd = MLSTM_DIM
    zpad[0:8, :] = jnp.zeros((8, hd), F32)
    zpad[8:, :] = z_ref[...].astype(F32)
    zc = cb_ref[...] + jnp.zeros((S, hd), F32)
    for j in range(CONV_WIDTH):
        zc = zc + cw_ref[j:j + 1, :] * zpad[pl.ds(8 - (CONV_WIDTH - 1) + j, S), :]
    zs = _silu(zc).astype(BF16)
    q_s[...] = _dot(zs, wq_ref[...]).astype(BF16)
    k_s[...] = (_dot(zs, wk_ref[...]) * (hd ** -0.5)).astype(BF16)

    st_s[...] = jnp.zeros_like(st_s)
    m_s[...] = jnp.zeros_like(m_s)
    b_i = bi_ref[h]
    b_f = bf_ref[h]
    r_i = lax.broadcasted_iota(I32, (L, L), 0)
    c_i = lax.broadcasted_iota(I32, (L, L), 1)
    tril = r_i >= c_i
    strict_lower = jnp.where(r_i > c_i, 1.0, 0.0).astype(BF16)
    ones_row = jnp.where(lax.broadcasted_iota(I32, (hd, L), 0) == 0, 1.0, 0.0).astype(BF16)

    def chunk(c, _):
        start = pl.multiple_of(c * L, L)
        qc = q_s[pl.ds(start, L), :]
        kc = k_s[pl.ds(start, L), :]
        vt_aug = jnp.concatenate([vt_ref[c], ones_row], axis=0)
        li = g_ref[c, pl.ds(h, 1), :] + b_i
        fp = g_ref[c, pl.ds(MLSTM_HEADS + h, 1), :] + b_f
        lf = jnp.minimum(fp, 0.0) - jnp.log(1.0 + jnp.exp(-jnp.abs(fp)))
        a_mat = jnp.where(tril, lf, 0.0)
        a_hi, a_mid, a_lo = _split3(a_mat)
        d0 = _dot(a_hi, strict_lower) + _dot(a_mid, strict_lower) + _dot(a_lo, strict_lower)
        dmat = jnp.where(tril, d0 + li, -jnp.inf)
        b_col = jnp.sum(a_mat, axis=1, keepdims=True)
        m_prev = m_s[...]
        inter = b_col + m_prev
        mt = jnp.maximum(inter, jnp.max(dmat, axis=1, keepdims=True))
        w = jnp.exp(dmat - mt) * _dot_nt(qc, kc)
        decay = jnp.exp(inter - mt)
        st = st_s[...]
        num_aug = decay * _dot_nt(qc, st.astype(BF16)) + _dot_nt(w.astype(BF16), vt_aug)
        num = num_aug[:, :hd]
        den = num_aug[:, hd:hd + 1]
        hv = num / jnp.maximum(jnp.abs(den), jnp.exp(-mt))
        g = d0[L - 1:L, :] + li
        b_last = b_col[L - 1:L, :]
        m_new = jnp.maximum(b_last + m_prev, jnp.max(g, axis=1, keepdims=True))
        a = jnp.exp(g - m_new)
        cd = jnp.exp(b_last + m_prev - m_new)
        st_s[...] = cd * st + _dot((vt_aug.astype(F32) * a).astype(BF16), kc)
        m_s[...] = m_new
        hn = hv * lax.rsqrt(jnp.mean(hv * hv, axis=-1, keepdims=True) + RMS_EPS) * gout_ref[...]
        og = og_ref[pl.ds(start, L), :].astype(F32)
        out_ref[pl.ds(start, L), :] = (_sigmoid(og) * hn).astype(BF16)
        return 0

    lax.fori_loop(0, nchunk, chunk, 0)


def _mlstm(z, vt3, g3, og, conv_w, conv_b, wq, wk, b_i, b_f, g_out, B, S):
    T = z.shape[0]
    L = SEQ_TILE
    per_b = S // L
    hd = MLSTM_DIM
    smem = pl.BlockSpec(memory_space=pltpu.SMEM)
    seq = pl.BlockSpec((S, hd), lambda b, h: (b, h))
    return pl.pallas_call(
        functools.partial(_mlstm_body, S=S),
        out_shape=jax.ShapeDtypeStruct((T, MLSTM_INNER), BF16),
        grid=(B, MLSTM_HEADS),
        in_specs=[
            smem, smem, seq,
            pl.BlockSpec((per_b, hd, L), lambda b, h: (b, h, 0)),
            pl.BlockSpec((per_b, 16, L), lambda b, h: (b, 0, 0)),
            seq,
            pl.BlockSpec((CONV_WIDTH, hd), lambda b, h: (0, h)),
            pl.BlockSpec((1, hd), lambda b, h: (0, h)),
            pl.BlockSpec((None, hd, hd), lambda b, h: (h, 0, 0)),
            pl.BlockSpec((None, hd, hd), lambda b, h: (h, 0, 0)),
            pl.BlockSpec((1, hd), lambda b, h: (0, h)),
        ],
        out_specs=seq,
        scratch_shapes=[
            pltpu.VMEM((S + 8, hd), F32),
            pltpu.VMEM((S, hd), BF16),
            pltpu.VMEM((S, hd), BF16),
            pltpu.VMEM((2 * hd, hd), F32),
            pltpu.VMEM((1, 1), F32),
        ],
        compiler_params=_params(2), name="mlstm",
    )(b_i, b_f, z, vt3, g3, og, conv_w, conv_b, wq, wk, g_out)


def _post_body(ao_ref, mo_ref, x_ref, gt1_ref, gattn_ref, wout_ref, g2_ref, sc2_ref, sh2_ref, gt2_ref,
               wrt_ref, wgu_ref, wds_ref, base_ref, h2_ref, lg_ref):
    ao = ao_ref[...].astype(F32)
    aon = ao * lax.rsqrt(jnp.mean(ao * ao, axis=-1, keepdims=True) + RMS_EPS) * gattn_ref[...]
    mix_in = jnp.concatenate([aon.astype(BF16), mo_ref[...]], axis=1)
    x1 = x_ref[...] + gt1_ref[...] * _dot(mix_in, wout_ref[...])
    hn = x1 * lax.rsqrt(jnp.mean(x1 * x1, axis=-1, keepdims=True) + RMS_EPS) * g2_ref[...]
    h2 = hn * sc2_ref[...] + sh2_ref[...]
    words = _pack_rows(h2)
    h2_ref[0] = words[:, :HALF]
    h2_ref[1] = words[:, HALF:]
    h2b = h2.astype(BF16)
    gu = _dot(h2b, wgu_ref[...])
    act = (_silu(gu[:, :SHARED_DIM]) * gu[:, SHARED_DIM:]).astype(BF16)
    base_ref[...] = x1 + gt2_ref[...] * _dot(act, wds_ref[...])
    lg_ref[...] = _dot_nt(wrt_ref[...], h2b)


def _post(ao, mo, x2, gt1, gattn, wout, g2, sc2p, sh2, gt2, wrt, wgu, wds, S):
    T = x2.shape[0]
    tm = SEQ_TILE
    per_b = S // tm
    full = lambda a: pl.BlockSpec(a.shape, lambda i: (0,) * a.ndim)
    row = lambda w: pl.BlockSpec((tm, w), lambda i: (i, 0))
    bmod = pl.BlockSpec((None, 1, D_MODEL), lambda i: (i // per_b, 0, 0))
    return pl.pallas_call(
        _post_body,
        out_shape=(jax.ShapeDtypeStruct((T, D_MODEL), F32), jax.ShapeDtypeStruct((2, T, HALF), U32),
                   jax.ShapeDtypeStruct((N_EXPERTS, T), F32)),
        grid=(T // tm,),
        in_specs=[row(MLA_HEADS * V_DIM), row(MLSTM_INNER), row(D_MODEL), bmod, full(gattn), full(wout),
                  full(g2), bmod, bmod, bmod, full(wrt), full(wgu), full(wds)],
        out_specs=(row(D_MODEL), pl.BlockSpec((2, tm, HALF), lambda i: (0, i, 0)),
                   pl.BlockSpec((N_EXPERTS, tm), lambda i: (0, i))),
        compiler_params=_params(1), name="post",
    )(ao, mo, x2, gt1, gattn, wout, g2, sc2p, sh2, gt2, wrt, wgu, wds)


def _first_max(x, rows, n):
    mx = jnp.max(x, axis=0, keepdims=True)
    idx = jnp.min(jnp.where(x == mx, rows, n), axis=0, keepdims=True)
    return mx, idx


def _route_body(lg_ref, bias_ref, idx_ref, w_ref, rank_ref, cnt_ref, carry):
    i = pl.program_id(0)
    tt = lg_ref.shape[1]

    @pl.when(i == 0)
    def _():
        carry[...] = jnp.zeros_like(carry)

    sc = _sigmoid(lg_ref[...])
    bi = sc + bias_ref[...]
    rows = lax.broadcasted_iota(I32, (N_EXPERTS, tt), 0)
    rows_g = lax.broadcasted_iota(I32, (GROUP_SIZE, tt), 0)
    rows_n = lax.broadcasted_iota(I32, (N_GROUPS, tt), 0)
    neg = -jnp.inf
    gs = []
    for g in range(N_GROUPS):
        xg = bi[g * GROUP_SIZE:(g + 1) * GROUP_SIZE]
        m1, i1 = _first_max(xg, rows_g, GROUP_SIZE)
        m2 = jnp.max(jnp.where(rows_g == i1, neg, xg), axis=0, keepdims=True)
        gs.append(m1 + m2)
    cur = jnp.concatenate(gs, axis=0)
    keep = jnp.zeros((N_GROUPS, tt), F32)
    for _ in range(TOPK_GROUPS):
        _, gi = _first_max(cur, rows_n, N_GROUPS)
        sel = rows_n == gi
        keep = jnp.where(sel, 1.0, keep)
        cur = jnp.where(sel, neg, cur)
    masked = jnp.concatenate(
        [jnp.where(keep[g:g + 1] > 0.0, bi[g * GROUP_SIZE:(g + 1) * GROUP_SIZE], neg) for g in range(N_GROUPS)], axis=0)
    cur = masked
    onehot = jnp.zeros((N_EXPERTS, tt), F32)
    idxs, ws = [], []
    for _ in range(TOP_K):
        _, ei = _first_max(cur, rows, N_EXPERTS)
        sel = rows == ei
        ws.append(jnp.sum(jnp.where(sel, sc, 0.0), axis=0, keepdims=True))
        idxs.append(ei)
        cur = jnp.where(sel, neg, cur)
        onehot = jnp.where(sel, 1.0, onehot)
    wsum = ws[0]
    for k in range(1, TOP_K):
        wsum = wsum + ws[k]
    idx_ref[...] = jnp.concatenate(idxs, axis=0)
    w_ref[...] = jnp.concatenate([w / wsum * ROUTED_SCALE for w in ws], axis=0)
    r_i = lax.broadcasted_iota(I32, (tt, tt), 0)
    c_i = lax.broadcasted_iota(I32, (tt, tt), 1)
    before = jnp.where(r_i < c_i, 1.0, 0.0).astype(BF16)
    tot = _dot(onehot.astype(BF16), before) + carry[...]
    rank_ref[...] = jnp.concatenate(
        [jnp.sum(jnp.where(rows == ei, tot, 0.0), axis=0, keepdims=True) for ei in idxs], axis=0).astype(I32)
    carry[...] = carry[...] + jnp.sum(onehot, axis=1, keepdims=True)
    cnt_ref[...] = jnp.broadcast_to(carry[...], cnt_ref.shape)


def _route(lgt, bias_col):
    T = lgt.shape[1]
    tt = SEQ_TILE
    blk = pl.BlockSpec((TOP_K, tt), lambda i: (0, i))
    return pl.pallas_call(
        _route_body,
        out_shape=(jax.ShapeDtypeStruct((TOP_K, T), I32), jax.ShapeDtypeStruct((TOP_K, T), F32),
                   jax.ShapeDtypeStruct((TOP_K, T), I32), jax.ShapeDtypeStruct((N_EXPERTS, LANES), F32)),
        grid=(T // tt,),
        in_specs=[pl.BlockSpec((N_EXPERTS, tt), lambda i: (0, i)), pl.BlockSpec((N_EXPERTS, 1), lambda i: (0, 0))],
        out_specs=(blk, blk, blk, pl.BlockSpec((N_EXPERTS, LANES), lambda i: (0, 0))),
        scratch_shapes=[pltpu.VMEM((N_EXPERTS, 1), F32)],
        compiler_params=_params(1), name="route",
    )(lgt, bias_col)


def _dest_body(idx_ref, rank_ref, off_ref, dest_ref):
    tt = idx_ref.shape[1]
    rows = lax.broadcasted_iota(I32, (N_EXPERTS, tt), 0)
    off = off_ref[...]
    outs = []
    for k in range(TOP_K):
        sel = rows == idx_ref[k:k + 1, :]
        outs.append(jnp.sum(jnp.where(sel, off, 0), axis=0, keepdims=True) + rank_ref[k:k + 1, :])
    dest_ref[...] = jnp.concatenate(outs, axis=0)


def _dest(idx_t, rank_t, off_col):
    T = idx_t.shape[1]
    tt = min(4 * SEQ_TILE, T)
    blk = pl.BlockSpec((TOP_K, tt), lambda i: (0, i))
    return pl.pallas_call(
        _dest_body, out_shape=jax.ShapeDtypeStruct((TOP_K, T), I32), grid=(T // tt,),
        in_specs=[blk, blk, pl.BlockSpec((N_EXPERTS, 1), lambda i: (0, 0))], out_specs=blk,
        compiler_params=_params(1), name="dest",
    )(idx_t, rank_t, off_col)


SC_WINDOW = 128
HALF = PACKED // 2


def _sc_mesh():
    return plsc.VectorSubcoreMesh(core_axis_name="c", subcore_axis_name="s")


def _scatter_rows(dest2_t, rows, n_out):
    n = rows.shape[0]

    @pl.kernel(out_type=jax.ShapeDtypeStruct((n_out, HALF), U32), mesh=_sc_mesh(), scratch_types=[])
    def scatter(d_hbm, x_hbm, o_hbm):
        def body(d_vmem, x_vmem):
            for k in range(TOP_K):
                pltpu.sync_copy(x_vmem, o_hbm.at[d_vmem.at[k]])

        pltpu.emit_pipeline(
            body, grid=(n // SC_WINDOW,),
            in_specs=[pl.BlockSpec((TOP_K, SC_WINDOW), lambda i: (0, i)),
                      pl.BlockSpec((SC_WINDOW, HALF), lambda i: (i, 0))],
            out_specs=[], core_axis_name=("c", "s"), dimension_semantics=(pltpu.PARALLEL,),
        )(d_hbm, x_hbm)

    return scatter(dest2_t, rows)


def _gather_rows(index_row, rows):
    n = index_row.shape[1]

    @pl.kernel(out_type=jax.ShapeDtypeStruct((n, HALF), U32), mesh=_sc_mesh(), scratch_types=[])
    def gather(i_hbm, y_hbm, o_hbm):
        def body(i_vmem, o_vmem):
            pltpu.sync_copy(y_hbm.at[i_vmem.at[0]], o_vmem)

        pltpu.emit_pipeline(
            body, grid=(n // SC_WINDOW,),
            in_specs=[pl.BlockSpec((1, SC_WINDOW), lambda i: (0, i))],
            out_specs=[pl.BlockSpec((SC_WINDOW, HALF), lambda i: (i, 0))],
            core_axis_name=("c", "s"), dimension_semantics=(pltpu.PARALLEL,),
        )(i_hbm, o_hbm)

    return gather(index_row, rows)


EXPERT_RING = 4


def _experts_body(first_ref, nblk_ref, cnt_ref, xs_ref, wg_ref, wu_ref, wd_ref, y_ref,
                  xbuf, ybuf, wgb, wub, wdb, xsem, ysem):
    e = pl.program_id(0)
    last = pl.num_programs(0) - 1
    b0 = first_ref[e]
    nb = nblk_ref[e]
    total = first_ref[last] + nblk_ref[last]
    ring = xbuf.shape[0]

    def rows(g):
        return pl.ds(pl.multiple_of(g * ROW_BLOCK, ROW_BLOCK), ROW_BLOCK)

    def x_copy(g):
        slot = g % ring
        return pltpu.make_async_copy(xs_ref.at[:, rows(g), :], xbuf.at[slot], xsem.at[slot])

    def y_copy(g):
        slot = g % ring
        return pltpu.make_async_copy(ybuf.at[slot], y_ref.at[:, rows(g), :], ysem.at[slot])

    @pl.when(e == 0)
    def _():
        for g in range(ring - 1):
            @pl.when(g < total)
            def _():
                x_copy(g).start()

    @pl.when(nb > 0)
    def _():
        wgb[...] = wg_ref[...].astype(BF16)
        wub[...] = wu_ref[...].astype(BF16)
        wdb[...] = wd_ref[...].astype(BF16)

    def block(i, carry):
        g = b0 + i
        slot = g % ring

        @pl.when(g + ring - 1 < total)
        def _():
            x_copy(g + ring - 1).start()

        x_copy(g).wait()

        @pl.when(g >= ring)
        def _():
            y_copy(g - ring).wait()

        words = jnp.concatenate([xbuf[slot, 0], xbuf[slot, 1]], axis=1)
        n_live = cnt_ref[e] - i * ROW_BLOCK
        words = jnp.where(lax.broadcasted_iota(I32, words.shape, 0) < n_live, words, U32(0))
        x = jnp.concatenate(_unpack_rows(words), axis=1).astype(BF16)
        act = (_silu(_dot(x, wgb[...])) * _dot(x, wub[...])).astype(BF16)
        y = _pack_rows(_dot(act, wdb[...]))
        ybuf[slot, 0] = y[:, :HALF]
        ybuf[slot, 1] = y[:, HALF:]
        y_copy(g).start()
        return carry

    lax.fori_loop(0, nb, block, 0)

    @pl.when(e == last)
    def _():
        for back in range(ring, 0, -1):
            @pl.when(total >= back)
            def _():
                y_copy(total - back).wait()


def _experts(first_block, n_blocks, counts, xs, wg, wu, wd):
    n_rows = xs.shape[1]
    wmap = lambda e, fb, nb, cnt: (e, 0, 0)
    anyspec = pl.BlockSpec(memory_space=pl.ANY)
    ring = pltpu.VMEM((EXPERT_RING, 2, ROW_BLOCK, HALF), U32)
    grid_spec = pltpu.PrefetchScalarGridSpec(
        num_scalar_prefetch=3, grid=(N_EXPERTS,),
        in_specs=[
            anyspec,
            pl.BlockSpec((None, D_MODEL, EXPERT_DIM), wmap),
            pl.BlockSpec((None, D_MODEL, EXPERT_DIM), wmap),
            pl.BlockSpec((None, EXPERT_DIM, D_MODEL), wmap),
        ],
        out_specs=anyspec,
        scratch_shapes=[ring, ring,
                        pltpu.VMEM((D_MODEL, EXPERT_DIM), BF16), pltpu.VMEM((D_MODEL, EXPERT_DIM), BF16),
                        pltpu.VMEM((EXPERT_DIM, D_MODEL), BF16),
                        pltpu.SemaphoreType.DMA((EXPERT_RING,)), pltpu.SemaphoreType.DMA((EXPERT_RING,))],
    )
    return pl.pallas_call(
        _experts_body, out_shape=jax.ShapeDtypeStruct((2, n_rows, HALF), U32), grid_spec=grid_spec,
        compiler_params=_params(1), name="experts",
    )(first_block, n_blocks, counts, xs, wg, wu, wd)


def _combine_body(*refs):
    ya_refs = refs[:TOP_K]
    yb_refs = refs[TOP_K:2 * TOP_K]
    base_ref, w_ref, gt2_ref, out_ref = refs[2 * TOP_K:]
    w = w_ref[...]
    r_lo = r_hi = None
    for k in range(TOP_K):
        lo, hi = _unpack_rows(jnp.concatenate([ya_refs[k][...], yb_refs[k][...]], axis=1))
        wk = w[:, k:k + 1]
        r_lo = wk * lo if r_lo is None else r_lo + wk * lo
        r_hi = wk * hi if r_hi is None else r_hi + wk * hi
    out_ref[...] = base_ref[...] + gt2_ref[...] * jnp.concatenate([r_lo, r_hi], axis=1)


def _combine(y_slots, base, w_tk, gt2, S):
    T = base.shape[0]
    tt = SEQ_TILE
    nt = T // tt
    per_b = S // tt
    yspec = lambda half, k: pl.BlockSpec((tt, HALF), lambda i: ((half * TOP_K + k) * nt + i, 0))
    return pl.pallas_call(
        _combine_body, out_shape=jax.ShapeDtypeStruct((T, D_MODEL), F32), grid=(nt,),
        in_specs=[yspec(half, k) for half in range(2) for k in range(TOP_K)] + [
            pl.BlockSpec((tt, D_MODEL), lambda i: (i, 0)),
            pl.BlockSpec((tt, TOP_K), lambda i: (i, 0)),
            pl.BlockSpec((None, 1, D_MODEL), lambda i: (i // per_b, 0, 0)),
        ],
        out_specs=pl.BlockSpec((tt, D_MODEL), lambda i: (i, 0)),
        compiler_params=_params(1), name="combine",
    )(*([y_slots] * (2 * TOP_K)), base, w_tk, gt2)


def _pad_heads(w, width):
    kdim = w.shape[0]
    w3 = w.reshape(kdim, MLA_HEADS, width)
    return jnp.pad(w3, ((0, 0), (0, 0), (0, LANES - width))).reshape(kdim, MLA_HEADS * LANES)


def _rot_partner(w_rope):
    half = QK_ROPE // 2
    return jnp.concatenate([-w_rope[..., half:], w_rope[..., :half]], axis=-1)


def _lane_vec(nope, rope):
    return jnp.concatenate([nope, rope, jnp.zeros((LANES - QK_DIM,), F32)]).reshape(1, LANES)


def _layer(x2, ada, positions, B, S, g_norm1, w_in, g_cq, w_uq, g_ckv, w_ukv, g_qn, g_kn, g_attn_out,
           conv_w, conv_b, w_mq, w_mk, b_igate, b_fgate, g_mlstm_out, w_out, g_norm2, w_router, router_bias,
           w_gate_exp, w_up_exp, w_down_exp, w_gate_sh, w_up_sh, w_down_sh):
    T = B * S
    sh1, sc1, gt1, sh2, sc2, gt2 = [a.reshape(B, 1, D_MODEL) for a in jnp.split(ada, 6, axis=-1)]
    row = lambda v: v.reshape(1, -1).astype(F32)

    inv_freq = ROPE_THETA ** (-jnp.arange(0, QK_ROPE, 2, dtype=F32) / QK_ROPE)
    ang = positions.astype(F32).reshape(T, 1) * inv_freq
    half = QK_ROPE // 2
    freq_lane = jnp.arange(half)[:, None]
    lane = jnp.arange(LANES)[None, :]
    place = ((lane == QK_NOPE + freq_lane) | (lane == QK_NOPE + half + freq_lane)).astype(F32)
    spread = lambda tbl: jnp.dot(tbl, place, precision=lax.Precision.HIGHEST)
    cosp = (jnp.arange(LANES) < QK_NOPE).astype(F32)[None, :] + spread(jnp.cos(ang))
    sinp = spread(jnp.sin(ang))

    o0 = 0
    w_cq = w_in[:, o0:o0 + Q_LORA]; o0 += Q_LORA
    w_ckv = w_in[:, o0:o0 + KV_LORA]; o0 += KV_LORA
    w_kr = w_in[:, o0:o0 + QK_ROPE]; o0 += QK_ROPE
    w_z = w_in[:, o0:o0 + MLSTM_INNER]; o0 += MLSTM_INNER
    w_v = w_in[:, o0:o0 + MLSTM_INNER]; o0 += MLSTM_INNER
    w_o = w_in[:, o0:o0 + MLSTM_INNER]; o0 += MLSTM_INNER
    w_i = w_in[:, o0:o0 + MLSTM_HEADS]; o0 += MLSTM_HEADS
    w_f = w_in[:, o0:o0 + MLSTM_HEADS]
    zl = jnp.zeros((D_MODEL, QK_NOPE), F32)
    zr = jnp.zeros((D_MODEL, LANES - QK_DIM), F32)
    wa = jnp.concatenate([w_cq, w_ckv, w_z, w_o, zl, w_kr, zr, zl, _rot_partner(w_kr), zr], axis=1).astype(BF16)
    wt = jnp.concatenate([w_v.T, w_i.T, w_f.T, jnp.zeros((16 - 2 * MLSTM_HEADS, D_MODEL), F32)], axis=0).astype(BF16)

    uq3 = w_uq.reshape(Q_LORA, MLA_HEADS, QK_DIM)
    uq_rot = jnp.concatenate([jnp.zeros((Q_LORA, MLA_HEADS, QK_NOPE), F32), _rot_partner(uq3[..., QK_NOPE:])], axis=-1)
    wuq = _pad_heads(w_uq, QK_DIM).astype(BF16)
    wuqs = _pad_heads(uq_rot.reshape(Q_LORA, MLA_HEADS * QK_DIM), QK_DIM).astype(BF16)
    ukv3 = w_ukv.reshape(KV_LORA, MLA_HEADS, QK_NOPE + V_DIM)
    wuk = _pad_heads(ukv3[..., :QK_NOPE].reshape(KV_LORA, MLA_HEADS * QK_NOPE), QK_NOPE).astype(BF16)
    wuv = _pad_heads(ukv3[..., QK_NOPE:].reshape(KV_LORA, MLA_HEADS * V_DIM), V_DIM).astype(BF16)
    vone = jnp.tile((jnp.arange(LANES) == V_DIM).astype(F32), MLA_HEADS).reshape(1, MLA_HEADS * LANES)
    gq = _lane_vec(g_qn[:QK_NOPE], g_qn[QK_NOPE:])
    gqs = _lane_vec(jnp.zeros((QK_NOPE,), F32), _rot_partner(g_qn[QK_NOPE:]) * jnp.concatenate(
        [-jnp.ones((QK_ROPE // 2,), F32), jnp.ones((QK_ROPE // 2,), F32)]))
    gk = _lane_vec(g_kn[:QK_NOPE], g_kn[QK_NOPE:])
    gks = _lane_vec(jnp.zeros((QK_NOPE,), F32), _rot_partner(g_kn[QK_NOPE:]) * jnp.concatenate(
        [-jnp.ones((QK_ROPE // 2,), F32), jnp.ones((QK_ROPE // 2,), F32)]))

    z, og, vt3, g3, q, k, mv = _inproj(
        x2, 1.0 + sc1, sh1, row(g_norm1), wa, wt, row(g_cq), wuq, wuqs, row(g_ckv), wuk, wuv, vone,
        gq, gqs, gk, gks, cosp, sinp, S)
    attn_o = _attn(q, k, mv, B, S)
    mlstm_o = _mlstm(z, vt3, g3, og, conv_w.reshape(CONV_WIDTH, MLSTM_INNER), row(conv_b),
                     w_mq.astype(BF16), w_mk.astype(BF16), b_igate, b_fgate, row(g_mlstm_out), B, S)

    wgu = jnp.concatenate([w_gate_sh, w_up_sh], axis=1).astype(BF16)
    base, h2, lgt = _post(attn_o, mlstm_o, x2, gt1, row(g_attn_out), w_out.astype(BF16), row(g_norm2),
                          1.0 + sc2, sh2, gt2, w_router.T.astype(BF16), wgu, w_down_sh.astype(BF16), S)

    idx_t, w_t, rank_t, cnt = _route(lgt, router_bias.reshape(N_EXPERTS, 1))
    counts = cnt[:, 0].astype(I32)
    padded = (counts + ROW_BLOCK - 1) // ROW_BLOCK * ROW_BLOCK
    padded_end = jnp.cumsum(padded)
    off = padded_end - padded
    n_rows = T * TOP_K + N_EXPERTS * ROW_BLOCK
    dest_t = _dest(idx_t, rank_t, off.reshape(N_EXPERTS, 1))

    dest2_t = jnp.concatenate([dest_t, dest_t + n_rows], axis=1)
    xs = _scatter_rows(dest2_t, h2.reshape(2 * T, HALF), 2 * n_rows).reshape(2, n_rows, HALF)
    y = _experts(off // ROW_BLOCK, padded // ROW_BLOCK, counts, xs, w_gate_exp, w_up_exp, w_down_exp)
    dest_row = dest_t.reshape(1, TOP_K * T)
    y_slots = _gather_rows(jnp.concatenate([dest_row, dest_row + n_rows], axis=1), y.reshape(2 * n_rows, HALF))
    return _combine(y_slots, base, w_t.T, gt2, S)


def kernel(x, c, positions, w_ada, b_ada, g_norm1, w_in, g_cq, w_uq, g_ckv, w_ukv, g_qn, g_kn, g_attn_out, conv_w, conv_b, w_mq, w_mk, b_igate, b_fgate, g_mlstm_out, w_out, g_norm2, w_router, router_bias, w_gate_exp, w_up_exp, w_down_exp, w_gate_sh, w_up_sh, w_down_sh):
    B, S, _ = x.shape
    depth = w_ada.shape[0]
    x2 = x.reshape(B * S, D_MODEL)
    for l in range(depth):
        ada = _ada(c, w_ada[l], b_ada[l])
        x2 = _layer(x2, ada, positions, B, S, g_norm1[l], w_in[l], g_cq[l], w_uq[l], g_ckv[l], w_ukv[l], g_qn[l],
                    g_kn[l], g_attn_out[l], conv_w[l], conv_b[l], w_mq[l], w_mk[l], b_igate[l], b_fgate[l],
                    g_mlstm_out[l], w_out[l], g_norm2[l], w_router[l], router_bias[l], w_gate_exp[l], w_up_exp[l],
                    w_down_exp[l], w_gate_sh[l], w_up_sh[l], w_down_sh[l])
    return x2.reshape(B, S, D_MODEL)
```

```python
import functools
import math

import jax
import jax.numpy as jnp
from jax import lax
from jax.experimental import pallas as pl
from jax.experimental.pallas import tpu as pltpu
from jax.experimental.pallas import tpu_sc as plsc

F32 = jnp.float32
BF16 = jnp.bfloat16
I32 = jnp.int32

D_MODEL = 1024
MLA_HEADS = 8
QK_NOPE = 64
QK_ROPE = 32
QK_DIM = QK_NOPE + QK_ROPE
V_DIM = 64
Q_LORA = 384
KV_LORA = 256
ROPE_THETA = 10000.0
MLSTM_HEADS = 4
MLSTM_DIM = 128
MLSTM_INNER = MLSTM_HEADS * MLSTM_DIM
CONV_WIDTH = 4
N_EXPERTS = 256
TOP_K = 8
N_GROUPS = 8
TOPK_GROUPS = 4
GROUP_SIZE = N_EXPERTS // N_GROUPS
EXPERT_DIM = 256
SHARED_DIM = 256
ROUTED_SCALE = 2.5
RMS_EPS = 1e-6

LANES = 128
SEQ_TILE = 256
ROW_BLOCK = 256
VMEM_LIMIT = 56 * 1024 * 1024

_OFF_CQ = 0
_OFF_CKV = _OFF_CQ + Q_LORA
_OFF_Z = _OFF_CKV + KV_LORA
_OFF_O = _OFF_Z + MLSTM_INNER
_OFF_KR = _OFF_O + MLSTM_INNER
_OFF_KRS = _OFF_KR + LANES
_WA_COLS = _OFF_KRS + LANES
_WT_ROWS = MLSTM_INNER + 16


def _params(n_axes, vmem=VMEM_LIMIT):
    return pltpu.CompilerParams(dimension_semantics=("arbitrary",) * n_axes, vmem_limit_bytes=vmem)


def _dot(a, b):
    return jnp.dot(a, b, preferred_element_type=F32)


def _dot_nt(a, b):
    return lax.dot_general(a, b, (((1,), (1,)), ((), ())), preferred_element_type=F32)


def _sigmoid(x):
    return 1.0 / (1.0 + jnp.exp(-x))


def _silu(x):
    return x * _sigmoid(x)


U32 = jnp.uint32
PACKED = D_MODEL // 2


def _pack_rows(x):
    w = x.shape[1] // 2
    lo = lax.bitcast_convert_type(x[:, :w].astype(BF16).astype(F32), U32)
    hi = lax.bitcast_convert_type(x[:, w:].astype(BF16).astype(F32), U32)
    return lax.shift_right_logical(lo, U32(16)) | (hi & U32(0xFFFF0000))


def _unpack_rows(p):
    lo = lax.bitcast_convert_type(lax.shift_left(p, U32(16)), F32)
    hi = lax.bitcast_convert_type(p & U32(0xFFFF0000), F32)
    return lo, hi


def _ada_body(c_ref, w_ref, b_ref, o_ref):
    c = c_ref[...]
    cond = _silu(c).astype(BF16)
    o_ref[...] = _dot(cond, w_ref[...].astype(BF16)) + b_ref[...]


def _ada(c, w_ada, b_ada):
    B = c.shape[0]
    n = w_ada.shape[1]
    tn = 1024
    return pl.pallas_call(
        _ada_body,
        out_shape=jax.ShapeDtypeStruct((B, n), F32),
        grid=(n // tn,),
        in_specs=[
            pl.BlockSpec((B, D_MODEL), lambda j: (0, 0)),
            pl.BlockSpec((D_MODEL, tn), lambda j: (0, j)),
            pl.BlockSpec((1, tn), lambda j: (0, j)),
        ],
        out_specs=pl.BlockSpec((B, tn), lambda j: (0, j)),
        compiler_params=_params(1),
        name="ada",
    )(c, w_ada, b_ada.reshape(1, n))


def _inproj_body(x_ref, sc_ref, sh_ref, g1_ref, wa_ref, wt_ref, gcq_ref, wuq_ref, wuqs_ref, gckv_ref,
                 wuk_ref, wuv_ref, vone_ref, gq_ref, gqs_ref, gk_ref, gks_ref, cos_ref, sin_ref,
                 z_ref, o_ref, vt_ref, gt_ref, q_ref, k_ref, mv_ref):
    x = x_ref[...]
    hn = x * lax.rsqrt(jnp.mean(x * x, axis=-1, keepdims=True) + RMS_EPS) * g1_ref[...]
    hm = (hn * sc_ref[...] + sh_ref[...]).astype(BF16)
    p = _dot(hm, wa_ref[...])
    z_ref[...] = p[:, _OFF_Z:_OFF_Z + MLSTM_INNER].astype(BF16)
    o_ref[...] = p[:, _OFF_O:_OFF_O + MLSTM_INNER].astype(BF16)
    rt = _dot_nt(wt_ref[...], hm)
    vt_ref[...] = rt[:MLSTM_INNER].astype(BF16)
    gt_ref[...] = rt[MLSTM_INNER:]

    cq = p[:, _OFF_CQ:_OFF_CQ + Q_LORA]
    cqn = (cq * lax.rsqrt(jnp.mean(cq * cq, axis=-1, keepdims=True) + RMS_EPS) * gcq_ref[...]).astype(BF16)
    ckv = p[:, _OFF_CKV:_OFF_CKV + KV_LORA]
    ckvn = (ckv * lax.rsqrt(jnp.mean(ckv * ckv, axis=-1, keepdims=True) + RMS_EPS) * gckv_ref[...]).astype(BF16)
    qa = _dot(cqn, wuq_ref[...])
    qb = _dot(cqn, wuqs_ref[...])
    ka = _dot(ckvn, wuk_ref[...])
    mv_ref[...] = (_dot(ckvn, wuv_ref[...]) + vone_ref[...]).astype(BF16)
    kr = p[:, _OFF_KR:_OFF_KR + LANES]
    krs = p[:, _OFF_KRS:_OFF_KRS + LANES]

    cosp = cos_ref[...]
    sinp = sin_ref[...]
    gqc = gq_ref[...] * cosp
    gqs = gqs_ref[...] * sinp
    gkc = gk_ref[...] * cosp
    k_rot = krs * (gks_ref[...] * sinp)
    kr_ss = jnp.sum(kr * kr, axis=-1, keepdims=True)
    q_scale = (QK_DIM ** -0.5) * math.log2(math.e)
    for h in range(MLA_HEADS):
        sl = slice(h * LANES, (h + 1) * LANES)
        qah = qa[:, sl]
        rq = lax.rsqrt(jnp.sum(qah * qah, axis=-1, keepdims=True) * (1.0 / QK_DIM) + RMS_EPS) * q_scale
        q_ref[:, sl] = ((qah * gqc + qb[:, sl] * gqs) * rq).astype(BF16)
        kah = ka[:, sl]
        rk = lax.rsqrt((jnp.sum(kah * kah, axis=-1, keepdims=True) + kr_ss) * (1.0 / QK_DIM) + RMS_EPS)
        k_ref[:, sl] = (((kah + kr) * gkc + k_rot) * rk).astype(BF16)


def _inproj(x2, sc1p, sh1, g1, wa, wt, gcq, wuq, wuqs, gckv, wuk, wuv, vone, gq, gqs, gk, gks, cosp, sinp, S):
    T = x2.shape[0]
    tm = SEQ_TILE
    nt = T // tm
    per_b = S // tm
    hp = MLA_HEADS * LANES
    full = lambda a: pl.BlockSpec(a.shape, lambda i: (0,) * a.ndim)
    row = lambda w: pl.BlockSpec((tm, w), lambda i: (i, 0))
    bmod = pl.BlockSpec((None, 1, D_MODEL), lambda i: (i // per_b, 0, 0))
    out_shapes = (
        jax.ShapeDtypeStruct((T, MLSTM_INNER), BF16),
        jax.ShapeDtypeStruct((T, MLSTM_INNER), BF16),
        jax.ShapeDtypeStruct((nt, MLSTM_INNER, tm), BF16),
        jax.ShapeDtypeStruct((nt, 16, tm), F32),
        jax.ShapeDtypeStruct((T, hp), BF16),
        jax.ShapeDtypeStruct((T, hp), BF16),
        jax.ShapeDtypeStruct((T, hp), BF16),
    )
    out_specs = (
        row(MLSTM_INNER), row(MLSTM_INNER),
        pl.BlockSpec((None, MLSTM_INNER, tm), lambda i: (i, 0, 0)),
        pl.BlockSpec((None, 16, tm), lambda i: (i, 0, 0)),
        row(hp), row(hp), row(hp),
    )
    in_specs = [row(D_MODEL), bmod, bmod, full(g1), full(wa), full(wt), full(gcq), full(wuq), full(wuqs),
                full(gckv), full(wuk), full(wuv), full(vone), full(gq), full(gqs), full(gk), full(gks),
                row(LANES), row(LANES)]
    return pl.pallas_call(
        _inproj_body, out_shape=out_shapes, grid=(nt,), in_specs=in_specs, out_specs=out_specs,
        compiler_params=_params(1), name="inproj",
    )(x2, sc1p, sh1, g1, wa, wt, gcq, wuq, wuqs, gckv, wuk, wuv, vone, gq, gqs, gk, gks, cosp, sinp)


ATTN_HEADS_PER_STEP = 4


def _pairs_loop(n, fn, init):
    c = lax.fori_loop(0, n // 2, lambda jj, c: fn(2 * jj + 1, fn(2 * jj, c)), init)
    return lax.fori_loop(0, n % 2, lambda _, c: fn(n - 1, c), c)


def _attn_body(q_ref, k_ref, v_ref, o_ref, s_scr):
    i = pl.program_id(2)
    t = SEQ_TILE
    nh = ATTN_HEADS_PER_STEP
    qry_pos = lax.broadcasted_iota(I32, (t, t), 0)
    key_pos = lax.broadcasted_iota(I32, (t, t), 1)
    causal = key_pos <= qry_pos
    qs = [q_ref[:, hh * LANES:(hh + 1) * LANES] for hh in range(nh)]

    def scores(j, hh, diag):
        kj = k_ref[pl.ds(pl.multiple_of(j * t, t), t), hh * LANES:(hh + 1) * LANES]
        s = _dot_nt(qs[hh], kj)
        return jnp.where(causal, s, -jnp.inf) if diag else s

    def halves(x):
        return x[:, :LANES], x[:, LANES:]

    def max_pass(j, mx, diag):
        out = []
        for hh in range(nh):
            s = scores(j, hh, diag)
            s_scr[j, hh] = s
            lo, hi = halves(s)
            out.append(jnp.maximum(mx[hh], jnp.maximum(lo, hi)))
        return tuple(out)

    mx = tuple(jnp.full((t, LANES), -jnp.inf, F32) for _ in range(nh))
    mx = _pairs_loop(i, lambda j, c: max_pass(j, c, False), mx)
    mx = max_pass(i, mx, True)
    mb = [jnp.broadcast_to(jnp.max(mx[hh], axis=1, keepdims=True), (t, LANES)) for hh in range(nh)]

    def acc_pass(j, carry, diag):
        row0 = pl.multiple_of(j * t, t)
        out = []
        for hh in range(nh):
            lo, hi = halves(s_scr[j, hh])
            p = jnp.concatenate([jnp.exp2(lo - mb[hh]), jnp.exp2(hi - mb[hh])], axis=1).astype(BF16)
            out.append(carry[hh] + _dot(p, v_ref[pl.ds(row0, t), hh * LANES:(hh + 1) * LANES]))
        return tuple(out)

    init = tuple(jnp.zeros((t, LANES), F32) for _ in range(nh))
    carry = _pairs_loop(i, lambda j, c: acc_pass(j, c, False), init)
    final = acc_pass(i, carry, True)
    outs = [acc / acc[:, V_DIM:V_DIM + 1] for acc in final]
    first_head = lax.broadcasted_iota(I32, (t, LANES), 1) < V_DIM
    o_ref[...] = jnp.concatenate(
        [jnp.where(first_head, outs[2 * pr], pltpu.roll(outs[2 * pr + 1], V_DIM, axis=1)) for pr in range(nh // 2)],
        axis=1).astype(BF16)


def _attn(q, k, mvt, B, S):
    T = q.shape[0]
    t = SEQ_TILE
    nh = ATTN_HEADS_PER_STEP
    per_b = S // t
    return pl.pallas_call(
        _attn_body,
        out_shape=jax.ShapeDtypeStruct((T, MLA_HEADS * V_DIM), BF16),
        grid=(B, MLA_HEADS // nh, per_b),
        in_specs=[
            pl.BlockSpec((t, nh * LANES), lambda b, hp, i: (b * per_b + i, hp)),
            pl.BlockSpec((S, nh * LANES), lambda b, hp, i: (b, hp)),
            pl.BlockSpec((S, nh * LANES), lambda b, hp, i: (b, hp)),
        ],
        out_specs=pl.BlockSpec((t, nh * V_DIM), lambda b, hp, i: (b * per_b + i, hp)),
        scratch_shapes=[pltpu.VMEM((per_b, nh, t, t), F32)],
        compiler_params=_params(3), name="attn",
    )(q, k, mvt)


def _split3(a):
    hi = a.astype(BF16)
    r = a - hi.astype(F32)
    mid = r.astype(BF16)
    lo = (r - mid.astype(F32)).astype(BF16)
    return hi, mid, lo


def _mlstm_body(bi_ref, bf_ref, z_ref, vt_ref, g_ref, og_ref, cw_ref, cb_ref, wq_ref, wk_ref, gout_ref,
                out_ref, zpad, q_s, k_s, st_s, m_s, *, S):
    h = pl.program_id(1)
    L = SEQ_TILE
    nchunk = S // L
    hd = MLSTM_DIM
    zpad[0:8, :] = jnp.zeros((8, hd), F32)
    zpad[8:, :] = z_ref[...].astype(F32)
    zc = cb_ref[...] + jnp.zeros((S, hd), F32)
    for j in range(CONV_WIDTH):
        zc = zc + cw_ref[j:j + 1, :] * zpad[pl.ds(8 - (CONV_WIDTH - 1) + j, S), :]
    zs = _silu(zc).astype(BF16)
    q_s[...] = _dot(zs, wq_ref[...]).astype(BF16)
    k_s[...] = (_dot(zs, wk_ref[...]) * (hd ** -0.5)).astype(BF16)

    st_s[...] = jnp.zeros_like(st_s)
    m_s[...] = jnp.zeros_like(m_s)
    b_i = bi_ref[h]
    b_f = bf_ref[h]
    r_i = lax.broadcasted_iota(I32, (L, L), 0)
    c_i = lax.broadcasted_iota(I32, (L, L), 1)
    tril = r_i >= c_i
    strict_lower = jnp.where(r_i > c_i, 1.0, 0.0).astype(BF16)
    ones_row = jnp.where(lax.broadcasted_iota(I32, (hd, L), 0) == 0, 1.0, 0.0).astype(BF16)

    def chunk(c, _):
        start = pl.multiple_of(c * L, L)
        qc = q_s[pl.ds(start, L), :]
        kc = k_s[pl.ds(start, L), :]
        vt_aug = jnp.concatenate([vt_ref[c], ones_row], axis=0)
        li = g_ref[c, pl.ds(h, 1), :] + b_i
        fp = g_ref[c, pl.ds(MLSTM_HEADS + h, 1), :] + b_f
        lf = jnp.minimum(fp, 0.0) - jnp.log(1.0 + jnp.exp(-jnp.abs(fp)))
        a_mat = jnp.where(tril, lf, 0.0)
        a_hi, a_mid, a_lo = _split3(a_mat)
        d0 = _dot(a_hi, strict_lower) + _dot(a_mid, strict_lower) + _dot(a_lo, strict_lower)
        dmat = jnp.where(tril, d0 + li, -jnp.inf)
        b_col = jnp.sum(a_mat, axis=1, keepdims=True)
        m_prev = m_s[...]
        inter = b_col + m_prev
        mt = jnp.maximum(inter, jnp.max(dmat, axis=1, keepdims=True))
        w = jnp.exp(dmat - mt) * _dot_nt(qc, kc)
        decay = jnp.exp(inter - mt)
        st = st_s[...]
        num_aug = decay * _dot_nt(qc, st.astype(BF16)) + _dot_nt(w.astype(BF16), vt_aug)
        num = num_aug[:, :hd]
        den = num_aug[:, hd:hd + 1]
        hv = num / jnp.maximum(jnp.abs(den), jnp.exp(-mt))
        g = d0[L - 1:L, :] + li
        b_last = b_col[L - 1:L, :]
        m_new = jnp.maximum(b_last + m_prev, jnp.max(g, axis=1, keepdims=True))
        a = jnp.exp(g - m_new)
        cd = jnp.exp(b_last + m_prev - m_new)
        st_s[...] = cd * st + _dot((vt_aug.astype(F32) * a).astype(BF16), kc)
        m_s[...] = m_new
        hn = hv * lax.rsqrt(jnp.mean(hv * hv, axis=-1, keepdims=True) + RMS_EPS) * gout_ref[...]
        og = og_ref[pl.ds(start, L), :].astype(F32)
        out_ref[pl.ds(start, L), :] = (_sigmoid(og) * hn).astype(BF16)
        return 0

    lax.fori_loop(0, nchunk, chunk, 0)


def _mlstm(z, vt3, g3, og, conv_w, conv_b, wq, wk, b_i, b_f, g_out, B, S):
    T = z.shape[0]
    L = SEQ_TILE
    per_b = S // L
    hd = MLSTM_DIM
    smem = pl.BlockSpec(memory_space=pltpu.SMEM)
    seq = pl.BlockSpec((S, hd), lambda b, h: (b, h))
    return pl.pallas_call(
        functools.partial(_mlstm_body, S=S),
        out_shape=jax.ShapeDtypeStruct((T, MLSTM_INNER), BF16),
        grid=(B, MLSTM_HEADS),
        in_specs=[
            smem, smem, seq,
            pl.BlockSpec((per_b, hd, L), lambda b, h: (b, h, 0)),
            pl.BlockSpec((per_b, 16, L), lambda b, h: (b, 0, 0)),
            seq,
            pl.BlockSpec((CONV_WIDTH, hd), lambda b, h: (0, h)),
            pl.BlockSpec((1, hd), lambda b, h: (0, h)),
            pl.BlockSpec((None, hd, hd), lambda b, h: (h, 0, 0)),
            pl.BlockSpec((None, hd, hd), lambda b, h: (h, 0, 0)),
            pl.BlockSpec((1, hd), lambda b, h: (0, h)),
        ],
        out_specs=seq,
        scratch_shapes=[
            pltpu.VMEM((S + 8, hd), F32),
            pltpu.VMEM((S, hd), BF16),
            pltpu.VMEM((S, hd), BF16),
            pltpu.VMEM((2 * hd, hd), F32),
            pltpu.VMEM((1, 1), F32),
        ],
        compiler_params=_params(2), name="mlstm",
    )(b_i, b_f, z, vt3, g3, og, conv_w, conv_b, wq, wk, g_out)


def _post_body(ao_ref, mo_ref, x_ref, gt1_ref, gattn_ref, wout_ref, g2_ref, sc2_ref, sh2_ref, gt2_ref,
               wrt_ref, wgu_ref, wds_ref, base_ref, h2_ref, lg_ref):
    ao = ao_ref[...].astype(F32)
    aon = ao * lax.rsqrt(jnp.mean(ao * ao, axis=-1, keepdims=True) + RMS_EPS) * gattn_ref[...]
    mix_in = jnp.concatenate([aon.astype(BF16), mo_ref[...]], axis=1)
    x1 = x_ref[...] + gt1_ref[...] * _dot(mix_in, wout_ref[...])
    hn = x1 * lax.rsqrt(jnp.mean(x1 * x1, axis=-1, keepdims=True) + RMS_EPS) * g2_ref[...]
    h2 = hn * sc2_ref[...] + sh2_ref[...]
    words = _pack_rows(h2)
    h2_ref[0] = words[:, :HALF]
    h2_ref[1] = words[:, HALF:]
    h2b = h2.astype(BF16)
    gu = _dot(h2b, wgu_ref[...])
    act = (_silu(gu[:, :SHARED_DIM]) * gu[:, SHARED_DIM:]).astype(BF16)
    base_ref[...] = x1 + gt2_ref[...] * _dot(act, wds_ref[...])
    lg_ref[...] = _dot_nt(wrt_ref[...], h2b)


def _post(ao, mo, x2, gt1, gattn, wout, g2, sc2p, sh2, gt2, wrt, wgu, wds, S):
    T = x2.shape[0]
    tm = SEQ_TILE
    per_b = S // tm
    full = lambda a: pl.BlockSpec(a.shape, lambda i: (0,) * a.ndim)
    row = lambda w: pl.BlockSpec((tm, w), lambda i: (i, 0))
    bmod = pl.BlockSpec((None, 1, D_MODEL), lambda i: (i // per_b, 0, 0))
    return pl.pallas_call(
        _post_body,
        out_shape=(jax.ShapeDtypeStruct((T, D_MODEL), F32), jax.ShapeDtypeStruct((2, T, HALF), U32),
                   jax.ShapeDtypeStruct((N_EXPERTS, T), F32)),
        grid=(T // tm,),
        in_specs=[row(MLA_HEADS * V_DIM), row(MLSTM_INNER), row(D_MODEL), bmod, full(gattn), full(wout),
                  full(g2), bmod, bmod, bmod, full(wrt), full(wgu), full(wds)],
        out_specs=(row(D_MODEL), pl.BlockSpec((2, tm, HALF), lambda i: (0, i, 0)),
                   pl.BlockSpec((N_EXPERTS, tm), lambda i: (0, i))),
        compiler_params=_params(1), name="post",
    )(ao, mo, x2, gt1, gattn, wout, g2, sc2p, sh2, gt2, wrt, wgu, wds)


def _first_max(x, rows, n):
    mx = jnp.max(x, axis=0, keepdims=True)
    idx = jnp.min(jnp.where(x == mx, rows, n), axis=0, keepdims=True)
    return mx, idx


def _route_body(lg_ref, bias_ref, idx_ref, w_ref, rank_ref, cnt_ref, carry):
    i = pl.program_id(0)
    tt = lg_ref.shape[1]

    @pl.when(i == 0)
    def _():
        carry[...] = jnp.zeros_like(carry)

    sc = _sigmoid(lg_ref[...])
    bi = sc + bias_ref[...]
    rows = lax.broadcasted_iota(I32, (N_EXPERTS, tt), 0)
    rows_g = lax.broadcasted_iota(I32, (GROUP_SIZE, tt), 0)
    rows_n = lax.broadcasted_iota(I32, (N_GROUPS, tt), 0)
    neg = -jnp.inf
    gs = []
    for g in range(N_GROUPS):
        xg = bi[g * GROUP_SIZE:(g + 1) * GROUP_SIZE]
        m1, i1 = _first_max(xg, rows_g, GROUP_SIZE)
        m2 = jnp.max(jnp.where(rows_g == i1, neg, xg), axis=0, keepdims=True)
        gs.append(m1 + m2)
    cur = jnp.concatenate(gs, axis=0)
    keep = jnp.zeros((N_GROUPS, tt), F32)
    for _ in range(TOPK_GROUPS):
        _, gi = _first_max(cur, rows_n, N_GROUPS)
        sel = rows_n == gi
        keep = jnp.where(sel, 1.0, keep)
        cur = jnp.where(sel, neg, cur)
    masked = jnp.concatenate(
        [jnp.where(keep[g:g + 1] > 0.0, bi[g * GROUP_SIZE:(g + 1) * GROUP_SIZE], neg) for g in range(N_GROUPS)], axis=0)
    cur = masked
    onehot = jnp.zeros((N_EXPERTS, tt), F32)
    idxs, ws = [], []
    for _ in range(TOP_K):
        _, ei = _first_max(cur, rows, N_EXPERTS)
        sel = rows == ei
        ws.append(jnp.sum(jnp.where(sel, sc, 0.0), axis=0, keepdims=True))
        idxs.append(ei)
        cur = jnp.where(sel, neg, cur)
        onehot = jnp.where(sel, 1.0, onehot)
    wsum = ws[0]
    for k in range(1, TOP_K):
        wsum = wsum + ws[k]
    idx_ref[...] = jnp.concatenate(idxs, axis=0)
    w_ref[...] = jnp.concatenate([w / wsum * ROUTED_SCALE for w in ws], axis=0)
    r_i = lax.broadcasted_iota(I32, (tt, tt), 0)
    c_i = lax.broadcasted_iota(I32, (tt, tt), 1)
    before = jnp.where(r_i < c_i, 1.0, 0.0).astype(BF16)
    tot = _dot(onehot.astype(BF16), before) + carry[...]
    rank_ref[...] = jnp.concatenate(
        [jnp.sum(jnp.where(rows == ei, tot, 0.0), axis=0, keepdims=True) for ei in idxs], axis=0).astype(I32)
    carry[...] = carry[...] + jnp.sum(onehot, axis=1, keepdims=True)
    cnt_ref[...] = jnp.broadcast_to(carry[...], cnt_ref.shape)


def _route(lgt, bias_col):
    T = lgt.shape[1]
    tt = SEQ_TILE
    blk = pl.BlockSpec((TOP_K, tt), lambda i: (0, i))
    return pl.pallas_call(
        _route_body,
        out_shape=(jax.ShapeDtypeStruct((TOP_K, T), I32), jax.ShapeDtypeStruct((TOP_K, T), F32),
                   jax.ShapeDtypeStruct((TOP_K, T), I32), jax.ShapeDtypeStruct((N_EXPERTS, LANES), F32)),
        grid=(T // tt,),
        in_specs=[pl.BlockSpec((N_EXPERTS, tt), lambda i: (0, i)), pl.BlockSpec((N_EXPERTS, 1), lambda i: (0, 0))],
        out_specs=(blk, blk, blk, pl.BlockSpec((N_EXPERTS, LANES), lambda i: (0, 0))),
        scratch_shapes=[pltpu.VMEM((N_EXPERTS, 1), F32)],
        compiler_params=_params(1), name="route",
    )(lgt, bias_col)


def _dest_body(idx_ref, rank_ref, off_ref, dest_ref):
    tt = idx_ref.shape[1]
    rows = lax.broadcasted_iota(I32, (N_EXPERTS, tt), 0)
    off = off_ref[...]
    outs = []
    for k in range(TOP_K):
        sel = rows == idx_ref[k:k + 1, :]
        outs.append(jnp.sum(jnp.where(sel, off, 0), axis=0, keepdims=True) + rank_ref[k:k + 1, :])
    dest_ref[...] = jnp.concatenate(outs, axis=0)


def _dest(idx_t, rank_t, off_col):
    T = idx_t.shape[1]
    tt = min(4 * SEQ_TILE, T)
    blk = pl.BlockSpec((TOP_K, tt), lambda i: (0, i))
    return pl.pallas_call(
        _dest_body, out_shape=jax.ShapeDtypeStruct((TOP_K, T), I32), grid=(T // tt,),
        in_specs=[blk, blk, pl.BlockSpec((N_EXPERTS, 1), lambda i: (0, 0))], out_specs=blk,
        compiler_params=_params(1), name="dest",
    )(idx_t, rank_t, off_col)


SC_WINDOW = 128
HALF = PACKED // 2


def _sc_mesh():
    return plsc.VectorSubcoreMesh(core_axis_name="c", subcore_axis_name="s")


def _scatter_rows(dest2_t, rows, n_out):
    n = rows.shape[0]

    @pl.kernel(out_type=jax.ShapeDtypeStruct((n_out, HALF), U32), mesh=_sc_mesh(), scratch_types=[])
    def scatter(d_hbm, x_hbm, o_hbm):
        def body(d_vmem, x_vmem):
            for k in range(TOP_K):
                pltpu.sync_copy(x_vmem, o_hbm.at[d_vmem.at[k]])

        pltpu.emit_pipeline(
            body, grid=(n // SC_WINDOW,),
            in_specs=[pl.BlockSpec((TOP_K, SC_WINDOW), lambda i: (0, i)),
                      pl.BlockSpec((SC_WINDOW, HALF), lambda i: (i, 0))],
            out_specs=[], core_axis_name=("c", "s"), dimension_semantics=(pltpu.PARALLEL,),
        )(d_hbm, x_hbm)

    return scatter(dest2_t, rows)


def _gather_rows(index_row, rows):
    n = index_row.shape[1]

    @pl.kernel(out_type=jax.ShapeDtypeStruct((n, HALF), U32), mesh=_sc_mesh(), scratch_types=[])
    def gather(i_hbm, y_hbm, o_hbm):
        def body(i_vmem, o_vmem):
            pltpu.sync_copy(y_hbm.at[i_vmem.at[0]], o_vmem)

        pltpu.emit_pipeline(
            body, grid=(n // SC_WINDOW,),
            in_specs=[pl.BlockSpec((1, SC_WINDOW), lambda i: (0, i))],
            out_specs=[pl.BlockSpec((SC_WINDOW, HALF), lambda i: (i, 0))],
            core_axis_name=("c", "s"), dimension_semantics=(pltpu.PARALLEL,),
        )(i_hbm, o_hbm)

    return gather(index_row, rows)


EXPERT_RING = 4


def _experts_body(first_ref, nblk_ref, cnt_ref, xs_ref, wg_ref, wu_ref, wd_ref, y_ref,
                  xbuf, ybuf, wgb, wub, wdb, xsem, ysem):
    e = pl.program_id(0)
    last = pl.num_programs(0) - 1
    b0 = first_ref[e]
    nb = nblk_ref[e]
    total = first_ref[last] + nblk_ref[last]
    ring = xbuf.shape[0]

    def rows(g):
        return pl.ds(pl.multiple_of(g * ROW_BLOCK, ROW_BLOCK), ROW_BLOCK)

    def x_copy(g):
        slot = g % ring
        return pltpu.make_async_copy(xs_ref.at[:, rows(g), :], xbuf.at[slot], xsem.at[slot])

    def y_copy(g):
        slot = g % ring
        return pltpu.make_async_copy(ybuf.at[slot], y_ref.at[:, rows(g), :], ysem.at[slot])

    @pl.when(e == 0)
    def _():
        for g in range(ring - 1):
            @pl.when(g < total)
            def _():
                x_copy(g).start()

    @pl.when(nb > 0)
    def _():
        wgb[...] = wg_ref[...].astype(BF16)
        wub[...] = wu_ref[...].astype(BF16)
        wdb[...] = wd_ref[...].astype(BF16)

    def block(i, carry):
        g = b0 + i
        slot = g % ring

        @pl.when(g + ring - 1 < total)
        def _():
            x_copy(g + ring - 1).start()

        x_copy(g).wait()

        @pl.when(g >= ring)
        def _():
            y_copy(g - ring).wait()

        words = jnp.concatenate([xbuf[slot, 0], xbuf[slot, 1]], axis=1)
        n_live = cnt_ref[e] - i * ROW_BLOCK
        words = jnp.where(lax.broadcasted_iota(I32, words.shape, 0) < n_live, words, U32(0))
        x = jnp.concatenate(_unpack_rows(words), axis=1).astype(BF16)
        act = (_silu(_dot(x, wgb[...])) * _dot(x, wub[...])).astype(BF16)
        y = _pack_rows(_dot(act, wdb[...]))
        ybuf[slot, 0] = y[:, :HALF]
        ybuf[slot, 1] = y[:, HALF:]
        y_copy(g).start()
        return carry

    lax.fori_loop(0, nb, block, 0)

    @pl.when(e == last)
    def _():
        for back in range(ring, 0, -1):
            @pl.when(total >= back)
            def _():
                y_copy(total - back).wait()


def _experts(first_block, n_blocks, counts, xs, wg, wu, wd):
    n_rows = xs.shape[1]
    wmap = lambda e, fb, nb, cnt: (e, 0, 0)
    anyspec = pl.BlockSpec(memory_space=pl.ANY)
    ring = pltpu.VMEM((EXPERT_RING, 2, ROW_BLOCK, HALF), U32)
    grid_spec = pltpu.PrefetchScalarGridSpec(
        num_scalar_prefetch=3, grid=(N_EXPERTS,),
        in_specs=[
            anyspec,
            pl.BlockSpec((None, D_MODEL, EXPERT_DIM), wmap),
            pl.BlockSpec((None, D_MODEL, EXPERT_DIM), wmap),
            pl.BlockSpec((None, EXPERT_DIM, D_MODEL), wmap),
        ],
        out_specs=anyspec,
        scratch_shapes=[ring, ring,
                        pltpu.VMEM((D_MODEL, EXPERT_DIM), BF16), pltpu.VMEM((D_MODEL, EXPERT_DIM), BF16),
                        pltpu.VMEM((EXPERT_DIM, D_MODEL), BF16),
                        pltpu.SemaphoreType.DMA((EXPERT_RING,)), pltpu.SemaphoreType.DMA((EXPERT_RING,))],
    )
    return pl.pallas_call(
        _experts_body, out_shape=jax.ShapeDtypeStruct((2, n_rows, HALF), U32), grid_spec=grid_spec,
        compiler_params=_params(1), name="experts",
    )(first_block, n_blocks, counts, xs, wg, wu, wd)


COMBINE_PARTS = 4


def _combine_body(*refs):
    ya_refs = refs[:TOP_K]
    yb_refs = refs[TOP_K:2 * TOP_K]
    base_ref, w_ref, gt2_ref = refs[2 * TOP_K:2 * TOP_K + 3]
    out_ref = refs[-1]
    w = w_ref[...]
    r_lo = r_hi = None
    for k in range(TOP_K):
        lo, hi = _unpack_rows(jnp.concatenate([ya_refs[k][...], yb_refs[k][...]], axis=1))
        wk = w[:, k:k + 1]
        r_lo = wk * lo if r_lo is None else r_lo + wk * lo
        r_hi = wk * hi if r_hi is None else r_hi + wk * hi
    out_ref[...] = base_ref[...] + gt2_ref[...] * jnp.concatenate([r_lo, r_hi], axis=1)


def _combine(y_slots, base, w_tk, gt2, S, part, prev_out):
    T = base.shape[0]
    tt = SEQ_TILE
    nt = T // tt // COMBINE_PARTS
    t0 = part * nt
    per_b = S // tt
    yspec = lambda half, k: pl.BlockSpec((tt, HALF), lambda i: ((half * TOP_K + k) * nt + i, 0))
    in_specs = [yspec(half, k) for half in range(2) for k in range(TOP_K)] + [
        pl.BlockSpec((tt, D_MODEL), lambda i: (t0 + i, 0)),
        pl.BlockSpec((tt, TOP_K), lambda i: (t0 + i, 0)),
        pl.BlockSpec((None, 1, D_MODEL), lambda i: ((t0 + i) // per_b, 0, 0)),
    ]
    args = [y_slots] * (2 * TOP_K) + [base, w_tk, gt2]
    aliases = {}
    if prev_out is not None:
        in_specs.append(pl.BlockSpec(memory_space=pl.ANY))
        args.append(prev_out)
        aliases = {len(args) - 1: 0}
    return pl.pallas_call(
        _combine_body, out_shape=jax.ShapeDtypeStruct((T, D_MODEL), F32), grid=(nt,),
        in_specs=in_specs,
        out_specs=pl.BlockSpec((tt, D_MODEL), lambda i: (t0 + i, 0)),
        input_output_aliases=aliases,
        compiler_params=_params(1), name=f"combine{part}",
    )(*args)


def _pad_heads(w, width):
    kdim = w.shape[0]
    w3 = w.reshape(kdim, MLA_HEADS, width)
    return jnp.pad(w3, ((0, 0), (0, 0), (0, LANES - width))).reshape(kdim, MLA_HEADS * LANES)


def _rot_partner(w_rope):
    half = QK_ROPE // 2
    return jnp.concatenate([-w_rope[..., half:], w_rope[..., :half]], axis=-1)


def _lane_vec(nope, rope):
    return jnp.concatenate([nope, rope, jnp.zeros((LANES - QK_DIM,), F32)]).reshape(1, LANES)


def _layer(x2, ada, positions, B, S, g_norm1, w_in, g_cq, w_uq, g_ckv, w_ukv, g_qn, g_kn, g_attn_out,
           conv_w, conv_b, w_mq, w_mk, b_igate, b_fgate, g_mlstm_out, w_out, g_norm2, w_router, router_bias,
           w_gate_exp, w_up_exp, w_down_exp, w_gate_sh, w_up_sh, w_down_sh):
    T = B * S
    sh1, sc1, gt1, sh2, sc2, gt2 = [a.reshape(B, 1, D_MODEL) for a in jnp.split(ada, 6, axis=-1)]
    row = lambda v: v.reshape(1, -1).astype(F32)

    inv_freq = ROPE_THETA ** (-jnp.arange(0, QK_ROPE, 2, dtype=F32) / QK_ROPE)
    ang = positions.astype(F32).reshape(T, 1) * inv_freq
    half = QK_ROPE // 2
    freq_lane = jnp.arange(half)[:, None]
    lane = jnp.arange(LANES)[None, :]
    place = ((lane == QK_NOPE + freq_lane) | (lane == QK_NOPE + half + freq_lane)).astype(F32)
    spread = lambda tbl: jnp.dot(tbl, place, precision=lax.Precision.HIGHEST)
    cosp = (jnp.arange(LANES) < QK_NOPE).astype(F32)[None, :] + spread(jnp.cos(ang))
    sinp = spread(jnp.sin(ang))

    o0 = 0
    w_cq = w_in[:, o0:o0 + Q_LORA]; o0 += Q_LORA
    w_ckv = w_in[:, o0:o0 + KV_LORA]; o0 += KV_LORA
    w_kr = w_in[:, o0:o0 + QK_ROPE]; o0 += QK_ROPE
    w_z = w_in[:, o0:o0 + MLSTM_INNER]; o0 += MLSTM_INNER
    w_v = w_in[:, o0:o0 + MLSTM_INNER]; o0 += MLSTM_INNER
    w_o = w_in[:, o0:o0 + MLSTM_INNER]; o0 += MLSTM_INNER
    w_i = w_in[:, o0:o0 + MLSTM_HEADS]; o0 += MLSTM_HEADS
    w_f = w_in[:, o0:o0 + MLSTM_HEADS]
    zl = jnp.zeros((D_MODEL, QK_NOPE), F32)
    zr = jnp.zeros((D_MODEL, LANES - QK_DIM), F32)
    wa = jnp.concatenate([w_cq, w_ckv, w_z, w_o, zl, w_kr, zr, zl, _rot_partner(w_kr), zr], axis=1).astype(BF16)
    wt = jnp.concatenate([w_v.T, w_i.T, w_f.T, jnp.zeros((16 - 2 * MLSTM_HEADS, D_MODEL), F32)], axis=0).astype(BF16)

    uq3 = w_uq.reshape(Q_LORA, MLA_HEADS, QK_DIM)
    uq_rot = jnp.concatenate([jnp.zeros((Q_LORA, MLA_HEADS, QK_NOPE), F32), _rot_partner(uq3[..., QK_NOPE:])], axis=-1)
    wuq = _pad_heads(w_uq, QK_DIM).astype(BF16)
    wuqs = _pad_heads(uq_rot.reshape(Q_LORA, MLA_HEADS * QK_DIM), QK_DIM).astype(BF16)
    ukv3 = w_ukv.reshape(KV_LORA, MLA_HEADS, QK_NOPE + V_DIM)
    wuk = _pad_heads(ukv3[..., :QK_NOPE].reshape(KV_LORA, MLA_HEADS * QK_NOPE), QK_NOPE).astype(BF16)
    wuv = _pad_heads(ukv3[..., QK_NOPE:].reshape(KV_LORA, MLA_HEADS * V_DIM), V_DIM).astype(BF16)
    vone = jnp.tile((jnp.arange(LANES) == V_DIM).astype(F32), MLA_HEADS).reshape(1, MLA_HEADS * LANES)
    gq = _lane_vec(g_qn[:QK_NOPE], g_qn[QK_NOPE:])
    gqs = _lane_vec(jnp.zeros((QK_NOPE,), F32), _rot_partner(g_qn[QK_NOPE:]) * jnp.concatenate(
        [-jnp.ones((QK_ROPE // 2,), F32), jnp.ones((QK_ROPE // 2,), F32)]))
    gk = _lane_vec(g_kn[:QK_NOPE], g_kn[QK_NOPE:])
    gks = _lane_vec(jnp.zeros((QK_NOPE,), F32), _rot_partner(g_kn[QK_NOPE:]) * jnp.concatenate(
        [-jnp.ones((QK_ROPE // 2,), F32), jnp.ones((QK_ROPE // 2,), F32)]))

    z, og, vt3, g3, q, k, mv = _inproj(
        x2, 1.0 + sc1, sh1, row(g_norm1), wa, wt, row(g_cq), wuq, wuqs, row(g_ckv), wuk, wuv, vone,
        gq, gqs, gk, gks, cosp, sinp, S)
    attn_o = _attn(q, k, mv, B, S)
    mlstm_o = _mlstm(z, vt3, g3, og, conv_w.reshape(CONV_WIDTH, MLSTM_INNER), row(conv_b),
                     w_mq.astype(BF16), w_mk.astype(BF16), b_igate, b_fgate, row(g_mlstm_out), B, S)

    wgu = jnp.concatenate([w_gate_sh, w_up_sh], axis=1).astype(BF16)
    base, h2, lgt = _post(attn_o, mlstm_o, x2, gt1, row(g_attn_out), w_out.astype(BF16), row(g_norm2),
                          1.0 + sc2, sh2, gt2, w_router.T.astype(BF16), wgu, w_down_sh.astype(BF16), S)

    idx_t, w_t, rank_t, cnt = _route(lgt, router_bias.reshape(N_EXPERTS, 1))
    counts = cnt[:, 0].astype(I32)
    padded = (counts + ROW_BLOCK - 1) // ROW_BLOCK * ROW_BLOCK
    padded_end = jnp.cumsum(padded)
    off = padded_end - padded
    n_rows = T * TOP_K + N_EXPERTS * ROW_BLOCK
    dest_t = _dest(idx_t, rank_t, off.reshape(N_EXPERTS, 1))

    dest2_t = jnp.concatenate([dest_t, dest_t + n_rows], axis=1)
    xs = _scatter_rows(dest2_t, h2.reshape(2 * T, HALF), 2 * n_rows).reshape(2, n_rows, HALF)
    y = _experts(off // ROW_BLOCK, padded // ROW_BLOCK, counts, xs, w_gate_exp, w_up_exp, w_down_exp)
    y_rows = y.reshape(2 * n_rows, HALF)
    w_tk = w_t.T
    tp = T // COMBINE_PARTS
    out = None
    for part in range(COMBINE_PARTS):
        dest_p = dest_t[:, part * tp:(part + 1) * tp].reshape(1, TOP_K * tp)
        y_slots = _gather_rows(jnp.concatenate([dest_p, dest_p + n_rows], axis=1), y_rows)
        out = _combine(y_slots, base, w_tk, gt2, S, part, out)
    return out


def kernel(x, c, positions, w_ada, b_ada, g_norm1, w_in, g_cq, w_uq, g_ckv, w_ukv, g_qn, g_kn, g_attn_out, conv_w, conv_b, w_mq, w_mk, b_igate, b_fgate, g_mlstm_out, w_out, g_norm2, w_router, router_bias, w_gate_exp, w_up_exp, w_down_exp, w_gate_sh, w_up_sh, w_down_sh):
    B, S, _ = x.shape
    depth = w_ada.shape[0]
    x2 = x.reshape(B * S, D_MODEL)
    for l in range(depth):
        ada = _ada(c, w_ada[l], b_ada[l])
        x2 = _layer(x2, ada, positions, B, S, g_norm1[l], w_in[l], g_cq[l], w_uq[l], g_ckv[l], w_ukv[l], g_qn[l],
                    g_kn[l], g_attn_out[l], conv_w[l], conv_b[l], w_mq[l], w_mk[l], b_igate[l], b_fgate[l],
                    g_mlstm_out[l], w_out[l], g_norm2[l], w_router[l], router_bias[l], w_gate_exp[l], w_up_exp[l],
                    w_down_exp[l], w_gate_sh[l], w_up_sh[l], w_down_sh[l])
    return x2.reshape(B, S, D_MODEL)
```

```python
import functools
import math

import jax
import jax.numpy as jnp
from jax import lax
from jax.experimental import pallas as pl
from jax.experimental.pallas import tpu as pltpu
from jax.experimental.pallas import tpu_sc as plsc

F32 = jnp.float32
BF16 = jnp.bfloat16
I32 = jnp.int32

D_MODEL = 1024
MLA_HEADS = 8
QK_NOPE = 64
QK_ROPE = 32
QK_DIM = QK_NOPE + QK_ROPE
V_DIM = 64
Q_LORA = 384
KV_LORA = 256
ROPE_THETA = 10000.0
MLSTM_HEADS = 4
MLSTM_DIM = 128
MLSTM_INNER = MLSTM_HEADS * MLSTM_DIM
CONV_WIDTH = 4
N_EXPERTS = 256
TOP_K = 8
N_GROUPS = 8
TOPK_GROUPS = 4
GROUP_SIZE = N_EXPERTS // N_GROUPS
EXPERT_DIM = 256
SHARED_DIM = 256
ROUTED_SCALE = 2.5
RMS_EPS = 1e-6

LANES = 128
SEQ_TILE = 256
WIDE_TILE = 512
ROW_BLOCK = 256
VMEM_LIMIT = 56 * 1024 * 1024

_OFF_CQ = 0
_OFF_CKV = _OFF_CQ + Q_LORA
_OFF_Z = _OFF_CKV + KV_LORA
_OFF_O = _OFF_Z + MLSTM_INNER
_OFF_KR = _OFF_O + MLSTM_INNER
_OFF_KRS = _OFF_KR + LANES
_WA_COLS = _OFF_KRS + LANES
_WT_ROWS = MLSTM_INNER + 16


def _params(n_axes, vmem=VMEM_LIMIT):
    return pltpu.CompilerParams(dimension_semantics=("arbitrary",) * n_axes, vmem_limit_bytes=vmem)


def _dot(a, b):
    return jnp.dot(a, b, preferred_element_type=F32)


def _dot_nt(a, b):
    return lax.dot_general(a, b, (((1,), (1,)), ((), ())), preferred_element_type=F32)


def _sigmoid(x):
    return 1.0 / (1.0 + jnp.exp(-x))


def _silu(x):
    return x * _sigmoid(x)


U32 = jnp.uint32
PACKED = D_MODEL // 2


def _pack_rows(x):
    w = x.shape[1] // 2
    lo = lax.bitcast_convert_type(x[:, :w].astype(BF16).astype(F32), U32)
    hi = lax.bitcast_convert_type(x[:, w:].astype(BF16).astype(F32), U32)
    return lax.shift_right_logical(lo, U32(16)) | (hi & U32(0xFFFF0000))


def _unpack_rows(p):
    lo = lax.bitcast_convert_type(lax.shift_left(p, U32(16)), F32)
    hi = lax.bitcast_convert_type(p & U32(0xFFFF0000), F32)
    return lo, hi


def _ada_body(c_ref, w_ref, b_ref, o_ref):
    c = c_ref[...]
    cond = _silu(c).astype(BF16)
    o_ref[...] = _dot(cond, w_ref[...].astype(BF16)) + b_ref[...]


def _ada(c, w_ada, b_ada):
    B = c.shape[0]
    n = w_ada.shape[1]
    tn = 1024
    return pl.pallas_call(
        _ada_body,
        out_shape=jax.ShapeDtypeStruct((B, n), F32),
        grid=(n // tn,),
        in_specs=[
            pl.BlockSpec((B, D_MODEL), lambda j: (0, 0)),
            pl.BlockSpec((D_MODEL, tn), lambda j: (0, j)),
            pl.BlockSpec((1, tn), lambda j: (0, j)),
        ],
        out_specs=pl.BlockSpec((B, tn), lambda j: (0, j)),
        compiler_params=_params(1),
        name="ada",
    )(c, w_ada, b_ada.reshape(1, n))


def _inproj_body(x_ref, sc_ref, sh_ref, g1_ref, wa_ref, wt_ref, gcq_ref, wuq_ref, wuqs_ref, gckv_ref,
                 wuk_ref, wuv_ref, vone_ref, gq_ref, gqs_ref, gk_ref, gks_ref, cos_ref, sin_ref,
                 z_ref, o_ref, vt_ref, gt_ref, q_ref, k_ref, mv_ref):
    x = x_ref[...]
    hn = x * lax.rsqrt(jnp.mean(x * x, axis=-1, keepdims=True) + RMS_EPS) * g1_ref[...]
    hm = (hn * sc_ref[...] + sh_ref[...]).astype(BF16)
    p = _dot(hm, wa_ref[...])
    z_ref[...] = p[:, _OFF_Z:_OFF_Z + MLSTM_INNER].astype(BF16)
    o_ref[...] = p[:, _OFF_O:_OFF_O + MLSTM_INNER].astype(BF16)
    rt = _dot_nt(wt_ref[...], hm)
    vt_ref[...] = rt[:MLSTM_INNER].astype(BF16)
    gt_ref[...] = rt[MLSTM_INNER:]

    cq = p[:, _OFF_CQ:_OFF_CQ + Q_LORA]
    cqn = (cq * lax.rsqrt(jnp.mean(cq * cq, axis=-1, keepdims=True) + RMS_EPS) * gcq_ref[...]).astype(BF16)
    ckv = p[:, _OFF_CKV:_OFF_CKV + KV_LORA]
    ckvn = (ckv * lax.rsqrt(jnp.mean(ckv * ckv, axis=-1, keepdims=True) + RMS_EPS) * gckv_ref[...]).astype(BF16)
    qa = _dot(cqn, wuq_ref[...])
    qb = _dot(cqn, wuqs_ref[...])
    ka = _dot(ckvn, wuk_ref[...])
    mv_ref[...] = (_dot(ckvn, wuv_ref[...]) + vone_ref[...]).astype(BF16)
    kr = p[:, _OFF_KR:_OFF_KR + LANES]
    krs = p[:, _OFF_KRS:_OFF_KRS + LANES]

    cosp = cos_ref[...]
    sinp = sin_ref[...]
    gqc = gq_ref[...] * cosp
    gqs = gqs_ref[...] * sinp
    gkc = gk_ref[...] * cosp
    k_rot = krs * (gks_ref[...] * sinp)
    kr_ss = jnp.sum(kr * kr, axis=-1, keepdims=True)
    q_scale = (QK_DIM ** -0.5) * math.log2(math.e)
    for h in range(MLA_HEADS):
        sl = slice(h * LANES, (h + 1) * LANES)
        qah = qa[:, sl]
        rq = lax.rsqrt(jnp.sum(qah * qah, axis=-1, keepdims=True) * (1.0 / QK_DIM) + RMS_EPS) * q_scale
        q_ref[:, sl] = ((qah * gqc + qb[:, sl] * gqs) * rq).astype(BF16)
        kah = ka[:, sl]
        rk = lax.rsqrt((jnp.sum(kah * kah, axis=-1, keepdims=True) + kr_ss) * (1.0 / QK_DIM) + RMS_EPS)
        k_ref[:, sl] = (((kah + kr) * gkc + k_rot) * rk).astype(BF16)


def _inproj(x2, sc1p, sh1, g1, wa, wt, gcq, wuq, wuqs, gckv, wuk, wuv, vone, gq, gqs, gk, gks, cosp, sinp, S):
    T = x2.shape[0]
    tm = SEQ_TILE
    nt = T // tm
    per_b = S // tm
    hp = MLA_HEADS * LANES
    full = lambda a: pl.BlockSpec(a.shape, lambda i: (0,) * a.ndim)
    row = lambda w: pl.BlockSpec((tm, w), lambda i: (i, 0))
    bmod = pl.BlockSpec((None, 1, D_MODEL), lambda i: (i // per_b, 0, 0))
    out_shapes = (
        jax.ShapeDtypeStruct((T, MLSTM_INNER), BF16),
        jax.ShapeDtypeStruct((T, MLSTM_INNER), BF16),
        jax.ShapeDtypeStruct((nt, MLSTM_INNER, tm), BF16),
        jax.ShapeDtypeStruct((nt, 16, tm), F32),
        jax.ShapeDtypeStruct((T, hp), BF16),
        jax.ShapeDtypeStruct((T, hp), BF16),
        jax.ShapeDtypeStruct((T, hp), BF16),
    )
    out_specs = (
        row(MLSTM_INNER), row(MLSTM_INNER),
        pl.BlockSpec((None, MLSTM_INNER, tm), lambda i: (i, 0, 0)),
        pl.BlockSpec((None, 16, tm), lambda i: (i, 0, 0)),
        row(hp), row(hp), row(hp),
    )
    in_specs = [row(D_MODEL), bmod, bmod, full(g1), full(wa), full(wt), full(gcq), full(wuq), full(wuqs),
                full(gckv), full(wuk), full(wuv), full(vone), full(gq), full(gqs), full(gk), full(gks),
                row(LANES), row(LANES)]
    return pl.pallas_call(
        _inproj_body, out_shape=out_shapes, grid=(nt,), in_specs=in_specs, out_specs=out_specs,
        compiler_params=_params(1), name="inproj",
    )(x2, sc1p, sh1, g1, wa, wt, gcq, wuq, wuqs, gckv, wuk, wuv, vone, gq, gqs, gk, gks, cosp, sinp)


ATTN_HEADS_PER_STEP = 4


def _pairs_loop(n, fn, init):
    c = lax.fori_loop(0, n // 2, lambda jj, c: fn(2 * jj + 1, fn(2 * jj, c)), init)
    return lax.fori_loop(0, n % 2, lambda _, c: fn(n - 1, c), c)


def _attn_body(q_ref, k_ref, v_ref, o_ref, s_scr):
    i = pl.program_id(2)
    t = SEQ_TILE
    nh = ATTN_HEADS_PER_STEP
    qry_pos = lax.broadcasted_iota(I32, (t, t), 0)
    key_pos = lax.broadcasted_iota(I32, (t, t), 1)
    causal = key_pos <= qry_pos
    qs = [q_ref[:, hh * LANES:(hh + 1) * LANES] for hh in range(nh)]

    def scores(j, hh, diag):
        kj = k_ref[pl.ds(pl.multiple_of(j * t, t), t), hh * LANES:(hh + 1) * LANES]
        s = _dot_nt(qs[hh], kj)
        return jnp.where(causal, s, -jnp.inf) if diag else s

    def halves(x):
        return x[:, :LANES], x[:, LANES:]

    def max_pass(j, mx, diag):
        out = []
        for hh in range(nh):
            s = scores(j, hh, diag)
            s_scr[j, hh] = s
            lo, hi = halves(s)
            out.append(jnp.maximum(mx[hh], jnp.maximum(lo, hi)))
        return tuple(out)

    mx = tuple(jnp.full((t, LANES), -jnp.inf, F32) for _ in range(nh))
    mx = _pairs_loop(i, lambda j, c: max_pass(j, c, False), mx)
    mx = max_pass(i, mx, True)
    mb = [jnp.broadcast_to(jnp.max(mx[hh], axis=1, keepdims=True), (t, LANES)) for hh in range(nh)]

    def acc_pass(j, carry, diag):
        row0 = pl.multiple_of(j * t, t)
        out = []
        for hh in range(nh):
            lo, hi = halves(s_scr[j, hh])
            p = jnp.concatenate([jnp.exp2(lo - mb[hh]), jnp.exp2(hi - mb[hh])], axis=1).astype(BF16)
            out.append(carry[hh] + _dot(p, v_ref[pl.ds(row0, t), hh * LANES:(hh + 1) * LANES]))
        return tuple(out)

    init = tuple(jnp.zeros((t, LANES), F32) for _ in range(nh))
    carry = _pairs_loop(i, lambda j, c: acc_pass(j, c, False), init)
    final = acc_pass(i, carry, True)
    outs = [acc / acc[:, V_DIM:V_DIM + 1] for acc in final]
    first_head = lax.broadcasted_iota(I32, (t, LANES), 1) < V_DIM
    o_ref[...] = jnp.concatenate(
        [jnp.where(first_head, outs[2 * pr], pltpu.roll(outs[2 * pr + 1], V_DIM, axis=1)) for pr in range(nh // 2)],
        axis=1).astype(BF16)


def _attn(q, k, mvt, B, S):
    T = q.shape[0]
    t = SEQ_TILE
    nh = ATTN_HEADS_PER_STEP
    per_b = S // t
    return pl.pallas_call(
        _attn_body,
        out_shape=jax.ShapeDtypeStruct((T, MLA_HEADS * V_DIM), BF16),
        grid=(B, MLA_HEADS // nh, per_b),
        in_specs=[
            pl.BlockSpec((t, nh * LANES), lambda b, hp, i: (b * per_b + i, hp)),
            pl.BlockSpec((S, nh * LANES), lambda b, hp, i: (b, hp)),
            pl.BlockSpec((S, nh * LANES), lambda b, hp, i: (b, hp)),
        ],
        out_specs=pl.BlockSpec((t, nh * V_DIM), lambda b, hp, i: (b * per_b + i, hp)),
        scratch_shapes=[pltpu.VMEM((per_b, nh, t, t), F32)],
        compiler_params=_params(3), name="attn",
    )(q, k, mvt)


def _split3(a):
    hi = a.astype(BF16)
    r = a - hi.astype(F32)
    mid = r.astype(BF16)
    lo = (r - mid.astype(F32)).astype(BF16)
    return hi, mid, lo


def _mlstm_body(bi_ref, bf_ref, z_ref, vt_ref, g_ref, og_ref, cw_ref, cb_ref, wq_ref, wk_ref, gout_ref,
                out_ref, zpad, q_s, k_s, st_s, m_s, *, S):
    h = pl.program_id(1)
    L = SEQ_TILE
    nchunk = S // L
    hd = MLSTM_DIM
    zpad[0:8, :] = jnp.zeros((8, hd), F32)
    zpad[8:, :] = z_ref[...].astype(F32)
    zc = cb_ref[...] + jnp.zeros((S, hd), F32)
    for j in range(CONV_WIDTH):
        zc = zc + cw_ref[j:j + 1, :] * zpad[pl.ds(8 - (CONV_WIDTH - 1) + j, S), :]
    zs = _silu(zc).astype(BF16)
    q_s[...] = _dot(zs, wq_ref[...]).astype(BF16)
    k_s[...] = (_dot(zs, wk_ref[...]) * (hd ** -0.5)).astype(BF16)

    st_s[...] = jnp.zeros_like(st_s)
    m_s[...] = jnp.zeros_like(m_s)
    b_i = bi_ref[h]
    b_f = bf_ref[h]
    r_i = lax.broadcasted_iota(I32, (L, L), 0)
    c_i = lax.broadcasted_iota(I32, (L, L), 1)
    tril = r_i >= c_i
    strict_lower = jnp.where(r_i > c_i, 1.0, 0.0).astype(BF16)
    ones_row = jnp.where(lax.broadcasted_iota(I32, (hd, L), 0) == 0, 1.0, 0.0).astype(BF16)

    def chunk(c, _):
        start = pl.multiple_of(c * L, L)
        qc = q_s[pl.ds(start, L), :]
        kc = k_s[pl.ds(start, L), :]
        vt_aug = jnp.concatenate([vt_ref[c], ones_row], axis=0)
        li = g_ref[c, pl.ds(h, 1), :] + b_i
        fp = g_ref[c, pl.ds(MLSTM_HEADS + h, 1), :] + b_f
        lf = jnp.minimum(fp, 0.0) - jnp.log(1.0 + jnp.exp(-jnp.abs(fp)))
        a_mat = jnp.where(tril, lf, 0.0)
        a_hi, a_mid, a_lo = _split3(a_mat)
        d0 = _dot(a_hi, strict_lower) + _dot(a_mid, strict_lower) + _dot(a_lo, strict_lower)
        dmat = jnp.where(tril, d0 + li, -jnp.inf)
        b_col = jnp.sum(a_mat, axis=1, keepdims=True)
        m_prev = m_s[...]
        inter = b_col + m_prev
        mt = jnp.maximum(inter, jnp.max(dmat, axis=1, keepdims=True))
        w = jnp.exp(dmat - mt) * _dot_nt(qc, kc)
        decay = jnp.exp(inter - mt)
        st = st_s[...]
        num_aug = decay * _dot_nt(qc, st.astype(BF16)) + _dot_nt(w.astype(BF16), vt_aug)
        num = num_aug[:, :hd]
        den = num_aug[:, hd:hd + 1]
        hv = num / jnp.maximum(jnp.abs(den), jnp.exp(-mt))
        g = d0[L - 1:L, :] + li
        b_last = b_col[L - 1:L, :]
        m_new = jnp.maximum(b_last + m_prev, jnp.max(g, axis=1, keepdims=True))
        a = jnp.exp(g - m_new)
        cd = jnp.exp(b_last + m_prev - m_new)
        st_s[...] = cd * st + _dot((vt_aug.astype(F32) * a).astype(BF16), kc)
        m_s[...] = m_new
        hn = hv * lax.rsqrt(jnp.mean(hv * hv, axis=-1, keepdims=True) + RMS_EPS) * gout_ref[...]
        og = og_ref[pl.ds(start, L), :].astype(F32)
        out_ref[pl.ds(start, L), :] = (_sigmoid(og) * hn).astype(BF16)
        return 0

    lax.fori_loop(0, nchunk, chunk, 0)


def _mlstm(z, vt3, g3, og, conv_w, conv_b, wq, wk, b_i, b_f, g_out, B, S):
    T = z.shape[0]
    L = SEQ_TILE
    per_b = S // L
    hd = MLSTM_DIM
    smem = pl.BlockSpec(memory_space=pltpu.SMEM)
    seq = pl.BlockSpec((S, hd), lambda b, h: (b, h))
    return pl.pallas_call(
        functools.partial(_mlstm_body, S=S),
        out_shape=jax.ShapeDtypeStruct((T, MLSTM_INNER), BF16),
        grid=(B, MLSTM_HEADS),
        in_specs=[
            smem, smem, seq,
            pl.BlockSpec((per_b, hd, L), lambda b, h: (b, h, 0)),
            pl.BlockSpec((per_b, 16, L), lambda b, h: (b, 0, 0)),
            seq,
            pl.BlockSpec((CONV_WIDTH, hd), lambda b, h: (0, h)),
            pl.BlockSpec((1, hd), lambda b, h: (0, h)),
            pl.BlockSpec((None, hd, hd), lambda b, h: (h, 0, 0)),
            pl.BlockSpec((None, hd, hd), lambda b, h: (h, 0, 0)),
            pl.BlockSpec((1, hd), lambda b, h: (0, h)),
        ],
        out_specs=seq,
        scratch_shapes=[
            pltpu.VMEM((S + 8, hd), F32),
            pltpu.VMEM((S, hd), BF16),
            pltpu.VMEM((S, hd), BF16),
            pltpu.VMEM((2 * hd, hd), F32),
            pltpu.VMEM((1, 1), F32),
        ],
        compiler_params=_params(2), name="mlstm",
    )(b_i, b_f, z, vt3, g3, og, conv_w, conv_b, wq, wk, g_out)


def _post_body(ao_ref, mo_ref, x_ref, gt1_ref, gattn_ref, wout_ref, g2_ref, sc2_ref, sh2_ref, gt2_ref,
               wrt_ref, wgu_ref, wds_ref, base_ref, h2_ref, lg_ref):
    ao = ao_ref[...].astype(F32)
    aon = ao * lax.rsqrt(jnp.mean(ao * ao, axis=-1, keepdims=True) + RMS_EPS) * gattn_ref[...]
    mix_in = jnp.concatenate([aon.astype(BF16), mo_ref[...]], axis=1)
    x1 = x_ref[...] + gt1_ref[...] * _dot(mix_in, wout_ref[...])
    hn = x1 * lax.rsqrt(jnp.mean(x1 * x1, axis=-1, keepdims=True) + RMS_EPS) * g2_ref[...]
    h2 = hn * sc2_ref[...] + sh2_ref[...]
    words = _pack_rows(h2)
    h2_ref[0] = words[:, :HALF]
    h2_ref[1] = words[:, HALF:]
    h2b = h2.astype(BF16)
    gu = _dot(h2b, wgu_ref[...])
    act = (_silu(gu[:, :SHARED_DIM]) * gu[:, SHARED_DIM:]).astype(BF16)
    base_ref[...] = x1 + gt2_ref[...] * _dot(act, wds_ref[...])
    lg_ref[...] = _dot_nt(wrt_ref[...], h2b)


def _post(ao, mo, x2, gt1, gattn, wout, g2, sc2p, sh2, gt2, wrt, wgu, wds, S):
    T = x2.shape[0]
    tm = WIDE_TILE
    per_b = S // tm
    full = lambda a: pl.BlockSpec(a.shape, lambda i: (0,) * a.ndim)
    row = lambda w: pl.BlockSpec((tm, w), lambda i: (i, 0))
    bmod = pl.BlockSpec((None, 1, D_MODEL), lambda i: (i // per_b, 0, 0))
    return pl.pallas_call(
        _post_body,
        out_shape=(jax.ShapeDtypeStruct((T, D_MODEL), F32), jax.ShapeDtypeStruct((2, T, HALF), U32),
                   jax.ShapeDtypeStruct((N_EXPERTS, T), F32)),
        grid=(T // tm,),
        in_specs=[row(MLA_HEADS * V_DIM), row(MLSTM_INNER), row(D_MODEL), bmod, full(gattn), full(wout),
                  full(g2), bmod, bmod, bmod, full(wrt), full(wgu), full(wds)],
        out_specs=(row(D_MODEL), pl.BlockSpec((2, tm, HALF), lambda i: (0, i, 0)),
                   pl.BlockSpec((N_EXPERTS, tm), lambda i: (0, i))),
        compiler_params=_params(1), name="post",
    )(ao, mo, x2, gt1, gattn, wout, g2, sc2p, sh2, gt2, wrt, wgu, wds)


def _first_max(x, rows, n):
    mx = jnp.max(x, axis=0, keepdims=True)
    idx = jnp.min(jnp.where(x == mx, rows, n), axis=0, keepdims=True)
    return mx, idx


def _route_body(lg_ref, bias_ref, idx_ref, w_ref, rank_ref, cnt_ref, carry):
    i = pl.program_id(0)
    tt = lg_ref.shape[1]

    @pl.when(i == 0)
    def _():
        carry[...] = jnp.zeros_like(carry)

    sc = _sigmoid(lg_ref[...])
    bi = sc + bias_ref[...]
    rows = lax.broadcasted_iota(I32, (N_EXPERTS, tt), 0)
    rows_g = lax.broadcasted_iota(I32, (GROUP_SIZE, tt), 0)
    rows_n = lax.broadcasted_iota(I32, (N_GROUPS, tt), 0)
    neg = -jnp.inf
    gs = []
    for g in range(N_GROUPS):
        xg = bi[g * GROUP_SIZE:(g + 1) * GROUP_SIZE]
        m1, i1 = _first_max(xg, rows_g, GROUP_SIZE)
        m2 = jnp.max(jnp.where(rows_g == i1, neg, xg), axis=0, keepdims=True)
        gs.append(m1 + m2)
    cur = jnp.concatenate(gs, axis=0)
    keep = jnp.zeros((N_GROUPS, tt), F32)
    for _ in range(TOPK_GROUPS):
        _, gi = _first_max(cur, rows_n, N_GROUPS)
        sel = rows_n == gi
        keep = jnp.where(sel, 1.0, keep)
        cur = jnp.where(sel, neg, cur)
    masked = jnp.concatenate(
        [jnp.where(keep[g:g + 1] > 0.0, bi[g * GROUP_SIZE:(g + 1) * GROUP_SIZE], neg) for g in range(N_GROUPS)], axis=0)
    cur = masked
    onehot = jnp.zeros((N_EXPERTS, tt), F32)
    idxs, ws = [], []
    for _ in range(TOP_K):
        _, ei = _first_max(cur, rows, N_EXPERTS)
        sel = rows == ei
        ws.append(jnp.sum(jnp.where(sel, sc, 0.0), axis=0, keepdims=True))
        idxs.append(ei)
        cur = jnp.where(sel, neg, cur)
        onehot = jnp.where(sel, 1.0, onehot)
    wsum = ws[0]
    for k in range(1, TOP_K):
        wsum = wsum + ws[k]
    idx_ref[...] = jnp.concatenate(idxs, axis=0)
    w_ref[...] = jnp.concatenate([w / wsum * ROUTED_SCALE for w in ws], axis=0)
    r_i = lax.broadcasted_iota(I32, (tt, tt), 0)
    c_i = lax.broadcasted_iota(I32, (tt, tt), 1)
    before = jnp.where(r_i < c_i, 1.0, 0.0).astype(BF16)
    tot = _dot(onehot.astype(BF16), before) + carry[...]
    rank_ref[...] = jnp.concatenate(
        [jnp.sum(jnp.where(rows == ei, tot, 0.0), axis=0, keepdims=True) for ei in idxs], axis=0).astype(I32)
    carry[...] = carry[...] + jnp.sum(onehot, axis=1, keepdims=True)
    cnt_ref[...] = jnp.broadcast_to(carry[...], cnt_ref.shape)


def _route(lgt, bias_col):
    T = lgt.shape[1]
    tt = WIDE_TILE
    blk = pl.BlockSpec((TOP_K, tt), lambda i: (0, i))
    return pl.pallas_call(
        _route_body,
        out_shape=(jax.ShapeDtypeStruct((TOP_K, T), I32), jax.ShapeDtypeStruct((TOP_K, T), F32),
                   jax.ShapeDtypeStruct((TOP_K, T), I32), jax.ShapeDtypeStruct((N_EXPERTS, LANES), F32)),
        grid=(T // tt,),
        in_specs=[pl.BlockSpec((N_EXPERTS, tt), lambda i: (0, i)), pl.BlockSpec((N_EXPERTS, 1), lambda i: (0, 0))],
        out_specs=(blk, blk, blk, pl.BlockSpec((N_EXPERTS, LANES), lambda i: (0, 0))),
        scratch_shapes=[pltpu.VMEM((N_EXPERTS, 1), F32)],
        compiler_params=_params(1), name="route",
    )(lgt, bias_col)


def _dest_body(idx_ref, rank_ref, off_ref, dest_ref):
    tt = idx_ref.shape[1]
    rows = lax.broadcasted_iota(I32, (N_EXPERTS, tt), 0)
    off = off_ref[...]
    outs = []
    for k in range(TOP_K):
        sel = rows == idx_ref[k:k + 1, :]
        outs.append(jnp.sum(jnp.where(sel, off, 0), axis=0, keepdims=True) + rank_ref[k:k + 1, :])
    dest_ref[...] = jnp.concatenate(outs, axis=0)


def _dest(idx_t, rank_t, off_col):
    T = idx_t.shape[1]
    tt = min(4 * SEQ_TILE, T)
    blk = pl.BlockSpec((TOP_K, tt), lambda i: (0, i))
    return pl.pallas_call(
        _dest_body, out_shape=jax.ShapeDtypeStruct((TOP_K, T), I32), grid=(T // tt,),
        in_specs=[blk, blk, pl.BlockSpec((N_EXPERTS, 1), lambda i: (0, 0))], out_specs=blk,
        compiler_params=_params(1), name="dest",
    )(idx_t, rank_t, off_col)


SC_WINDOW = 128
HALF = PACKED // 2


def _sc_mesh():
    return plsc.VectorSubcoreMesh(core_axis_name="c", subcore_axis_name="s")


def _scatter_rows(dest2_t, rows, n_out):
    n = rows.shape[0]

    @pl.kernel(out_type=jax.ShapeDtypeStruct((n_out, HALF), U32), mesh=_sc_mesh(), scratch_types=[])
    def scatter(d_hbm, x_hbm, o_hbm):
        def body(d_vmem, x_vmem):
            for k in range(TOP_K):
                pltpu.sync_copy(x_vmem, o_hbm.at[d_vmem.at[k]])

        pltpu.emit_pipeline(
            body, grid=(n // SC_WINDOW,),
            in_specs=[pl.BlockSpec((TOP_K, SC_WINDOW), lambda i: (0, i)),
                      pl.BlockSpec((SC_WINDOW, HALF), lambda i: (i, 0))],
            out_specs=[], core_axis_name=("c", "s"), dimension_semantics=(pltpu.PARALLEL,),
        )(d_hbm, x_hbm)

    return scatter(dest2_t, rows)


def _gather_rows(index_row, rows):
    n = index_row.shape[1]

    @pl.kernel(out_type=jax.ShapeDtypeStruct((n, HALF), U32), mesh=_sc_mesh(), scratch_types=[])
    def gather(i_hbm, y_hbm, o_hbm):
        def body(i_vmem, o_vmem):
            pltpu.sync_copy(y_hbm.at[i_vmem.at[0]], o_vmem)

        pltpu.emit_pipeline(
            body, grid=(n // SC_WINDOW,),
            in_specs=[pl.BlockSpec((1, SC_WINDOW), lambda i: (0, i))],
            out_specs=[pl.BlockSpec((SC_WINDOW, HALF), lambda i: (i, 0))],
            core_axis_name=("c", "s"), dimension_semantics=(pltpu.PARALLEL,),
        )(i_hbm, o_hbm)

    return gather(index_row, rows)


EXPERT_RING = 4


def _experts_body(first_ref, nblk_ref, cnt_ref, xs_ref, wg_ref, wu_ref, wd_ref, y_ref,
                  xbuf, ybuf, wgb, wub, wdb, xsem, ysem):
    e = pl.program_id(0)
    last = pl.num_programs(0) - 1
    b0 = first_ref[e]
    nb = nblk_ref[e]
    total = first_ref[last] + nblk_ref[last]
    ring = xbuf.shape[0]

    def rows(g):
        return pl.ds(pl.multiple_of(g * ROW_BLOCK, ROW_BLOCK), ROW_BLOCK)

    def x_copy(g):
        slot = g % ring
        return pltpu.make_async_copy(xs_ref.at[:, rows(g), :], xbuf.at[slot], xsem.at[slot])

    def y_copy(g):
        slot = g % ring
        return pltpu.make_async_copy(ybuf.at[slot], y_ref.at[:, rows(g), :], ysem.at[slot])

    @pl.when(e == 0)
    def _():
        for g in range(ring - 1):
            @pl.when(g < total)
            def _():
                x_copy(g).start()

    @pl.when(nb > 0)
    def _():
        wgb[...] = wg_ref[...].astype(BF16)
        wub[...] = wu_ref[...].astype(BF16)
        wdb[...] = wd_ref[...].astype(BF16)

    def prefetch(g):
        @pl.when(g + ring - 1 < total)
        def _():
            x_copy(g + ring - 1).start()

    def compute(i):
        slot = (b0 + i) % ring
        words = jnp.concatenate([xbuf[slot, 0], xbuf[slot, 1]], axis=1)
        n_live = cnt_ref[e] - i * ROW_BLOCK
        words = jnp.where(lax.broadcasted_iota(I32, words.shape, 0) < n_live, words, U32(0))
        x = jnp.concatenate(_unpack_rows(words), axis=1).astype(BF16)
        act = (_silu(_dot(x, wgb[...])) * _dot(x, wub[...])).astype(BF16)
        y = _pack_rows(_dot(act, wdb[...]))
        ybuf[slot, 0] = y[:, :HALF]
        ybuf[slot, 1] = y[:, HALF:]

    def blocks(i, n):
        g0 = b0 + i
        prefetch(g0)
        for d in range(n):
            x_copy(g0 + d).wait()

            @pl.when(g0 + d >= ring)
            def _():
                y_copy(g0 + d - ring).wait()

        for d in range(n):
            compute(i + d)
        for d in range(1, n):
            prefetch(g0 + d)
        for d in range(n):
            y_copy(g0 + d).start()

    def pair(ii, carry):
        blocks(2 * ii, 2)
        return carry

    def single(_, carry):
        blocks(nb - 1, 1)
        return carry

    lax.fori_loop(0, nb // 2, pair, 0)
    lax.fori_loop(0, nb % 2, single, 0)

    @pl.when(e == last)
    def _():
        for back in range(ring, 0, -1):
            @pl.when(total >= back)
            def _():
                y_copy(total - back).wait()


def _experts(first_block, n_blocks, counts, xs, wg, wu, wd):
    n_rows = xs.shape[1]
    wmap = lambda e, fb, nb, cnt: (e, 0, 0)
    anyspec = pl.BlockSpec(memory_space=pl.ANY)
    ring = pltpu.VMEM((EXPERT_RING, 2, ROW_BLOCK, HALF), U32)
    grid_spec = pltpu.PrefetchScalarGridSpec(
        num_scalar_prefetch=3, grid=(N_EXPERTS,),
        in_specs=[
            anyspec,
            pl.BlockSpec((None, D_MODEL, EXPERT_DIM), wmap),
            pl.BlockSpec((None, D_MODEL, EXPERT_DIM), wmap),
            pl.BlockSpec((None, EXPERT_DIM, D_MODEL), wmap),
        ],
        out_specs=anyspec,
        scratch_shapes=[ring, ring,
                        pltpu.VMEM((D_MODEL, EXPERT_DIM), BF16), pltpu.VMEM((D_MODEL, EXPERT_DIM), BF16),
                        pltpu.VMEM((EXPERT_DIM, D_MODEL), BF16),
                        pltpu.SemaphoreType.DMA((EXPERT_RING,)), pltpu.SemaphoreType.DMA((EXPERT_RING,))],
    )
    return pl.pallas_call(
        _experts_body, out_shape=jax.ShapeDtypeStruct((2, n_rows, HALF), U32), grid_spec=grid_spec,
        compiler_params=_params(1), name="experts",
    )(first_block, n_blocks, counts, xs, wg, wu, wd)


COMBINE_PARTS = 1


def _combine_body(*refs):
    ya_refs = refs[:TOP_K]
    yb_refs = refs[TOP_K:2 * TOP_K]
    base_ref, w_ref, gt2_ref = refs[2 * TOP_K:2 * TOP_K + 3]
    out_ref = refs[-1]
    w = w_ref[...]
    r_lo = r_hi = None
    for k in range(TOP_K):
        lo, hi = _unpack_rows(jnp.concatenate([ya_refs[k][...], yb_refs[k][...]], axis=1))
        wk = w[:, k:k + 1]
        r_lo = wk * lo if r_lo is None else r_lo + wk * lo
        r_hi = wk * hi if r_hi is None else r_hi + wk * hi
    out_ref[...] = base_ref[...] + gt2_ref[...] * jnp.concatenate([r_lo, r_hi], axis=1)


def _combine(y_slots, base, w_tk, gt2, S, part, prev_out):
    T = base.shape[0]
    tt = WIDE_TILE
    nt = T // tt // COMBINE_PARTS
    t0 = part * nt
    per_b = S // tt
    yspec = lambda half, k: pl.BlockSpec((tt, HALF), lambda i: ((half * TOP_K + k) * nt + i, 0))
    in_specs = [yspec(half, k) for half in range(2) for k in range(TOP_K)] + [
        pl.BlockSpec((tt, D_MODEL), lambda i: (t0 + i, 0)),
        pl.BlockSpec((tt, TOP_K), lambda i: (t0 + i, 0)),
        pl.BlockSpec((None, 1, D_MODEL), lambda i: ((t0 + i) // per_b, 0, 0)),
    ]
    args = [y_slots] * (2 * TOP_K) + [base, w_tk, gt2]
    aliases = {}
    if prev_out is not None:
        in_specs.append(pl.BlockSpec(memory_space=pl.ANY))
        args.append(prev_out)
        aliases = {len(args) - 1: 0}
    return pl.pallas_call(
        _combine_body, out_shape=jax.ShapeDtypeStruct((T, D_MODEL), F32), grid=(nt,),
        in_specs=in_specs,
        out_specs=pl.BlockSpec((tt, D_MODEL), lambda i: (t0 + i, 0)),
        input_output_aliases=aliases,
        compiler_params=_params(1), name=f"combine{part}",
    )(*args)


def _pad_heads(w, width):
    kdim = w.shape[0]
    w3 = w.reshape(kdim, MLA_HEADS, width)
    return jnp.pad(w3, ((0, 0), (0, 0), (0, LANES - width))).reshape(kdim, MLA_HEADS * LANES)


def _rot_partner(w_rope):
    half = QK_ROPE // 2
    return jnp.concatenate([-w_rope[..., half:], w_rope[..., :half]], axis=-1)


def _lane_vec(nope, rope):
    return jnp.concatenate([nope, rope, jnp.zeros((LANES - QK_DIM,), F32)]).reshape(1, LANES)


def _layer(x2, ada, positions, B, S, g_norm1, w_in, g_cq, w_uq, g_ckv, w_ukv, g_qn, g_kn, g_attn_out,
           conv_w, conv_b, w_mq, w_mk, b_igate, b_fgate, g_mlstm_out, w_out, g_norm2, w_router, router_bias,
           w_gate_exp, w_up_exp, w_down_exp, w_gate_sh, w_up_sh, w_down_sh):
    T = B * S
    sh1, sc1, gt1, sh2, sc2, gt2 = [a.reshape(B, 1, D_MODEL) for a in jnp.split(ada, 6, axis=-1)]
    row = lambda v: v.reshape(1, -1).astype(F32)

    inv_freq = ROPE_THETA ** (-jnp.arange(0, QK_ROPE, 2, dtype=F32) / QK_ROPE)
    ang = positions.astype(F32).reshape(T, 1) * inv_freq
    half = QK_ROPE // 2
    freq_lane = jnp.arange(half)[:, None]
    lane = jnp.arange(LANES)[None, :]
    place = ((lane == QK_NOPE + freq_lane) | (lane == QK_NOPE + half + freq_lane)).astype(F32)
    spread = lambda tbl: jnp.dot(tbl, place, precision=lax.Precision.HIGHEST)
    cosp = (jnp.arange(LANES) < QK_NOPE).astype(F32)[None, :] + spread(jnp.cos(ang))
    sinp = spread(jnp.sin(ang))

    o0 = 0
    w_cq = w_in[:, o0:o0 + Q_LORA]; o0 += Q_LORA
    w_ckv = w_in[:, o0:o0 + KV_LORA]; o0 += KV_LORA
    w_kr = w_in[:, o0:o0 + QK_ROPE]; o0 += QK_ROPE
    w_z = w_in[:, o0:o0 + MLSTM_INNER]; o0 += MLSTM_INNER
    w_v = w_in[:, o0:o0 + MLSTM_INNER]; o0 += MLSTM_INNER
    w_o = w_in[:, o0:o0 + MLSTM_INNER]; o0 += MLSTM_INNER
    w_i = w_in[:, o0:o0 + MLSTM_HEADS]; o0 += MLSTM_HEADS
    w_f = w_in[:, o0:o0 + MLSTM_HEADS]
    zl = jnp.zeros((D_MODEL, QK_NOPE), F32)
    zr = jnp.zeros((D_MODEL, LANES - QK_DIM), F32)
    wa = jnp.concatenate([w_cq, w_ckv, w_z, w_o, zl, w_kr, zr, zl, _rot_partner(w_kr), zr], axis=1).astype(BF16)
    wt = jnp.concatenate([w_v.T, w_i.T, w_f.T, jnp.zeros((16 - 2 * MLSTM_HEADS, D_MODEL), F32)], axis=0).astype(BF16)

    uq3 = w_uq.reshape(Q_LORA, MLA_HEADS, QK_DIM)
    uq_rot = jnp.concatenate([jnp.zeros((Q_LORA, MLA_HEADS, QK_NOPE), F32), _rot_partner(uq3[..., QK_NOPE:])], axis=-1)
    wuq = _pad_heads(w_uq, QK_DIM).astype(BF16)
    wuqs = _pad_heads(uq_rot.reshape(Q_LORA, MLA_HEADS * QK_DIM), QK_DIM).astype(BF16)
    ukv3 = w_ukv.reshape(KV_LORA, MLA_HEADS, QK_NOPE + V_DIM)
    wuk = _pad_heads(ukv3[..., :QK_NOPE].reshape(KV_LORA, MLA_HEADS * QK_NOPE), QK_NOPE).astype(BF16)
    wuv = _pad_heads(ukv3[..., QK_NOPE:].reshape(KV_LORA, MLA_HEADS * V_DIM), V_DIM).astype(BF16)
    vone = jnp.tile((jnp.arange(LANES) == V_DIM).astype(F32), MLA_HEADS).reshape(1, MLA_HEADS * LANES)
    gq = _lane_vec(g_qn[:QK_NOPE], g_qn[QK_NOPE:])
    gqs = _lane_vec(jnp.zeros((QK_NOPE,), F32), _rot_partner(g_qn[QK_NOPE:]) * jnp.concatenate(
        [-jnp.ones((QK_ROPE // 2,), F32), jnp.ones((QK_ROPE // 2,), F32)]))
    gk = _lane_vec(g_kn[:QK_NOPE], g_kn[QK_NOPE:])
    gks = _lane_vec(jnp.zeros((QK_NOPE,), F32), _rot_partner(g_kn[QK_NOPE:]) * jnp.concatenate(
        [-jnp.ones((QK_ROPE // 2,), F32), jnp.ones((QK_ROPE // 2,), F32)]))

    z, og, vt3, g3, q, k, mv = _inproj(
        x2, 1.0 + sc1, sh1, row(g_norm1), wa, wt, row(g_cq), wuq, wuqs, row(g_ckv), wuk, wuv, vone,
        gq, gqs, gk, gks, cosp, sinp, S)
    attn_o = _attn(q, k, mv, B, S)
    mlstm_o = _mlstm(z, vt3, g3, og, conv_w.reshape(CONV_WIDTH, MLSTM_INNER), row(conv_b),
                     w_mq.astype(BF16), w_mk.astype(BF16), b_igate, b_fgate, row(g_mlstm_out), B, S)

    wgu = jnp.concatenate([w_gate_sh, w_up_sh], axis=1).astype(BF16)
    base, h2, lgt = _post(attn_o, mlstm_o, x2, gt1, row(g_attn_out), w_out.astype(BF16), row(g_norm2),
                          1.0 + sc2, sh2, gt2, w_router.T.astype(BF16), wgu, w_down_sh.astype(BF16), S)

    idx_t, w_t, rank_t, cnt = _route(lgt, router_bias.reshape(N_EXPERTS, 1))
    counts = cnt[:, 0].astype(I32)
    padded = (counts + ROW_BLOCK - 1) // ROW_BLOCK * ROW_BLOCK
    padded_end = jnp.cumsum(padded)
    off = padded_end - padded
    n_rows = T * TOP_K + N_EXPERTS * ROW_BLOCK
    dest_t = _dest(idx_t, rank_t, off.reshape(N_EXPERTS, 1))

    dest2_t = jnp.concatenate([dest_t, dest_t + n_rows], axis=1)
    xs = _scatter_rows(dest2_t, h2.reshape(2 * T, HALF), 2 * n_rows).reshape(2, n_rows, HALF)
    y = _experts(off // ROW_BLOCK, padded // ROW_BLOCK, counts, xs, w_gate_exp, w_up_exp, w_down_exp)
    y_rows = y.reshape(2 * n_rows, HALF)
    w_tk = w_t.T
    tp = T // COMBINE_PARTS
    out = None
    for part in range(COMBINE_PARTS):
        dest_p = dest_t[:, part * tp:(part + 1) * tp].reshape(1, TOP_K * tp)
        y_slots = _gather_rows(jnp.concatenate([dest_p, dest_p + n_rows], axis=1), y_rows)
        out = _combine(y_slots, base, w_tk, gt2, S, part, out)
    return out


def kernel(x, c, positions, w_ada, b_ada, g_norm1, w_in, g_cq, w_uq, g_ckv, w_ukv, g_qn, g_kn, g_attn_out, conv_w, conv_b, w_mq, w_mk, b_igate, b_fgate, g_mlstm_out, w_out, g_norm2, w_router, router_bias, w_gate_exp, w_up_exp, w_down_exp, w_gate_sh, w_up_sh, w_down_sh):
    B, S, _ = x.shape
    depth = w_ada.shape[0]
    x2 = x.reshape(B * S, D_MODEL)
    for l in range(depth):
        ada = _ada(c, w_ada[l], b_ada[l])
        x2 = _layer(x2, ada, positions, B, S, g_norm1[l], w_in[l], g_cq[l], w_uq[l], g_ckv[l], w_ukv[l], g_qn[l],
                    g_kn[l], g_attn_out[l], conv_w[l], conv_b[l], w_mq[l], w_mk[l], b_igate[l], b_fgate[l],
                    g_mlstm_out[l], w_out[l], g_norm2[l], w_router[l], router_bias[l], w_gate_exp[l], w_up_exp[l],
                    w_down_exp[l], w_gate_sh[l], w_up_sh[l], w_down_sh[l])
    return x2.reshape(B, S, D_MODEL)
```

```python
import functools
import math

import jax
import jax.numpy as jnp
from jax import lax
from jax.experimental import pallas as pl
from jax.experimental.pallas import tpu as pltpu
from jax.experimental.pallas import tpu_sc as plsc

F32 = jnp.float32
BF16 = jnp.bfloat16
I32 = jnp.int32

D_MODEL = 1024
MLA_HEADS = 8
QK_NOPE = 64
QK_ROPE = 32
QK_DIM = QK_NOPE + QK_ROPE
V_DIM = 64
Q_LORA = 384
KV_LORA = 256
ROPE_THETA = 10000.0
MLSTM_HEADS = 4
MLSTM_DIM = 128
MLSTM_INNER = MLSTM_HEADS * MLSTM_DIM
CONV_WIDTH = 4
N_EXPERTS = 256
TOP_K = 8
N_GROUPS = 8
TOPK_GROUPS = 4
GROUP_SIZE = N_EXPERTS // N_GROUPS
EXPERT_DIM = 256
SHARED_DIM = 256
ROUTED_SCALE = 2.5
RMS_EPS = 1e-6

LANES = 128
SEQ_TILE = 256
WIDE_TILE = 512
ROW_BLOCK = 256
VMEM_LIMIT = 56 * 1024 * 1024

_OFF_CQ = 0
_OFF_CKV = _OFF_CQ + Q_LORA
_OFF_Z = _OFF_CKV + KV_LORA
_OFF_O = _OFF_Z + MLSTM_INNER
_OFF_KR = _OFF_O + MLSTM_INNER
_OFF_KRS = _OFF_KR + LANES
_WA_COLS = _OFF_KRS + LANES
_WT_ROWS = MLSTM_INNER + 16


def _params(n_axes, vmem=VMEM_LIMIT):
    return pltpu.CompilerParams(dimension_semantics=("arbitrary",) * n_axes, vmem_limit_bytes=vmem)


def _dot(a, b):
    return jnp.dot(a, b, preferred_element_type=F32)


def _dot_nt(a, b):
    return lax.dot_general(a, b, (((1,), (1,)), ((), ())), preferred_element_type=F32)


def _sigmoid(x):
    return 1.0 / (1.0 + jnp.exp(-x))


def _silu(x):
    return x * _sigmoid(x)


U32 = jnp.uint32
PACKED = D_MODEL // 2


def _pack_rows(x):
    w = x.shape[1] // 2
    lo = lax.bitcast_convert_type(x[:, :w].astype(BF16).astype(F32), U32)
    hi = lax.bitcast_convert_type(x[:, w:].astype(BF16).astype(F32), U32)
    return lax.shift_right_logical(lo, U32(16)) | (hi & U32(0xFFFF0000))


def _unpack_rows(p):
    lo = lax.bitcast_convert_type(lax.shift_left(p, U32(16)), F32)
    hi = lax.bitcast_convert_type(p & U32(0xFFFF0000), F32)
    return lo, hi


def _ada_body(c_ref, w_ref, b_ref, o_ref):
    c = c_ref[...]
    cond = _silu(c).astype(BF16)
    o_ref[...] = _dot(cond, w_ref[...].astype(BF16)) + b_ref[...]


def _ada(c, w_ada, b_ada):
    B = c.shape[0]
    n = w_ada.shape[1]
    tn = 1024
    return pl.pallas_call(
        _ada_body,
        out_shape=jax.ShapeDtypeStruct((B, n), F32),
        grid=(n // tn,),
        in_specs=[
            pl.BlockSpec((B, D_MODEL), lambda j: (0, 0)),
            pl.BlockSpec((D_MODEL, tn), lambda j: (0, j)),
            pl.BlockSpec((1, tn), lambda j: (0, j)),
        ],
        out_specs=pl.BlockSpec((B, tn), lambda j: (0, j)),
        compiler_params=_params(1),
        name="ada",
    )(c, w_ada, b_ada.reshape(1, n))


def _inproj_body(x_ref, sc_ref, sh_ref, g1_ref, wa_ref, wt_ref, gcq_ref, wuq_ref, wuqs_ref, gckv_ref,
                 wuk_ref, wuv_ref, vone_ref, gq_ref, gqs_ref, gk_ref, gks_ref, cos_ref, sin_ref,
                 z_ref, o_ref, vt_ref, gt_ref, q_ref, k_ref, mv_ref):
    x = x_ref[...]
    hn = x * lax.rsqrt(jnp.mean(x * x, axis=-1, keepdims=True) + RMS_EPS) * g1_ref[...]
    hm = (hn * sc_ref[...] + sh_ref[...]).astype(BF16)
    p = _dot(hm, wa_ref[...])
    z_ref[...] = p[:, _OFF_Z:_OFF_Z + MLSTM_INNER].astype(BF16)
    o_ref[...] = p[:, _OFF_O:_OFF_O + MLSTM_INNER].astype(BF16)
    rt = _dot_nt(wt_ref[...], hm)
    vt_ref[...] = rt[:MLSTM_INNER].astype(BF16)
    gt_ref[...] = rt[MLSTM_INNER:]

    cq = p[:, _OFF_CQ:_OFF_CQ + Q_LORA]
    cqn = (cq * lax.rsqrt(jnp.mean(cq * cq, axis=-1, keepdims=True) + RMS_EPS) * gcq_ref[...]).astype(BF16)
    ckv = p[:, _OFF_CKV:_OFF_CKV + KV_LORA]
    ckvn = (ckv * lax.rsqrt(jnp.mean(ckv * ckv, axis=-1, keepdims=True) + RMS_EPS) * gckv_ref[...]).astype(BF16)
    qa = _dot(cqn, wuq_ref[...])
    qb = _dot(cqn, wuqs_ref[...])
    ka = _dot(ckvn, wuk_ref[...])
    mv_ref[...] = (_dot(ckvn, wuv_ref[...]) + vone_ref[...]).astype(BF16)
    kr = p[:, _OFF_KR:_OFF_KR + LANES]
    krs = p[:, _OFF_KRS:_OFF_KRS + LANES]

    cosp = cos_ref[...]
    sinp = sin_ref[...]
    gqc = gq_ref[...] * cosp
    gqs = gqs_ref[...] * sinp
    gkc = gk_ref[...] * cosp
    k_rot = krs * (gks_ref[...] * sinp)
    kr_ss = jnp.sum(kr * kr, axis=-1, keepdims=True)
    q_scale = (QK_DIM ** -0.5) * math.log2(math.e)
    for h in range(MLA_HEADS):
        sl = slice(h * LANES, (h + 1) * LANES)
        qah = qa[:, sl]
        rq = lax.rsqrt(jnp.sum(qah * qah, axis=-1, keepdims=True) * (1.0 / QK_DIM) + RMS_EPS) * q_scale
        q_ref[:, sl] = ((qah * gqc + qb[:, sl] * gqs) * rq).astype(BF16)
        kah = ka[:, sl]
        rk = lax.rsqrt((jnp.sum(kah * kah, axis=-1, keepdims=True) + kr_ss) * (1.0 / QK_DIM) + RMS_EPS)
        k_ref[:, sl] = (((kah + kr) * gkc + k_rot) * rk).astype(BF16)


def _inproj(x2, sc1p, sh1, g1, wa, wt, gcq, wuq, wuqs, gckv, wuk, wuv, vone, gq, gqs, gk, gks, cosp, sinp, S):
    T = x2.shape[0]
    tm = SEQ_TILE
    nt = T // tm
    per_b = S // tm
    hp = MLA_HEADS * LANES
    full = lambda a: pl.BlockSpec(a.shape, lambda i: (0,) * a.ndim)
    row = lambda w: pl.BlockSpec((tm, w), lambda i: (i, 0))
    bmod = pl.BlockSpec((None, 1, D_MODEL), lambda i: (i // per_b, 0, 0))
    out_shapes = (
        jax.ShapeDtypeStruct((T, MLSTM_INNER), BF16),
        jax.ShapeDtypeStruct((T, MLSTM_INNER), BF16),
        jax.ShapeDtypeStruct((nt, MLSTM_INNER, tm), BF16),
        jax.ShapeDtypeStruct((nt, 16, tm), F32),
        jax.ShapeDtypeStruct((T, hp), BF16),
        jax.ShapeDtypeStruct((T, hp), BF16),
        jax.ShapeDtypeStruct((T, hp), BF16),
    )
    out_specs = (
        row(MLSTM_INNER), row(MLSTM_INNER),
        pl.BlockSpec((None, MLSTM_INNER, tm), lambda i: (i, 0, 0)),
        pl.BlockSpec((None, 16, tm), lambda i: (i, 0, 0)),
        row(hp), row(hp), row(hp),
    )
    in_specs = [row(D_MODEL), bmod, bmod, full(g1), full(wa), full(wt), full(gcq), full(wuq), full(wuqs),
                full(gckv), full(wuk), full(wuv), full(vone), full(gq), full(gqs), full(gk), full(gks),
                row(LANES), row(LANES)]
    return pl.pallas_call(
        _inproj_body, out_shape=out_shapes, grid=(nt,), in_specs=in_specs, out_specs=out_specs,
        compiler_params=_params(1), name="inproj",
    )(x2, sc1p, sh1, g1, wa, wt, gcq, wuq, wuqs, gckv, wuk, wuv, vone, gq, gqs, gk, gks, cosp, sinp)


ATTN_HEADS_PER_STEP = 4


def _pairs_loop(n, fn, init):
    c = lax.fori_loop(0, n // 2, lambda jj, c: fn(2 * jj + 1, fn(2 * jj, c)), init)
    return lax.fori_loop(0, n % 2, lambda _, c: fn(n - 1, c), c)


def _attn_body(q_ref, k_ref, v_ref, o_ref, s_scr):
    i = pl.program_id(2)
    t = SEQ_TILE
    nh = ATTN_HEADS_PER_STEP
    qry_pos = lax.broadcasted_iota(I32, (t, t), 0)
    key_pos = lax.broadcasted_iota(I32, (t, t), 1)
    causal = key_pos <= qry_pos
    qs = [q_ref[:, hh * LANES:(hh + 1) * LANES] for hh in range(nh)]

    def scores(j, hh, diag):
        kj = k_ref[pl.ds(pl.multiple_of(j * t, t), t), hh * LANES:(hh + 1) * LANES]
        s = _dot_nt(qs[hh], kj)
        return jnp.where(causal, s, -jnp.inf) if diag else s

    def halves(x):
        return x[:, :LANES], x[:, LANES:]

    def max_pass(j, mx, diag):
        out = []
        for hh in range(nh):
            s = scores(j, hh, diag)
            s_scr[j, hh] = s
            lo, hi = halves(s)
            out.append(jnp.maximum(mx[hh], jnp.maximum(lo, hi)))
        return tuple(out)

    mx = tuple(jnp.full((t, LANES), -jnp.inf, F32) for _ in range(nh))
    mx = _pairs_loop(i, lambda j, c: max_pass(j, c, False), mx)
    mx = max_pass(i, mx, True)
    mb = [jnp.broadcast_to(jnp.max(mx[hh], axis=1, keepdims=True), (t, LANES)) for hh in range(nh)]

    def acc_pass(j, carry, diag):
        row0 = pl.multiple_of(j * t, t)
        out = []
        for hh in range(nh):
            lo, hi = halves(s_scr[j, hh])
            p = jnp.concatenate([jnp.exp2(lo - mb[hh]), jnp.exp2(hi - mb[hh])], axis=1).astype(BF16)
            out.append(carry[hh] + _dot(p, v_ref[pl.ds(row0, t), hh * LANES:(hh + 1) * LANES]))
        return tuple(out)

    init = tuple(jnp.zeros((t, LANES), F32) for _ in range(nh))
    carry = _pairs_loop(i, lambda j, c: acc_pass(j, c, False), init)
    final = acc_pass(i, carry, True)
    outs = [acc / acc[:, V_DIM:V_DIM + 1] for acc in final]
    first_head = lax.broadcasted_iota(I32, (t, LANES), 1) < V_DIM
    o_ref[...] = jnp.concatenate(
        [jnp.where(first_head, outs[2 * pr], pltpu.roll(outs[2 * pr + 1], V_DIM, axis=1)) for pr in range(nh // 2)],
        axis=1).astype(BF16)


def _attn(q, k, mvt, B, S):
    T = q.shape[0]
    t = SEQ_TILE
    nh = ATTN_HEADS_PER_STEP
    per_b = S // t
    return pl.pallas_call(
        _attn_body,
        out_shape=jax.ShapeDtypeStruct((T, MLA_HEADS * V_DIM), BF16),
        grid=(B, MLA_HEADS // nh, per_b),
        in_specs=[
            pl.BlockSpec((t, nh * LANES), lambda b, hp, i: (b * per_b + i, hp)),
            pl.BlockSpec((S, nh * LANES), lambda b, hp, i: (b, hp)),
            pl.BlockSpec((S, nh * LANES), lambda b, hp, i: (b, hp)),
        ],
        out_specs=pl.BlockSpec((t, nh * V_DIM), lambda b, hp, i: (b * per_b + i, hp)),
        scratch_shapes=[pltpu.VMEM((per_b, nh, t, t), F32)],
        compiler_params=_params(3), name="attn",
    )(q, k, mvt)


def _split3(a):
    hi = a.astype(BF16)
    r = a - hi.astype(F32)
    mid = r.astype(BF16)
    lo = (r - mid.astype(F32)).astype(BF16)
    return hi, mid, lo


def _mlstm_body(bi_ref, bf_ref, z_ref, vt_ref, g_ref, og_ref, cw_ref, cb_ref, wq_ref, wk_ref, gout_ref,
                out_ref, zpad, q_s, k_s, st_s, m_s, *, S):
    h = pl.program_id(1)
    L = SEQ_TILE
    nchunk = S // L
    hd = MLSTM_DIM
    zpad[0:8, :] = jnp.zeros((8, hd), F32)
    zpad[8:, :] = z_ref[...].astype(F32)
    zc = cb_ref[...] + jnp.zeros((S, hd), F32)
    for j in range(CONV_WIDTH):
        zc = zc + cw_ref[j:j + 1, :] * zpad[pl.ds(8 - (CONV_WIDTH - 1) + j, S), :]
    zs = _silu(zc).astype(BF16)
    q_s[...] = _dot(zs, wq_ref[...]).astype(BF16)
    k_s[...] = (_dot(zs, wk_ref[...]) * (hd ** -0.5)).astype(BF16)

    st_s[...] = jnp.zeros_like(st_s)
    m_s[...] = jnp.zeros_like(m_s)
    b_i = bi_ref[h]
    b_f = bf_ref[h]
    r_i = lax.broadcasted_iota(I32, (L, L), 0)
    c_i = lax.broadcasted_iota(I32, (L, L), 1)
    tril = r_i >= c_i
    strict_lower = jnp.where(r_i > c_i, 1.0, 0.0).astype(BF16)
    ones_row = jnp.where(lax.broadcasted_iota(I32, (hd, L), 0) == 0, 1.0, 0.0).astype(BF16)

    def chunk(c, _):
        start = pl.multiple_of(c * L, L)
        qc = q_s[pl.ds(start, L), :]
        kc = k_s[pl.ds(start, L), :]
        vt_aug = jnp.concatenate([vt_ref[c], ones_row], axis=0)
        li = g_ref[c, pl.ds(h, 1), :] + b_i
        fp = g_ref[c, pl.ds(MLSTM_HEADS + h, 1), :] + b_f
        lf = jnp.minimum(fp, 0.0) - jnp.log(1.0 + jnp.exp(-jnp.abs(fp)))
        a_mat = jnp.where(tril, lf, 0.0)
        a_hi, a_mid, a_lo = _split3(a_mat)
        d0 = _dot(a_hi, strict_lower) + _dot(a_mid, strict_lower) + _dot(a_lo, strict_lower)
        dmat = jnp.where(tril, d0 + li, -jnp.inf)
        b_col = jnp.sum(a_mat, axis=1, keepdims=True)
        m_prev = m_s[...]
        inter = b_col + m_prev
        mt = jnp.maximum(inter, jnp.max(dmat, axis=1, keepdims=True))
        w = jnp.exp(dmat - mt) * _dot_nt(qc, kc)
        decay = jnp.exp(inter - mt)
        st = st_s[...]
        num_aug = decay * _dot_nt(qc, st.astype(BF16)) + _dot_nt(w.astype(BF16), vt_aug)
        num = num_aug[:, :hd]
        den = num_aug[:, hd:hd + 1]
        hv = num / jnp.maximum(jnp.abs(den), jnp.exp(-mt))
        g = d0[L - 1:L, :] + li
        b_last = b_col[L - 1:L, :]
        m_new = jnp.maximum(b_last + m_prev, jnp.max(g, axis=1, keepdims=True))
        a = jnp.exp(g - m_new)
        cd = jnp.exp(b_last + m_prev - m_new)
        st_s[...] = cd * st + _dot((vt_aug.astype(F32) * a).astype(BF16), kc)
        m_s[...] = m_new
        hn = hv * lax.rsqrt(jnp.mean(hv * hv, axis=-1, keepdims=True) + RMS_EPS) * gout_ref[...]
        og = og_ref[pl.ds(start, L), :].astype(F32)
        out_ref[pl.ds(start, L), :] = (_sigmoid(og) * hn).astype(BF16)
        return 0

    lax.fori_loop(0, nchunk, chunk, 0)


def _mlstm(z, vt3, g3, og, conv_w, conv_b, wq, wk, b_i, b_f, g_out, B, S):
    T = z.shape[0]
    L = SEQ_TILE
    per_b = S // L
    hd = MLSTM_DIM
    smem = pl.BlockSpec(memory_space=pltpu.SMEM)
    seq = pl.BlockSpec((S, hd), lambda b, h: (b, h))
    return pl.pallas_call(
        functools.partial(_mlstm_body, S=S),
        out_shape=jax.ShapeDtypeStruct((T, MLSTM_INNER), BF16),
        grid=(B, MLSTM_HEADS),
        in_specs=[
            smem, smem, seq,
            pl.BlockSpec((per_b, hd, L), lambda b, h: (b, h, 0)),
            pl.BlockSpec((per_b, 16, L), lambda b, h: (b, 0, 0)),
            seq,
            pl.BlockSpec((CONV_WIDTH, hd), lambda b, h: (0, h)),
            pl.BlockSpec((1, hd), lambda b, h: (0, h)),
            pl.BlockSpec((None, hd, hd), lambda b, h: (h, 0, 0)),
            pl.BlockSpec((None, hd, hd), lambda b, h: (h, 0, 0)),
            pl.BlockSpec((1, hd), lambda b, h: (0, h)),
        ],
        out_specs=seq,
        scratch_shapes=[
            pltpu.VMEM((S + 8, hd), F32),
            pltpu.VMEM((S, hd), BF16),
            pltpu.VMEM((S, hd), BF16),
            pltpu.VMEM((2 * hd, hd), F32),
            pltpu.VMEM((1, 1), F32),
        ],
        compiler_params=_params(2), name="mlstm",
    )(b_i, b_f, z, vt3, g3, og, conv_w, conv_b, wq, wk, g_out)


def _post_body(ao_ref, mo_ref, x_ref, gt1_ref, gattn_ref, wout_ref, g2_ref, sc2_ref, sh2_ref, gt2_ref,
               wrt_ref, wgu_ref, wds_ref, base_ref, h2_ref, lg_ref):
    ao = ao_ref[...].astype(F32)
    aon = ao * lax.rsqrt(jnp.mean(ao * ao, axis=-1, keepdims=True) + RMS_EPS) * gattn_ref[...]
    mix_in = jnp.concatenate([aon.astype(BF16), mo_ref[...]], axis=1)
    x1 = x_ref[...] + gt1_ref[...] * _dot(mix_in, wout_ref[...])
    hn = x1 * lax.rsqrt(jnp.mean(x1 * x1, axis=-1, keepdims=True) + RMS_EPS) * g2_ref[...]
    h2 = hn * sc2_ref[...] + sh2_ref[...]
    words = _pack_rows(h2)
    h2_ref[0] = words[:, :HALF]
    h2_ref[1] = words[:, HALF:]
    h2b = h2.astype(BF16)
    gu = _dot(h2b, wgu_ref[...])
    act = (_silu(gu[:, :SHARED_DIM]) * gu[:, SHARED_DIM:]).astype(BF16)
    base_ref[...] = x1 + gt2_ref[...] * _dot(act, wds_ref[...])
    lg_ref[...] = _dot_nt(wrt_ref[...], h2b)


def _post(ao, mo, x2, gt1, gattn, wout, g2, sc2p, sh2, gt2, wrt, wgu, wds, S):
    T = x2.shape[0]
    tm = WIDE_TILE
    per_b = S // tm
    full = lambda a: pl.BlockSpec(a.shape, lambda i: (0,) * a.ndim)
    row = lambda w: pl.BlockSpec((tm, w), lambda i: (i, 0))
    bmod = pl.BlockSpec((None, 1, D_MODEL), lambda i: (i // per_b, 0, 0))
    return pl.pallas_call(
        _post_body,
        out_shape=(jax.ShapeDtypeStruct((T, D_MODEL), F32), jax.ShapeDtypeStruct((2, T, HALF), U32),
                   jax.ShapeDtypeStruct((N_EXPERTS, T), F32)),
        grid=(T // tm,),
        in_specs=[row(MLA_HEADS * V_DIM), row(MLSTM_INNER), row(D_MODEL), bmod, full(gattn), full(wout),
                  full(g2), bmod, bmod, bmod, full(wrt), full(wgu), full(wds)],
        out_specs=(row(D_MODEL), pl.BlockSpec((2, tm, HALF), lambda i: (0, i, 0)),
                   pl.BlockSpec((N_EXPERTS, tm), lambda i: (0, i))),
        compiler_params=_params(1), name="post",
    )(ao, mo, x2, gt1, gattn, wout, g2, sc2p, sh2, gt2, wrt, wgu, wds)


def _first_max(x, rows, n):
    mx = jnp.max(x, axis=0, keepdims=True)
    idx = jnp.min(jnp.where(x == mx, rows, n), axis=0, keepdims=True)
    return mx, idx


def _route_body(lg_ref, bias_ref, idx_ref, w_ref, rank_ref, cnt_ref, carry):
    i = pl.program_id(0)
    tt = lg_ref.shape[1]

    @pl.when(i == 0)
    def _():
        carry[...] = jnp.zeros_like(carry)

    sc = _sigmoid(lg_ref[...])
    bi = sc + bias_ref[...]
    rows = lax.broadcasted_iota(I32, (N_EXPERTS, tt), 0)
    rows_g = lax.broadcasted_iota(I32, (GROUP_SIZE, tt), 0)
    rows_n = lax.broadcasted_iota(I32, (N_GROUPS, tt), 0)
    neg = -jnp.inf
    gs = []
    for g in range(N_GROUPS):
        xg = bi[g * GROUP_SIZE:(g + 1) * GROUP_SIZE]
        m1, i1 = _first_max(xg, rows_g, GROUP_SIZE)
        m2 = jnp.max(jnp.where(rows_g == i1, neg, xg), axis=0, keepdims=True)
        gs.append(m1 + m2)
    cur = jnp.concatenate(gs, axis=0)
    keep = jnp.zeros((N_GROUPS, tt), F32)
    for _ in range(TOPK_GROUPS):
        _, gi = _first_max(cur, rows_n, N_GROUPS)
        sel = rows_n == gi
        keep = jnp.where(sel, 1.0, keep)
        cur = jnp.where(sel, neg, cur)
    masked = jnp.concatenate(
        [jnp.where(keep[g:g + 1] > 0.0, bi[g * GROUP_SIZE:(g + 1) * GROUP_SIZE], neg) for g in range(N_GROUPS)], axis=0)
    cur = masked
    onehot = jnp.zeros((N_EXPERTS, tt), F32)
    idxs, ws = [], []
    for _ in range(TOP_K):
        _, ei = _first_max(cur, rows, N_EXPERTS)
        sel = rows == ei
        ws.append(jnp.sum(jnp.where(sel, sc, 0.0), axis=0, keepdims=True))
        idxs.append(ei)
        cur = jnp.where(sel, neg, cur)
        onehot = jnp.where(sel, 1.0, onehot)
    wsum = ws[0]
    for k in range(1, TOP_K):
        wsum = wsum + ws[k]
    idx_ref[...] = jnp.concatenate(idxs, axis=0)
    w_ref[...] = jnp.concatenate([w / wsum * ROUTED_SCALE for w in ws], axis=0)
    r_i = lax.broadcasted_iota(I32, (tt, tt), 0)
    c_i = lax.broadcasted_iota(I32, (tt, tt), 1)
    before = jnp.where(r_i < c_i, 1.0, 0.0).astype(BF16)
    tot = _dot(onehot.astype(BF16), before) + carry[...]
    rank_ref[...] = jnp.concatenate(
        [jnp.sum(jnp.where(rows == ei, tot, 0.0), axis=0, keepdims=True) for ei in idxs], axis=0).astype(I32)
    carry[...] = carry[...] + jnp.sum(onehot, axis=1, keepdims=True)
    cnt_ref[...] = jnp.broadcast_to(carry[...], cnt_ref.shape)


def _route(lgt, bias_col):
    T = lgt.shape[1]
    tt = WIDE_TILE
    blk = pl.BlockSpec((TOP_K, tt), lambda i: (0, i))
    return pl.pallas_call(
        _route_body,
        out_shape=(jax.ShapeDtypeStruct((TOP_K, T), I32), jax.ShapeDtypeStruct((TOP_K, T), F32),
                   jax.ShapeDtypeStruct((TOP_K, T), I32), jax.ShapeDtypeStruct((N_EXPERTS, LANES), F32)),
        grid=(T // tt,),
        in_specs=[pl.BlockSpec((N_EXPERTS, tt), lambda i: (0, i)), pl.BlockSpec((N_EXPERTS, 1), lambda i: (0, 0))],
        out_specs=(blk, blk, blk, pl.BlockSpec((N_EXPERTS, LANES), lambda i: (0, 0))),
        scratch_shapes=[pltpu.VMEM((N_EXPERTS, 1), F32)],
        compiler_params=_params(1), name="route",
    )(lgt, bias_col)


def _dest_body(idx_ref, rank_ref, off_ref, dest_ref):
    tt = idx_ref.shape[1]
    rows = lax.broadcasted_iota(I32, (N_EXPERTS, tt), 0)
    off = off_ref[...]
    outs = []
    for k in range(TOP_K):
        sel = rows == idx_ref[k:k + 1, :]
        outs.append(jnp.sum(jnp.where(sel, off, 0), axis=0, keepdims=True) + rank_ref[k:k + 1, :])
    dest_ref[...] = jnp.concatenate(outs, axis=0)


def _dest(idx_t, rank_t, off_col):
    T = idx_t.shape[1]
    tt = min(4 * SEQ_TILE, T)
    blk = pl.BlockSpec((TOP_K, tt), lambda i: (0, i))
    return pl.pallas_call(
        _dest_body, out_shape=jax.ShapeDtypeStruct((TOP_K, T), I32), grid=(T // tt,),
        in_specs=[blk, blk, pl.BlockSpec((N_EXPERTS, 1), lambda i: (0, 0))], out_specs=blk,
        compiler_params=_params(1), name="dest",
    )(idx_t, rank_t, off_col)


SC_WINDOW = 128
HALF = PACKED // 2


def _sc_mesh():
    return plsc.VectorSubcoreMesh(core_axis_name="c", subcore_axis_name="s")


def _scatter_rows(dest2_t, rows, n_out):
    n = rows.shape[0]

    @pl.kernel(out_type=jax.ShapeDtypeStruct((n_out, HALF), U32), mesh=_sc_mesh(), scratch_types=[])
    def scatter(d_hbm, x_hbm, o_hbm):
        def body(d_vmem, x_vmem):
            for k in range(TOP_K):
                pltpu.sync_copy(x_vmem, o_hbm.at[d_vmem.at[k]])

        pltpu.emit_pipeline(
            body, grid=(n // SC_WINDOW,),
            in_specs=[pl.BlockSpec((TOP_K, SC_WINDOW), lambda i: (0, i)),
                      pl.BlockSpec((SC_WINDOW, HALF), lambda i: (i, 0))],
            out_specs=[], core_axis_name=("c", "s"), dimension_semantics=(pltpu.PARALLEL,),
        )(d_hbm, x_hbm)

    return scatter(dest2_t, rows)


def _gather_rows(index_row, rows):
    n = index_row.shape[1]

    @pl.kernel(out_type=jax.ShapeDtypeStruct((n, HALF), U32), mesh=_sc_mesh(), scratch_types=[])
    def gather(i_hbm, y_hbm, o_hbm):
        def body(i_vmem, o_vmem):
            pltpu.sync_copy(y_hbm.at[i_vmem.at[0]], o_vmem)

        pltpu.emit_pipeline(
            body, grid=(n // SC_WINDOW,),
            in_specs=[pl.BlockSpec((1, SC_WINDOW), lambda i: (0, i))],
            out_specs=[pl.BlockSpec((SC_WINDOW, HALF), lambda i: (i, 0))],
            core_axis_name=("c", "s"), dimension_semantics=(pltpu.PARALLEL,),
        )(i_hbm, o_hbm)

    return gather(index_row, rows)


EXPERT_RING = 4
WEIGHT_SPLIT = 4


def _experts_body(first_ref, nblk_ref, cnt_ref, xs_ref, *refs):
    ws = WEIGHT_SPLIT
    wg_refs, wu_refs, wd_refs = refs[:ws], refs[ws:2 * ws], refs[2 * ws:3 * ws]
    y_ref, xbuf, ybuf, wgb, wub, wdb, xsem, ysem = refs[3 * ws:]
    e = pl.program_id(0)
    last = pl.num_programs(0) - 1
    b0 = first_ref[e]
    nb = nblk_ref[e]
    total = first_ref[last] + nblk_ref[last]
    ring = xbuf.shape[0]

    def rows(g):
        return pl.ds(pl.multiple_of(g * ROW_BLOCK, ROW_BLOCK), ROW_BLOCK)

    def x_copy(g):
        slot = g % ring
        return pltpu.make_async_copy(xs_ref.at[:, rows(g), :], xbuf.at[slot], xsem.at[slot])

    def y_copy(g):
        slot = g % ring
        return pltpu.make_async_copy(ybuf.at[slot], y_ref.at[:, rows(g), :], ysem.at[slot])

    @pl.when(e == 0)
    def _():
        for g in range(ring - 1):
            @pl.when(g < total)
            def _():
                x_copy(g).start()

    @pl.when(nb > 0)
    def _():
        kp = D_MODEL // ws
        for p in range(ws):
            wgb[p * kp:(p + 1) * kp, :] = wg_refs[p][...].astype(BF16)
            wub[p * kp:(p + 1) * kp, :] = wu_refs[p][...].astype(BF16)
            wdb[:, p * kp:(p + 1) * kp] = wd_refs[p][...].astype(BF16)

    def prefetch(g):
        @pl.when(g + ring - 1 < total)
        def _():
            x_copy(g + ring - 1).start()

    def compute(i):
        slot = (b0 + i) % ring
        words = jnp.concatenate([xbuf[slot, 0], xbuf[slot, 1]], axis=1)
        n_live = cnt_ref[e] - i * ROW_BLOCK
        words = jnp.where(lax.broadcasted_iota(I32, words.shape, 0) < n_live, words, U32(0))
        x = jnp.concatenate(_unpack_rows(words), axis=1).astype(BF16)
        act = (_silu(_dot(x, wgb[...])) * _dot(x, wub[...])).astype(BF16)
        y = _pack_rows(_dot(act, wdb[...]))
        ybuf[slot, 0] = y[:, :HALF]
        ybuf[slot, 1] = y[:, HALF:]

    def blocks(i, n):
        g0 = b0 + i
        prefetch(g0)
        for d in range(n):
            x_copy(g0 + d).wait()

            @pl.when(g0 + d >= ring)
            def _():
                y_copy(g0 + d - ring).wait()

        for d in range(n):
            compute(i + d)
        for d in range(1, n):
            prefetch(g0 + d)
        for d in range(n):
            y_copy(g0 + d).start()

    def pair(ii, carry):
        blocks(2 * ii, 2)
        return carry

    def single(_, carry):
        blocks(nb - 1, 1)
        return carry

    lax.fori_loop(0, nb // 2, pair, 0)
    lax.fori_loop(0, nb % 2, single, 0)

    @pl.when(e == last)
    def _():
        for back in range(ring, 0, -1):
            @pl.when(total >= back)
            def _():
                y_copy(total - back).wait()


def _experts(first_block, n_blocks, counts, xs, wg, wu, wd):
    n_rows = xs.shape[1]
    ws = WEIGHT_SPLIT
    kp = D_MODEL // ws
    in_rows = lambda p: pl.BlockSpec((None, kp, EXPERT_DIM), lambda e, fb, nb, cnt: (e, p, 0))
    out_cols = lambda p: pl.BlockSpec((None, EXPERT_DIM, kp), lambda e, fb, nb, cnt: (e, 0, p))
    anyspec = pl.BlockSpec(memory_space=pl.ANY)
    ring = pltpu.VMEM((EXPERT_RING, 2, ROW_BLOCK, HALF), U32)
    grid_spec = pltpu.PrefetchScalarGridSpec(
        num_scalar_prefetch=3, grid=(N_EXPERTS,),
        in_specs=([anyspec] + [in_rows(p) for p in range(ws)] + [in_rows(p) for p in range(ws)]
                  + [out_cols(p) for p in range(ws)]),
        out_specs=anyspec,
        scratch_shapes=[ring, ring,
                        pltpu.VMEM((D_MODEL, EXPERT_DIM), BF16), pltpu.VMEM((D_MODEL, EXPERT_DIM), BF16),
                        pltpu.VMEM((EXPERT_DIM, D_MODEL), BF16),
                        pltpu.SemaphoreType.DMA((EXPERT_RING,)), pltpu.SemaphoreType.DMA((EXPERT_RING,))],
    )
    return pl.pallas_call(
        _experts_body, out_shape=jax.ShapeDtypeStruct((2, n_rows, HALF), U32), grid_spec=grid_spec,
        compiler_params=_params(1), name="experts",
    )(first_block, n_blocks, counts, xs, *([wg] * ws + [wu] * ws + [wd] * ws))


COMBINE_PARTS = 1


def _combine_body(*refs):
    ya_refs = refs[:TOP_K]
    yb_refs = refs[TOP_K:2 * TOP_K]
    base_ref, w_ref, gt2_ref = refs[2 * TOP_K:2 * TOP_K + 3]
    out_ref = refs[-1]
    w = w_ref[...]
    r_lo = r_hi = None
    for k in range(TOP_K):
        lo, hi = _unpack_rows(jnp.concatenate([ya_refs[k][...], yb_refs[k][...]], axis=1))
        wk = w[:, k:k + 1]
        r_lo = wk * lo if r_lo is None else r_lo + wk * lo
        r_hi = wk * hi if r_hi is None else r_hi + wk * hi
    out_ref[...] = base_ref[...] + gt2_ref[...] * jnp.concatenate([r_lo, r_hi], axis=1)


def _combine(y_slots, base, w_tk, gt2, S, part, prev_out):
    T = base.shape[0]
    tt = WIDE_TILE
    nt = T // tt // COMBINE_PARTS
    t0 = part * nt
    per_b = S // tt
    yspec = lambda half, k: pl.BlockSpec((tt, HALF), lambda i: ((half * TOP_K + k) * nt + i, 0))
    in_specs = [yspec(half, k) for half in range(2) for k in range(TOP_K)] + [
        pl.BlockSpec((tt, D_MODEL), lambda i: (t0 + i, 0)),
        pl.BlockSpec((tt, TOP_K), lambda i: (t0 + i, 0)),
        pl.BlockSpec((None, 1, D_MODEL), lambda i: ((t0 + i) // per_b, 0, 0)),
    ]
    args = [y_slots] * (2 * TOP_K) + [base, w_tk, gt2]
    aliases = {}
    if prev_out is not None:
        in_specs.append(pl.BlockSpec(memory_space=pl.ANY))
        args.append(prev_out)
        aliases = {len(args) - 1: 0}
    return pl.pallas_call(
        _combine_body, out_shape=jax.ShapeDtypeStruct((T, D_MODEL), F32), grid=(nt,),
        in_specs=in_specs,
        out_specs=pl.BlockSpec((tt, D_MODEL), lambda i: (t0 + i, 0)),
        input_output_aliases=aliases,
        compiler_params=_params(1), name=f"combine{part}",
    )(*args)


def _pad_heads(w, width):
    kdim = w.shape[0]
    w3 = w.reshape(kdim, MLA_HEADS, width)
    return jnp.pad(w3, ((0, 0), (0, 0), (0, LANES - width))).reshape(kdim, MLA_HEADS * LANES)


def _rot_partner(w_rope):
    half = QK_ROPE // 2
    return jnp.concatenate([-w_rope[..., half:], w_rope[..., :half]], axis=-1)


def _lane_vec(nope, rope):
    return jnp.concatenate([nope, rope, jnp.zeros((LANES - QK_DIM,), F32)]).reshape(1, LANES)


def _layer(x2, ada, positions, B, S, g_norm1, w_in, g_cq, w_uq, g_ckv, w_ukv, g_qn, g_kn, g_attn_out,
           conv_w, conv_b, w_mq, w_mk, b_igate, b_fgate, g_mlstm_out, w_out, g_norm2, w_router, router_bias,
           w_gate_exp, w_up_exp, w_down_exp, w_gate_sh, w_up_sh, w_down_sh):
    T = B * S
    sh1, sc1, gt1, sh2, sc2, gt2 = [a.reshape(B, 1, D_MODEL) for a in jnp.split(ada, 6, axis=-1)]
    row = lambda v: v.reshape(1, -1).astype(F32)

    inv_freq = ROPE_THETA ** (-jnp.arange(0, QK_ROPE, 2, dtype=F32) / QK_ROPE)
    ang = positions.astype(F32).reshape(T, 1) * inv_freq
    half = QK_ROPE // 2
    freq_lane = jnp.arange(half)[:, None]
    lane = jnp.arange(LANES)[None, :]
    place = ((lane == QK_NOPE + freq_lane) | (lane == QK_NOPE + half + freq_lane)).astype(F32)
    spread = lambda tbl: jnp.dot(tbl, place, precision=lax.Precision.HIGHEST)
    cosp = (jnp.arange(LANES) < QK_NOPE).astype(F32)[None, :] + spread(jnp.cos(ang))
    sinp = spread(jnp.sin(ang))

    o0 = 0
    w_cq = w_in[:, o0:o0 + Q_LORA]; o0 += Q_LORA
    w_ckv = w_in[:, o0:o0 + KV_LORA]; o0 += KV_LORA
    w_kr = w_in[:, o0:o0 + QK_ROPE]; o0 += QK_ROPE
    w_z = w_in[:, o0:o0 + MLSTM_INNER]; o0 += MLSTM_INNER
    w_v = w_in[:, o0:o0 + MLSTM_INNER]; o0 += MLSTM_INNER
    w_o = w_in[:, o0:o0 + MLSTM_INNER]; o0 += MLSTM_INNER
    w_i = w_in[:, o0:o0 + MLSTM_HEADS]; o0 += MLSTM_HEADS
    w_f = w_in[:, o0:o0 + MLSTM_HEADS]
    zl = jnp.zeros((D_MODEL, QK_NOPE), F32)
    zr = jnp.zeros((D_MODEL, LANES - QK_DIM), F32)
    wa = jnp.concatenate([w_cq, w_ckv, w_z, w_o, zl, w_kr, zr, zl, _rot_partner(w_kr), zr], axis=1).astype(BF16)
    wt = jnp.concatenate([w_v.T, w_i.T, w_f.T, jnp.zeros((16 - 2 * MLSTM_HEADS, D_MODEL), F32)], axis=0).astype(BF16)

    uq3 = w_uq.reshape(Q_LORA, MLA_HEADS, QK_DIM)
    uq_rot = jnp.concatenate([jnp.zeros((Q_LORA, MLA_HEADS, QK_NOPE), F32), _rot_partner(uq3[..., QK_NOPE:])], axis=-1)
    wuq = _pad_heads(w_uq, QK_DIM).astype(BF16)
    wuqs = _pad_heads(uq_rot.reshape(Q_LORA, MLA_HEADS * QK_DIM), QK_DIM).astype(BF16)
    ukv3 = w_ukv.reshape(KV_LORA, MLA_HEADS, QK_NOPE + V_DIM)
    wuk = _pad_heads(ukv3[..., :QK_NOPE].reshape(KV_LORA, MLA_HEADS * QK_NOPE), QK_NOPE).astype(BF16)
    wuv = _pad_heads(ukv3[..., QK_NOPE:].reshape(KV_LORA, MLA_HEADS * V_DIM), V_DIM).astype(BF16)
    vone = jnp.tile((jnp.arange(LANES) == V_DIM).astype(F32), MLA_HEADS).reshape(1, MLA_HEADS * LANES)
    gq = _lane_vec(g_qn[:QK_NOPE], g_qn[QK_NOPE:])
    gqs = _lane_vec(jnp.zeros((QK_NOPE,), F32), _rot_partner(g_qn[QK_NOPE:]) * jnp.concatenate(
        [-jnp.ones((QK_ROPE // 2,), F32), jnp.ones((QK_ROPE // 2,), F32)]))
    gk = _lane_vec(g_kn[:QK_NOPE], g_kn[QK_NOPE:])
    gks = _lane_vec(jnp.zeros((QK_NOPE,), F32), _rot_partner(g_kn[QK_NOPE:]) * jnp.concatenate(
        [-jnp.ones((QK_ROPE // 2,), F32), jnp.ones((QK_ROPE // 2,), F32)]))

    z, og, vt3, g3, q, k, mv = _inproj(
        x2, 1.0 + sc1, sh1, row(g_norm1), wa, wt, row(g_cq), wuq, wuqs, row(g_ckv), wuk, wuv, vone,
        gq, gqs, gk, gks, cosp, sinp, S)
    attn_o = _attn(q, k, mv, B, S)
    mlstm_o = _mlstm(z, vt3, g3, og, conv_w.reshape(CONV_WIDTH, MLSTM_INNER), row(conv_b),
                     w_mq.astype(BF16), w_mk.astype(BF16), b_igate, b_fgate, row(g_mlstm_out), B, S)

    wgu = jnp.concatenate([w_gate_sh, w_up_sh], axis=1).astype(BF16)
    base, h2, lgt = _post(attn_o, mlstm_o, x2, gt1, row(g_attn_out), w_out.astype(BF16), row(g_norm2),
                          1.0 + sc2, sh2, gt2, w_router.T.astype(BF16), wgu, w_down_sh.astype(BF16), S)

    idx_t, w_t, rank_t, cnt = _route(lgt, router_bias.reshape(N_EXPERTS, 1))
    counts = cnt[:, 0].astype(I32)
    padded = (counts + ROW_BLOCK - 1) // ROW_BLOCK * ROW_BLOCK
    padded_end = jnp.cumsum(padded)
    off = padded_end - padded
    n_rows = T * TOP_K + N_EXPERTS * ROW_BLOCK
    dest_t = _dest(idx_t, rank_t, off.reshape(N_EXPERTS, 1))

    dest2_t = jnp.concatenate([dest_t, dest_t + n_rows], axis=1)
    xs = _scatter_rows(dest2_t, h2.reshape(2 * T, HALF), 2 * n_rows).reshape(2, n_rows, HALF)
    y = _experts(off // ROW_BLOCK, padded // ROW_BLOCK, counts, xs, w_gate_exp, w_up_exp, w_down_exp)
    y_rows = y.reshape(2 * n_rows, HALF)
    w_tk = w_t.T
    tp = T // COMBINE_PARTS
    out = None
    for part in range(COMBINE_PARTS):
        dest_p = dest_t[:, part * tp:(part + 1) * tp].reshape(1, TOP_K * tp)
        y_slots = _gather_rows(jnp.concatenate([dest_p, dest_p + n_rows], axis=1), y_rows)
        out = _combine(y_slots, base, w_tk, gt2, S, part, out)
    return out


def kernel(x, c, positions, w_ada, b_ada, g_norm1, w_in, g_cq, w_uq, g_ckv, w_ukv, g_qn, g_kn, g_attn_out, conv_w, conv_b, w_mq, w_mk, b_igate, b_fgate, g_mlstm_out, w_out, g_norm2, w_router, router_bias, w_gate_exp, w_up_exp, w_down_exp, w_gate_sh, w_up_sh, w_down_sh):
    B, S, _ = x.shape
    depth = w_ada.shape[0]
    x2 = x.reshape(B * S, D_MODEL)
    for l in range(depth):
        ada = _ada(c, w_ada[l], b_ada[l])
        x2 = _layer(x2, ada, positions, B, S, g_norm1[l], w_in[l], g_cq[l], w_uq[l], g_ckv[l], w_ukv[l], g_qn[l],
                    g_kn[l], g_attn_out[l], conv_w[l], conv_b[l], w_mq[l], w_mk[l], b_igate[l], b_fgate[l],
                    g_mlstm_out[l], w_out[l], g_norm2[l], w_router[l], router_bias[l], w_gate_exp[l], w_up_exp[l],
                    w_down_exp[l], w_gate_sh[l], w_up_sh[l], w_down_sh[l])
    return x2.reshape(B, S, D_MODEL)
```

```python
import functools
import math

import jax
import jax.numpy as jnp
from jax import lax
from jax.experimental import pallas as pl
from jax.experimental.pallas import tpu as pltpu
from jax.experimental.pallas import tpu_sc as plsc

F32 = jnp.float32
BF16 = jnp.bfloat16
I32 = jnp.int32

D_MODEL = 1024
MLA_HEADS = 8
QK_NOPE = 64
QK_ROPE = 32
QK_DIM = QK_NOPE + QK_ROPE
V_DIM = 64
Q_LORA = 384
KV_LORA = 256
ROPE_THETA = 10000.0
MLSTM_HEADS = 4
MLSTM_DIM = 128
MLSTM_INNER = MLSTM_HEADS * MLSTM_DIM
CONV_WIDTH = 4
N_EXPERTS = 256
TOP_K = 8
N_GROUPS = 8
TOPK_GROUPS = 4
GROUP_SIZE = N_EXPERTS // N_GROUPS
EXPERT_DIM = 256
SHARED_DIM = 256
ROUTED_SCALE = 2.5
RMS_EPS = 1e-6

LANES = 128
SEQ_TILE = 256
WIDE_TILE = 512
ROW_BLOCK = 256
VMEM_LIMIT = 56 * 1024 * 1024

_OFF_CQ = 0
_OFF_CKV = _OFF_CQ + Q_LORA
_OFF_Z = _OFF_CKV + KV_LORA
_OFF_O = _OFF_Z + MLSTM_INNER
_OFF_KR = _OFF_O + MLSTM_INNER
_OFF_KRS = _OFF_KR + LANES
_WA_COLS = _OFF_KRS + LANES
_WT_ROWS = MLSTM_INNER + 16


def _params(n_axes, vmem=VMEM_LIMIT):
    return pltpu.CompilerParams(dimension_semantics=("arbitrary",) * n_axes, vmem_limit_bytes=vmem)


def _dot(a, b):
    return jnp.dot(a, b, preferred_element_type=F32)


def _dot_nt(a, b):
    return lax.dot_general(a, b, (((1,), (1,)), ((), ())), preferred_element_type=F32)


def _sigmoid(x):
    return 1.0 / (1.0 + jnp.exp(-x))


def _silu(x):
    return x * _sigmoid(x)


U32 = jnp.uint32
PACKED = D_MODEL // 2


def _pack_rows(x):
    w = x.shape[1] // 2
    lo = lax.bitcast_convert_type(x[:, :w].astype(BF16).astype(F32), U32)
    hi = lax.bitcast_convert_type(x[:, w:].astype(BF16).astype(F32), U32)
    return lax.shift_right_logical(lo, U32(16)) | (hi & U32(0xFFFF0000))


def _unpack_rows(p):
    lo = lax.bitcast_convert_type(lax.shift_left(p, U32(16)), F32)
    hi = lax.bitcast_convert_type(p & U32(0xFFFF0000), F32)
    return lo, hi


def _ada_body(c_ref, w_ref, b_ref, o_ref):
    c = c_ref[...]
    cond = _silu(c).astype(BF16)
    o_ref[...] = _dot(cond, w_ref[...].astype(BF16)) + b_ref[...]


def _ada(c, w_ada, b_ada):
    B = c.shape[0]
    n = w_ada.shape[1]
    tn = 1024
    return pl.pallas_call(
        _ada_body,
        out_shape=jax.ShapeDtypeStruct((B, n), F32),
        grid=(n // tn,),
        in_specs=[
            pl.BlockSpec((B, D_MODEL), lambda j: (0, 0)),
            pl.BlockSpec((D_MODEL, tn), lambda j: (0, j)),
            pl.BlockSpec((1, tn), lambda j: (0, j)),
        ],
        out_specs=pl.BlockSpec((B, tn), lambda j: (0, j)),
        compiler_params=_params(1),
        name="ada",
    )(c, w_ada, b_ada.reshape(1, n))


def _inproj_body(x_ref, sc_ref, sh_ref, g1_ref, wa_ref, wt_ref, gcq_ref, wuq_ref, wuqs_ref, gckv_ref,
                 wuk_ref, wuv_ref, vone_ref, gq_ref, gqs_ref, gk_ref, gks_ref, cos_ref, sin_ref,
                 z_ref, o_ref, vt_ref, gt_ref, q_ref, k_ref, mv_ref):
    x = x_ref[...]
    hn = x * lax.rsqrt(jnp.mean(x * x, axis=-1, keepdims=True) + RMS_EPS) * g1_ref[...]
    hm = (hn * sc_ref[...] + sh_ref[...]).astype(BF16)
    p = _dot(hm, wa_ref[...])
    z_ref[...] = p[:, _OFF_Z:_OFF_Z + MLSTM_INNER].astype(BF16)
    o_ref[...] = p[:, _OFF_O:_OFF_O + MLSTM_INNER].astype(BF16)
    rt = _dot_nt(wt_ref[...], hm)
    vt_ref[...] = rt[:MLSTM_INNER].astype(BF16)
    gt_ref[...] = rt[MLSTM_INNER:]

    cq = p[:, _OFF_CQ:_OFF_CQ + Q_LORA]
    cqn = (cq * lax.rsqrt(jnp.mean(cq * cq, axis=-1, keepdims=True) + RMS_EPS) * gcq_ref[...]).astype(BF16)
    ckv = p[:, _OFF_CKV:_OFF_CKV + KV_LORA]
    ckvn = (ckv * lax.rsqrt(jnp.mean(ckv * ckv, axis=-1, keepdims=True) + RMS_EPS) * gckv_ref[...]).astype(BF16)
    qa = _dot(cqn, wuq_ref[...])
    qb = _dot(cqn, wuqs_ref[...])
    ka = _dot(ckvn, wuk_ref[...])
    mv_ref[...] = (_dot(ckvn, wuv_ref[...]) + vone_ref[...]).astype(BF16)
    kr = p[:, _OFF_KR:_OFF_KR + LANES]
    krs = p[:, _OFF_KRS:_OFF_KRS + LANES]

    cosp = cos_ref[...]
    sinp = sin_ref[...]
    gqc = gq_ref[...] * cosp
    gqs = gqs_ref[...] * sinp
    gkc = gk_ref[...] * cosp
    k_rot = krs * (gks_ref[...] * sinp)
    kr_ss = jnp.sum(kr * kr, axis=-1, keepdims=True)
    q_scale = (QK_DIM ** -0.5) * math.log2(math.e)
    for h in range(MLA_HEADS):
        sl = slice(h * LANES, (h + 1) * LANES)
        qah = qa[:, sl]
        rq = lax.rsqrt(jnp.sum(qah * qah, axis=-1, keepdims=True) * (1.0 / QK_DIM) + RMS_EPS) * q_scale
        q_ref[:, sl] = ((qah * gqc + qb[:, sl] * gqs) * rq).astype(BF16)
        kah = ka[:, sl]
        rk = lax.rsqrt((jnp.sum(kah * kah, axis=-1, keepdims=True) + kr_ss) * (1.0 / QK_DIM) + RMS_EPS)
        k_ref[:, sl] = (((kah + kr) * gkc + k_rot) * rk).astype(BF16)


def _inproj(x2, sc1p, sh1, g1, wa, wt, gcq, wuq, wuqs, gckv, wuk, wuv, vone, gq, gqs, gk, gks, cosp, sinp, S):
    T = x2.shape[0]
    tm = SEQ_TILE
    nt = T // tm
    per_b = S // tm
    hp = MLA_HEADS * LANES
    full = lambda a: pl.BlockSpec(a.shape, lambda i: (0,) * a.ndim)
    row = lambda w: pl.BlockSpec((tm, w), lambda i: (i, 0))
    bmod = pl.BlockSpec((None, 1, D_MODEL), lambda i: (i // per_b, 0, 0))
    out_shapes = (
        jax.ShapeDtypeStruct((T, MLSTM_INNER), BF16),
        jax.ShapeDtypeStruct((T, MLSTM_INNER), BF16),
        jax.ShapeDtypeStruct((nt, MLSTM_INNER, tm), BF16),
        jax.ShapeDtypeStruct((nt, 16, tm), F32),
        jax.ShapeDtypeStruct((T, hp), BF16),
        jax.ShapeDtypeStruct((T, hp), BF16),
        jax.ShapeDtypeStruct((T, hp), BF16),
    )
    out_specs = (
        row(MLSTM_INNER), row(MLSTM_INNER),
        pl.BlockSpec((None, MLSTM_INNER, tm), lambda i: (i, 0, 0)),
        pl.BlockSpec((None, 16, tm), lambda i: (i, 0, 0)),
        row(hp), row(hp), row(hp),
    )
    in_specs = [row(D_MODEL), bmod, bmod, full(g1), full(wa), full(wt), full(gcq), full(wuq), full(wuqs),
                full(gckv), full(wuk), full(wuv), full(vone), full(gq), full(gqs), full(gk), full(gks),
                row(LANES), row(LANES)]
    return pl.pallas_call(
        _inproj_body, out_shape=out_shapes, grid=(nt,), in_specs=in_specs, out_specs=out_specs,
        compiler_params=_params(1), name="inproj",
    )(x2, sc1p, sh1, g1, wa, wt, gcq, wuq, wuqs, gckv, wuk, wuv, vone, gq, gqs, gk, gks, cosp, sinp)


ATTN_HEADS_PER_STEP = 4


def _pairs_loop(n, fn, init):
    c = lax.fori_loop(0, n // 2, lambda jj, c: fn(2 * jj + 1, fn(2 * jj, c)), init)
    return lax.fori_loop(0, n % 2, lambda _, c: fn(n - 1, c), c)


def _attn_body(q_ref, k_ref, v_ref, o_ref, s_scr):
    i = pl.program_id(2)
    t = SEQ_TILE
    nh = ATTN_HEADS_PER_STEP
    qry_pos = lax.broadcasted_iota(I32, (t, t), 0)
    key_pos = lax.broadcasted_iota(I32, (t, t), 1)
    causal = key_pos <= qry_pos
    qs = [q_ref[:, hh * LANES:(hh + 1) * LANES] for hh in range(nh)]

    def scores(j, hh, diag):
        kj = k_ref[pl.ds(pl.multiple_of(j * t, t), t), hh * LANES:(hh + 1) * LANES]
        s = _dot_nt(qs[hh], kj)
        return jnp.where(causal, s, -jnp.inf) if diag else s

    def halves(x):
        return x[:, :LANES], x[:, LANES:]

    def max_pass(j, mx, diag):
        out = []
        for hh in range(nh):
            s = scores(j, hh, diag)
            s_scr[j, hh] = s
            lo, hi = halves(s)
            out.append(jnp.maximum(mx[hh], jnp.maximum(lo, hi)))
        return tuple(out)

    mx = tuple(jnp.full((t, LANES), -jnp.inf, F32) for _ in range(nh))
    mx = _pairs_loop(i, lambda j, c: max_pass(j, c, False), mx)
    mx = max_pass(i, mx, True)
    mb = [jnp.broadcast_to(jnp.max(mx[hh], axis=1, keepdims=True), (t, LANES)) for hh in range(nh)]

    def acc_pass(j, carry, diag):
        row0 = pl.multiple_of(j * t, t)
        out = []
        for hh in range(nh):
            lo, hi = halves(s_scr[j, hh])
            p = jnp.concatenate([jnp.exp2(lo - mb[hh]), jnp.exp2(hi - mb[hh])], axis=1).astype(BF16)
            out.append(carry[hh] + _dot(p, v_ref[pl.ds(row0, t), hh * LANES:(hh + 1) * LANES]))
        return tuple(out)

    init = tuple(jnp.zeros((t, LANES), F32) for _ in range(nh))
    carry = _pairs_loop(i, lambda j, c: acc_pass(j, c, False), init)
    final = acc_pass(i, carry, True)
    outs = [acc / acc[:, V_DIM:V_DIM + 1] for acc in final]
    first_head = lax.broadcasted_iota(I32, (t, LANES), 1) < V_DIM
    o_ref[...] = jnp.concatenate(
        [jnp.where(first_head, outs[2 * pr], pltpu.roll(outs[2 * pr + 1], V_DIM, axis=1)) for pr in range(nh // 2)],
        axis=1).astype(BF16)


def _attn(q, k, mvt, B, S):
    T = q.shape[0]
    t = SEQ_TILE
    nh = ATTN_HEADS_PER_STEP
    per_b = S // t
    return pl.pallas_call(
        _attn_body,
        out_shape=jax.ShapeDtypeStruct((T, MLA_HEADS * V_DIM), BF16),
        grid=(B, MLA_HEADS // nh, per_b),
        in_specs=[
            pl.BlockSpec((t, nh * LANES), lambda b, hp, i: (b * per_b + i, hp)),
            pl.BlockSpec((S, nh * LANES), lambda b, hp, i: (b, hp)),
            pl.BlockSpec((S, nh * LANES), lambda b, hp, i: (b, hp)),
        ],
        out_specs=pl.BlockSpec((t, nh * V_DIM), lambda b, hp, i: (b * per_b + i, hp)),
        scratch_shapes=[pltpu.VMEM((per_b, nh, t, t), F32)],
        compiler_params=_params(3), name="attn",
    )(q, k, mvt)


def _split3(a):
    hi = a.astype(BF16)
    r = a - hi.astype(F32)
    mid = r.astype(BF16)
    lo = (r - mid.astype(F32)).astype(BF16)
    return hi, mid, lo


def _mlstm_body(bi_ref, bf_ref, z_ref, vt_ref, g_ref, og_ref, cw_ref, cb_ref, wq_ref, wk_ref, gout_ref,
                out_ref, zpad, q_s, k_s, st_s, m_s, *, S):
    h = pl.program_id(1)
    L = SEQ_TILE
    nchunk = S // L
    hd = MLSTM_DIM
    zpad[0:8, :] = jnp.zeros((8, hd), F32)
    zpad[8:, :] = z_ref[...].astype(F32)
    zc = cb_ref[...] + jnp.zeros((S, hd), F32)
    for j in range(CONV_WIDTH):
        zc = zc + cw_ref[j:j + 1, :] * zpad[pl.ds(8 - (CONV_WIDTH - 1) + j, S), :]
    zs = _silu(zc).astype(BF16)
    q_s[...] = _dot(zs, wq_ref[...]).astype(BF16)
    k_s[...] = (_dot(zs, wk_ref[...]) * (hd ** -0.5)).astype(BF16)

    st_s[...] = jnp.zeros_like(st_s)
    m_s[...] = jnp.zeros_like(m_s)
    b_i = bi_ref[h]
    b_f = bf_ref[h]
    r_i = lax.broadcasted_iota(I32, (L, L), 0)
    c_i = lax.broadcasted_iota(I32, (L, L), 1)
    tril = r_i >= c_i
    strict_lower = jnp.where(r_i > c_i, 1.0, 0.0).astype(BF16)
    ones_row = jnp.where(lax.broadcasted_iota(I32, (hd, L), 0) == 0, 1.0, 0.0).astype(BF16)

    def chunk(c, _):
        start = pl.multiple_of(c * L, L)
        qc = q_s[pl.ds(start, L), :]
        kc = k_s[pl.ds(start, L), :]
        vt_aug = jnp.concatenate([vt_ref[c], ones_row], axis=0)
        li = g_ref[c, pl.ds(h, 1), :] + b_i
        fp = g_ref[c, pl.ds(MLSTM_HEADS + h, 1), :] + b_f
        lf = jnp.minimum(fp, 0.0) - jnp.log(1.0 + jnp.exp(-jnp.abs(fp)))
        a_mat = jnp.where(tril, lf, 0.0)
        a_hi, a_mid, a_lo = _split3(a_mat)
        d0 = _dot(a_hi, strict_lower) + _dot(a_mid, strict_lower) + _dot(a_lo, strict_lower)
        dmat = jnp.where(tril, d0 + li, -jnp.inf)
        b_col = jnp.sum(a_mat, axis=1, keepdims=True)
        m_prev = m_s[...]
        inter = b_col + m_prev
        mt = jnp.maximum(inter, jnp.max(dmat, axis=1, keepdims=True))
        w = jnp.exp(dmat - mt) * _dot_nt(qc, kc)
        decay = jnp.exp(inter - mt)
        st = st_s[...]
        num_aug = decay * _dot_nt(qc, st.astype(BF16)) + _dot_nt(w.astype(BF16), vt_aug)
        num = num_aug[:, :hd]
        den = num_aug[:, hd:hd + 1]
        hv = num / jnp.maximum(jnp.abs(den), jnp.exp(-mt))
        g = d0[L - 1:L, :] + li
        b_last = b_col[L - 1:L, :]
        m_new = jnp.maximum(b_last + m_prev, jnp.max(g, axis=1, keepdims=True))
        a = jnp.exp(g - m_new)
        cd = jnp.exp(b_last + m_prev - m_new)
        st_s[...] = cd * st + _dot((vt_aug.astype(F32) * a).astype(BF16), kc)
        m_s[...] = m_new
        hn = hv * lax.rsqrt(jnp.mean(hv * hv, axis=-1, keepdims=True) + RMS_EPS) * gout_ref[...]
        og = og_ref[pl.ds(start, L), :].astype(F32)
        out_ref[pl.ds(start, L), :] = (_sigmoid(og) * hn).astype(BF16)
        return 0

    lax.fori_loop(0, nchunk, chunk, 0)


def _mlstm(z, vt3, g3, og, conv_w, conv_b, wq, wk, b_i, b_f, g_out, B, S):
    T = z.shape[0]
    L = SEQ_TILE
    per_b = S // L
    hd = MLSTM_DIM
    smem = pl.BlockSpec(memory_space=pltpu.SMEM)
    seq = pl.BlockSpec((S, hd), lambda b, h: (b, h))
    return pl.pallas_call(
        functools.partial(_mlstm_body, S=S),
        out_shape=jax.ShapeDtypeStruct((T, MLSTM_INNER), BF16),
        grid=(B, MLSTM_HEADS),
        in_specs=[
            smem, smem, seq,
            pl.BlockSpec((per_b, hd, L), lambda b, h: (b, h, 0)),
            pl.BlockSpec((per_b, 16, L), lambda b, h: (b, 0, 0)),
            seq,
            pl.BlockSpec((CONV_WIDTH, hd), lambda b, h: (0, h)),
            pl.BlockSpec((1, hd), lambda b, h: (0, h)),
            pl.BlockSpec((None, hd, hd), lambda b, h: (h, 0, 0)),
            pl.BlockSpec((None, hd, hd), lambda b, h: (h, 0, 0)),
            pl.BlockSpec((1, hd), lambda b, h: (0, h)),
        ],
        out_specs=seq,
        scratch_shapes=[
            pltpu.VMEM((S + 8, hd), F32),
            pltpu.VMEM((S, hd), BF16),
            pltpu.VMEM((S, hd), BF16),
            pltpu.VMEM((2 * hd, hd), F32),
            pltpu.VMEM((1, 1), F32),
        ],
        compiler_params=_params(2), name="mlstm",
    )(b_i, b_f, z, vt3, g3, og, conv_w, conv_b, wq, wk, g_out)


def _post_body(ao_ref, mo_ref, x_ref, gt1_ref, gattn_ref, wout_ref, g2_ref, sc2_ref, sh2_ref, gt2_ref,
               wrt_ref, wgu_ref, wds_ref, base_ref, h2_ref, lg_ref):
    ao = ao_ref[...].astype(F32)
    aon = ao * lax.rsqrt(jnp.mean(ao * ao, axis=-1, keepdims=True) + RMS_EPS) * gattn_ref[...]
    mix_in = jnp.concatenate([aon.astype(BF16), mo_ref[...]], axis=1)
    x1 = x_ref[...] + gt1_ref[...] * _dot(mix_in, wout_ref[...])
    hn = x1 * lax.rsqrt(jnp.mean(x1 * x1, axis=-1, keepdims=True) + RMS_EPS) * g2_ref[...]
    h2 = hn * sc2_ref[...] + sh2_ref[...]
    words = _pack_rows(h2)
    h2_ref[0] = words[:, :HALF]
    h2_ref[1] = words[:, HALF:]
    h2b = h2.astype(BF16)
    gu = _dot(h2b, wgu_ref[...])
    act = (_silu(gu[:, :SHARED_DIM]) * gu[:, SHARED_DIM:]).astype(BF16)
    base_ref[...] = x1 + gt2_ref[...] * _dot(act, wds_ref[...])
    lg_ref[...] = _dot_nt(wrt_ref[...], h2b)


def _post(ao, mo, x2, gt1, gattn, wout, g2, sc2p, sh2, gt2, wrt, wgu, wds, S):
    T = x2.shape[0]
    tm = WIDE_TILE
    per_b = S // tm
    full = lambda a: pl.BlockSpec(a.shape, lambda i: (0,) * a.ndim)
    row = lambda w: pl.BlockSpec((tm, w), lambda i: (i, 0))
    bmod = pl.BlockSpec((None, 1, D_MODEL), lambda i: (i // per_b, 0, 0))
    return pl.pallas_call(
        _post_body,
        out_shape=(jax.ShapeDtypeStruct((T, D_MODEL), F32), jax.ShapeDtypeStruct((2, T, HALF), U32),
                   jax.ShapeDtypeStruct((N_EXPERTS, T), F32)),
        grid=(T // tm,),
        in_specs=[row(MLA_HEADS * V_DIM), row(MLSTM_INNER), row(D_MODEL), bmod, full(gattn), full(wout),
                  full(g2), bmod, bmod, bmod, full(wrt), full(wgu), full(wds)],
        out_specs=(row(D_MODEL), pl.BlockSpec((2, tm, HALF), lambda i: (0, i, 0)),
                   pl.BlockSpec((N_EXPERTS, tm), lambda i: (0, i))),
        compiler_params=_params(1), name="post",
    )(ao, mo, x2, gt1, gattn, wout, g2, sc2p, sh2, gt2, wrt, wgu, wds)


def _first_max(x, rows, n):
    mx = jnp.max(x, axis=0, keepdims=True)
    idx = jnp.min(jnp.where(x == mx, rows, n), axis=0, keepdims=True)
    return mx, idx


def _route_body(lg_ref, bias_ref, idx_ref, w_ref, rank_ref, cnt_ref, carry):
    i = pl.program_id(0)
    tt = lg_ref.shape[1]

    @pl.when(i == 0)
    def _():
        carry[...] = jnp.zeros_like(carry)

    sc = _sigmoid(lg_ref[...])
    bi = sc + bias_ref[...]
    rows = lax.broadcasted_iota(I32, (N_EXPERTS, tt), 0)
    rows_g = lax.broadcasted_iota(I32, (GROUP_SIZE, tt), 0)
    rows_n = lax.broadcasted_iota(I32, (N_GROUPS, tt), 0)
    neg = -jnp.inf
    gs = []
    for g in range(N_GROUPS):
        xg = bi[g * GROUP_SIZE:(g + 1) * GROUP_SIZE]
        m1, i1 = _first_max(xg, rows_g, GROUP_SIZE)
        m2 = jnp.max(jnp.where(rows_g == i1, neg, xg), axis=0, keepdims=True)
        gs.append(m1 + m2)
    cur = jnp.concatenate(gs, axis=0)
    keep = jnp.zeros((N_GROUPS, tt), F32)
    for _ in range(TOPK_GROUPS):
        _, gi = _first_max(cur, rows_n, N_GROUPS)
        sel = rows_n == gi
        keep = jnp.where(sel, 1.0, keep)
        cur = jnp.where(sel, neg, cur)
    masked = jnp.concatenate(
        [jnp.where(keep[g:g + 1] > 0.0, bi[g * GROUP_SIZE:(g + 1) * GROUP_SIZE], neg) for g in range(N_GROUPS)], axis=0)
    cur = masked
    onehot = jnp.zeros((N_EXPERTS, tt), F32)
    idxs, ws = [], []
    for _ in range(TOP_K):
        _, ei = _first_max(cur, rows, N_EXPERTS)
        sel = rows == ei
        ws.append(jnp.sum(jnp.where(sel, sc, 0.0), axis=0, keepdims=True))
        idxs.append(ei)
        cur = jnp.where(sel, neg, cur)
        onehot = jnp.where(sel, 1.0, onehot)
    wsum = ws[0]
    for k in range(1, TOP_K):
        wsum = wsum + ws[k]
    idx_ref[...] = jnp.concatenate(idxs, axis=0)
    w_ref[...] = jnp.concatenate([w / wsum * ROUTED_SCALE for w in ws], axis=0)
    r_i = lax.broadcasted_iota(I32, (tt, tt), 0)
    c_i = lax.broadcasted_iota(I32, (tt, tt), 1)
    before = jnp.where(r_i < c_i, 1.0, 0.0).astype(BF16)
    tot = _dot(onehot.astype(BF16), before) + carry[...]
    rank_ref[...] = jnp.concatenate(
        [jnp.sum(jnp.where(rows == ei, tot, 0.0), axis=0, keepdims=True) for ei in idxs], axis=0).astype(I32)
    carry[...] = carry[...] + jnp.sum(onehot, axis=1, keepdims=True)
    cnt_ref[...] = jnp.broadcast_to(carry[...], cnt_ref.shape)


def _route(lgt, bias_col):
    T = lgt.shape[1]
    tt = WIDE_TILE
    blk = pl.BlockSpec((TOP_K, tt), lambda i: (0, i))
    return pl.pallas_call(
        _route_body,
        out_shape=(jax.ShapeDtypeStruct((TOP_K, T), I32), jax.ShapeDtypeStruct((TOP_K, T), F32),
                   jax.ShapeDtypeStruct((TOP_K, T), I32), jax.ShapeDtypeStruct((N_EXPERTS, LANES), F32)),
        grid=(T // tt,),
        in_specs=[pl.BlockSpec((N_EXPERTS, tt), lambda i: (0, i)), pl.BlockSpec((N_EXPERTS, 1), lambda i: (0, 0))],
        out_specs=(blk, blk, blk, pl.BlockSpec((N_EXPERTS, LANES), lambda i: (0, 0))),
        scratch_shapes=[pltpu.VMEM((N_EXPERTS, 1), F32)],
        compiler_params=_params(1), name="route",
    )(lgt, bias_col)


def _dest_body(idx_ref, rank_ref, off_ref, dest_ref):
    tt = idx_ref.shape[1]
    rows = lax.broadcasted_iota(I32, (N_EXPERTS, tt), 0)
    off = off_ref[...]
    outs = []
    for k in range(TOP_K):
        sel = rows == idx_ref[k:k + 1, :]
        outs.append(jnp.sum(jnp.where(sel, off, 0), axis=0, keepdims=True) + rank_ref[k:k + 1, :])
    dest_ref[...] = jnp.concatenate(outs, axis=0)


def _dest(idx_t, rank_t, off_col):
    T = idx_t.shape[1]
    tt = min(4 * SEQ_TILE, T)
    blk = pl.BlockSpec((TOP_K, tt), lambda i: (0, i))
    return pl.pallas_call(
        _dest_body, out_shape=jax.ShapeDtypeStruct((TOP_K, T), I32), grid=(T // tt,),
        in_specs=[blk, blk, pl.BlockSpec((N_EXPERTS, 1), lambda i: (0, 0))], out_specs=blk,
        compiler_params=_params(1), name="dest",
    )(idx_t, rank_t, off_col)


SC_WINDOW = 128
HALF = PACKED // 2


def _sc_mesh():
    return plsc.VectorSubcoreMesh(core_axis_name="c", subcore_axis_name="s")


def _scatter_rows(dest2_t, rows, n_out):
    n = rows.shape[0]

    @pl.kernel(out_type=jax.ShapeDtypeStruct((n_out, HALF), U32), mesh=_sc_mesh(), scratch_types=[])
    def scatter(d_hbm, x_hbm, o_hbm):
        def body(d_vmem, x_vmem):
            for k in range(TOP_K):
                pltpu.sync_copy(x_vmem, o_hbm.at[d_vmem.at[k]])

        pltpu.emit_pipeline(
            body, grid=(n // SC_WINDOW,),
            in_specs=[pl.BlockSpec((TOP_K, SC_WINDOW), lambda i: (0, i)),
                      pl.BlockSpec((SC_WINDOW, HALF), lambda i: (i, 0))],
            out_specs=[], core_axis_name=("c", "s"), dimension_semantics=(pltpu.PARALLEL,),
        )(d_hbm, x_hbm)

    return scatter(dest2_t, rows)


def _gather_rows(index_row, rows):
    n = index_row.shape[1]

    @pl.kernel(out_type=jax.ShapeDtypeStruct((n, HALF), U32), mesh=_sc_mesh(), scratch_types=[])
    def gather(i_hbm, y_hbm, o_hbm):
        def body(i_vmem, o_vmem):
            pltpu.sync_copy(y_hbm.at[i_vmem.at[0]], o_vmem)

        pltpu.emit_pipeline(
            body, grid=(n // SC_WINDOW,),
            in_specs=[pl.BlockSpec((1, SC_WINDOW), lambda i: (0, i))],
            out_specs=[pl.BlockSpec((SC_WINDOW, HALF), lambda i: (i, 0))],
            core_axis_name=("c", "s"), dimension_semantics=(pltpu.PARALLEL,),
        )(i_hbm, o_hbm)

    return gather(index_row, rows)


EXPERT_RING = 4
WEIGHT_RING = 3


def _experts_body(first_ref, nblk_ref, cnt_ref, xs_ref, wg_ref, wu_ref, wd_ref, y_ref,
                  xbuf, ybuf, wgf, wuf, wdf, wgb, wub, wdb, xsem, ysem, wsem):
    e = pl.program_id(0)
    last = pl.num_programs(0) - 1
    wring = wgf.shape[0]

    def w_copies(ex):
        slot = ex % wring
        return [pltpu.make_async_copy(src.at[ex], dst.at[slot], wsem.at[slot])
                for src, dst in ((wg_ref, wgf), (wu_ref, wuf), (wd_ref, wdf))]

    @pl.when(e == 0)
    def _():
        for ex in range(wring - 1):
            for cp in w_copies(ex):
                cp.start()

    b0 = first_ref[e]
    nb = nblk_ref[e]
    total = first_ref[last] + nblk_ref[last]
    ring = xbuf.shape[0]

    def rows(g):
        return pl.ds(pl.multiple_of(g * ROW_BLOCK, ROW_BLOCK), ROW_BLOCK)

    def x_copy(g):
        slot = g % ring
        return pltpu.make_async_copy(xs_ref.at[:, rows(g), :], xbuf.at[slot], xsem.at[slot])

    def y_copy(g):
        slot = g % ring
        return pltpu.make_async_copy(ybuf.at[slot], y_ref.at[:, rows(g), :], ysem.at[slot])

    @pl.when(e == 0)
    def _():
        for g in range(ring - 1):
            @pl.when(g < total)
            def _():
                x_copy(g).start()

    for cp in w_copies(e):
        cp.wait()

    @pl.when(nb > 0)
    def _():
        slot = e % wring
        wgb[...] = wgf[slot].astype(BF16)
        wub[...] = wuf[slot].astype(BF16)
        wdb[...] = wdf[slot].astype(BF16)

    @pl.when(e + wring - 1 <= last)
    def _():
        for cp in w_copies(e + wring - 1):
            cp.start()

    def prefetch(g):
        @pl.when(g + ring - 1 < total)
        def _():
            x_copy(g + ring - 1).start()

    def compute(i):
        slot = (b0 + i) % ring
        words = jnp.concatenate([xbuf[slot, 0], xbuf[slot, 1]], axis=1)
        n_live = cnt_ref[e] - i * ROW_BLOCK
        words = jnp.where(lax.broadcasted_iota(I32, words.shape, 0) < n_live, words, U32(0))
        x = jnp.concatenate(_unpack_rows(words), axis=1).astype(BF16)
        act = (_silu(_dot(x, wgb[...])) * _dot(x, wub[...])).astype(BF16)
        y = _pack_rows(_dot(act, wdb[...]))
        ybuf[slot, 0] = y[:, :HALF]
        ybuf[slot, 1] = y[:, HALF:]

    def blocks(i, n):
        g0 = b0 + i
        prefetch(g0)
        for d in range(n):
            x_copy(g0 + d).wait()

            @pl.when(g0 + d >= ring)
            def _():
                y_copy(g0 + d - ring).wait()

        for d in range(n):
            compute(i + d)
        for d in range(1, n):
            prefetch(g0 + d)
        for d in range(n):
            y_copy(g0 + d).start()

    def pair(ii, carry):
        blocks(2 * ii, 2)
        return carry

    def single(_, carry):
        blocks(nb - 1, 1)
        return carry

    lax.fori_loop(0, nb // 2, pair, 0)
    lax.fori_loop(0, nb % 2, single, 0)

    @pl.when(e == last)
    def _():
        for back in range(ring, 0, -1):
            @pl.when(total >= back)
            def _():
                y_copy(total - back).wait()


def _experts(first_block, n_blocks, counts, xs, wg, wu, wd):
    n_rows = xs.shape[1]
    anyspec = pl.BlockSpec(memory_space=pl.ANY)
    ring = pltpu.VMEM((EXPERT_RING, 2, ROW_BLOCK, HALF), U32)
    grid_spec = pltpu.PrefetchScalarGridSpec(
        num_scalar_prefetch=3, grid=(N_EXPERTS,),
        in_specs=[anyspec] * 4,
        out_specs=anyspec,
        scratch_shapes=[ring, ring,
                        pltpu.VMEM((WEIGHT_RING, D_MODEL, EXPERT_DIM), F32),
                        pltpu.VMEM((WEIGHT_RING, D_MODEL, EXPERT_DIM), F32),
                        pltpu.VMEM((WEIGHT_RING, EXPERT_DIM, D_MODEL), F32),
                        pltpu.VMEM((D_MODEL, EXPERT_DIM), BF16), pltpu.VMEM((D_MODEL, EXPERT_DIM), BF16),
                        pltpu.VMEM((EXPERT_DIM, D_MODEL), BF16),
                        pltpu.SemaphoreType.DMA((EXPERT_RING,)), pltpu.SemaphoreType.DMA((EXPERT_RING,)),
                        pltpu.SemaphoreType.DMA((WEIGHT_RING,))],
    )
    return pl.pallas_call(
        _experts_body, out_shape=jax.ShapeDtypeStruct((2, n_rows, HALF), U32), grid_spec=grid_spec,
        compiler_params=_params(1), name="experts",
    )(first_block, n_blocks, counts, xs, wg, wu, wd)


COMBINE_PARTS = 1


def _combine_body(*refs):
    ya_refs = refs[:TOP_K]
    yb_refs = refs[TOP_K:2 * TOP_K]
    base_ref, w_ref, gt2_ref = refs[2 * TOP_K:2 * TOP_K + 3]
    out_ref = refs[-1]
    w = w_ref[...]
    r_lo = r_hi = None
    for k in range(TOP_K):
        lo, hi = _unpack_rows(jnp.concatenate([ya_refs[k][...], yb_refs[k][...]], axis=1))
        wk = w[:, k:k + 1]
        r_lo = wk * lo if r_lo is None else r_lo + wk * lo
        r_hi = wk * hi if r_hi is None else r_hi + wk * hi
    out_ref[...] = base_ref[...] + gt2_ref[...] * jnp.concatenate([r_lo, r_hi], axis=1)


def _combine(y_slots, base, w_tk, gt2, S, part, prev_out):
    T = base.shape[0]
    tt = WIDE_TILE
    nt = T // tt // COMBINE_PARTS
    t0 = part * nt
    per_b = S // tt
    yspec = lambda half, k: pl.BlockSpec((tt, HALF), lambda i: ((half * TOP_K + k) * nt + i, 0))
    in_specs = [yspec(half, k) for half in range(2) for k in range(TOP_K)] + [
        pl.BlockSpec((tt, D_MODEL), lambda i: (t0 + i, 0)),
        pl.BlockSpec((tt, TOP_K), lambda i: (t0 + i, 0)),
        pl.BlockSpec((None, 1, D_MODEL), lambda i: ((t0 + i) // per_b, 0, 0)),
    ]
    args = [y_slots] * (2 * TOP_K) + [base, w_tk, gt2]
    aliases = {}
    if prev_out is not None:
        in_specs.append(pl.BlockSpec(memory_space=pl.ANY))
        args.append(prev_out)
        aliases = {len(args) - 1: 0}
    return pl.pallas_call(
        _combine_body, out_shape=jax.ShapeDtypeStruct((T, D_MODEL), F32), grid=(nt,),
        in_specs=in_specs,
        out_specs=pl.BlockSpec((tt, D_MODEL), lambda i: (t0 + i, 0)),
        input_output_aliases=aliases,
        compiler_params=_params(1), name=f"combine{part}",
    )(*args)


def _pad_heads(w, width):
    kdim = w.shape[0]
    w3 = w.reshape(kdim, MLA_HEADS, width)
    return jnp.pad(w3, ((0, 0), (0, 0), (0, LANES - width))).reshape(kdim, MLA_HEADS * LANES)


def _rot_partner(w_rope):
    half = QK_ROPE // 2
    return jnp.concatenate([-w_rope[..., half:], w_rope[..., :half]], axis=-1)


def _lane_vec(nope, rope):
    return jnp.concatenate([nope, rope, jnp.zeros((LANES - QK_DIM,), F32)]).reshape(1, LANES)


def _layer(x2, ada, positions, B, S, g_norm1, w_in, g_cq, w_uq, g_ckv, w_ukv, g_qn, g_kn, g_attn_out,
           conv_w, conv_b, w_mq, w_mk, b_igate, b_fgate, g_mlstm_out, w_out, g_norm2, w_router, router_bias,
           w_gate_exp, w_up_exp, w_down_exp, w_gate_sh, w_up_sh, w_down_sh):
    T = B * S
    sh1, sc1, gt1, sh2, sc2, gt2 = [a.reshape(B, 1, D_MODEL) for a in jnp.split(ada, 6, axis=-1)]
    row = lambda v: v.reshape(1, -1).astype(F32)

    inv_freq = ROPE_THETA ** (-jnp.arange(0, QK_ROPE, 2, dtype=F32) / QK_ROPE)
    ang = positions.astype(F32).reshape(T, 1) * inv_freq
    half = QK_ROPE // 2
    freq_lane = jnp.arange(half)[:, None]
    lane = jnp.arange(LANES)[None, :]
    place = ((lane == QK_NOPE + freq_lane) | (lane == QK_NOPE + half + freq_lane)).astype(F32)
    spread = lambda tbl: jnp.dot(tbl, place, precision=lax.Precision.HIGHEST)
    cosp = (jnp.arange(LANES) < QK_NOPE).astype(F32)[None, :] + spread(jnp.cos(ang))
    sinp = spread(jnp.sin(ang))

    o0 = 0
    w_cq = w_in[:, o0:o0 + Q_LORA]; o0 += Q_LORA
    w_ckv = w_in[:, o0:o0 + KV_LORA]; o0 += KV_LORA
    w_kr = w_in[:, o0:o0 + QK_ROPE]; o0 += QK_ROPE
    w_z = w_in[:, o0:o0 + MLSTM_INNER]; o0 += MLSTM_INNER
    w_v = w_in[:, o0:o0 + MLSTM_INNER]; o0 += MLSTM_INNER
    w_o = w_in[:, o0:o0 + MLSTM_INNER]; o0 += MLSTM_INNER
    w_i = w_in[:, o0:o0 + MLSTM_HEADS]; o0 += MLSTM_HEADS
    w_f = w_in[:, o0:o0 + MLSTM_HEADS]
    zl = jnp.zeros((D_MODEL, QK_NOPE), F32)
    zr = jnp.zeros((D_MODEL, LANES - QK_DIM), F32)
    wa = jnp.concatenate([w_cq, w_ckv, w_z, w_o, zl, w_kr, zr, zl, _rot_partner(w_kr), zr], axis=1).astype(BF16)
    wt = jnp.concatenate([w_v.T, w_i.T, w_f.T, jnp.zeros((16 - 2 * MLSTM_HEADS, D_MODEL), F32)], axis=0).astype(BF16)

    uq3 = w_uq.reshape(Q_LORA, MLA_HEADS, QK_DIM)
    uq_rot = jnp.concatenate([jnp.zeros((Q_LORA, MLA_HEADS, QK_NOPE), F32), _rot_partner(uq3[..., QK_NOPE:])], axis=-1)
    wuq = _pad_heads(w_uq, QK_DIM).astype(BF16)
    wuqs = _pad_heads(uq_rot.reshape(Q_LORA, MLA_HEADS * QK_DIM), QK_DIM).astype(BF16)
    ukv3 = w_ukv.reshape(KV_LORA, MLA_HEADS, QK_NOPE + V_DIM)
    wuk = _pad_heads(ukv3[..., :QK_NOPE].reshape(KV_LORA, MLA_HEADS * QK_NOPE), QK_NOPE).astype(BF16)
    wuv = _pad_heads(ukv3[..., QK_NOPE:].reshape(KV_LORA, MLA_HEADS * V_DIM), V_DIM).astype(BF16)
    vone = jnp.tile((jnp.arange(LANES) == V_DIM).astype(F32), MLA_HEADS).reshape(1, MLA_HEADS * LANES)
    gq = _lane_vec(g_qn[:QK_NOPE], g_qn[QK_NOPE:])
    gqs = _lane_vec(jnp.zeros((QK_NOPE,), F32), _rot_partner(g_qn[QK_NOPE:]) * jnp.concatenate(
        [-jnp.ones((QK_ROPE // 2,), F32), jnp.ones((QK_ROPE // 2,), F32)]))
    gk = _lane_vec(g_kn[:QK_NOPE], g_kn[QK_NOPE:])
    gks = _lane_vec(jnp.zeros((QK_NOPE,), F32), _rot_partner(g_kn[QK_NOPE:]) * jnp.concatenate(
        [-jnp.ones((QK_ROPE // 2,), F32), jnp.ones((QK_ROPE // 2,), F32)]))

    z, og, vt3, g3, q, k, mv = _inproj(
        x2, 1.0 + sc1, sh1, row(g_norm1), wa, wt, row(g_cq), wuq, wuqs, row(g_ckv), wuk, wuv, vone,
        gq, gqs, gk, gks, cosp, sinp, S)
    attn_o = _attn(q, k, mv, B, S)
    mlstm_o = _mlstm(z, vt3, g3, og, conv_w.reshape(CONV_WIDTH, MLSTM_INNER), row(conv_b),
                     w_mq.astype(BF16), w_mk.astype(BF16), b_igate, b_fgate, row(g_mlstm_out), B, S)

    wgu = jnp.concatenate([w_gate_sh, w_up_sh], axis=1).astype(BF16)
    base, h2, lgt = _post(attn_o, mlstm_o, x2, gt1, row(g_attn_out), w_out.astype(BF16), row(g_norm2),
                          1.0 + sc2, sh2, gt2, w_router.T.astype(BF16), wgu, w_down_sh.astype(BF16), S)

    idx_t, w_t, rank_t, cnt = _route(lgt, router_bias.reshape(N_EXPERTS, 1))
    counts = cnt[:, 0].astype(I32)
    padded = (counts + ROW_BLOCK - 1) // ROW_BLOCK * ROW_BLOCK
    padded_end = jnp.cumsum(padded)
    off = padded_end - padded
    n_rows = T * TOP_K + N_EXPERTS * ROW_BLOCK
    dest_t = _dest(idx_t, rank_t, off.reshape(N_EXPERTS, 1))

    dest2_t = jnp.concatenate([dest_t, dest_t + n_rows], axis=1)
    xs = _scatter_rows(dest2_t, h2.reshape(2 * T, HALF), 2 * n_rows).reshape(2, n_rows, HALF)
    y = _experts(off // ROW_BLOCK, padded // ROW_BLOCK, counts, xs, w_gate_exp, w_up_exp, w_down_exp)
    y_rows = y.reshape(2 * n_rows, HALF)
    w_tk = w_t.T
    tp = T // COMBINE_PARTS
    out = None
    for part in range(COMBINE_PARTS):
        dest_p = dest_t[:, part * tp:(part + 1) * tp].reshape(1, TOP_K * tp)
        y_slots = _gather_rows(jnp.concatenate([dest_p, dest_p + n_rows], axis=1), y_rows)
        out = _combine(y_slots, base, w_tk, gt2, S, part, out)
    return out


def kernel(x, c, positions, w_ada, b_ada, g_norm1, w_in, g_cq, w_uq, g_ckv, w_ukv, g_qn, g_kn, g_attn_out, conv_w, conv_b, w_mq, w_mk, b_igate, b_fgate, g_mlstm_out, w_out, g_norm2, w_router, router_bias, w_gate_exp, w_up_exp, w_down_exp, w_gate_sh, w_up_sh, w_down_sh):
    B, S, _ = x.shape
    depth = w_ada.shape[0]
    x2 = x.reshape(B * S, D_MODEL)
    for l in range(depth):
        ada = _ada(c, w_ada[l], b_ada[l])
        x2 = _layer(x2, ada, positions, B, S, g_norm1[l], w_in[l], g_cq[l], w_uq[l], g_ckv[l], w_ukv[l], g_qn[l],
                    g_kn[l], g_attn_out[l], conv_w[l], conv_b[l], w_mq[l], w_mk[l], b_igate[l], b_fgate[l],
                    g_mlstm_out[l], w_out[l], g_norm2[l], w_router[l], router_bias[l], w_gate_exp[l], w_up_exp[l],
                    w_down_exp[l], w_gate_sh[l], w_up_sh[l], w_down_sh[l])
    return x2.reshape(B, S, D_MODEL)
```

```python
import functools
import math

import jax
import jax.numpy as jnp
from jax import lax
from jax.experimental import pallas as pl
from jax.experimental.pallas import tpu as pltpu
from jax.experimental.pallas import tpu_sc as plsc

F32 = jnp.float32
BF16 = jnp.bfloat16
I32 = jnp.int32

D_MODEL = 1024
MLA_HEADS = 8
QK_NOPE = 64
QK_ROPE = 32
QK_DIM = QK_NOPE + QK_ROPE
V_DIM = 64
Q_LORA = 384
KV_LORA = 256
ROPE_THETA = 10000.0
MLSTM_HEADS = 4
MLSTM_DIM = 128
MLSTM_INNER = MLSTM_HEADS * MLSTM_DIM
CONV_WIDTH = 4
N_EXPERTS = 256
TOP_K = 8
N_GROUPS = 8
TOPK_GROUPS = 4
GROUP_SIZE = N_EXPERTS // N_GROUPS
EXPERT_DIM = 256
SHARED_DIM = 256
ROUTED_SCALE = 2.5
RMS_EPS = 1e-6

LANES = 128
SEQ_TILE = 256
WIDE_TILE = 512
ROW_BLOCK = 256
VMEM_LIMIT = 56 * 1024 * 1024

_OFF_CQ = 0
_OFF_CKV = _OFF_CQ + Q_LORA
_OFF_Z = _OFF_CKV + KV_LORA
_OFF_O = _OFF_Z + MLSTM_INNER
_OFF_KR = _OFF_O + MLSTM_INNER
_OFF_KRS = _OFF_KR + LANES
_WA_COLS = _OFF_KRS + LANES
_WT_ROWS = MLSTM_INNER + 16


def _params(n_axes, vmem=VMEM_LIMIT):
    return pltpu.CompilerParams(dimension_semantics=("arbitrary",) * n_axes, vmem_limit_bytes=vmem)


def _dot(a, b):
    return jnp.dot(a, b, preferred_element_type=F32)


def _dot_nt(a, b):
    return lax.dot_general(a, b, (((1,), (1,)), ((), ())), preferred_element_type=F32)


def _sigmoid(x):
    return 1.0 / (1.0 + jnp.exp(-x))


def _silu(x):
    return x * _sigmoid(x)


U32 = jnp.uint32
PACKED = D_MODEL // 2


def _pack_rows(x):
    w = x.shape[1] // 2
    lo = lax.bitcast_convert_type(x[:, :w].astype(BF16).astype(F32), U32)
    hi = lax.bitcast_convert_type(x[:, w:].astype(BF16).astype(F32), U32)
    return lax.shift_right_logical(lo, U32(16)) | (hi & U32(0xFFFF0000))


def _unpack_rows(p):
    lo = lax.bitcast_convert_type(lax.shift_left(p, U32(16)), F32)
    hi = lax.bitcast_convert_type(p & U32(0xFFFF0000), F32)
    return lo, hi


def _ada_body(c_ref, w_ref, b_ref, o_ref):
    c = c_ref[...]
    cond = _silu(c).astype(BF16)
    o_ref[...] = _dot(cond, w_ref[...].astype(BF16)) + b_ref[...]


def _ada(c, w_ada, b_ada):
    B = c.shape[0]
    n = w_ada.shape[1]
    tn = 1024
    return pl.pallas_call(
        _ada_body,
        out_shape=jax.ShapeDtypeStruct((B, n), F32),
        grid=(n // tn,),
        in_specs=[
            pl.BlockSpec((B, D_MODEL), lambda j: (0, 0)),
            pl.BlockSpec((D_MODEL, tn), lambda j: (0, j)),
            pl.BlockSpec((1, tn), lambda j: (0, j)),
        ],
        out_specs=pl.BlockSpec((B, tn), lambda j: (0, j)),
        compiler_params=_params(1),
        name="ada",
    )(c, w_ada, b_ada.reshape(1, n))


def _inproj_body(x_ref, sc_ref, sh_ref, g1_ref, wa_ref, wt_ref, gcq_ref, wuq_ref, wuqs_ref, gckv_ref,
                 wuk_ref, wuv_ref, vone_ref, gq_ref, gqs_ref, gk_ref, gks_ref, cos_ref, sin_ref,
                 z_ref, o_ref, vt_ref, gt_ref, q_ref, k_ref, mv_ref):
    x = x_ref[...]
    hn = x * lax.rsqrt(jnp.mean(x * x, axis=-1, keepdims=True) + RMS_EPS) * g1_ref[...]
    hm = (hn * sc_ref[...] + sh_ref[...]).astype(BF16)
    p = _dot(hm, wa_ref[...])
    z_ref[...] = p[:, _OFF_Z:_OFF_Z + MLSTM_INNER].astype(BF16)
    o_ref[...] = p[:, _OFF_O:_OFF_O + MLSTM_INNER].astype(BF16)
    rt = _dot_nt(wt_ref[...], hm)
    vt_ref[...] = rt[:MLSTM_INNER].astype(BF16)
    gt_ref[...] = rt[MLSTM_INNER:]

    cq = p[:, _OFF_CQ:_OFF_CQ + Q_LORA]
    cqn = (cq * lax.rsqrt(jnp.mean(cq * cq, axis=-1, keepdims=True) + RMS_EPS) * gcq_ref[...]).astype(BF16)
    ckv = p[:, _OFF_CKV:_OFF_CKV + KV_LORA]
    ckvn = (ckv * lax.rsqrt(jnp.mean(ckv * ckv, axis=-1, keepdims=True) + RMS_EPS) * gckv_ref[...]).astype(BF16)
    qa = _dot(cqn, wuq_ref[...])
    qb = _dot(cqn, wuqs_ref[...])
    ka = _dot(ckvn, wuk_ref[...])
    mv_ref[...] = (_dot(ckvn, wuv_ref[...]) + vone_ref[...]).astype(BF16)
    kr = p[:, _OFF_KR:_OFF_KR + LANES]
    krs = p[:, _OFF_KRS:_OFF_KRS + LANES]

    cosp = cos_ref[...]
    sinp = sin_ref[...]
    gqc = gq_ref[...] * cosp
    gqs = gqs_ref[...] * sinp
    gkc = gk_ref[...] * cosp
    k_rot = krs * (gks_ref[...] * sinp)
    kr_ss = jnp.sum(kr * kr, axis=-1, keepdims=True)
    q_scale = (QK_DIM ** -0.5) * math.log2(math.e)
    for h in range(MLA_HEADS):
        sl = slice(h * LANES, (h + 1) * LANES)
        qah = qa[:, sl]
        rq = lax.rsqrt(jnp.sum(qah * qah, axis=-1, keepdims=True) * (1.0 / QK_DIM) + RMS_EPS) * q_scale
        q_ref[:, sl] = ((qah * gqc + qb[:, sl] * gqs) * rq).astype(BF16)
        kah = ka[:, sl]
        rk = lax.rsqrt((jnp.sum(kah * kah, axis=-1, keepdims=True) + kr_ss) * (1.0 / QK_DIM) + RMS_EPS)
        k_ref[:, sl] = (((kah + kr) * gkc + k_rot) * rk).astype(BF16)


def _inproj(x2, sc1p, sh1, g1, wa, wt, gcq, wuq, wuqs, gckv, wuk, wuv, vone, gq, gqs, gk, gks, cosp, sinp, S):
    T = x2.shape[0]
    tm = SEQ_TILE
    nt = T // tm
    per_b = S // tm
    hp = MLA_HEADS * LANES
    full = lambda a: pl.BlockSpec(a.shape, lambda i: (0,) * a.ndim)
    row = lambda w: pl.BlockSpec((tm, w), lambda i: (i, 0))
    bmod = pl.BlockSpec((None, 1, D_MODEL), lambda i: (i // per_b, 0, 0))
    out_shapes = (
        jax.ShapeDtypeStruct((T, MLSTM_INNER), BF16),
        jax.ShapeDtypeStruct((T, MLSTM_INNER), BF16),
        jax.ShapeDtypeStruct((nt, MLSTM_INNER, tm), BF16),
        jax.ShapeDtypeStruct((nt, 16, tm), F32),
        jax.ShapeDtypeStruct((T, hp), BF16),
        jax.ShapeDtypeStruct((T, hp), BF16),
        jax.ShapeDtypeStruct((T, hp), BF16),
    )
    out_specs = (
        row(MLSTM_INNER), row(MLSTM_INNER),
        pl.BlockSpec((None, MLSTM_INNER, tm), lambda i: (i, 0, 0)),
        pl.BlockSpec((None, 16, tm), lambda i: (i, 0, 0)),
        row(hp), row(hp), row(hp),
    )
    in_specs = [row(D_MODEL), bmod, bmod, full(g1), full(wa), full(wt), full(gcq), full(wuq), full(wuqs),
                full(gckv), full(wuk), full(wuv), full(vone), full(gq), full(gqs), full(gk), full(gks),
                row(LANES), row(LANES)]
    return pl.pallas_call(
        _inproj_body, out_shape=out_shapes, grid=(nt,), in_specs=in_specs, out_specs=out_specs,
        compiler_params=_params(1), name="inproj",
    )(x2, sc1p, sh1, g1, wa, wt, gcq, wuq, wuqs, gckv, wuk, wuv, vone, gq, gqs, gk, gks, cosp, sinp)


ATTN_HEADS_PER_STEP = 4


def _pairs_loop(n, fn, init):
    c = lax.fori_loop(0, n // 2, lambda jj, c: fn(2 * jj + 1, fn(2 * jj, c)), init)
    return lax.fori_loop(0, n % 2, lambda _, c: fn(n - 1, c), c)


def _attn_body(q_ref, k_ref, v_ref, o_ref, s_scr):
    i = pl.program_id(2)
    t = SEQ_TILE
    nh = ATTN_HEADS_PER_STEP
    qry_pos = lax.broadcasted_iota(I32, (t, t), 0)
    key_pos = lax.broadcasted_iota(I32, (t, t), 1)
    causal = key_pos <= qry_pos
    qs = [q_ref[:, hh * LANES:(hh + 1) * LANES] for hh in range(nh)]

    def scores(j, hh, diag):
        kj = k_ref[pl.ds(pl.multiple_of(j * t, t), t), hh * LANES:(hh + 1) * LANES]
        s = _dot_nt(qs[hh], kj)
        return jnp.where(causal, s, -jnp.inf) if diag else s

    def halves(x):
        return x[:, :LANES], x[:, LANES:]

    def max_pass(j, mx, diag):
        out = []
        for hh in range(nh):
            s = scores(j, hh, diag)
            s_scr[j, hh] = s
            lo, hi = halves(s)
            out.append(jnp.maximum(mx[hh], jnp.maximum(lo, hi)))
        return tuple(out)

    mx = tuple(jnp.full((t, LANES), -jnp.inf, F32) for _ in range(nh))
    mx = _pairs_loop(i, lambda j, c: max_pass(j, c, False), mx)
    mx = max_pass(i, mx, True)
    mb = [jnp.broadcast_to(jnp.max(mx[hh], axis=1, keepdims=True), (t, LANES)) for hh in range(nh)]

    def acc_pass(j, carry, diag):
        row0 = pl.multiple_of(j * t, t)
        out = []
        for hh in range(nh):
            lo, hi = halves(s_scr[j, hh])
            p = jnp.concatenate([jnp.exp2(lo - mb[hh]), jnp.exp2(hi - mb[hh])], axis=1).astype(BF16)
            out.append(carry[hh] + _dot(p, v_ref[pl.ds(row0, t), hh * LANES:(hh + 1) * LANES]))
        return tuple(out)

    init = tuple(jnp.zeros((t, LANES), F32) for _ in range(nh))
    carry = _pairs_loop(i, lambda j, c: acc_pass(j, c, False), init)
    final = acc_pass(i, carry, True)
    outs = [acc / acc[:, V_DIM:V_DIM + 1] for acc in final]
    first_head = lax.broadcasted_iota(I32, (t, LANES), 1) < V_DIM
    o_ref[...] = jnp.concatenate(
        [jnp.where(first_head, outs[2 * pr], pltpu.roll(outs[2 * pr + 1], V_DIM, axis=1)) for pr in range(nh // 2)],
        axis=1).astype(BF16)


def _attn(q, k, mvt, B, S):
    T = q.shape[0]
    t = SEQ_TILE
    nh = ATTN_HEADS_PER_STEP
    per_b = S // t
    return pl.pallas_call(
        _attn_body,
        out_shape=jax.ShapeDtypeStruct((T, MLA_HEADS * V_DIM), BF16),
        grid=(B, MLA_HEADS // nh, per_b),
        in_specs=[
            pl.BlockSpec((t, nh * LANES), lambda b, hp, i: (b * per_b + i, hp)),
            pl.BlockSpec((S, nh * LANES), lambda b, hp, i: (b, hp)),
            pl.BlockSpec((S, nh * LANES), lambda b, hp, i: (b, hp)),
        ],
        out_specs=pl.BlockSpec((t, nh * V_DIM), lambda b, hp, i: (b * per_b + i, hp)),
        scratch_shapes=[pltpu.VMEM((per_b, nh, t, t), F32)],
        compiler_params=_params(3), name="attn",
    )(q, k, mvt)


def _split3(a):
    hi = a.astype(BF16)
    r = a - hi.astype(F32)
    mid = r.astype(BF16)
    lo = (r - mid.astype(F32)).astype(BF16)
    return hi, mid, lo


def _mlstm_body(bi_ref, bf_ref, z_ref, vt_ref, g_ref, og_ref, cw_ref, cb_ref, wq_ref, wk_ref, gout_ref,
                out_ref, zpad, q_s, k_s, st_s, m_s, *, S):
    h = pl.program_id(1)
    L = SEQ_TILE
    nchunk = S // L
    hd = MLSTM_DIM
    zpad[0:8, :] = jnp.zeros((8, hd), F32)
    zpad[8:, :] = z_ref[...].astype(F32)
    zc = cb_ref[...] + jnp.zeros((S, hd), F32)
    for j in range(CONV_WIDTH):
        zc = zc + cw_ref[j:j + 1, :] * zpad[pl.ds(8 - (CONV_WIDTH - 1) + j, S), :]
    zs = _silu(zc).astype(BF16)
    q_s[...] = _dot(zs, wq_ref[...]).astype(BF16)
    k_s[...] = (_dot(zs, wk_ref[...]) * (hd ** -0.5)).astype(BF16)

    st_s[...] = jnp.zeros_like(st_s)
    m_s[...] = jnp.zeros_like(m_s)
    b_i = bi_ref[h]
    b_f = bf_ref[h]
    r_i = lax.broadcasted_iota(I32, (L, L), 0)
    c_i = lax.broadcasted_iota(I32, (L, L), 1)
    tril = r_i >= c_i
    strict_lower = jnp.where(r_i > c_i, 1.0, 0.0).astype(BF16)
    ones_row = jnp.where(lax.broadcasted_iota(I32, (hd, L), 0) == 0, 1.0, 0.0).astype(BF16)

    def chunk(c, _):
        start = pl.multiple_of(c * L, L)
        qc = q_s[pl.ds(start, L), :]
        kc = k_s[pl.ds(start, L), :]
        vt_aug = jnp.concatenate([vt_ref[c], ones_row], axis=0)
        li = g_ref[c, pl.ds(h, 1), :] + b_i
        fp = g_ref[c, pl.ds(MLSTM_HEADS + h, 1), :] + b_f
        lf = jnp.minimum(fp, 0.0) - jnp.log(1.0 + jnp.exp(-jnp.abs(fp)))
        a_mat = jnp.where(tril, lf, 0.0)
        a_hi, a_mid, a_lo = _split3(a_mat)
        d0 = _dot(a_hi, strict_lower) + _dot(a_mid, strict_lower) + _dot(a_lo, strict_lower)
        dmat = jnp.where(tril, d0 + li, -jnp.inf)
        b_col = jnp.sum(a_mat, axis=1, keepdims=True)
        m_prev = m_s[...]
        inter = b_col + m_prev
        mt = jnp.maximum(inter, jnp.max(dmat, axis=1, keepdims=True))
        w = jnp.exp(dmat - mt) * _dot_nt(qc, kc)
        decay = jnp.exp(inter - mt)
        st = st_s[...]
        num_aug = decay * _dot_nt(qc, st.astype(BF16)) + _dot_nt(w.astype(BF16), vt_aug)
        num = num_aug[:, :hd]
        den = num_aug[:, hd:hd + 1]
        hv = num / jnp.maximum(jnp.abs(den), jnp.exp(-mt))
        g = d0[L - 1:L, :] + li
        b_last = b_col[L - 1:L, :]
        m_new = jnp.maximum(b_last + m_prev, jnp.max(g, axis=1, keepdims=True))
        a = jnp.exp(g - m_new)
        cd = jnp.exp(b_last + m_prev - m_new)
        st_s[...] = cd * st + _dot((vt_aug.astype(F32) * a).astype(BF16), kc)
        m_s[...] = m_new
        hn = hv * lax.rsqrt(jnp.mean(hv * hv, axis=-1, keepdims=True) + RMS_EPS) * gout_ref[...]
        og = og_ref[pl.ds(start, L), :].astype(F32)
        out_ref[pl.ds(start, L), :] = (_sigmoid(og) * hn).astype(BF16)
        return 0

    lax.fori_loop(0, nchunk, chunk, 0)


def _mlstm(z, vt3, g3, og, conv_w, conv_b, wq, wk, b_i, b_f, g_out, B, S):
    T = z.shape[0]
    L = SEQ_TILE
    per_b = S // L
    hd = MLSTM_DIM
    smem = pl.BlockSpec(memory_space=pltpu.SMEM)
    seq = pl.BlockSpec((S, hd), lambda b, h: (b, h))
    return pl.pallas_call(
        functools.partial(_mlstm_body, S=S),
        out_shape=jax.ShapeDtypeStruct((T, MLSTM_INNER), BF16),
        grid=(B, MLSTM_HEADS),
        in_specs=[
            smem, smem, seq,
            pl.BlockSpec((per_b, hd, L), lambda b, h: (b, h, 0)),
            pl.BlockSpec((per_b, 16, L), lambda b, h: (b, 0, 0)),
            seq,
            pl.BlockSpec((CONV_WIDTH, hd), lambda b, h: (0, h)),
            pl.BlockSpec((1, hd), lambda b, h: (0, h)),
            pl.BlockSpec((None, hd, hd), lambda b, h: (h, 0, 0)),
            pl.BlockSpec((None, hd, hd), lambda b, h: (h, 0, 0)),
            pl.BlockSpec((1, hd), lambda b, h: (0, h)),
        ],
        out_specs=seq,
        scratch_shapes=[
            pltpu.VMEM((S + 8, hd), F32),
            pltpu.VMEM((S, hd), BF16),
            pltpu.VMEM((S, hd), BF16),
            pltpu.VMEM((2 * hd, hd), F32),
            pltpu.VMEM((1, 1), F32),
        ],
        compiler_params=_params(2), name="mlstm",
    )(b_i, b_f, z, vt3, g3, og, conv_w, conv_b, wq, wk, g_out)


def _post_body(ao_ref, mo_ref, x_ref, gt1_ref, gattn_ref, wout_ref, g2_ref, sc2_ref, sh2_ref, gt2_ref,
               wrt_ref, wgu_ref, wds_ref, base_ref, h2_ref, lg_ref):
    ao = ao_ref[...].astype(F32)
    aon = ao * lax.rsqrt(jnp.mean(ao * ao, axis=-1, keepdims=True) + RMS_EPS) * gattn_ref[...]
    mix_in = jnp.concatenate([aon.astype(BF16), mo_ref[...]], axis=1)
    x1 = x_ref[...] + gt1_ref[...] * _dot(mix_in, wout_ref[...])
    hn = x1 * lax.rsqrt(jnp.mean(x1 * x1, axis=-1, keepdims=True) + RMS_EPS) * g2_ref[...]
    h2 = hn * sc2_ref[...] + sh2_ref[...]
    words = _pack_rows(h2)
    h2_ref[0] = words[:, :HALF]
    h2_ref[1] = words[:, HALF:]
    h2b = h2.astype(BF16)
    gu = _dot(h2b, wgu_ref[...])
    act = (_silu(gu[:, :SHARED_DIM]) * gu[:, SHARED_DIM:]).astype(BF16)
    base_ref[...] = x1 + gt2_ref[...] * _dot(act, wds_ref[...])
    lg_ref[...] = _dot_nt(wrt_ref[...], h2b)


def _post(ao, mo, x2, gt1, gattn, wout, g2, sc2p, sh2, gt2, wrt, wgu, wds, S):
    T = x2.shape[0]
    tm = WIDE_TILE
    per_b = S // tm
    full = lambda a: pl.BlockSpec(a.shape, lambda i: (0,) * a.ndim)
    row = lambda w: pl.BlockSpec((tm, w), lambda i: (i, 0))
    bmod = pl.BlockSpec((None, 1, D_MODEL), lambda i: (i // per_b, 0, 0))
    return pl.pallas_call(
        _post_body,
        out_shape=(jax.ShapeDtypeStruct((T, D_MODEL), F32), jax.ShapeDtypeStruct((2, T, HALF), U32),
                   jax.ShapeDtypeStruct((N_EXPERTS, T), F32)),
        grid=(T // tm,),
        in_specs=[row(MLA_HEADS * V_DIM), row(MLSTM_INNER), row(D_MODEL), bmod, full(gattn), full(wout),
                  full(g2), bmod, bmod, bmod, full(wrt), full(wgu), full(wds)],
        out_specs=(row(D_MODEL), pl.BlockSpec((2, tm, HALF), lambda i: (0, i, 0)),
                   pl.BlockSpec((N_EXPERTS, tm), lambda i: (0, i))),
        compiler_params=_params(1), name="post",
    )(ao, mo, x2, gt1, gattn, wout, g2, sc2p, sh2, gt2, wrt, wgu, wds)


def _first_max(x, rows, n):
    mx = jnp.max(x, axis=0, keepdims=True)
    idx = jnp.min(jnp.where(x == mx, rows, n), axis=0, keepdims=True)
    return mx, idx


def _route_body(lg_ref, bias_ref, idx_ref, w_ref, rank_ref, cnt_ref, carry):
    i = pl.program_id(0)
    tt = lg_ref.shape[1]

    @pl.when(i == 0)
    def _():
        carry[...] = jnp.zeros_like(carry)

    sc = _sigmoid(lg_ref[...])
    bi = sc + bias_ref[...]
    rows = lax.broadcasted_iota(I32, (N_EXPERTS, tt), 0)
    rows_g = lax.broadcasted_iota(I32, (GROUP_SIZE, tt), 0)
    rows_n = lax.broadcasted_iota(I32, (N_GROUPS, tt), 0)
    neg = -jnp.inf
    gs = []
    for g in range(N_GROUPS):
        xg = bi[g * GROUP_SIZE:(g + 1) * GROUP_SIZE]
        m1, i1 = _first_max(xg, rows_g, GROUP_SIZE)
        m2 = jnp.max(jnp.where(rows_g == i1, neg, xg), axis=0, keepdims=True)
        gs.append(m1 + m2)
    cur = jnp.concatenate(gs, axis=0)
    keep = jnp.zeros((N_GROUPS, tt), F32)
    for _ in range(TOPK_GROUPS):
        _, gi = _first_max(cur, rows_n, N_GROUPS)
        sel = rows_n == gi
        keep = jnp.where(sel, 1.0, keep)
        cur = jnp.where(sel, neg, cur)
    masked = jnp.concatenate(
        [jnp.where(keep[g:g + 1] > 0.0, bi[g * GROUP_SIZE:(g + 1) * GROUP_SIZE], neg) for g in range(N_GROUPS)], axis=0)
    cur = masked
    onehot = jnp.zeros((N_EXPERTS, tt), F32)
    idxs, ws = [], []
    for _ in range(TOP_K):
        _, ei = _first_max(cur, rows, N_EXPERTS)
        sel = rows == ei
        ws.append(jnp.sum(jnp.where(sel, sc, 0.0), axis=0, keepdims=True))
        idxs.append(ei)
        cur = jnp.where(sel, neg, cur)
        onehot = jnp.where(sel, 1.0, onehot)
    wsum = ws[0]
    for k in range(1, TOP_K):
        wsum = wsum + ws[k]
    idx_ref[...] = jnp.concatenate(idxs, axis=0)
    w_ref[...] = jnp.concatenate([w / wsum * ROUTED_SCALE for w in ws], axis=0)
    r_i = lax.broadcasted_iota(I32, (tt, tt), 0)
    c_i = lax.broadcasted_iota(I32, (tt, tt), 1)
    before = jnp.where(r_i < c_i, 1.0, 0.0).astype(BF16)
    tot = _dot(onehot.astype(BF16), before) + carry[...]
    rank_ref[...] = jnp.concatenate(
        [jnp.sum(jnp.where(rows == ei, tot, 0.0), axis=0, keepdims=True) for ei in idxs], axis=0).astype(I32)
    carry[...] = carry[...] + jnp.sum(onehot, axis=1, keepdims=True)
    cnt_ref[...] = jnp.broadcast_to(carry[...], cnt_ref.shape)


def _route(lgt, bias_col):
    T = lgt.shape[1]
    tt = WIDE_TILE
    blk = pl.BlockSpec((TOP_K, tt), lambda i: (0, i))
    return pl.pallas_call(
        _route_body,
        out_shape=(jax.ShapeDtypeStruct((TOP_K, T), I32), jax.ShapeDtypeStruct((TOP_K, T), F32),
                   jax.ShapeDtypeStruct((TOP_K, T), I32), jax.ShapeDtypeStruct((N_EXPERTS, LANES), F32)),
        grid=(T // tt,),
        in_specs=[pl.BlockSpec((N_EXPERTS, tt), lambda i: (0, i)), pl.BlockSpec((N_EXPERTS, 1), lambda i: (0, 0))],
        out_specs=(blk, blk, blk, pl.BlockSpec((N_EXPERTS, LANES), lambda i: (0, 0))),
        scratch_shapes=[pltpu.VMEM((N_EXPERTS, 1), F32)],
        compiler_params=_params(1), name="route",
    )(lgt, bias_col)


def _dest_body(idx_ref, rank_ref, off_ref, dest_ref):
    tt = idx_ref.shape[1]
    rows = lax.broadcasted_iota(I32, (N_EXPERTS, tt), 0)
    off = off_ref[...]
    outs = []
    for k in range(TOP_K):
        sel = rows == idx_ref[k:k + 1, :]
        outs.append(jnp.sum(jnp.where(sel, off, 0), axis=0, keepdims=True) + rank_ref[k:k + 1, :])
    dest_ref[...] = jnp.concatenate(outs, axis=0)


def _dest(idx_t, rank_t, off_col):
    T = idx_t.shape[1]
    tt = min(4 * SEQ_TILE, T)
    blk = pl.BlockSpec((TOP_K, tt), lambda i: (0, i))
    return pl.pallas_call(
        _dest_body, out_shape=jax.ShapeDtypeStruct((TOP_K, T), I32), grid=(T // tt,),
        in_specs=[blk, blk, pl.BlockSpec((N_EXPERTS, 1), lambda i: (0, 0))], out_specs=blk,
        compiler_params=_params(1), name="dest",
    )(idx_t, rank_t, off_col)


SC_WINDOW = 128
HALF = PACKED // 2


def _sc_mesh():
    return plsc.VectorSubcoreMesh(core_axis_name="c", subcore_axis_name="s")


def _scatter_rows(dest2_t, rows, n_out):
    n = rows.shape[0]

    @pl.kernel(out_type=jax.ShapeDtypeStruct((n_out, HALF), U32), mesh=_sc_mesh(), scratch_types=[])
    def scatter(d_hbm, x_hbm, o_hbm):
        def body(d_vmem, x_vmem):
            for k in range(TOP_K):
                pltpu.sync_copy(x_vmem, o_hbm.at[d_vmem.at[k]])

        pltpu.emit_pipeline(
            body, grid=(n // SC_WINDOW,),
            in_specs=[pl.BlockSpec((TOP_K, SC_WINDOW), lambda i: (0, i)),
                      pl.BlockSpec((SC_WINDOW, HALF), lambda i: (i, 0))],
            out_specs=[], core_axis_name=("c", "s"), dimension_semantics=(pltpu.PARALLEL,),
        )(d_hbm, x_hbm)

    return scatter(dest2_t, rows)


def _gather_rows(index_row, rows):
    n = index_row.shape[1]

    @pl.kernel(out_type=jax.ShapeDtypeStruct((n, HALF), U32), mesh=_sc_mesh(), scratch_types=[])
    def gather(i_hbm, y_hbm, o_hbm):
        def body(i_vmem, o_vmem):
            pltpu.sync_copy(y_hbm.at[i_vmem.at[0]], o_vmem)

        pltpu.emit_pipeline(
            body, grid=(n // SC_WINDOW,),
            in_specs=[pl.BlockSpec((1, SC_WINDOW), lambda i: (0, i))],
            out_specs=[pl.BlockSpec((SC_WINDOW, HALF), lambda i: (i, 0))],
            core_axis_name=("c", "s"), dimension_semantics=(pltpu.PARALLEL,),
        )(i_hbm, o_hbm)

    return gather(index_row, rows)


EXPERT_RING = 16
WEIGHT_RING = 3


def _experts_body(first_ref, nblk_ref, cnt_ref, xs_ref, wg_ref, wu_ref, wd_ref, y_ref,
                  xbuf, ybuf, wgf, wuf, wdf, wgb, wub, wdb, xsem, ysem, wsem):
    e = pl.program_id(0)
    last = pl.num_programs(0) - 1
    wring = wgf.shape[0]

    def w_copies(ex):
        slot = ex % wring
        return [pltpu.make_async_copy(src.at[ex], dst.at[slot], wsem.at[slot])
                for src, dst in ((wg_ref, wgf), (wu_ref, wuf), (wd_ref, wdf))]

    @pl.when(e == 0)
    def _():
        for ex in range(wring - 1):
            for cp in w_copies(ex):
                cp.start()

    b0 = first_ref[e]
    nb = nblk_ref[e]
    total = first_ref[last] + nblk_ref[last]
    ring = xbuf.shape[0]

    def rows(g):
        return pl.ds(pl.multiple_of(g * ROW_BLOCK, ROW_BLOCK), ROW_BLOCK)

    def x_copy(g):
        slot = g % ring
        return pltpu.make_async_copy(xs_ref.at[:, rows(g), :], xbuf.at[slot], xsem.at[slot])

    def y_copy(g):
        slot = g % ring
        return pltpu.make_async_copy(ybuf.at[slot], y_ref.at[:, rows(g), :], ysem.at[slot])

    @pl.when(e == 0)
    def _():
        for g in range(ring - 1):
            @pl.when(g < total)
            def _():
                x_copy(g).start()

    for cp in w_copies(e):
        cp.wait()

    @pl.when(nb > 0)
    def _():
        slot = e % wring
        wgb[...] = wgf[slot].astype(BF16)
        wub[...] = wuf[slot].astype(BF16)
        wdb[...] = wdf[slot].astype(BF16)

    @pl.when(e + wring - 1 <= last)
    def _():
        for cp in w_copies(e + wring - 1):
            cp.start()

    def prefetch(g):
        @pl.when(g + ring - 1 < total)
        def _():
            x_copy(g + ring - 1).start()

    def compute(i):
        slot = (b0 + i) % ring
        words = jnp.concatenate([xbuf[slot, 0], xbuf[slot, 1]], axis=1)
        n_live = cnt_ref[e] - i * ROW_BLOCK
        words = jnp.where(lax.broadcasted_iota(I32, words.shape, 0) < n_live, words, U32(0))
        x = jnp.concatenate(_unpack_rows(words), axis=1).astype(BF16)
        act = (_silu(_dot(x, wgb[...])) * _dot(x, wub[...])).astype(BF16)
        y = _pack_rows(_dot(act, wdb[...]))
        ybuf[slot, 0] = y[:, :HALF]
        ybuf[slot, 1] = y[:, HALF:]

    def blocks(i, n):
        g0 = b0 + i
        prefetch(g0)
        for d in range(n):
            x_copy(g0 + d).wait()

            @pl.when(g0 + d >= ring)
            def _():
                y_copy(g0 + d - ring).wait()

        for d in range(n):
            compute(i + d)
        for d in range(1, n):
            prefetch(g0 + d)
        for d in range(n):
            y_copy(g0 + d).start()

    def pair(ii, carry):
        blocks(2 * ii, 2)
        return carry

    def single(_, carry):
        blocks(nb - 1, 1)
        return carry

    lax.fori_loop(0, nb // 2, pair, 0)
    lax.fori_loop(0, nb % 2, single, 0)

    @pl.when(e == last)
    def _():
        for back in range(ring, 0, -1):
            @pl.when(total >= back)
            def _():
                y_copy(total - back).wait()


def _experts(first_block, n_blocks, counts, xs, wg, wu, wd):
    n_rows = xs.shape[1]
    anyspec = pl.BlockSpec(memory_space=pl.ANY)
    ring = pltpu.VMEM((EXPERT_RING, 2, ROW_BLOCK, HALF), U32)
    grid_spec = pltpu.PrefetchScalarGridSpec(
        num_scalar_prefetch=3, grid=(N_EXPERTS,),
        in_specs=[anyspec] * 4,
        out_specs=anyspec,
        scratch_shapes=[ring, ring,
                        pltpu.VMEM((WEIGHT_RING, D_MODEL, EXPERT_DIM), F32),
                        pltpu.VMEM((WEIGHT_RING, D_MODEL, EXPERT_DIM), F32),
                        pltpu.VMEM((WEIGHT_RING, EXPERT_DIM, D_MODEL), F32),
                        pltpu.VMEM((D_MODEL, EXPERT_DIM), BF16), pltpu.VMEM((D_MODEL, EXPERT_DIM), BF16),
                        pltpu.VMEM((EXPERT_DIM, D_MODEL), BF16),
                        pltpu.SemaphoreType.DMA((EXPERT_RING,)), pltpu.SemaphoreType.DMA((EXPERT_RING,)),
                        pltpu.SemaphoreType.DMA((WEIGHT_RING,))],
    )
    return pl.pallas_call(
        _experts_body, out_shape=jax.ShapeDtypeStruct((2, n_rows, HALF), U32), grid_spec=grid_spec,
        compiler_params=_params(1), name="experts",
    )(first_block, n_blocks, counts, xs, wg, wu, wd)


COMBINE_PARTS = 1


def _combine_body(*refs):
    ya_refs = refs[:TOP_K]
    yb_refs = refs[TOP_K:2 * TOP_K]
    base_ref, w_ref, gt2_ref = refs[2 * TOP_K:2 * TOP_K + 3]
    out_ref = refs[-1]
    w = w_ref[...]
    r_lo = r_hi = None
    for k in range(TOP_K):
        lo, hi = _unpack_rows(jnp.concatenate([ya_refs[k][...], yb_refs[k][...]], axis=1))
        wk = w[:, k:k + 1]
        r_lo = wk * lo if r_lo is None else r_lo + wk * lo
        r_hi = wk * hi if r_hi is None else r_hi + wk * hi
    out_ref[...] = base_ref[...] + gt2_ref[...] * jnp.concatenate([r_lo, r_hi], axis=1)


def _combine(y_slots, base, w_tk, gt2, S, part, prev_out):
    T = base.shape[0]
    tt = WIDE_TILE
    nt = T // tt // COMBINE_PARTS
    t0 = part * nt
    per_b = S // tt
    yspec = lambda half, k: pl.BlockSpec((tt, HALF), lambda i: ((half * TOP_K + k) * nt + i, 0))
    in_specs = [yspec(half, k) for half in range(2) for k in range(TOP_K)] + [
        pl.BlockSpec((tt, D_MODEL), lambda i: (t0 + i, 0)),
        pl.BlockSpec((tt, TOP_K), lambda i: (t0 + i, 0)),
        pl.BlockSpec((None, 1, D_MODEL), lambda i: ((t0 + i) // per_b, 0, 0)),
    ]
    args = [y_slots] * (2 * TOP_K) + [base, w_tk, gt2]
    aliases = {}
    if prev_out is not None:
        in_specs.append(pl.BlockSpec(memory_space=pl.ANY))
        args.append(prev_out)
        aliases = {len(args) - 1: 0}
    return pl.pallas_call(
        _combine_body, out_shape=jax.ShapeDtypeStruct((T, D_MODEL), F32), grid=(nt,),
        in_specs=in_specs,
        out_specs=pl.BlockSpec((tt, D_MODEL), lambda i: (t0 + i, 0)),
        input_output_aliases=aliases,
        compiler_params=_params(1), name=f"combine{part}",
    )(*args)


def _pad_heads(w, width):
    kdim = w.shape[0]
    w3 = w.reshape(kdim, MLA_HEADS, width)
    return jnp.pad(w3, ((0, 0), (0, 0), (0, LANES - width))).reshape(kdim, MLA_HEADS * LANES)


def _rot_partner(w_rope):
    half = QK_ROPE // 2
    return jnp.concatenate([-w_rope[..., half:], w_rope[..., :half]], axis=-1)


def _lane_vec(nope, rope):
    return jnp.concatenate([nope, rope, jnp.zeros((LANES - QK_DIM,), F32)]).reshape(1, LANES)


def _layer(x2, ada, positions, B, S, g_norm1, w_in, g_cq, w_uq, g_ckv, w_ukv, g_qn, g_kn, g_attn_out,
           conv_w, conv_b, w_mq, w_mk, b_igate, b_fgate, g_mlstm_out, w_out, g_norm2, w_router, router_bias,
           w_gate_exp, w_up_exp, w_down_exp, w_gate_sh, w_up_sh, w_down_sh):
    T = B * S
    sh1, sc1, gt1, sh2, sc2, gt2 = [a.reshape(B, 1, D_MODEL) for a in jnp.split(ada, 6, axis=-1)]
    row = lambda v: v.reshape(1, -1).astype(F32)

    inv_freq = ROPE_THETA ** (-jnp.arange(0, QK_ROPE, 2, dtype=F32) / QK_ROPE)
    ang = positions.astype(F32).reshape(T, 1) * inv_freq
    half = QK_ROPE // 2
    freq_lane = jnp.arange(half)[:, None]
    lane = jnp.arange(LANES)[None, :]
    place = ((lane == QK_NOPE + freq_lane) | (lane == QK_NOPE + half + freq_lane)).astype(F32)
    spread = lambda tbl: jnp.dot(tbl, place, precision=lax.Precision.HIGHEST)
    cosp = (jnp.arange(LANES) < QK_NOPE).astype(F32)[None, :] + spread(jnp.cos(ang))
    sinp = spread(jnp.sin(ang))

    o0 = 0
    w_cq = w_in[:, o0:o0 + Q_LORA]; o0 += Q_LORA
    w_ckv = w_in[:, o0:o0 + KV_LORA]; o0 += KV_LORA
    w_kr = w_in[:, o0:o0 + QK_ROPE]; o0 += QK_ROPE
    w_z = w_in[:, o0:o0 + MLSTM_INNER]; o0 += MLSTM_INNER
    w_v = w_in[:, o0:o0 + MLSTM_INNER]; o0 += MLSTM_INNER
    w_o = w_in[:, o0:o0 + MLSTM_INNER]; o0 += MLSTM_INNER
    w_i = w_in[:, o0:o0 + MLSTM_HEADS]; o0 += MLSTM_HEADS
    w_f = w_in[:, o0:o0 + MLSTM_HEADS]
    zl = jnp.zeros((D_MODEL, QK_NOPE), F32)
    zr = jnp.zeros((D_MODEL, LANES - QK_DIM), F32)
    wa = jnp.concatenate([w_cq, w_ckv, w_z, w_o, zl, w_kr, zr, zl, _rot_partner(w_kr), zr], axis=1).astype(BF16)
    wt = jnp.concatenate([w_v.T, w_i.T, w_f.T, jnp.zeros((16 - 2 * MLSTM_HEADS, D_MODEL), F32)], axis=0).astype(BF16)

    uq3 = w_uq.reshape(Q_LORA, MLA_HEADS, QK_DIM)
    uq_rot = jnp.concatenate([jnp.zeros((Q_LORA, MLA_HEADS, QK_NOPE), F32), _rot_partner(uq3[..., QK_NOPE:])], axis=-1)
    wuq = _pad_heads(w_uq, QK_DIM).astype(BF16)
    wuqs = _pad_heads(uq_rot.reshape(Q_LORA, MLA_HEADS * QK_DIM), QK_DIM).astype(BF16)
    ukv3 = w_ukv.reshape(KV_LORA, MLA_HEADS, QK_NOPE + V_DIM)
    wuk = _pad_heads(ukv3[..., :QK_NOPE].reshape(KV_LORA, MLA_HEADS * QK_NOPE), QK_NOPE).astype(BF16)
    wuv = _pad_heads(ukv3[..., QK_NOPE:].reshape(KV_LORA, MLA_HEADS * V_DIM), V_DIM).astype(BF16)
    vone = jnp.tile((jnp.arange(LANES) == V_DIM).astype(F32), MLA_HEADS).reshape(1, MLA_HEADS * LANES)
    gq = _lane_vec(g_qn[:QK_NOPE], g_qn[QK_NOPE:])
    gqs = _lane_vec(jnp.zeros((QK_NOPE,), F32), _rot_partner(g_qn[QK_NOPE:]) * jnp.concatenate(
        [-jnp.ones((QK_ROPE // 2,), F32), jnp.ones((QK_ROPE // 2,), F32)]))
    gk = _lane_vec(g_kn[:QK_NOPE], g_kn[QK_NOPE:])
    gks = _lane_vec(jnp.zeros((QK_NOPE,), F32), _rot_partner(g_kn[QK_NOPE:]) * jnp.concatenate(
        [-jnp.ones((QK_ROPE // 2,), F32), jnp.ones((QK_ROPE // 2,), F32)]))

    z, og, vt3, g3, q, k, mv = _inproj(
        x2, 1.0 + sc1, sh1, row(g_norm1), wa, wt, row(g_cq), wuq, wuqs, row(g_ckv), wuk, wuv, vone,
        gq, gqs, gk, gks, cosp, sinp, S)
    attn_o = _attn(q, k, mv, B, S)
    mlstm_o = _mlstm(z, vt3, g3, og, conv_w.reshape(CONV_WIDTH, MLSTM_INNER), row(conv_b),
                     w_mq.astype(BF16), w_mk.astype(BF16), b_igate, b_fgate, row(g_mlstm_out), B, S)

    wgu = jnp.concatenate([w_gate_sh, w_up_sh], axis=1).astype(BF16)
    base, h2, lgt = _post(attn_o, mlstm_o, x2, gt1, row(g_attn_out), w_out.astype(BF16), row(g_norm2),
                          1.0 + sc2, sh2, gt2, w_router.T.astype(BF16), wgu, w_down_sh.astype(BF16), S)

    idx_t, w_t, rank_t, cnt = _route(lgt, router_bias.reshape(N_EXPERTS, 1))
    counts = cnt[:, 0].astype(I32)
    padded = (counts + ROW_BLOCK - 1) // ROW_BLOCK * ROW_BLOCK
    padded_end = jnp.cumsum(padded)
    off = padded_end - padded
    n_rows = T * TOP_K + N_EXPERTS * ROW_BLOCK
    dest_t = _dest(idx_t, rank_t, off.reshape(N_EXPERTS, 1))

    dest2_t = jnp.concatenate([dest_t, dest_t + n_rows], axis=1)
    xs = _scatter_rows(dest2_t, h2.reshape(2 * T, HALF), 2 * n_rows).reshape(2, n_rows, HALF)
    y = _experts(off // ROW_BLOCK, padded // ROW_BLOCK, counts, xs, w_gate_exp, w_up_exp, w_down_exp)
    y_rows = y.reshape(2 * n_rows, HALF)
    w_tk = w_t.T
    tp = T // COMBINE_PARTS
    out = None
    for part in range(COMBINE_PARTS):
        dest_p = dest_t[:, part * tp:(part + 1) * tp].reshape(1, TOP_K * tp)
        y_slots = _gather_rows(jnp.concatenate([dest_p, dest_p + n_rows], axis=1), y_rows)
        out = _combine(y_slots, base, w_tk, gt2, S, part, out)
    return out


def kernel(x, c, positions, w_ada, b_ada, g_norm1, w_in, g_cq, w_uq, g_ckv, w_ukv, g_qn, g_kn, g_attn_out, conv_w, conv_b, w_mq, w_mk, b_igate, b_fgate, g_mlstm_out, w_out, g_norm2, w_router, router_bias, w_gate_exp, w_up_exp, w_down_exp, w_gate_sh, w_up_sh, w_down_sh):
    B, S, _ = x.shape
    depth = w_ada.shape[0]
    x2 = x.reshape(B * S, D_MODEL)
    for l in range(depth):
        ada = _ada(c, w_ada[l], b_ada[l])
        x2 = _layer(x2, ada, positions, B, S, g_norm1[l], w_in[l], g_cq[l], w_uq[l], g_ckv[l], w_ukv[l], g_qn[l],
                    g_kn[l], g_attn_out[l], conv_w[l], conv_b[l], w_mq[l], w_mk[l], b_igate[l], b_fgate[l],
                    g_mlstm_out[l], w_out[l], g_norm2[l], w_router[l], router_bias[l], w_gate_exp[l], w_up_exp[l],
                    w_down_exp[l], w_gate_sh[l], w_up_sh[l], w_down_sh[l])
    return x2.reshape(B, S, D_MODEL)
```

```python
import functools
import math

import jax
import jax.numpy as jnp
from jax import lax
from jax.experimental import pallas as pl
from jax.experimental.pallas import tpu as pltpu
from jax.experimental.pallas import tpu_sc as plsc

F32 = jnp.float32
BF16 = jnp.bfloat16
I32 = jnp.int32

D_MODEL = 1024
MLA_HEADS = 8
QK_NOPE = 64
QK_ROPE = 32
QK_DIM = QK_NOPE + QK_ROPE
V_DIM = 64
Q_LORA = 384
KV_LORA = 256
ROPE_THETA = 10000.0
MLSTM_HEADS = 4
MLSTM_DIM = 128
MLSTM_INNER = MLSTM_HEADS * MLSTM_DIM
CONV_WIDTH = 4
N_EXPERTS = 256
TOP_K = 8
N_GROUPS = 8
TOPK_GROUPS = 4
GROUP_SIZE = N_EXPERTS // N_GROUPS
EXPERT_DIM = 256
SHARED_DIM = 256
ROUTED_SCALE = 2.5
RMS_EPS = 1e-6

LANES = 128
SEQ_TILE = 256
WIDE_TILE = 512
ROW_BLOCK = 256
VMEM_LIMIT = 56 * 1024 * 1024

_OFF_CQ = 0
_OFF_CKV = _OFF_CQ + Q_LORA
_OFF_Z = _OFF_CKV + KV_LORA
_OFF_O = _OFF_Z + MLSTM_INNER
_OFF_KR = _OFF_O + MLSTM_INNER
_OFF_KRS = _OFF_KR + LANES
_WA_COLS = _OFF_KRS + LANES
_WT_ROWS = MLSTM_INNER + 16


def _params(n_axes, vmem=VMEM_LIMIT):
    return pltpu.CompilerParams(dimension_semantics=("arbitrary",) * n_axes, vmem_limit_bytes=vmem)


def _dot(a, b):
    return jnp.dot(a, b, preferred_element_type=F32)


def _dot_nt(a, b):
    return lax.dot_general(a, b, (((1,), (1,)), ((), ())), preferred_element_type=F32)


def _sigmoid(x):
    return 1.0 / (1.0 + jnp.exp(-x))


def _silu(x):
    return x * _sigmoid(x)


U32 = jnp.uint32
PACKED = D_MODEL // 2


def _pack_rows(x):
    w = x.shape[1] // 2
    lo = lax.bitcast_convert_type(x[:, :w].astype(BF16).astype(F32), U32)
    hi = lax.bitcast_convert_type(x[:, w:].astype(BF16).astype(F32), U32)
    return lax.shift_right_logical(lo, U32(16)) | (hi & U32(0xFFFF0000))


def _unpack_rows(p):
    lo = lax.bitcast_convert_type(lax.shift_left(p, U32(16)), F32)
    hi = lax.bitcast_convert_type(p & U32(0xFFFF0000), F32)
    return lo, hi


def _ada_body(c_ref, w_ref, b_ref, o_ref):
    c = c_ref[...]
    cond = _silu(c).astype(BF16)
    o_ref[...] = _dot(cond, w_ref[...].astype(BF16)) + b_ref[...]


def _ada(c, w_ada, b_ada):
    B = c.shape[0]
    n = w_ada.shape[1]
    tn = 1024
    return pl.pallas_call(
        _ada_body,
        out_shape=jax.ShapeDtypeStruct((B, n), F32),
        grid=(n // tn,),
        in_specs=[
            pl.BlockSpec((B, D_MODEL), lambda j: (0, 0)),
            pl.BlockSpec((D_MODEL, tn), lambda j: (0, j)),
            pl.BlockSpec((1, tn), lambda j: (0, j)),
        ],
        out_specs=pl.BlockSpec((B, tn), lambda j: (0, j)),
        compiler_params=_params(1),
        name="ada",
    )(c, w_ada, b_ada.reshape(1, n))


def _inproj_body(x_ref, sc_ref, sh_ref, g1_ref, wa_ref, wt_ref, gcq_ref, wuq_ref, wuqs_ref, gckv_ref,
                 wuk_ref, wuv_ref, vone_ref, gq_ref, gqs_ref, gk_ref, gks_ref, cos_ref, sin_ref,
                 z_ref, o_ref, vt_ref, gt_ref, q_ref, k_ref, mv_ref):
    x = x_ref[...]
    hn = x * lax.rsqrt(jnp.mean(x * x, axis=-1, keepdims=True) + RMS_EPS) * g1_ref[...]
    hm = (hn * sc_ref[...] + sh_ref[...]).astype(BF16)
    p = _dot(hm, wa_ref[...])
    z_ref[...] = p[:, _OFF_Z:_OFF_Z + MLSTM_INNER].astype(BF16)
    o_ref[...] = p[:, _OFF_O:_OFF_O + MLSTM_INNER].astype(BF16)
    rt = _dot_nt(wt_ref[...], hm)
    vt_ref[...] = rt[:MLSTM_INNER].astype(BF16)
    gt_ref[...] = rt[MLSTM_INNER:]

    cq = p[:, _OFF_CQ:_OFF_CQ + Q_LORA]
    cqn = (cq * lax.rsqrt(jnp.mean(cq * cq, axis=-1, keepdims=True) + RMS_EPS) * gcq_ref[...]).astype(BF16)
    ckv = p[:, _OFF_CKV:_OFF_CKV + KV_LORA]
    ckvn = (ckv * lax.rsqrt(jnp.mean(ckv * ckv, axis=-1, keepdims=True) + RMS_EPS) * gckv_ref[...]).astype(BF16)
    qa = _dot(cqn, wuq_ref[...])
    qb = _dot(cqn, wuqs_ref[...])
    ka = _dot(ckvn, wuk_ref[...])
    mv_ref[...] = (_dot(ckvn, wuv_ref[...]) + vone_ref[...]).astype(BF16)
    kr = p[:, _OFF_KR:_OFF_KR + LANES]
    krs = p[:, _OFF_KRS:_OFF_KRS + LANES]

    cosp = cos_ref[...]
    sinp = sin_ref[...]
    gqc = gq_ref[...] * cosp
    gqs = gqs_ref[...] * sinp
    gkc = gk_ref[...] * cosp
    k_rot = krs * (gks_ref[...] * sinp)
    kr_ss = jnp.sum(kr * kr, axis=-1, keepdims=True)
    q_scale = (QK_DIM ** -0.5) * math.log2(math.e)
    for h in range(MLA_HEADS):
        sl = slice(h * LANES, (h + 1) * LANES)
        qah = qa[:, sl]
        rq = lax.rsqrt(jnp.sum(qah * qah, axis=-1, keepdims=True) * (1.0 / QK_DIM) + RMS_EPS) * q_scale
        q_ref[:, sl] = ((qah * gqc + qb[:, sl] * gqs) * rq).astype(BF16)
        kah = ka[:, sl]
        rk = lax.rsqrt((jnp.sum(kah * kah, axis=-1, keepdims=True) + kr_ss) * (1.0 / QK_DIM) + RMS_EPS)
        k_ref[:, sl] = (((kah + kr) * gkc + k_rot) * rk).astype(BF16)


def _inproj(x2, sc1p, sh1, g1, wa, wt, gcq, wuq, wuqs, gckv, wuk, wuv, vone, gq, gqs, gk, gks, cosp, sinp, S):
    T = x2.shape[0]
    tm = SEQ_TILE
    nt = T // tm
    per_b = S // tm
    hp = MLA_HEADS * LANES
    full = lambda a: pl.BlockSpec(a.shape, lambda i: (0,) * a.ndim)
    row = lambda w: pl.BlockSpec((tm, w), lambda i: (i, 0))
    bmod = pl.BlockSpec((None, 1, D_MODEL), lambda i: (i // per_b, 0, 0))
    out_shapes = (
        jax.ShapeDtypeStruct((T, MLSTM_INNER), BF16),
        jax.ShapeDtypeStruct((T, MLSTM_INNER), BF16),
        jax.ShapeDtypeStruct((nt, MLSTM_INNER, tm), BF16),
        jax.ShapeDtypeStruct((nt, 16, tm), F32),
        jax.ShapeDtypeStruct((T, hp), BF16),
        jax.ShapeDtypeStruct((T, hp), BF16),
        jax.ShapeDtypeStruct((T, hp), BF16),
    )
    out_specs = (
        row(MLSTM_INNER), row(MLSTM_INNER),
        pl.BlockSpec((None, MLSTM_INNER, tm), lambda i: (i, 0, 0)),
        pl.BlockSpec((None, 16, tm), lambda i: (i, 0, 0)),
        row(hp), row(hp), row(hp),
    )
    in_specs = [row(D_MODEL), bmod, bmod, full(g1), full(wa), full(wt), full(gcq), full(wuq), full(wuqs),
                full(gckv), full(wuk), full(wuv), full(vone), full(gq), full(gqs), full(gk), full(gks),
                row(LANES), row(LANES)]
    return pl.pallas_call(
        _inproj_body, out_shape=out_shapes, grid=(nt,), in_specs=in_specs, out_specs=out_specs,
        compiler_params=_params(1), name="inproj",
    )(x2, sc1p, sh1, g1, wa, wt, gcq, wuq, wuqs, gckv, wuk, wuv, vone, gq, gqs, gk, gks, cosp, sinp)


ATTN_HEADS_PER_STEP = 4


def _pairs_loop(n, fn, init):
    c = lax.fori_loop(0, n // 2, lambda jj, c: fn(2 * jj + 1, fn(2 * jj, c)), init)
    return lax.fori_loop(0, n % 2, lambda _, c: fn(n - 1, c), c)


def _attn_body(q_ref, k_ref, v_ref, o_ref, s_scr):
    i = pl.program_id(2)
    t = SEQ_TILE
    nh = ATTN_HEADS_PER_STEP
    qry_pos = lax.broadcasted_iota(I32, (t, t), 0)
    key_pos = lax.broadcasted_iota(I32, (t, t), 1)
    causal = key_pos <= qry_pos
    qs = [q_ref[:, hh * LANES:(hh + 1) * LANES] for hh in range(nh)]

    def scores(j, hh, diag):
        kj = k_ref[pl.ds(pl.multiple_of(j * t, t), t), hh * LANES:(hh + 1) * LANES]
        s = _dot_nt(qs[hh], kj)
        return jnp.where(causal, s, -jnp.inf) if diag else s

    def halves(x):
        return x[:, :LANES], x[:, LANES:]

    def max_pass(j, mx, diag):
        out = []
        for hh in range(nh):
            s = scores(j, hh, diag)
            s_scr[j, hh] = s
            lo, hi = halves(s)
            out.append(jnp.maximum(mx[hh], jnp.maximum(lo, hi)))
        return tuple(out)

    mx = tuple(jnp.full((t, LANES), -jnp.inf, F32) for _ in range(nh))
    mx = _pairs_loop(i, lambda j, c: max_pass(j, c, False), mx)
    mx = max_pass(i, mx, True)
    mb = [jnp.broadcast_to(jnp.max(mx[hh], axis=1, keepdims=True), (t, LANES)) for hh in range(nh)]

    def acc_pass(j, carry, diag):
        row0 = pl.multiple_of(j * t, t)
        out = []
        for hh in range(nh):
            lo, hi = halves(s_scr[j, hh])
            p = jnp.concatenate([jnp.exp2(lo - mb[hh]), jnp.exp2(hi - mb[hh])], axis=1).astype(BF16)
            out.append(carry[hh] + _dot(p, v_ref[pl.ds(row0, t), hh * LANES:(hh + 1) * LANES]))
        return tuple(out)

    init = tuple(jnp.zeros((t, LANES), F32) for _ in range(nh))
    carry = _pairs_loop(i, lambda j, c: acc_pass(j, c, False), init)
    final = acc_pass(i, carry, True)
    outs = [acc / acc[:, V_DIM:V_DIM + 1] for acc in final]
    first_head = lax.broadcasted_iota(I32, (t, LANES), 1) < V_DIM
    o_ref[...] = jnp.concatenate(
        [jnp.where(first_head, outs[2 * pr], pltpu.roll(outs[2 * pr + 1], V_DIM, axis=1)) for pr in range(nh // 2)],
        axis=1).astype(BF16)


def _attn(q, k, mvt, B, S):
    T = q.shape[0]
    t = SEQ_TILE
    nh = ATTN_HEADS_PER_STEP
    per_b = S // t
    return pl.pallas_call(
        _attn_body,
        out_shape=jax.ShapeDtypeStruct((T, MLA_HEADS * V_DIM), BF16),
        grid=(B, MLA_HEADS // nh, per_b),
        in_specs=[
            pl.BlockSpec((t, nh * LANES), lambda b, hp, i: (b * per_b + i, hp)),
            pl.BlockSpec((S, nh * LANES), lambda b, hp, i: (b, hp)),
            pl.BlockSpec((S, nh * LANES), lambda b, hp, i: (b, hp)),
        ],
        out_specs=pl.BlockSpec((t, nh * V_DIM), lambda b, hp, i: (b * per_b + i, hp)),
        scratch_shapes=[pltpu.VMEM((per_b, nh, t, t), F32)],
        compiler_params=_params(3), name="attn",
    )(q, k, mvt)


def _split3(a):
    hi = a.astype(BF16)
    r = a - hi.astype(F32)
    mid = r.astype(BF16)
    lo = (r - mid.astype(F32)).astype(BF16)
    return hi, mid, lo


def _mlstm_body(bi_ref, bf_ref, z_ref, vt_ref, g_ref, og_ref, cw_ref, cb_ref, wq_ref, wk_ref, gout_ref,
                out_ref, zpad, q_s, k_s, st_s, m_s, *, S):
    h = pl.program_id(1)
    L = SEQ_TILE
    nchunk = S // L
    hd = MLSTM_DIM
    zpad[0:8, :] = jnp.zeros((8, hd), F32)
    zpad[8:, :] = z_ref[...].astype(F32)
    zc = cb_ref[...] + jnp.zeros((S, hd), F32)
    for j in range(CONV_WIDTH):
        zc = zc + cw_ref[j:j + 1, :] * zpad[pl.ds(8 - (CONV_WIDTH - 1) + j, S), :]
    zs = _silu(zc).astype(BF16)
    q_s[...] = _dot(zs, wq_ref[...]).astype(BF16)
    k_s[...] = (_dot(zs, wk_ref[...]) * (hd ** -0.5)).astype(BF16)

    st_s[...] = jnp.zeros_like(st_s)
    m_s[...] = jnp.zeros_like(m_s)
    b_i = bi_ref[h]
    b_f = bf_ref[h]
    r_i = lax.broadcasted_iota(I32, (L, L), 0)
    c_i = lax.broadcasted_iota(I32, (L, L), 1)
    tril = r_i >= c_i
    strict_lower = jnp.where(r_i > c_i, 1.0, 0.0).astype(BF16)
    ones_row = jnp.where(lax.broadcasted_iota(I32, (hd, L), 0) == 0, 1.0, 0.0).astype(BF16)

    def chunk(c, _):
        start = pl.multiple_of(c * L, L)
        qc = q_s[pl.ds(start, L), :]
        kc = k_s[pl.ds(start, L), :]
        vt_aug = jnp.concatenate([vt_ref[c], ones_row], axis=0)
        li = g_ref[c, pl.ds(h, 1), :] + b_i
        fp = g_ref[c, pl.ds(MLSTM_HEADS + h, 1), :] + b_f
        lf = jnp.minimum(fp, 0.0) - jnp.log(1.0 + jnp.exp(-jnp.abs(fp)))
        a_mat = jnp.where(tril, lf, 0.0)
        a_hi, a_mid, a_lo = _split3(a_mat)
        d0 = _dot(a_hi, strict_lower) + _dot(a_mid, strict_lower) + _dot(a_lo, strict_lower)
        dmat = jnp.where(tril, d0 + li, -jnp.inf)
        b_col = jnp.sum(a_mat, axis=1, keepdims=True)
        m_prev = m_s[...]
        inter = b_col + m_prev
        mt = jnp.maximum(inter, jnp.max(dmat, axis=1, keepdims=True))
        w = jnp.exp(dmat - mt) * _dot_nt(qc, kc)
        decay = jnp.exp(inter - mt)
        st = st_s[...]
        num_aug = decay * _dot_nt(qc, st.astype(BF16)) + _dot_nt(w.astype(BF16), vt_aug)
        num = num_aug[:, :hd]
        den = num_aug[:, hd:hd + 1]
        hv = num / jnp.maximum(jnp.abs(den), jnp.exp(-mt))
        g = d0[L - 1:L, :] + li
        b_last = b_col[L - 1:L, :]
        m_new = jnp.maximum(b_last + m_prev, jnp.max(g, axis=1, keepdims=True))
        a = jnp.exp(g - m_new)
        cd = jnp.exp(b_last + m_prev - m_new)
        st_s[...] = cd * st + _dot((vt_aug.astype(F32) * a).astype(BF16), kc)
        m_s[...] = m_new
        hn = hv * lax.rsqrt(jnp.mean(hv * hv, axis=-1, keepdims=True) + RMS_EPS) * gout_ref[...]
        og = og_ref[pl.ds(start, L), :].astype(F32)
        out_ref[pl.ds(start, L), :] = (_sigmoid(og) * hn).astype(BF16)
        return 0

    lax.fori_loop(0, nchunk, chunk, 0)


def _mlstm(z, vt3, g3, og, conv_w, conv_b, wq, wk, b_i, b_f, g_out, B, S):
    T = z.shape[0]
    L = SEQ_TILE
    per_b = S // L
    hd = MLSTM_DIM
    smem = pl.BlockSpec(memory_space=pltpu.SMEM)
    seq = pl.BlockSpec((S, hd), lambda b, h: (b, h))
    return pl.pallas_call(
        functools.partial(_mlstm_body, S=S),
        out_shape=jax.ShapeDtypeStruct((T, MLSTM_INNER), BF16),
        grid=(B, MLSTM_HEADS),
        in_specs=[
            smem, smem, seq,
            pl.BlockSpec((per_b, hd, L), lambda b, h: (b, h, 0)),
            pl.BlockSpec((per_b, 16, L), lambda b, h: (b, 0, 0)),
            seq,
            pl.BlockSpec((CONV_WIDTH, hd), lambda b, h: (0, h)),
            pl.BlockSpec((1, hd), lambda b, h: (0, h)),
            pl.BlockSpec((None, hd, hd), lambda b, h: (h, 0, 0)),
            pl.BlockSpec((None, hd, hd), lambda b, h: (h, 0, 0)),
            pl.BlockSpec((1, hd), lambda b, h: (0, h)),
        ],
        out_specs=seq,
        scratch_shapes=[
            pltpu.VMEM((S + 8, hd), F32),
            pltpu.VMEM((S, hd), BF16),
            pltpu.VMEM((S, hd), BF16),
            pltpu.VMEM((2 * hd, hd), F32),
            pltpu.VMEM((1, 1), F32),
        ],
        compiler_params=_params(2), name="mlstm",
    )(b_i, b_f, z, vt3, g3, og, conv_w, conv_b, wq, wk, g_out)


def _post_body(ao_ref, mo_ref, x_ref, gt1_ref, gattn_ref, wout_ref, g2_ref, sc2_ref, sh2_ref, gt2_ref,
               wrt_ref, wgu_ref, wds_ref, base_ref, h2_ref, lg_ref):
    ao = ao_ref[...].astype(F32)
    aon = ao * lax.rsqrt(jnp.mean(ao * ao, axis=-1, keepdims=True) + RMS_EPS) * gattn_ref[...]
    mix_in = jnp.concatenate([aon.astype(BF16), mo_ref[...]], axis=1)
    x1 = x_ref[...] + gt1_ref[...] * _dot(mix_in, wout_ref[...])
    hn = x1 * lax.rsqrt(jnp.mean(x1 * x1, axis=-1, keepdims=True) + RMS_EPS) * g2_ref[...]
    h2 = hn * sc2_ref[...] + sh2_ref[...]
    words = _pack_rows(h2)
    h2_ref[0] = words[:, :HALF]
    h2_ref[1] = words[:, HALF:]
    h2b = h2.astype(BF16)
    gu = _dot(h2b, wgu_ref[...])
    act = (_silu(gu[:, :SHARED_DIM]) * gu[:, SHARED_DIM:]).astype(BF16)
    base_ref[...] = x1 + gt2_ref[...] * _dot(act, wds_ref[...])
    lg_ref[...] = _dot_nt(wrt_ref[...], h2b)


def _post(ao, mo, x2, gt1, gattn, wout, g2, sc2p, sh2, gt2, wrt, wgu, wds, S):
    T = x2.shape[0]
    tm = WIDE_TILE
    per_b = S // tm
    full = lambda a: pl.BlockSpec(a.shape, lambda i: (0,) * a.ndim)
    row = lambda w: pl.BlockSpec((tm, w), lambda i: (i, 0))
    bmod = pl.BlockSpec((None, 1, D_MODEL), lambda i: (i // per_b, 0, 0))
    return pl.pallas_call(
        _post_body,
        out_shape=(jax.ShapeDtypeStruct((T, D_MODEL), F32), jax.ShapeDtypeStruct((2, T, HALF), U32),
                   jax.ShapeDtypeStruct((N_EXPERTS, T), F32)),
        grid=(T // tm,),
        in_specs=[row(MLA_HEADS * V_DIM), row(MLSTM_INNER), row(D_MODEL), bmod, full(gattn), full(wout),
                  full(g2), bmod, bmod, bmod, full(wrt), full(wgu), full(wds)],
        out_specs=(row(D_MODEL), pl.BlockSpec((2, tm, HALF), lambda i: (0, i, 0)),
                   pl.BlockSpec((N_EXPERTS, tm), lambda i: (0, i))),
        compiler_params=_params(1), name="post",
    )(ao, mo, x2, gt1, gattn, wout, g2, sc2p, sh2, gt2, wrt, wgu, wds)


def _first_max(x, rows, n):
    mx = jnp.max(x, axis=0, keepdims=True)
    idx = jnp.min(jnp.where(x == mx, rows, n), axis=0, keepdims=True)
    return mx, idx


def _route_body(lg_ref, bias_ref, idx_ref, w_ref, rank_ref, cnt_ref, carry):
    i = pl.program_id(0)
    tt = lg_ref.shape[1]

    @pl.when(i == 0)
    def _():
        carry[...] = jnp.zeros_like(carry)

    sc = _sigmoid(lg_ref[...])
    bi = sc + bias_ref[...]
    rows = lax.broadcasted_iota(I32, (N_EXPERTS, tt), 0)
    rows_g = lax.broadcasted_iota(I32, (GROUP_SIZE, tt), 0)
    rows_n = lax.broadcasted_iota(I32, (N_GROUPS, tt), 0)
    neg = -jnp.inf
    gs = []
    for g in range(N_GROUPS):
        xg = bi[g * GROUP_SIZE:(g + 1) * GROUP_SIZE]
        m1, i1 = _first_max(xg, rows_g, GROUP_SIZE)
        m2 = jnp.max(jnp.where(rows_g == i1, neg, xg), axis=0, keepdims=True)
        gs.append(m1 + m2)
    cur = jnp.concatenate(gs, axis=0)
    keep = jnp.zeros((N_GROUPS, tt), F32)
    for _ in range(TOPK_GROUPS):
        _, gi = _first_max(cur, rows_n, N_GROUPS)
        sel = rows_n == gi
        keep = jnp.where(sel, 1.0, keep)
        cur = jnp.where(sel, neg, cur)
    masked = jnp.concatenate(
        [jnp.where(keep[g:g + 1] > 0.0, bi[g * GROUP_SIZE:(g + 1) * GROUP_SIZE], neg) for g in range(N_GROUPS)], axis=0)
    cur = masked
    onehot = jnp.zeros((N_EXPERTS, tt), F32)
    idxs, ws = [], []
    for _ in range(TOP_K):
        _, ei = _first_max(cur, rows, N_EXPERTS)
        sel = rows == ei
        ws.append(jnp.sum(jnp.where(sel, sc, 0.0), axis=0, keepdims=True))
        idxs.append(ei)
        cur = jnp.where(sel, neg, cur)
        onehot = jnp.where(sel, 1.0, onehot)
    wsum = ws[0]
    for k in range(1, TOP_K):
        wsum = wsum + ws[k]
    idx_ref[...] = jnp.concatenate(idxs, axis=0)
    w_ref[...] = jnp.concatenate([w / wsum * ROUTED_SCALE for w in ws], axis=0)
    r_i = lax.broadcasted_iota(I32, (tt, tt), 0)
    c_i = lax.broadcasted_iota(I32, (tt, tt), 1)
    before = jnp.where(r_i < c_i, 1.0, 0.0).astype(BF16)
    tot = _dot(onehot.astype(BF16), before) + carry[...]
    rank_ref[...] = jnp.concatenate(
        [jnp.sum(jnp.where(rows == ei, tot, 0.0), axis=0, keepdims=True) for ei in idxs], axis=0).astype(I32)
    carry[...] = carry[...] + jnp.sum(onehot, axis=1, keepdims=True)
    cnt_ref[...] = jnp.broadcast_to(carry[...], cnt_ref.shape)


def _route(lgt, bias_col):
    T = lgt.shape[1]
    tt = WIDE_TILE
    blk = pl.BlockSpec((TOP_K, tt), lambda i: (0, i))
    return pl.pallas_call(
        _route_body,
        out_shape=(jax.ShapeDtypeStruct((TOP_K, T), I32), jax.ShapeDtypeStruct((TOP_K, T), F32),
                   jax.ShapeDtypeStruct((TOP_K, T), I32), jax.ShapeDtypeStruct((N_EXPERTS, LANES), F32)),
        grid=(T // tt,),
        in_specs=[pl.BlockSpec((N_EXPERTS, tt), lambda i: (0, i)), pl.BlockSpec((N_EXPERTS, 1), lambda i: (0, 0))],
        out_specs=(blk, blk, blk, pl.BlockSpec((N_EXPERTS, LANES), lambda i: (0, 0))),
        scratch_shapes=[pltpu.VMEM((N_EXPERTS, 1), F32)],
        compiler_params=_params(1), name="route",
    )(lgt, bias_col)


def _dest_body(idx_ref, rank_ref, off_ref, dest_ref):
    tt = idx_ref.shape[1]
    rows = lax.broadcasted_iota(I32, (N_EXPERTS, tt), 0)
    off = off_ref[...]
    outs = []
    for k in range(TOP_K):
        sel = rows == idx_ref[k:k + 1, :]
        outs.append(jnp.sum(jnp.where(sel, off, 0), axis=0, keepdims=True) + rank_ref[k:k + 1, :])
    dest_ref[...] = jnp.concatenate(outs, axis=0)


def _dest(idx_t, rank_t, off_col):
    T = idx_t.shape[1]
    tt = min(4 * SEQ_TILE, T)
    blk = pl.BlockSpec((TOP_K, tt), lambda i: (0, i))
    return pl.pallas_call(
        _dest_body, out_shape=jax.ShapeDtypeStruct((TOP_K, T), I32), grid=(T // tt,),
        in_specs=[blk, blk, pl.BlockSpec((N_EXPERTS, 1), lambda i: (0, 0))], out_specs=blk,
        compiler_params=_params(1), name="dest",
    )(idx_t, rank_t, off_col)


SC_WINDOW = 128
HALF = PACKED // 2


def _sc_mesh():
    return plsc.VectorSubcoreMesh(core_axis_name="c", subcore_axis_name="s")


def _scatter_rows(dest2_t, rows, n_out):
    n = rows.shape[0]

    @pl.kernel(out_type=jax.ShapeDtypeStruct((n_out, HALF), U32), mesh=_sc_mesh(), scratch_types=[])
    def scatter(d_hbm, x_hbm, o_hbm):
        def body(d_vmem, x_vmem):
            for k in range(TOP_K):
                pltpu.sync_copy(x_vmem, o_hbm.at[d_vmem.at[k]])

        pltpu.emit_pipeline(
            body, grid=(n // SC_WINDOW,),
            in_specs=[pl.BlockSpec((TOP_K, SC_WINDOW), lambda i: (0, i)),
                      pl.BlockSpec((SC_WINDOW, HALF), lambda i: (i, 0))],
            out_specs=[], core_axis_name=("c", "s"), dimension_semantics=(pltpu.PARALLEL,),
        )(d_hbm, x_hbm)

    return scatter(dest2_t, rows)


def _gather_rows(index_row, rows):
    n = index_row.shape[1]

    @pl.kernel(out_type=jax.ShapeDtypeStruct((n, HALF), U32), mesh=_sc_mesh(), scratch_types=[])
    def gather(i_hbm, y_hbm, o_hbm):
        def body(i_vmem, o_vmem):
            pltpu.sync_copy(y_hbm.at[i_vmem.at[0]], o_vmem)

        pltpu.emit_pipeline(
            body, grid=(n // SC_WINDOW,),
            in_specs=[pl.BlockSpec((1, SC_WINDOW), lambda i: (0, i))],
            out_specs=[pl.BlockSpec((SC_WINDOW, HALF), lambda i: (i, 0))],
            core_axis_name=("c", "s"), dimension_semantics=(pltpu.PARALLEL,),
        )(i_hbm, o_hbm)

    return gather(index_row, rows)


EXPERT_RING = 8


def _experts_body(first_ref, nblk_ref, cnt_ref, xs_ref, wg_ref, wu_ref, wd_ref, y_ref,
                  xbuf, ybuf, wgb, wub, wdb, xsem, ysem):
    e = pl.program_id(0)
    last = pl.num_programs(0) - 1
    b0 = first_ref[e]
    nb = nblk_ref[e]
    total = first_ref[last] + nblk_ref[last]
    ring = xbuf.shape[0]

    def rows(g):
        return pl.ds(pl.multiple_of(g * ROW_BLOCK, ROW_BLOCK), ROW_BLOCK)

    def x_copy(g):
        slot = g % ring
        return pltpu.make_async_copy(xs_ref.at[:, rows(g), :], xbuf.at[slot], xsem.at[slot])

    def y_copy(g):
        slot = g % ring
        return pltpu.make_async_copy(ybuf.at[slot], y_ref.at[:, rows(g), :], ysem.at[slot])

    @pl.when(e == 0)
    def _():
        for g in range(ring - 1):
            @pl.when(g < total)
            def _():
                x_copy(g).start()

    @pl.when(nb > 0)
    def _():
        wgb[...] = wg_ref[...].astype(BF16)
        wub[...] = wu_ref[...].astype(BF16)
        wdb[...] = wd_ref[...].astype(BF16)

    def prefetch(g):
        @pl.when(g + ring - 1 < total)
        def _():
            x_copy(g + ring - 1).start()

    def compute(i):
        slot = (b0 + i) % ring
        words = jnp.concatenate([xbuf[slot, 0], xbuf[slot, 1]], axis=1)
        n_live = cnt_ref[e] - i * ROW_BLOCK
        words = jnp.where(lax.broadcasted_iota(I32, words.shape, 0) < n_live, words, U32(0))
        x = jnp.concatenate(_unpack_rows(words), axis=1).astype(BF16)
        act = (_silu(_dot(x, wgb[...])) * _dot(x, wub[...])).astype(BF16)
        y = _pack_rows(_dot(act, wdb[...]))
        ybuf[slot, 0] = y[:, :HALF]
        ybuf[slot, 1] = y[:, HALF:]

    def blocks(i, n):
        g0 = b0 + i
        prefetch(g0)
        for d in range(n):
            x_copy(g0 + d).wait()

            @pl.when(g0 + d >= ring)
            def _():
                y_copy(g0 + d - ring).wait()

        for d in range(n):
            compute(i + d)
        for d in range(1, n):
            prefetch(g0 + d)
        for d in range(n):
            y_copy(g0 + d).start()

    def pair(ii, carry):
        blocks(2 * ii, 2)
        return carry

    def single(_, carry):
        blocks(nb - 1, 1)
        return carry

    lax.fori_loop(0, nb // 2, pair, 0)
    lax.fori_loop(0, nb % 2, single, 0)

    @pl.when(e == last)
    def _():
        for back in range(ring, 0, -1):
            @pl.when(total >= back)
            def _():
                y_copy(total - back).wait()


def _experts(first_block, n_blocks, counts, xs, wg, wu, wd):
    n_rows = xs.shape[1]
    wmap = lambda e, fb, nb, cnt: (e, 0, 0)
    anyspec = pl.BlockSpec(memory_space=pl.ANY)
    ring = pltpu.VMEM((EXPERT_RING, 2, ROW_BLOCK, HALF), U32)
    grid_spec = pltpu.PrefetchScalarGridSpec(
        num_scalar_prefetch=3, grid=(N_EXPERTS,),
        in_specs=[
            anyspec,
            pl.BlockSpec((None, D_MODEL, EXPERT_DIM), wmap),
            pl.BlockSpec((None, D_MODEL, EXPERT_DIM), wmap),
            pl.BlockSpec((None, EXPERT_DIM, D_MODEL), wmap),
        ],
        out_specs=anyspec,
        scratch_shapes=[ring, ring,
                        pltpu.VMEM((D_MODEL, EXPERT_DIM), BF16), pltpu.VMEM((D_MODEL, EXPERT_DIM), BF16),
                        pltpu.VMEM((EXPERT_DIM, D_MODEL), BF16),
                        pltpu.SemaphoreType.DMA((EXPERT_RING,)), pltpu.SemaphoreType.DMA((EXPERT_RING,))],
    )
    return pl.pallas_call(
        _experts_body, out_shape=jax.ShapeDtypeStruct((2, n_rows, HALF), U32), grid_spec=grid_spec,
        compiler_params=_params(1), name="experts",
    )(first_block, n_blocks, counts, xs, wg, wu, wd)


COMBINE_PARTS = 1


def _combine_body(*refs):
    ya_refs = refs[:TOP_K]
    yb_refs = refs[TOP_K:2 * TOP_K]
    base_ref, w_ref, gt2_ref = refs[2 * TOP_K:2 * TOP_K + 3]
    out_ref = refs[-1]
    w = w_ref[...]
    r_lo = r_hi = None
    for k in range(TOP_K):
        lo, hi = _unpack_rows(jnp.concatenate([ya_refs[k][...], yb_refs[k][...]], axis=1))
        wk = w[:, k:k + 1]
        r_lo = wk * lo if r_lo is None else r_lo + wk * lo
        r_hi = wk * hi if r_hi is None else r_hi + wk * hi
    out_ref[...] = base_ref[...] + gt2_ref[...] * jnp.concatenate([r_lo, r_hi], axis=1)


def _combine(y_slots, base, w_tk, gt2, S, part, prev_out):
    T = base.shape[0]
    tt = WIDE_TILE
    nt = T // tt // COMBINE_PARTS
    t0 = part * nt
    per_b = S // tt
    yspec = lambda half, k: pl.BlockSpec((tt, HALF), lambda i: ((half * TOP_K + k) * nt + i, 0))
    in_specs = [yspec(half, k) for half in range(2) for k in range(TOP_K)] + [
        pl.BlockSpec((tt, D_MODEL), lambda i: (t0 + i, 0)),
        pl.BlockSpec((tt, TOP_K), lambda i: (t0 + i, 0)),
        pl.BlockSpec((None, 1, D_MODEL), lambda i: ((t0 + i) // per_b, 0, 0)),
    ]
    args = [y_slots] * (2 * TOP_K) + [base, w_tk, gt2]
    aliases = {}
    if prev_out is not None:
        in_specs.append(pl.BlockSpec(memory_space=pl.ANY))
        args.append(prev_out)
        aliases = {len(args) - 1: 0}
    return pl.pallas_call(
        _combine_body, out_shape=jax.ShapeDtypeStruct((T, D_MODEL), F32), grid=(nt,),
        in_specs=in_specs,
        out_specs=pl.BlockSpec((tt, D_MODEL), lambda i: (t0 + i, 0)),
        input_output_aliases=aliases,
        compiler_params=_params(1), name=f"combine{part}",
    )(*args)


def _pad_heads(w, width):
    kdim = w.shape[0]
    w3 = w.reshape(kdim, MLA_HEADS, width)
    return jnp.pad(w3, ((0, 0), (0, 0), (0, LANES - width))).reshape(kdim, MLA_HEADS * LANES)


def _rot_partner(w_rope):
    half = QK_ROPE // 2
    return jnp.concatenate([-w_rope[..., half:], w_rope[..., :half]], axis=-1)


def _lane_vec(nope, rope):
    return jnp.concatenate([nope, rope, jnp.zeros((LANES - QK_DIM,), F32)]).reshape(1, LANES)


def _layer(x2, ada, positions, B, S, g_norm1, w_in, g_cq, w_uq, g_ckv, w_ukv, g_qn, g_kn, g_attn_out,
           conv_w, conv_b, w_mq, w_mk, b_igate, b_fgate, g_mlstm_out, w_out, g_norm2, w_router, router_bias,
           w_gate_exp, w_up_exp, w_down_exp, w_gate_sh, w_up_sh, w_down_sh):
    T = B * S
    sh1, sc1, gt1, sh2, sc2, gt2 = [a.reshape(B, 1, D_MODEL) for a in jnp.split(ada, 6, axis=-1)]
    row = lambda v: v.reshape(1, -1).astype(F32)

    inv_freq = ROPE_THETA ** (-jnp.arange(0, QK_ROPE, 2, dtype=F32) / QK_ROPE)
    ang = positions.astype(F32).reshape(T, 1) * inv_freq
    half = QK_ROPE // 2
    freq_lane = jnp.arange(half)[:, None]
    lane = jnp.arange(LANES)[None, :]
    place = ((lane == QK_NOPE + freq_lane) | (lane == QK_NOPE + half + freq_lane)).astype(F32)
    spread = lambda tbl: jnp.dot(tbl, place, precision=lax.Precision.HIGHEST)
    cosp = (jnp.arange(LANES) < QK_NOPE).astype(F32)[None, :] + spread(jnp.cos(ang))
    sinp = spread(jnp.sin(ang))

    o0 = 0
    w_cq = w_in[:, o0:o0 + Q_LORA]; o0 += Q_LORA
    w_ckv = w_in[:, o0:o0 + KV_LORA]; o0 += KV_LORA
    w_kr = w_in[:, o0:o0 + QK_ROPE]; o0 += QK_ROPE
    w_z = w_in[:, o0:o0 + MLSTM_INNER]; o0 += MLSTM_INNER
    w_v = w_in[:, o0:o0 + MLSTM_INNER]; o0 += MLSTM_INNER
    w_o = w_in[:, o0:o0 + MLSTM_INNER]; o0 += MLSTM_INNER
    w_i = w_in[:, o0:o0 + MLSTM_HEADS]; o0 += MLSTM_HEADS
    w_f = w_in[:, o0:o0 + MLSTM_HEADS]
    zl = jnp.zeros((D_MODEL, QK_NOPE), F32)
    zr = jnp.zeros((D_MODEL, LANES - QK_DIM), F32)
    wa = jnp.concatenate([w_cq, w_ckv, w_z, w_o, zl, w_kr, zr, zl, _rot_partner(w_kr), zr], axis=1).astype(BF16)
    wt = jnp.concatenate([w_v.T, w_i.T, w_f.T, jnp.zeros((16 - 2 * MLSTM_HEADS, D_MODEL), F32)], axis=0).astype(BF16)

    uq3 = w_uq.reshape(Q_LORA, MLA_HEADS, QK_DIM)
    uq_rot = jnp.concatenate([jnp.zeros((Q_LORA, MLA_HEADS, QK_NOPE), F32), _rot_partner(uq3[..., QK_NOPE:])], axis=-1)
    wuq = _pad_heads(w_uq, QK_DIM).astype(BF16)
    wuqs = _pad_heads(uq_rot.reshape(Q_LORA, MLA_HEADS * QK_DIM), QK_DIM).astype(BF16)
    ukv3 = w_ukv.reshape(KV_LORA, MLA_HEADS, QK_NOPE + V_DIM)
    wuk = _pad_heads(ukv3[..., :QK_NOPE].reshape(KV_LORA, MLA_HEADS * QK_NOPE), QK_NOPE).astype(BF16)
    wuv = _pad_heads(ukv3[..., QK_NOPE:].reshape(KV_LORA, MLA_HEADS * V_DIM), V_DIM).astype(BF16)
    vone = jnp.tile((jnp.arange(LANES) == V_DIM).astype(F32), MLA_HEADS).reshape(1, MLA_HEADS * LANES)
    gq = _lane_vec(g_qn[:QK_NOPE], g_qn[QK_NOPE:])
    gqs = _lane_vec(jnp.zeros((QK_NOPE,), F32), _rot_partner(g_qn[QK_NOPE:]) * jnp.concatenate(
        [-jnp.ones((QK_ROPE // 2,), F32), jnp.ones((QK_ROPE // 2,), F32)]))
    gk = _lane_vec(g_kn[:QK_NOPE], g_kn[QK_NOPE:])
    gks = _lane_vec(jnp.zeros((QK_NOPE,), F32), _rot_partner(g_kn[QK_NOPE:]) * jnp.concatenate(
        [-jnp.ones((QK_ROPE // 2,), F32), jnp.ones((QK_ROPE // 2,), F32)]))

    z, og, vt3, g3, q, k, mv = _inproj(
        x2, 1.0 + sc1, sh1, row(g_norm1), wa, wt, row(g_cq), wuq, wuqs, row(g_ckv), wuk, wuv, vone,
        gq, gqs, gk, gks, cosp, sinp, S)
    attn_o = _attn(q, k, mv, B, S)
    mlstm_o = _mlstm(z, vt3, g3, og, conv_w.reshape(CONV_WIDTH, MLSTM_INNER), row(conv_b),
                     w_mq.astype(BF16), w_mk.astype(BF16), b_igate, b_fgate, row(g_mlstm_out), B, S)

    wgu = jnp.concatenate([w_gate_sh, w_up_sh], axis=1).astype(BF16)
    base, h2, lgt = _post(attn_o, mlstm_o, x2, gt1, row(g_attn_out), w_out.astype(BF16), row(g_norm2),
                          1.0 + sc2, sh2, gt2, w_router.T.astype(BF16), wgu, w_down_sh.astype(BF16), S)

    idx_t, w_t, rank_t, cnt = _route(lgt, router_bias.reshape(N_EXPERTS, 1))
    counts = cnt[:, 0].astype(I32)
    padded = (counts + ROW_BLOCK - 1) // ROW_BLOCK * ROW_BLOCK
    padded_end = jnp.cumsum(padded)
    off = padded_end - padded
    n_rows = T * TOP_K + N_EXPERTS * ROW_BLOCK
    dest_t = _dest(idx_t, rank_t, off.reshape(N_EXPERTS, 1))

    dest2_t = jnp.concatenate([dest_t, dest_t + n_rows], axis=1)
    xs = _scatter_rows(dest2_t, h2.reshape(2 * T, HALF), 2 * n_rows).reshape(2, n_rows, HALF)
    y = _experts(off // ROW_BLOCK, padded // ROW_BLOCK, counts, xs, w_gate_exp, w_up_exp, w_down_exp)
    y_rows = y.reshape(2 * n_rows, HALF)
    w_tk = w_t.T
    tp = T // COMBINE_PARTS
    out = None
    for part in range(COMBINE_PARTS):
        dest_p = dest_t[:, part * tp:(part + 1) * tp].reshape(1, TOP_K * tp)
        y_slots = _gather_rows(jnp.concatenate([dest_p, dest_p + n_rows], axis=1), y_rows)
        out = _combine(y_slots, base, w_tk, gt2, S, part, out)
    return out


def kernel(x, c, positions, w_ada, b_ada, g_norm1, w_in, g_cq, w_uq, g_ckv, w_ukv, g_qn, g_kn, g_attn_out, conv_w, conv_b, w_mq, w_mk, b_igate, b_fgate, g_mlstm_out, w_out, g_norm2, w_router, router_bias, w_gate_exp, w_up_exp, w_down_exp, w_gate_sh, w_up_sh, w_down_sh):
    B, S, _ = x.shape
    depth = w_ada.shape[0]
    x2 = x.reshape(B * S, D_MODEL)
    for l in range(depth):
        ada = _ada(c, w_ada[l], b_ada[l])
        x2 = _layer(x2, ada, positions, B, S, g_norm1[l], w_in[l], g_cq[l], w_uq[l], g_ckv[l], w_ukv[l], g_qn[l],
                    g_kn[l], g_attn_out[l], conv_w[l], conv_b[l], w_mq[l], w_mk[l], b_igate[l], b_fgate[l],
                    g_mlstm_out[l], w_out[l], g_norm2[l], w_router[l], router_bias[l], w_gate_exp[l], w_up_exp[l],
                    w_down_exp[l], w_gate_sh[l], w_up_sh[l], w_down_sh[l])
    return x2.reshape(B, S, D_MODEL)
```

```python
import functools
import math

import jax
import jax.numpy as jnp
from jax import lax
from jax.experimental import pallas as pl
from jax.experimental.pallas import tpu as pltpu
from jax.experimental.pallas import tpu_sc as plsc

F32 = jnp.float32
BF16 = jnp.bfloat16
I32 = jnp.int32

D_MODEL = 1024
MLA_HEADS = 8
QK_NOPE = 64
QK_ROPE = 32
QK_DIM = QK_NOPE + QK_ROPE
V_DIM = 64
Q_LORA = 384
KV_LORA = 256
ROPE_THETA = 10000.0
MLSTM_HEADS = 4
MLSTM_DIM = 128
MLSTM_INNER = MLSTM_HEADS * MLSTM_DIM
CONV_WIDTH = 4
N_EXPERTS = 256
TOP_K = 8
N_GROUPS = 8
TOPK_GROUPS = 4
GROUP_SIZE = N_EXPERTS // N_GROUPS
EXPERT_DIM = 256
SHARED_DIM = 256
ROUTED_SCALE = 2.5
RMS_EPS = 1e-6

LANES = 128
SEQ_TILE = 256
WIDE_TILE = 512
ROW_BLOCK = 256
VMEM_LIMIT = 56 * 1024 * 1024

_OFF_CQ = 0
_OFF_CKV = _OFF_CQ + Q_LORA
_OFF_Z = _OFF_CKV + KV_LORA
_OFF_O = _OFF_Z + MLSTM_INNER
_OFF_KR = _OFF_O + MLSTM_INNER
_OFF_KRS = _OFF_KR + LANES
_WA_COLS = _OFF_KRS + LANES
_WT_ROWS = MLSTM_INNER + 16


def _params(n_axes, vmem=VMEM_LIMIT):
    return pltpu.CompilerParams(dimension_semantics=("arbitrary",) * n_axes, vmem_limit_bytes=vmem)


def _dot(a, b):
    return jnp.dot(a, b, preferred_element_type=F32)


def _dot_nt(a, b):
    return lax.dot_general(a, b, (((1,), (1,)), ((), ())), preferred_element_type=F32)


def _sigmoid(x):
    return 1.0 / (1.0 + jnp.exp(-x))


def _silu(x):
    return x * _sigmoid(x)


U32 = jnp.uint32
PACKED = D_MODEL // 2


def _pack_rows(x):
    w = x.shape[1] // 2
    lo = lax.bitcast_convert_type(x[:, :w].astype(BF16).astype(F32), U32)
    hi = lax.bitcast_convert_type(x[:, w:].astype(BF16).astype(F32), U32)
    return lax.shift_right_logical(lo, U32(16)) | (hi & U32(0xFFFF0000))


def _unpack_rows(p):
    lo = lax.bitcast_convert_type(lax.shift_left(p, U32(16)), F32)
    hi = lax.bitcast_convert_type(p & U32(0xFFFF0000), F32)
    return lo, hi


def _ada_body(c_ref, w_ref, b_ref, o_ref):
    c = c_ref[...]
    cond = _silu(c).astype(BF16)
    o_ref[...] = _dot(cond, w_ref[...].astype(BF16)) + b_ref[...]


def _ada(c, w_ada, b_ada):
    B = c.shape[0]
    n = w_ada.shape[1]
    tn = 1024
    return pl.pallas_call(
        _ada_body,
        out_shape=jax.ShapeDtypeStruct((B, n), F32),
        grid=(n // tn,),
        in_specs=[
            pl.BlockSpec((B, D_MODEL), lambda j: (0, 0)),
            pl.BlockSpec((D_MODEL, tn), lambda j: (0, j)),
            pl.BlockSpec((1, tn), lambda j: (0, j)),
        ],
        out_specs=pl.BlockSpec((B, tn), lambda j: (0, j)),
        compiler_params=_params(1),
        name="ada",
    )(c, w_ada, b_ada.reshape(1, n))


def _inproj_body(x_ref, sc_ref, sh_ref, g1_ref, wa_ref, wt_ref, gcq_ref, wuq_ref, wuqs_ref, gckv_ref,
                 wuk_ref, wuv_ref, vone_ref, gq_ref, gqs_ref, gk_ref, gks_ref, cos_ref, sin_ref,
                 z_ref, o_ref, vt_ref, gt_ref, q_ref, k_ref, mv_ref):
    x = x_ref[...]
    hn = x * lax.rsqrt(jnp.mean(x * x, axis=-1, keepdims=True) + RMS_EPS) * g1_ref[...]
    hm = (hn * sc_ref[...] + sh_ref[...]).astype(BF16)
    p = _dot(hm, wa_ref[...])
    z_ref[...] = p[:, _OFF_Z:_OFF_Z + MLSTM_INNER].astype(BF16)
    o_ref[...] = p[:, _OFF_O:_OFF_O + MLSTM_INNER].astype(BF16)
    rt = _dot_nt(wt_ref[...], hm)
    vt_ref[...] = rt[:MLSTM_INNER].astype(BF16)
    gt_ref[...] = rt[MLSTM_INNER:]

    cq = p[:, _OFF_CQ:_OFF_CQ + Q_LORA]
    cqn = (cq * lax.rsqrt(jnp.mean(cq * cq, axis=-1, keepdims=True) + RMS_EPS) * gcq_ref[...]).astype(BF16)
    ckv = p[:, _OFF_CKV:_OFF_CKV + KV_LORA]
    ckvn = (ckv * lax.rsqrt(jnp.mean(ckv * ckv, axis=-1, keepdims=True) + RMS_EPS) * gckv_ref[...]).astype(BF16)
    qa = _dot(cqn, wuq_ref[...])
    qb = _dot(cqn, wuqs_ref[...])
    ka = _dot(ckvn, wuk_ref[...])
    mv_ref[...] = (_dot(ckvn, wuv_ref[...]) + vone_ref[...]).astype(BF16)
    kr = p[:, _OFF_KR:_OFF_KR + LANES]
    krs = p[:, _OFF_KRS:_OFF_KRS + LANES]

    cosp = cos_ref[...]
    sinp = sin_ref[...]
    gqc = gq_ref[...] * cosp
    gqs = gqs_ref[...] * sinp
    gkc = gk_ref[...] * cosp
    k_rot = krs * (gks_ref[...] * sinp)
    kr_ss = jnp.sum(kr * kr, axis=-1, keepdims=True)
    q_scale = (QK_DIM ** -0.5) * math.log2(math.e)
    for h in range(MLA_HEADS):
        sl = slice(h * LANES, (h + 1) * LANES)
        qah = qa[:, sl]
        rq = lax.rsqrt(jnp.sum(qah * qah, axis=-1, keepdims=True) * (1.0 / QK_DIM) + RMS_EPS) * q_scale
        q_ref[:, sl] = ((qah * gqc + qb[:, sl] * gqs) * rq).astype(BF16)
        kah = ka[:, sl]
        rk = lax.rsqrt((jnp.sum(kah * kah, axis=-1, keepdims=True) + kr_ss) * (1.0 / QK_DIM) + RMS_EPS)
        k_ref[:, sl] = (((kah + kr) * gkc + k_rot) * rk).astype(BF16)


def _inproj(x2, sc1p, sh1, g1, wa, wt, gcq, wuq, wuqs, gckv, wuk, wuv, vone, gq, gqs, gk, gks, cosp, sinp, S):
    T = x2.shape[0]
    tm = SEQ_TILE
    nt = T // tm
    per_b = S // tm
    hp = MLA_HEADS * LANES
    full = lambda a: pl.BlockSpec(a.shape, lambda i: (0,) * a.ndim)
    row = lambda w: pl.BlockSpec((tm, w), lambda i: (i, 0))
    bmod = pl.BlockSpec((None, 1, D_MODEL), lambda i: (i // per_b, 0, 0))
    out_shapes = (
        jax.ShapeDtypeStruct((T, MLSTM_INNER), BF16),
        jax.ShapeDtypeStruct((T, MLSTM_INNER), BF16),
        jax.ShapeDtypeStruct((nt, MLSTM_INNER, tm), BF16),
        jax.ShapeDtypeStruct((nt, 16, tm), F32),
        jax.ShapeDtypeStruct((T, hp), BF16),
        jax.ShapeDtypeStruct((T, hp), BF16),
        jax.ShapeDtypeStruct((T, hp), BF16),
    )
    out_specs = (
        row(MLSTM_INNER), row(MLSTM_INNER),
        pl.BlockSpec((None, MLSTM_INNER, tm), lambda i: (i, 0, 0)),
        pl.BlockSpec((None, 16, tm), lambda i: (i, 0, 0)),
        row(hp), row(hp), row(hp),
    )
    in_specs = [row(D_MODEL), bmod, bmod, full(g1), full(wa), full(wt), full(gcq), full(wuq), full(wuqs),
                full(gckv), full(wuk), full(wuv), full(vone), full(gq), full(gqs), full(gk), full(gks),
                row(LANES), row(LANES)]
    return pl.pallas_call(
        _inproj_body, out_shape=out_shapes, grid=(nt,), in_specs=in_specs, out_specs=out_specs,
        compiler_params=_params(1), name="inproj",
    )(x2, sc1p, sh1, g1, wa, wt, gcq, wuq, wuqs, gckv, wuk, wuv, vone, gq, gqs, gk, gks, cosp, sinp)


ATTN_HEADS_PER_STEP = 4


def _pairs_loop(n, fn, init):
    c = lax.fori_loop(0, n // 2, lambda jj, c: fn(2 * jj + 1, fn(2 * jj, c)), init)
    return lax.fori_loop(0, n % 2, lambda _, c: fn(n - 1, c), c)


def _attn_body(q_ref, k_ref, v_ref, o_ref, s_scr):
    i = pl.program_id(2)
    t = SEQ_TILE
    nh = ATTN_HEADS_PER_STEP
    qry_pos = lax.broadcasted_iota(I32, (t, t), 0)
    key_pos = lax.broadcasted_iota(I32, (t, t), 1)
    causal = key_pos <= qry_pos
    qs = [q_ref[:, hh * LANES:(hh + 1) * LANES] for hh in range(nh)]

    def scores(j, hh, diag):
        kj = k_ref[pl.ds(pl.multiple_of(j * t, t), t), hh * LANES:(hh + 1) * LANES]
        s = _dot_nt(qs[hh], kj)
        return jnp.where(causal, s, -jnp.inf) if diag else s

    def halves(x):
        return x[:, :LANES], x[:, LANES:]

    def max_pass(j, mx, diag):
        out = []
        for hh in range(nh):
            s = scores(j, hh, diag)
            s_scr[j, hh] = s
            lo, hi = halves(s)
            out.append(jnp.maximum(mx[hh], jnp.maximum(lo, hi)))
        return tuple(out)

    mx = tuple(jnp.full((t, LANES), -jnp.inf, F32) for _ in range(nh))
    mx = _pairs_loop(i, lambda j, c: max_pass(j, c, False), mx)
    mx = max_pass(i, mx, True)
    mb = [jnp.broadcast_to(jnp.max(mx[hh], axis=1, keepdims=True), (t, LANES)) for hh in range(nh)]

    def acc_pass(j, carry, diag):
        row0 = pl.multiple_of(j * t, t)
        out = []
        for hh in range(nh):
            lo, hi = halves(s_scr[j, hh])
            p = jnp.concatenate([jnp.exp2(lo - mb[hh]), jnp.exp2(hi - mb[hh])], axis=1).astype(BF16)
            out.append(carry[hh] + _dot(p, v_ref[pl.ds(row0, t), hh * LANES:(hh + 1) * LANES]))
        return tuple(out)

    init = tuple(jnp.zeros((t, LANES), F32) for _ in range(nh))
    carry = _pairs_loop(i, lambda j, c: acc_pass(j, c, False), init)
    final = acc_pass(i, carry, True)
    outs = [acc / acc[:, V_DIM:V_DIM + 1] for acc in final]
    first_head = lax.broadcasted_iota(I32, (t, LANES), 1) < V_DIM
    o_ref[...] = jnp.concatenate(
        [jnp.where(first_head, outs[2 * pr], pltpu.roll(outs[2 * pr + 1], V_DIM, axis=1)) for pr in range(nh // 2)],
        axis=1).astype(BF16)


def _attn(q, k, mvt, B, S):
    T = q.shape[0]
    t = SEQ_TILE
    nh = ATTN_HEADS_PER_STEP
    per_b = S // t
    return pl.pallas_call(
        _attn_body,
        out_shape=jax.ShapeDtypeStruct((T, MLA_HEADS * V_DIM), BF16),
        grid=(B, MLA_HEADS // nh, per_b),
        in_specs=[
            pl.BlockSpec((t, nh * LANES), lambda b, hp, i: (b * per_b + i, hp)),
            pl.BlockSpec((S, nh * LANES), lambda b, hp, i: (b, hp)),
            pl.BlockSpec((S, nh * LANES), lambda b, hp, i: (b, hp)),
        ],
        out_specs=pl.BlockSpec((t, nh * V_DIM), lambda b, hp, i: (b * per_b + i, hp)),
        scratch_shapes=[pltpu.VMEM((per_b, nh, t, t), F32)],
        compiler_params=_params(3), name="attn",
    )(q, k, mvt)


def _split3(a):
    hi = a.astype(BF16)
    r = a - hi.astype(F32)
    mid = r.astype(BF16)
    lo = (r - mid.astype(F32)).astype(BF16)
    return hi, mid, lo


def _mlstm_body(bi_ref, bf_ref, z_ref, vt_ref, g_ref, og_ref, cw_ref, cb_ref, wq_ref, wk_ref, gout_ref,
                out_ref, zpad, q_s, k_s, st_s, m_s, *, S):
    h = pl.program_id(1)
    L = SEQ_TILE
    nchunk = S // L
    hd = MLSTM_DIM
    zpad[0:8, :] = jnp.zeros((8, hd), F32)
    zpad[8:, :] = z_ref[...].astype(F32)
    zc = cb_ref[...] + jnp.zeros((S, hd), F32)
    for j in range(CONV_WIDTH):
        zc = zc + cw_ref[j:j + 1, :] * zpad[pl.ds(8 - (CONV_WIDTH - 1) + j, S), :]
    zs = _silu(zc).astype(BF16)
    q_s[...] = _dot(zs, wq_ref[...]).astype(BF16)
    k_s[...] = (_dot(zs, wk_ref[...]) * (hd ** -0.5)).astype(BF16)

    st_s[...] = jnp.zeros_like(st_s)
    m_s[...] = jnp.zeros_like(m_s)
    b_i = bi_ref[h]
    b_f = bf_ref[h]
    r_i = lax.broadcasted_iota(I32, (L, L), 0)
    c_i = lax.broadcasted_iota(I32, (L, L), 1)
    tril = r_i >= c_i
    strict_lower = jnp.where(r_i > c_i, 1.0, 0.0).astype(BF16)
    ones_row = jnp.where(lax.broadcasted_iota(I32, (hd, L), 0) == 0, 1.0, 0.0).astype(BF16)

    def chunk(c, _):
        start = pl.multiple_of(c * L, L)
        qc = q_s[pl.ds(start, L), :]
        kc = k_s[pl.ds(start, L), :]
        vt_aug = jnp.concatenate([vt_ref[c], ones_row], axis=0)
        li = g_ref[c, pl.ds(h, 1), :] + b_i
        fp = g_ref[c, pl.ds(MLSTM_HEADS + h, 1), :] + b_f
        lf = jnp.minimum(fp, 0.0) - jnp.log(1.0 + jnp.exp(-jnp.abs(fp)))
        a_mat = jnp.where(tril, lf, 0.0)
        a_hi, a_mid, a_lo = _split3(a_mat)
        d0 = _dot(a_hi, strict_lower) + _dot(a_mid, strict_lower) + _dot(a_lo, strict_lower)
        dmat = jnp.where(tril, d0 + li, -jnp.inf)
        b_col = jnp.sum(a_mat, axis=1, keepdims=True)
        m_prev = m_s[...]
        inter = b_col + m_prev
        mt = jnp.maximum(inter, jnp.max(dmat, axis=1, keepdims=True))
        w = jnp.exp(dmat - mt) * _dot_nt(qc, kc)
        decay = jnp.exp(inter - mt)
        st = st_s[...]
        num_aug = decay * _dot_nt(qc, st.astype(BF16)) + _dot_nt(w.astype(BF16), vt_aug)
        num = num_aug[:, :hd]
        den = num_aug[:, hd:hd + 1]
        hv = num / jnp.maximum(jnp.abs(den), jnp.exp(-mt))
        g = d0[L - 1:L, :] + li
        b_last = b_col[L - 1:L, :]
        m_new = jnp.maximum(b_last + m_prev, jnp.max(g, axis=1, keepdims=True))
        a = jnp.exp(g - m_new)
        cd = jnp.exp(b_last + m_prev - m_new)
        st_s[...] = cd * st + _dot((vt_aug.astype(F32) * a).astype(BF16), kc)
        m_s[...] = m_new
        hn = hv * lax.rsqrt(jnp.mean(hv * hv, axis=-1, keepdims=True) + RMS_EPS) * gout_ref[...]
        og = og_ref[pl.ds(start, L), :].astype(F32)
        out_ref[pl.ds(start, L), :] = (_sigmoid(og) * hn).astype(BF16)
        return 0

    lax.fori_loop(0, nchunk, chunk, 0)


def _mlstm(z, vt3, g3, og, conv_w, conv_b, wq, wk, b_i, b_f, g_out, B, S):
    T = z.shape[0]
    L = SEQ_TILE
    per_b = S // L
    hd = MLSTM_DIM
    smem = pl.BlockSpec(memory_space=pltpu.SMEM)
    seq = pl.BlockSpec((S, hd), lambda b, h: (b, h))
    return pl.pallas_call(
        functools.partial(_mlstm_body, S=S),
        out_shape=jax.ShapeDtypeStruct((T, MLSTM_INNER), BF16),
        grid=(B, MLSTM_HEADS),
        in_specs=[
            smem, smem, seq,
            pl.BlockSpec((per_b, hd, L), lambda b, h: (b, h, 0)),
            pl.BlockSpec((per_b, 16, L), lambda b, h: (b, 0, 0)),
            seq,
            pl.BlockSpec((CONV_WIDTH, hd), lambda b, h: (0, h)),
            pl.BlockSpec((1, hd), lambda b, h: (0, h)),
            pl.BlockSpec((None, hd, hd), lambda b, h: (h, 0, 0)),
            pl.BlockSpec((None, hd, hd), lambda b, h: (h, 0, 0)),
            pl.BlockSpec((1, hd), lambda b, h: (0, h)),
        ],
        out_specs=seq,
        scratch_shapes=[
            pltpu.VMEM((S + 8, hd), F32),
            pltpu.VMEM((S, hd), BF16),
            pltpu.VMEM((S, hd), BF16),
            pltpu.VMEM((2 * hd, hd), F32),
            pltpu.VMEM((1, 1), F32),
        ],
        compiler_params=_params(2), name="mlstm",
    )(b_i, b_f, z, vt3, g3, og, conv_w, conv_b, wq, wk, g_out)


def _post_body(ao_ref, mo_ref, x_ref, gt1_ref, gattn_ref, wout_ref, g2_ref, sc2_ref, sh2_ref, gt2_ref,
               wrt_ref, wgu_ref, wds_ref, base_ref, h2_ref, lg_ref):
    ao = ao_ref[...].astype(F32)
    aon = ao * lax.rsqrt(jnp.mean(ao * ao, axis=-1, keepdims=True) + RMS_EPS) * gattn_ref[...]
    mix_in = jnp.concatenate([aon.astype(BF16), mo_ref[...]], axis=1)
    x1 = x_ref[...] + gt1_ref[...] * _dot(mix_in, wout_ref[...])
    hn = x1 * lax.rsqrt(jnp.mean(x1 * x1, axis=-1, keepdims=True) + RMS_EPS) * g2_ref[...]
    h2 = hn * sc2_ref[...] + sh2_ref[...]
    words = _pack_rows(h2)
    h2_ref[0] = words[:, :HALF]
    h2_ref[1] = words[:, HALF:]
    h2b = h2.astype(BF16)
    gu = _dot(h2b, wgu_ref[...])
    act = (_silu(gu[:, :SHARED_DIM]) * gu[:, SHARED_DIM:]).astype(BF16)
    base_ref[...] = x1 + gt2_ref[...] * _dot(act, wds_ref[...])
    lg_ref[...] = _dot_nt(wrt_ref[...], h2b)


def _post(ao, mo, x2, gt1, gattn, wout, g2, sc2p, sh2, gt2, wrt, wgu, wds, S):
    T = x2.shape[0]
    tm = WIDE_TILE
    per_b = S // tm
    full = lambda a: pl.BlockSpec(a.shape, lambda i: (0,) * a.ndim)
    row = lambda w: pl.BlockSpec((tm, w), lambda i: (i, 0))
    bmod = pl.BlockSpec((None, 1, D_MODEL), lambda i: (i // per_b, 0, 0))
    return pl.pallas_call(
        _post_body,
        out_shape=(jax.ShapeDtypeStruct((T, D_MODEL), F32), jax.ShapeDtypeStruct((2, T, HALF), U32),
                   jax.ShapeDtypeStruct((N_EXPERTS, T), F32)),
        grid=(T // tm,),
        in_specs=[row(MLA_HEADS * V_DIM), row(MLSTM_INNER), row(D_MODEL), bmod, full(gattn), full(wout),
                  full(g2), bmod, bmod, bmod, full(wrt), full(wgu), full(wds)],
        out_specs=(row(D_MODEL), pl.BlockSpec((2, tm, HALF), lambda i: (0, i, 0)),
                   pl.BlockSpec((N_EXPERTS, tm), lambda i: (0, i))),
        compiler_params=_params(1), name="post",
    )(ao, mo, x2, gt1, gattn, wout, g2, sc2p, sh2, gt2, wrt, wgu, wds)


def _first_max(x, rows, n):
    mx = jnp.max(x, axis=0, keepdims=True)
    idx = jnp.min(jnp.where(x == mx, rows, n), axis=0, keepdims=True)
    return mx, idx


def _route_body(lg_ref, bias_ref, idx_ref, w_ref, rank_ref, cnt_ref, carry):
    i = pl.program_id(0)
    tt = lg_ref.shape[1]

    @pl.when(i == 0)
    def _():
        carry[...] = jnp.zeros_like(carry)

    sc = _sigmoid(lg_ref[...])
    bi = sc + bias_ref[...]
    rows = lax.broadcasted_iota(I32, (N_EXPERTS, tt), 0)
    rows_g = lax.broadcasted_iota(I32, (GROUP_SIZE, tt), 0)
    rows_n = lax.broadcasted_iota(I32, (N_GROUPS, tt), 0)
    neg = -jnp.inf
    gs = []
    for g in range(N_GROUPS):
        xg = bi[g * GROUP_SIZE:(g + 1) * GROUP_SIZE]
        m1, i1 = _first_max(xg, rows_g, GROUP_SIZE)
        m2 = jnp.max(jnp.where(rows_g == i1, neg, xg), axis=0, keepdims=True)
        gs.append(m1 + m2)
    cur = jnp.concatenate(gs, axis=0)
    keep = jnp.zeros((N_GROUPS, tt), F32)
    for _ in range(TOPK_GROUPS):
        _, gi = _first_max(cur, rows_n, N_GROUPS)
        sel = rows_n == gi
        keep = jnp.where(sel, 1.0, keep)
        cur = jnp.where(sel, neg, cur)
    masked = jnp.concatenate(
        [jnp.where(keep[g:g + 1] > 0.0, bi[g * GROUP_SIZE:(g + 1) * GROUP_SIZE], neg) for g in range(N_GROUPS)], axis=0)
    cur = masked
    onehot = jnp.zeros((N_EXPERTS, tt), F32)
    idxs, ws = [], []
    for _ in range(TOP_K):
        _, ei = _first_max(cur, rows, N_EXPERTS)
        sel = rows == ei
        ws.append(jnp.sum(jnp.where(sel, sc, 0.0), axis=0, keepdims=True))
        idxs.append(ei)
        cur = jnp.where(sel, neg, cur)
        onehot = jnp.where(sel, 1.0, onehot)
    wsum = ws[0]
    for k in range(1, TOP_K):
        wsum = wsum + ws[k]
    idx_ref[...] = jnp.concatenate(idxs, axis=0)
    w_ref[...] = jnp.concatenate([w / wsum * ROUTED_SCALE for w in ws], axis=0)
    r_i = lax.broadcasted_iota(I32, (tt, tt), 0)
    c_i = lax.broadcasted_iota(I32, (tt, tt), 1)
    before = jnp.where(r_i < c_i, 1.0, 0.0).astype(BF16)
    tot = _dot(onehot.astype(BF16), before) + carry[...]
    rank_ref[...] = jnp.concatenate(
        [jnp.sum(jnp.where(rows == ei, tot, 0.0), axis=0, keepdims=True) for ei in idxs], axis=0).astype(I32)
    carry[...] = carry[...] + jnp.sum(onehot, axis=1, keepdims=True)
    cnt_ref[...] = jnp.broadcast_to(carry[...], cnt_ref.shape)


def _route(lgt, bias_col):
    T = lgt.shape[1]
    tt = WIDE_TILE
    blk = pl.BlockSpec((TOP_K, tt), lambda i: (0, i))
    return pl.pallas_call(
        _route_body,
        out_shape=(jax.ShapeDtypeStruct((TOP_K, T), I32), jax.ShapeDtypeStruct((TOP_K, T), F32),
                   jax.ShapeDtypeStruct((TOP_K, T), I32), jax.ShapeDtypeStruct((N_EXPERTS, LANES), F32)),
        grid=(T // tt,),
        in_specs=[pl.BlockSpec((N_EXPERTS, tt), lambda i: (0, i)), pl.BlockSpec((N_EXPERTS, 1), lambda i: (0, 0))],
        out_specs=(blk, blk, blk, pl.BlockSpec((N_EXPERTS, LANES), lambda i: (0, 0))),
        scratch_shapes=[pltpu.VMEM((N_EXPERTS, 1), F32)],
        compiler_params=_params(1), name="route",
    )(lgt, bias_col)


def _dest_body(idx_ref, rank_ref, off_ref, dest_ref):
    tt = idx_ref.shape[1]
    rows = lax.broadcasted_iota(I32, (N_EXPERTS, tt), 0)
    off = off_ref[...]
    outs = []
    for k in range(TOP_K):
        sel = rows == idx_ref[k:k + 1, :]
        outs.append(jnp.sum(jnp.where(sel, off, 0), axis=0, keepdims=True) + rank_ref[k:k + 1, :])
    dest_ref[...] = jnp.concatenate(outs, axis=0)


def _dest(idx_t, rank_t, off_col):
    T = idx_t.shape[1]
    tt = min(4 * SEQ_TILE, T)
    blk = pl.BlockSpec((TOP_K, tt), lambda i: (0, i))
    return pl.pallas_call(
        _dest_body, out_shape=jax.ShapeDtypeStruct((TOP_K, T), I32), grid=(T // tt,),
        in_specs=[blk, blk, pl.BlockSpec((N_EXPERTS, 1), lambda i: (0, 0))], out_specs=blk,
        compiler_params=_params(1), name="dest",
    )(idx_t, rank_t, off_col)


SC_WINDOW = 128
HALF = PACKED // 2


def _sc_mesh():
    return plsc.VectorSubcoreMesh(core_axis_name="c", subcore_axis_name="s")


def _scatter_rows(dest2_t, rows, n_out):
    n = rows.shape[0]

    @pl.kernel(out_type=jax.ShapeDtypeStruct((n_out, HALF), U32), mesh=_sc_mesh(), scratch_types=[])
    def scatter(d_hbm, x_hbm, o_hbm):
        def body(d_vmem, x_vmem):
            for k in range(TOP_K):
                pltpu.sync_copy(x_vmem, o_hbm.at[d_vmem.at[k]])

        pltpu.emit_pipeline(
            body, grid=(n // SC_WINDOW,),
            in_specs=[pl.BlockSpec((TOP_K, SC_WINDOW), lambda i: (0, i)),
                      pl.BlockSpec((SC_WINDOW, HALF), lambda i: (i, 0))],
            out_specs=[], core_axis_name=("c", "s"), dimension_semantics=(pltpu.PARALLEL,),
        )(d_hbm, x_hbm)

    return scatter(dest2_t, rows)


def _gather_rows(index_row, rows):
    n = index_row.shape[1]

    @pl.kernel(out_type=jax.ShapeDtypeStruct((n, HALF), U32), mesh=_sc_mesh(), scratch_types=[])
    def gather(i_hbm, y_hbm, o_hbm):
        def body(i_vmem, o_vmem):
            pltpu.sync_copy(y_hbm.at[i_vmem.at[0]], o_vmem)

        pltpu.emit_pipeline(
            body, grid=(n // SC_WINDOW,),
            in_specs=[pl.BlockSpec((1, SC_WINDOW), lambda i: (0, i))],
            out_specs=[pl.BlockSpec((SC_WINDOW, HALF), lambda i: (i, 0))],
            core_axis_name=("c", "s"), dimension_semantics=(pltpu.PARALLEL,),
        )(i_hbm, o_hbm)

    return gather(index_row, rows)


EXPERT_RING = 16
WEIGHT_RING = 3


def _experts_body(first_ref, nblk_ref, cnt_ref, xs_ref, wg_ref, wu_ref, wd_ref, y_ref,
                  xbuf, ybuf, wgf, wuf, wdf, wgb, wub, wdb, xsem, ysem, wsem):
    e = pl.program_id(0)
    last = pl.num_programs(0) - 1
    wring = wgf.shape[0]

    def w_copies(ex):
        slot = ex % wring
        return [pltpu.make_async_copy(src.at[ex], dst.at[slot], wsem.at[slot])
                for src, dst in ((wg_ref, wgf), (wu_ref, wuf), (wd_ref, wdf))]

    @pl.when(e == 0)
    def _():
        for ex in range(wring - 1):
            for cp in w_copies(ex):
                cp.start()

    b0 = first_ref[e]
    nb = nblk_ref[e]
    total = first_ref[last] + nblk_ref[last]
    ring = xbuf.shape[0]

    def rows(g):
        return pl.ds(pl.multiple_of(g * ROW_BLOCK, ROW_BLOCK), ROW_BLOCK)

    def x_copy(g):
        slot = g % ring
        return pltpu.make_async_copy(xs_ref.at[:, rows(g), :], xbuf.at[slot], xsem.at[slot])

    def y_copy(g):
        slot = g % ring
        return pltpu.make_async_copy(ybuf.at[slot], y_ref.at[:, rows(g), :], ysem.at[slot])

    @pl.when(e == 0)
    def _():
        for g in range(ring - 1):
            @pl.when(g < total)
            def _():
                x_copy(g).start()

    for cp in w_copies(e):
        cp.wait()

    @pl.when(nb > 0)
    def _():
        slot = e % wring
        wgb[...] = wgf[slot].astype(BF16)
        wub[...] = wuf[slot].astype(BF16)
        wdb[...] = wdf[slot].astype(BF16)

    @pl.when(e + wring - 1 <= last)
    def _():
        for cp in w_copies(e + wring - 1):
            cp.start()

    def prefetch(g):
        @pl.when(g + ring - 1 < total)
        def _():
            x_copy(g + ring - 1).start()

    def compute(i):
        slot = (b0 + i) % ring
        words = jnp.concatenate([xbuf[slot, 0], xbuf[slot, 1]], axis=1)
        n_live = cnt_ref[e] - i * ROW_BLOCK
        words = jnp.where(lax.broadcasted_iota(I32, words.shape, 0) < n_live, words, U32(0))
        x = jnp.concatenate(_unpack_rows(words), axis=1).astype(BF16)
        act = (_silu(_dot(x, wgb[...])) * _dot(x, wub[...])).astype(BF16)
        y = _pack_rows(_dot(act, wdb[...]))
        ybuf[slot, 0] = y[:, :HALF]
        ybuf[slot, 1] = y[:, HALF:]

    def blocks(i, n):
        g0 = b0 + i
        prefetch(g0)
        for d in range(n):
            x_copy(g0 + d).wait()

            @pl.when(g0 + d >= ring)
            def _():
                y_copy(g0 + d - ring).wait()

        for d in range(n):
            compute(i + d)
        for d in range(1, n):
            prefetch(g0 + d)
        for d in range(n):
            y_copy(g0 + d).start()

    def pair(ii, carry):
        blocks(2 * ii, 2)
        return carry

    def single(_, carry):
        blocks(nb - 1, 1)
        return carry

    lax.fori_loop(0, nb // 2, pair, 0)
    lax.fori_loop(0, nb % 2, single, 0)

    @pl.when(e == last)
    def _():
        for back in range(ring, 0, -1):
            @pl.when(total >= back)
            def _():
                y_copy(total - back).wait()


def _experts(first_block, n_blocks, counts, xs, wg, wu, wd):
    n_rows = xs.shape[1]
    anyspec = pl.BlockSpec(memory_space=pl.ANY)
    ring = pltpu.VMEM((EXPERT_RING, 2, ROW_BLOCK, HALF), U32)
    grid_spec = pltpu.PrefetchScalarGridSpec(
        num_scalar_prefetch=3, grid=(N_EXPERTS,),
        in_specs=[anyspec] * 4,
        out_specs=anyspec,
        scratch_shapes=[ring, ring,
                        pltpu.VMEM((WEIGHT_RING, D_MODEL, EXPERT_DIM), F32),
                        pltpu.VMEM((WEIGHT_RING, D_MODEL, EXPERT_DIM), F32),
                        pltpu.VMEM((WEIGHT_RING, EXPERT_DIM, D_MODEL), F32),
                        pltpu.VMEM((D_MODEL, EXPERT_DIM), BF16), pltpu.VMEM((D_MODEL, EXPERT_DIM), BF16),
                        pltpu.VMEM((EXPERT_DIM, D_MODEL), BF16),
                        pltpu.SemaphoreType.DMA((EXPERT_RING,)), pltpu.SemaphoreType.DMA((EXPERT_RING,)),
                        pltpu.SemaphoreType.DMA((WEIGHT_RING,))],
    )
    return pl.pallas_call(
        _experts_body, out_shape=jax.ShapeDtypeStruct((2, n_rows, HALF), U32), grid_spec=grid_spec,
        compiler_params=_params(1), name="experts",
    )(first_block, n_blocks, counts, xs, wg, wu, wd)


def _combine_body(*refs):
    ya_refs = refs[:TOP_K]
    yb_refs = refs[TOP_K:2 * TOP_K]
    base_ref, w_ref, gt2_ref, out_ref = refs[2 * TOP_K:]
    w = w_ref[...]
    r_lo = r_hi = None
    for k in range(TOP_K):
        lo, hi = _unpack_rows(jnp.concatenate([ya_refs[k][...], yb_refs[k][...]], axis=1))
        wk = w[:, k:k + 1]
        r_lo = wk * lo if r_lo is None else r_lo + wk * lo
        r_hi = wk * hi if r_hi is None else r_hi + wk * hi
    out_ref[...] = base_ref[...] + gt2_ref[...] * jnp.concatenate([r_lo, r_hi], axis=1)


def _combine(y_slots, base, w_tk, gt2, S):
    T = base.shape[0]
    tt = WIDE_TILE
    nt = T // tt
    per_b = S // tt
    yspec = lambda half, k: pl.BlockSpec((tt, HALF), lambda i: ((half * TOP_K + k) * nt + i, 0))
    return pl.pallas_call(
        _combine_body, out_shape=jax.ShapeDtypeStruct((T, D_MODEL), F32), grid=(nt,),
        in_specs=[yspec(half, k) for half in range(2) for k in range(TOP_K)] + [
            pl.BlockSpec((tt, D_MODEL), lambda i: (i, 0)),
            pl.BlockSpec((tt, TOP_K), lambda i: (i, 0)),
            pl.BlockSpec((None, 1, D_MODEL), lambda i: (i // per_b, 0, 0)),
        ],
        out_specs=pl.BlockSpec((tt, D_MODEL), lambda i: (i, 0)),
        compiler_params=_params(1), name="combine",
    )(*([y_slots] * (2 * TOP_K)), base, w_tk, gt2)


def _pad_heads(w, width):
    kdim = w.shape[0]
    w3 = w.reshape(kdim, MLA_HEADS, width)
    return jnp.pad(w3, ((0, 0), (0, 0), (0, LANES - width))).reshape(kdim, MLA_HEADS * LANES)


def _rot_partner(w_rope):
    half = QK_ROPE // 2
    return jnp.concatenate([-w_rope[..., half:], w_rope[..., :half]], axis=-1)


def _lane_vec(nope, rope):
    return jnp.concatenate([nope, rope, jnp.zeros((LANES - QK_DIM,), F32)]).reshape(1, LANES)


def _layer(x2, ada, positions, B, S, g_norm1, w_in, g_cq, w_uq, g_ckv, w_ukv, g_qn, g_kn, g_attn_out,
           conv_w, conv_b, w_mq, w_mk, b_igate, b_fgate, g_mlstm_out, w_out, g_norm2, w_router, router_bias,
           w_gate_exp, w_up_exp, w_down_exp, w_gate_sh, w_up_sh, w_down_sh):
    T = B * S
    sh1, sc1, gt1, sh2, sc2, gt2 = [a.reshape(B, 1, D_MODEL) for a in jnp.split(ada, 6, axis=-1)]
    row = lambda v: v.reshape(1, -1).astype(F32)

    inv_freq = ROPE_THETA ** (-jnp.arange(0, QK_ROPE, 2, dtype=F32) / QK_ROPE)
    ang = positions.astype(F32).reshape(T, 1) * inv_freq
    half = QK_ROPE // 2
    freq_lane = jnp.arange(half)[:, None]
    lane = jnp.arange(LANES)[None, :]
    place = ((lane == QK_NOPE + freq_lane) | (lane == QK_NOPE + half + freq_lane)).astype(F32)
    spread = lambda tbl: jnp.dot(tbl, place, precision=lax.Precision.HIGHEST)
    cosp = (jnp.arange(LANES) < QK_NOPE).astype(F32)[None, :] + spread(jnp.cos(ang))
    sinp = spread(jnp.sin(ang))

    o0 = 0
    w_cq = w_in[:, o0:o0 + Q_LORA]; o0 += Q_LORA
    w_ckv = w_in[:, o0:o0 + KV_LORA]; o0 += KV_LORA
    w_kr = w_in[:, o0:o0 + QK_ROPE]; o0 += QK_ROPE
    w_z = w_in[:, o0:o0 + MLSTM_INNER]; o0 += MLSTM_INNER
    w_v = w_in[:, o0:o0 + MLSTM_INNER]; o0 += MLSTM_INNER
    w_o = w_in[:, o0:o0 + MLSTM_INNER]; o0 += MLSTM_INNER
    w_i = w_in[:, o0:o0 + MLSTM_HEADS]; o0 += MLSTM_HEADS
    w_f = w_in[:, o0:o0 + MLSTM_HEADS]
    zl = jnp.zeros((D_MODEL, QK_NOPE), F32)
    zr = jnp.zeros((D_MODEL, LANES - QK_DIM), F32)
    wa = jnp.concatenate([w_cq, w_ckv, w_z, w_o, zl, w_kr, zr, zl, _rot_partner(w_kr), zr], axis=1).astype(BF16)
    wt = jnp.concatenate([w_v.T, w_i.T, w_f.T, jnp.zeros((16 - 2 * MLSTM_HEADS, D_MODEL), F32)], axis=0).astype(BF16)

    uq3 = w_uq.reshape(Q_LORA, MLA_HEADS, QK_DIM)
    uq_rot = jnp.concatenate([jnp.zeros((Q_LORA, MLA_HEADS, QK_NOPE), F32), _rot_partner(uq3[..., QK_NOPE:])], axis=-1)
    wuq = _pad_heads(w_uq, QK_DIM).astype(BF16)
    wuqs = _pad_heads(uq_rot.reshape(Q_LORA, MLA_HEADS * QK_DIM), QK_DIM).astype(BF16)
    ukv3 = w_ukv.reshape(KV_LORA, MLA_HEADS, QK_NOPE + V_DIM)
    wuk = _pad_heads(ukv3[..., :QK_NOPE].reshape(KV_LORA, MLA_HEADS * QK_NOPE), QK_NOPE).astype(BF16)
    wuv = _pad_heads(ukv3[..., QK_NOPE:].reshape(KV_LORA, MLA_HEADS * V_DIM), V_DIM).astype(BF16)
    vone = jnp.tile((jnp.arange(LANES) == V_DIM).astype(F32), MLA_HEADS).reshape(1, MLA_HEADS * LANES)
    swap = lambda g_rope: jnp.concatenate([g_rope[half:], g_rope[:half]])
    no_nope = jnp.zeros((QK_NOPE,), F32)
    gq = _lane_vec(g_qn[:QK_NOPE], g_qn[QK_NOPE:])
    gqs = _lane_vec(no_nope, swap(g_qn[QK_NOPE:]))
    gk = _lane_vec(g_kn[:QK_NOPE], g_kn[QK_NOPE:])
    gks = _lane_vec(no_nope, swap(g_kn[QK_NOPE:]))

    z, og, vt3, g3, q, k, mv = _inproj(
        x2, 1.0 + sc1, sh1, row(g_norm1), wa, wt, row(g_cq), wuq, wuqs, row(g_ckv), wuk, wuv, vone,
        gq, gqs, gk, gks, cosp, sinp, S)
    attn_o = _attn(q, k, mv, B, S)
    mlstm_o = _mlstm(z, vt3, g3, og, conv_w.reshape(CONV_WIDTH, MLSTM_INNER), row(conv_b),
                     w_mq.astype(BF16), w_mk.astype(BF16), b_igate, b_fgate, row(g_mlstm_out), B, S)

    wgu = jnp.concatenate([w_gate_sh, w_up_sh], axis=1).astype(BF16)
    base, h2, lgt = _post(attn_o, mlstm_o, x2, gt1, row(g_attn_out), w_out.astype(BF16), row(g_norm2),
                          1.0 + sc2, sh2, gt2, w_router.T.astype(BF16), wgu, w_down_sh.astype(BF16), S)

    idx_t, w_t, rank_t, cnt = _route(lgt, router_bias.reshape(N_EXPERTS, 1))
    counts = cnt[:, 0].astype(I32)
    padded = (counts + ROW_BLOCK - 1) // ROW_BLOCK * ROW_BLOCK
    padded_end = jnp.cumsum(padded)
    off = padded_end - padded
    n_rows = T * TOP_K + N_EXPERTS * ROW_BLOCK
    dest_t = _dest(idx_t, rank_t, off.reshape(N_EXPERTS, 1))

    dest2_t = jnp.concatenate([dest_t, dest_t + n_rows], axis=1)
    xs = _scatter_rows(dest2_t, h2.reshape(2 * T, HALF), 2 * n_rows).reshape(2, n_rows, HALF)
    y = _experts(off // ROW_BLOCK, padded // ROW_BLOCK, counts, xs, w_gate_exp, w_up_exp, w_down_exp)
    dest_row = dest_t.reshape(1, TOP_K * T)
    y_slots = _gather_rows(jnp.concatenate([dest_row, dest_row + n_rows], axis=1), y.reshape(2 * n_rows, HALF))
    return _combine(y_slots, base, w_t.T, gt2, S)


def kernel(x, c, positions, w_ada, b_ada, g_norm1, w_in, g_cq, w_uq, g_ckv, w_ukv, g_qn, g_kn, g_attn_out, conv_w, conv_b, w_mq, w_mk, b_igate, b_fgate, g_mlstm_out, w_out, g_norm2, w_router, router_bias, w_gate_exp, w_up_exp, w_down_exp, w_gate_sh, w_up_sh, w_down_sh):
    B, S, d_model = x.shape
    assert d_model == D_MODEL and S % WIDE_TILE == 0 and S % SEQ_TILE == 0, (B, S, d_model)
    assert w_gate_exp.shape[1:] == (N_EXPERTS, D_MODEL, EXPERT_DIM), w_gate_exp.shape
    depth = w_ada.shape[0]
    x2 = x.reshape(B * S, D_MODEL)
    for l in range(depth):
        ada = _ada(c, w_ada[l], b_ada[l])
        x2 = _layer(x2, ada, positions, B, S, g_norm1[l], w_in[l], g_cq[l], w_uq[l], g_ckv[l], w_ukv[l], g_qn[l],
                    g_kn[l], g_attn_out[l], conv_w[l], conv_b[l], w_mq[l], w_mk[l], b_igate[l], b_fgate[l],
                    g_mlstm_out[l], w_out[l], g_norm2[l], w_router[l], router_bias[l], w_gate_exp[l], w_up_exp[l],
                    w_down_exp[l], w_gate_sh[l], w_up_sh[l], w_down_sh[l])
    return x2.reshape(B, S, D_MODEL)
```

```python
import functools
import math

import jax
import jax.numpy as jnp
from jax import lax
from jax.experimental import pallas as pl
from jax.experimental.pallas import tpu as pltpu
from jax.experimental.pallas import tpu_sc as plsc

F32 = jnp.float32
BF16 = jnp.bfloat16
I32 = jnp.int32

D_MODEL = 1024
MLA_HEADS = 8
QK_NOPE = 64
QK_ROPE = 32
QK_DIM = QK_NOPE + QK_ROPE
V_DIM = 64
Q_LORA = 384
KV_LORA = 256
ROPE_THETA = 10000.0
MLSTM_HEADS = 4
MLSTM_DIM = 128
MLSTM_INNER = MLSTM_HEADS * MLSTM_DIM
CONV_WIDTH = 4
N_EXPERTS = 256
TOP_K = 8
N_GROUPS = 8
TOPK_GROUPS = 4
GROUP_SIZE = N_EXPERTS // N_GROUPS
EXPERT_DIM = 256
SHARED_DIM = 256
ROUTED_SCALE = 2.5
RMS_EPS = 1e-6

LANES = 128
SEQ_TILE = 256
WIDE_TILE = 512
ROW_BLOCK = 256
VMEM_LIMIT = 56 * 1024 * 1024

_OFF_CQ = 0
_OFF_CKV = _OFF_CQ + Q_LORA
_OFF_Z = _OFF_CKV + KV_LORA
_OFF_O = _OFF_Z + MLSTM_INNER
_OFF_KR = _OFF_O + MLSTM_INNER
_OFF_KRS = _OFF_KR + LANES
_WA_COLS = _OFF_KRS + LANES
_WT_ROWS = MLSTM_INNER + 16


def _params(n_axes, vmem=VMEM_LIMIT):
    return pltpu.CompilerParams(dimension_semantics=("arbitrary",) * n_axes, vmem_limit_bytes=vmem)


def _dot(a, b):
    return jnp.dot(a, b, preferred_element_type=F32)


def _dot_nt(a, b):
    return lax.dot_general(a, b, (((1,), (1,)), ((), ())), preferred_element_type=F32)


def _sigmoid(x):
    return 1.0 / (1.0 + jnp.exp(-x))


def _silu(x):
    return x * _sigmoid(x)


U32 = jnp.uint32
PACKED = D_MODEL // 2


def _pack_rows(x):
    w = x.shape[1] // 2
    lo = lax.bitcast_convert_type(x[:, :w].astype(BF16).astype(F32), U32)
    hi = lax.bitcast_convert_type(x[:, w:].astype(BF16).astype(F32), U32)
    return lax.shift_right_logical(lo, U32(16)) | (hi & U32(0xFFFF0000))


def _unpack_rows(p):
    lo = lax.bitcast_convert_type(lax.shift_left(p, U32(16)), F32)
    hi = lax.bitcast_convert_type(p & U32(0xFFFF0000), F32)
    return lo, hi


def _ada_body(c_ref, w_ref, b_ref, o_ref):
    c = c_ref[...]
    cond = _silu(c).astype(BF16)
    o_ref[...] = _dot(cond, w_ref[...].astype(BF16)) + b_ref[...]


def _ada(c, w_ada, b_ada):
    B = c.shape[0]
    n = w_ada.shape[1]
    tn = 1024
    return pl.pallas_call(
        _ada_body,
        out_shape=jax.ShapeDtypeStruct((B, n), F32),
        grid=(n // tn,),
        in_specs=[
            pl.BlockSpec((B, D_MODEL), lambda j: (0, 0)),
            pl.BlockSpec((D_MODEL, tn), lambda j: (0, j)),
            pl.BlockSpec((1, tn), lambda j: (0, j)),
        ],
        out_specs=pl.BlockSpec((B, tn), lambda j: (0, j)),
        compiler_params=_params(1),
        name="ada",
    )(c, w_ada, b_ada.reshape(1, n))


def _inproj_body(x_ref, sc_ref, sh_ref, g1_ref, wa_ref, wt_ref, gcq_ref, wuq_ref, wuqs_ref, gckv_ref,
                 wuk_ref, wuv_ref, vone_ref, gq_ref, gqs_ref, gk_ref, gks_ref, cos_ref, sin_ref,
                 z_ref, o_ref, vt_ref, gt_ref, q_ref, k_ref, mv_ref):
    x = x_ref[...]
    hn = x * lax.rsqrt(jnp.mean(x * x, axis=-1, keepdims=True) + RMS_EPS) * g1_ref[...]
    hm = (hn * sc_ref[...] + sh_ref[...]).astype(BF16)
    p = _dot(hm, wa_ref[...])
    z_ref[...] = p[:, _OFF_Z:_OFF_Z + MLSTM_INNER].astype(BF16)
    o_ref[...] = p[:, _OFF_O:_OFF_O + MLSTM_INNER].astype(BF16)
    rt = _dot_nt(wt_ref[...], hm)
    vt_ref[...] = rt[:MLSTM_INNER].astype(BF16)
    gt_ref[...] = rt[MLSTM_INNER:]

    cq = p[:, _OFF_CQ:_OFF_CQ + Q_LORA]
    cqn = (cq * lax.rsqrt(jnp.mean(cq * cq, axis=-1, keepdims=True) + RMS_EPS) * gcq_ref[...]).astype(BF16)
    ckv = p[:, _OFF_CKV:_OFF_CKV + KV_LORA]
    ckvn = (ckv * lax.rsqrt(jnp.mean(ckv * ckv, axis=-1, keepdims=True) + RMS_EPS) * gckv_ref[...]).astype(BF16)
    qa = _dot(cqn, wuq_ref[...])
    qb = _dot(cqn, wuqs_ref[...])
    ka = _dot(ckvn, wuk_ref[...])
    mv_ref[...] = (_dot(ckvn, wuv_ref[...]) + vone_ref[...]).astype(BF16)
    kr = p[:, _OFF_KR:_OFF_KR + LANES]
    krs = p[:, _OFF_KRS:_OFF_KRS + LANES]

    cosp = cos_ref[...]
    sinp = sin_ref[...]
    gqc = gq_ref[...] * cosp
    gqs = gqs_ref[...] * sinp
    gkc = gk_ref[...] * cosp
    k_rot = krs * (gks_ref[...] * sinp)
    kr_ss = jnp.sum(kr * kr, axis=-1, keepdims=True)
    q_scale = (QK_DIM ** -0.5) * math.log2(math.e)
    for h in range(MLA_HEADS):
        sl = slice(h * LANES, (h + 1) * LANES)
        qah = qa[:, sl]
        rq = lax.rsqrt(jnp.sum(qah * qah, axis=-1, keepdims=True) * (1.0 / QK_DIM) + RMS_EPS) * q_scale
        q_ref[:, sl] = ((qah * gqc + qb[:, sl] * gqs) * rq).astype(BF16)
        kah = ka[:, sl]
        rk = lax.rsqrt((jnp.sum(kah * kah, axis=-1, keepdims=True) + kr_ss) * (1.0 / QK_DIM) + RMS_EPS)
        k_ref[:, sl] = (((kah + kr) * gkc + k_rot) * rk).astype(BF16)


def _inproj(x2, sc1p, sh1, g1, wa, wt, gcq, wuq, wuqs, gckv, wuk, wuv, vone, gq, gqs, gk, gks, cosp, sinp, S):
    T = x2.shape[0]
    tm = SEQ_TILE
    nt = T // tm
    per_b = S // tm
    hp = MLA_HEADS * LANES
    full = lambda a: pl.BlockSpec(a.shape, lambda i: (0,) * a.ndim)
    row = lambda w: pl.BlockSpec((tm, w), lambda i: (i, 0))
    bmod = pl.BlockSpec((None, 1, D_MODEL), lambda i: (i // per_b, 0, 0))
    out_shapes = (
        jax.ShapeDtypeStruct((T, MLSTM_INNER), BF16),
        jax.ShapeDtypeStruct((T, MLSTM_INNER), BF16),
        jax.ShapeDtypeStruct((nt, MLSTM_INNER, tm), BF16),
        jax.ShapeDtypeStruct((nt, 16, tm), F32),
        jax.ShapeDtypeStruct((T, hp), BF16),
        jax.ShapeDtypeStruct((T, hp), BF16),
        jax.ShapeDtypeStruct((T, hp), BF16),
    )
    out_specs = (
        row(MLSTM_INNER), row(MLSTM_INNER),
        pl.BlockSpec((None, MLSTM_INNER, tm), lambda i: (i, 0, 0)),
        pl.BlockSpec((None, 16, tm), lambda i: (i, 0, 0)),
        row(hp), row(hp), row(hp),
    )
    in_specs = [row(D_MODEL), bmod, bmod, full(g1), full(wa), full(wt), full(gcq), full(wuq), full(wuqs),
                full(gckv), full(wuk), full(wuv), full(vone), full(gq), full(gqs), full(gk), full(gks),
                row(LANES), row(LANES)]
    return pl.pallas_call(
        _inproj_body, out_shape=out_shapes, grid=(nt,), in_specs=in_specs, out_specs=out_specs,
        compiler_params=_params(1), name="inproj",
    )(x2, sc1p, sh1, g1, wa, wt, gcq, wuq, wuqs, gckv, wuk, wuv, vone, gq, gqs, gk, gks, cosp, sinp)


ATTN_HEADS_PER_STEP = 8


def _pairs_loop(n, fn, init):
    c = lax.fori_loop(0, n // 2, lambda jj, c: fn(2 * jj + 1, fn(2 * jj, c)), init)
    return lax.fori_loop(0, n % 2, lambda _, c: fn(n - 1, c), c)


def _attn_body(q_ref, k_ref, v_ref, o_ref, s_scr):
    i = pl.program_id(2)
    t = SEQ_TILE
    nh = ATTN_HEADS_PER_STEP
    qry_pos = lax.broadcasted_iota(I32, (t, t), 0)
    key_pos = lax.broadcasted_iota(I32, (t, t), 1)
    causal = key_pos <= qry_pos
    qs = [q_ref[:, hh * LANES:(hh + 1) * LANES] for hh in range(nh)]

    def scores(j, hh, diag):
        kj = k_ref[pl.ds(pl.multiple_of(j * t, t), t), hh * LANES:(hh + 1) * LANES]
        s = _dot_nt(qs[hh], kj)
        return jnp.where(causal, s, -jnp.inf) if diag else s

    def halves(x):
        return x[:, :LANES], x[:, LANES:]

    def max_pass(j, mx, diag):
        out = []
        for hh in range(nh):
            s = scores(j, hh, diag)
            s_scr[j, hh] = s
            lo, hi = halves(s)
            out.append(jnp.maximum(mx[hh], jnp.maximum(lo, hi)))
        return tuple(out)

    mx = tuple(jnp.full((t, LANES), -jnp.inf, F32) for _ in range(nh))
    mx = _pairs_loop(i, lambda j, c: max_pass(j, c, False), mx)
    mx = max_pass(i, mx, True)
    mb = [jnp.broadcast_to(jnp.max(mx[hh], axis=1, keepdims=True), (t, LANES)) for hh in range(nh)]

    def acc_pass(j, carry, diag):
        row0 = pl.multiple_of(j * t, t)
        out = []
        for hh in range(nh):
            lo, hi = halves(s_scr[j, hh])
            p = jnp.concatenate([jnp.exp2(lo - mb[hh]), jnp.exp2(hi - mb[hh])], axis=1).astype(BF16)
            out.append(carry[hh] + _dot(p, v_ref[pl.ds(row0, t), hh * LANES:(hh + 1) * LANES]))
        return tuple(out)

    init = tuple(jnp.zeros((t, LANES), F32) for _ in range(nh))
    carry = _pairs_loop(i, lambda j, c: acc_pass(j, c, False), init)
    final = acc_pass(i, carry, True)
    outs = [acc / acc[:, V_DIM:V_DIM + 1] for acc in final]
    first_head = lax.broadcasted_iota(I32, (t, LANES), 1) < V_DIM
    o_ref[...] = jnp.concatenate(
        [jnp.where(first_head, outs[2 * pr], pltpu.roll(outs[2 * pr + 1], V_DIM, axis=1)) for pr in range(nh // 2)],
        axis=1).astype(BF16)


def _attn(q, k, mvt, B, S):
    T = q.shape[0]
    t = SEQ_TILE
    nh = ATTN_HEADS_PER_STEP
    per_b = S // t
    return pl.pallas_call(
        _attn_body,
        out_shape=jax.ShapeDtypeStruct((T, MLA_HEADS * V_DIM), BF16),
        grid=(B, MLA_HEADS // nh, per_b),
        in_specs=[
            pl.BlockSpec((t, nh * LANES), lambda b, hp, i: (b * per_b + i, hp)),
            pl.BlockSpec((S, nh * LANES), lambda b, hp, i: (b, hp)),
            pl.BlockSpec((S, nh * LANES), lambda b, hp, i: (b, hp)),
        ],
        out_specs=pl.BlockSpec((t, nh * V_DIM), lambda b, hp, i: (b * per_b + i, hp)),
        scratch_shapes=[pltpu.VMEM((per_b, nh, t, t), F32)],
        compiler_params=_params(3), name="attn",
    )(q, k, mvt)


def _split3(a):
    hi = a.astype(BF16)
    r = a - hi.astype(F32)
    mid = r.astype(BF16)
    lo = (r - mid.astype(F32)).astype(BF16)
    return hi, mid, lo


def _mlstm_body(bi_ref, bf_ref, z_ref, vt_ref, g_ref, og_ref, cw_ref, cb_ref, wq_ref, wk_ref, gout_ref,
                out_ref, zpad, q_s, k_s, st_s, m_s, *, S):
    h = pl.program_id(1)
    L = SEQ_TILE
    nchunk = S // L
    hd = MLSTM_DIM
    zpad[0:8, :] = jnp.zeros((8, hd), F32)
    zpad[8:, :] = z_ref[...].astype(F32)
    zc = cb_ref[...] + jnp.zeros((S, hd), F32)
    for j in range(CONV_WIDTH):
        zc = zc + cw_ref[j:j + 1, :] * zpad[pl.ds(8 - (CONV_WIDTH - 1) + j, S), :]
    zs = _silu(zc).astype(BF16)
    q_s[...] = _dot(zs, wq_ref[...]).astype(BF16)
    k_s[...] = (_dot(zs, wk_ref[...]) * (hd ** -0.5)).astype(BF16)

    st_s[...] = jnp.zeros_like(st_s)
    m_s[...] = jnp.zeros_like(m_s)
    b_i = bi_ref[h]
    b_f = bf_ref[h]
    r_i = lax.broadcasted_iota(I32, (L, L), 0)
    c_i = lax.broadcasted_iota(I32, (L, L), 1)
    tril = r_i >= c_i
    strict_lower = jnp.where(r_i > c_i, 1.0, 0.0).astype(BF16)
    ones_row = jnp.where(lax.broadcasted_iota(I32, (hd, L), 0) == 0, 1.0, 0.0).astype(BF16)

    def chunk(c, _):
        start = pl.multiple_of(c * L, L)
        qc = q_s[pl.ds(start, L), :]
        kc = k_s[pl.ds(start, L), :]
        vt_aug = jnp.concatenate([vt_ref[c], ones_row], axis=0)
        li = g_ref[c, pl.ds(h, 1), :] + b_i
        fp = g_ref[c, pl.ds(MLSTM_HEADS + h, 1), :] + b_f
        lf = jnp.minimum(fp, 0.0) - jnp.log(1.0 + jnp.exp(-jnp.abs(fp)))
        a_mat = jnp.where(tril, lf, 0.0)
        a_hi, a_mid, a_lo = _split3(a_mat)
        d0 = _dot(a_hi, strict_lower) + _dot(a_mid, strict_lower) + _dot(a_lo, strict_lower)
        dmat = jnp.where(tril, d0 + li, -jnp.inf)
        b_col = jnp.sum(a_mat, axis=1, keepdims=True)
        m_prev = m_s[...]
        inter = b_col + m_prev
        mt = jnp.maximum(inter, jnp.max(dmat, axis=1, keepdims=True))
        w = jnp.exp(dmat - mt) * _dot_nt(qc, kc)
        decay = jnp.exp(inter - mt)
        st = st_s[...]
        num_aug = decay * _dot_nt(qc, st.astype(BF16)) + _dot_nt(w.astype(BF16), vt_aug)
        num = num_aug[:, :hd]
        den = num_aug[:, hd:hd + 1]
        hv = num / jnp.maximum(jnp.abs(den), jnp.exp(-mt))
        g = d0[L - 1:L, :] + li
        b_last = b_col[L - 1:L, :]
        m_new = jnp.maximum(b_last + m_prev, jnp.max(g, axis=1, keepdims=True))
        a = jnp.exp(g - m_new)
        cd = jnp.exp(b_last + m_prev - m_new)
        st_s[...] = cd * st + _dot((vt_aug.astype(F32) * a).astype(BF16), kc)
        m_s[...] = m_new
        hn = hv * lax.rsqrt(jnp.mean(hv * hv, axis=-1, keepdims=True) + RMS_EPS) * gout_ref[...]
        og = og_ref[pl.ds(start, L), :].astype(F32)
        out_ref[pl.ds(start, L), :] = (_sigmoid(og) * hn).astype(BF16)
        return 0

    lax.fori_loop(0, nchunk, chunk, 0)


def _mlstm(z, vt3, g3, og, conv_w, conv_b, wq, wk, b_i, b_f, g_out, B, S):
    T = z.shape[0]
    L = SEQ_TILE
    per_b = S // L
    hd = MLSTM_DIM
    smem = pl.BlockSpec(memory_space=pltpu.SMEM)
    seq = pl.BlockSpec((S, hd), lambda b, h: (b, h))
    return pl.pallas_call(
        functools.partial(_mlstm_body, S=S),
        out_shape=jax.ShapeDtypeStruct((T, MLSTM_INNER), BF16),
        grid=(B, MLSTM_HEADS),
        in_specs=[
            smem, smem, seq,
            pl.BlockSpec((per_b, hd, L), lambda b, h: (b, h, 0)),
            pl.BlockSpec((per_b, 16, L), lambda b, h: (b, 0, 0)),
            seq,
            pl.BlockSpec((CONV_WIDTH, hd), lambda b, h: (0, h)),
            pl.BlockSpec((1, hd), lambda b, h: (0, h)),
            pl.BlockSpec((None, hd, hd), lambda b, h: (h, 0, 0)),
            pl.BlockSpec((None, hd, hd), lambda b, h: (h, 0, 0)),
            pl.BlockSpec((1, hd), lambda b, h: (0, h)),
        ],
        out_specs=seq,
        scratch_shapes=[
            pltpu.VMEM((S + 8, hd), F32),
            pltpu.VMEM((S, hd), BF16),
            pltpu.VMEM((S, hd), BF16),
            pltpu.VMEM((2 * hd, hd), F32),
            pltpu.VMEM((1, 1), F32),
        ],
        compiler_params=_params(2), name="mlstm",
    )(b_i, b_f, z, vt3, g3, og, conv_w, conv_b, wq, wk, g_out)


def _post_body(ao_ref, mo_ref, x_ref, gt1_ref, gattn_ref, wout_ref, g2_ref, sc2_ref, sh2_ref, gt2_ref,
               wrt_ref, wgu_ref, wds_ref, base_ref, h2_ref, lg_ref):
    ao = ao_ref[...].astype(F32)
    aon = ao * lax.rsqrt(jnp.mean(ao * ao, axis=-1, keepdims=True) + RMS_EPS) * gattn_ref[...]
    mix_in = jnp.concatenate([aon.astype(BF16), mo_ref[...]], axis=1)
    x1 = x_ref[...] + gt1_ref[...] * _dot(mix_in, wout_ref[...])
    hn = x1 * lax.rsqrt(jnp.mean(x1 * x1, axis=-1, keepdims=True) + RMS_EPS) * g2_ref[...]
    h2 = hn * sc2_ref[...] + sh2_ref[...]
    words = _pack_rows(h2)
    h2_ref[0] = words[:, :HALF]
    h2_ref[1] = words[:, HALF:]
    h2b = h2.astype(BF16)
    gu = _dot(h2b, wgu_ref[...])
    act = (_silu(gu[:, :SHARED_DIM]) * gu[:, SHARED_DIM:]).astype(BF16)
    base_ref[...] = x1 + gt2_ref[...] * _dot(act, wds_ref[...])
    lg_ref[...] = _dot_nt(wrt_ref[...], h2b)


def _post(ao, mo, x2, gt1, gattn, wout, g2, sc2p, sh2, gt2, wrt, wgu, wds, S):
    T = x2.shape[0]
    tm = WIDE_TILE
    per_b = S // tm
    full = lambda a: pl.BlockSpec(a.shape, lambda i: (0,) * a.ndim)
    row = lambda w: pl.BlockSpec((tm, w), lambda i: (i, 0))
    bmod = pl.BlockSpec((None, 1, D_MODEL), lambda i: (i // per_b, 0, 0))
    return pl.pallas_call(
        _post_body,
        out_shape=(jax.ShapeDtypeStruct((T, D_MODEL), F32), jax.ShapeDtypeStruct((2, T, HALF), U32),
                   jax.ShapeDtypeStruct((N_EXPERTS, T), F32)),
        grid=(T // tm,),
        in_specs=[row(MLA_HEADS * V_DIM), row(MLSTM_INNER), row(D_MODEL), bmod, full(gattn), full(wout),
                  full(g2), bmod, bmod, bmod, full(wrt), full(wgu), full(wds)],
        out_specs=(row(D_MODEL), pl.BlockSpec((2, tm, HALF), lambda i: (0, i, 0)),
                   pl.BlockSpec((N_EXPERTS, tm), lambda i: (0, i))),
        compiler_params=_params(1), name="post",
    )(ao, mo, x2, gt1, gattn, wout, g2, sc2p, sh2, gt2, wrt, wgu, wds)


def _first_max(x, rows, n):
    mx = jnp.max(x, axis=0, keepdims=True)
    idx = jnp.min(jnp.where(x == mx, rows, n), axis=0, keepdims=True)
    return mx, idx


def _route_body(lg_ref, bias_ref, idx_ref, w_ref, rank_ref, cnt_ref, carry):
    i = pl.program_id(0)
    tt = lg_ref.shape[1]

    @pl.when(i == 0)
    def _():
        carry[...] = jnp.zeros_like(carry)

    sc = _sigmoid(lg_ref[...])
    bi = sc + bias_ref[...]
    rows = lax.broadcasted_iota(I32, (N_EXPERTS, tt), 0)
    rows_g = lax.broadcasted_iota(I32, (GROUP_SIZE, tt), 0)
    rows_n = lax.broadcasted_iota(I32, (N_GROUPS, tt), 0)
    neg = -jnp.inf
    gs = []
    for g in range(N_GROUPS):
        xg = bi[g * GROUP_SIZE:(g + 1) * GROUP_SIZE]
        m1, i1 = _first_max(xg, rows_g, GROUP_SIZE)
        m2 = jnp.max(jnp.where(rows_g == i1, neg, xg), axis=0, keepdims=True)
        gs.append(m1 + m2)
    cur = jnp.concatenate(gs, axis=0)
    keep = jnp.zeros((N_GROUPS, tt), F32)
    for _ in range(TOPK_GROUPS):
        _, gi = _first_max(cur, rows_n, N_GROUPS)
        sel = rows_n == gi
        keep = jnp.where(sel, 1.0, keep)
        cur = jnp.where(sel, neg, cur)
    masked = jnp.concatenate(
        [jnp.where(keep[g:g + 1] > 0.0, bi[g * GROUP_SIZE:(g + 1) * GROUP_SIZE], neg) for g in range(N_GROUPS)], axis=0)
    cur = masked
    onehot = jnp.zeros((N_EXPERTS, tt), F32)
    idxs, ws = [], []
    for _ in range(TOP_K):
        _, ei = _first_max(cur, rows, N_EXPERTS)
        sel = rows == ei
        ws.append(jnp.sum(jnp.where(sel, sc, 0.0), axis=0, keepdims=True))
        idxs.append(ei)
        cur = jnp.where(sel, neg, cur)
        onehot = jnp.where(sel, 1.0, onehot)
    wsum = ws[0]
    for k in range(1, TOP_K):
        wsum = wsum + ws[k]
    idx_ref[...] = jnp.concatenate(idxs, axis=0)
    w_ref[...] = jnp.concatenate([w / wsum * ROUTED_SCALE for w in ws], axis=0)
    r_i = lax.broadcasted_iota(I32, (tt, tt), 0)
    c_i = lax.broadcasted_iota(I32, (tt, tt), 1)
    before = jnp.where(r_i < c_i, 1.0, 0.0).astype(BF16)
    tot = _dot(onehot.astype(BF16), before) + carry[...]
    rank_ref[...] = jnp.concatenate(
        [jnp.sum(jnp.where(rows == ei, tot, 0.0), axis=0, keepdims=True) for ei in idxs], axis=0).astype(I32)
    carry[...] = carry[...] + jnp.sum(onehot, axis=1, keepdims=True)
    cnt_ref[...] = jnp.broadcast_to(carry[...], cnt_ref.shape)


def _route(lgt, bias_col):
    T = lgt.shape[1]
    tt = WIDE_TILE
    blk = pl.BlockSpec((TOP_K, tt), lambda i: (0, i))
    return pl.pallas_call(
        _route_body,
        out_shape=(jax.ShapeDtypeStruct((TOP_K, T), I32), jax.ShapeDtypeStruct((TOP_K, T), F32),
                   jax.ShapeDtypeStruct((TOP_K, T), I32), jax.ShapeDtypeStruct((N_EXPERTS, LANES), F32)),
        grid=(T // tt,),
        in_specs=[pl.BlockSpec((N_EXPERTS, tt), lambda i: (0, i)), pl.BlockSpec((N_EXPERTS, 1), lambda i: (0, 0))],
        out_specs=(blk, blk, blk, pl.BlockSpec((N_EXPERTS, LANES), lambda i: (0, 0))),
        scratch_shapes=[pltpu.VMEM((N_EXPERTS, 1), F32)],
        compiler_params=_params(1), name="route",
    )(lgt, bias_col)


def _dest_body(idx_ref, rank_ref, off_ref, dest_ref):
    tt = idx_ref.shape[1]
    rows = lax.broadcasted_iota(I32, (N_EXPERTS, tt), 0)
    off = off_ref[...]
    outs = []
    for k in range(TOP_K):
        sel = rows == idx_ref[k:k + 1, :]
        outs.append(jnp.sum(jnp.where(sel, off, 0), axis=0, keepdims=True) + rank_ref[k:k + 1, :])
    dest_ref[...] = jnp.concatenate(outs, axis=0)


def _dest(idx_t, rank_t, off_col):
    T = idx_t.shape[1]
    tt = min(4 * SEQ_TILE, T)
    blk = pl.BlockSpec((TOP_K, tt), lambda i: (0, i))
    return pl.pallas_call(
        _dest_body, out_shape=jax.ShapeDtypeStruct((TOP_K, T), I32), grid=(T // tt,),
        in_specs=[blk, blk, pl.BlockSpec((N_EXPERTS, 1), lambda i: (0, 0))], out_specs=blk,
        compiler_params=_params(1), name="dest",
    )(idx_t, rank_t, off_col)


SC_WINDOW = 128
HALF = PACKED // 2


def _sc_mesh():
    return plsc.VectorSubcoreMesh(core_axis_name="c", subcore_axis_name="s")


def _scatter_rows(dest2_t, rows, n_out):
    n = rows.shape[0]

    @pl.kernel(out_type=jax.ShapeDtypeStruct((n_out, HALF), U32), mesh=_sc_mesh(), scratch_types=[])
    def scatter(d_hbm, x_hbm, o_hbm):
        def body(d_vmem, x_vmem):
            for k in range(TOP_K):
                pltpu.sync_copy(x_vmem, o_hbm.at[d_vmem.at[k]])

        pltpu.emit_pipeline(
            body, grid=(n // SC_WINDOW,),
            in_specs=[pl.BlockSpec((TOP_K, SC_WINDOW), lambda i: (0, i)),
                      pl.BlockSpec((SC_WINDOW, HALF), lambda i: (i, 0))],
            out_specs=[], core_axis_name=("c", "s"), dimension_semantics=(pltpu.PARALLEL,),
        )(d_hbm, x_hbm)

    return scatter(dest2_t, rows)


def _gather_rows(index_row, rows):
    n = index_row.shape[1]

    @pl.kernel(out_type=jax.ShapeDtypeStruct((n, HALF), U32), mesh=_sc_mesh(), scratch_types=[])
    def gather(i_hbm, y_hbm, o_hbm):
        def body(i_vmem, o_vmem):
            pltpu.sync_copy(y_hbm.at[i_vmem.at[0]], o_vmem)

        pltpu.emit_pipeline(
            body, grid=(n // SC_WINDOW,),
            in_specs=[pl.BlockSpec((1, SC_WINDOW), lambda i: (0, i))],
            out_specs=[pl.BlockSpec((SC_WINDOW, HALF), lambda i: (i, 0))],
            core_axis_name=("c", "s"), dimension_semantics=(pltpu.PARALLEL,),
        )(i_hbm, o_hbm)

    return gather(index_row, rows)


EXPERT_RING = 16
WEIGHT_RING = 3


def _experts_body(first_ref, nblk_ref, cnt_ref, xs_ref, wg_ref, wu_ref, wd_ref, y_ref,
                  xbuf, ybuf, wgf, wuf, wdf, wgb, wub, wdb, xsem, ysem, wsem):
    e = pl.program_id(0)
    last = pl.num_programs(0) - 1
    wring = wgf.shape[0]

    def w_copies(ex):
        slot = ex % wring
        return [pltpu.make_async_copy(src.at[ex], dst.at[slot], wsem.at[slot])
                for src, dst in ((wg_ref, wgf), (wu_ref, wuf), (wd_ref, wdf))]

    @pl.when(e == 0)
    def _():
        for ex in range(wring - 1):
            for cp in w_copies(ex):
                cp.start()

    b0 = first_ref[e]
    nb = nblk_ref[e]
    total = first_ref[last] + nblk_ref[last]
    ring = xbuf.shape[0]

    def rows(g):
        return pl.ds(pl.multiple_of(g * ROW_BLOCK, ROW_BLOCK), ROW_BLOCK)

    def x_copy(g):
        slot = g % ring
        return pltpu.make_async_copy(xs_ref.at[:, rows(g), :], xbuf.at[slot], xsem.at[slot])

    def y_copy(g):
        slot = g % ring
        return pltpu.make_async_copy(ybuf.at[slot], y_ref.at[:, rows(g), :], ysem.at[slot])

    @pl.when(e == 0)
    def _():
        for g in range(ring - 1):
            @pl.when(g < total)
            def _():
                x_copy(g).start()

    for cp in w_copies(e):
        cp.wait()

    @pl.when(nb > 0)
    def _():
        slot = e % wring
        wgb[...] = wgf[slot].astype(BF16)
        wub[...] = wuf[slot].astype(BF16)
        wdb[...] = wdf[slot].astype(BF16)

    @pl.when(e + wring - 1 <= last)
    def _():
        for cp in w_copies(e + wring - 1):
            cp.start()

    def prefetch(g):
        @pl.when(g + ring - 1 < total)
        def _():
            x_copy(g + ring - 1).start()

    def compute(i):
        slot = (b0 + i) % ring
        words = jnp.concatenate([xbuf[slot, 0], xbuf[slot, 1]], axis=1)
        n_live = cnt_ref[e] - i * ROW_BLOCK
        words = jnp.where(lax.broadcasted_iota(I32, words.shape, 0) < n_live, words, U32(0))
        x = jnp.concatenate(_unpack_rows(words), axis=1).astype(BF16)
        act = (_silu(_dot(x, wgb[...])) * _dot(x, wub[...])).astype(BF16)
        y = _pack_rows(_dot(act, wdb[...]))
        ybuf[slot, 0] = y[:, :HALF]
        ybuf[slot, 1] = y[:, HALF:]

    def blocks(i, n):
        g0 = b0 + i
        prefetch(g0)
        for d in range(n):
            x_copy(g0 + d).wait()

            @pl.when(g0 + d >= ring)
            def _():
                y_copy(g0 + d - ring).wait()

        for d in range(n):
            compute(i + d)
        for d in range(1, n):
            prefetch(g0 + d)
        for d in range(n):
            y_copy(g0 + d).start()

    def pair(ii, carry):
        blocks(2 * ii, 2)
        return carry

    def single(_, carry):
        blocks(nb - 1, 1)
        return carry

    lax.fori_loop(0, nb // 2, pair, 0)
    lax.fori_loop(0, nb % 2, single, 0)

    @pl.when(e == last)
    def _():
        for back in range(ring, 0, -1):
            @pl.when(total >= back)
            def _():
                y_copy(total - back).wait()


def _experts(first_block, n_blocks, counts, xs, wg, wu, wd):
    n_rows = xs.shape[1]
    anyspec = pl.BlockSpec(memory_space=pl.ANY)
    ring = pltpu.VMEM((EXPERT_RING, 2, ROW_BLOCK, HALF), U32)
    grid_spec = pltpu.PrefetchScalarGridSpec(
        num_scalar_prefetch=3, grid=(N_EXPERTS,),
        in_specs=[anyspec] * 4,
        out_specs=anyspec,
        scratch_shapes=[ring, ring,
                        pltpu.VMEM((WEIGHT_RING, D_MODEL, EXPERT_DIM), F32),
                        pltpu.VMEM((WEIGHT_RING, D_MODEL, EXPERT_DIM), F32),
                        pltpu.VMEM((WEIGHT_RING, EXPERT_DIM, D_MODEL), F32),
                        pltpu.VMEM((D_MODEL, EXPERT_DIM), BF16), pltpu.VMEM((D_MODEL, EXPERT_DIM), BF16),
                        pltpu.VMEM((EXPERT_DIM, D_MODEL), BF16),
                        pltpu.SemaphoreType.DMA((EXPERT_RING,)), pltpu.SemaphoreType.DMA((EXPERT_RING,)),
                        pltpu.SemaphoreType.DMA((WEIGHT_RING,))],
    )
    return pl.pallas_call(
        _experts_body, out_shape=jax.ShapeDtypeStruct((2, n_rows, HALF), U32), grid_spec=grid_spec,
        compiler_params=_params(1), name="experts",
    )(first_block, n_blocks, counts, xs, wg, wu, wd)


def _combine_body(*refs):
    ya_refs = refs[:TOP_K]
    yb_refs = refs[TOP_K:2 * TOP_K]
    base_ref, w_ref, gt2_ref, out_ref = refs[2 * TOP_K:]
    w = w_ref[...]
    r_lo = r_hi = None
    for k in range(TOP_K):
        lo, hi = _unpack_rows(jnp.concatenate([ya_refs[k][...], yb_refs[k][...]], axis=1))
        wk = w[:, k:k + 1]
        r_lo = wk * lo if r_lo is None else r_lo + wk * lo
        r_hi = wk * hi if r_hi is None else r_hi + wk * hi
    out_ref[...] = base_ref[...] + gt2_ref[...] * jnp.concatenate([r_lo, r_hi], axis=1)


def _combine(y_slots, base, w_tk, gt2, S):
    T = base.shape[0]
    tt = WIDE_TILE
    nt = T // tt
    per_b = S // tt
    yspec = lambda half, k: pl.BlockSpec((tt, HALF), lambda i: ((half * TOP_K + k) * nt + i, 0))
    return pl.pallas_call(
        _combine_body, out_shape=jax.ShapeDtypeStruct((T, D_MODEL), F32), grid=(nt,),
        in_specs=[yspec(half, k) for half in range(2) for k in range(TOP_K)] + [
            pl.BlockSpec((tt, D_MODEL), lambda i: (i, 0)),
            pl.BlockSpec((tt, TOP_K), lambda i: (i, 0)),
            pl.BlockSpec((None, 1, D_MODEL), lambda i: (i // per_b, 0, 0)),
        ],
        out_specs=pl.BlockSpec((tt, D_MODEL), lambda i: (i, 0)),
        compiler_params=_params(1), name="combine",
    )(*([y_slots] * (2 * TOP_K)), base, w_tk, gt2)


def _pad_heads(w, width):
    kdim = w.shape[0]
    w3 = w.reshape(kdim, MLA_HEADS, width)
    return jnp.pad(w3, ((0, 0), (0, 0), (0, LANES - width))).reshape(kdim, MLA_HEADS * LANES)


def _rot_partner(w_rope):
    half = QK_ROPE // 2
    return jnp.concatenate([-w_rope[..., half:], w_rope[..., :half]], axis=-1)


def _lane_vec(nope, rope):
    return jnp.concatenate([nope, rope, jnp.zeros((LANES - QK_DIM,), F32)]).reshape(1, LANES)


def _layer(x2, ada, positions, B, S, g_norm1, w_in, g_cq, w_uq, g_ckv, w_ukv, g_qn, g_kn, g_attn_out,
           conv_w, conv_b, w_mq, w_mk, b_igate, b_fgate, g_mlstm_out, w_out, g_norm2, w_router, router_bias,
           w_gate_exp, w_up_exp, w_down_exp, w_gate_sh, w_up_sh, w_down_sh):
    T = B * S
    sh1, sc1, gt1, sh2, sc2, gt2 = [a.reshape(B, 1, D_MODEL) for a in jnp.split(ada, 6, axis=-1)]
    row = lambda v: v.reshape(1, -1).astype(F32)

    inv_freq = ROPE_THETA ** (-jnp.arange(0, QK_ROPE, 2, dtype=F32) / QK_ROPE)
    ang = positions.astype(F32).reshape(T, 1) * inv_freq
    half = QK_ROPE // 2
    freq_lane = jnp.arange(half)[:, None]
    lane = jnp.arange(LANES)[None, :]
    place = ((lane == QK_NOPE + freq_lane) | (lane == QK_NOPE + half + freq_lane)).astype(F32)
    spread = lambda tbl: jnp.dot(tbl, place, precision=lax.Precision.HIGHEST)
    cosp = (jnp.arange(LANES) < QK_NOPE).astype(F32)[None, :] + spread(jnp.cos(ang))
    sinp = spread(jnp.sin(ang))

    o0 = 0
    w_cq = w_in[:, o0:o0 + Q_LORA]; o0 += Q_LORA
    w_ckv = w_in[:, o0:o0 + KV_LORA]; o0 += KV_LORA
    w_kr = w_in[:, o0:o0 + QK_ROPE]; o0 += QK_ROPE
    w_z = w_in[:, o0:o0 + MLSTM_INNER]; o0 += MLSTM_INNER
    w_v = w_in[:, o0:o0 + MLSTM_INNER]; o0 += MLSTM_INNER
    w_o = w_in[:, o0:o0 + MLSTM_INNER]; o0 += MLSTM_INNER
    w_i = w_in[:, o0:o0 + MLSTM_HEADS]; o0 += MLSTM_HEADS
    w_f = w_in[:, o0:o0 + MLSTM_HEADS]
    zl = jnp.zeros((D_MODEL, QK_NOPE), F32)
    zr = jnp.zeros((D_MODEL, LANES - QK_DIM), F32)
    wa = jnp.concatenate([w_cq, w_ckv, w_z, w_o, zl, w_kr, zr, zl, _rot_partner(w_kr), zr], axis=1).astype(BF16)
    wt = jnp.concatenate([w_v.T, w_i.T, w_f.T, jnp.zeros((16 - 2 * MLSTM_HEADS, D_MODEL), F32)], axis=0).astype(BF16)

    uq3 = w_uq.reshape(Q_LORA, MLA_HEADS, QK_DIM)
    uq_rot = jnp.concatenate([jnp.zeros((Q_LORA, MLA_HEADS, QK_NOPE), F32), _rot_partner(uq3[..., QK_NOPE:])], axis=-1)
    wuq = _pad_heads(w_uq, QK_DIM).astype(BF16)
    wuqs = _pad_heads(uq_rot.reshape(Q_LORA, MLA_HEADS * QK_DIM), QK_DIM).astype(BF16)
    ukv3 = w_ukv.reshape(KV_LORA, MLA_HEADS, QK_NOPE + V_DIM)
    wuk = _pad_heads(ukv3[..., :QK_NOPE].reshape(KV_LORA, MLA_HEADS * QK_NOPE), QK_NOPE).astype(BF16)
    wuv = _pad_heads(ukv3[..., QK_NOPE:].reshape(KV_LORA, MLA_HEADS * V_DIM), V_DIM).astype(BF16)
    vone = jnp.tile((jnp.arange(LANES) == V_DIM).astype(F32), MLA_HEADS).reshape(1, MLA_HEADS * LANES)
    swap = lambda g_rope: jnp.concatenate([g_rope[half:], g_rope[:half]])
    no_nope = jnp.zeros((QK_NOPE,), F32)
    gq = _lane_vec(g_qn[:QK_NOPE], g_qn[QK_NOPE:])
    gqs = _lane_vec(no_nope, swap(g_qn[QK_NOPE:]))
    gk = _lane_vec(g_kn[:QK_NOPE], g_kn[QK_NOPE:])
    gks = _lane_vec(no_nope, swap(g_kn[QK_NOPE:]))

    z, og, vt3, g3, q, k, mv = _inproj(
        x2, 1.0 + sc1, sh1, row(g_norm1), wa, wt, row(g_cq), wuq, wuqs, row(g_ckv), wuk, wuv, vone,
        gq, gqs, gk, gks, cosp, sinp, S)
    attn_o = _attn(q, k, mv, B, S)
    mlstm_o = _mlstm(z, vt3, g3, og, conv_w.reshape(CONV_WIDTH, MLSTM_INNER), row(conv_b),
                     w_mq.astype(BF16), w_mk.astype(BF16), b_igate, b_fgate, row(g_mlstm_out), B, S)

    wgu = jnp.concatenate([w_gate_sh, w_up_sh], axis=1).astype(BF16)
    base, h2, lgt = _post(attn_o, mlstm_o, x2, gt1, row(g_attn_out), w_out.astype(BF16), row(g_norm2),
                          1.0 + sc2, sh2, gt2, w_router.T.astype(BF16), wgu, w_down_sh.astype(BF16), S)

    idx_t, w_t, rank_t, cnt = _route(lgt, router_bias.reshape(N_EXPERTS, 1))
    counts = cnt[:, 0].astype(I32)
    padded = (counts + ROW_BLOCK - 1) // ROW_BLOCK * ROW_BLOCK
    padded_end = jnp.cumsum(padded)
    off = padded_end - padded
    n_rows = T * TOP_K + N_EXPERTS * ROW_BLOCK
    dest_t = _dest(idx_t, rank_t, off.reshape(N_EXPERTS, 1))

    dest2_t = jnp.concatenate([dest_t, dest_t + n_rows], axis=1)
    xs = _scatter_rows(dest2_t, h2.reshape(2 * T, HALF), 2 * n_rows).reshape(2, n_rows, HALF)
    y = _experts(off // ROW_BLOCK, padded // ROW_BLOCK, counts, xs, w_gate_exp, w_up_exp, w_down_exp)
    dest_row = dest_t.reshape(1, TOP_K * T)
    y_slots = _gather_rows(jnp.concatenate([dest_row, dest_row + n_rows], axis=1), y.reshape(2 * n_rows, HALF))
    return _combine(y_slots, base, w_t.T, gt2, S)


def kernel(x, c, positions, w_ada, b_ada, g_norm1, w_in, g_cq, w_uq, g_ckv, w_ukv, g_qn, g_kn, g_attn_out, conv_w, conv_b, w_mq, w_mk, b_igate, b_fgate, g_mlstm_out, w_out, g_norm2, w_router, router_bias, w_gate_exp, w_up_exp, w_down_exp, w_gate_sh, w_up_sh, w_down_sh):
    B, S, d_model = x.shape
    assert d_model == D_MODEL and S % WIDE_TILE == 0 and S % SEQ_TILE == 0, (B, S, d_model)
    assert w_gate_exp.shape[1:] == (N_EXPERTS, D_MODEL, EXPERT_DIM), w_gate_exp.shape
    depth = w_ada.shape[0]
    x2 = x.reshape(B * S, D_MODEL)
    for l in range(depth):
        ada = _ada(c, w_ada[l], b_ada[l])
        x2 = _layer(x2, ada, positions, B, S, g_norm1[l], w_in[l], g_cq[l], w_uq[l], g_ckv[l], w_ukv[l], g_qn[l],
                    g_kn[l], g_attn_out[l], conv_w[l], conv_b[l], w_mq[l], w_mk[l], b_igate[l], b_fgate[l],
                    g_mlstm_out[l], w_out[l], g_norm2[l], w_router[l], router_bias[l], w_gate_exp[l], w_up_exp[l],
                    w_down_exp[l], w_gate_sh[l], w_up_sh[l], w_down_sh[l])
    return x2.reshape(B, S, D_MODEL)
```

```python
import functools
import math

import jax
import jax.numpy as jnp
from jax import lax
from jax.experimental import pallas as pl
from jax.experimental.pallas import tpu as pltpu
from jax.experimental.pallas import tpu_sc as plsc

F32 = jnp.float32
BF16 = jnp.bfloat16
I32 = jnp.int32

D_MODEL = 1024
MLA_HEADS = 8
QK_NOPE = 64
QK_ROPE = 32
QK_DIM = QK_NOPE + QK_ROPE
V_DIM = 64
Q_LORA = 384
KV_LORA = 256
ROPE_THETA = 10000.0
MLSTM_HEADS = 4
MLSTM_DIM = 128
MLSTM_INNER = MLSTM_HEADS * MLSTM_DIM
CONV_WIDTH = 4
N_EXPERTS = 256
TOP_K = 8
N_GROUPS = 8
TOPK_GROUPS = 4
GROUP_SIZE = N_EXPERTS // N_GROUPS
EXPERT_DIM = 256
SHARED_DIM = 256
ROUTED_SCALE = 2.5
RMS_EPS = 1e-6

LANES = 128
SEQ_TILE = 256
WIDE_TILE = 512
ROW_BLOCK = 256
VMEM_LIMIT = 56 * 1024 * 1024

_OFF_CQ = 0
_OFF_CKV = _OFF_CQ + Q_LORA
_OFF_Z = _OFF_CKV + KV_LORA
_OFF_O = _OFF_Z + MLSTM_INNER
_OFF_KR = _OFF_O + MLSTM_INNER
_OFF_KRS = _OFF_KR + LANES
_WA_COLS = _OFF_KRS + LANES
_WT_ROWS = MLSTM_INNER + 16


def _params(n_axes, vmem=VMEM_LIMIT):
    return pltpu.CompilerParams(dimension_semantics=("arbitrary",) * n_axes, vmem_limit_bytes=vmem)


def _dot(a, b):
    return jnp.dot(a, b, preferred_element_type=F32)


def _dot_nt(a, b):
    return lax.dot_general(a, b, (((1,), (1,)), ((), ())), preferred_element_type=F32)


def _sigmoid(x):
    return 1.0 / (1.0 + jnp.exp(-x))


def _silu(x):
    return x * _sigmoid(x)


U32 = jnp.uint32
PACKED = D_MODEL // 2


def _pack_rows(x):
    w = x.shape[1] // 2
    lo = lax.bitcast_convert_type(x[:, :w].astype(BF16).astype(F32), U32)
    hi = lax.bitcast_convert_type(x[:, w:].astype(BF16).astype(F32), U32)
    return lax.shift_right_logical(lo, U32(16)) | (hi & U32(0xFFFF0000))


def _unpack_rows(p):
    lo = lax.bitcast_convert_type(lax.shift_left(p, U32(16)), F32)
    hi = lax.bitcast_convert_type(p & U32(0xFFFF0000), F32)
    return lo, hi


def _ada_body(c_ref, w_ref, b_ref, o_ref):
    c = c_ref[...]
    cond = _silu(c).astype(BF16)
    o_ref[...] = _dot(cond, w_ref[...].astype(BF16)) + b_ref[...]


def _ada(c, w_ada, b_ada):
    B = c.shape[0]
    n = w_ada.shape[1]
    tn = 1024
    return pl.pallas_call(
        _ada_body,
        out_shape=jax.ShapeDtypeStruct((B, n), F32),
        grid=(n // tn,),
        in_specs=[
            pl.BlockSpec((B, D_MODEL), lambda j: (0, 0)),
            pl.BlockSpec((D_MODEL, tn), lambda j: (0, j)),
            pl.BlockSpec((1, tn), lambda j: (0, j)),
        ],
        out_specs=pl.BlockSpec((B, tn), lambda j: (0, j)),
        compiler_params=_params(1),
        name="ada",
    )(c, w_ada, b_ada.reshape(1, n))


def _inproj_body(x_ref, sc_ref, sh_ref, g1_ref, wa_ref, wt_ref, gcq_ref, wuq_ref, wuqs_ref, gckv_ref,
                 wuk_ref, wuv_ref, vone_ref, gq_ref, gqs_ref, gk_ref, gks_ref, cos_ref, sin_ref,
                 z_ref, o_ref, vt_ref, gt_ref, q_ref, k_ref, mv_ref):
    x = x_ref[...]
    hn = x * lax.rsqrt(jnp.mean(x * x, axis=-1, keepdims=True) + RMS_EPS) * g1_ref[...]
    hm = (hn * sc_ref[...] + sh_ref[...]).astype(BF16)
    p = _dot(hm, wa_ref[...])
    z_ref[...] = p[:, _OFF_Z:_OFF_Z + MLSTM_INNER].astype(BF16)
    o_ref[...] = p[:, _OFF_O:_OFF_O + MLSTM_INNER].astype(BF16)
    rt = _dot_nt(wt_ref[...], hm)
    vt_ref[...] = rt[:MLSTM_INNER].astype(BF16)
    gt_ref[...] = rt[MLSTM_INNER:]

    cq = p[:, _OFF_CQ:_OFF_CQ + Q_LORA]
    cqn = (cq * lax.rsqrt(jnp.mean(cq * cq, axis=-1, keepdims=True) + RMS_EPS) * gcq_ref[...]).astype(BF16)
    ckv = p[:, _OFF_CKV:_OFF_CKV + KV_LORA]
    ckvn = (ckv * lax.rsqrt(jnp.mean(ckv * ckv, axis=-1, keepdims=True) + RMS_EPS) * gckv_ref[...]).astype(BF16)
    qa = _dot(cqn, wuq_ref[...])
    qb = _dot(cqn, wuqs_ref[...])
    ka = _dot(ckvn, wuk_ref[...])
    mv_ref[...] = (_dot(ckvn, wuv_ref[...]) + vone_ref[...]).astype(BF16)
    kr = p[:, _OFF_KR:_OFF_KR + LANES]
    krs = p[:, _OFF_KRS:_OFF_KRS + LANES]

    cosp = cos_ref[...]
    sinp = sin_ref[...]
    gqc = gq_ref[...] * cosp
    gqs = gqs_ref[...] * sinp
    gkc = gk_ref[...] * cosp
    k_rot = krs * (gks_ref[...] * sinp)
    kr_ss = jnp.sum(kr * kr, axis=-1, keepdims=True)
    q_scale = (QK_DIM ** -0.5) * math.log2(math.e)
    for h in range(MLA_HEADS):
        sl = slice(h * LANES, (h + 1) * LANES)
        qah = qa[:, sl]
        rq = lax.rsqrt(jnp.sum(qah * qah, axis=-1, keepdims=True) * (1.0 / QK_DIM) + RMS_EPS) * q_scale
        q_ref[:, sl] = ((qah * gqc + qb[:, sl] * gqs) * rq).astype(BF16)
        kah = ka[:, sl]
        rk = lax.rsqrt((jnp.sum(kah * kah, axis=-1, keepdims=True) + kr_ss) * (1.0 / QK_DIM) + RMS_EPS)
        k_ref[:, sl] = (((kah + kr) * gkc + k_rot) * rk).astype(BF16)


def _inproj(x2, sc1p, sh1, g1, wa, wt, gcq, wuq, wuqs, gckv, wuk, wuv, vone, gq, gqs, gk, gks, cosp, sinp, S):
    T = x2.shape[0]
    tm = SEQ_TILE
    nt = T // tm
    per_b = S // tm
    hp = MLA_HEADS * LANES
    full = lambda a: pl.BlockSpec(a.shape, lambda i: (0,) * a.ndim)
    row = lambda w: pl.BlockSpec((tm, w), lambda i: (i, 0))
    bmod = pl.BlockSpec((None, 1, D_MODEL), lambda i: (i // per_b, 0, 0))
    out_shapes = (
        jax.ShapeDtypeStruct((T, MLSTM_INNER), BF16),
        jax.ShapeDtypeStruct((T, MLSTM_INNER), BF16),
        jax.ShapeDtypeStruct((nt, MLSTM_INNER, tm), BF16),
        jax.ShapeDtypeStruct((nt, 16, tm), F32),
        jax.ShapeDtypeStruct((T, hp), BF16),
        jax.ShapeDtypeStruct((T, hp), BF16),
        jax.ShapeDtypeStruct((T, hp), BF16),
    )
    out_specs = (
        row(MLSTM_INNER), row(MLSTM_INNER),
        pl.BlockSpec((None, MLSTM_INNER, tm), lambda i: (i, 0, 0)),
        pl.BlockSpec((None, 16, tm), lambda i: (i, 0, 0)),
        row(hp), row(hp), row(hp),
    )
    in_specs = [row(D_MODEL), bmod, bmod, full(g1), full(wa), full(wt), full(gcq), full(wuq), full(wuqs),
                full(gckv), full(wuk), full(wuv), full(vone), full(gq), full(gqs), full(gk), full(gks),
                row(LANES), row(LANES)]
    return pl.pallas_call(
        _inproj_body, out_shape=out_shapes, grid=(nt,), in_specs=in_specs, out_specs=out_specs,
        compiler_params=_params(1), name="inproj",
    )(x2, sc1p, sh1, g1, wa, wt, gcq, wuq, wuqs, gckv, wuk, wuv, vone, gq, gqs, gk, gks, cosp, sinp)


ATTN_HEADS_PER_STEP = 8


def _blocked_loop(n, fn, init):
    def run(width, start, trips, c):
        def trip(t, c):
            for d in range(width):
                c = fn(start + width * t + d, c)
            return c
        return lax.fori_loop(0, trips, trip, c)

    quads = n // 4
    c = run(4, 0, quads, init)
    c = run(2, 4 * quads, (n // 2) % 2, c)
    return run(1, n - n % 2, n % 2, c)


def _attn_body(q_ref, k_ref, v_ref, o_ref, s_scr):
    i = pl.program_id(2)
    t = SEQ_TILE
    nh = ATTN_HEADS_PER_STEP
    qry_pos = lax.broadcasted_iota(I32, (t, t), 0)
    key_pos = lax.broadcasted_iota(I32, (t, t), 1)
    causal = key_pos <= qry_pos
    qs = [q_ref[:, hh * LANES:(hh + 1) * LANES] for hh in range(nh)]

    def scores(j, hh, diag):
        kj = k_ref[pl.ds(pl.multiple_of(j * t, t), t), hh * LANES:(hh + 1) * LANES]
        s = _dot_nt(qs[hh], kj)
        return jnp.where(causal, s, -jnp.inf) if diag else s

    def halves(x):
        return x[:, :LANES], x[:, LANES:]

    def max_pass(j, mx, diag):
        out = []
        for hh in range(nh):
            s = scores(j, hh, diag)
            s_scr[j, hh] = s
            lo, hi = halves(s)
            out.append(jnp.maximum(mx[hh], jnp.maximum(lo, hi)))
        return tuple(out)

    mx = tuple(jnp.full((t, LANES), -jnp.inf, F32) for _ in range(nh))
    mx = _blocked_loop(i, lambda j, c: max_pass(j, c, False), mx)
    mx = max_pass(i, mx, True)
    mb = [jnp.broadcast_to(jnp.max(mx[hh], axis=1, keepdims=True), (t, LANES)) for hh in range(nh)]

    def acc_pass(j, carry, diag):
        row0 = pl.multiple_of(j * t, t)
        out = []
        for hh in range(nh):
            lo, hi = halves(s_scr[j, hh])
            p = jnp.concatenate([jnp.exp2(lo - mb[hh]), jnp.exp2(hi - mb[hh])], axis=1).astype(BF16)
            out.append(carry[hh] + _dot(p, v_ref[pl.ds(row0, t), hh * LANES:(hh + 1) * LANES]))
        return tuple(out)

    init = tuple(jnp.zeros((t, LANES), F32) for _ in range(nh))
    carry = _blocked_loop(i, lambda j, c: acc_pass(j, c, False), init)
    final = acc_pass(i, carry, True)
    outs = [acc / acc[:, V_DIM:V_DIM + 1] for acc in final]
    first_head = lax.broadcasted_iota(I32, (t, LANES), 1) < V_DIM
    o_ref[...] = jnp.concatenate(
        [jnp.where(first_head, outs[2 * pr], pltpu.roll(outs[2 * pr + 1], V_DIM, axis=1)) for pr in range(nh // 2)],
        axis=1).astype(BF16)


def _attn(q, k, mvt, B, S):
    T = q.shape[0]
    t = SEQ_TILE
    nh = ATTN_HEADS_PER_STEP
    per_b = S // t
    return pl.pallas_call(
        _attn_body,
        out_shape=jax.ShapeDtypeStruct((T, MLA_HEADS * V_DIM), BF16),
        grid=(B, MLA_HEADS // nh, per_b),
        in_specs=[
            pl.BlockSpec((t, nh * LANES), lambda b, hp, i: (b * per_b + i, hp)),
            pl.BlockSpec((S, nh * LANES), lambda b, hp, i: (b, hp)),
            pl.BlockSpec((S, nh * LANES), lambda b, hp, i: (b, hp)),
        ],
        out_specs=pl.BlockSpec((t, nh * V_DIM), lambda b, hp, i: (b * per_b + i, hp)),
        scratch_shapes=[pltpu.VMEM((per_b, nh, t, t), F32)],
        compiler_params=_params(3), name="attn",
    )(q, k, mvt)


def _split3(a):
    hi = a.astype(BF16)
    r = a - hi.astype(F32)
    mid = r.astype(BF16)
    lo = (r - mid.astype(F32)).astype(BF16)
    return hi, mid, lo


def _mlstm_body(bi_ref, bf_ref, z_ref, vt_ref, g_ref, og_ref, cw_ref, cb_ref, wq_ref, wk_ref, gout_ref,
                out_ref, zpad, q_s, k_s, st_s, m_s, *, S):
    h = pl.program_id(1)
    L = SEQ_TILE
    nchunk = S // L
    hd = MLSTM_DIM
    zpad[0:8, :] = jnp.zeros((8, hd), F32)
    zpad[8:, :] = z_ref[...].astype(F32)
    zc = cb_ref[...] + jnp.zeros((S, hd), F32)
    for j in range(CONV_WIDTH):
        zc = zc + cw_ref[j:j + 1, :] * zpad[pl.ds(8 - (CONV_WIDTH - 1) + j, S), :]
    zs = _silu(zc).astype(BF16)
    q_s[...] = _dot(zs, wq_ref[...]).astype(BF16)
    k_s[...] = (_dot(zs, wk_ref[...]) * (hd ** -0.5)).astype(BF16)

    st_s[...] = jnp.zeros_like(st_s)
    m_s[...] = jnp.zeros_like(m_s)
    b_i = bi_ref[h]
    b_f = bf_ref[h]
    r_i = lax.broadcasted_iota(I32, (L, L), 0)
    c_i = lax.broadcasted_iota(I32, (L, L), 1)
    tril = r_i >= c_i
    strict_lower = jnp.where(r_i > c_i, 1.0, 0.0).astype(BF16)
    ones_row = jnp.where(lax.broadcasted_iota(I32, (hd, L), 0) == 0, 1.0, 0.0).astype(BF16)

    def chunk(c, _):
        start = pl.multiple_of(c * L, L)
        qc = q_s[pl.ds(start, L), :]
        kc = k_s[pl.ds(start, L), :]
        vt_aug = jnp.concatenate([vt_ref[c], ones_row], axis=0)
        li = g_ref[c, pl.ds(h, 1), :] + b_i
        fp = g_ref[c, pl.ds(MLSTM_HEADS + h, 1), :] + b_f
        lf = jnp.minimum(fp, 0.0) - jnp.log(1.0 + jnp.exp(-jnp.abs(fp)))
        a_mat = jnp.where(tril, lf, 0.0)
        a_hi, a_mid, a_lo = _split3(a_mat)
        d0 = _dot(a_hi, strict_lower) + _dot(a_mid, strict_lower) + _dot(a_lo, strict_lower)
        dmat = jnp.where(tril, d0 + li, -jnp.inf)
        b_col = jnp.sum(a_mat, axis=1, keepdims=True)
        m_prev = m_s[...]
        inter = b_col + m_prev
        mt = jnp.maximum(inter, jnp.max(dmat, axis=1, keepdims=True))
        w = jnp.exp(dmat - mt) * _dot_nt(qc, kc)
        decay = jnp.exp(inter - mt)
        st = st_s[...]
        num_aug = decay * _dot_nt(qc, st.astype(BF16)) + _dot_nt(w.astype(BF16), vt_aug)
        num = num_aug[:, :hd]
        den = num_aug[:, hd:hd + 1]
        hv = num / jnp.maximum(jnp.abs(den), jnp.exp(-mt))
        g = d0[L - 1:L, :] + li
        b_last = b_col[L - 1:L, :]
        m_new = jnp.maximum(b_last + m_prev, jnp.max(g, axis=1, keepdims=True))
        a = jnp.exp(g - m_new)
        cd = jnp.exp(b_last + m_prev - m_new)
        st_s[...] = cd * st + _dot((vt_aug.astype(F32) * a).astype(BF16), kc)
        m_s[...] = m_new
        hn = hv * lax.rsqrt(jnp.mean(hv * hv, axis=-1, keepdims=True) + RMS_EPS) * gout_ref[...]
        og = og_ref[pl.ds(start, L), :].astype(F32)
        out_ref[pl.ds(start, L), :] = (_sigmoid(og) * hn).astype(BF16)
        return 0

    lax.fori_loop(0, nchunk, chunk, 0)


def _mlstm(z, vt3, g3, og, conv_w, conv_b, wq, wk, b_i, b_f, g_out, B, S):
    T = z.shape[0]
    L = SEQ_TILE
    per_b = S // L
    hd = MLSTM_DIM
    smem = pl.BlockSpec(memory_space=pltpu.SMEM)
    seq = pl.BlockSpec((S, hd), lambda b, h: (b, h))
    return pl.pallas_call(
        functools.partial(_mlstm_body, S=S),
        out_shape=jax.ShapeDtypeStruct((T, MLSTM_INNER), BF16),
        grid=(B, MLSTM_HEADS),
        in_specs=[
            smem, smem, seq,
            pl.BlockSpec((per_b, hd, L), lambda b, h: (b, h, 0)),
            pl.BlockSpec((per_b, 16, L), lambda b, h: (b, 0, 0)),
            seq,
            pl.BlockSpec((CONV_WIDTH, hd), lambda b, h: (0, h)),
            pl.BlockSpec((1, hd), lambda b, h: (0, h)),
            pl.BlockSpec((None, hd, hd), lambda b, h: (h, 0, 0)),
            pl.BlockSpec((None, hd, hd), lambda b, h: (h, 0, 0)),
            pl.BlockSpec((1, hd), lambda b, h: (0, h)),
        ],
        out_specs=seq,
        scratch_shapes=[
            pltpu.VMEM((S + 8, hd), F32),
            pltpu.VMEM((S, hd), BF16),
            pltpu.VMEM((S, hd), BF16),
            pltpu.VMEM((2 * hd, hd), F32),
            pltpu.VMEM((1, 1), F32),
        ],
        compiler_params=_params(2), name="mlstm",
    )(b_i, b_f, z, vt3, g3, og, conv_w, conv_b, wq, wk, g_out)


def _post_body(ao_ref, mo_ref, x_ref, gt1_ref, gattn_ref, wout_ref, g2_ref, sc2_ref, sh2_ref, gt2_ref,
               wrt_ref, wgu_ref, wds_ref, base_ref, h2_ref, lg_ref):
    ao = ao_ref[...].astype(F32)
    aon = ao * lax.rsqrt(jnp.mean(ao * ao, axis=-1, keepdims=True) + RMS_EPS) * gattn_ref[...]
    mix_in = jnp.concatenate([aon.astype(BF16), mo_ref[...]], axis=1)
    x1 = x_ref[...] + gt1_ref[...] * _dot(mix_in, wout_ref[...])
    hn = x1 * lax.rsqrt(jnp.mean(x1 * x1, axis=-1, keepdims=True) + RMS_EPS) * g2_ref[...]
    h2 = hn * sc2_ref[...] + sh2_ref[...]
    words = _pack_rows(h2)
    h2_ref[0] = words[:, :HALF]
    h2_ref[1] = words[:, HALF:]
    h2b = h2.astype(BF16)
    gu = _dot(h2b, wgu_ref[...])
    act = (_silu(gu[:, :SHARED_DIM]) * gu[:, SHARED_DIM:]).astype(BF16)
    base_ref[...] = x1 + gt2_ref[...] * _dot(act, wds_ref[...])
    lg_ref[...] = _dot_nt(wrt_ref[...], h2b)


def _post(ao, mo, x2, gt1, gattn, wout, g2, sc2p, sh2, gt2, wrt, wgu, wds, S):
    T = x2.shape[0]
    tm = WIDE_TILE
    per_b = S // tm
    full = lambda a: pl.BlockSpec(a.shape, lambda i: (0,) * a.ndim)
    row = lambda w: pl.BlockSpec((tm, w), lambda i: (i, 0))
    bmod = pl.BlockSpec((None, 1, D_MODEL), lambda i: (i // per_b, 0, 0))
    return pl.pallas_call(
        _post_body,
        out_shape=(jax.ShapeDtypeStruct((T, D_MODEL), F32), jax.ShapeDtypeStruct((2, T, HALF), U32),
                   jax.ShapeDtypeStruct((N_EXPERTS, T), F32)),
        grid=(T // tm,),
        in_specs=[row(MLA_HEADS * V_DIM), row(MLSTM_INNER), row(D_MODEL), bmod, full(gattn), full(wout),
                  full(g2), bmod, bmod, bmod, full(wrt), full(wgu), full(wds)],
        out_specs=(row(D_MODEL), pl.BlockSpec((2, tm, HALF), lambda i: (0, i, 0)),
                   pl.BlockSpec((N_EXPERTS, tm), lambda i: (0, i))),
        compiler_params=_params(1), name="post",
    )(ao, mo, x2, gt1, gattn, wout, g2, sc2p, sh2, gt2, wrt, wgu, wds)


def _first_max(x, rows, n):
    mx = jnp.max(x, axis=0, keepdims=True)
    idx = jnp.min(jnp.where(x == mx, rows, n), axis=0, keepdims=True)
    return mx, idx


def _route_body(lg_ref, bias_ref, idx_ref, w_ref, rank_ref, cnt_ref, carry):
    i = pl.program_id(0)
    tt = lg_ref.shape[1]

    @pl.when(i == 0)
    def _():
        carry[...] = jnp.zeros_like(carry)

    sc = _sigmoid(lg_ref[...])
    bi = sc + bias_ref[...]
    rows = lax.broadcasted_iota(I32, (N_EXPERTS, tt), 0)
    rows_g = lax.broadcasted_iota(I32, (GROUP_SIZE, tt), 0)
    rows_n = lax.broadcasted_iota(I32, (N_GROUPS, tt), 0)
    neg = -jnp.inf
    gs = []
    for g in range(N_GROUPS):
        xg = bi[g * GROUP_SIZE:(g + 1) * GROUP_SIZE]
        m1, i1 = _first_max(xg, rows_g, GROUP_SIZE)
        m2 = jnp.max(jnp.where(rows_g == i1, neg, xg), axis=0, keepdims=True)
        gs.append(m1 + m2)
    cur = jnp.concatenate(gs, axis=0)
    keep = jnp.zeros((N_GROUPS, tt), F32)
    for _ in range(TOPK_GROUPS):
        _, gi = _first_max(cur, rows_n, N_GROUPS)
        sel = rows_n == gi
        keep = jnp.where(sel, 1.0, keep)
        cur = jnp.where(sel, neg, cur)
    masked = jnp.concatenate(
        [jnp.where(keep[g:g + 1] > 0.0, bi[g * GROUP_SIZE:(g + 1) * GROUP_SIZE], neg) for g in range(N_GROUPS)], axis=0)
    cur = masked
    onehot = jnp.zeros((N_EXPERTS, tt), F32)
    idxs, ws = [], []
    for _ in range(TOP_K):
        _, ei = _first_max(cur, rows, N_EXPERTS)
        sel = rows == ei
        ws.append(jnp.sum(jnp.where(sel, sc, 0.0), axis=0, keepdims=True))
        idxs.append(ei)
        cur = jnp.where(sel, neg, cur)
        onehot = jnp.where(sel, 1.0, onehot)
    wsum = ws[0]
    for k in range(1, TOP_K):
        wsum = wsum + ws[k]
    idx_ref[...] = jnp.concatenate(idxs, axis=0)
    w_ref[...] = jnp.concatenate([w / wsum * ROUTED_SCALE for w in ws], axis=0)
    r_i = lax.broadcasted_iota(I32, (tt, tt), 0)
    c_i = lax.broadcasted_iota(I32, (tt, tt), 1)
    before = jnp.where(r_i < c_i, 1.0, 0.0).astype(BF16)
    tot = _dot(onehot.astype(BF16), before) + carry[...]
    rank_ref[...] = jnp.concatenate(
        [jnp.sum(jnp.where(rows == ei, tot, 0.0), axis=0, keepdims=True) for ei in idxs], axis=0).astype(I32)
    carry[...] = carry[...] + jnp.sum(onehot, axis=1, keepdims=True)
    cnt_ref[...] = jnp.broadcast_to(carry[...], cnt_ref.shape)


def _route(lgt, bias_col):
    T = lgt.shape[1]
    tt = WIDE_TILE
    blk = pl.BlockSpec((TOP_K, tt), lambda i: (0, i))
    return pl.pallas_call(
        _route_body,
        out_shape=(jax.ShapeDtypeStruct((TOP_K, T), I32), jax.ShapeDtypeStruct((TOP_K, T), F32),
                   jax.ShapeDtypeStruct((TOP_K, T), I32), jax.ShapeDtypeStruct((N_EXPERTS, LANES), F32)),
        grid=(T // tt,),
        in_specs=[pl.BlockSpec((N_EXPERTS, tt), lambda i: (0, i)), pl.BlockSpec((N_EXPERTS, 1), lambda i: (0, 0))],
        out_specs=(blk, blk, blk, pl.BlockSpec((N_EXPERTS, LANES), lambda i: (0, 0))),
        scratch_shapes=[pltpu.VMEM((N_EXPERTS, 1), F32)],
        compiler_params=_params(1), name="route",
    )(lgt, bias_col)


def _dest_body(idx_ref, rank_ref, off_ref, dest_ref):
    tt = idx_ref.shape[1]
    rows = lax.broadcasted_iota(I32, (N_EXPERTS, tt), 0)
    off = off_ref[...]
    outs = []
    for k in range(TOP_K):
        sel = rows == idx_ref[k:k + 1, :]
        outs.append(jnp.sum(jnp.where(sel, off, 0), axis=0, keepdims=True) + rank_ref[k:k + 1, :])
    dest_ref[...] = jnp.concatenate(outs, axis=0)


def _dest(idx_t, rank_t, off_col):
    T = idx_t.shape[1]
    tt = min(4 * SEQ_TILE, T)
    blk = pl.BlockSpec((TOP_K, tt), lambda i: (0, i))
    return pl.pallas_call(
        _dest_body, out_shape=jax.ShapeDtypeStruct((TOP_K, T), I32), grid=(T // tt,),
        in_specs=[blk, blk, pl.BlockSpec((N_EXPERTS, 1), lambda i: (0, 0))], out_specs=blk,
        compiler_params=_params(1), name="dest",
    )(idx_t, rank_t, off_col)


SC_WINDOW = 128
HALF = PACKED // 2


def _sc_mesh():
    return plsc.VectorSubcoreMesh(core_axis_name="c", subcore_axis_name="s")


def _scatter_rows(dest2_t, rows, n_out):
    n = rows.shape[0]

    @pl.kernel(out_type=jax.ShapeDtypeStruct((n_out, HALF), U32), mesh=_sc_mesh(), scratch_types=[])
    def scatter(d_hbm, x_hbm, o_hbm):
        def body(d_vmem, x_vmem):
            for k in range(TOP_K):
                pltpu.sync_copy(x_vmem, o_hbm.at[d_vmem.at[k]])

        pltpu.emit_pipeline(
            body, grid=(n // SC_WINDOW,),
            in_specs=[pl.BlockSpec((TOP_K, SC_WINDOW), lambda i: (0, i)),
                      pl.BlockSpec((SC_WINDOW, HALF), lambda i: (i, 0))],
            out_specs=[], core_axis_name=("c", "s"), dimension_semantics=(pltpu.PARALLEL,),
        )(d_hbm, x_hbm)

    return scatter(dest2_t, rows)


def _gather_rows(index_row, rows):
    n = index_row.shape[1]

    @pl.kernel(out_type=jax.ShapeDtypeStruct((n, HALF), U32), mesh=_sc_mesh(), scratch_types=[])
    def gather(i_hbm, y_hbm, o_hbm):
        def body(i_vmem, o_vmem):
            pltpu.sync_copy(y_hbm.at[i_vmem.at[0]], o_vmem)

        pltpu.emit_pipeline(
            body, grid=(n // SC_WINDOW,),
            in_specs=[pl.BlockSpec((1, SC_WINDOW), lambda i: (0, i))],
            out_specs=[pl.BlockSpec((SC_WINDOW, HALF), lambda i: (i, 0))],
            core_axis_name=("c", "s"), dimension_semantics=(pltpu.PARALLEL,),
        )(i_hbm, o_hbm)

    return gather(index_row, rows)


EXPERT_RING = 16
WEIGHT_RING = 3


def _experts_body(first_ref, nblk_ref, cnt_ref, xs_ref, wg_ref, wu_ref, wd_ref, y_ref,
                  xbuf, ybuf, wgf, wuf, wdf, wgb, wub, wdb, xsem, ysem, wsem):
    e = pl.program_id(0)
    last = pl.num_programs(0) - 1
    wring = wgf.shape[0]

    def w_copies(ex):
        slot = ex % wring
        return [pltpu.make_async_copy(src.at[ex], dst.at[slot], wsem.at[slot])
                for src, dst in ((wg_ref, wgf), (wu_ref, wuf), (wd_ref, wdf))]

    @pl.when(e == 0)
    def _():
        for ex in range(wring - 1):
            for cp in w_copies(ex):
                cp.start()

    b0 = first_ref[e]
    nb = nblk_ref[e]
    total = first_ref[last] + nblk_ref[last]
    ring = xbuf.shape[0]

    def rows(g):
        return pl.ds(pl.multiple_of(g * ROW_BLOCK, ROW_BLOCK), ROW_BLOCK)

    def x_copy(g):
        slot = g % ring
        return pltpu.make_async_copy(xs_ref.at[:, rows(g), :], xbuf.at[slot], xsem.at[slot])

    def y_copy(g):
        slot = g % ring
        return pltpu.make_async_copy(ybuf.at[slot], y_ref.at[:, rows(g), :], ysem.at[slot])

    @pl.when(e == 0)
    def _():
        for g in range(ring - 1):
            @pl.when(g < total)
            def _():
                x_copy(g).start()

    for cp in w_copies(e):
        cp.wait()

    @pl.when(nb > 0)
    def _():
        slot = e % wring
        wgb[...] = wgf[slot].astype(BF16)
        wub[...] = wuf[slot].astype(BF16)
        wdb[...] = wdf[slot].astype(BF16)

    @pl.when(e + wring - 1 <= last)
    def _():
        for cp in w_copies(e + wring - 1):
            cp.start()

    def prefetch(g):
        @pl.when(g + ring - 1 < total)
        def _():
            x_copy(g + ring - 1).start()

    def compute(i):
        slot = (b0 + i) % ring
        words = jnp.concatenate([xbuf[slot, 0], xbuf[slot, 1]], axis=1)
        n_live = cnt_ref[e] - i * ROW_BLOCK
        words = jnp.where(lax.broadcasted_iota(I32, words.shape, 0) < n_live, words, U32(0))
        x = jnp.concatenate(_unpack_rows(words), axis=1).astype(BF16)
        act = (_silu(_dot(x, wgb[...])) * _dot(x, wub[...])).astype(BF16)
        y = _pack_rows(_dot(act, wdb[...]))
        ybuf[slot, 0] = y[:, :HALF]
        ybuf[slot, 1] = y[:, HALF:]

    def blocks(i, n):
        g0 = b0 + i
        prefetch(g0)
        for d in range(n):
            x_copy(g0 + d).wait()

            @pl.when(g0 + d >= ring)
            def _():
                y_copy(g0 + d - ring).wait()

        for d in range(n):
            compute(i + d)
        for d in range(1, n):
            prefetch(g0 + d)
        for d in range(n):
            y_copy(g0 + d).start()

    def pair(ii, carry):
        blocks(2 * ii, 2)
        return carry

    def single(_, carry):
        blocks(nb - 1, 1)
        return carry

    lax.fori_loop(0, nb // 2, pair, 0)
    lax.fori_loop(0, nb % 2, single, 0)

    @pl.when(e == last)
    def _():
        for back in range(ring, 0, -1):
            @pl.when(total >= back)
            def _():
                y_copy(total - back).wait()


def _experts(first_block, n_blocks, counts, xs, wg, wu, wd):
    n_rows = xs.shape[1]
    anyspec = pl.BlockSpec(memory_space=pl.ANY)
    ring = pltpu.VMEM((EXPERT_RING, 2, ROW_BLOCK, HALF), U32)
    grid_spec = pltpu.PrefetchScalarGridSpec(
        num_scalar_prefetch=3, grid=(N_EXPERTS,),
        in_specs=[anyspec] * 4,
        out_specs=anyspec,
        scratch_shapes=[ring, ring,
                        pltpu.VMEM((WEIGHT_RING, D_MODEL, EXPERT_DIM), F32),
                        pltpu.VMEM((WEIGHT_RING, D_MODEL, EXPERT_DIM), F32),
                        pltpu.VMEM((WEIGHT_RING, EXPERT_DIM, D_MODEL), F32),
                        pltpu.VMEM((D_MODEL, EXPERT_DIM), BF16), pltpu.VMEM((D_MODEL, EXPERT_DIM), BF16),
                        pltpu.VMEM((EXPERT_DIM, D_MODEL), BF16),
                        pltpu.SemaphoreType.DMA((EXPERT_RING,)), pltpu.SemaphoreType.DMA((EXPERT_RING,)),
                        pltpu.SemaphoreType.DMA((WEIGHT_RING,))],
    )
    return pl.pallas_call(
        _experts_body, out_shape=jax.ShapeDtypeStruct((2, n_rows, HALF), U32), grid_spec=grid_spec,
        compiler_params=_params(1), name="experts",
    )(first_block, n_blocks, counts, xs, wg, wu, wd)


def _combine_body(*refs):
    ya_refs = refs[:TOP_K]
    yb_refs = refs[TOP_K:2 * TOP_K]
    base_ref, w_ref, gt2_ref, out_ref = refs[2 * TOP_K:]
    w = w_ref[...]
    r_lo = r_hi = None
    for k in range(TOP_K):
        lo, hi = _unpack_rows(jnp.concatenate([ya_refs[k][...], yb_refs[k][...]], axis=1))
        wk = w[:, k:k + 1]
        r_lo = wk * lo if r_lo is None else r_lo + wk * lo
        r_hi = wk * hi if r_hi is None else r_hi + wk * hi
    out_ref[...] = base_ref[...] + gt2_ref[...] * jnp.concatenate([r_lo, r_hi], axis=1)


def _combine(y_slots, base, w_tk, gt2, S):
    T = base.shape[0]
    tt = WIDE_TILE
    nt = T // tt
    per_b = S // tt
    yspec = lambda half, k: pl.BlockSpec((tt, HALF), lambda i: ((half * TOP_K + k) * nt + i, 0))
    return pl.pallas_call(
        _combine_body, out_shape=jax.ShapeDtypeStruct((T, D_MODEL), F32), grid=(nt,),
        in_specs=[yspec(half, k) for half in range(2) for k in range(TOP_K)] + [
            pl.BlockSpec((tt, D_MODEL), lambda i: (i, 0)),
            pl.BlockSpec((tt, TOP_K), lambda i: (i, 0)),
            pl.BlockSpec((None, 1, D_MODEL), lambda i: (i // per_b, 0, 0)),
        ],
        out_specs=pl.BlockSpec((tt, D_MODEL), lambda i: (i, 0)),
        compiler_params=_params(1), name="combine",
    )(*([y_slots] * (2 * TOP_K)), base, w_tk, gt2)


def _pad_heads(w, width):
    kdim = w.shape[0]
    w3 = w.reshape(kdim, MLA_HEADS, width)
    return jnp.pad(w3, ((0, 0), (0, 0), (0, LANES - width))).reshape(kdim, MLA_HEADS * LANES)


def _rot_partner(w_rope):
    half = QK_ROPE // 2
    return jnp.concatenate([-w_rope[..., half:], w_rope[..., :half]], axis=-1)


def _lane_vec(nope, rope):
    return jnp.concatenate([nope, rope, jnp.zeros((LANES - QK_DIM,), F32)]).reshape(1, LANES)


def _layer(x2, ada, positions, B, S, g_norm1, w_in, g_cq, w_uq, g_ckv, w_ukv, g_qn, g_kn, g_attn_out,
           conv_w, conv_b, w_mq, w_mk, b_igate, b_fgate, g_mlstm_out, w_out, g_norm2, w_router, router_bias,
           w_gate_exp, w_up_exp, w_down_exp, w_gate_sh, w_up_sh, w_down_sh):
    T = B * S
    sh1, sc1, gt1, sh2, sc2, gt2 = [a.reshape(B, 1, D_MODEL) for a in jnp.split(ada, 6, axis=-1)]
    row = lambda v: v.reshape(1, -1).astype(F32)

    inv_freq = ROPE_THETA ** (-jnp.arange(0, QK_ROPE, 2, dtype=F32) / QK_ROPE)
    ang = positions.astype(F32).reshape(T, 1) * inv_freq
    half = QK_ROPE // 2
    freq_lane = jnp.arange(half)[:, None]
    lane = jnp.arange(LANES)[None, :]
    place = ((lane == QK_NOPE + freq_lane) | (lane == QK_NOPE + half + freq_lane)).astype(F32)
    spread = lambda tbl: jnp.dot(tbl, place, precision=lax.Precision.HIGHEST)
    cosp = (jnp.arange(LANES) < QK_NOPE).astype(F32)[None, :] + spread(jnp.cos(ang))
    sinp = spread(jnp.sin(ang))

    o0 = 0
    w_cq = w_in[:, o0:o0 + Q_LORA]; o0 += Q_LORA
    w_ckv = w_in[:, o0:o0 + KV_LORA]; o0 += KV_LORA
    w_kr = w_in[:, o0:o0 + QK_ROPE]; o0 += QK_ROPE
    w_z = w_in[:, o0:o0 + MLSTM_INNER]; o0 += MLSTM_INNER
    w_v = w_in[:, o0:o0 + MLSTM_INNER]; o0 += MLSTM_INNER
    w_o = w_in[:, o0:o0 + MLSTM_INNER]; o0 += MLSTM_INNER
    w_i = w_in[:, o0:o0 + MLSTM_HEADS]; o0 += MLSTM_HEADS
    w_f = w_in[:, o0:o0 + MLSTM_HEADS]
    zl = jnp.zeros((D_MODEL, QK_NOPE), F32)
    zr = jnp.zeros((D_MODEL, LANES - QK_DIM), F32)
    wa = jnp.concatenate([w_cq, w_ckv, w_z, w_o, zl, w_kr, zr, zl, _rot_partner(w_kr), zr], axis=1).astype(BF16)
    wt = jnp.concatenate([w_v.T, w_i.T, w_f.T, jnp.zeros((16 - 2 * MLSTM_HEADS, D_MODEL), F32)], axis=0).astype(BF16)

    uq3 = w_uq.reshape(Q_LORA, MLA_HEADS, QK_DIM)
    uq_rot = jnp.concatenate([jnp.zeros((Q_LORA, MLA_HEADS, QK_NOPE), F32), _rot_partner(uq3[..., QK_NOPE:])], axis=-1)
    wuq = _pad_heads(w_uq, QK_DIM).astype(BF16)
    wuqs = _pad_heads(uq_rot.reshape(Q_LORA, MLA_HEADS * QK_DIM), QK_DIM).astype(BF16)
    ukv3 = w_ukv.reshape(KV_LORA, MLA_HEADS, QK_NOPE + V_DIM)
    wuk = _pad_heads(ukv3[..., :QK_NOPE].reshape(KV_LORA, MLA_HEADS * QK_NOPE), QK_NOPE).astype(BF16)
    wuv = _pad_heads(ukv3[..., QK_NOPE:].reshape(KV_LORA, MLA_HEADS * V_DIM), V_DIM).astype(BF16)
    vone = jnp.tile((jnp.arange(LANES) == V_DIM).astype(F32), MLA_HEADS).reshape(1, MLA_HEADS * LANES)
    swap = lambda g_rope: jnp.concatenate([g_rope[half:], g_rope[:half]])
    no_nope = jnp.zeros((QK_NOPE,), F32)
    gq = _lane_vec(g_qn[:QK_NOPE], g_qn[QK_NOPE:])
    gqs = _lane_vec(no_nope, swap(g_qn[QK_NOPE:]))
    gk = _lane_vec(g_kn[:QK_NOPE], g_kn[QK_NOPE:])
    gks = _lane_vec(no_nope, swap(g_kn[QK_NOPE:]))

    z, og, vt3, g3, q, k, mv = _inproj(
        x2, 1.0 + sc1, sh1, row(g_norm1), wa, wt, row(g_cq), wuq, wuqs, row(g_ckv), wuk, wuv, vone,
        gq, gqs, gk, gks, cosp, sinp, S)
    attn_o = _attn(q, k, mv, B, S)
    mlstm_o = _mlstm(z, vt3, g3, og, conv_w.reshape(CONV_WIDTH, MLSTM_INNER), row(conv_b),
                     w_mq.astype(BF16), w_mk.astype(BF16), b_igate, b_fgate, row(g_mlstm_out), B, S)

    wgu = jnp.concatenate([w_gate_sh, w_up_sh], axis=1).astype(BF16)
    base, h2, lgt = _post(attn_o, mlstm_o, x2, gt1, row(g_attn_out), w_out.astype(BF16), row(g_norm2),
                          1.0 + sc2, sh2, gt2, w_router.T.astype(BF16), wgu, w_down_sh.astype(BF16), S)

    idx_t, w_t, rank_t, cnt = _route(lgt, router_bias.reshape(N_EXPERTS, 1))
    counts = cnt[:, 0].astype(I32)
    padded = (counts + ROW_BLOCK - 1) // ROW_BLOCK * ROW_BLOCK
    padded_end = jnp.cumsum(padded)
    off = padded_end - padded
    n_rows = T * TOP_K + N_EXPERTS * ROW_BLOCK
    dest_t = _dest(idx_t, rank_t, off.reshape(N_EXPERTS, 1))

    dest2_t = jnp.concatenate([dest_t, dest_t + n_rows], axis=1)
    xs = _scatter_rows(dest2_t, h2.reshape(2 * T, HALF), 2 * n_rows).reshape(2, n_rows, HALF)
    y = _experts(off // ROW_BLOCK, padded // ROW_BLOCK, counts, xs, w_gate_exp, w_up_exp, w_down_exp)
    dest_row = dest_t.reshape(1, TOP_K * T)
    y_slots = _gather_rows(jnp.concatenate([dest_row, dest_row + n_rows], axis=1), y.reshape(2 * n_rows, HALF))
    return _combine(y_slots, base, w_t.T, gt2, S)


def kernel(x, c, positions, w_ada, b_ada, g_norm1, w_in, g_cq, w_uq, g_ckv, w_ukv, g_qn, g_kn, g_attn_out, conv_w, conv_b, w_mq, w_mk, b_igate, b_fgate, g_mlstm_out, w_out, g_norm2, w_router, router_bias, w_gate_exp, w_up_exp, w_down_exp, w_gate_sh, w_up_sh, w_down_sh):
    B, S, d_model = x.shape
    assert d_model == D_MODEL and S % WIDE_TILE == 0 and S % SEQ_TILE == 0, (B, S, d_model)
    assert w_gate_exp.shape[1:] == (N_EXPERTS, D_MODEL, EXPERT_DIM), w_gate_exp.shape
    depth = w_ada.shape[0]
    x2 = x.reshape(B * S, D_MODEL)
    for l in range(depth):
        ada = _ada(c, w_ada[l], b_ada[l])
        x2 = _layer(x2, ada, positions, B, S, g_norm1[l], w_in[l], g_cq[l], w_uq[l], g_ckv[l], w_ukv[l], g_qn[l],
                    g_kn[l], g_attn_out[l], conv_w[l], conv_b[l], w_mq[l], w_mk[l], b_igate[l], b_fgate[l],
                    g_mlstm_out[l], w_out[l], g_norm2[l], w_router[l], router_bias[l], w_gate_exp[l], w_up_exp[l],
                    w_down_exp[l], w_gate_sh[l], w_up_sh[l], w_down_sh[l])
    return x2.reshape(B, S, D_MODEL)
```

```python
import functools
import math

import jax
import jax.numpy as jnp
from jax import lax
from jax.experimental import pallas as pl
from jax.experimental.pallas import tpu as pltpu
from jax.experimental.pallas import tpu_sc as plsc

F32 = jnp.float32
BF16 = jnp.bfloat16
I32 = jnp.int32

D_MODEL = 1024
MLA_HEADS = 8
QK_NOPE = 64
QK_ROPE = 32
QK_DIM = QK_NOPE + QK_ROPE
V_DIM = 64
Q_LORA = 384
KV_LORA = 256
ROPE_THETA = 10000.0
MLSTM_HEADS = 4
MLSTM_DIM = 128
MLSTM_INNER = MLSTM_HEADS * MLSTM_DIM
CONV_WIDTH = 4
N_EXPERTS = 256
TOP_K = 8
N_GROUPS = 8
TOPK_GROUPS = 4
GROUP_SIZE = N_EXPERTS // N_GROUPS
EXPERT_DIM = 256
SHARED_DIM = 256
ROUTED_SCALE = 2.5
RMS_EPS = 1e-6

LANES = 128
SEQ_TILE = 256
WIDE_TILE = 512
ROW_BLOCK = 256
VMEM_LIMIT = 56 * 1024 * 1024

_OFF_CQ = 0
_OFF_CKV = _OFF_CQ + Q_LORA
_OFF_Z = _OFF_CKV + KV_LORA
_OFF_O = _OFF_Z + MLSTM_INNER
_OFF_KR = _OFF_O + MLSTM_INNER
_OFF_KRS = _OFF_KR + LANES
_WA_COLS = _OFF_KRS + LANES
_WT_ROWS = MLSTM_INNER + 16


def _params(n_axes, vmem=VMEM_LIMIT):
    return pltpu.CompilerParams(dimension_semantics=("arbitrary",) * n_axes, vmem_limit_bytes=vmem)


def _dot(a, b):
    return jnp.dot(a, b, preferred_element_type=F32)


def _dot_nt(a, b):
    return lax.dot_general(a, b, (((1,), (1,)), ((), ())), preferred_element_type=F32)


def _sigmoid(x):
    return 1.0 / (1.0 + jnp.exp(-x))


def _silu(x):
    return x * _sigmoid(x)


U32 = jnp.uint32
PACKED = D_MODEL // 2


def _pack_rows(x):
    w = x.shape[1] // 2
    lo = lax.bitcast_convert_type(x[:, :w].astype(BF16).astype(F32), U32)
    hi = lax.bitcast_convert_type(x[:, w:].astype(BF16).astype(F32), U32)
    return lax.shift_right_logical(lo, U32(16)) | (hi & U32(0xFFFF0000))


def _unpack_rows(p):
    lo = lax.bitcast_convert_type(lax.shift_left(p, U32(16)), F32)
    hi = lax.bitcast_convert_type(p & U32(0xFFFF0000), F32)
    return lo, hi


def _ada_body(c_ref, w_ref, b_ref, o_ref):
    c = c_ref[...]
    cond = _silu(c).astype(BF16)
    o_ref[...] = _dot(cond, w_ref[...].astype(BF16)) + b_ref[...]


def _ada(c, w_ada, b_ada):
    B = c.shape[0]
    n = w_ada.shape[1]
    tn = 1024
    return pl.pallas_call(
        _ada_body,
        out_shape=jax.ShapeDtypeStruct((B, n), F32),
        grid=(n // tn,),
        in_specs=[
            pl.BlockSpec((B, D_MODEL), lambda j: (0, 0)),
            pl.BlockSpec((D_MODEL, tn), lambda j: (0, j)),
            pl.BlockSpec((1, tn), lambda j: (0, j)),
        ],
        out_specs=pl.BlockSpec((B, tn), lambda j: (0, j)),
        compiler_params=_params(1),
        name="ada",
    )(c, w_ada, b_ada.reshape(1, n))


def _inproj_body(x_ref, sc_ref, sh_ref, g1_ref, wa_ref, wt_ref, gcq_ref, wuq_ref, wuqs_ref, gckv_ref,
                 wuk_ref, wuv_ref, vone_ref, gq_ref, gqs_ref, gk_ref, gks_ref, cos_ref, sin_ref,
                 z_ref, o_ref, vt_ref, gt_ref, q_ref, k_ref, mv_ref):
    x = x_ref[...]
    hn = x * lax.rsqrt(jnp.mean(x * x, axis=-1, keepdims=True) + RMS_EPS) * g1_ref[...]
    hm = (hn * sc_ref[...] + sh_ref[...]).astype(BF16)
    p = _dot(hm, wa_ref[...])
    z_ref[...] = p[:, _OFF_Z:_OFF_Z + MLSTM_INNER].astype(BF16)
    o_ref[...] = p[:, _OFF_O:_OFF_O + MLSTM_INNER].astype(BF16)
    for part in range(vt_ref.shape[0]):
        rt = _dot_nt(wt_ref[...], hm[part * SEQ_TILE:(part + 1) * SEQ_TILE])
        vt_ref[part] = rt[:MLSTM_INNER].astype(BF16)
        gt_ref[part] = rt[MLSTM_INNER:]

    cq = p[:, _OFF_CQ:_OFF_CQ + Q_LORA]
    cqn = (cq * lax.rsqrt(jnp.mean(cq * cq, axis=-1, keepdims=True) + RMS_EPS) * gcq_ref[...]).astype(BF16)
    ckv = p[:, _OFF_CKV:_OFF_CKV + KV_LORA]
    ckvn = (ckv * lax.rsqrt(jnp.mean(ckv * ckv, axis=-1, keepdims=True) + RMS_EPS) * gckv_ref[...]).astype(BF16)
    qa = _dot(cqn, wuq_ref[...])
    qb = _dot(cqn, wuqs_ref[...])
    ka = _dot(ckvn, wuk_ref[...])
    mv_ref[...] = (_dot(ckvn, wuv_ref[...]) + vone_ref[...]).astype(BF16)
    kr = p[:, _OFF_KR:_OFF_KR + LANES]
    krs = p[:, _OFF_KRS:_OFF_KRS + LANES]

    cosp = cos_ref[...]
    sinp = sin_ref[...]
    gqc = gq_ref[...] * cosp
    gqs = gqs_ref[...] * sinp
    gkc = gk_ref[...] * cosp
    k_rot = krs * (gks_ref[...] * sinp)
    kr_ss = jnp.sum(kr * kr, axis=-1, keepdims=True)
    q_scale = (QK_DIM ** -0.5) * math.log2(math.e)
    for h in range(MLA_HEADS):
        sl = slice(h * LANES, (h + 1) * LANES)
        qah = qa[:, sl]
        rq = lax.rsqrt(jnp.sum(qah * qah, axis=-1, keepdims=True) * (1.0 / QK_DIM) + RMS_EPS) * q_scale
        q_ref[:, sl] = ((qah * gqc + qb[:, sl] * gqs) * rq).astype(BF16)
        kah = ka[:, sl]
        rk = lax.rsqrt((jnp.sum(kah * kah, axis=-1, keepdims=True) + kr_ss) * (1.0 / QK_DIM) + RMS_EPS)
        k_ref[:, sl] = (((kah + kr) * gkc + k_rot) * rk).astype(BF16)


def _inproj(x2, sc1p, sh1, g1, wa, wt, gcq, wuq, wuqs, gckv, wuk, wuv, vone, gq, gqs, gk, gks, cosp, sinp, S):
    T = x2.shape[0]
    tm = WIDE_TILE
    nt = T // tm
    per_b = S // tm
    sub = tm // SEQ_TILE
    hp = MLA_HEADS * LANES
    full = lambda a: pl.BlockSpec(a.shape, lambda i: (0,) * a.ndim)
    row = lambda w: pl.BlockSpec((tm, w), lambda i: (i, 0))
    bmod = pl.BlockSpec((None, 1, D_MODEL), lambda i: (i // per_b, 0, 0))
    out_shapes = (
        jax.ShapeDtypeStruct((T, MLSTM_INNER), BF16),
        jax.ShapeDtypeStruct((T, MLSTM_INNER), BF16),
        jax.ShapeDtypeStruct((T // SEQ_TILE, MLSTM_INNER, SEQ_TILE), BF16),
        jax.ShapeDtypeStruct((T // SEQ_TILE, 16, SEQ_TILE), F32),
        jax.ShapeDtypeStruct((T, hp), BF16),
        jax.ShapeDtypeStruct((T, hp), BF16),
        jax.ShapeDtypeStruct((T, hp), BF16),
    )
    out_specs = (
        row(MLSTM_INNER), row(MLSTM_INNER),
        pl.BlockSpec((sub, MLSTM_INNER, SEQ_TILE), lambda i: (i, 0, 0)),
        pl.BlockSpec((sub, 16, SEQ_TILE), lambda i: (i, 0, 0)),
        row(hp), row(hp), row(hp),
    )
    in_specs = [row(D_MODEL), bmod, bmod, full(g1), full(wa), full(wt), full(gcq), full(wuq), full(wuqs),
                full(gckv), full(wuk), full(wuv), full(vone), full(gq), full(gqs), full(gk), full(gks),
                row(LANES), row(LANES)]
    return pl.pallas_call(
        _inproj_body, out_shape=out_shapes, grid=(nt,), in_specs=in_specs, out_specs=out_specs,
        compiler_params=_params(1), name="inproj",
    )(x2, sc1p, sh1, g1, wa, wt, gcq, wuq, wuqs, gckv, wuk, wuv, vone, gq, gqs, gk, gks, cosp, sinp)


ATTN_HEADS_PER_STEP = 8


def _pairs_loop(n, fn, init):
    c = lax.fori_loop(0, n // 2, lambda jj, c: fn(2 * jj + 1, fn(2 * jj, c)), init)
    return lax.fori_loop(0, n % 2, lambda _, c: fn(n - 1, c), c)


def _attn_body(q_ref, k_ref, v_ref, o_ref, s_scr):
    i = pl.program_id(2)
    t = SEQ_TILE
    nh = ATTN_HEADS_PER_STEP
    qry_pos = lax.broadcasted_iota(I32, (t, t), 0)
    key_pos = lax.broadcasted_iota(I32, (t, t), 1)
    causal = key_pos <= qry_pos
    qs = [q_ref[:, hh * LANES:(hh + 1) * LANES] for hh in range(nh)]

    def scores(j, hh, diag):
        kj = k_ref[pl.ds(pl.multiple_of(j * t, t), t), hh * LANES:(hh + 1) * LANES]
        s = _dot_nt(qs[hh], kj)
        return jnp.where(causal, s, -jnp.inf) if diag else s

    def halves(x):
        return x[:, :LANES], x[:, LANES:]

    def max_pass(j, mx, diag):
        out = []
        for hh in range(nh):
            s = scores(j, hh, diag)
            s_scr[j, hh] = s
            lo, hi = halves(s)
            out.append(jnp.maximum(mx[hh], jnp.maximum(lo, hi)))
        return tuple(out)

    mx = tuple(jnp.full((t, LANES), -jnp.inf, F32) for _ in range(nh))
    mx = _pairs_loop(i, lambda j, c: max_pass(j, c, False), mx)
    mx = max_pass(i, mx, True)
    mb = [jnp.broadcast_to(jnp.max(mx[hh], axis=1, keepdims=True), (t, LANES)) for hh in range(nh)]

    def acc_pass(j, carry, diag):
        row0 = pl.multiple_of(j * t, t)
        out = []
        for hh in range(nh):
            lo, hi = halves(s_scr[j, hh])
            p = jnp.concatenate([jnp.exp2(lo - mb[hh]), jnp.exp2(hi - mb[hh])], axis=1).astype(BF16)
            out.append(carry[hh] + _dot(p, v_ref[pl.ds(row0, t), hh * LANES:(hh + 1) * LANES]))
        return tuple(out)

    init = tuple(jnp.zeros((t, LANES), F32) for _ in range(nh))
    carry = _pairs_loop(i, lambda j, c: acc_pass(j, c, False), init)
    final = acc_pass(i, carry, True)
    outs = [acc / acc[:, V_DIM:V_DIM + 1] for acc in final]
    first_head = lax.broadcasted_iota(I32, (t, LANES), 1) < V_DIM
    o_ref[...] = jnp.concatenate(
        [jnp.where(first_head, outs[2 * pr], pltpu.roll(outs[2 * pr + 1], V_DIM, axis=1)) for pr in range(nh // 2)],
        axis=1).astype(BF16)


def _attn(q, k, mvt, B, S):
    T = q.shape[0]
    t = SEQ_TILE
    nh = ATTN_HEADS_PER_STEP
    per_b = S // t
    return pl.pallas_call(
        _attn_body,
        out_shape=jax.ShapeDtypeStruct((T, MLA_HEADS * V_DIM), BF16),
        grid=(B, MLA_HEADS // nh, per_b),
        in_specs=[
            pl.BlockSpec((t, nh * LANES), lambda b, hp, i: (b * per_b + i, hp)),
            pl.BlockSpec((S, nh * LANES), lambda b, hp, i: (b, hp)),
            pl.BlockSpec((S, nh * LANES), lambda b, hp, i: (b, hp)),
        ],
        out_specs=pl.BlockSpec((t, nh * V_DIM), lambda b, hp, i: (b * per_b + i, hp)),
        scratch_shapes=[pltpu.VMEM((per_b, nh, t, t), F32)],
        compiler_params=_params(3), name="attn",
    )(q, k, mvt)


def _split3(a):
    hi = a.astype(BF16)
    r = a - hi.astype(F32)
    mid = r.astype(BF16)
    lo = (r - mid.astype(F32)).astype(BF16)
    return hi, mid, lo


def _mlstm_body(bi_ref, bf_ref, z_ref, vt_ref, g_ref, og_ref, cw_ref, cb_ref, wq_ref, wk_ref, gout_ref,
                out_ref, zpad, q_s, k_s, st_s, m_s, *, S):
    h = pl.program_id(1)
    L = SEQ_TILE
    nchunk = S // L
    hd = MLSTM_DIM
    zpad[0:8, :] = jnp.zeros((8, hd), F32)
    zpad[8:, :] = z_ref[...].astype(F32)
    zc = cb_ref[...] + jnp.zeros((S, hd), F32)
    for j in range(CONV_WIDTH):
        zc = zc + cw_ref[j:j + 1, :] * zpad[pl.ds(8 - (CONV_WIDTH - 1) + j, S), :]
    zs = _silu(zc).astype(BF16)
    q_s[...] = _dot(zs, wq_ref[...]).astype(BF16)
    k_s[...] = (_dot(zs, wk_ref[...]) * (hd ** -0.5)).astype(BF16)

    st_s[...] = jnp.zeros_like(st_s)
    m_s[...] = jnp.zeros_like(m_s)
    b_i = bi_ref[h]
    b_f = bf_ref[h]
    r_i = lax.broadcasted_iota(I32, (L, L), 0)
    c_i = lax.broadcasted_iota(I32, (L, L), 1)
    tril = r_i >= c_i
    strict_lower = jnp.where(r_i > c_i, 1.0, 0.0).astype(BF16)
    ones_row = jnp.where(lax.broadcasted_iota(I32, (hd, L), 0) == 0, 1.0, 0.0).astype(BF16)

    def chunk(c, _):
        start = pl.multiple_of(c * L, L)
        qc = q_s[pl.ds(start, L), :]
        kc = k_s[pl.ds(start, L), :]
        vt_aug = jnp.concatenate([vt_ref[c], ones_row], axis=0)
        li = g_ref[c, pl.ds(h, 1), :] + b_i
        fp = g_ref[c, pl.ds(MLSTM_HEADS + h, 1), :] + b_f
        lf = jnp.minimum(fp, 0.0) - jnp.log(1.0 + jnp.exp(-jnp.abs(fp)))
        a_mat = jnp.where(tril, lf, 0.0)
        a_hi, a_mid, a_lo = _split3(a_mat)
        d0 = _dot(a_hi, strict_lower) + _dot(a_mid, strict_lower) + _dot(a_lo, strict_lower)
        dmat = jnp.where(tril, d0 + li, -jnp.inf)
        b_col = jnp.sum(a_mat, axis=1, keepdims=True)
        m_prev = m_s[...]
        inter = b_col + m_prev
        mt = jnp.maximum(inter, jnp.max(dmat, axis=1, keepdims=True))
        w = jnp.exp(dmat - mt) * _dot_nt(qc, kc)
        decay = jnp.exp(inter - mt)
        st = st_s[...]
        num_aug = decay * _dot_nt(qc, st.astype(BF16)) + _dot_nt(w.astype(BF16), vt_aug)
        num = num_aug[:, :hd]
        den = num_aug[:, hd:hd + 1]
        hv = num / jnp.maximum(jnp.abs(den), jnp.exp(-mt))
        g = d0[L - 1:L, :] + li
        b_last = b_col[L - 1:L, :]
        m_new = jnp.maximum(b_last + m_prev, jnp.max(g, axis=1, keepdims=True))
        a = jnp.exp(g - m_new)
        cd = jnp.exp(b_last + m_prev - m_new)
        st_s[...] = cd * st + _dot((vt_aug.astype(F32) * a).astype(BF16), kc)
        m_s[...] = m_new
        hn = hv * lax.rsqrt(jnp.mean(hv * hv, axis=-1, keepdims=True) + RMS_EPS) * gout_ref[...]
        og = og_ref[pl.ds(start, L), :].astype(F32)
        out_ref[pl.ds(start, L), :] = (_sigmoid(og) * hn).astype(BF16)
        return 0

    lax.fori_loop(0, nchunk, chunk, 0)


def _mlstm(z, vt3, g3, og, conv_w, conv_b, wq, wk, b_i, b_f, g_out, B, S):
    T = z.shape[0]
    L = SEQ_TILE
    per_b = S // L
    hd = MLSTM_DIM
    smem = pl.BlockSpec(memory_space=pltpu.SMEM)
    seq = pl.BlockSpec((S, hd), lambda b, h: (b, h))
    return pl.pallas_call(
        functools.partial(_mlstm_body, S=S),
        out_shape=jax.ShapeDtypeStruct((T, MLSTM_INNER), BF16),
        grid=(B, MLSTM_HEADS),
        in_specs=[
            smem, smem, seq,
            pl.BlockSpec((per_b, hd, L), lambda b, h: (b, h, 0)),
            pl.BlockSpec((per_b, 16, L), lambda b, h: (b, 0, 0)),
            seq,
            pl.BlockSpec((CONV_WIDTH, hd), lambda b, h: (0, h)),
            pl.BlockSpec((1, hd), lambda b, h: (0, h)),
            pl.BlockSpec((None, hd, hd), lambda b, h: (h, 0, 0)),
            pl.BlockSpec((None, hd, hd), lambda b, h: (h, 0, 0)),
            pl.BlockSpec((1, hd), lambda b, h: (0, h)),
        ],
        out_specs=seq,
        scratch_shapes=[
            pltpu.VMEM((S + 8, hd), F32),
            pltpu.VMEM((S, hd), BF16),
            pltpu.VMEM((S, hd), BF16),
            pltpu.VMEM((2 * hd, hd), F32),
            pltpu.VMEM((1, 1), F32),
        ],
        compiler_params=_params(2), name="mlstm",
    )(b_i, b_f, z, vt3, g3, og, conv_w, conv_b, wq, wk, g_out)


def _post_body(ao_ref, mo_ref, x_ref, gt1_ref, gattn_ref, wout_ref, g2_ref, sc2_ref, sh2_ref, gt2_ref,
               wrt_ref, wgu_ref, wds_ref, base_ref, h2_ref, lg_ref):
    ao = ao_ref[...].astype(F32)
    aon = ao * lax.rsqrt(jnp.mean(ao * ao, axis=-1, keepdims=True) + RMS_EPS) * gattn_ref[...]
    mix_in = jnp.concatenate([aon.astype(BF16), mo_ref[...]], axis=1)
    x1 = x_ref[...] + gt1_ref[...] * _dot(mix_in, wout_ref[...])
    hn = x1 * lax.rsqrt(jnp.mean(x1 * x1, axis=-1, keepdims=True) + RMS_EPS) * g2_ref[...]
    h2 = hn * sc2_ref[...] + sh2_ref[...]
    words = _pack_rows(h2)
    h2_ref[0] = words[:, :HALF]
    h2_ref[1] = words[:, HALF:]
    h2b = h2.astype(BF16)
    gu = _dot(h2b, wgu_ref[...])
    act = (_silu(gu[:, :SHARED_DIM]) * gu[:, SHARED_DIM:]).astype(BF16)
    base_ref[...] = x1 + gt2_ref[...] * _dot(act, wds_ref[...])
    lg_ref[...] = _dot_nt(wrt_ref[...], h2b)


def _post(ao, mo, x2, gt1, gattn, wout, g2, sc2p, sh2, gt2, wrt, wgu, wds, S):
    T = x2.shape[0]
    tm = WIDE_TILE
    per_b = S // tm
    full = lambda a: pl.BlockSpec(a.shape, lambda i: (0,) * a.ndim)
    row = lambda w: pl.BlockSpec((tm, w), lambda i: (i, 0))
    bmod = pl.BlockSpec((None, 1, D_MODEL), lambda i: (i // per_b, 0, 0))
    return pl.pallas_call(
        _post_body,
        out_shape=(jax.ShapeDtypeStruct((T, D_MODEL), F32), jax.ShapeDtypeStruct((2, T, HALF), U32),
                   jax.ShapeDtypeStruct((N_EXPERTS, T), F32)),
        grid=(T // tm,),
        in_specs=[row(MLA_HEADS * V_DIM), row(MLSTM_INNER), row(D_MODEL), bmod, full(gattn), full(wout),
                  full(g2), bmod, bmod, bmod, full(wrt), full(wgu), full(wds)],
        out_specs=(row(D_MODEL), pl.BlockSpec((2, tm, HALF), lambda i: (0, i, 0)),
                   pl.BlockSpec((N_EXPERTS, tm), lambda i: (0, i))),
        compiler_params=_params(1), name="post",
    )(ao, mo, x2, gt1, gattn, wout, g2, sc2p, sh2, gt2, wrt, wgu, wds)


def _first_max(x, rows, n):
    mx = jnp.max(x, axis=0, keepdims=True)
    idx = jnp.min(jnp.where(x == mx, rows, n), axis=0, keepdims=True)
    return mx, idx


def _route_body(lg_ref, bias_ref, idx_ref, w_ref, rank_ref, cnt_ref, carry):
    i = pl.program_id(0)
    tt = lg_ref.shape[1]

    @pl.when(i == 0)
    def _():
        carry[...] = jnp.zeros_like(carry)

    sc = _sigmoid(lg_ref[...])
    bi = sc + bias_ref[...]
    rows = lax.broadcasted_iota(I32, (N_EXPERTS, tt), 0)
    rows_g = lax.broadcasted_iota(I32, (GROUP_SIZE, tt), 0)
    rows_n = lax.broadcasted_iota(I32, (N_GROUPS, tt), 0)
    neg = -jnp.inf
    gs = []
    for g in range(N_GROUPS):
        xg = bi[g * GROUP_SIZE:(g + 1) * GROUP_SIZE]
        m1, i1 = _first_max(xg, rows_g, GROUP_SIZE)
        m2 = jnp.max(jnp.where(rows_g == i1, neg, xg), axis=0, keepdims=True)
        gs.append(m1 + m2)
    cur = jnp.concatenate(gs, axis=0)
    keep = jnp.zeros((N_GROUPS, tt), F32)
    for _ in range(TOPK_GROUPS):
        _, gi = _first_max(cur, rows_n, N_GROUPS)
        sel = rows_n == gi
        keep = jnp.where(sel, 1.0, keep)
        cur = jnp.where(sel, neg, cur)
    masked = jnp.concatenate(
        [jnp.where(keep[g:g + 1] > 0.0, bi[g * GROUP_SIZE:(g + 1) * GROUP_SIZE], neg) for g in range(N_GROUPS)], axis=0)
    cur = masked
    onehot = jnp.zeros((N_EXPERTS, tt), F32)
    idxs, ws = [], []
    for _ in range(TOP_K):
        _, ei = _first_max(cur, rows, N_EXPERTS)
        sel = rows == ei
        ws.append(jnp.sum(jnp.where(sel, sc, 0.0), axis=0, keepdims=True))
        idxs.append(ei)
        cur = jnp.where(sel, neg, cur)
        onehot = jnp.where(sel, 1.0, onehot)
    wsum = ws[0]
    for k in range(1, TOP_K):
        wsum = wsum + ws[k]
    idx_ref[...] = jnp.concatenate(idxs, axis=0)
    w_ref[...] = jnp.concatenate([w / wsum * ROUTED_SCALE for w in ws], axis=0)
    r_i = lax.broadcasted_iota(I32, (tt, tt), 0)
    c_i = lax.broadcasted_iota(I32, (tt, tt), 1)
    before = jnp.where(r_i < c_i, 1.0, 0.0).astype(BF16)
    tot = _dot(onehot.astype(BF16), before) + carry[...]
    rank_ref[...] = jnp.concatenate(
        [jnp.sum(jnp.where(rows == ei, tot, 0.0), axis=0, keepdims=True) for ei in idxs], axis=0).astype(I32)
    carry[...] = carry[...] + jnp.sum(onehot, axis=1, keepdims=True)
    cnt_ref[...] = jnp.broadcast_to(carry[...], cnt_ref.shape)


def _route(lgt, bias_col):
    T = lgt.shape[1]
    tt = WIDE_TILE
    blk = pl.BlockSpec((TOP_K, tt), lambda i: (0, i))
    return pl.pallas_call(
        _route_body,
        out_shape=(jax.ShapeDtypeStruct((TOP_K, T), I32), jax.ShapeDtypeStruct((TOP_K, T), F32),
                   jax.ShapeDtypeStruct((TOP_K, T), I32), jax.ShapeDtypeStruct((N_EXPERTS, LANES), F32)),
        grid=(T // tt,),
        in_specs=[pl.BlockSpec((N_EXPERTS, tt), lambda i: (0, i)), pl.BlockSpec((N_EXPERTS, 1), lambda i: (0, 0))],
        out_specs=(blk, blk, blk, pl.BlockSpec((N_EXPERTS, LANES), lambda i: (0, 0))),
        scratch_shapes=[pltpu.VMEM((N_EXPERTS, 1), F32)],
        compiler_params=_params(1), name="route",
    )(lgt, bias_col)


def _dest_body(idx_ref, rank_ref, off_ref, dest_ref):
    tt = idx_ref.shape[1]
    rows = lax.broadcasted_iota(I32, (N_EXPERTS, tt), 0)
    off = off_ref[...]
    outs = []
    for k in range(TOP_K):
        sel = rows == idx_ref[k:k + 1, :]
        outs.append(jnp.sum(jnp.where(sel, off, 0), axis=0, keepdims=True) + rank_ref[k:k + 1, :])
    dest_ref[...] = jnp.concatenate(outs, axis=0)


def _dest(idx_t, rank_t, off_col):
    T = idx_t.shape[1]
    tt = min(4 * SEQ_TILE, T)
    blk = pl.BlockSpec((TOP_K, tt), lambda i: (0, i))
    return pl.pallas_call(
        _dest_body, out_shape=jax.ShapeDtypeStruct((TOP_K, T), I32), grid=(T // tt,),
        in_specs=[blk, blk, pl.BlockSpec((N_EXPERTS, 1), lambda i: (0, 0))], out_specs=blk,
        compiler_params=_params(1), name="dest",
    )(idx_t, rank_t, off_col)


SC_WINDOW = 128
HALF = PACKED // 2


def _sc_mesh():
    return plsc.VectorSubcoreMesh(core_axis_name="c", subcore_axis_name="s")


def _scatter_rows(dest2_t, rows, n_out):
    n = rows.shape[0]

    @pl.kernel(out_type=jax.ShapeDtypeStruct((n_out, HALF), U32), mesh=_sc_mesh(), scratch_types=[])
    def scatter(d_hbm, x_hbm, o_hbm):
        def body(d_vmem, x_vmem):
            for k in range(TOP_K):
                pltpu.sync_copy(x_vmem, o_hbm.at[d_vmem.at[k]])

        pltpu.emit_pipeline(
            body, grid=(n // SC_WINDOW,),
            in_specs=[pl.BlockSpec((TOP_K, SC_WINDOW), lambda i: (0, i)),
                      pl.BlockSpec((SC_WINDOW, HALF), lambda i: (i, 0))],
            out_specs=[], core_axis_name=("c", "s"), dimension_semantics=(pltpu.PARALLEL,),
        )(d_hbm, x_hbm)

    return scatter(dest2_t, rows)


def _gather_rows(index_row, rows):
    n = index_row.shape[1]

    @pl.kernel(out_type=jax.ShapeDtypeStruct((n, HALF), U32), mesh=_sc_mesh(), scratch_types=[])
    def gather(i_hbm, y_hbm, o_hbm):
        def body(i_vmem, o_vmem):
            pltpu.sync_copy(y_hbm.at[i_vmem.at[0]], o_vmem)

        pltpu.emit_pipeline(
            body, grid=(n // SC_WINDOW,),
            in_specs=[pl.BlockSpec((1, SC_WINDOW), lambda i: (0, i))],
            out_specs=[pl.BlockSpec((SC_WINDOW, HALF), lambda i: (i, 0))],
            core_axis_name=("c", "s"), dimension_semantics=(pltpu.PARALLEL,),
        )(i_hbm, o_hbm)

    return gather(index_row, rows)


EXPERT_RING = 16
WEIGHT_RING = 3


def _experts_body(first_ref, nblk_ref, cnt_ref, xs_ref, wg_ref, wu_ref, wd_ref, y_ref,
                  xbuf, ybuf, wgf, wuf, wdf, wgb, wub, wdb, xsem, ysem, wsem):
    e = pl.program_id(0)
    last = pl.num_programs(0) - 1
    wring = wgf.shape[0]

    def w_copies(ex):
        slot = ex % wring
        return [pltpu.make_async_copy(src.at[ex], dst.at[slot], wsem.at[slot])
                for src, dst in ((wg_ref, wgf), (wu_ref, wuf), (wd_ref, wdf))]

    @pl.when(e == 0)
    def _():
        for ex in range(wring - 1):
            for cp in w_copies(ex):
                cp.start()

    b0 = first_ref[e]
    nb = nblk_ref[e]
    total = first_ref[last] + nblk_ref[last]
    ring = xbuf.shape[0]

    def rows(g):
        return pl.ds(pl.multiple_of(g * ROW_BLOCK, ROW_BLOCK), ROW_BLOCK)

    def x_copy(g):
        slot = g % ring
        return pltpu.make_async_copy(xs_ref.at[:, rows(g), :], xbuf.at[slot], xsem.at[slot])

    def y_copy(g):
        slot = g % ring
        return pltpu.make_async_copy(ybuf.at[slot], y_ref.at[:, rows(g), :], ysem.at[slot])

    @pl.when(e == 0)
    def _():
        for g in range(ring - 1):
            @pl.when(g < total)
            def _():
                x_copy(g).start()

    for cp in w_copies(e):
        cp.wait()

    @pl.when(nb > 0)
    def _():
        slot = e % wring
        wgb[...] = wgf[slot].astype(BF16)
        wub[...] = wuf[slot].astype(BF16)
        wdb[...] = wdf[slot].astype(BF16)

    @pl.when(e + wring - 1 <= last)
    def _():
        for cp in w_copies(e + wring - 1):
            cp.start()

    def prefetch(g):
        @pl.when(g + ring - 1 < total)
        def _():
            x_copy(g + ring - 1).start()

    def compute(i):
        slot = (b0 + i) % ring
        words = jnp.concatenate([xbuf[slot, 0], xbuf[slot, 1]], axis=1)
        n_live = cnt_ref[e] - i * ROW_BLOCK
        words = jnp.where(lax.broadcasted_iota(I32, words.shape, 0) < n_live, words, U32(0))
        x = jnp.concatenate(_unpack_rows(words), axis=1).astype(BF16)
        act = (_silu(_dot(x, wgb[...])) * _dot(x, wub[...])).astype(BF16)
        y = _pack_rows(_dot(act, wdb[...]))
        ybuf[slot, 0] = y[:, :HALF]
        ybuf[slot, 1] = y[:, HALF:]

    def blocks(i, n):
        g0 = b0 + i
        prefetch(g0)
        for d in range(n):
            x_copy(g0 + d).wait()

            @pl.when(g0 + d >= ring)
            def _():
                y_copy(g0 + d - ring).wait()

        for d in range(n):
            compute(i + d)
        for d in range(1, n):
            prefetch(g0 + d)
        for d in range(n):
            y_copy(g0 + d).start()

    def pair(ii, carry):
        blocks(2 * ii, 2)
        return carry

    def single(_, carry):
        blocks(nb - 1, 1)
        return carry

    lax.fori_loop(0, nb // 2, pair, 0)
    lax.fori_loop(0, nb % 2, single, 0)

    @pl.when(e == last)
    def _():
        for back in range(ring, 0, -1):
            @pl.when(total >= back)
            def _():
                y_copy(total - back).wait()


def _experts(first_block, n_blocks, counts, xs, wg, wu, wd):
    n_rows = xs.shape[1]
    anyspec = pl.BlockSpec(memory_space=pl.ANY)
    ring = pltpu.VMEM((EXPERT_RING, 2, ROW_BLOCK, HALF), U32)
    grid_spec = pltpu.PrefetchScalarGridSpec(
        num_scalar_prefetch=3, grid=(N_EXPERTS,),
        in_specs=[anyspec] * 4,
        out_specs=anyspec,
        scratch_shapes=[ring, ring,
                        pltpu.VMEM((WEIGHT_RING, D_MODEL, EXPERT_DIM), F32),
                        pltpu.VMEM((WEIGHT_RING, D_MODEL, EXPERT_DIM), F32),
                        pltpu.VMEM((WEIGHT_RING, EXPERT_DIM, D_MODEL), F32),
                        pltpu.VMEM((D_MODEL, EXPERT_DIM), BF16), pltpu.VMEM((D_MODEL, EXPERT_DIM), BF16),
                        pltpu.VMEM((EXPERT_DIM, D_MODEL), BF16),
                        pltpu.SemaphoreType.DMA((EXPERT_RING,)), pltpu.SemaphoreType.DMA((EXPERT_RING,)),
                        pltpu.SemaphoreType.DMA((WEIGHT_RING,))],
    )
    return pl.pallas_call(
        _experts_body, out_shape=jax.ShapeDtypeStruct((2, n_rows, HALF), U32), grid_spec=grid_spec,
        compiler_params=_params(1), name="experts",
    )(first_block, n_blocks, counts, xs, wg, wu, wd)


def _combine_body(*refs):
    ya_refs = refs[:TOP_K]
    yb_refs = refs[TOP_K:2 * TOP_K]
    base_ref, w_ref, gt2_ref, out_ref = refs[2 * TOP_K:]
    w = w_ref[...]
    r_lo = r_hi = None
    for k in range(TOP_K):
        lo, hi = _unpack_rows(jnp.concatenate([ya_refs[k][...], yb_refs[k][...]], axis=1))
        wk = w[:, k:k + 1]
        r_lo = wk * lo if r_lo is None else r_lo + wk * lo
        r_hi = wk * hi if r_hi is None else r_hi + wk * hi
    out_ref[...] = base_ref[...] + gt2_ref[...] * jnp.concatenate([r_lo, r_hi], axis=1)


def _combine(y_slots, base, w_tk, gt2, S):
    T = base.shape[0]
    tt = WIDE_TILE
    nt = T // tt
    per_b = S // tt
    yspec = lambda half, k: pl.BlockSpec((tt, HALF), lambda i: ((half * TOP_K + k) * nt + i, 0))
    return pl.pallas_call(
        _combine_body, out_shape=jax.ShapeDtypeStruct((T, D_MODEL), F32), grid=(nt,),
        in_specs=[yspec(half, k) for half in range(2) for k in range(TOP_K)] + [
            pl.BlockSpec((tt, D_MODEL), lambda i: (i, 0)),
            pl.BlockSpec((tt, TOP_K), lambda i: (i, 0)),
            pl.BlockSpec((None, 1, D_MODEL), lambda i: (i // per_b, 0, 0)),
        ],
        out_specs=pl.BlockSpec((tt, D_MODEL), lambda i: (i, 0)),
        compiler_params=_params(1), name="combine",
    )(*([y_slots] * (2 * TOP_K)), base, w_tk, gt2)


def _pad_heads(w, width):
    kdim = w.shape[0]
    w3 = w.reshape(kdim, MLA_HEADS, width)
    return jnp.pad(w3, ((0, 0), (0, 0), (0, LANES - width))).reshape(kdim, MLA_HEADS * LANES)


def _rot_partner(w_rope):
    half = QK_ROPE // 2
    return jnp.concatenate([-w_rope[..., half:], w_rope[..., :half]], axis=-1)


def _lane_vec(nope, rope):
    return jnp.concatenate([nope, rope, jnp.zeros((LANES - QK_DIM,), F32)]).reshape(1, LANES)


def _layer(x2, ada, positions, B, S, g_norm1, w_in, g_cq, w_uq, g_ckv, w_ukv, g_qn, g_kn, g_attn_out,
           conv_w, conv_b, w_mq, w_mk, b_igate, b_fgate, g_mlstm_out, w_out, g_norm2, w_router, router_bias,
           w_gate_exp, w_up_exp, w_down_exp, w_gate_sh, w_up_sh, w_down_sh):
    T = B * S
    sh1, sc1, gt1, sh2, sc2, gt2 = [a.reshape(B, 1, D_MODEL) for a in jnp.split(ada, 6, axis=-1)]
    row = lambda v: v.reshape(1, -1).astype(F32)

    inv_freq = ROPE_THETA ** (-jnp.arange(0, QK_ROPE, 2, dtype=F32) / QK_ROPE)
    ang = positions.astype(F32).reshape(T, 1) * inv_freq
    half = QK_ROPE // 2
    freq_lane = jnp.arange(half)[:, None]
    lane = jnp.arange(LANES)[None, :]
    place = ((lane == QK_NOPE + freq_lane) | (lane == QK_NOPE + half + freq_lane)).astype(F32)
    spread = lambda tbl: jnp.dot(tbl, place, precision=lax.Precision.HIGHEST)
    cosp = (jnp.arange(LANES) < QK_NOPE).astype(F32)[None, :] + spread(jnp.cos(ang))
    sinp = spread(jnp.sin(ang))

    o0 = 0
    w_cq = w_in[:, o0:o0 + Q_LORA]; o0 += Q_LORA
    w_ckv = w_in[:, o0:o0 + KV_LORA]; o0 += KV_LORA
    w_kr = w_in[:, o0:o0 + QK_ROPE]; o0 += QK_ROPE
    w_z = w_in[:, o0:o0 + MLSTM_INNER]; o0 += MLSTM_INNER
    w_v = w_in[:, o0:o0 + MLSTM_INNER]; o0 += MLSTM_INNER
    w_o = w_in[:, o0:o0 + MLSTM_INNER]; o0 += MLSTM_INNER
    w_i = w_in[:, o0:o0 + MLSTM_HEADS]; o0 += MLSTM_HEADS
    w_f = w_in[:, o0:o0 + MLSTM_HEADS]
    zl = jnp.zeros((D_MODEL, QK_NOPE), F32)
    zr = jnp.zeros((D_MODEL, LANES - QK_DIM), F32)
    wa = jnp.concatenate([w_cq, w_ckv, w_z, w_o, zl, w_kr, zr, zl, _rot_partner(w_kr), zr], axis=1).astype(BF16)
    wt = jnp.concatenate([w_v.T, w_i.T, w_f.T, jnp.zeros((16 - 2 * MLSTM_HEADS, D_MODEL), F32)], axis=0).astype(BF16)

    uq3 = w_uq.reshape(Q_LORA, MLA_HEADS, QK_DIM)
    uq_rot = jnp.concatenate([jnp.zeros((Q_LORA, MLA_HEADS, QK_NOPE), F32), _rot_partner(uq3[..., QK_NOPE:])], axis=-1)
    wuq = _pad_heads(w_uq, QK_DIM).astype(BF16)
    wuqs = _pad_heads(uq_rot.reshape(Q_LORA, MLA_HEADS * QK_DIM), QK_DIM).astype(BF16)
    ukv3 = w_ukv.reshape(KV_LORA, MLA_HEADS, QK_NOPE + V_DIM)
    wuk = _pad_heads(ukv3[..., :QK_NOPE].reshape(KV_LORA, MLA_HEADS * QK_NOPE), QK_NOPE).astype(BF16)
    wuv = _pad_heads(ukv3[..., QK_NOPE:].reshape(KV_LORA, MLA_HEADS * V_DIM), V_DIM).astype(BF16)
    vone = jnp.tile((jnp.arange(LANES) == V_DIM).astype(F32), MLA_HEADS).reshape(1, MLA_HEADS * LANES)
    swap = lambda g_rope: jnp.concatenate([g_rope[half:], g_rope[:half]])
    no_nope = jnp.zeros((QK_NOPE,), F32)
    gq = _lane_vec(g_qn[:QK_NOPE], g_qn[QK_NOPE:])
    gqs = _lane_vec(no_nope, swap(g_qn[QK_NOPE:]))
    gk = _lane_vec(g_kn[:QK_NOPE], g_kn[QK_NOPE:])
    gks = _lane_vec(no_nope, swap(g_kn[QK_NOPE:]))

    z, og, vt3, g3, q, k, mv = _inproj(
        x2, 1.0 + sc1, sh1, row(g_norm1), wa, wt, row(g_cq), wuq, wuqs, row(g_ckv), wuk, wuv, vone,
        gq, gqs, gk, gks, cosp, sinp, S)
    attn_o = _attn(q, k, mv, B, S)
    mlstm_o = _mlstm(z, vt3, g3, og, conv_w.reshape(CONV_WIDTH, MLSTM_INNER), row(conv_b),
                     w_mq.astype(BF16), w_mk.astype(BF16), b_igate, b_fgate, row(g_mlstm_out), B, S)

    wgu = jnp.concatenate([w_gate_sh, w_up_sh], axis=1).astype(BF16)
    base, h2, lgt = _post(attn_o, mlstm_o, x2, gt1, row(g_attn_out), w_out.astype(BF16), row(g_norm2),
                          1.0 + sc2, sh2, gt2, w_router.T.astype(BF16), wgu, w_down_sh.astype(BF16), S)

    idx_t, w_t, rank_t, cnt = _route(lgt, router_bias.reshape(N_EXPERTS, 1))
    counts = cnt[:, 0].astype(I32)
    padded = (counts + ROW_BLOCK - 1) // ROW_BLOCK * ROW_BLOCK
    padded_end = jnp.cumsum(padded)
    off = padded_end - padded
    n_rows = T * TOP_K + N_EXPERTS * ROW_BLOCK
    dest_t = _dest(idx_t, rank_t, off.reshape(N_EXPERTS, 1))

    dest2_t = jnp.concatenate([dest_t, dest_t + n_rows], axis=1)
    xs = _scatter_rows(dest2_t, h2.reshape(2 * T, HALF), 2 * n_rows).reshape(2, n_rows, HALF)
    y = _experts(off // ROW_BLOCK, padded // ROW_BLOCK, counts, xs, w_gate_exp, w_up_exp, w_down_exp)
    dest_row = dest_t.reshape(1, TOP_K * T)
    y_slots = _gather_rows(jnp.concatenate([dest_row, dest_row + n_rows], axis=1), y.reshape(2 * n_rows, HALF))
    return _combine(y_slots, base, w_t.T, gt2, S)


def kernel(x, c, positions, w_ada, b_ada, g_norm1, w_in, g_cq, w_uq, g_ckv, w_ukv, g_qn, g_kn, g_attn_out, conv_w, conv_b, w_mq, w_mk, b_igate, b_fgate, g_mlstm_out, w_out, g_norm2, w_router, router_bias, w_gate_exp, w_up_exp, w_down_exp, w_gate_sh, w_up_sh, w_down_sh):
    B, S, d_model = x.shape
    assert d_model == D_MODEL and S % WIDE_TILE == 0 and S % SEQ_TILE == 0, (B, S, d_model)
    assert w_gate_exp.shape[1:] == (N_EXPERTS, D_MODEL, EXPERT_DIM), w_gate_exp.shape
    depth = w_ada.shape[0]
    x2 = x.reshape(B * S, D_MODEL)
    for l in range(depth):
        ada = _ada(c, w_ada[l], b_ada[l])
        x2 = _layer(x2, ada, positions, B, S, g_norm1[l], w_in[l], g_cq[l], w_uq[l], g_ckv[l], w_ukv[l], g_qn[l],
                    g_kn[l], g_attn_out[l], conv_w[l], conv_b[l], w_mq[l], w_mk[l], b_igate[l], b_fgate[l],
                    g_mlstm_out[l], w_out[l], g_norm2[l], w_router[l], router_bias[l], w_gate_exp[l], w_up_exp[l],
                    w_down_exp[l], w_gate_sh[l], w_up_sh[l], w_down_sh[l])
    return x2.reshape(B, S, D_MODEL)
```

```python
import functools
import math

import jax
import jax.numpy as jnp
from jax import lax
from jax.experimental import pallas as pl
from jax.experimental.pallas import tpu as pltpu
from jax.experimental.pallas import tpu_sc as plsc

F32 = jnp.float32
BF16 = jnp.bfloat16
I32 = jnp.int32

D_MODEL = 1024
MLA_HEADS = 8
QK_NOPE = 64
QK_ROPE = 32
QK_DIM = QK_NOPE + QK_ROPE
V_DIM = 64
Q_LORA = 384
KV_LORA = 256
ROPE_THETA = 10000.0
MLSTM_HEADS = 4
MLSTM_DIM = 128
MLSTM_INNER = MLSTM_HEADS * MLSTM_DIM
CONV_WIDTH = 4
N_EXPERTS = 256
TOP_K = 8
N_GROUPS = 8
TOPK_GROUPS = 4
GROUP_SIZE = N_EXPERTS // N_GROUPS
EXPERT_DIM = 256
SHARED_DIM = 256
ROUTED_SCALE = 2.5
RMS_EPS = 1e-6

LANES = 128
SEQ_TILE = 256
WIDE_TILE = 512
ROW_BLOCK = 256
VMEM_LIMIT = 56 * 1024 * 1024

_OFF_CQ = 0
_OFF_CKV = _OFF_CQ + Q_LORA
_OFF_Z = _OFF_CKV + KV_LORA
_OFF_O = _OFF_Z + MLSTM_INNER
_OFF_KR = _OFF_O + MLSTM_INNER
_OFF_KRS = _OFF_KR + LANES
_OFF_G = _OFF_KRS + LANES
_WA_COLS = _OFF_G + LANES
_WT_ROWS = MLSTM_INNER + 16


def _params(n_axes, vmem=VMEM_LIMIT):
    return pltpu.CompilerParams(dimension_semantics=("arbitrary",) * n_axes, vmem_limit_bytes=vmem)


def _dot(a, b):
    return jnp.dot(a, b, preferred_element_type=F32)


def _dot_nt(a, b):
    return lax.dot_general(a, b, (((1,), (1,)), ((), ())), preferred_element_type=F32)


def _sigmoid(x):
    return 1.0 / (1.0 + jnp.exp(-x))


def _silu(x):
    return x * _sigmoid(x)


U32 = jnp.uint32
PACKED = D_MODEL // 2


def _pack_rows(x):
    w = x.shape[1] // 2
    lo = lax.bitcast_convert_type(x[:, :w].astype(BF16).astype(F32), U32)
    hi = lax.bitcast_convert_type(x[:, w:].astype(BF16).astype(F32), U32)
    return lax.shift_right_logical(lo, U32(16)) | (hi & U32(0xFFFF0000))


def _unpack_rows(p):
    lo = lax.bitcast_convert_type(lax.shift_left(p, U32(16)), F32)
    hi = lax.bitcast_convert_type(p & U32(0xFFFF0000), F32)
    return lo, hi


def _ada_body(c_ref, w_ref, b_ref, o_ref):
    c = c_ref[...]
    cond = _silu(c).astype(BF16)
    o_ref[...] = _dot(cond, w_ref[...].astype(BF16)) + b_ref[...]


def _ada(c, w_ada, b_ada):
    B = c.shape[0]
    n = w_ada.shape[1]
    tn = 1024
    return pl.pallas_call(
        _ada_body,
        out_shape=jax.ShapeDtypeStruct((B, n), F32),
        grid=(n // tn,),
        in_specs=[
            pl.BlockSpec((B, D_MODEL), lambda j: (0, 0)),
            pl.BlockSpec((D_MODEL, tn), lambda j: (0, j)),
            pl.BlockSpec((1, tn), lambda j: (0, j)),
        ],
        out_specs=pl.BlockSpec((B, tn), lambda j: (0, j)),
        compiler_params=_params(1),
        name="ada",
    )(c, w_ada, b_ada.reshape(1, n))


def _inproj_body(x_ref, sc_ref, sh_ref, g1_ref, wa_ref, wt_ref, gcq_ref, wuq_ref, wuqs_ref, gckv_ref,
                 wuk_ref, wuv_ref, vone_ref, gq_ref, gqs_ref, gk_ref, gks_ref, cos_ref, sin_ref,
                 z_ref, o_ref, vt_ref, gt_ref, gn_ref, q_ref, k_ref, mv_ref):
    x = x_ref[...]
    hn = x * lax.rsqrt(jnp.mean(x * x, axis=-1, keepdims=True) + RMS_EPS) * g1_ref[...]
    hm = (hn * sc_ref[...] + sh_ref[...]).astype(BF16)
    p = _dot(hm, wa_ref[...])
    z_ref[...] = p[:, _OFF_Z:_OFF_Z + MLSTM_INNER].astype(BF16)
    o_ref[...] = p[:, _OFF_O:_OFF_O + MLSTM_INNER].astype(BF16)
    gn_ref[...] = p[:, _OFF_G:_OFF_G + LANES]
    for part in range(vt_ref.shape[0]):
        rt = _dot_nt(wt_ref[...], hm[part * SEQ_TILE:(part + 1) * SEQ_TILE])
        vt_ref[part] = rt[:MLSTM_INNER].astype(BF16)
        gt_ref[part] = rt[MLSTM_INNER:]

    cq = p[:, _OFF_CQ:_OFF_CQ + Q_LORA]
    cqn = (cq * lax.rsqrt(jnp.mean(cq * cq, axis=-1, keepdims=True) + RMS_EPS) * gcq_ref[...]).astype(BF16)
    ckv = p[:, _OFF_CKV:_OFF_CKV + KV_LORA]
    ckvn = (ckv * lax.rsqrt(jnp.mean(ckv * ckv, axis=-1, keepdims=True) + RMS_EPS) * gckv_ref[...]).astype(BF16)
    qa = _dot(cqn, wuq_ref[...])
    qb = _dot(cqn, wuqs_ref[...])
    ka = _dot(ckvn, wuk_ref[...])
    mv_ref[...] = (_dot(ckvn, wuv_ref[...]) + vone_ref[...]).astype(BF16)
    kr = p[:, _OFF_KR:_OFF_KR + LANES]
    krs = p[:, _OFF_KRS:_OFF_KRS + LANES]

    cosp = cos_ref[...]
    sinp = sin_ref[...]
    gqc = gq_ref[...] * cosp
    gqs = gqs_ref[...] * sinp
    gkc = gk_ref[...] * cosp
    k_rot = krs * (gks_ref[...] * sinp)
    kr_ss = jnp.sum(kr * kr, axis=-1, keepdims=True)
    q_scale = (QK_DIM ** -0.5) * math.log2(math.e)
    for h in range(MLA_HEADS):
        sl = slice(h * LANES, (h + 1) * LANES)
        qah = qa[:, sl]
        rq = lax.rsqrt(jnp.sum(qah * qah, axis=-1, keepdims=True) * (1.0 / QK_DIM) + RMS_EPS) * q_scale
        q_ref[:, sl] = ((qah * gqc + qb[:, sl] * gqs) * rq).astype(BF16)
        kah = ka[:, sl]
        rk = lax.rsqrt((jnp.sum(kah * kah, axis=-1, keepdims=True) + kr_ss) * (1.0 / QK_DIM) + RMS_EPS)
        k_ref[:, sl] = (((kah + kr) * gkc + k_rot) * rk).astype(BF16)


def _inproj(x2, sc1p, sh1, g1, wa, wt, gcq, wuq, wuqs, gckv, wuk, wuv, vone, gq, gqs, gk, gks, cosp, sinp, S):
    T = x2.shape[0]
    tm = WIDE_TILE
    nt = T // tm
    per_b = S // tm
    sub = tm // SEQ_TILE
    hp = MLA_HEADS * LANES
    full = lambda a: pl.BlockSpec(a.shape, lambda i: (0,) * a.ndim)
    row = lambda w: pl.BlockSpec((tm, w), lambda i: (i, 0))
    bmod = pl.BlockSpec((None, 1, D_MODEL), lambda i: (i // per_b, 0, 0))
    out_shapes = (
        jax.ShapeDtypeStruct((T, MLSTM_INNER), BF16),
        jax.ShapeDtypeStruct((T, MLSTM_INNER), BF16),
        jax.ShapeDtypeStruct((T // SEQ_TILE, MLSTM_INNER, SEQ_TILE), BF16),
        jax.ShapeDtypeStruct((T // SEQ_TILE, 16, SEQ_TILE), F32),
        jax.ShapeDtypeStruct((T, LANES), F32),
        jax.ShapeDtypeStruct((T, hp), BF16),
        jax.ShapeDtypeStruct((T, hp), BF16),
        jax.ShapeDtypeStruct((T, hp), BF16),
    )
    out_specs = (
        row(MLSTM_INNER), row(MLSTM_INNER),
        pl.BlockSpec((sub, MLSTM_INNER, SEQ_TILE), lambda i: (i, 0, 0)),
        pl.BlockSpec((sub, 16, SEQ_TILE), lambda i: (i, 0, 0)),
        row(LANES), row(hp), row(hp), row(hp),
    )
    in_specs = [row(D_MODEL), bmod, bmod, full(g1), full(wa), full(wt), full(gcq), full(wuq), full(wuqs),
                full(gckv), full(wuk), full(wuv), full(vone), full(gq), full(gqs), full(gk), full(gks),
                row(LANES), row(LANES)]
    return pl.pallas_call(
        _inproj_body, out_shape=out_shapes, grid=(nt,), in_specs=in_specs, out_specs=out_specs,
        compiler_params=_params(1), name="inproj",
    )(x2, sc1p, sh1, g1, wa, wt, gcq, wuq, wuqs, gckv, wuk, wuv, vone, gq, gqs, gk, gks, cosp, sinp)


ATTN_HEADS_PER_STEP = 8


def _pairs_loop(n, fn, init):
    c = lax.fori_loop(0, n // 2, lambda jj, c: fn(2 * jj + 1, fn(2 * jj, c)), init)
    return lax.fori_loop(0, n % 2, lambda _, c: fn(n - 1, c), c)


def _attn_body(q_ref, k_ref, v_ref, o_ref, s_scr):
    i = pl.program_id(2)
    t = SEQ_TILE
    nh = ATTN_HEADS_PER_STEP
    qry_pos = lax.broadcasted_iota(I32, (t, t), 0)
    key_pos = lax.broadcasted_iota(I32, (t, t), 1)
    causal = key_pos <= qry_pos
    qs = [q_ref[:, hh * LANES:(hh + 1) * LANES] for hh in range(nh)]

    def scores(j, hh, diag):
        kj = k_ref[pl.ds(pl.multiple_of(j * t, t), t), hh * LANES:(hh + 1) * LANES]
        s = _dot_nt(qs[hh], kj)
        return jnp.where(causal, s, -jnp.inf) if diag else s

    def halves(x):
        return x[:, :LANES], x[:, LANES:]

    def max_pass(j, mx, diag):
        out = []
        for hh in range(nh):
            s = scores(j, hh, diag)
            s_scr[j, hh] = s
            lo, hi = halves(s)
            out.append(jnp.maximum(mx[hh], jnp.maximum(lo, hi)))
        return tuple(out)

    mx = tuple(jnp.full((t, LANES), -jnp.inf, F32) for _ in range(nh))
    mx = _pairs_loop(i, lambda j, c: max_pass(j, c, False), mx)
    mx = max_pass(i, mx, True)
    mb = [jnp.broadcast_to(jnp.max(mx[hh], axis=1, keepdims=True), (t, LANES)) for hh in range(nh)]

    def acc_pass(j, carry, diag):
        row0 = pl.multiple_of(j * t, t)
        out = []
        for hh in range(nh):
            lo, hi = halves(s_scr[j, hh])
            p = jnp.concatenate([jnp.exp2(lo - mb[hh]), jnp.exp2(hi - mb[hh])], axis=1).astype(BF16)
            out.append(carry[hh] + _dot(p, v_ref[pl.ds(row0, t), hh * LANES:(hh + 1) * LANES]))
        return tuple(out)

    init = tuple(jnp.zeros((t, LANES), F32) for _ in range(nh))
    carry = _pairs_loop(i, lambda j, c: acc_pass(j, c, False), init)
    final = acc_pass(i, carry, True)
    outs = [acc / acc[:, V_DIM:V_DIM + 1] for acc in final]
    first_head = lax.broadcasted_iota(I32, (t, LANES), 1) < V_DIM
    o_ref[...] = jnp.concatenate(
        [jnp.where(first_head, outs[2 * pr], pltpu.roll(outs[2 * pr + 1], V_DIM, axis=1)) for pr in range(nh // 2)],
        axis=1).astype(BF16)


def _attn(q, k, mvt, B, S):
    T = q.shape[0]
    t = SEQ_TILE
    nh = ATTN_HEADS_PER_STEP
    per_b = S // t
    return pl.pallas_call(
        _attn_body,
        out_shape=jax.ShapeDtypeStruct((T, MLA_HEADS * V_DIM), BF16),
        grid=(B, MLA_HEADS // nh, per_b),
        in_specs=[
            pl.BlockSpec((t, nh * LANES), lambda b, hp, i: (b * per_b + i, hp)),
            pl.BlockSpec((S, nh * LANES), lambda b, hp, i: (b, hp)),
            pl.BlockSpec((S, nh * LANES), lambda b, hp, i: (b, hp)),
        ],
        out_specs=pl.BlockSpec((t, nh * V_DIM), lambda b, hp, i: (b * per_b + i, hp)),
        scratch_shapes=[pltpu.VMEM((per_b, nh, t, t), F32)],
        compiler_params=_params(3), name="attn",
    )(q, k, mvt)


def _log_sigmoid(x):
    return jnp.minimum(x, 0.0) - jnp.log(1.0 + jnp.exp(-jnp.abs(x)))


def _mlstm_body(bi_ref, bf_ref, z_ref, vt_ref, g_ref, gn_ref, og_ref, cw_ref, cb_ref, wq_ref, wk_ref, gout_ref,
                out_ref, zpad, q_s, k_s, st_s, m_s, *, S):
    h = pl.program_id(1)
    L = SEQ_TILE
    nchunk = S // L
    hd = MLSTM_DIM
    zpad[0:8, :] = jnp.zeros((8, hd), F32)
    zpad[8:, :] = z_ref[...].astype(F32)
    zc = cb_ref[...] + jnp.zeros((S, hd), F32)
    for j in range(CONV_WIDTH):
        zc = zc + cw_ref[j:j + 1, :] * zpad[pl.ds(8 - (CONV_WIDTH - 1) + j, S), :]
    zs = _silu(zc).astype(BF16)
    q_s[...] = _dot(zs, wq_ref[...]).astype(BF16)
    k_s[...] = (_dot(zs, wk_ref[...]) * (hd ** -0.5)).astype(BF16)

    st_s[...] = jnp.zeros_like(st_s)
    m_s[...] = jnp.zeros_like(m_s)
    b_i = bi_ref[h]
    b_f = bf_ref[h]
    r_i = lax.broadcasted_iota(I32, (L, L), 0)
    c_i = lax.broadcasted_iota(I32, (L, L), 1)
    tril = r_i >= c_i
    triu = r_i <= c_i
    f_lane = lax.broadcasted_iota(I32, (L, LANES), 1) == MLSTM_HEADS + h
    ones_row = jnp.where(lax.broadcasted_iota(I32, (hd, L), 0) == 0, 1.0, 0.0).astype(BF16)

    def chunk(c, _):
        start = pl.multiple_of(c * L, L)
        qc = q_s[pl.ds(start, L), :]
        kc = k_s[pl.ds(start, L), :]
        vt_aug = jnp.concatenate([vt_ref[c], ones_row], axis=0)
        li = g_ref[c, pl.ds(h, 1), :] + b_i
        fp = g_ref[c, pl.ds(MLSTM_HEADS + h, 1), :] + b_f
        lf = _log_sigmoid(fp)
        fp_col = jnp.sum(jnp.where(f_lane, gn_ref[pl.ds(start, L), :], 0.0), axis=1, keepdims=True) + b_f
        lf_col = _log_sigmoid(fp_col)
        b_col = jnp.sum(jnp.where(tril, lf, 0.0), axis=1, keepdims=True)
        b_row = jnp.sum(jnp.where(triu, lf_col, 0.0), axis=0, keepdims=True)
        d0 = b_col - b_row
        dmat = jnp.where(tril, d0 + li, -jnp.inf)
        m_prev = m_s[...]
        inter = b_col + m_prev
        mt = jnp.maximum(inter, jnp.max(dmat, axis=1, keepdims=True))
        w = jnp.exp(dmat - mt) * _dot_nt(qc, kc)
        decay = jnp.exp(inter - mt)
        st = st_s[...]
        num_aug = decay * _dot_nt(qc, st.astype(BF16)) + _dot_nt(w.astype(BF16), vt_aug)
        num = num_aug[:, :hd]
        den = num_aug[:, hd:hd + 1]
        hv = num / jnp.maximum(jnp.abs(den), jnp.exp(-mt))
        g = d0[L - 1:L, :] + li
        b_last = b_col[L - 1:L, :]
        m_new = jnp.maximum(b_last + m_prev, jnp.max(g, axis=1, keepdims=True))
        a = jnp.exp(g - m_new)
        cd = jnp.exp(b_last + m_prev - m_new)
        st_s[...] = cd * st + _dot((vt_aug.astype(F32) * a).astype(BF16), kc)
        m_s[...] = m_new
        hn = hv * lax.rsqrt(jnp.mean(hv * hv, axis=-1, keepdims=True) + RMS_EPS) * gout_ref[...]
        og = og_ref[pl.ds(start, L), :].astype(F32)
        out_ref[pl.ds(start, L), :] = (_sigmoid(og) * hn).astype(BF16)
        return 0

    lax.fori_loop(0, nchunk, chunk, 0)


def _mlstm(z, vt3, g3, gn, og, conv_w, conv_b, wq, wk, b_i, b_f, g_out, B, S):
    T = z.shape[0]
    L = SEQ_TILE
    per_b = S // L
    hd = MLSTM_DIM
    smem = pl.BlockSpec(memory_space=pltpu.SMEM)
    seq = pl.BlockSpec((S, hd), lambda b, h: (b, h))
    return pl.pallas_call(
        functools.partial(_mlstm_body, S=S),
        out_shape=jax.ShapeDtypeStruct((T, MLSTM_INNER), BF16),
        grid=(B, MLSTM_HEADS),
        in_specs=[
            smem, smem, seq,
            pl.BlockSpec((per_b, hd, L), lambda b, h: (b, h, 0)),
            pl.BlockSpec((per_b, 16, L), lambda b, h: (b, 0, 0)),
            pl.BlockSpec((S, LANES), lambda b, h: (b, 0)),
            seq,
            pl.BlockSpec((CONV_WIDTH, hd), lambda b, h: (0, h)),
            pl.BlockSpec((1, hd), lambda b, h: (0, h)),
            pl.BlockSpec((None, hd, hd), lambda b, h: (h, 0, 0)),
            pl.BlockSpec((None, hd, hd), lambda b, h: (h, 0, 0)),
            pl.BlockSpec((1, hd), lambda b, h: (0, h)),
        ],
        out_specs=seq,
        scratch_shapes=[
            pltpu.VMEM((S + 8, hd), F32),
            pltpu.VMEM((S, hd), BF16),
            pltpu.VMEM((S, hd), BF16),
            pltpu.VMEM((2 * hd, hd), F32),
            pltpu.VMEM((1, 1), F32),
        ],
        compiler_params=_params(2), name="mlstm",
    )(b_i, b_f, z, vt3, g3, gn, og, conv_w, conv_b, wq, wk, g_out)


def _post_body(ao_ref, mo_ref, x_ref, gt1_ref, gattn_ref, wout_ref, g2_ref, sc2_ref, sh2_ref, gt2_ref,
               wrt_ref, wgu_ref, wds_ref, base_ref, h2_ref, lg_ref):
    ao = ao_ref[...].astype(F32)
    aon = ao * lax.rsqrt(jnp.mean(ao * ao, axis=-1, keepdims=True) + RMS_EPS) * gattn_ref[...]
    mix_in = jnp.concatenate([aon.astype(BF16), mo_ref[...]], axis=1)
    x1 = x_ref[...] + gt1_ref[...] * _dot(mix_in, wout_ref[...])
    hn = x1 * lax.rsqrt(jnp.mean(x1 * x1, axis=-1, keepdims=True) + RMS_EPS) * g2_ref[...]
    h2 = hn * sc2_ref[...] + sh2_ref[...]
    words = _pack_rows(h2)
    h2_ref[0] = words[:, :HALF]
    h2_ref[1] = words[:, HALF:]
    h2b = h2.astype(BF16)
    gu = _dot(h2b, wgu_ref[...])
    act = (_silu(gu[:, :SHARED_DIM]) * gu[:, SHARED_DIM:]).astype(BF16)
    base_ref[...] = x1 + gt2_ref[...] * _dot(act, wds_ref[...])
    lg_ref[...] = _dot_nt(wrt_ref[...], h2b)


def _post(ao, mo, x2, gt1, gattn, wout, g2, sc2p, sh2, gt2, wrt, wgu, wds, S):
    T = x2.shape[0]
    tm = WIDE_TILE
    per_b = S // tm
    full = lambda a: pl.BlockSpec(a.shape, lambda i: (0,) * a.ndim)
    row = lambda w: pl.BlockSpec((tm, w), lambda i: (i, 0))
    bmod = pl.BlockSpec((None, 1, D_MODEL), lambda i: (i // per_b, 0, 0))
    return pl.pallas_call(
        _post_body,
        out_shape=(jax.ShapeDtypeStruct((T, D_MODEL), F32), jax.ShapeDtypeStruct((2, T, HALF), U32),
                   jax.ShapeDtypeStruct((N_EXPERTS, T), F32)),
        grid=(T // tm,),
        in_specs=[row(MLA_HEADS * V_DIM), row(MLSTM_INNER), row(D_MODEL), bmod, full(gattn), full(wout),
                  full(g2), bmod, bmod, bmod, full(wrt), full(wgu), full(wds)],
        out_specs=(row(D_MODEL), pl.BlockSpec((2, tm, HALF), lambda i: (0, i, 0)),
                   pl.BlockSpec((N_EXPERTS, tm), lambda i: (0, i))),
        compiler_params=_params(1), name="post",
    )(ao, mo, x2, gt1, gattn, wout, g2, sc2p, sh2, gt2, wrt, wgu, wds)


def _first_max(x, rows, n):
    mx = jnp.max(x, axis=0, keepdims=True)
    idx = jnp.min(jnp.where(x == mx, rows, n), axis=0, keepdims=True)
    return mx, idx


def _route_body(lg_ref, bias_ref, idx_ref, w_ref, rank_ref, cnt_ref, carry):
    i = pl.program_id(0)
    tt = lg_ref.shape[1]

    @pl.when(i == 0)
    def _():
        carry[...] = jnp.zeros_like(carry)

    sc = _sigmoid(lg_ref[...])
    bi = sc + bias_ref[...]
    rows = lax.broadcasted_iota(I32, (N_EXPERTS, tt), 0)
    rows_g = lax.broadcasted_iota(I32, (GROUP_SIZE, tt), 0)
    rows_n = lax.broadcasted_iota(I32, (N_GROUPS, tt), 0)
    neg = -jnp.inf
    gs = []
    for g in range(N_GROUPS):
        xg = bi[g * GROUP_SIZE:(g + 1) * GROUP_SIZE]
        m1, i1 = _first_max(xg, rows_g, GROUP_SIZE)
        m2 = jnp.max(jnp.where(rows_g == i1, neg, xg), axis=0, keepdims=True)
        gs.append(m1 + m2)
    cur = jnp.concatenate(gs, axis=0)
    keep = jnp.zeros((N_GROUPS, tt), F32)
    for _ in range(TOPK_GROUPS):
        _, gi = _first_max(cur, rows_n, N_GROUPS)
        sel = rows_n == gi
        keep = jnp.where(sel, 1.0, keep)
        cur = jnp.where(sel, neg, cur)
    masked = jnp.concatenate(
        [jnp.where(keep[g:g + 1] > 0.0, bi[g * GROUP_SIZE:(g + 1) * GROUP_SIZE], neg) for g in range(N_GROUPS)], axis=0)
    cur = masked
    onehot = jnp.zeros((N_EXPERTS, tt), F32)
    idxs, ws = [], []
    for _ in range(TOP_K):
        _, ei = _first_max(cur, rows, N_EXPERTS)
        sel = rows == ei
        ws.append(jnp.sum(jnp.where(sel, sc, 0.0), axis=0, keepdims=True))
        idxs.append(ei)
        cur = jnp.where(sel, neg, cur)
        onehot = jnp.where(sel, 1.0, onehot)
    wsum = ws[0]
    for k in range(1, TOP_K):
        wsum = wsum + ws[k]
    idx_ref[...] = jnp.concatenate(idxs, axis=0)
    w_ref[...] = jnp.concatenate([w / wsum * ROUTED_SCALE for w in ws], axis=0)
    r_i = lax.broadcasted_iota(I32, (tt, tt), 0)
    c_i = lax.broadcasted_iota(I32, (tt, tt), 1)
    before = jnp.where(r_i < c_i, 1.0, 0.0).astype(BF16)
    tot = _dot(onehot.astype(BF16), before) + carry[...]
    rank_ref[...] = jnp.concatenate(
        [jnp.sum(jnp.where(rows == ei, tot, 0.0), axis=0, keepdims=True) for ei in idxs], axis=0).astype(I32)
    carry[...] = carry[...] + jnp.sum(onehot, axis=1, keepdims=True)
    cnt_ref[...] = jnp.broadcast_to(carry[...], cnt_ref.shape)


def _route(lgt, bias_col):
    T = lgt.shape[1]
    tt = WIDE_TILE
    blk = pl.BlockSpec((TOP_K, tt), lambda i: (0, i))
    return pl.pallas_call(
        _route_body,
        out_shape=(jax.ShapeDtypeStruct((TOP_K, T), I32), jax.ShapeDtypeStruct((TOP_K, T), F32),
                   jax.ShapeDtypeStruct((TOP_K, T), I32), jax.ShapeDtypeStruct((N_EXPERTS, LANES), F32)),
        grid=(T // tt,),
        in_specs=[pl.BlockSpec((N_EXPERTS, tt), lambda i: (0, i)), pl.BlockSpec((N_EXPERTS, 1), lambda i: (0, 0))],
        out_specs=(blk, blk, blk, pl.BlockSpec((N_EXPERTS, LANES), lambda i: (0, 0))),
        scratch_shapes=[pltpu.VMEM((N_EXPERTS, 1), F32)],
        compiler_params=_params(1), name="route",
    )(lgt, bias_col)


def _dest_body(idx_ref, rank_ref, off_ref, dest_ref):
    tt = idx_ref.shape[1]
    rows = lax.broadcasted_iota(I32, (N_EXPERTS, tt), 0)
    off = off_ref[...]
    outs = []
    for k in range(TOP_K):
        sel = rows == idx_ref[k:k + 1, :]
        outs.append(jnp.sum(jnp.where(sel, off, 0), axis=0, keepdims=True) + rank_ref[k:k + 1, :])
    dest_ref[...] = jnp.concatenate(outs, axis=0)


def _dest(idx_t, rank_t, off_col):
    T = idx_t.shape[1]
    tt = min(4 * SEQ_TILE, T)
    blk = pl.BlockSpec((TOP_K, tt), lambda i: (0, i))
    return pl.pallas_call(
        _dest_body, out_shape=jax.ShapeDtypeStruct((TOP_K, T), I32), grid=(T // tt,),
        in_specs=[blk, blk, pl.BlockSpec((N_EXPERTS, 1), lambda i: (0, 0))], out_specs=blk,
        compiler_params=_params(1), name="dest",
    )(idx_t, rank_t, off_col)


SC_WINDOW = 128
HALF = PACKED // 2


def _sc_mesh():
    return plsc.VectorSubcoreMesh(core_axis_name="c", subcore_axis_name="s")


def _scatter_rows(dest2_t, rows, n_out):
    n = rows.shape[0]

    @pl.kernel(out_type=jax.ShapeDtypeStruct((n_out, HALF), U32), mesh=_sc_mesh(), scratch_types=[])
    def scatter(d_hbm, x_hbm, o_hbm):
        def body(d_vmem, x_vmem):
            for k in range(TOP_K):
                pltpu.sync_copy(x_vmem, o_hbm.at[d_vmem.at[k]])

        pltpu.emit_pipeline(
            body, grid=(n // SC_WINDOW,),
            in_specs=[pl.BlockSpec((TOP_K, SC_WINDOW), lambda i: (0, i)),
                      pl.BlockSpec((SC_WINDOW, HALF), lambda i: (i, 0))],
            out_specs=[], core_axis_name=("c", "s"), dimension_semantics=(pltpu.PARALLEL,),
        )(d_hbm, x_hbm)

    return scatter(dest2_t, rows)


def _gather_rows(index_row, rows):
    n = index_row.shape[1]

    @pl.kernel(out_type=jax.ShapeDtypeStruct((n, HALF), U32), mesh=_sc_mesh(), scratch_types=[])
    def gather(i_hbm, y_hbm, o_hbm):
        def body(i_vmem, o_vmem):
            pltpu.sync_copy(y_hbm.at[i_vmem.at[0]], o_vmem)

        pltpu.emit_pipeline(
            body, grid=(n // SC_WINDOW,),
            in_specs=[pl.BlockSpec((1, SC_WINDOW), lambda i: (0, i))],
            out_specs=[pl.BlockSpec((SC_WINDOW, HALF), lambda i: (i, 0))],
            core_axis_name=("c", "s"), dimension_semantics=(pltpu.PARALLEL,),
        )(i_hbm, o_hbm)

    return gather(index_row, rows)


EXPERT_RING = 16
WEIGHT_RING = 3


def _experts_body(first_ref, nblk_ref, cnt_ref, xs_ref, wg_ref, wu_ref, wd_ref, y_ref,
                  xbuf, ybuf, wgf, wuf, wdf, wgb, wub, wdb, xsem, ysem, wsem):
    e = pl.program_id(0)
    last = pl.num_programs(0) - 1
    wring = wgf.shape[0]

    def w_copies(ex):
        slot = ex % wring
        return [pltpu.make_async_copy(src.at[ex], dst.at[slot], wsem.at[slot])
                for src, dst in ((wg_ref, wgf), (wu_ref, wuf), (wd_ref, wdf))]

    @pl.when(e == 0)
    def _():
        for ex in range(wring - 1):
            for cp in w_copies(ex):
                cp.start()

    b0 = first_ref[e]
    nb = nblk_ref[e]
    total = first_ref[last] + nblk_ref[last]
    ring = xbuf.shape[0]

    def rows(g):
        return pl.ds(pl.multiple_of(g * ROW_BLOCK, ROW_BLOCK), ROW_BLOCK)

    def x_copy(g):
        slot = g % ring
        return pltpu.make_async_copy(xs_ref.at[:, rows(g), :], xbuf.at[slot], xsem.at[slot])

    def y_copy(g):
        slot = g % ring
        return pltpu.make_async_copy(ybuf.at[slot], y_ref.at[:, rows(g), :], ysem.at[slot])

    @pl.when(e == 0)
    def _():
        for g in range(ring - 1):
            @pl.when(g < total)
            def _():
                x_copy(g).start()

    for cp in w_copies(e):
        cp.wait()

    @pl.when(nb > 0)
    def _():
        slot = e % wring
        wgb[...] = wgf[slot].astype(BF16)
        wub[...] = wuf[slot].astype(BF16)
        wdb[...] = wdf[slot].astype(BF16)

    @pl.when(e + wring - 1 <= last)
    def _():
        for cp in w_copies(e + wring - 1):
            cp.start()

    def prefetch(g):
        @pl.when(g + ring - 1 < total)
        def _():
            x_copy(g + ring - 1).start()

    def compute(i):
        slot = (b0 + i) % ring
        words = jnp.concatenate([xbuf[slot, 0], xbuf[slot, 1]], axis=1)
        n_live = cnt_ref[e] - i * ROW_BLOCK
        words = jnp.where(lax.broadcasted_iota(I32, words.shape, 0) < n_live, words, U32(0))
        x = jnp.concatenate(_unpack_rows(words), axis=1).astype(BF16)
        act = (_silu(_dot(x, wgb[...])) * _dot(x, wub[...])).astype(BF16)
        y = _pack_rows(_dot(act, wdb[...]))
        ybuf[slot, 0] = y[:, :HALF]
        ybuf[slot, 1] = y[:, HALF:]

    def blocks(i, n):
        g0 = b0 + i
        prefetch(g0)
        for d in range(n):
            x_copy(g0 + d).wait()

            @pl.when(g0 + d >= ring)
            def _():
                y_copy(g0 + d - ring).wait()

        for d in range(n):
            compute(i + d)
        for d in range(1, n):
            prefetch(g0 + d)
        for d in range(n):
            y_copy(g0 + d).start()

    def pair(ii, carry):
        blocks(2 * ii, 2)
        return carry

    def single(_, carry):
        blocks(nb - 1, 1)
        return carry

    lax.fori_loop(0, nb // 2, pair, 0)
    lax.fori_loop(0, nb % 2, single, 0)

    @pl.when(e == last)
    def _():
        for back in range(ring, 0, -1):
            @pl.when(total >= back)
            def _():
                y_copy(total - back).wait()


def _experts(first_block, n_blocks, counts, xs, wg, wu, wd):
    n_rows = xs.shape[1]
    anyspec = pl.BlockSpec(memory_space=pl.ANY)
    ring = pltpu.VMEM((EXPERT_RING, 2, ROW_BLOCK, HALF), U32)
    grid_spec = pltpu.PrefetchScalarGridSpec(
        num_scalar_prefetch=3, grid=(N_EXPERTS,),
        in_specs=[anyspec] * 4,
        out_specs=anyspec,
        scratch_shapes=[ring, ring,
                        pltpu.VMEM((WEIGHT_RING, D_MODEL, EXPERT_DIM), F32),
                        pltpu.VMEM((WEIGHT_RING, D_MODEL, EXPERT_DIM), F32),
                        pltpu.VMEM((WEIGHT_RING, EXPERT_DIM, D_MODEL), F32),
                        pltpu.VMEM((D_MODEL, EXPERT_DIM), BF16), pltpu.VMEM((D_MODEL, EXPERT_DIM), BF16),
                        pltpu.VMEM((EXPERT_DIM, D_MODEL), BF16),
                        pltpu.SemaphoreType.DMA((EXPERT_RING,)), pltpu.SemaphoreType.DMA((EXPERT_RING,)),
                        pltpu.SemaphoreType.DMA((WEIGHT_RING,))],
    )
    return pl.pallas_call(
        _experts_body, out_shape=jax.ShapeDtypeStruct((2, n_rows, HALF), U32), grid_spec=grid_spec,
        compiler_params=_params(1), name="experts",
    )(first_block, n_blocks, counts, xs, wg, wu, wd)


def _combine_body(*refs):
    ya_refs = refs[:TOP_K]
    yb_refs = refs[TOP_K:2 * TOP_K]
    base_ref, w_ref, gt2_ref, out_ref = refs[2 * TOP_K:]
    w = w_ref[...]
    r_lo = r_hi = None
    for k in range(TOP_K):
        lo, hi = _unpack_rows(jnp.concatenate([ya_refs[k][...], yb_refs[k][...]], axis=1))
        wk = w[:, k:k + 1]
        r_lo = wk * lo if r_lo is None else r_lo + wk * lo
        r_hi = wk * hi if r_hi is None else r_hi + wk * hi
    out_ref[...] = base_ref[...] + gt2_ref[...] * jnp.concatenate([r_lo, r_hi], axis=1)


def _combine(y_slots, base, w_tk, gt2, S):
    T = base.shape[0]
    tt = WIDE_TILE
    nt = T // tt
    per_b = S // tt
    yspec = lambda half, k: pl.BlockSpec((tt, HALF), lambda i: ((half * TOP_K + k) * nt + i, 0))
    return pl.pallas_call(
        _combine_body, out_shape=jax.ShapeDtypeStruct((T, D_MODEL), F32), grid=(nt,),
        in_specs=[yspec(half, k) for half in range(2) for k in range(TOP_K)] + [
            pl.BlockSpec((tt, D_MODEL), lambda i: (i, 0)),
            pl.BlockSpec((tt, TOP_K), lambda i: (i, 0)),
            pl.BlockSpec((None, 1, D_MODEL), lambda i: (i // per_b, 0, 0)),
        ],
        out_specs=pl.BlockSpec((tt, D_MODEL), lambda i: (i, 0)),
        compiler_params=_params(1), name="combine",
    )(*([y_slots] * (2 * TOP_K)), base, w_tk, gt2)


def _pad_heads(w, width):
    kdim = w.shape[0]
    w3 = w.reshape(kdim, MLA_HEADS, width)
    return jnp.pad(w3, ((0, 0), (0, 0), (0, LANES - width))).reshape(kdim, MLA_HEADS * LANES)


def _rot_partner(w_rope):
    half = QK_ROPE // 2
    return jnp.concatenate([-w_rope[..., half:], w_rope[..., :half]], axis=-1)


def _lane_vec(nope, rope):
    return jnp.concatenate([nope, rope, jnp.zeros((LANES - QK_DIM,), F32)]).reshape(1, LANES)


def _layer(x2, ada, positions, B, S, g_norm1, w_in, g_cq, w_uq, g_ckv, w_ukv, g_qn, g_kn, g_attn_out,
           conv_w, conv_b, w_mq, w_mk, b_igate, b_fgate, g_mlstm_out, w_out, g_norm2, w_router, router_bias,
           w_gate_exp, w_up_exp, w_down_exp, w_gate_sh, w_up_sh, w_down_sh):
    T = B * S
    sh1, sc1, gt1, sh2, sc2, gt2 = [a.reshape(B, 1, D_MODEL) for a in jnp.split(ada, 6, axis=-1)]
    row = lambda v: v.reshape(1, -1).astype(F32)

    inv_freq = ROPE_THETA ** (-jnp.arange(0, QK_ROPE, 2, dtype=F32) / QK_ROPE)
    ang = positions.astype(F32).reshape(T, 1) * inv_freq
    half = QK_ROPE // 2
    freq_lane = jnp.arange(half)[:, None]
    lane = jnp.arange(LANES)[None, :]
    place = ((lane == QK_NOPE + freq_lane) | (lane == QK_NOPE + half + freq_lane)).astype(F32)
    spread = lambda tbl: jnp.dot(tbl, place, precision=lax.Precision.HIGHEST)
    cosp = (jnp.arange(LANES) < QK_NOPE).astype(F32)[None, :] + spread(jnp.cos(ang))
    sinp = spread(jnp.sin(ang))

    o0 = 0
    w_cq = w_in[:, o0:o0 + Q_LORA]; o0 += Q_LORA
    w_ckv = w_in[:, o0:o0 + KV_LORA]; o0 += KV_LORA
    w_kr = w_in[:, o0:o0 + QK_ROPE]; o0 += QK_ROPE
    w_z = w_in[:, o0:o0 + MLSTM_INNER]; o0 += MLSTM_INNER
    w_v = w_in[:, o0:o0 + MLSTM_INNER]; o0 += MLSTM_INNER
    w_o = w_in[:, o0:o0 + MLSTM_INNER]; o0 += MLSTM_INNER
    w_i = w_in[:, o0:o0 + MLSTM_HEADS]; o0 += MLSTM_HEADS
    w_f = w_in[:, o0:o0 + MLSTM_HEADS]
    zl = jnp.zeros((D_MODEL, QK_NOPE), F32)
    zr = jnp.zeros((D_MODEL, LANES - QK_DIM), F32)
    zg = jnp.zeros((D_MODEL, LANES - 2 * MLSTM_HEADS), F32)
    wa = jnp.concatenate([w_cq, w_ckv, w_z, w_o, zl, w_kr, zr, zl, _rot_partner(w_kr), zr, w_i, w_f, zg],
                         axis=1).astype(BF16)
    wt = jnp.concatenate([w_v.T, w_i.T, w_f.T, jnp.zeros((16 - 2 * MLSTM_HEADS, D_MODEL), F32)], axis=0).astype(BF16)

    uq3 = w_uq.reshape(Q_LORA, MLA_HEADS, QK_DIM)
    uq_rot = jnp.concatenate([jnp.zeros((Q_LORA, MLA_HEADS, QK_NOPE), F32), _rot_partner(uq3[..., QK_NOPE:])], axis=-1)
    wuq = _pad_heads(w_uq, QK_DIM).astype(BF16)
    wuqs = _pad_heads(uq_rot.reshape(Q_LORA, MLA_HEADS * QK_DIM), QK_DIM).astype(BF16)
    ukv3 = w_ukv.reshape(KV_LORA, MLA_HEADS, QK_NOPE + V_DIM)
    wuk = _pad_heads(ukv3[..., :QK_NOPE].reshape(KV_LORA, MLA_HEADS * QK_NOPE), QK_NOPE).astype(BF16)
    wuv = _pad_heads(ukv3[..., QK_NOPE:].reshape(KV_LORA, MLA_HEADS * V_DIM), V_DIM).astype(BF16)
    vone = jnp.tile((jnp.arange(LANES) == V_DIM).astype(F32), MLA_HEADS).reshape(1, MLA_HEADS * LANES)
    swap = lambda g_rope: jnp.concatenate([g_rope[half:], g_rope[:half]])
    no_nope = jnp.zeros((QK_NOPE,), F32)
    gq = _lane_vec(g_qn[:QK_NOPE], g_qn[QK_NOPE:])
    gqs = _lane_vec(no_nope, swap(g_qn[QK_NOPE:]))
    gk = _lane_vec(g_kn[:QK_NOPE], g_kn[QK_NOPE:])
    gks = _lane_vec(no_nope, swap(g_kn[QK_NOPE:]))

    z, og, vt3, g3, gn, q, k, mv = _inproj(
        x2, 1.0 + sc1, sh1, row(g_norm1), wa, wt, row(g_cq), wuq, wuqs, row(g_ckv), wuk, wuv, vone,
        gq, gqs, gk, gks, cosp, sinp, S)
    attn_o = _attn(q, k, mv, B, S)
    mlstm_o = _mlstm(z, vt3, g3, gn, og, conv_w.reshape(CONV_WIDTH, MLSTM_INNER), row(conv_b),
                     w_mq.astype(BF16), w_mk.astype(BF16), b_igate, b_fgate, row(g_mlstm_out), B, S)

    wgu = jnp.concatenate([w_gate_sh, w_up_sh], axis=1).astype(BF16)
    base, h2, lgt = _post(attn_o, mlstm_o, x2, gt1, row(g_attn_out), w_out.astype(BF16), row(g_norm2),
                          1.0 + sc2, sh2, gt2, w_router.T.astype(BF16), wgu, w_down_sh.astype(BF16), S)

    idx_t, w_t, rank_t, cnt = _route(lgt, router_bias.reshape(N_EXPERTS, 1))
    counts = cnt[:, 0].astype(I32)
    padded = (counts + ROW_BLOCK - 1) // ROW_BLOCK * ROW_BLOCK
    padded_end = jnp.cumsum(padded)
    off = padded_end - padded
    n_rows = T * TOP_K + N_EXPERTS * ROW_BLOCK
    dest_t = _dest(idx_t, rank_t, off.reshape(N_EXPERTS, 1))

    dest2_t = jnp.concatenate([dest_t, dest_t + n_rows], axis=1)
    xs = _scatter_rows(dest2_t, h2.reshape(2 * T, HALF), 2 * n_rows).reshape(2, n_rows, HALF)
    y = _experts(off // ROW_BLOCK, padded // ROW_BLOCK, counts, xs, w_gate_exp, w_up_exp, w_down_exp)
    dest_row = dest_t.reshape(1, TOP_K * T)
    y_slots = _gather_rows(jnp.concatenate([dest_row, dest_row + n_rows], axis=1), y.reshape(2 * n_rows, HALF))
    return _combine(y_slots, base, w_t.T, gt2, S)


def kernel(x, c, positions, w_ada, b_ada, g_norm1, w_in, g_cq, w_uq, g_ckv, w_ukv, g_qn, g_kn, g_attn_out, conv_w, conv_b, w_mq, w_mk, b_igate, b_fgate, g_mlstm_out, w_out, g_norm2, w_router, router_bias, w_gate_exp, w_up_exp, w_down_exp, w_gate_sh, w_up_sh, w_down_sh):
    B, S, d_model = x.shape
    assert d_model == D_MODEL and S % WIDE_TILE == 0 and S % SEQ_TILE == 0, (B, S, d_model)
    assert w_gate_exp.shape[1:] == (N_EXPERTS, D_MODEL, EXPERT_DIM), w_gate_exp.shape
    depth = w_ada.shape[0]
    x2 = x.reshape(B * S, D_MODEL)
    for l in range(depth):
        ada = _ada(c, w_ada[l], b_ada[l])
        x2 = _layer(x2, ada, positions, B, S, g_norm1[l], w_in[l], g_cq[l], w_uq[l], g_ckv[l], w_ukv[l], g_qn[l],
                    g_kn[l], g_attn_out[l], conv_w[l], conv_b[l], w_mq[l], w_mk[l], b_igate[l], b_fgate[l],
                    g_mlstm_out[l], w_out[l], g_norm2[l], w_router[l], router_bias[l], w_gate_exp[l], w_up_exp[l],
                    w_down_exp[l], w_gate_sh[l], w_up_sh[l], w_down_sh[l])
    return x2.reshape(B, S, D_MODEL)
```

```python
import functools
import math

import jax
import jax.numpy as jnp
from jax import lax
from jax.experimental import pallas as pl
from jax.experimental.pallas import tpu as pltpu
from jax.experimental.pallas import tpu_sc as plsc

F32 = jnp.float32
BF16 = jnp.bfloat16
I32 = jnp.int32

D_MODEL = 1024
MLA_HEADS = 8
QK_NOPE = 64
QK_ROPE = 32
QK_DIM = QK_NOPE + QK_ROPE
V_DIM = 64
Q_LORA = 384
KV_LORA = 256
ROPE_THETA = 10000.0
MLSTM_HEADS = 4
MLSTM_DIM = 128
MLSTM_INNER = MLSTM_HEADS * MLSTM_DIM
CONV_WIDTH = 4
N_EXPERTS = 256
TOP_K = 8
N_GROUPS = 8
TOPK_GROUPS = 4
GROUP_SIZE = N_EXPERTS // N_GROUPS
EXPERT_DIM = 256
SHARED_DIM = 256
ROUTED_SCALE = 2.5
RMS_EPS = 1e-6

LANES = 128
SEQ_TILE = 256
WIDE_TILE = 512
ROW_BLOCK = 256
VMEM_LIMIT = 56 * 1024 * 1024

_OFF_CQ = 0
_OFF_CKV = _OFF_CQ + Q_LORA
_OFF_Z = _OFF_CKV + KV_LORA
_OFF_O = _OFF_Z + MLSTM_INNER
_OFF_KR = _OFF_O + MLSTM_INNER
_OFF_KRS = _OFF_KR + LANES
_OFF_G = _OFF_KRS + LANES
_WA_COLS = _OFF_G + LANES
_WT_ROWS = MLSTM_INNER + 16


def _params(n_axes, vmem=VMEM_LIMIT):
    return pltpu.CompilerParams(dimension_semantics=("arbitrary",) * n_axes, vmem_limit_bytes=vmem)


def _dot(a, b):
    return jnp.dot(a, b, preferred_element_type=F32)


def _dot_nt(a, b):
    return lax.dot_general(a, b, (((1,), (1,)), ((), ())), preferred_element_type=F32)


def _sigmoid(x):
    return 1.0 / (1.0 + jnp.exp(-x))


def _silu(x):
    return x * _sigmoid(x)


U32 = jnp.uint32
PACKED = D_MODEL // 2


def _pack_rows(x):
    w = x.shape[1] // 2
    lo = lax.bitcast_convert_type(x[:, :w].astype(BF16).astype(F32), U32)
    hi = lax.bitcast_convert_type(x[:, w:].astype(BF16).astype(F32), U32)
    return lax.shift_right_logical(lo, U32(16)) | (hi & U32(0xFFFF0000))


def _unpack_rows(p):
    lo = lax.bitcast_convert_type(lax.shift_left(p, U32(16)), F32)
    hi = lax.bitcast_convert_type(p & U32(0xFFFF0000), F32)
    return lo, hi


def _ada_body(c_ref, w_ref, b_ref, o_ref):
    c = c_ref[...]
    cond = _silu(c).astype(BF16)
    o_ref[...] = _dot(cond, w_ref[...].astype(BF16)) + b_ref[...]


def _ada(c, w_ada, b_ada):
    B = c.shape[0]
    n = w_ada.shape[1]
    tn = 1024
    return pl.pallas_call(
        _ada_body,
        out_shape=jax.ShapeDtypeStruct((B, n), F32),
        grid=(n // tn,),
        in_specs=[
            pl.BlockSpec((B, D_MODEL), lambda j: (0, 0)),
            pl.BlockSpec((D_MODEL, tn), lambda j: (0, j)),
            pl.BlockSpec((1, tn), lambda j: (0, j)),
        ],
        out_specs=pl.BlockSpec((B, tn), lambda j: (0, j)),
        compiler_params=_params(1),
        name="ada",
    )(c, w_ada, b_ada.reshape(1, n))


def _inproj_body(x_ref, sc_ref, sh_ref, g1_ref, wa_ref, wt_ref, gcq_ref, wuq_ref, wuqs_ref, gckv_ref,
                 wuk_ref, wuv_ref, vone_ref, gq_ref, gqs_ref, gk_ref, gks_ref, cos_ref, sin_ref,
                 z_ref, o_ref, vt_ref, gt_ref, gn_ref, q_ref, k_ref, mv_ref):
    x = x_ref[...]
    hn = x * lax.rsqrt(jnp.mean(x * x, axis=-1, keepdims=True) + RMS_EPS) * g1_ref[...]
    hm = (hn * sc_ref[...] + sh_ref[...]).astype(BF16)
    p = _dot(hm, wa_ref[...])
    z_ref[...] = p[:, _OFF_Z:_OFF_Z + MLSTM_INNER].astype(BF16)
    o_ref[...] = p[:, _OFF_O:_OFF_O + MLSTM_INNER].astype(BF16)
    gn_ref[...] = p[:, _OFF_G:_OFF_G + LANES]
    for part in range(vt_ref.shape[0]):
        rt = _dot_nt(wt_ref[...], hm[part * SEQ_TILE:(part + 1) * SEQ_TILE])
        vt_ref[part] = rt[:MLSTM_INNER].astype(BF16)
        gt_ref[part] = rt[MLSTM_INNER:]

    cq = p[:, _OFF_CQ:_OFF_CQ + Q_LORA]
    cqn = (cq * lax.rsqrt(jnp.mean(cq * cq, axis=-1, keepdims=True) + RMS_EPS) * gcq_ref[...]).astype(BF16)
    ckv = p[:, _OFF_CKV:_OFF_CKV + KV_LORA]
    ckvn = (ckv * lax.rsqrt(jnp.mean(ckv * ckv, axis=-1, keepdims=True) + RMS_EPS) * gckv_ref[...]).astype(BF16)
    qa = _dot(cqn, wuq_ref[...])
    qb = _dot(cqn, wuqs_ref[...])
    ka = _dot(ckvn, wuk_ref[...])
    mv_ref[...] = (_dot(ckvn, wuv_ref[...]) + vone_ref[...]).astype(BF16)
    kr = p[:, _OFF_KR:_OFF_KR + LANES]
    krs = p[:, _OFF_KRS:_OFF_KRS + LANES]

    cosp = cos_ref[...]
    sinp = sin_ref[...]
    gqc = gq_ref[...] * cosp
    gqs = gqs_ref[...] * sinp
    gkc = gk_ref[...] * cosp
    k_rot = krs * (gks_ref[...] * sinp)
    kr_ss = jnp.sum(kr * kr, axis=-1, keepdims=True)
    q_scale = (QK_DIM ** -0.5) * math.log2(math.e)
    for h in range(MLA_HEADS):
        sl = slice(h * LANES, (h + 1) * LANES)
        qah = qa[:, sl]
        rq = lax.rsqrt(jnp.sum(qah * qah, axis=-1, keepdims=True) * (1.0 / QK_DIM) + RMS_EPS) * q_scale
        q_ref[:, sl] = ((qah * gqc + qb[:, sl] * gqs) * rq).astype(BF16)
        kah = ka[:, sl]
        rk = lax.rsqrt((jnp.sum(kah * kah, axis=-1, keepdims=True) + kr_ss) * (1.0 / QK_DIM) + RMS_EPS)
        k_ref[:, sl] = (((kah + kr) * gkc + k_rot) * rk).astype(BF16)


def _inproj(x2, sc1p, sh1, g1, wa, wt, gcq, wuq, wuqs, gckv, wuk, wuv, vone, gq, gqs, gk, gks, cosp, sinp, S):
    T = x2.shape[0]
    tm = WIDE_TILE
    nt = T // tm
    per_b = S // tm
    sub = tm // SEQ_TILE
    hp = MLA_HEADS * LANES
    full = lambda a: pl.BlockSpec(a.shape, lambda i: (0,) * a.ndim)
    row = lambda w: pl.BlockSpec((tm, w), lambda i: (i, 0))
    bmod = pl.BlockSpec((None, 1, D_MODEL), lambda i: (i // per_b, 0, 0))
    out_shapes = (
        jax.ShapeDtypeStruct((T, MLSTM_INNER), BF16),
        jax.ShapeDtypeStruct((T, MLSTM_INNER), BF16),
        jax.ShapeDtypeStruct((T // SEQ_TILE, MLSTM_INNER, SEQ_TILE), BF16),
        jax.ShapeDtypeStruct((T // SEQ_TILE, 16, SEQ_TILE), F32),
        jax.ShapeDtypeStruct((T, LANES), F32),
        jax.ShapeDtypeStruct((T, hp), BF16),
        jax.ShapeDtypeStruct((T, hp), BF16),
        jax.ShapeDtypeStruct((T, hp), BF16),
    )
    out_specs = (
        row(MLSTM_INNER), row(MLSTM_INNER),
        pl.BlockSpec((sub, MLSTM_INNER, SEQ_TILE), lambda i: (i, 0, 0)),
        pl.BlockSpec((sub, 16, SEQ_TILE), lambda i: (i, 0, 0)),
        row(LANES), row(hp), row(hp), row(hp),
    )
    in_specs = [row(D_MODEL), bmod, bmod, full(g1), full(wa), full(wt), full(gcq), full(wuq), full(wuqs),
                full(gckv), full(wuk), full(wuv), full(vone), full(gq), full(gqs), full(gk), full(gks),
                row(LANES), row(LANES)]
    return pl.pallas_call(
        _inproj_body, out_shape=out_shapes, grid=(nt,), in_specs=in_specs, out_specs=out_specs,
        compiler_params=_params(1), name="inproj",
    )(x2, sc1p, sh1, g1, wa, wt, gcq, wuq, wuqs, gckv, wuk, wuv, vone, gq, gqs, gk, gks, cosp, sinp)


ATTN_HEADS_PER_STEP = 8


def _pairs_loop(n, fn, init):
    c = lax.fori_loop(0, n // 2, lambda jj, c: fn(2 * jj + 1, fn(2 * jj, c)), init)
    return lax.fori_loop(0, n % 2, lambda _, c: fn(n - 1, c), c)


def _attn_body(q_ref, k_ref, v_ref, o_ref, s_scr):
    i = pl.program_id(2)
    t = SEQ_TILE
    nh = ATTN_HEADS_PER_STEP
    qry_pos = lax.broadcasted_iota(I32, (t, t), 0)
    key_pos = lax.broadcasted_iota(I32, (t, t), 1)
    causal = key_pos <= qry_pos
    qs = [q_ref[:, hh * LANES:(hh + 1) * LANES] for hh in range(nh)]

    def scores(j, hh, diag):
        kj = k_ref[pl.ds(pl.multiple_of(j * t, t), t), hh * LANES:(hh + 1) * LANES]
        s = _dot_nt(qs[hh], kj)
        return jnp.where(causal, s, -jnp.inf) if diag else s

    def halves(x):
        return x[:, :LANES], x[:, LANES:]

    def max_pass(j, mx, diag):
        out = []
        for hh in range(nh):
            s = scores(j, hh, diag)
            s_scr[j, hh] = s
            lo, hi = halves(s)
            out.append(jnp.maximum(mx[hh], jnp.maximum(lo, hi)))
        return tuple(out)

    mx = tuple(jnp.full((t, LANES), -jnp.inf, F32) for _ in range(nh))
    mx = _pairs_loop(i, lambda j, c: max_pass(j, c, False), mx)
    mx = max_pass(i, mx, True)
    mb = [jnp.broadcast_to(jnp.max(mx[hh], axis=1, keepdims=True), (t, LANES)) for hh in range(nh)]

    def acc_pass(j, carry, diag):
        row0 = pl.multiple_of(j * t, t)
        out = []
        for hh in range(nh):
            lo, hi = halves(s_scr[j, hh])
            p = jnp.concatenate([jnp.exp2(lo - mb[hh]), jnp.exp2(hi - mb[hh])], axis=1).astype(BF16)
            out.append(carry[hh] + _dot(p, v_ref[pl.ds(row0, t), hh * LANES:(hh + 1) * LANES]))
        return tuple(out)

    init = tuple(jnp.zeros((t, LANES), F32) for _ in range(nh))
    carry = _pairs_loop(i, lambda j, c: acc_pass(j, c, False), init)
    final = acc_pass(i, carry, True)
    outs = [acc / acc[:, V_DIM:V_DIM + 1] for acc in final]
    first_head = lax.broadcasted_iota(I32, (t, LANES), 1) < V_DIM
    o_ref[...] = jnp.concatenate(
        [jnp.where(first_head, outs[2 * pr], pltpu.roll(outs[2 * pr + 1], V_DIM, axis=1)) for pr in range(nh // 2)],
        axis=1).astype(BF16)


def _attn(q, k, mvt, B, S):
    T = q.shape[0]
    t = SEQ_TILE
    nh = ATTN_HEADS_PER_STEP
    per_b = S // t
    return pl.pallas_call(
        _attn_body,
        out_shape=jax.ShapeDtypeStruct((T, MLA_HEADS * V_DIM), BF16),
        grid=(B, MLA_HEADS // nh, per_b),
        in_specs=[
            pl.BlockSpec((t, nh * LANES), lambda b, hp, i: (b * per_b + i, hp)),
            pl.BlockSpec((S, nh * LANES), lambda b, hp, i: (b, hp)),
            pl.BlockSpec((S, nh * LANES), lambda b, hp, i: (b, hp)),
        ],
        out_specs=pl.BlockSpec((t, nh * V_DIM), lambda b, hp, i: (b * per_b + i, hp)),
        scratch_shapes=[pltpu.VMEM((per_b, nh, t, t), F32)],
        compiler_params=_params(3), name="attn",
    )(q, k, mvt)


def _log_sigmoid(x):
    return jnp.minimum(x, 0.0) - jnp.log(1.0 + jnp.exp(-jnp.abs(x)))


def _mlstm_body(bi_ref, bf_ref, z_ref, vt_ref, g_ref, gn_ref, og_ref, cw_ref, cb_ref, wq_ref, wk_ref, gout_ref,
                out_ref, zpad, q_s, k_s, st_s, m_s, *, S):
    h = pl.program_id(1)
    L = SEQ_TILE
    nchunk = S // L
    hd = MLSTM_DIM
    zpad[0:8, :] = jnp.zeros((8, hd), F32)
    zpad[8:, :] = z_ref[...].astype(F32)
    zc = cb_ref[...] + jnp.zeros((S, hd), F32)
    for j in range(CONV_WIDTH):
        zc = zc + cw_ref[j:j + 1, :] * zpad[pl.ds(8 - (CONV_WIDTH - 1) + j, S), :]
    zs = _silu(zc).astype(BF16)
    q_s[...] = _dot(zs, wq_ref[...]).astype(BF16)
    k_s[...] = (_dot(zs, wk_ref[...]) * (hd ** -0.5)).astype(BF16)

    st_s[...] = jnp.zeros_like(st_s)
    m_s[...] = jnp.zeros_like(m_s)
    b_i = bi_ref[h]
    b_f = bf_ref[h]
    r_i = lax.broadcasted_iota(I32, (L, L), 0)
    c_i = lax.broadcasted_iota(I32, (L, L), 1)
    tril = r_i >= c_i
    triu = r_i <= c_i
    f_lane = lax.broadcasted_iota(I32, (L, LANES), 1) == MLSTM_HEADS + h
    ones_row = jnp.where(lax.broadcasted_iota(I32, (hd, L), 0) == 0, 1.0, 0.0).astype(BF16)

    def chunk(c, _):
        start = pl.multiple_of(c * L, L)
        qc = q_s[pl.ds(start, L), :]
        kc = k_s[pl.ds(start, L), :]
        vt_aug = jnp.concatenate([vt_ref[c], ones_row], axis=0)
        li = g_ref[c, pl.ds(h, 1), :] + b_i
        fp = g_ref[c, pl.ds(MLSTM_HEADS + h, 1), :] + b_f
        lf = _log_sigmoid(fp)
        fp_col = jnp.sum(jnp.where(f_lane, gn_ref[pl.ds(start, L), :], 0.0), axis=1, keepdims=True) + b_f
        lf_col = _log_sigmoid(fp_col)
        b_col = jnp.sum(jnp.where(tril, lf, 0.0), axis=1, keepdims=True)
        b_row = jnp.sum(jnp.where(triu, lf_col, 0.0), axis=0, keepdims=True)
        d0 = b_col - b_row
        dmat = jnp.where(tril, d0 + li, -jnp.inf)
        m_prev = m_s[...]
        inter = b_col + m_prev
        mt = jnp.maximum(inter, jnp.max(dmat, axis=1, keepdims=True))
        w = jnp.exp(dmat - mt) * _dot_nt(qc, kc)
        decay = jnp.exp(inter - mt)
        st = st_s[...]
        num_aug = decay * _dot_nt(qc, st.astype(BF16)) + _dot_nt(w.astype(BF16), vt_aug)
        num = num_aug[:, :hd]
        den = num_aug[:, hd:hd + 1]
        hv = num / jnp.maximum(jnp.abs(den), jnp.exp(-mt))
        g = d0[L - 1:L, :] + li
        b_last = b_col[L - 1:L, :]
        m_new = jnp.maximum(b_last + m_prev, jnp.max(g, axis=1, keepdims=True))
        a = jnp.exp(g - m_new)
        cd = jnp.exp(b_last + m_prev - m_new)
        st_s[...] = cd * st + _dot((vt_aug.astype(F32) * a).astype(BF16), kc)
        m_s[...] = m_new
        hn = hv * lax.rsqrt(jnp.mean(hv * hv, axis=-1, keepdims=True) + RMS_EPS) * gout_ref[...]
        og = og_ref[pl.ds(start, L), :].astype(F32)
        out_ref[pl.ds(start, L), :] = (_sigmoid(og) * hn).astype(BF16)
        return 0

    lax.fori_loop(0, nchunk, chunk, 0)


def _mlstm(z, vt3, g3, gn, og, conv_w, conv_b, wq, wk, b_i, b_f, g_out, B, S):
    T = z.shape[0]
    L = SEQ_TILE
    per_b = S // L
    hd = MLSTM_DIM
    smem = pl.BlockSpec(memory_space=pltpu.SMEM)
    seq = pl.BlockSpec((S, hd), lambda b, h: (b, h))
    return pl.pallas_call(
        functools.partial(_mlstm_body, S=S),
        out_shape=jax.ShapeDtypeStruct((T, MLSTM_INNER), BF16),
        grid=(B, MLSTM_HEADS),
        in_specs=[
            smem, smem, seq,
            pl.BlockSpec((per_b, hd, L), lambda b, h: (b, h, 0)),
            pl.BlockSpec((per_b, 16, L), lambda b, h: (b, 0, 0)),
            pl.BlockSpec((S, LANES), lambda b, h: (b, 0)),
            seq,
            pl.BlockSpec((CONV_WIDTH, hd), lambda b, h: (0, h)),
            pl.BlockSpec((1, hd), lambda b, h: (0, h)),
            pl.BlockSpec((None, hd, hd), lambda b, h: (h, 0, 0)),
            pl.BlockSpec((None, hd, hd), lambda b, h: (h, 0, 0)),
            pl.BlockSpec((1, hd), lambda b, h: (0, h)),
        ],
        out_specs=seq,
        scratch_shapes=[
            pltpu.VMEM((S + 8, hd), F32),
            pltpu.VMEM((S, hd), BF16),
            pltpu.VMEM((S, hd), BF16),
            pltpu.VMEM((2 * hd, hd), F32),
            pltpu.VMEM((1, 1), F32),
        ],
        compiler_params=_params(2), name="mlstm",
    )(b_i, b_f, z, vt3, g3, gn, og, conv_w, conv_b, wq, wk, g_out)


def _post_body(ao_ref, mo_ref, x_ref, gt1_ref, gattn_ref, wout_ref, g2_ref, sc2_ref, sh2_ref, gt2_ref,
               wrt_ref, wgu_ref, wds_ref, base_ref, h2_ref, lg_ref):
    ao = ao_ref[...].astype(F32)
    aon = ao * lax.rsqrt(jnp.mean(ao * ao, axis=-1, keepdims=True) + RMS_EPS) * gattn_ref[...]
    mix_in = jnp.concatenate([aon.astype(BF16), mo_ref[...]], axis=1)
    x1 = x_ref[...] + gt1_ref[...] * _dot(mix_in, wout_ref[...])
    hn = x1 * lax.rsqrt(jnp.mean(x1 * x1, axis=-1, keepdims=True) + RMS_EPS) * g2_ref[...]
    h2 = hn * sc2_ref[...] + sh2_ref[...]
    words = _pack_rows(h2)
    h2_ref[0] = words[:, :HALF]
    h2_ref[1] = words[:, HALF:]
    h2b = h2.astype(BF16)
    gu = _dot(h2b, wgu_ref[...])
    act = (_silu(gu[:, :SHARED_DIM]) * gu[:, SHARED_DIM:]).astype(BF16)
    base_ref[...] = x1 + gt2_ref[...] * _dot(act, wds_ref[...])
    lg_ref[...] = _dot_nt(wrt_ref[...], h2b)


def _post(ao, mo, x2, gt1, gattn, wout, g2, sc2p, sh2, gt2, wrt, wgu, wds, S):
    T = x2.shape[0]
    tm = WIDE_TILE
    per_b = S // tm
    full = lambda a: pl.BlockSpec(a.shape, lambda i: (0,) * a.ndim)
    row = lambda w: pl.BlockSpec((tm, w), lambda i: (i, 0))
    bmod = pl.BlockSpec((None, 1, D_MODEL), lambda i: (i // per_b, 0, 0))
    return pl.pallas_call(
        _post_body,
        out_shape=(jax.ShapeDtypeStruct((T, D_MODEL), F32), jax.ShapeDtypeStruct((2, T, HALF), U32),
                   jax.ShapeDtypeStruct((N_EXPERTS, T), F32)),
        grid=(T // tm,),
        in_specs=[row(MLA_HEADS * V_DIM), row(MLSTM_INNER), row(D_MODEL), bmod, full(gattn), full(wout),
                  full(g2), bmod, bmod, bmod, full(wrt), full(wgu), full(wds)],
        out_specs=(row(D_MODEL), pl.BlockSpec((2, tm, HALF), lambda i: (0, i, 0)),
                   pl.BlockSpec((N_EXPERTS, tm), lambda i: (0, i))),
        compiler_params=_params(1), name="post",
    )(ao, mo, x2, gt1, gattn, wout, g2, sc2p, sh2, gt2, wrt, wgu, wds)


def _first_max(x, rows, n):
    mx = jnp.max(x, axis=0, keepdims=True)
    idx = jnp.min(jnp.where(x == mx, rows, n), axis=0, keepdims=True)
    return mx, idx


def _route_body(lg_ref, bias_ref, idx_ref, w_ref, rank_ref, cnt_ref, carry):
    i = pl.program_id(0)
    tt = lg_ref.shape[1]

    @pl.when(i == 0)
    def _():
        carry[...] = jnp.zeros_like(carry)

    sc = _sigmoid(lg_ref[...])
    bi = sc + bias_ref[...]
    rows = lax.broadcasted_iota(I32, (N_EXPERTS, tt), 0)
    rows_g = lax.broadcasted_iota(I32, (GROUP_SIZE, tt), 0)
    rows_n = lax.broadcasted_iota(I32, (N_GROUPS, tt), 0)
    neg = -jnp.inf
    gs = []
    for g in range(N_GROUPS):
        xg = bi[g * GROUP_SIZE:(g + 1) * GROUP_SIZE]
        m1, i1 = _first_max(xg, rows_g, GROUP_SIZE)
        m2 = jnp.max(jnp.where(rows_g == i1, neg, xg), axis=0, keepdims=True)
        gs.append(m1 + m2)
    cur = jnp.concatenate(gs, axis=0)
    keep = jnp.zeros((N_GROUPS, tt), F32)
    for _ in range(TOPK_GROUPS):
        _, gi = _first_max(cur, rows_n, N_GROUPS)
        sel = rows_n == gi
        keep = jnp.where(sel, 1.0, keep)
        cur = jnp.where(sel, neg, cur)
    masked = jnp.concatenate(
        [jnp.where(keep[g:g + 1] > 0.0, bi[g * GROUP_SIZE:(g + 1) * GROUP_SIZE], neg) for g in range(N_GROUPS)], axis=0)
    cur = masked
    onehot = jnp.zeros((N_EXPERTS, tt), F32)
    idxs, ws = [], []
    for _ in range(TOP_K):
        _, ei = _first_max(cur, rows, N_EXPERTS)
        sel = rows == ei
        ws.append(jnp.sum(jnp.where(sel, sc, 0.0), axis=0, keepdims=True))
        idxs.append(ei)
        cur = jnp.where(sel, neg, cur)
        onehot = jnp.where(sel, 1.0, onehot)
    wsum = ws[0]
    for k in range(1, TOP_K):
        wsum = wsum + ws[k]
    idx_ref[...] = jnp.concatenate(idxs, axis=0)
    w_ref[...] = jnp.concatenate([w / wsum * ROUTED_SCALE for w in ws], axis=0)
    r_i = lax.broadcasted_iota(I32, (tt, tt), 0)
    c_i = lax.broadcasted_iota(I32, (tt, tt), 1)
    before = jnp.where(r_i < c_i, 1.0, 0.0).astype(BF16)
    tot = _dot(onehot.astype(BF16), before) + carry[...]
    rank_ref[...] = jnp.concatenate(
        [jnp.sum(jnp.where(rows == ei, tot, 0.0), axis=0, keepdims=True) for ei in idxs], axis=0).astype(I32)
    carry[...] = carry[...] + jnp.sum(onehot, axis=1, keepdims=True)
    cnt_ref[...] = jnp.broadcast_to(carry[...], cnt_ref.shape)


def _route(lgt, bias_col):
    T = lgt.shape[1]
    tt = WIDE_TILE
    blk = pl.BlockSpec((TOP_K, tt), lambda i: (0, i))
    return pl.pallas_call(
        _route_body,
        out_shape=(jax.ShapeDtypeStruct((TOP_K, T), I32), jax.ShapeDtypeStruct((TOP_K, T), F32),
                   jax.ShapeDtypeStruct((TOP_K, T), I32), jax.ShapeDtypeStruct((N_EXPERTS, LANES), F32)),
        grid=(T // tt,),
        in_specs=[pl.BlockSpec((N_EXPERTS, tt), lambda i: (0, i)), pl.BlockSpec((N_EXPERTS, 1), lambda i: (0, 0))],
        out_specs=(blk, blk, blk, pl.BlockSpec((N_EXPERTS, LANES), lambda i: (0, 0))),
        scratch_shapes=[pltpu.VMEM((N_EXPERTS, 1), F32)],
        compiler_params=_params(1), name="route",
    )(lgt, bias_col)


def _dest_body(idx_ref, rank_ref, off_ref, dest_ref):
    tt = idx_ref.shape[1]
    rows = lax.broadcasted_iota(I32, (N_EXPERTS, tt), 0)
    off = off_ref[...]
    outs = []
    for k in range(TOP_K):
        sel = rows == idx_ref[k:k + 1, :]
        outs.append(jnp.sum(jnp.where(sel, off, 0), axis=0, keepdims=True) + rank_ref[k:k + 1, :])
    dest_ref[...] = jnp.concatenate(outs, axis=0)


def _dest(idx_t, rank_t, off_col):
    T = idx_t.shape[1]
    tt = min(4 * SEQ_TILE, T)
    blk = pl.BlockSpec((TOP_K, tt), lambda i: (0, i))
    return pl.pallas_call(
        _dest_body, out_shape=jax.ShapeDtypeStruct((TOP_K, T), I32), grid=(T // tt,),
        in_specs=[blk, blk, pl.BlockSpec((N_EXPERTS, 1), lambda i: (0, 0))], out_specs=blk,
        compiler_params=_params(1), name="dest",
    )(idx_t, rank_t, off_col)


SC_WINDOW = 128
GATHER_STREAMS = 4
HALF = PACKED // 2


def _sc_mesh():
    return plsc.VectorSubcoreMesh(core_axis_name="c", subcore_axis_name="s")


def _scatter_rows(dest2_t, rows, n_out):
    n = rows.shape[0]

    @pl.kernel(out_type=jax.ShapeDtypeStruct((n_out, HALF), U32), mesh=_sc_mesh(), scratch_types=[])
    def scatter(d_hbm, x_hbm, o_hbm):
        def body(d_vmem, x_vmem):
            for k in range(TOP_K):
                pltpu.sync_copy(x_vmem, o_hbm.at[d_vmem.at[k]])

        pltpu.emit_pipeline(
            body, grid=(n // SC_WINDOW,),
            in_specs=[pl.BlockSpec((TOP_K, SC_WINDOW), lambda i: (0, i)),
                      pl.BlockSpec((SC_WINDOW, HALF), lambda i: (i, 0))],
            out_specs=[], core_axis_name=("c", "s"), dimension_semantics=(pltpu.PARALLEL,),
        )(d_hbm, x_hbm)

    return scatter(dest2_t, rows)


def _gather_rows(index_row, rows):
    n = index_row.shape[1]

    @pl.kernel(out_type=jax.ShapeDtypeStruct((n, HALF), U32), mesh=_sc_mesh(),
               scratch_types=[pltpu.SemaphoreType.DMA((GATHER_STREAMS,))])
    def gather(i_hbm, y_hbm, o_hbm, sems):
        def body(i_vmem, o_vmem):
            part = SC_WINDOW // GATHER_STREAMS
            copies = [pltpu.async_copy(y_hbm.at[i_vmem.at[0, pl.ds(s * part, part)]],
                                       o_vmem.at[pl.ds(s * part, part)], sems.at[s]) for s in range(GATHER_STREAMS)]
            for cp in copies:
                cp.wait()

        pltpu.emit_pipeline(
            body, grid=(n // SC_WINDOW,),
            in_specs=[pl.BlockSpec((1, SC_WINDOW), lambda i: (0, i))],
            out_specs=[pl.BlockSpec((SC_WINDOW, HALF), lambda i: (i, 0))],
            core_axis_name=("c", "s"), dimension_semantics=(pltpu.PARALLEL,),
        )(i_hbm, o_hbm)

    return gather(index_row, rows)


EXPERT_RING = 16
WEIGHT_RING = 3


def _experts_body(first_ref, nblk_ref, cnt_ref, xs_ref, wg_ref, wu_ref, wd_ref, y_ref,
                  xbuf, ybuf, wgf, wuf, wdf, wgb, wub, wdb, xsem, ysem, wsem):
    e = pl.program_id(0)
    last = pl.num_programs(0) - 1
    wring = wgf.shape[0]

    def w_copies(ex):
        slot = ex % wring
        return [pltpu.make_async_copy(src.at[ex], dst.at[slot], wsem.at[slot])
                for src, dst in ((wg_ref, wgf), (wu_ref, wuf), (wd_ref, wdf))]

    @pl.when(e == 0)
    def _():
        for ex in range(wring - 1):
            for cp in w_copies(ex):
                cp.start()

    b0 = first_ref[e]
    nb = nblk_ref[e]
    total = first_ref[last] + nblk_ref[last]
    ring = xbuf.shape[0]

    def rows(g):
        return pl.ds(pl.multiple_of(g * ROW_BLOCK, ROW_BLOCK), ROW_BLOCK)

    def x_copy(g):
        slot = g % ring
        return pltpu.make_async_copy(xs_ref.at[:, rows(g), :], xbuf.at[slot], xsem.at[slot])

    def y_copy(g):
        slot = g % ring
        return pltpu.make_async_copy(ybuf.at[slot], y_ref.at[:, rows(g), :], ysem.at[slot])

    @pl.when(e == 0)
    def _():
        for g in range(ring - 1):
            @pl.when(g < total)
            def _():
                x_copy(g).start()

    for cp in w_copies(e):
        cp.wait()

    @pl.when(nb > 0)
    def _():
        slot = e % wring
        wgb[...] = wgf[slot].astype(BF16)
        wub[...] = wuf[slot].astype(BF16)
        wdb[...] = wdf[slot].astype(BF16)

    @pl.when(e + wring - 1 <= last)
    def _():
        for cp in w_copies(e + wring - 1):
            cp.start()

    def prefetch(g):
        @pl.when(g + ring - 1 < total)
        def _():
            x_copy(g + ring - 1).start()

    def compute(i):
        slot = (b0 + i) % ring
        words = jnp.concatenate([xbuf[slot, 0], xbuf[slot, 1]], axis=1)
        n_live = cnt_ref[e] - i * ROW_BLOCK
        words = jnp.where(lax.broadcasted_iota(I32, words.shape, 0) < n_live, words, U32(0))
        x = jnp.concatenate(_unpack_rows(words), axis=1).astype(BF16)
        act = (_silu(_dot(x, wgb[...])) * _dot(x, wub[...])).astype(BF16)
        y = _pack_rows(_dot(act, wdb[...]))
        ybuf[slot, 0] = y[:, :HALF]
        ybuf[slot, 1] = y[:, HALF:]

    def blocks(i, n):
        g0 = b0 + i
        prefetch(g0)
        for d in range(n):
            x_copy(g0 + d).wait()

            @pl.when(g0 + d >= ring)
            def _():
                y_copy(g0 + d - ring).wait()

        for d in range(n):
            compute(i + d)
        for d in range(1, n):
            prefetch(g0 + d)
        for d in range(n):
            y_copy(g0 + d).start()

    def pair(ii, carry):
        blocks(2 * ii, 2)
        return carry

    def single(_, carry):
        blocks(nb - 1, 1)
        return carry

    lax.fori_loop(0, nb // 2, pair, 0)
    lax.fori_loop(0, nb % 2, single, 0)

    @pl.when(e == last)
    def _():
        for back in range(ring, 0, -1):
            @pl.when(total >= back)
            def _():
                y_copy(total - back).wait()


def _experts(first_block, n_blocks, counts, xs, wg, wu, wd):
    n_rows = xs.shape[1]
    anyspec = pl.BlockSpec(memory_space=pl.ANY)
    ring = pltpu.VMEM((EXPERT_RING, 2, ROW_BLOCK, HALF), U32)
    grid_spec = pltpu.PrefetchScalarGridSpec(
        num_scalar_prefetch=3, grid=(N_EXPERTS,),
        in_specs=[anyspec] * 4,
        out_specs=anyspec,
        scratch_shapes=[ring, ring,
                        pltpu.VMEM((WEIGHT_RING, D_MODEL, EXPERT_DIM), F32),
                        pltpu.VMEM((WEIGHT_RING, D_MODEL, EXPERT_DIM), F32),
                        pltpu.VMEM((WEIGHT_RING, EXPERT_DIM, D_MODEL), F32),
                        pltpu.VMEM((D_MODEL, EXPERT_DIM), BF16), pltpu.VMEM((D_MODEL, EXPERT_DIM), BF16),
                        pltpu.VMEM((EXPERT_DIM, D_MODEL), BF16),
                        pltpu.SemaphoreType.DMA((EXPERT_RING,)), pltpu.SemaphoreType.DMA((EXPERT_RING,)),
                        pltpu.SemaphoreType.DMA((WEIGHT_RING,))],
    )
    return pl.pallas_call(
        _experts_body, out_shape=jax.ShapeDtypeStruct((2, n_rows, HALF), U32), grid_spec=grid_spec,
        compiler_params=_params(1), name="experts",
    )(first_block, n_blocks, counts, xs, wg, wu, wd)


def _combine_body(*refs):
    ya_refs = refs[:TOP_K]
    yb_refs = refs[TOP_K:2 * TOP_K]
    base_ref, w_ref, gt2_ref, out_ref = refs[2 * TOP_K:]
    w = w_ref[...]
    r_lo = r_hi = None
    for k in range(TOP_K):
        lo, hi = _unpack_rows(jnp.concatenate([ya_refs[k][...], yb_refs[k][...]], axis=1))
        wk = w[:, k:k + 1]
        r_lo = wk * lo if r_lo is None else r_lo + wk * lo
        r_hi = wk * hi if r_hi is None else r_hi + wk * hi
    out_ref[...] = base_ref[...] + gt2_ref[...] * jnp.concatenate([r_lo, r_hi], axis=1)


def _combine(y_slots, base, w_tk, gt2, S):
    T = base.shape[0]
    tt = WIDE_TILE
    nt = T // tt
    per_b = S // tt
    yspec = lambda half, k: pl.BlockSpec((tt, HALF), lambda i: ((half * TOP_K + k) * nt + i, 0))
    return pl.pallas_call(
        _combine_body, out_shape=jax.ShapeDtypeStruct((T, D_MODEL), F32), grid=(nt,),
        in_specs=[yspec(half, k) for half in range(2) for k in range(TOP_K)] + [
            pl.BlockSpec((tt, D_MODEL), lambda i: (i, 0)),
            pl.BlockSpec((tt, TOP_K), lambda i: (i, 0)),
            pl.BlockSpec((None, 1, D_MODEL), lambda i: (i // per_b, 0, 0)),
        ],
        out_specs=pl.BlockSpec((tt, D_MODEL), lambda i: (i, 0)),
        compiler_params=_params(1), name="combine",
    )(*([y_slots] * (2 * TOP_K)), base, w_tk, gt2)


def _pad_heads(w, width):
    kdim = w.shape[0]
    w3 = w.reshape(kdim, MLA_HEADS, width)
    return jnp.pad(w3, ((0, 0), (0, 0), (0, LANES - width))).reshape(kdim, MLA_HEADS * LANES)


def _rot_partner(w_rope):
    half = QK_ROPE // 2
    return jnp.concatenate([-w_rope[..., half:], w_rope[..., :half]], axis=-1)


def _lane_vec(nope, rope):
    return jnp.concatenate([nope, rope, jnp.zeros((LANES - QK_DIM,), F32)]).reshape(1, LANES)


def _layer(x2, ada, positions, B, S, g_norm1, w_in, g_cq, w_uq, g_ckv, w_ukv, g_qn, g_kn, g_attn_out,
           conv_w, conv_b, w_mq, w_mk, b_igate, b_fgate, g_mlstm_out, w_out, g_norm2, w_router, router_bias,
           w_gate_exp, w_up_exp, w_down_exp, w_gate_sh, w_up_sh, w_down_sh):
    T = B * S
    sh1, sc1, gt1, sh2, sc2, gt2 = [a.reshape(B, 1, D_MODEL) for a in jnp.split(ada, 6, axis=-1)]
    row = lambda v: v.reshape(1, -1).astype(F32)

    inv_freq = ROPE_THETA ** (-jnp.arange(0, QK_ROPE, 2, dtype=F32) / QK_ROPE)
    ang = positions.astype(F32).reshape(T, 1) * inv_freq
    half = QK_ROPE // 2
    freq_lane = jnp.arange(half)[:, None]
    lane = jnp.arange(LANES)[None, :]
    place = ((lane == QK_NOPE + freq_lane) | (lane == QK_NOPE + half + freq_lane)).astype(F32)
    spread = lambda tbl: jnp.dot(tbl, place, precision=lax.Precision.HIGHEST)
    cosp = (jnp.arange(LANES) < QK_NOPE).astype(F32)[None, :] + spread(jnp.cos(ang))
    sinp = spread(jnp.sin(ang))

    o0 = 0
    w_cq = w_in[:, o0:o0 + Q_LORA]; o0 += Q_LORA
    w_ckv = w_in[:, o0:o0 + KV_LORA]; o0 += KV_LORA
    w_kr = w_in[:, o0:o0 + QK_ROPE]; o0 += QK_ROPE
    w_z = w_in[:, o0:o0 + MLSTM_INNER]; o0 += MLSTM_INNER
    w_v = w_in[:, o0:o0 + MLSTM_INNER]; o0 += MLSTM_INNER
    w_o = w_in[:, o0:o0 + MLSTM_INNER]; o0 += MLSTM_INNER
    w_i = w_in[:, o0:o0 + MLSTM_HEADS]; o0 += MLSTM_HEADS
    w_f = w_in[:, o0:o0 + MLSTM_HEADS]
    zl = jnp.zeros((D_MODEL, QK_NOPE), F32)
    zr = jnp.zeros((D_MODEL, LANES - QK_DIM), F32)
    zg = jnp.zeros((D_MODEL, LANES - 2 * MLSTM_HEADS), F32)
    wa = jnp.concatenate([w_cq, w_ckv, w_z, w_o, zl, w_kr, zr, zl, _rot_partner(w_kr), zr, w_i, w_f, zg],
                         axis=1).astype(BF16)
    wt = jnp.concatenate([w_v.T, w_i.T, w_f.T, jnp.zeros((16 - 2 * MLSTM_HEADS, D_MODEL), F32)], axis=0).astype(BF16)

    uq3 = w_uq.reshape(Q_LORA, MLA_HEADS, QK_DIM)
    uq_rot = jnp.concatenate([jnp.zeros((Q_LORA, MLA_HEADS, QK_NOPE), F32), _rot_partner(uq3[..., QK_NOPE:])], axis=-1)
    wuq = _pad_heads(w_uq, QK_DIM).astype(BF16)
    wuqs = _pad_heads(uq_rot.reshape(Q_LORA, MLA_HEADS * QK_DIM), QK_DIM).astype(BF16)
    ukv3 = w_ukv.reshape(KV_LORA, MLA_HEADS, QK_NOPE + V_DIM)
    wuk = _pad_heads(ukv3[..., :QK_NOPE].reshape(KV_LORA, MLA_HEADS * QK_NOPE), QK_NOPE).astype(BF16)
    wuv = _pad_heads(ukv3[..., QK_NOPE:].reshape(KV_LORA, MLA_HEADS * V_DIM), V_DIM).astype(BF16)
    vone = jnp.tile((jnp.arange(LANES) == V_DIM).astype(F32), MLA_HEADS).reshape(1, MLA_HEADS * LANES)
    swap = lambda g_rope: jnp.concatenate([g_rope[half:], g_rope[:half]])
    no_nope = jnp.zeros((QK_NOPE,), F32)
    gq = _lane_vec(g_qn[:QK_NOPE], g_qn[QK_NOPE:])
    gqs = _lane_vec(no_nope, swap(g_qn[QK_NOPE:]))
    gk = _lane_vec(g_kn[:QK_NOPE], g_kn[QK_NOPE:])
    gks = _lane_vec(no_nope, swap(g_kn[QK_NOPE:]))

    z, og, vt3, g3, gn, q, k, mv = _inproj(
        x2, 1.0 + sc1, sh1, row(g_norm1), wa, wt, row(g_cq), wuq, wuqs, row(g_ckv), wuk, wuv, vone,
        gq, gqs, gk, gks, cosp, sinp, S)
    attn_o = _attn(q, k, mv, B, S)
    mlstm_o = _mlstm(z, vt3, g3, gn, og, conv_w.reshape(CONV_WIDTH, MLSTM_INNER), row(conv_b),
                     w_mq.astype(BF16), w_mk.astype(BF16), b_igate, b_fgate, row(g_mlstm_out), B, S)

    wgu = jnp.concatenate([w_gate_sh, w_up_sh], axis=1).astype(BF16)
    base, h2, lgt = _post(attn_o, mlstm_o, x2, gt1, row(g_attn_out), w_out.astype(BF16), row(g_norm2),
                          1.0 + sc2, sh2, gt2, w_router.T.astype(BF16), wgu, w_down_sh.astype(BF16), S)

    idx_t, w_t, rank_t, cnt = _route(lgt, router_bias.reshape(N_EXPERTS, 1))
    counts = cnt[:, 0].astype(I32)
    padded = (counts + ROW_BLOCK - 1) // ROW_BLOCK * ROW_BLOCK
    padded_end = jnp.cumsum(padded)
    off = padded_end - padded
    n_rows = T * TOP_K + N_EXPERTS * ROW_BLOCK
    dest_t = _dest(idx_t, rank_t, off.reshape(N_EXPERTS, 1))

    dest2_t = jnp.concatenate([dest_t, dest_t + n_rows], axis=1)
    xs = _scatter_rows(dest2_t, h2.reshape(2 * T, HALF), 2 * n_rows).reshape(2, n_rows, HALF)
    y = _experts(off // ROW_BLOCK, padded // ROW_BLOCK, counts, xs, w_gate_exp, w_up_exp, w_down_exp)
    dest_row = dest_t.reshape(1, TOP_K * T)
    y_slots = _gather_rows(jnp.concatenate([dest_row, dest_row + n_rows], axis=1), y.reshape(2 * n_rows, HALF))
    return _combine(y_slots, base, w_t.T, gt2, S)


def kernel(x, c, positions, w_ada, b_ada, g_norm1, w_in, g_cq, w_uq, g_ckv, w_ukv, g_qn, g_kn, g_attn_out, conv_w, conv_b, w_mq, w_mk, b_igate, b_fgate, g_mlstm_out, w_out, g_norm2, w_router, router_bias, w_gate_exp, w_up_exp, w_down_exp, w_gate_sh, w_up_sh, w_down_sh):
    B, S, d_model = x.shape
    assert d_model == D_MODEL and S % WIDE_TILE == 0 and S % SEQ_TILE == 0, (B, S, d_model)
    assert w_gate_exp.shape[1:] == (N_EXPERTS, D_MODEL, EXPERT_DIM), w_gate_exp.shape
    depth = w_ada.shape[0]
    x2 = x.reshape(B * S, D_MODEL)
    for l in range(depth):
        ada = _ada(c, w_ada[l], b_ada[l])
        x2 = _layer(x2, ada, positions, B, S, g_norm1[l], w_in[l], g_cq[l], w_uq[l], g_ckv[l], w_ukv[l], g_qn[l],
                    g_kn[l], g_attn_out[l], conv_w[l], conv_b[l], w_mq[l], w_mk[l], b_igate[l], b_fgate[l],
                    g_mlstm_out[l], w_out[l], g_norm2[l], w_router[l], router_bias[l], w_gate_exp[l], w_up_exp[l],
                    w_down_exp[l], w_gate_sh[l], w_up_sh[l], w_down_sh[l])
    return x2.reshape(B, S, D_MODEL)
```

```python
import functools
import math

import jax
import jax.numpy as jnp
from jax import lax
from jax.experimental import pallas as pl
from jax.experimental.pallas import tpu as pltpu
from jax.experimental.pallas import tpu_sc as plsc

F32 = jnp.float32
BF16 = jnp.bfloat16
I32 = jnp.int32

D_MODEL = 1024
MLA_HEADS = 8
QK_NOPE = 64
QK_ROPE = 32
QK_DIM = QK_NOPE + QK_ROPE
V_DIM = 64
Q_LORA = 384
KV_LORA = 256
ROPE_THETA = 10000.0
MLSTM_HEADS = 4
MLSTM_DIM = 128
MLSTM_INNER = MLSTM_HEADS * MLSTM_DIM
CONV_WIDTH = 4
N_EXPERTS = 256
TOP_K = 8
N_GROUPS = 8
TOPK_GROUPS = 4
GROUP_SIZE = N_EXPERTS // N_GROUPS
EXPERT_DIM = 256
SHARED_DIM = 256
ROUTED_SCALE = 2.5
RMS_EPS = 1e-6

LANES = 128
SEQ_TILE = 256
WIDE_TILE = 512
ROW_BLOCK = 256
VMEM_LIMIT = 56 * 1024 * 1024

_OFF_CQ = 0
_OFF_CKV = _OFF_CQ + Q_LORA
_OFF_Z = _OFF_CKV + KV_LORA
_OFF_O = _OFF_Z + MLSTM_INNER
_OFF_KR = _OFF_O + MLSTM_INNER
_OFF_KRS = _OFF_KR + LANES
_OFF_G = _OFF_KRS + LANES
_WA_COLS = _OFF_G + LANES
_WT_ROWS = MLSTM_INNER + 16


def _params(n_axes, vmem=VMEM_LIMIT):
    return pltpu.CompilerParams(dimension_semantics=("arbitrary",) * n_axes, vmem_limit_bytes=vmem)


def _dot(a, b):
    return jnp.dot(a, b, preferred_element_type=F32)


def _dot_nt(a, b):
    return lax.dot_general(a, b, (((1,), (1,)), ((), ())), preferred_element_type=F32)


def _sigmoid(x):
    return 1.0 / (1.0 + jnp.exp(-x))


def _silu(x):
    return x * _sigmoid(x)


U32 = jnp.uint32
PACKED = D_MODEL // 2


def _pack_rows(x):
    w = x.shape[1] // 2
    lo = lax.bitcast_convert_type(x[:, :w].astype(BF16).astype(F32), U32)
    hi = lax.bitcast_convert_type(x[:, w:].astype(BF16).astype(F32), U32)
    return lax.shift_right_logical(lo, U32(16)) | (hi & U32(0xFFFF0000))


def _unpack_rows(p):
    lo = lax.bitcast_convert_type(lax.shift_left(p, U32(16)), F32)
    hi = lax.bitcast_convert_type(p & U32(0xFFFF0000), F32)
    return lo, hi


def _ada_body(c_ref, w_ref, b_ref, o_ref):
    c = c_ref[...]
    cond = _silu(c).astype(BF16)
    o_ref[...] = _dot(cond, w_ref[...].astype(BF16)) + b_ref[...]


def _ada(c, w_ada, b_ada):
    B = c.shape[0]
    n = w_ada.shape[1]
    tn = 1024
    return pl.pallas_call(
        _ada_body,
        out_shape=jax.ShapeDtypeStruct((B, n), F32),
        grid=(n // tn,),
        in_specs=[
            pl.BlockSpec((B, D_MODEL), lambda j: (0, 0)),
            pl.BlockSpec((D_MODEL, tn), lambda j: (0, j)),
            pl.BlockSpec((1, tn), lambda j: (0, j)),
        ],
        out_specs=pl.BlockSpec((B, tn), lambda j: (0, j)),
        compiler_params=_params(1),
        name="ada",
    )(c, w_ada, b_ada.reshape(1, n))


def _inproj_body(x_ref, sc_ref, sh_ref, g1_ref, wa_ref, wt_ref, gcq_ref, wuq_ref, wuqs_ref, gckv_ref,
                 wuk_ref, wuv_ref, vone_ref, gq_ref, gqs_ref, gk_ref, gks_ref, cos_ref, sin_ref,
                 z_ref, o_ref, vt_ref, gt_ref, gn_ref, q_ref, k_ref, mv_ref):
    x = x_ref[...]
    hn = x * lax.rsqrt(jnp.mean(x * x, axis=-1, keepdims=True) + RMS_EPS) * g1_ref[...]
    hm = (hn * sc_ref[...] + sh_ref[...]).astype(BF16)
    p = _dot(hm, wa_ref[...])
    z_ref[...] = p[:, _OFF_Z:_OFF_Z + MLSTM_INNER].astype(BF16)
    o_ref[...] = p[:, _OFF_O:_OFF_O + MLSTM_INNER].astype(BF16)
    gn_ref[...] = p[:, _OFF_G:_OFF_G + LANES]
    for part in range(vt_ref.shape[0]):
        rt = _dot_nt(wt_ref[...], hm[part * SEQ_TILE:(part + 1) * SEQ_TILE])
        vt_ref[part] = rt[:MLSTM_INNER].astype(BF16)
        gt_ref[part] = rt[MLSTM_INNER:]

    cq = p[:, _OFF_CQ:_OFF_CQ + Q_LORA]
    cqn = (cq * lax.rsqrt(jnp.mean(cq * cq, axis=-1, keepdims=True) + RMS_EPS) * gcq_ref[...]).astype(BF16)
    ckv = p[:, _OFF_CKV:_OFF_CKV + KV_LORA]
    ckvn = (ckv * lax.rsqrt(jnp.mean(ckv * ckv, axis=-1, keepdims=True) + RMS_EPS) * gckv_ref[...]).astype(BF16)
    qa = _dot(cqn, wuq_ref[...])
    qb = _dot(cqn, wuqs_ref[...])
    ka = _dot(ckvn, wuk_ref[...])
    mv_ref[...] = (_dot(ckvn, wuv_ref[...]) + vone_ref[...]).astype(BF16)
    kr = p[:, _OFF_KR:_OFF_KR + LANES]
    krs = p[:, _OFF_KRS:_OFF_KRS + LANES]

    cosp = cos_ref[...]
    sinp = sin_ref[...]
    gqc = gq_ref[...] * cosp
    gqs = gqs_ref[...] * sinp
    gkc = gk_ref[...] * cosp
    k_rot = krs * (gks_ref[...] * sinp)
    kr_ss = jnp.sum(kr * kr, axis=-1, keepdims=True)
    q_scale = (QK_DIM ** -0.5) * math.log2(math.e)
    for h in range(MLA_HEADS):
        sl = slice(h * LANES, (h + 1) * LANES)
        qah = qa[:, sl]
        rq = lax.rsqrt(jnp.sum(qah * qah, axis=-1, keepdims=True) * (1.0 / QK_DIM) + RMS_EPS) * q_scale
        q_ref[:, sl] = ((qah * gqc + qb[:, sl] * gqs) * rq).astype(BF16)
        kah = ka[:, sl]
        rk = lax.rsqrt((jnp.sum(kah * kah, axis=-1, keepdims=True) + kr_ss) * (1.0 / QK_DIM) + RMS_EPS)
        k_ref[:, sl] = (((kah + kr) * gkc + k_rot) * rk).astype(BF16)


def _inproj(x2, sc1p, sh1, g1, wa, wt, gcq, wuq, wuqs, gckv, wuk, wuv, vone, gq, gqs, gk, gks, cosp, sinp, S):
    T = x2.shape[0]
    tm = WIDE_TILE
    nt = T // tm
    per_b = S // tm
    sub = tm // SEQ_TILE
    hp = MLA_HEADS * LANES
    full = lambda a: pl.BlockSpec(a.shape, lambda i: (0,) * a.ndim)
    row = lambda w: pl.BlockSpec((tm, w), lambda i: (i, 0))
    bmod = pl.BlockSpec((None, 1, D_MODEL), lambda i: (i // per_b, 0, 0))
    out_shapes = (
        jax.ShapeDtypeStruct((T, MLSTM_INNER), BF16),
        jax.ShapeDtypeStruct((T, MLSTM_INNER), BF16),
        jax.ShapeDtypeStruct((T // SEQ_TILE, MLSTM_INNER, SEQ_TILE), BF16),
        jax.ShapeDtypeStruct((T // SEQ_TILE, 16, SEQ_TILE), F32),
        jax.ShapeDtypeStruct((T, LANES), F32),
        jax.ShapeDtypeStruct((T, hp), BF16),
        jax.ShapeDtypeStruct((T, hp), BF16),
        jax.ShapeDtypeStruct((T, hp), BF16),
    )
    out_specs = (
        row(MLSTM_INNER), row(MLSTM_INNER),
        pl.BlockSpec((sub, MLSTM_INNER, SEQ_TILE), lambda i: (i, 0, 0)),
        pl.BlockSpec((sub, 16, SEQ_TILE), lambda i: (i, 0, 0)),
        row(LANES), row(hp), row(hp), row(hp),
    )
    in_specs = [row(D_MODEL), bmod, bmod, full(g1), full(wa), full(wt), full(gcq), full(wuq), full(wuqs),
                full(gckv), full(wuk), full(wuv), full(vone), full(gq), full(gqs), full(gk), full(gks),
                row(LANES), row(LANES)]
    return pl.pallas_call(
        _inproj_body, out_shape=out_shapes, grid=(nt,), in_specs=in_specs, out_specs=out_specs,
        compiler_params=_params(1), name="inproj",
    )(x2, sc1p, sh1, g1, wa, wt, gcq, wuq, wuqs, gckv, wuk, wuv, vone, gq, gqs, gk, gks, cosp, sinp)


ATTN_HEADS_PER_STEP = 8


def _pairs_loop(n, fn, init):
    c = lax.fori_loop(0, n // 2, lambda jj, c: fn(2 * jj + 1, fn(2 * jj, c)), init)
    return lax.fori_loop(0, n % 2, lambda _, c: fn(n - 1, c), c)


def _attn_body(q_ref, k_ref, v_ref, o_ref, s_scr):
    i = pl.program_id(2)
    t = SEQ_TILE
    nh = ATTN_HEADS_PER_STEP
    qry_pos = lax.broadcasted_iota(I32, (t, t), 0)
    key_pos = lax.broadcasted_iota(I32, (t, t), 1)
    causal = key_pos <= qry_pos
    qs = [q_ref[:, hh * LANES:(hh + 1) * LANES] for hh in range(nh)]

    def scores(j, hh, diag):
        kj = k_ref[pl.ds(pl.multiple_of(j * t, t), t), hh * LANES:(hh + 1) * LANES]
        s = _dot_nt(qs[hh], kj)
        return jnp.where(causal, s, -jnp.inf) if diag else s

    def halves(x):
        return x[:, :LANES], x[:, LANES:]

    def max_pass(j, mx, diag):
        out = []
        for hh in range(nh):
            s = scores(j, hh, diag)
            s_scr[j, hh] = s
            lo, hi = halves(s)
            out.append(jnp.maximum(mx[hh], jnp.maximum(lo, hi)))
        return tuple(out)

    mx = tuple(jnp.full((t, LANES), -jnp.inf, F32) for _ in range(nh))
    mx = _pairs_loop(i, lambda j, c: max_pass(j, c, False), mx)
    mx = max_pass(i, mx, True)
    mb = [jnp.broadcast_to(jnp.max(mx[hh], axis=1, keepdims=True), (t, LANES)) for hh in range(nh)]

    def acc_pass(j, carry, diag):
        row0 = pl.multiple_of(j * t, t)
        out = []
        for hh in range(nh):
            lo, hi = halves(s_scr[j, hh])
            p = jnp.concatenate([jnp.exp2(lo - mb[hh]), jnp.exp2(hi - mb[hh])], axis=1).astype(BF16)
            out.append(carry[hh] + _dot(p, v_ref[pl.ds(row0, t), hh * LANES:(hh + 1) * LANES]))
        return tuple(out)

    init = tuple(jnp.zeros((t, LANES), F32) for _ in range(nh))
    carry = _pairs_loop(i, lambda j, c: acc_pass(j, c, False), init)
    final = acc_pass(i, carry, True)
    outs = [acc / acc[:, V_DIM:V_DIM + 1] for acc in final]
    first_head = lax.broadcasted_iota(I32, (t, LANES), 1) < V_DIM
    o_ref[...] = jnp.concatenate(
        [jnp.where(first_head, outs[2 * pr], pltpu.roll(outs[2 * pr + 1], V_DIM, axis=1)) for pr in range(nh // 2)],
        axis=1).astype(BF16)


def _attn(q, k, mvt, B, S):
    T = q.shape[0]
    t = SEQ_TILE
    nh = ATTN_HEADS_PER_STEP
    per_b = S // t
    return pl.pallas_call(
        _attn_body,
        out_shape=jax.ShapeDtypeStruct((T, MLA_HEADS * V_DIM), BF16),
        grid=(B, MLA_HEADS // nh, per_b),
        in_specs=[
            pl.BlockSpec((t, nh * LANES), lambda b, hp, i: (b * per_b + i, hp)),
            pl.BlockSpec((S, nh * LANES), lambda b, hp, i: (b, hp)),
            pl.BlockSpec((S, nh * LANES), lambda b, hp, i: (b, hp)),
        ],
        out_specs=pl.BlockSpec((t, nh * V_DIM), lambda b, hp, i: (b * per_b + i, hp)),
        scratch_shapes=[pltpu.VMEM((per_b, nh, t, t), F32)],
        compiler_params=_params(3), name="attn",
    )(q, k, mvt)


def _log_sigmoid(x):
    return jnp.minimum(x, 0.0) - jnp.log(1.0 + jnp.exp(-jnp.abs(x)))


def _mlstm_body(bi_ref, bf_ref, z_ref, vt_ref, g_ref, gn_ref, og_ref, cw_ref, cb_ref, wq_ref, wk_ref, gout_ref,
                out_ref, zpad, q_s, k_s, st_s, m_s, *, S):
    h = pl.program_id(1)
    L = SEQ_TILE
    nchunk = S // L
    hd = MLSTM_DIM
    zpad[0:8, :] = jnp.zeros((8, hd), F32)
    zpad[8:, :] = z_ref[...].astype(F32)
    zc = cb_ref[...] + jnp.zeros((S, hd), F32)
    for j in range(CONV_WIDTH):
        zc = zc + cw_ref[j:j + 1, :] * zpad[pl.ds(8 - (CONV_WIDTH - 1) + j, S), :]
    zs = _silu(zc).astype(BF16)
    q_s[...] = _dot(zs, wq_ref[...]).astype(BF16)
    k_s[...] = (_dot(zs, wk_ref[...]) * (hd ** -0.5)).astype(BF16)

    st_s[...] = jnp.zeros_like(st_s)
    m_s[...] = jnp.zeros_like(m_s)
    b_i = bi_ref[h]
    b_f = bf_ref[h]
    r_i = lax.broadcasted_iota(I32, (L, L), 0)
    c_i = lax.broadcasted_iota(I32, (L, L), 1)
    tril = r_i >= c_i
    triu = r_i <= c_i
    f_lane = lax.broadcasted_iota(I32, (L, LANES), 1) == MLSTM_HEADS + h
    ones_row = jnp.where(lax.broadcasted_iota(I32, (hd, L), 0) == 0, 1.0, 0.0).astype(BF16)

    def chunk(c, _):
        start = pl.multiple_of(c * L, L)
        qc = q_s[pl.ds(start, L), :]
        kc = k_s[pl.ds(start, L), :]
        vt_aug = jnp.concatenate([vt_ref[c], ones_row], axis=0)
        li = g_ref[c, pl.ds(h, 1), :] + b_i
        fp = g_ref[c, pl.ds(MLSTM_HEADS + h, 1), :] + b_f
        lf = _log_sigmoid(fp)
        fp_col = jnp.sum(jnp.where(f_lane, gn_ref[pl.ds(start, L), :], 0.0), axis=1, keepdims=True) + b_f
        lf_col = _log_sigmoid(fp_col)
        b_col = jnp.sum(jnp.where(tril, lf, 0.0), axis=1, keepdims=True)
        b_row = jnp.sum(jnp.where(triu, lf_col, 0.0), axis=0, keepdims=True)
        d0 = b_col - b_row
        dmat = jnp.where(tril, d0 + li, -jnp.inf)
        m_prev = m_s[...]
        inter = b_col + m_prev
        mt = jnp.maximum(inter, jnp.max(dmat, axis=1, keepdims=True))
        w = jnp.exp(dmat - mt) * _dot_nt(qc, kc)
        decay = jnp.exp(inter - mt)
        st = st_s[...]
        num_aug = decay * _dot_nt(qc, st.astype(BF16)) + _dot_nt(w.astype(BF16), vt_aug)
        num = num_aug[:, :hd]
        den = num_aug[:, hd:hd + 1]
        hv = num / jnp.maximum(jnp.abs(den), jnp.exp(-mt))
        g = d0[L - 1:L, :] + li
        b_last = b_col[L - 1:L, :]
        m_new = jnp.maximum(b_last + m_prev, jnp.max(g, axis=1, keepdims=True))
        a = jnp.exp(g - m_new)
        cd = jnp.exp(b_last + m_prev - m_new)
        st_s[...] = cd * st + _dot((vt_aug.astype(F32) * a).astype(BF16), kc)
        m_s[...] = m_new
        hn = hv * lax.rsqrt(jnp.mean(hv * hv, axis=-1, keepdims=True) + RMS_EPS) * gout_ref[...]
        og = og_ref[pl.ds(start, L), :].astype(F32)
        out_ref[pl.ds(start, L), :] = (_sigmoid(og) * hn).astype(BF16)
        return 0

    lax.fori_loop(0, nchunk, chunk, 0)


def _mlstm(z, vt3, g3, gn, og, conv_w, conv_b, wq, wk, b_i, b_f, g_out, B, S):
    T = z.shape[0]
    L = SEQ_TILE
    per_b = S // L
    hd = MLSTM_DIM
    smem = pl.BlockSpec(memory_space=pltpu.SMEM)
    seq = pl.BlockSpec((S, hd), lambda b, h: (b, h))
    return pl.pallas_call(
        functools.partial(_mlstm_body, S=S),
        out_shape=jax.ShapeDtypeStruct((T, MLSTM_INNER), BF16),
        grid=(B, MLSTM_HEADS),
        in_specs=[
            smem, smem, seq,
            pl.BlockSpec((per_b, hd, L), lambda b, h: (b, h, 0)),
            pl.BlockSpec((per_b, 16, L), lambda b, h: (b, 0, 0)),
            pl.BlockSpec((S, LANES), lambda b, h: (b, 0)),
            seq,
            pl.BlockSpec((CONV_WIDTH, hd), lambda b, h: (0, h)),
            pl.BlockSpec((1, hd), lambda b, h: (0, h)),
            pl.BlockSpec((None, hd, hd), lambda b, h: (h, 0, 0)),
            pl.BlockSpec((None, hd, hd), lambda b, h: (h, 0, 0)),
            pl.BlockSpec((1, hd), lambda b, h: (0, h)),
        ],
        out_specs=seq,
        scratch_shapes=[
            pltpu.VMEM((S + 8, hd), F32),
            pltpu.VMEM((S, hd), BF16),
            pltpu.VMEM((S, hd), BF16),
            pltpu.VMEM((2 * hd, hd), F32),
            pltpu.VMEM((1, 1), F32),
        ],
        compiler_params=_params(2), name="mlstm",
    )(b_i, b_f, z, vt3, g3, gn, og, conv_w, conv_b, wq, wk, g_out)


def _post_body(ao_ref, mo_ref, x_ref, gt1_ref, gattn_ref, wout_ref, g2_ref, sc2_ref, sh2_ref, gt2_ref,
               wrt_ref, wgu_ref, wds_ref, bias_ref, base_ref, h2_ref, idx_ref, w_ref, rank_ref, cnt_ref, carry):
    ao = ao_ref[...].astype(F32)
    aon = ao * lax.rsqrt(jnp.mean(ao * ao, axis=-1, keepdims=True) + RMS_EPS) * gattn_ref[...]
    mix_in = jnp.concatenate([aon.astype(BF16), mo_ref[...]], axis=1)
    x1 = x_ref[...] + gt1_ref[...] * _dot(mix_in, wout_ref[...])
    hn = x1 * lax.rsqrt(jnp.mean(x1 * x1, axis=-1, keepdims=True) + RMS_EPS) * g2_ref[...]
    h2 = hn * sc2_ref[...] + sh2_ref[...]
    words = _pack_rows(h2)
    h2_ref[0] = words[:, :HALF]
    h2_ref[1] = words[:, HALF:]
    h2b = h2.astype(BF16)
    gu = _dot(h2b, wgu_ref[...])
    act = (_silu(gu[:, :SHARED_DIM]) * gu[:, SHARED_DIM:]).astype(BF16)
    base_ref[...] = x1 + gt2_ref[...] * _dot(act, wds_ref[...])
    lg = _dot_nt(wrt_ref[...], h2b)
    _route_tile(lg, bias_ref, idx_ref, w_ref, rank_ref, cnt_ref, carry)


def _post(ao, mo, x2, gt1, gattn, wout, g2, sc2p, sh2, gt2, wrt, wgu, wds, bias_col, S):
    T = x2.shape[0]
    tm = WIDE_TILE
    per_b = S // tm
    full = lambda a: pl.BlockSpec(a.shape, lambda i: (0,) * a.ndim)
    row = lambda w: pl.BlockSpec((tm, w), lambda i: (i, 0))
    bmod = pl.BlockSpec((None, 1, D_MODEL), lambda i: (i // per_b, 0, 0))
    blk = pl.BlockSpec((TOP_K, tm), lambda i: (0, i))
    return pl.pallas_call(
        _post_body,
        out_shape=(jax.ShapeDtypeStruct((T, D_MODEL), F32), jax.ShapeDtypeStruct((2, T, HALF), U32),
                   jax.ShapeDtypeStruct((TOP_K, T), I32), jax.ShapeDtypeStruct((TOP_K, T), F32),
                   jax.ShapeDtypeStruct((TOP_K, T), I32), jax.ShapeDtypeStruct((N_EXPERTS, LANES), F32)),
        grid=(T // tm,),
        in_specs=[row(MLA_HEADS * V_DIM), row(MLSTM_INNER), row(D_MODEL), bmod, full(gattn), full(wout),
                  full(g2), bmod, bmod, bmod, full(wrt), full(wgu), full(wds), full(bias_col)],
        out_specs=(row(D_MODEL), pl.BlockSpec((2, tm, HALF), lambda i: (0, i, 0)), blk, blk, blk,
                   pl.BlockSpec((N_EXPERTS, LANES), lambda i: (0, 0))),
        scratch_shapes=[pltpu.VMEM((N_EXPERTS, 1), F32)],
        compiler_params=_params(1), name="post",
    )(ao, mo, x2, gt1, gattn, wout, g2, sc2p, sh2, gt2, wrt, wgu, wds, bias_col)


def _first_max(x, rows, n):
    mx = jnp.max(x, axis=0, keepdims=True)
    idx = jnp.min(jnp.where(x == mx, rows, n), axis=0, keepdims=True)
    return mx, idx


def _route_tile(lg, bias_ref, idx_ref, w_ref, rank_ref, cnt_ref, carry):
    i = pl.program_id(0)
    tt = lg.shape[1]

    @pl.when(i == 0)
    def _():
        carry[...] = jnp.zeros_like(carry)

    sc = _sigmoid(lg)
    bi = sc + bias_ref[...]
    rows = lax.broadcasted_iota(I32, (N_EXPERTS, tt), 0)
    rows_g = lax.broadcasted_iota(I32, (GROUP_SIZE, tt), 0)
    rows_n = lax.broadcasted_iota(I32, (N_GROUPS, tt), 0)
    neg = -jnp.inf
    gs = []
    for g in range(N_GROUPS):
        xg = bi[g * GROUP_SIZE:(g + 1) * GROUP_SIZE]
        m1, i1 = _first_max(xg, rows_g, GROUP_SIZE)
        m2 = jnp.max(jnp.where(rows_g == i1, neg, xg), axis=0, keepdims=True)
        gs.append(m1 + m2)
    cur = jnp.concatenate(gs, axis=0)
    keep = jnp.zeros((N_GROUPS, tt), F32)
    for _ in range(TOPK_GROUPS):
        _, gi = _first_max(cur, rows_n, N_GROUPS)
        sel = rows_n == gi
        keep = jnp.where(sel, 1.0, keep)
        cur = jnp.where(sel, neg, cur)
    masked = jnp.concatenate(
        [jnp.where(keep[g:g + 1] > 0.0, bi[g * GROUP_SIZE:(g + 1) * GROUP_SIZE], neg) for g in range(N_GROUPS)], axis=0)
    cur = masked
    onehot = jnp.zeros((N_EXPERTS, tt), F32)
    idxs, ws = [], []
    for _ in range(TOP_K):
        _, ei = _first_max(cur, rows, N_EXPERTS)
        sel = rows == ei
        ws.append(jnp.sum(jnp.where(sel, sc, 0.0), axis=0, keepdims=True))
        idxs.append(ei)
        cur = jnp.where(sel, neg, cur)
        onehot = jnp.where(sel, 1.0, onehot)
    wsum = ws[0]
    for k in range(1, TOP_K):
        wsum = wsum + ws[k]
    idx_ref[...] = jnp.concatenate(idxs, axis=0)
    w_ref[...] = jnp.concatenate([w / wsum * ROUTED_SCALE for w in ws], axis=0)
    r_i = lax.broadcasted_iota(I32, (tt, tt), 0)
    c_i = lax.broadcasted_iota(I32, (tt, tt), 1)
    before = jnp.where(r_i < c_i, 1.0, 0.0).astype(BF16)
    tot = _dot(onehot.astype(BF16), before) + carry[...]
    rank_ref[...] = jnp.concatenate(
        [jnp.sum(jnp.where(rows == ei, tot, 0.0), axis=0, keepdims=True) for ei in idxs], axis=0).astype(I32)
    carry[...] = carry[...] + jnp.sum(onehot, axis=1, keepdims=True)
    cnt_ref[...] = jnp.broadcast_to(carry[...], cnt_ref.shape)


def _dest_body(idx_ref, rank_ref, off_ref, dest_ref):
    tt = idx_ref.shape[1]
    rows = lax.broadcasted_iota(I32, (N_EXPERTS, tt), 0)
    off = off_ref[...]
    outs = []
    for k in range(TOP_K):
        sel = rows == idx_ref[k:k + 1, :]
        outs.append(jnp.sum(jnp.where(sel, off, 0), axis=0, keepdims=True) + rank_ref[k:k + 1, :])
    dest_ref[...] = jnp.concatenate(outs, axis=0)


def _dest(idx_t, rank_t, off_col):
    T = idx_t.shape[1]
    tt = min(4 * SEQ_TILE, T)
    blk = pl.BlockSpec((TOP_K, tt), lambda i: (0, i))
    return pl.pallas_call(
        _dest_body, out_shape=jax.ShapeDtypeStruct((TOP_K, T), I32), grid=(T // tt,),
        in_specs=[blk, blk, pl.BlockSpec((N_EXPERTS, 1), lambda i: (0, 0))], out_specs=blk,
        compiler_params=_params(1), name="dest",
    )(idx_t, rank_t, off_col)


SC_WINDOW = 128
HALF = PACKED // 2


def _sc_mesh():
    return plsc.VectorSubcoreMesh(core_axis_name="c", subcore_axis_name="s")


def _scatter_rows(dest2_t, rows, n_out):
    n = rows.shape[0]

    @pl.kernel(out_type=jax.ShapeDtypeStruct((n_out, HALF), U32), mesh=_sc_mesh(), scratch_types=[])
    def scatter(d_hbm, x_hbm, o_hbm):
        def body(d_vmem, x_vmem):
            for k in range(TOP_K):
                pltpu.sync_copy(x_vmem, o_hbm.at[d_vmem.at[k]])

        pltpu.emit_pipeline(
            body, grid=(n // SC_WINDOW,),
            in_specs=[pl.BlockSpec((TOP_K, SC_WINDOW), lambda i: (0, i)),
                      pl.BlockSpec((SC_WINDOW, HALF), lambda i: (i, 0))],
            out_specs=[], core_axis_name=("c", "s"), dimension_semantics=(pltpu.PARALLEL,),
        )(d_hbm, x_hbm)

    return scatter(dest2_t, rows)


def _gather_rows(index_row, rows):
    n = index_row.shape[1]

    @pl.kernel(out_type=jax.ShapeDtypeStruct((n, HALF), U32), mesh=_sc_mesh(), scratch_types=[])
    def gather(i_hbm, y_hbm, o_hbm):
        def body(i_vmem, o_vmem):
            pltpu.sync_copy(y_hbm.at[i_vmem.at[0]], o_vmem)

        pltpu.emit_pipeline(
            body, grid=(n // SC_WINDOW,),
            in_specs=[pl.BlockSpec((1, SC_WINDOW), lambda i: (0, i))],
            out_specs=[pl.BlockSpec((SC_WINDOW, HALF), lambda i: (i, 0))],
            core_axis_name=("c", "s"), dimension_semantics=(pltpu.PARALLEL,),
        )(i_hbm, o_hbm)

    return gather(index_row, rows)


EXPERT_RING = 16
WEIGHT_RING = 3


def _experts_body(first_ref, nblk_ref, cnt_ref, xs_ref, wg_ref, wu_ref, wd_ref, y_ref,
                  xbuf, ybuf, wgf, wuf, wdf, wgb, wub, wdb, xsem, ysem, wsem):
    e = pl.program_id(0)
    last = pl.num_programs(0) - 1
    wring = wgf.shape[0]

    def w_copies(ex):
        slot = ex % wring
        return [pltpu.make_async_copy(src.at[ex], dst.at[slot], wsem.at[slot])
                for src, dst in ((wg_ref, wgf), (wu_ref, wuf), (wd_ref, wdf))]

    @pl.when(e == 0)
    def _():
        for ex in range(wring - 1):
            for cp in w_copies(ex):
                cp.start()

    b0 = first_ref[e]
    nb = nblk_ref[e]
    total = first_ref[last] + nblk_ref[last]
    ring = xbuf.shape[0]

    def rows(g):
        return pl.ds(pl.multiple_of(g * ROW_BLOCK, ROW_BLOCK), ROW_BLOCK)

    def x_copy(g):
        slot = g % ring
        return pltpu.make_async_copy(xs_ref.at[:, rows(g), :], xbuf.at[slot], xsem.at[slot])

    def y_copy(g):
        slot = g % ring
        return pltpu.make_async_copy(ybuf.at[slot], y_ref.at[:, rows(g), :], ysem.at[slot])

    @pl.when(e == 0)
    def _():
        for g in range(ring - 1):
            @pl.when(g < total)
            def _():
                x_copy(g).start()

    for cp in w_copies(e):
        cp.wait()

    @pl.when(nb > 0)
    def _():
        slot = e % wring
        wgb[...] = wgf[slot].astype(BF16)
        wub[...] = wuf[slot].astype(BF16)
        wdb[...] = wdf[slot].astype(BF16)

    @pl.when(e + wring - 1 <= last)
    def _():
        for cp in w_copies(e + wring - 1):
            cp.start()

    def prefetch(g):
        @pl.when(g + ring - 1 < total)
        def _():
            x_copy(g + ring - 1).start()

    def compute(i):
        slot = (b0 + i) % ring
        words = jnp.concatenate([xbuf[slot, 0], xbuf[slot, 1]], axis=1)
        n_live = cnt_ref[e] - i * ROW_BLOCK
        words = jnp.where(lax.broadcasted_iota(I32, words.shape, 0) < n_live, words, U32(0))
        x = jnp.concatenate(_unpack_rows(words), axis=1).astype(BF16)
        act = (_silu(_dot(x, wgb[...])) * _dot(x, wub[...])).astype(BF16)
        y = _pack_rows(_dot(act, wdb[...]))
        ybuf[slot, 0] = y[:, :HALF]
        ybuf[slot, 1] = y[:, HALF:]

    def blocks(i, n):
        g0 = b0 + i
        prefetch(g0)
        for d in range(n):
            x_copy(g0 + d).wait()

            @pl.when(g0 + d >= ring)
            def _():
                y_copy(g0 + d - ring).wait()

        for d in range(n):
            compute(i + d)
        for d in range(1, n):
            prefetch(g0 + d)
        for d in range(n):
            y_copy(g0 + d).start()

    def pair(ii, carry):
        blocks(2 * ii, 2)
        return carry

    def single(_, carry):
        blocks(nb - 1, 1)
        return carry

    lax.fori_loop(0, nb // 2, pair, 0)
    lax.fori_loop(0, nb % 2, single, 0)

    @pl.when(e == last)
    def _():
        for back in range(ring, 0, -1):
            @pl.when(total >= back)
            def _():
                y_copy(total - back).wait()


def _experts(first_block, n_blocks, counts, xs, wg, wu, wd):
    n_rows = xs.shape[1]
    anyspec = pl.BlockSpec(memory_space=pl.ANY)
    ring = pltpu.VMEM((EXPERT_RING, 2, ROW_BLOCK, HALF), U32)
    grid_spec = pltpu.PrefetchScalarGridSpec(
        num_scalar_prefetch=3, grid=(N_EXPERTS,),
        in_specs=[anyspec] * 4,
        out_specs=anyspec,
        scratch_shapes=[ring, ring,
                        pltpu.VMEM((WEIGHT_RING, D_MODEL, EXPERT_DIM), F32),
                        pltpu.VMEM((WEIGHT_RING, D_MODEL, EXPERT_DIM), F32),
                        pltpu.VMEM((WEIGHT_RING, EXPERT_DIM, D_MODEL), F32),
                        pltpu.VMEM((D_MODEL, EXPERT_DIM), BF16), pltpu.VMEM((D_MODEL, EXPERT_DIM), BF16),
                        pltpu.VMEM((EXPERT_DIM, D_MODEL), BF16),
                        pltpu.SemaphoreType.DMA((EXPERT_RING,)), pltpu.SemaphoreType.DMA((EXPERT_RING,)),
                        pltpu.SemaphoreType.DMA((WEIGHT_RING,))],
    )
    return pl.pallas_call(
        _experts_body, out_shape=jax.ShapeDtypeStruct((2, n_rows, HALF), U32), grid_spec=grid_spec,
        compiler_params=_params(1), name="experts",
    )(first_block, n_blocks, counts, xs, wg, wu, wd)


def _combine_body(*refs):
    ya_refs = refs[:TOP_K]
    yb_refs = refs[TOP_K:2 * TOP_K]
    base_ref, w_ref, gt2_ref, out_ref = refs[2 * TOP_K:]
    w = w_ref[...]
    r_lo = r_hi = None
    for k in range(TOP_K):
        lo, hi = _unpack_rows(jnp.concatenate([ya_refs[k][...], yb_refs[k][...]], axis=1))
        wk = w[:, k:k + 1]
        r_lo = wk * lo if r_lo is None else r_lo + wk * lo
        r_hi = wk * hi if r_hi is None else r_hi + wk * hi
    out_ref[...] = base_ref[...] + gt2_ref[...] * jnp.concatenate([r_lo, r_hi], axis=1)


def _combine(y_slots, base, w_tk, gt2, S):
    T = base.shape[0]
    tt = WIDE_TILE
    nt = T // tt
    per_b = S // tt
    yspec = lambda half, k: pl.BlockSpec((tt, HALF), lambda i: ((half * TOP_K + k) * nt + i, 0))
    return pl.pallas_call(
        _combine_body, out_shape=jax.ShapeDtypeStruct((T, D_MODEL), F32), grid=(nt,),
        in_specs=[yspec(half, k) for half in range(2) for k in range(TOP_K)] + [
            pl.BlockSpec((tt, D_MODEL), lambda i: (i, 0)),
            pl.BlockSpec((tt, TOP_K), lambda i: (i, 0)),
            pl.BlockSpec((None, 1, D_MODEL), lambda i: (i // per_b, 0, 0)),
        ],
        out_specs=pl.BlockSpec((tt, D_MODEL), lambda i: (i, 0)),
        compiler_params=_params(1), name="combine",
    )(*([y_slots] * (2 * TOP_K)), base, w_tk, gt2)


def _pad_heads(w, width):
    kdim = w.shape[0]
    w3 = w.reshape(kdim, MLA_HEADS, width)
    return jnp.pad(w3, ((0, 0), (0, 0), (0, LANES - width))).reshape(kdim, MLA_HEADS * LANES)


def _rot_partner(w_rope):
    half = QK_ROPE // 2
    return jnp.concatenate([-w_rope[..., half:], w_rope[..., :half]], axis=-1)


def _lane_vec(nope, rope):
    return jnp.concatenate([nope, rope, jnp.zeros((LANES - QK_DIM,), F32)]).reshape(1, LANES)


def _layer(x2, ada, positions, B, S, g_norm1, w_in, g_cq, w_uq, g_ckv, w_ukv, g_qn, g_kn, g_attn_out,
           conv_w, conv_b, w_mq, w_mk, b_igate, b_fgate, g_mlstm_out, w_out, g_norm2, w_router, router_bias,
           w_gate_exp, w_up_exp, w_down_exp, w_gate_sh, w_up_sh, w_down_sh):
    T = B * S
    sh1, sc1, gt1, sh2, sc2, gt2 = [a.reshape(B, 1, D_MODEL) for a in jnp.split(ada, 6, axis=-1)]
    row = lambda v: v.reshape(1, -1).astype(F32)

    inv_freq = ROPE_THETA ** (-jnp.arange(0, QK_ROPE, 2, dtype=F32) / QK_ROPE)
    ang = positions.astype(F32).reshape(T, 1) * inv_freq
    half = QK_ROPE // 2
    freq_lane = jnp.arange(half)[:, None]
    lane = jnp.arange(LANES)[None, :]
    place = ((lane == QK_NOPE + freq_lane) | (lane == QK_NOPE + half + freq_lane)).astype(F32)
    spread = lambda tbl: jnp.dot(tbl, place, precision=lax.Precision.HIGHEST)
    cosp = (jnp.arange(LANES) < QK_NOPE).astype(F32)[None, :] + spread(jnp.cos(ang))
    sinp = spread(jnp.sin(ang))

    o0 = 0
    w_cq = w_in[:, o0:o0 + Q_LORA]; o0 += Q_LORA
    w_ckv = w_in[:, o0:o0 + KV_LORA]; o0 += KV_LORA
    w_kr = w_in[:, o0:o0 + QK_ROPE]; o0 += QK_ROPE
    w_z = w_in[:, o0:o0 + MLSTM_INNER]; o0 += MLSTM_INNER
    w_v = w_in[:, o0:o0 + MLSTM_INNER]; o0 += MLSTM_INNER
    w_o = w_in[:, o0:o0 + MLSTM_INNER]; o0 += MLSTM_INNER
    w_i = w_in[:, o0:o0 + MLSTM_HEADS]; o0 += MLSTM_HEADS
    w_f = w_in[:, o0:o0 + MLSTM_HEADS]
    zl = jnp.zeros((D_MODEL, QK_NOPE), F32)
    zr = jnp.zeros((D_MODEL, LANES - QK_DIM), F32)
    zg = jnp.zeros((D_MODEL, LANES - 2 * MLSTM_HEADS), F32)
    wa = jnp.concatenate([w_cq, w_ckv, w_z, w_o, zl, w_kr, zr, zl, _rot_partner(w_kr), zr, w_i, w_f, zg],
                         axis=1).astype(BF16)
    wt = jnp.concatenate([w_v.T, w_i.T, w_f.T, jnp.zeros((16 - 2 * MLSTM_HEADS, D_MODEL), F32)], axis=0).astype(BF16)

    uq3 = w_uq.reshape(Q_LORA, MLA_HEADS, QK_DIM)
    uq_rot = jnp.concatenate([jnp.zeros((Q_LORA, MLA_HEADS, QK_NOPE), F32), _rot_partner(uq3[..., QK_NOPE:])], axis=-1)
    wuq = _pad_heads(w_uq, QK_DIM).astype(BF16)
    wuqs = _pad_heads(uq_rot.reshape(Q_LORA, MLA_HEADS * QK_DIM), QK_DIM).astype(BF16)
    ukv3 = w_ukv.reshape(KV_LORA, MLA_HEADS, QK_NOPE + V_DIM)
    wuk = _pad_heads(ukv3[..., :QK_NOPE].reshape(KV_LORA, MLA_HEADS * QK_NOPE), QK_NOPE).astype(BF16)
    wuv = _pad_heads(ukv3[..., QK_NOPE:].reshape(KV_LORA, MLA_HEADS * V_DIM), V_DIM).astype(BF16)
    vone = jnp.tile((jnp.arange(LANES) == V_DIM).astype(F32), MLA_HEADS).reshape(1, MLA_HEADS * LANES)
    swap = lambda g_rope: jnp.concatenate([g_rope[half:], g_rope[:half]])
    no_nope = jnp.zeros((QK_NOPE,), F32)
    gq = _lane_vec(g_qn[:QK_NOPE], g_qn[QK_NOPE:])
    gqs = _lane_vec(no_nope, swap(g_qn[QK_NOPE:]))
    gk = _lane_vec(g_kn[:QK_NOPE], g_kn[QK_NOPE:])
    gks = _lane_vec(no_nope, swap(g_kn[QK_NOPE:]))

    z, og, vt3, g3, gn, q, k, mv = _inproj(
        x2, 1.0 + sc1, sh1, row(g_norm1), wa, wt, row(g_cq), wuq, wuqs, row(g_ckv), wuk, wuv, vone,
        gq, gqs, gk, gks, cosp, sinp, S)
    attn_o = _attn(q, k, mv, B, S)
    mlstm_o = _mlstm(z, vt3, g3, gn, og, conv_w.reshape(CONV_WIDTH, MLSTM_INNER), row(conv_b),
                     w_mq.astype(BF16), w_mk.astype(BF16), b_igate, b_fgate, row(g_mlstm_out), B, S)

    wgu = jnp.concatenate([w_gate_sh, w_up_sh], axis=1).astype(BF16)
    base, h2, idx_t, w_t, rank_t, cnt = _post(
        attn_o, mlstm_o, x2, gt1, row(g_attn_out), w_out.astype(BF16), row(g_norm2), 1.0 + sc2, sh2, gt2,
        w_router.T.astype(BF16), wgu, w_down_sh.astype(BF16), router_bias.reshape(N_EXPERTS, 1), S)
    counts = cnt[:, 0].astype(I32)
    padded = (counts + ROW_BLOCK - 1) // ROW_BLOCK * ROW_BLOCK
    padded_end = jnp.cumsum(padded)
    off = padded_end - padded
    n_rows = T * TOP_K + N_EXPERTS * ROW_BLOCK
    dest_t = _dest(idx_t, rank_t, off.reshape(N_EXPERTS, 1))

    dest2_t = jnp.concatenate([dest_t, dest_t + n_rows], axis=1)
    xs = _scatter_rows(dest2_t, h2.reshape(2 * T, HALF), 2 * n_rows).reshape(2, n_rows, HALF)
    y = _experts(off // ROW_BLOCK, padded // ROW_BLOCK, counts, xs, w_gate_exp, w_up_exp, w_down_exp)
    dest_row = dest_t.reshape(1, TOP_K * T)
    y_slots = _gather_rows(jnp.concatenate([dest_row, dest_row + n_rows], axis=1), y.reshape(2 * n_rows, HALF))
    return _combine(y_slots, base, w_t.T, gt2, S)


def kernel(x, c, positions, w_ada, b_ada, g_norm1, w_in, g_cq, w_uq, g_ckv, w_ukv, g_qn, g_kn, g_attn_out, conv_w, conv_b, w_mq, w_mk, b_igate, b_fgate, g_mlstm_out, w_out, g_norm2, w_router, router_bias, w_gate_exp, w_up_exp, w_down_exp, w_gate_sh, w_up_sh, w_down_sh):
    B, S, d_model = x.shape
    assert d_model == D_MODEL and S % WIDE_TILE == 0 and S % SEQ_TILE == 0, (B, S, d_model)
    assert w_gate_exp.shape[1:] == (N_EXPERTS, D_MODEL, EXPERT_DIM), w_gate_exp.shape
    depth = w_ada.shape[0]
    x2 = x.reshape(B * S, D_MODEL)
    for l in range(depth):
        ada = _ada(c, w_ada[l], b_ada[l])
        x2 = _layer(x2, ada, positions, B, S, g_norm1[l], w_in[l], g_cq[l], w_uq[l], g_ckv[l], w_ukv[l], g_qn[l],
                    g_kn[l], g_attn_out[l], conv_w[l], conv_b[l], w_mq[l], w_mk[l], b_igate[l], b_fgate[l],
                    g_mlstm_out[l], w_out[l], g_norm2[l], w_router[l], router_bias[l], w_gate_exp[l], w_up_exp[l],
                    w_down_exp[l], w_gate_sh[l], w_up_sh[l], w_down_sh[l])
    return x2.reshape(B, S, D_MODEL)
```

```python
import functools
import math

import jax
import jax.numpy as jnp
from jax import lax
from jax.experimental import pallas as pl
from jax.experimental.pallas import tpu as pltpu
from jax.experimental.pallas import tpu_sc as plsc

F32 = jnp.float32
BF16 = jnp.bfloat16
I32 = jnp.int32

D_MODEL = 1024
MLA_HEADS = 8
QK_NOPE = 64
QK_ROPE = 32
QK_DIM = QK_NOPE + QK_ROPE
V_DIM = 64
Q_LORA = 384
KV_LORA = 256
ROPE_THETA = 10000.0
MLSTM_HEADS = 4
MLSTM_DIM = 128
MLSTM_INNER = MLSTM_HEADS * MLSTM_DIM
CONV_WIDTH = 4
N_EXPERTS = 256
TOP_K = 8
N_GROUPS = 8
TOPK_GROUPS = 4
GROUP_SIZE = N_EXPERTS // N_GROUPS
EXPERT_DIM = 256
SHARED_DIM = 256
ROUTED_SCALE = 2.5
RMS_EPS = 1e-6

LANES = 128
SEQ_TILE = 256
WIDE_TILE = 512
ROW_BLOCK = 256
VMEM_LIMIT = 56 * 1024 * 1024

_OFF_CQ = 0
_OFF_CKV = _OFF_CQ + Q_LORA
_OFF_Z = _OFF_CKV + KV_LORA
_OFF_O = _OFF_Z + MLSTM_INNER
_OFF_KR = _OFF_O + MLSTM_INNER
_OFF_KRS = _OFF_KR + LANES
_OFF_G = _OFF_KRS + LANES
_WA_COLS = _OFF_G + LANES
_WT_ROWS = MLSTM_INNER + 16


def _params(n_axes, vmem=VMEM_LIMIT):
    return pltpu.CompilerParams(dimension_semantics=("arbitrary",) * n_axes, vmem_limit_bytes=vmem)


def _dot(a, b):
    return jnp.dot(a, b, preferred_element_type=F32)


def _dot_nt(a, b):
    return lax.dot_general(a, b, (((1,), (1,)), ((), ())), preferred_element_type=F32)


def _sigmoid(x):
    return 1.0 / (1.0 + jnp.exp(-x))


def _silu(x):
    return x * _sigmoid(x)


U32 = jnp.uint32
PACKED = D_MODEL // 2


def _pack_rows(x):
    w = x.shape[1] // 2
    lo = lax.bitcast_convert_type(x[:, :w].astype(BF16).astype(F32), U32)
    hi = lax.bitcast_convert_type(x[:, w:].astype(BF16).astype(F32), U32)
    return lax.shift_right_logical(lo, U32(16)) | (hi & U32(0xFFFF0000))


def _unpack_rows(p):
    lo = lax.bitcast_convert_type(lax.shift_left(p, U32(16)), F32)
    hi = lax.bitcast_convert_type(p & U32(0xFFFF0000), F32)
    return lo, hi


def _ada_body(c_ref, w_ref, b_ref, o_ref):
    c = c_ref[...]
    cond = _silu(c).astype(BF16)
    o_ref[...] = _dot(cond, w_ref[...].astype(BF16)) + b_ref[...]


def _ada(c, w_ada, b_ada):
    B = c.shape[0]
    n = w_ada.shape[1]
    tn = 1024
    return pl.pallas_call(
        _ada_body,
        out_shape=jax.ShapeDtypeStruct((B, n), F32),
        grid=(n // tn,),
        in_specs=[
            pl.BlockSpec((B, D_MODEL), lambda j: (0, 0)),
            pl.BlockSpec((D_MODEL, tn), lambda j: (0, j)),
            pl.BlockSpec((1, tn), lambda j: (0, j)),
        ],
        out_specs=pl.BlockSpec((B, tn), lambda j: (0, j)),
        compiler_params=_params(1),
        name="ada",
    )(c, w_ada, b_ada.reshape(1, n))


def _inproj_body(x_ref, sc_ref, sh_ref, g1_ref, wa_ref, wt_ref, gcq_ref, wuq_ref, wuqs_ref, gckv_ref,
                 wuk_ref, wuv_ref, vone_ref, gq_ref, gqs_ref, gk_ref, gks_ref, cos_ref, sin_ref,
                 z_ref, o_ref, vt_ref, gt_ref, gn_ref, q_ref, k_ref, mv_ref):
    x = x_ref[...]
    hn = x * lax.rsqrt(jnp.mean(x * x, axis=-1, keepdims=True) + RMS_EPS) * g1_ref[...]
    hm = (hn * sc_ref[...] + sh_ref[...]).astype(BF16)
    p = _dot(hm, wa_ref[...])
    z_ref[...] = p[:, _OFF_Z:_OFF_Z + MLSTM_INNER].astype(BF16)
    o_ref[...] = p[:, _OFF_O:_OFF_O + MLSTM_INNER].astype(BF16)
    gn_ref[...] = p[:, _OFF_G:_OFF_G + LANES]
    for part in range(vt_ref.shape[0]):
        rt = _dot_nt(wt_ref[...], hm[part * SEQ_TILE:(part + 1) * SEQ_TILE])
        vt_ref[part] = rt[:MLSTM_INNER].astype(BF16)
        gt_ref[part] = rt[MLSTM_INNER:]

    cq = p[:, _OFF_CQ:_OFF_CQ + Q_LORA]
    cqn = (cq * lax.rsqrt(jnp.mean(cq * cq, axis=-1, keepdims=True) + RMS_EPS) * gcq_ref[...]).astype(BF16)
    ckv = p[:, _OFF_CKV:_OFF_CKV + KV_LORA]
    ckvn = (ckv * lax.rsqrt(jnp.mean(ckv * ckv, axis=-1, keepdims=True) + RMS_EPS) * gckv_ref[...]).astype(BF16)
    qa = _dot(cqn, wuq_ref[...])
    qb = _dot(cqn, wuqs_ref[...])
    ka = _dot(ckvn, wuk_ref[...])
    mv_ref[...] = (_dot(ckvn, wuv_ref[...]) + vone_ref[...]).astype(BF16)
    kr = p[:, _OFF_KR:_OFF_KR + LANES]
    krs = p[:, _OFF_KRS:_OFF_KRS + LANES]

    cosp = cos_ref[...]
    sinp = sin_ref[...]
    gqc = gq_ref[...] * cosp
    gqs = gqs_ref[...] * sinp
    gkc = gk_ref[...] * cosp
    k_rot = krs * (gks_ref[...] * sinp)
    kr_ss = jnp.sum(kr * kr, axis=-1, keepdims=True)
    q_scale = (QK_DIM ** -0.5) * math.log2(math.e)
    for h in range(MLA_HEADS):
        sl = slice(h * LANES, (h + 1) * LANES)
        qah = qa[:, sl]
        rq = lax.rsqrt(jnp.sum(qah * qah, axis=-1, keepdims=True) * (1.0 / QK_DIM) + RMS_EPS) * q_scale
        q_ref[:, sl] = ((qah * gqc + qb[:, sl] * gqs) * rq).astype(BF16)
        kah = ka[:, sl]
        rk = lax.rsqrt((jnp.sum(kah * kah, axis=-1, keepdims=True) + kr_ss) * (1.0 / QK_DIM) + RMS_EPS)
        k_ref[:, sl] = (((kah + kr) * gkc + k_rot) * rk).astype(BF16)


def _inproj(x2, sc1p, sh1, g1, wa, wt, gcq, wuq, wuqs, gckv, wuk, wuv, vone, gq, gqs, gk, gks, cosp, sinp, S):
    T = x2.shape[0]
    tm = WIDE_TILE
    nt = T // tm
    per_b = S // tm
    sub = tm // SEQ_TILE
    hp = MLA_HEADS * LANES
    full = lambda a: pl.BlockSpec(a.shape, lambda i: (0,) * a.ndim)
    row = lambda w: pl.BlockSpec((tm, w), lambda i: (i, 0))
    bmod = pl.BlockSpec((None, 1, D_MODEL), lambda i: (i // per_b, 0, 0))
    out_shapes = (
        jax.ShapeDtypeStruct((T, MLSTM_INNER), BF16),
        jax.ShapeDtypeStruct((T, MLSTM_INNER), BF16),
        jax.ShapeDtypeStruct((T // SEQ_TILE, MLSTM_INNER, SEQ_TILE), BF16),
        jax.ShapeDtypeStruct((T // SEQ_TILE, 16, SEQ_TILE), F32),
        jax.ShapeDtypeStruct((T, LANES), F32),
        jax.ShapeDtypeStruct((T, hp), BF16),
        jax.ShapeDtypeStruct((T, hp), BF16),
        jax.ShapeDtypeStruct((T, hp), BF16),
    )
    out_specs = (
        row(MLSTM_INNER), row(MLSTM_INNER),
        pl.BlockSpec((sub, MLSTM_INNER, SEQ_TILE), lambda i: (i, 0, 0)),
        pl.BlockSpec((sub, 16, SEQ_TILE), lambda i: (i, 0, 0)),
        row(LANES), row(hp), row(hp), row(hp),
    )
    in_specs = [row(D_MODEL), bmod, bmod, full(g1), full(wa), full(wt), full(gcq), full(wuq), full(wuqs),
                full(gckv), full(wuk), full(wuv), full(vone), full(gq), full(gqs), full(gk), full(gks),
                row(LANES), row(LANES)]
    return pl.pallas_call(
        _inproj_body, out_shape=out_shapes, grid=(nt,), in_specs=in_specs, out_specs=out_specs,
        compiler_params=_params(1), name="inproj",
    )(x2, sc1p, sh1, g1, wa, wt, gcq, wuq, wuqs, gckv, wuk, wuv, vone, gq, gqs, gk, gks, cosp, sinp)


ATTN_HEADS_PER_STEP = 8


def _pairs_loop(n, fn, init):
    c = lax.fori_loop(0, n // 2, lambda jj, c: fn(2 * jj + 1, fn(2 * jj, c)), init)
    return lax.fori_loop(0, n % 2, lambda _, c: fn(n - 1, c), c)


def _attn_body(q_ref, k_ref, v_ref, o_ref, s_scr):
    i = pl.program_id(2)
    t = SEQ_TILE
    nh = ATTN_HEADS_PER_STEP
    qry_pos = lax.broadcasted_iota(I32, (t, t), 0)
    key_pos = lax.broadcasted_iota(I32, (t, t), 1)
    causal = key_pos <= qry_pos
    qs = [q_ref[:, hh * LANES:(hh + 1) * LANES] for hh in range(nh)]

    def scores(j, hh, diag):
        kj = k_ref[pl.ds(pl.multiple_of(j * t, t), t), hh * LANES:(hh + 1) * LANES]
        s = _dot_nt(qs[hh], kj)
        return jnp.where(causal, s, -jnp.inf) if diag else s

    def halves(x):
        return x[:, :LANES], x[:, LANES:]

    def max_pass(j, mx, diag):
        out = []
        for hh in range(nh):
            s = scores(j, hh, diag)
            s_scr[j, hh] = s
            lo, hi = halves(s)
            out.append(jnp.maximum(mx[hh], jnp.maximum(lo, hi)))
        return tuple(out)

    mx = tuple(jnp.full((t, LANES), -jnp.inf, F32) for _ in range(nh))
    mx = _pairs_loop(i, lambda j, c: max_pass(j, c, False), mx)
    mx = max_pass(i, mx, True)
    mb = [jnp.broadcast_to(jnp.max(mx[hh], axis=1, keepdims=True), (t, LANES)) for hh in range(nh)]

    def acc_pass(j, carry, diag):
        row0 = pl.multiple_of(j * t, t)
        out = []
        for hh in range(nh):
            lo, hi = halves(s_scr[j, hh])
            p = jnp.concatenate([jnp.exp2(lo - mb[hh]), jnp.exp2(hi - mb[hh])], axis=1).astype(BF16)
            out.append(carry[hh] + _dot(p, v_ref[pl.ds(row0, t), hh * LANES:(hh + 1) * LANES]))
        return tuple(out)

    init = tuple(jnp.zeros((t, LANES), F32) for _ in range(nh))
    carry = _pairs_loop(i, lambda j, c: acc_pass(j, c, False), init)
    final = acc_pass(i, carry, True)
    outs = [acc / acc[:, V_DIM:V_DIM + 1] for acc in final]
    first_head = lax.broadcasted_iota(I32, (t, LANES), 1) < V_DIM
    o_ref[...] = jnp.concatenate(
        [jnp.where(first_head, outs[2 * pr], pltpu.roll(outs[2 * pr + 1], V_DIM, axis=1)) for pr in range(nh // 2)],
        axis=1).astype(BF16)


def _attn(q, k, mvt, B, S):
    T = q.shape[0]
    t = SEQ_TILE
    nh = ATTN_HEADS_PER_STEP
    per_b = S // t
    return pl.pallas_call(
        _attn_body,
        out_shape=jax.ShapeDtypeStruct((T, MLA_HEADS * V_DIM), BF16),
        grid=(B, MLA_HEADS // nh, per_b),
        in_specs=[
            pl.BlockSpec((t, nh * LANES), lambda b, hp, i: (b * per_b + i, hp)),
            pl.BlockSpec((S, nh * LANES), lambda b, hp, i: (b, hp)),
            pl.BlockSpec((S, nh * LANES), lambda b, hp, i: (b, hp)),
        ],
        out_specs=pl.BlockSpec((t, nh * V_DIM), lambda b, hp, i: (b * per_b + i, hp)),
        scratch_shapes=[pltpu.VMEM((per_b, nh, t, t), F32)],
        compiler_params=_params(3), name="attn",
    )(q, k, mvt)


def _log_sigmoid(x):
    return jnp.minimum(x, 0.0) - jnp.log(1.0 + jnp.exp(-jnp.abs(x)))


def _mlstm_body(bi_ref, bf_ref, z_ref, vt_ref, g_ref, gn_ref, og_ref, cw_ref, cb_ref, wq_ref, wk_ref, gout_ref,
                out_ref, zpad, q_s, k_s, st_s, m_s, *, S):
    h = pl.program_id(1)
    L = SEQ_TILE
    nchunk = S // L
    hd = MLSTM_DIM
    zpad[0:8, :] = jnp.zeros((8, hd), F32)
    zpad[8:, :] = z_ref[...].astype(F32)
    zc = cb_ref[...] + jnp.zeros((S, hd), F32)
    for j in range(CONV_WIDTH):
        zc = zc + cw_ref[j:j + 1, :] * zpad[pl.ds(8 - (CONV_WIDTH - 1) + j, S), :]
    zs = _silu(zc).astype(BF16)
    q_s[...] = _dot(zs, wq_ref[...]).astype(BF16)
    k_s[...] = (_dot(zs, wk_ref[...]) * (hd ** -0.5)).astype(BF16)

    st_s[...] = jnp.zeros_like(st_s)
    m_s[...] = jnp.zeros_like(m_s)
    b_i = bi_ref[h]
    b_f = bf_ref[h]
    r_i = lax.broadcasted_iota(I32, (L, L), 0)
    c_i = lax.broadcasted_iota(I32, (L, L), 1)
    tril = r_i >= c_i
    triu = r_i <= c_i
    f_lane = lax.broadcasted_iota(I32, (L, LANES), 1) == MLSTM_HEADS + h
    ones_row = jnp.where(lax.broadcasted_iota(I32, (hd, L), 0) == 0, 1.0, 0.0).astype(BF16)

    def chunk(c, _):
        start = pl.multiple_of(c * L, L)
        qc = q_s[pl.ds(start, L), :]
        kc = k_s[pl.ds(start, L), :]
        vt_aug = jnp.concatenate([vt_ref[c], ones_row], axis=0)
        li = g_ref[c, pl.ds(h, 1), :] + b_i
        fp = g_ref[c, pl.ds(MLSTM_HEADS + h, 1), :] + b_f
        lf = _log_sigmoid(fp)
        fp_col = jnp.sum(jnp.where(f_lane, gn_ref[pl.ds(start, L), :], 0.0), axis=1, keepdims=True) + b_f
        lf_col = _log_sigmoid(fp_col)
        b_col = jnp.sum(jnp.where(tril, lf, 0.0), axis=1, keepdims=True)
        b_row = jnp.sum(jnp.where(triu, lf_col, 0.0), axis=0, keepdims=True)
        d0 = b_col - b_row
        dmat = jnp.where(tril, d0 + li, -jnp.inf)
        m_prev = m_s[...]
        inter = b_col + m_prev
        mt = jnp.maximum(inter, jnp.max(dmat, axis=1, keepdims=True))
        w = jnp.exp(dmat - mt) * _dot_nt(qc, kc)
        decay = jnp.exp(inter - mt)
        st = st_s[...]
        num_aug = decay * _dot_nt(qc, st.astype(BF16)) + _dot_nt(w.astype(BF16), vt_aug)
        num = num_aug[:, :hd]
        den = num_aug[:, hd:hd + 1]
        hv = num / jnp.maximum(jnp.abs(den), jnp.exp(-mt))
        g = d0[L - 1:L, :] + li
        b_last = b_col[L - 1:L, :]
        m_new = jnp.maximum(b_last + m_prev, jnp.max(g, axis=1, keepdims=True))
        a = jnp.exp(g - m_new)
        cd = jnp.exp(b_last + m_prev - m_new)
        st_s[...] = cd * st + _dot((vt_aug.astype(F32) * a).astype(BF16), kc)
        m_s[...] = m_new
        hn = hv * lax.rsqrt(jnp.mean(hv * hv, axis=-1, keepdims=True) + RMS_EPS) * gout_ref[...]
        og = og_ref[pl.ds(start, L), :].astype(F32)
        out_ref[pl.ds(start, L), :] = (_sigmoid(og) * hn).astype(BF16)
        return 0

    lax.fori_loop(0, nchunk, chunk, 0)


def _mlstm(z, vt3, g3, gn, og, conv_w, conv_b, wq, wk, b_i, b_f, g_out, B, S):
    T = z.shape[0]
    L = SEQ_TILE
    per_b = S // L
    hd = MLSTM_DIM
    smem = pl.BlockSpec(memory_space=pltpu.SMEM)
    seq = pl.BlockSpec((S, hd), lambda b, h: (b, h))
    return pl.pallas_call(
        functools.partial(_mlstm_body, S=S),
        out_shape=jax.ShapeDtypeStruct((T, MLSTM_INNER), BF16),
        grid=(B, MLSTM_HEADS),
        in_specs=[
            smem, smem, seq,
            pl.BlockSpec((per_b, hd, L), lambda b, h: (b, h, 0)),
            pl.BlockSpec((per_b, 16, L), lambda b, h: (b, 0, 0)),
            pl.BlockSpec((S, LANES), lambda b, h: (b, 0)),
            seq,
            pl.BlockSpec((CONV_WIDTH, hd), lambda b, h: (0, h)),
            pl.BlockSpec((1, hd), lambda b, h: (0, h)),
            pl.BlockSpec((None, hd, hd), lambda b, h: (h, 0, 0)),
            pl.BlockSpec((None, hd, hd), lambda b, h: (h, 0, 0)),
            pl.BlockSpec((1, hd), lambda b, h: (0, h)),
        ],
        out_specs=seq,
        scratch_shapes=[
            pltpu.VMEM((S + 8, hd), F32),
            pltpu.VMEM((S, hd), BF16),
            pltpu.VMEM((S, hd), BF16),
            pltpu.VMEM((2 * hd, hd), F32),
            pltpu.VMEM((1, 1), F32),
        ],
        compiler_params=_params(2), name="mlstm",
    )(b_i, b_f, z, vt3, g3, gn, og, conv_w, conv_b, wq, wk, g_out)


def _post_body(ao_ref, mo_ref, x_ref, gt1_ref, gattn_ref, wout_ref, g2_ref, sc2_ref, sh2_ref, gt2_ref,
               wrt_ref, wgu_ref, wds_ref, bias_ref, base_ref, h2_ref, idx_ref, w_ref, rank_ref, cnt_ref, carry):
    ao = ao_ref[...].astype(F32)
    aon = ao * lax.rsqrt(jnp.mean(ao * ao, axis=-1, keepdims=True) + RMS_EPS) * gattn_ref[...]
    mix_in = jnp.concatenate([aon.astype(BF16), mo_ref[...]], axis=1)
    x1 = x_ref[...] + gt1_ref[...] * _dot(mix_in, wout_ref[...])
    hn = x1 * lax.rsqrt(jnp.mean(x1 * x1, axis=-1, keepdims=True) + RMS_EPS) * g2_ref[...]
    h2 = hn * sc2_ref[...] + sh2_ref[...]
    words = _pack_rows(h2)
    h2_ref[0] = words[:, :HALF]
    h2_ref[1] = words[:, HALF:]
    h2b = h2.astype(BF16)
    gu = _dot(h2b, wgu_ref[...])
    act = (_silu(gu[:, :SHARED_DIM]) * gu[:, SHARED_DIM:]).astype(BF16)
    base_ref[...] = x1 + gt2_ref[...] * _dot(act, wds_ref[...])
    lg = _dot_nt(wrt_ref[...], h2b)
    _route_tile(lg, bias_ref, idx_ref, w_ref, rank_ref, cnt_ref, carry)


def _post(ao, mo, x2, gt1, gattn, wout, g2, sc2p, sh2, gt2, wrt, wgu, wds, bias_col, S):
    T = x2.shape[0]
    tm = WIDE_TILE
    per_b = S // tm
    full = lambda a: pl.BlockSpec(a.shape, lambda i: (0,) * a.ndim)
    row = lambda w: pl.BlockSpec((tm, w), lambda i: (i, 0))
    bmod = pl.BlockSpec((None, 1, D_MODEL), lambda i: (i // per_b, 0, 0))
    blk = pl.BlockSpec((TOP_K, tm), lambda i: (0, i))
    return pl.pallas_call(
        _post_body,
        out_shape=(jax.ShapeDtypeStruct((T, D_MODEL), F32), jax.ShapeDtypeStruct((2, T, HALF), U32),
                   jax.ShapeDtypeStruct((TOP_K, T), I32), jax.ShapeDtypeStruct((TOP_K, T), F32),
                   jax.ShapeDtypeStruct((TOP_K, T), I32), jax.ShapeDtypeStruct((N_EXPERTS, LANES), F32)),
        grid=(T // tm,),
        in_specs=[row(MLA_HEADS * V_DIM), row(MLSTM_INNER), row(D_MODEL), bmod, full(gattn), full(wout),
                  full(g2), bmod, bmod, bmod, full(wrt), full(wgu), full(wds), full(bias_col)],
        out_specs=(row(D_MODEL), pl.BlockSpec((2, tm, HALF), lambda i: (0, i, 0)), blk, blk, blk,
                   pl.BlockSpec((N_EXPERTS, LANES), lambda i: (0, 0))),
        scratch_shapes=[pltpu.VMEM((N_EXPERTS, 1), F32)],
        compiler_params=_params(1), name="post",
    )(ao, mo, x2, gt1, gattn, wout, g2, sc2p, sh2, gt2, wrt, wgu, wds, bias_col)


def _first_max(x, rows, n):
    mx = jnp.max(x, axis=0, keepdims=True)
    idx = jnp.min(jnp.where(x == mx, rows, n), axis=0, keepdims=True)
    return mx, idx


def _route_tile(lg, bias_ref, idx_ref, w_ref, rank_ref, cnt_ref, carry):
    i = pl.program_id(0)
    tt = lg.shape[1]

    @pl.when(i == 0)
    def _():
        carry[...] = jnp.zeros_like(carry)

    sc = _sigmoid(lg)
    bi = sc + bias_ref[...]
    rows = lax.broadcasted_iota(I32, (N_EXPERTS, tt), 0)
    rows_g = lax.broadcasted_iota(I32, (GROUP_SIZE, tt), 0)
    rows_n = lax.broadcasted_iota(I32, (N_GROUPS, tt), 0)
    neg = -jnp.inf
    gs = []
    for g in range(N_GROUPS):
        xg = bi[g * GROUP_SIZE:(g + 1) * GROUP_SIZE]
        m1, i1 = _first_max(xg, rows_g, GROUP_SIZE)
        m2 = jnp.max(jnp.where(rows_g == i1, neg, xg), axis=0, keepdims=True)
        gs.append(m1 + m2)
    cur = jnp.concatenate(gs, axis=0)
    keep = jnp.zeros((N_GROUPS, tt), F32)
    for _ in range(TOPK_GROUPS):
        _, gi = _first_max(cur, rows_n, N_GROUPS)
        sel = rows_n == gi
        keep = jnp.where(sel, 1.0, keep)
        cur = jnp.where(sel, neg, cur)
    masked = jnp.concatenate(
        [jnp.where(keep[g:g + 1] > 0.0, bi[g * GROUP_SIZE:(g + 1) * GROUP_SIZE], neg) for g in range(N_GROUPS)], axis=0)
    cur = masked
    onehot = jnp.zeros((N_EXPERTS, tt), F32)
    idxs, ws = [], []
    for _ in range(TOP_K):
        _, ei = _first_max(cur, rows, N_EXPERTS)
        sel = rows == ei
        ws.append(jnp.sum(jnp.where(sel, sc, 0.0), axis=0, keepdims=True))
        idxs.append(ei)
        cur = jnp.where(sel, neg, cur)
        onehot = jnp.where(sel, 1.0, onehot)
    wsum = ws[0]
    for k in range(1, TOP_K):
        wsum = wsum + ws[k]
    idx_ref[...] = jnp.concatenate(idxs, axis=0)
    w_ref[...] = jnp.concatenate([w / wsum * ROUTED_SCALE for w in ws], axis=0)
    r_i = lax.broadcasted_iota(I32, (tt, tt), 0)
    c_i = lax.broadcasted_iota(I32, (tt, tt), 1)
    before = jnp.where(r_i < c_i, 1.0, 0.0).astype(BF16)
    tot = _dot(onehot.astype(BF16), before) + carry[...]
    rank_ref[...] = jnp.concatenate(
        [jnp.sum(jnp.where(rows == ei, tot, 0.0), axis=0, keepdims=True) for ei in idxs], axis=0).astype(I32)
    carry[...] = carry[...] + jnp.sum(onehot, axis=1, keepdims=True)
    cnt_ref[...] = jnp.broadcast_to(carry[...], cnt_ref.shape)


def _dest_body(idx_ref, rank_ref, off_ref, dest_ref):
    tt = idx_ref.shape[1]
    rows = lax.broadcasted_iota(I32, (N_EXPERTS, tt), 0)
    off = off_ref[...]
    outs = []
    for k in range(TOP_K):
        sel = rows == idx_ref[k:k + 1, :]
        outs.append(jnp.sum(jnp.where(sel, off, 0), axis=0, keepdims=True) + rank_ref[k:k + 1, :])
    dest_ref[...] = jnp.concatenate(outs, axis=0)


def _dest(idx_t, rank_t, off_col):
    T = idx_t.shape[1]
    tt = min(4 * SEQ_TILE, T)
    blk = pl.BlockSpec((TOP_K, tt), lambda i: (0, i))
    return pl.pallas_call(
        _dest_body, out_shape=jax.ShapeDtypeStruct((TOP_K, T), I32), grid=(T // tt,),
        in_specs=[blk, blk, pl.BlockSpec((N_EXPERTS, 1), lambda i: (0, 0))], out_specs=blk,
        compiler_params=_params(1), name="dest",
    )(idx_t, rank_t, off_col)


SC_WINDOW = 128
HALF = PACKED // 2


def _sc_mesh():
    return plsc.VectorSubcoreMesh(core_axis_name="c", subcore_axis_name="s")


def _scatter_rows(dest2_t, rows, n_out):
    n = rows.shape[0]

    @pl.kernel(out_type=jax.ShapeDtypeStruct((n_out, HALF), U32), mesh=_sc_mesh(),
               scratch_types=[pltpu.SemaphoreType.DMA((TOP_K,))])
    def scatter(d_hbm, x_hbm, o_hbm, sems):
        def body(d_vmem, x_vmem):
            copies = [pltpu.async_copy(x_vmem, o_hbm.at[d_vmem.at[k]], sems.at[k]) for k in range(TOP_K)]
            for cp in copies:
                cp.wait()

        pltpu.emit_pipeline(
            body, grid=(n // SC_WINDOW,),
            in_specs=[pl.BlockSpec((TOP_K, SC_WINDOW), lambda i: (0, i)),
                      pl.BlockSpec((SC_WINDOW, HALF), lambda i: (i, 0))],
            out_specs=[], core_axis_name=("c", "s"), dimension_semantics=(pltpu.PARALLEL,),
        )(d_hbm, x_hbm)

    return scatter(dest2_t, rows)


def _gather_rows(index_row, rows):
    n = index_row.shape[1]

    @pl.kernel(out_type=jax.ShapeDtypeStruct((n, HALF), U32), mesh=_sc_mesh(), scratch_types=[])
    def gather(i_hbm, y_hbm, o_hbm):
        def body(i_vmem, o_vmem):
            pltpu.sync_copy(y_hbm.at[i_vmem.at[0]], o_vmem)

        pltpu.emit_pipeline(
            body, grid=(n // SC_WINDOW,),
            in_specs=[pl.BlockSpec((1, SC_WINDOW), lambda i: (0, i))],
            out_specs=[pl.BlockSpec((SC_WINDOW, HALF), lambda i: (i, 0))],
            core_axis_name=("c", "s"), dimension_semantics=(pltpu.PARALLEL,),
        )(i_hbm, o_hbm)

    return gather(index_row, rows)


EXPERT_RING = 16
WEIGHT_RING = 3


def _experts_body(first_ref, nblk_ref, cnt_ref, xs_ref, wg_ref, wu_ref, wd_ref, y_ref,
                  xbuf, ybuf, wgf, wuf, wdf, wgb, wub, wdb, xsem, ysem, wsem):
    e = pl.program_id(0)
    last = pl.num_programs(0) - 1
    wring = wgf.shape[0]

    def w_copies(ex):
        slot = ex % wring
        return [pltpu.make_async_copy(src.at[ex], dst.at[slot], wsem.at[slot])
                for src, dst in ((wg_ref, wgf), (wu_ref, wuf), (wd_ref, wdf))]

    @pl.when(e == 0)
    def _():
        for ex in range(wring - 1):
            for cp in w_copies(ex):
                cp.start()

    b0 = first_ref[e]
    nb = nblk_ref[e]
    total = first_ref[last] + nblk_ref[last]
    ring = xbuf.shape[0]

    def rows(g):
        return pl.ds(pl.multiple_of(g * ROW_BLOCK, ROW_BLOCK), ROW_BLOCK)

    def x_copy(g):
        slot = g % ring
        return pltpu.make_async_copy(xs_ref.at[:, rows(g), :], xbuf.at[slot], xsem.at[slot])

    def y_copy(g):
        slot = g % ring
        return pltpu.make_async_copy(ybuf.at[slot], y_ref.at[:, rows(g), :], ysem.at[slot])

    @pl.when(e == 0)
    def _():
        for g in range(ring - 1):
            @pl.when(g < total)
            def _():
                x_copy(g).start()

    for cp in w_copies(e):
        cp.wait()

    @pl.when(nb > 0)
    def _():
        slot = e % wring
        wgb[...] = wgf[slot].astype(BF16)
        wub[...] = wuf[slot].astype(BF16)
        wdb[...] = wdf[slot].astype(BF16)

    @pl.when(e + wring - 1 <= last)
    def _():
        for cp in w_copies(e + wring - 1):
            cp.start()

    def prefetch(g):
        @pl.when(g + ring - 1 < total)
        def _():
            x_copy(g + ring - 1).start()

    def compute(i):
        slot = (b0 + i) % ring
        words = jnp.concatenate([xbuf[slot, 0], xbuf[slot, 1]], axis=1)
        n_live = cnt_ref[e] - i * ROW_BLOCK
        words = jnp.where(lax.broadcasted_iota(I32, words.shape, 0) < n_live, words, U32(0))
        x = jnp.concatenate(_unpack_rows(words), axis=1).astype(BF16)
        act = (_silu(_dot(x, wgb[...])) * _dot(x, wub[...])).astype(BF16)
        y = _pack_rows(_dot(act, wdb[...]))
        ybuf[slot, 0] = y[:, :HALF]
        ybuf[slot, 1] = y[:, HALF:]

    def blocks(i, n):
        g0 = b0 + i
        prefetch(g0)
        for d in range(n):
            x_copy(g0 + d).wait()

            @pl.when(g0 + d >= ring)
            def _():
                y_copy(g0 + d - ring).wait()

        for d in range(n):
            compute(i + d)
        for d in range(1, n):
            prefetch(g0 + d)
        for d in range(n):
            y_copy(g0 + d).start()

    def pair(ii, carry):
        blocks(2 * ii, 2)
        return carry

    def single(_, carry):
        blocks(nb - 1, 1)
        return carry

    lax.fori_loop(0, nb // 2, pair, 0)
    lax.fori_loop(0, nb % 2, single, 0)

    @pl.when(e == last)
    def _():
        for back in range(ring, 0, -1):
            @pl.when(total >= back)
            def _():
                y_copy(total - back).wait()


def _experts(first_block, n_blocks, counts, xs, wg, wu, wd):
    n_rows = xs.shape[1]
    anyspec = pl.BlockSpec(memory_space=pl.ANY)
    ring = pltpu.VMEM((EXPERT_RING, 2, ROW_BLOCK, HALF), U32)
    grid_spec = pltpu.PrefetchScalarGridSpec(
        num_scalar_prefetch=3, grid=(N_EXPERTS,),
        in_specs=[anyspec] * 4,
        out_specs=anyspec,
        scratch_shapes=[ring, ring,
                        pltpu.VMEM((WEIGHT_RING, D_MODEL, EXPERT_DIM), F32),
                        pltpu.VMEM((WEIGHT_RING, D_MODEL, EXPERT_DIM), F32),
                        pltpu.VMEM((WEIGHT_RING, EXPERT_DIM, D_MODEL), F32),
                        pltpu.VMEM((D_MODEL, EXPERT_DIM), BF16), pltpu.VMEM((D_MODEL, EXPERT_DIM), BF16),
                        pltpu.VMEM((EXPERT_DIM, D_MODEL), BF16),
                        pltpu.SemaphoreType.DMA((EXPERT_RING,)), pltpu.SemaphoreType.DMA((EXPERT_RING,)),
                        pltpu.SemaphoreType.DMA((WEIGHT_RING,))],
    )
    return pl.pallas_call(
        _experts_body, out_shape=jax.ShapeDtypeStruct((2, n_rows, HALF), U32), grid_spec=grid_spec,
        compiler_params=_params(1), name="experts",
    )(first_block, n_blocks, counts, xs, wg, wu, wd)


def _combine_body(*refs):
    ya_refs = refs[:TOP_K]
    yb_refs = refs[TOP_K:2 * TOP_K]
    base_ref, w_ref, gt2_ref, out_ref = refs[2 * TOP_K:]
    w = w_ref[...]
    r_lo = r_hi = None
    for k in range(TOP_K):
        lo, hi = _unpack_rows(jnp.concatenate([ya_refs[k][...], yb_refs[k][...]], axis=1))
        wk = w[:, k:k + 1]
        r_lo = wk * lo if r_lo is None else r_lo + wk * lo
        r_hi = wk * hi if r_hi is None else r_hi + wk * hi
    out_ref[...] = base_ref[...] + gt2_ref[...] * jnp.concatenate([r_lo, r_hi], axis=1)


def _combine(y_slots, base, w_tk, gt2, S):
    T = base.shape[0]
    tt = WIDE_TILE
    nt = T // tt
    per_b = S // tt
    yspec = lambda half, k: pl.BlockSpec((tt, HALF), lambda i: ((half * TOP_K + k) * nt + i, 0))
    return pl.pallas_call(
        _combine_body, out_shape=jax.ShapeDtypeStruct((T, D_MODEL), F32), grid=(nt,),
        in_specs=[yspec(half, k) for half in range(2) for k in range(TOP_K)] + [
            pl.BlockSpec((tt, D_MODEL), lambda i: (i, 0)),
            pl.BlockSpec((tt, TOP_K), lambda i: (i, 0)),
            pl.BlockSpec((None, 1, D_MODEL), lambda i: (i // per_b, 0, 0)),
        ],
        out_specs=pl.BlockSpec((tt, D_MODEL), lambda i: (i, 0)),
        compiler_params=_params(1), name="combine",
    )(*([y_slots] * (2 * TOP_K)), base, w_tk, gt2)


def _pad_heads(w, width):
    kdim = w.shape[0]
    w3 = w.reshape(kdim, MLA_HEADS, width)
    return jnp.pad(w3, ((0, 0), (0, 0), (0, LANES - width))).reshape(kdim, MLA_HEADS * LANES)


def _rot_partner(w_rope):
    half = QK_ROPE // 2
    return jnp.concatenate([-w_rope[..., half:], w_rope[..., :half]], axis=-1)


def _lane_vec(nope, rope):
    return jnp.concatenate([nope, rope, jnp.zeros((LANES - QK_DIM,), F32)]).reshape(1, LANES)


def _layer(x2, ada, positions, B, S, g_norm1, w_in, g_cq, w_uq, g_ckv, w_ukv, g_qn, g_kn, g_attn_out,
           conv_w, conv_b, w_mq, w_mk, b_igate, b_fgate, g_mlstm_out, w_out, g_norm2, w_router, router_bias,
           w_gate_exp, w_up_exp, w_down_exp, w_gate_sh, w_up_sh, w_down_sh):
    T = B * S
    sh1, sc1, gt1, sh2, sc2, gt2 = [a.reshape(B, 1, D_MODEL) for a in jnp.split(ada, 6, axis=-1)]
    row = lambda v: v.reshape(1, -1).astype(F32)

    inv_freq = ROPE_THETA ** (-jnp.arange(0, QK_ROPE, 2, dtype=F32) / QK_ROPE)
    ang = positions.astype(F32).reshape(T, 1) * inv_freq
    half = QK_ROPE // 2
    freq_lane = jnp.arange(half)[:, None]
    lane = jnp.arange(LANES)[None, :]
    place = ((lane == QK_NOPE + freq_lane) | (lane == QK_NOPE + half + freq_lane)).astype(F32)
    spread = lambda tbl: jnp.dot(tbl, place, precision=lax.Precision.HIGHEST)
    cosp = (jnp.arange(LANES) < QK_NOPE).astype(F32)[None, :] + spread(jnp.cos(ang))
    sinp = spread(jnp.sin(ang))

    o0 = 0
    w_cq = w_in[:, o0:o0 + Q_LORA]; o0 += Q_LORA
    w_ckv = w_in[:, o0:o0 + KV_LORA]; o0 += KV_LORA
    w_kr = w_in[:, o0:o0 + QK_ROPE]; o0 += QK_ROPE
    w_z = w_in[:, o0:o0 + MLSTM_INNER]; o0 += MLSTM_INNER
    w_v = w_in[:, o0:o0 + MLSTM_INNER]; o0 += MLSTM_INNER
    w_o = w_in[:, o0:o0 + MLSTM_INNER]; o0 += MLSTM_INNER
    w_i = w_in[:, o0:o0 + MLSTM_HEADS]; o0 += MLSTM_HEADS
    w_f = w_in[:, o0:o0 + MLSTM_HEADS]
    zl = jnp.zeros((D_MODEL, QK_NOPE), F32)
    zr = jnp.zeros((D_MODEL, LANES - QK_DIM), F32)
    zg = jnp.zeros((D_MODEL, LANES - 2 * MLSTM_HEADS), F32)
    wa = jnp.concatenate([w_cq, w_ckv, w_z, w_o, zl, w_kr, zr, zl, _rot_partner(w_kr), zr, w_i, w_f, zg],
                         axis=1).astype(BF16)
    wt = jnp.concatenate([w_v.T, w_i.T, w_f.T, jnp.zeros((16 - 2 * MLSTM_HEADS, D_MODEL), F32)], axis=0).astype(BF16)

    uq3 = w_uq.reshape(Q_LORA, MLA_HEADS, QK_DIM)
    uq_rot = jnp.concatenate([jnp.zeros((Q_LORA, MLA_HEADS, QK_NOPE), F32), _rot_partner(uq3[..., QK_NOPE:])], axis=-1)
    wuq = _pad_heads(w_uq, QK_DIM).astype(BF16)
    wuqs = _pad_heads(uq_rot.reshape(Q_LORA, MLA_HEADS * QK_DIM), QK_DIM).astype(BF16)
    ukv3 = w_ukv.reshape(KV_LORA, MLA_HEADS, QK_NOPE + V_DIM)
    wuk = _pad_heads(ukv3[..., :QK_NOPE].reshape(KV_LORA, MLA_HEADS * QK_NOPE), QK_NOPE).astype(BF16)
    wuv = _pad_heads(ukv3[..., QK_NOPE:].reshape(KV_LORA, MLA_HEADS * V_DIM), V_DIM).astype(BF16)
    vone = jnp.tile((jnp.arange(LANES) == V_DIM).astype(F32), MLA_HEADS).reshape(1, MLA_HEADS * LANES)
    swap = lambda g_rope: jnp.concatenate([g_rope[half:], g_rope[:half]])
    no_nope = jnp.zeros((QK_NOPE,), F32)
    gq = _lane_vec(g_qn[:QK_NOPE], g_qn[QK_NOPE:])
    gqs = _lane_vec(no_nope, swap(g_qn[QK_NOPE:]))
    gk = _lane_vec(g_kn[:QK_NOPE], g_kn[QK_NOPE:])
    gks = _lane_vec(no_nope, swap(g_kn[QK_NOPE:]))

    z, og, vt3, g3, gn, q, k, mv = _inproj(
        x2, 1.0 + sc1, sh1, row(g_norm1), wa, wt, row(g_cq), wuq, wuqs, row(g_ckv), wuk, wuv, vone,
        gq, gqs, gk, gks, cosp, sinp, S)
    attn_o = _attn(q, k, mv, B, S)
    mlstm_o = _mlstm(z, vt3, g3, gn, og, conv_w.reshape(CONV_WIDTH, MLSTM_INNER), row(conv_b),
                     w_mq.astype(BF16), w_mk.astype(BF16), b_igate, b_fgate, row(g_mlstm_out), B, S)

    wgu = jnp.concatenate([w_gate_sh, w_up_sh], axis=1).astype(BF16)
    base, h2, idx_t, w_t, rank_t, cnt = _post(
        attn_o, mlstm_o, x2, gt1, row(g_attn_out), w_out.astype(BF16), row(g_norm2), 1.0 + sc2, sh2, gt2,
        w_router.T.astype(BF16), wgu, w_down_sh.astype(BF16), router_bias.reshape(N_EXPERTS, 1), S)
    counts = cnt[:, 0].astype(I32)
    padded = (counts + ROW_BLOCK - 1) // ROW_BLOCK * ROW_BLOCK
    padded_end = jnp.cumsum(padded)
    off = padded_end - padded
    n_rows = T * TOP_K + N_EXPERTS * ROW_BLOCK
    dest_t = _dest(idx_t, rank_t, off.reshape(N_EXPERTS, 1))

    dest2_t = jnp.concatenate([dest_t, dest_t + n_rows], axis=1)
    xs = _scatter_rows(dest2_t, h2.reshape(2 * T, HALF), 2 * n_rows).reshape(2, n_rows, HALF)
    y = _experts(off // ROW_BLOCK, padded // ROW_BLOCK, counts, xs, w_gate_exp, w_up_exp, w_down_exp)
    dest_row = dest_t.reshape(1, TOP_K * T)
    y_slots = _gather_rows(jnp.concatenate([dest_row, dest_row + n_rows], axis=1), y.reshape(2 * n_rows, HALF))
    return _combine(y_slots, base, w_t.T, gt2, S)


def kernel(x, c, positions, w_ada, b_ada, g_norm1, w_in, g_cq, w_uq, g_ckv, w_ukv, g_qn, g_kn, g_attn_out, conv_w, conv_b, w_mq, w_mk, b_igate, b_fgate, g_mlstm_out, w_out, g_norm2, w_router, router_bias, w_gate_exp, w_up_exp, w_down_exp, w_gate_sh, w_up_sh, w_down_sh):
    B, S, d_model = x.shape
    assert d_model == D_MODEL and S % WIDE_TILE == 0 and S % SEQ_TILE == 0, (B, S, d_model)
    assert w_gate_exp.shape[1:] == (N_EXPERTS, D_MODEL, EXPERT_DIM), w_gate_exp.shape
    depth = w_ada.shape[0]
    x2 = x.reshape(B * S, D_MODEL)
    for l in range(depth):
        ada = _ada(c, w_ada[l], b_ada[l])
        x2 = _layer(x2, ada, positions, B, S, g_norm1[l], w_in[l], g_cq[l], w_uq[l], g_ckv[l], w_ukv[l], g_qn[l],
                    g_kn[l], g_attn_out[l], conv_w[l], conv_b[l], w_mq[l], w_mk[l], b_igate[l], b_fgate[l],
                    g_mlstm_out[l], w_out[l], g_norm2[l], w_router[l], router_bias[l], w_gate_exp[l], w_up_exp[l],
                    w_down_exp[l], w_gate_sh[l], w_up_sh[l], w_down_sh[l])
    return x2.reshape(B, S, D_MODEL)
```
